```python
import math
import jax, jax.numpy as jnp
from jax import lax
import numpy as np

D_MODEL = 1024
BATCH = 16
SEQ = 4096
DEPTH = 1

EXPAND = 2
D_INNER = EXPAND * D_MODEL
D_SB = D_INNER // 2
D_SSD = D_INNER - D_SB
SB_HEAD_DIM = 64
SB_HEADS = D_SB // SB_HEAD_DIM
SB_BLOCK = 128
SSD_HEAD_DIM = 64
SSD_HEADS = D_SSD // SSD_HEAD_DIM
SSD_GROUPS = 2
SSD_STATE = 128
SSD_CHUNK = 128
CONV_W = 4
CONV_DIM = D_SSD + 2 * SSD_GROUPS * SSD_STATE
D_IN_PROJ = 4 * D_SB + D_SSD + CONV_DIM + SSD_HEADS
EPS = 1e-6

kernel_name = "hymba_stickbreaking_ssd_layer"


def rmsnorm(x, w):
    xf = x.astype(jnp.float32)
    y = xf * lax.rsqrt(jnp.mean(xf * xf, axis=-1, keepdims=True) + EPS)
    return (y * w.astype(jnp.float32)).astype(x.dtype)


def causal_depthwise_conv(u, w, b):
    l = u.shape[1]
    up = jnp.pad(u, ((0, 0), (CONV_W - 1, 0), (0, 0)))
    out = b
    for i in range(CONV_W):
        out = out + up[:, i:i + l] * w[i]
    return out


def stick_breaking_attention(q, k, v):
    b, l, h, d = q.shape
    nb = l // SB_BLOCK
    scale = 1.0 / math.sqrt(d)
    qb = q.astype(jnp.float32).reshape(b, nb, SB_BLOCK, h, d).transpose(1, 0, 2, 3, 4)
    kf = k.astype(jnp.float32)
    vf = v.astype(jnp.float32)
    key_pos = jnp.arange(l)

    def block(args):
        qi, i = args
        z = jnp.einsum('bqhd,bkhd->bhqk', qi, kf) * scale
        q_pos = i * SB_BLOCK + jnp.arange(SB_BLOCK)
        mask = key_pos[None, :] < q_pos[:, None]
        log_keep = jnp.where(mask, jax.nn.log_sigmoid(-z), 0.0)
        rest = lax.cumsum(log_keep, axis=3, reverse=True) - log_keep
        w = jnp.where(mask, jnp.exp(jax.nn.log_sigmoid(z) + rest), 0.0)
        return jnp.einsum('bhqk,bkhd->bqhd', w, vf)

    o = lax.map(block, (qb, jnp.arange(nb)))
    return o.transpose(1, 0, 2, 3, 4).reshape(b, l, h, d)


def ssd_chunked(xh, dt, a, Bm, Cm):
    b, l, h, p = xh.shape
    g, n = Bm.shape[2], Bm.shape[3]
    hpg = h // g
    c = l // SSD_CHUNK
    Q = SSD_CHUNK
    x = (xh.astype(jnp.float32) * dt[..., None]).reshape(b, c, Q, g, hpg, p)
    dA = (dt * a).reshape(b, c, Q, g, hpg).transpose(0, 1, 3, 4, 2)
    Bc = Bm.astype(jnp.float32).reshape(b, c, Q, g, n)
    Cc = Cm.astype(jnp.float32).reshape(b, c, Q, g, n)
    a_cs = jnp.cumsum(dA, axis=-1)

    seg = a_cs[..., :, None] - a_cs[..., None, :]
    tri = jnp.tril(jnp.ones((Q, Q), dtype=bool))
    Lmat = jnp.exp(jnp.where(tri, seg, -jnp.inf))
    cb = jnp.einsum('bctgn,bcsgn->bcgts', Cc, Bc)
    wts = cb[:, :, :, None] * Lmat
    y_diag = jnp.einsum('bcgkts,bcsgkp->bctgkp', wts, x)

    decay_to_end = jnp.exp(a_cs[..., -1:] - a_cs)
    states = jnp.einsum('bcsgn,bcgks,bcsgkp->bcgkpn', Bc, decay_to_end, x)
    chunk_decay = jnp.exp(a_cs[..., -1])

    def step(carry, inp):
        st, dec = inp
        return carry * dec[..., None, None] + st, carry

    init = jnp.zeros((b, g, hpg, p, n), jnp.float32)
    _, states_in = lax.scan(step, init, (jnp.moveaxis(states, 1, 0), jnp.moveaxis(chunk_decay, 1, 0)))
    states_in = jnp.moveaxis(states_in, 0, 1)

    y_off = jnp.einsum('bctgn,bcgkpn,bcgkt->bctgkp', Cc, states_in, jnp.exp(a_cs))
    return (y_diag + y_off).reshape(b, l, h, p)


def hybrid_layer(x, norm_w, w_in, q_norm_w, k_norm_w, conv_w, conv_b, dt_bias,
                 A_log, D_skip, sb_norm_w, ssd_norm_w, w_out):
    b, l, _ = x.shape
    hn = rmsnorm(x, norm_w)
    proj = hn @ w_in
    i1 = D_SB; i2 = 2 * D_SB; i3 = 3 * D_SB; i4 = 4 * D_SB
    i5 = i4 + D_SSD; i6 = i5 + CONV_DIM
    q, k, v, z_sb, z_ssd, xBC, dt_raw = jnp.split(proj, [i1, i2, i3, i4, i5, i6], axis=-1)

    q = rmsnorm(q.reshape(b, l, SB_HEADS, SB_HEAD_DIM), q_norm_w)
    k = rmsnorm(k.reshape(b, l, SB_HEADS, SB_HEAD_DIM), k_norm_w)
    v = v.reshape(b, l, SB_HEADS, SB_HEAD_DIM)
    o_sb = stick_breaking_attention(q, k, v).reshape(b, l, D_SB).astype(x.dtype)
    y_sb = rmsnorm(o_sb * jax.nn.silu(z_sb), sb_norm_w)

    xBC = jax.nn.silu(causal_depthwise_conv(xBC, conv_w, conv_b))
    xs, Bm, Cm = jnp.split(xBC, [D_SSD, D_SSD + SSD_GROUPS * SSD_STATE], axis=-1)
    xs = xs.reshape(b, l, SSD_HEADS, SSD_HEAD_DIM)
    dt = jax.nn.softplus(dt_raw.astype(jnp.float32) + dt_bias.astype(jnp.float32))
    a = -jnp.exp(A_log.astype(jnp.float32))
    y = ssd_chunked(xs, dt, a,
                    Bm.reshape(b, l, SSD_GROUPS, SSD_STATE),
                    Cm.reshape(b, l, SSD_GROUPS, SSD_STATE))
    y = y + D_skip.astype(jnp.float32)[:, None] * xs.astype(jnp.float32)
    y = y.reshape(b, l, D_SSD).astype(x.dtype)
    y_ssd = rmsnorm(y * jax.nn.silu(z_ssd), ssd_norm_w)

    mixed = jnp.concatenate([y_sb, y_ssd], axis=-1)
    return x + mixed @ w_out


def _fwd_setup_inputs(seed: int = 0) -> dict:
    key = jax.random.key(seed)
    ks = jax.random.split(key, 14)
    f32 = jnp.float32
    x = jax.random.normal(ks[0], (BATCH, SEQ, D_MODEL), f32)
    norm_w = 1.0 + 0.02 * jax.random.normal(ks[1], (DEPTH, D_MODEL), f32)
    w_in = jax.random.normal(ks[2], (DEPTH, D_MODEL, D_IN_PROJ), f32) * D_MODEL ** -0.5
    q_norm_w = 1.0 + 0.02 * jax.random.normal(ks[3], (DEPTH, SB_HEAD_DIM), f32)
    k_norm_w = 1.0 + 0.02 * jax.random.normal(ks[4], (DEPTH, SB_HEAD_DIM), f32)
    conv_w = jax.random.normal(ks[5], (DEPTH, CONV_W, CONV_DIM), f32) * CONV_W ** -0.5
    conv_b = 0.01 * jax.random.normal(ks[6], (DEPTH, CONV_DIM), f32)
    u = jax.random.uniform(ks[7], (DEPTH, SSD_HEADS), f32)
    dt0 = jnp.exp(u * (math.log(0.1) - math.log(0.001)) + math.log(0.001))
    dt_bias = dt0 + jnp.log(-jnp.expm1(-dt0))
    A_log = jnp.log(jax.random.uniform(ks[8], (DEPTH, SSD_HEADS), f32, minval=1.0, maxval=16.0))
    D_skip = 1.0 + 0.02 * jax.random.normal(ks[9], (DEPTH, SSD_HEADS), f32)
    sb_norm_w = 1.0 + 0.02 * jax.random.normal(ks[10], (DEPTH, D_SB), f32)
    ssd_norm_w = 1.0 + 0.02 * jax.random.normal(ks[11], (DEPTH, D_SSD), f32)
    w_out = jax.random.normal(ks[12], (DEPTH, D_INNER, D_MODEL), f32) * D_INNER ** -0.5
    return {"x": x, "norm_w": norm_w, "w_in": w_in, "q_norm_w": q_norm_w,
            "k_norm_w": k_norm_w, "conv_w": conv_w, "conv_b": conv_b,
            "dt_bias": dt_bias, "A_log": A_log, "D_skip": D_skip,
            "sb_norm_w": sb_norm_w, "ssd_norm_w": ssd_norm_w, "w_out": w_out}


def _fwd_reference(x, norm_w, w_in, q_norm_w, k_norm_w, conv_w, conv_b, dt_bias,
              A_log, D_skip, sb_norm_w, ssd_norm_w, w_out):
    for layer in range(DEPTH):
        x = hybrid_layer(x, norm_w[layer], w_in[layer], q_norm_w[layer], k_norm_w[layer],
                         conv_w[layer], conv_b[layer], dt_bias[layer], A_log[layer],
                         D_skip[layer], sb_norm_w[layer], ssd_norm_w[layer], w_out[layer])
    return x


import jax as _jax
import jax.numpy as _jnp

TWIN_FORMAT = 'train_step'
FWD_PARAMS = ['x', 'norm_w', 'w_in', 'q_norm_w', 'k_norm_w', 'conv_w', 'conv_b', 'dt_bias', 'A_log', 'D_skip', 'sb_norm_w', 'ssd_norm_w', 'w_out']
TWIN_WEIGHTS = ['norm_w', 'w_in', 'q_norm_w', 'k_norm_w', 'conv_w', 'conv_b', 'dt_bias', 'A_log', 'D_skip', 'sb_norm_w', 'ssd_norm_w', 'w_out']
TWIN_DIFF_INPUT = 'x'
TWIN_INPUTS = ['x', 'norm_w', 'w_in', 'q_norm_w', 'k_norm_w', 'conv_w', 'conv_b', 'dt_bias', 'A_log', 'D_skip', 'sb_norm_w', 'ssd_norm_w', 'w_out', 'loss_target', 'm_norm_w', 'm_w_in', 'm_q_norm_w', 'm_k_norm_w', 'm_conv_w', 'm_conv_b', 'm_dt_bias', 'm_A_log', 'm_D_skip', 'm_sb_norm_w', 'm_ssd_norm_w', 'm_w_out', 'v_norm_w', 'v_w_in', 'v_q_norm_w', 'v_k_norm_w', 'v_conv_w', 'v_conv_b', 'v_dt_bias', 'v_A_log', 'v_D_skip', 'v_sb_norm_w', 'v_ssd_norm_w', 'v_w_out']
TWIN_OUTPUTS = ['loss', 'grad_x', 'grad_norm_w', 'grad_w_in', 'grad_q_norm_w', 'grad_k_norm_w', 'grad_conv_w', 'grad_conv_b', 'grad_dt_bias', 'grad_A_log', 'grad_D_skip', 'grad_sb_norm_w', 'grad_ssd_norm_w', 'grad_w_out', 'delta_norm_w', 'delta_w_in', 'delta_q_norm_w', 'delta_k_norm_w', 'delta_conv_w', 'delta_conv_b', 'delta_dt_bias', 'delta_A_log', 'delta_D_skip', 'delta_sb_norm_w', 'delta_ssd_norm_w', 'delta_w_out', 'new_m_norm_w', 'new_m_w_in', 'new_m_q_norm_w', 'new_m_k_norm_w', 'new_m_conv_w', 'new_m_conv_b', 'new_m_dt_bias', 'new_m_A_log', 'new_m_D_skip', 'new_m_sb_norm_w', 'new_m_ssd_norm_w', 'new_m_w_out', 'new_v_norm_w', 'new_v_w_in', 'new_v_q_norm_w', 'new_v_k_norm_w', 'new_v_conv_w', 'new_v_conv_b', 'new_v_dt_bias', 'new_v_A_log', 'new_v_D_skip', 'new_v_sb_norm_w', 'new_v_ssd_norm_w', 'new_v_w_out']
TWIN_LEAF_KINDS = {'loss': 'loss', 'grad_x': 'grad_x', 'grad_norm_w': 'grad_w', 'grad_w_in': 'grad_w', 'grad_q_norm_w': 'grad_w', 'grad_k_norm_w': 'grad_w', 'grad_conv_w': 'grad_w', 'grad_conv_b': 'grad_w', 'grad_dt_bias': 'grad_w', 'grad_A_log': 'grad_w', 'grad_D_skip': 'grad_w', 'grad_sb_norm_w': 'grad_w', 'grad_ssd_norm_w': 'grad_w', 'grad_w_out': 'grad_w', 'delta_norm_w': 'delta_w', 'delta_w_in': 'delta_w', 'delta_q_norm_w': 'delta_w', 'delta_k_norm_w': 'delta_w', 'delta_conv_w': 'delta_w', 'delta_conv_b': 'delta_w', 'delta_dt_bias': 'delta_w', 'delta_A_log': 'delta_w', 'delta_D_skip': 'delta_w', 'delta_sb_norm_w': 'delta_w', 'delta_ssd_norm_w': 'delta_w', 'delta_w_out': 'delta_w', 'new_m_norm_w': 'new_m', 'new_m_w_in': 'new_m', 'new_m_q_norm_w': 'new_m', 'new_m_k_norm_w': 'new_m', 'new_m_conv_w': 'new_m', 'new_m_conv_b': 'new_m', 'new_m_dt_bias': 'new_m', 'new_m_A_log': 'new_m', 'new_m_D_skip': 'new_m', 'new_m_sb_norm_w': 'new_m', 'new_m_ssd_norm_w': 'new_m', 'new_m_w_out': 'new_m', 'new_v_norm_w': 'new_v', 'new_v_w_in': 'new_v', 'new_v_q_norm_w': 'new_v', 'new_v_k_norm_w': 'new_v', 'new_v_conv_w': 'new_v', 'new_v_conv_b': 'new_v', 'new_v_dt_bias': 'new_v', 'new_v_A_log': 'new_v', 'new_v_D_skip': 'new_v', 'new_v_sb_norm_w': 'new_v', 'new_v_ssd_norm_w': 'new_v', 'new_v_w_out': 'new_v'}


def _forward(args):
    return _fwd_reference(*[args[k] for k in FWD_PARAMS])


def _output_shape():
    out = _jax.eval_shape(lambda: _forward(_fwd_setup_inputs(0)))
    return out.shape, out.dtype

N_MICROBATCH = 1
ADAM_LR = 0.001
ADAM_B1 = 0.9
ADAM_B2 = 0.999
ADAM_EPS = 1e-08
ADAM_WD = 0.01
ADAM_STEP = 10
PER_EXAMPLE_BATCH_AXIS = {'x': 0, 'loss_target': 0}
SHARED_INPUTS = []
_WEIGHT_DTYPES = {'norm_w': _jnp.float32, 'w_in': _jnp.float32, 'q_norm_w': _jnp.float32, 'k_norm_w': _jnp.float32, 'conv_w': _jnp.float32, 'conv_b': _jnp.float32, 'dt_bias': _jnp.float32, 'A_log': _jnp.float32, 'D_skip': _jnp.float32, 'sb_norm_w': _jnp.float32, 'ssd_norm_w': _jnp.float32, 'w_out': _jnp.float32}
MOMENT_SCALE = {'norm_w': 7.902713e-01, 'w_in': 2.887944e-01, 'q_norm_w': 6.485804e-01, 'k_norm_w': 6.505240e-01, 'conv_w': 9.495899e-01, 'conv_b': 3.313868e+00, 'dt_bias': 1.256315e+00, 'A_log': 1.246809e+01, 'D_skip': 7.696273e+00, 'sb_norm_w': 3.134747e+01, 'ssd_norm_w': 4.907728e+01, 'w_out': 1.574025e+00}


def _to_microbatches(a, axis):
    t = _jnp.moveaxis(a, axis, 0)
    t = t.reshape((N_MICROBATCH, t.shape[0] // N_MICROBATCH) + t.shape[1:])
    return _jnp.moveaxis(t, 1, axis + 1)


def setup_inputs(seed: int = 0) -> dict:
    inp = _fwd_setup_inputs(seed)
    key = _jax.random.fold_in(_jax.random.key(seed), 7919)
    shape, _ = _output_shape()
    out = dict(inp)
    out["loss_target"] = _jax.random.normal(_jax.random.fold_in(key, 0), shape, _jnp.float32)
    for i, name in enumerate(TWIN_WEIGHTS):
        w = inp[name].astype(_jnp.float32)
        if MOMENT_SCALE is None:
            s = _jnp.sqrt(_jnp.mean(_jnp.square(w)) + 1e-30)
        else:
            s = MOMENT_SCALE[name]
        km, kv = _jax.random.split(_jax.random.fold_in(key, i + 1))
        out[name] = w
        out["m_" + name] = s * _jax.random.normal(km, w.shape, _jnp.float32)
        out["v_" + name] = (s * s) * _jax.random.uniform(kv, w.shape, _jnp.float32, 0.5, 1.5)
    if N_MICROBATCH > 1:
        for name, axis in PER_EXAMPLE_BATCH_AXIS.items():
            out[name] = _to_microbatches(out[name], axis)
    return {'x': out['x'], 'norm_w': out['norm_w'], 'w_in': out['w_in'], 'q_norm_w': out['q_norm_w'], 'k_norm_w': out['k_norm_w'], 'conv_w': out['conv_w'], 'conv_b': out['conv_b'], 'dt_bias': out['dt_bias'], 'A_log': out['A_log'], 'D_skip': out['D_skip'], 'sb_norm_w': out['sb_norm_w'], 'ssd_norm_w': out['ssd_norm_w'], 'w_out': out['w_out'], 'loss_target': out['loss_target'], 'm_norm_w': out['m_norm_w'], 'm_w_in': out['m_w_in'], 'm_q_norm_w': out['m_q_norm_w'], 'm_k_norm_w': out['m_k_norm_w'], 'm_conv_w': out['m_conv_w'], 'm_conv_b': out['m_conv_b'], 'm_dt_bias': out['m_dt_bias'], 'm_A_log': out['m_A_log'], 'm_D_skip': out['m_D_skip'], 'm_sb_norm_w': out['m_sb_norm_w'], 'm_ssd_norm_w': out['m_ssd_norm_w'], 'm_w_out': out['m_w_out'], 'v_norm_w': out['v_norm_w'], 'v_w_in': out['v_w_in'], 'v_q_norm_w': out['v_q_norm_w'], 'v_k_norm_w': out['v_k_norm_w'], 'v_conv_w': out['v_conv_w'], 'v_conv_b': out['v_conv_b'], 'v_dt_bias': out['v_dt_bias'], 'v_A_log': out['v_A_log'], 'v_D_skip': out['v_D_skip'], 'v_sb_norm_w': out['v_sb_norm_w'], 'v_ssd_norm_w': out['v_ssd_norm_w'], 'v_w_out': out['v_w_out']}


def _loss(weights, diff, rest, loss_target):
    with _jax.named_scope("forward"):
        args = {**rest, TWIN_DIFF_INPUT: diff, **{k: w.astype(_WEIGHT_DTYPES[k]) for k, w in weights.items()}}
        y = _forward(args)
    with _jax.named_scope("loss_head"):
        err = _jnp.square(y.astype(_jnp.float32) - loss_target)
        return 0.5 * _jnp.sum(_jnp.mean(err, axis=-1)) if err.ndim else 0.5 * err


def _adamw(w, g, m, v):
    m = ADAM_B1 * m + (1.0 - ADAM_B1) * g
    v = ADAM_B2 * v + (1.0 - ADAM_B2) * _jnp.square(g)
    m_hat = m / (1.0 - ADAM_B1 ** ADAM_STEP)
    v_hat = v / (1.0 - ADAM_B2 ** ADAM_STEP)
    delta = -ADAM_LR * (m_hat / (_jnp.sqrt(v_hat) + ADAM_EPS) + ADAM_WD * w)
    return delta, m, v


def reference(x, norm_w, w_in, q_norm_w, k_norm_w, conv_w, conv_b, dt_bias, A_log, D_skip, sb_norm_w, ssd_norm_w, w_out, loss_target, m_norm_w, m_w_in, m_q_norm_w, m_k_norm_w, m_conv_w, m_conv_b, m_dt_bias, m_A_log, m_D_skip, m_sb_norm_w, m_ssd_norm_w, m_w_out, v_norm_w, v_w_in, v_q_norm_w, v_k_norm_w, v_conv_w, v_conv_b, v_dt_bias, v_A_log, v_D_skip, v_sb_norm_w, v_ssd_norm_w, v_w_out):
    given = dict(x=x, norm_w=norm_w, w_in=w_in, q_norm_w=q_norm_w, k_norm_w=k_norm_w, conv_w=conv_w, conv_b=conv_b, dt_bias=dt_bias, A_log=A_log, D_skip=D_skip, sb_norm_w=sb_norm_w, ssd_norm_w=ssd_norm_w, w_out=w_out, loss_target=loss_target, m_norm_w=m_norm_w, m_w_in=m_w_in, m_q_norm_w=m_q_norm_w, m_k_norm_w=m_k_norm_w, m_conv_w=m_conv_w, m_conv_b=m_conv_b, m_dt_bias=m_dt_bias, m_A_log=m_A_log, m_D_skip=m_D_skip, m_sb_norm_w=m_sb_norm_w, m_ssd_norm_w=m_ssd_norm_w, m_w_out=m_w_out, v_norm_w=v_norm_w, v_w_in=v_w_in, v_q_norm_w=v_q_norm_w, v_k_norm_w=v_k_norm_w, v_conv_w=v_conv_w, v_conv_b=v_conv_b, v_dt_bias=v_dt_bias, v_A_log=v_A_log, v_D_skip=v_D_skip, v_sb_norm_w=v_sb_norm_w, v_ssd_norm_w=v_ssd_norm_w, v_w_out=v_w_out)
    weights = {n: given[n] for n in TWIN_WEIGHTS}
    shared = {n: given[n] for n in SHARED_INPUTS}
    per_example = {n: given[n] for n in ['x']}
    grad_fn = _jax.value_and_grad(_loss, argnums=(0, 1))

    def one_microbatch(ex, loss_target):
        ex = dict(ex)
        diff = ex.pop(TWIN_DIFF_INPUT)
        return grad_fn(weights, diff, {**shared, **ex}, loss_target)

    if N_MICROBATCH == 1:
        loss, (grad_w, grad_x) = one_microbatch(per_example, given["loss_target"])
    else:
        def body(carry, xs):
            loss_sum, grad_sum = carry
            l_k, (gw_k, gx_k) = one_microbatch(xs[0], xs[1])
            with _jax.named_scope("update"):
                return (loss_sum + l_k, _jax.tree.map(_jnp.add, grad_sum, gw_k)), gx_k

        init = (_jnp.zeros((), _jnp.float32), _jax.tree.map(_jnp.zeros_like, weights))
        (loss, grad_w), grad_x = _jax.lax.scan(body, init, (per_example, given["loss_target"]))
    with _jax.named_scope("update"):
        delta_w, new_m, new_v = {}, {}, {}
        for n in TWIN_WEIGHTS:
            delta_w[n], new_m[n], new_v[n] = _adamw(weights[n], grad_w[n], given["m_" + n], given["v_" + n])
    return (loss, grad_x, *[grad_w[n] for n in TWIN_WEIGHTS], *[delta_w[n] for n in TWIN_WEIGHTS],
            *[new_m[n] for n in TWIN_WEIGHTS], *[new_v[n] for n in TWIN_WEIGHTS])
```

```python
import functools
import math

import jax
import jax.numpy as jnp
from jax import lax
from jax.experimental import pallas as pl
from jax.experimental.pallas import tpu as pltpu

F32 = jnp.float32
BF16 = jnp.bfloat16
EPS = 1e-6
HEAD_DIM = 64
PAIR = 2 * HEAD_DIM
LANES = 128
SSD_STATE = 128
SSD_GROUPS = 2
BLK = 128
CONV_K = 4
HALO = 8
N_CHIPS = 4
ADAM_LR, ADAM_B1, ADAM_B2, ADAM_EPS, ADAM_WD, ADAM_STEP = 0.001, 0.9, 0.999, 1e-08, 0.01, 10
VMEM_LIMIT_V7X = 56 * 1024 * 1024
MESH = pl.DeviceIdType.MESH
ANY = pl.BlockSpec(memory_space=pl.ANY)
NT = (((1,), (1,)), ((), ()))


def _params(sem=None):
    kw = dict(vmem_limit_bytes=VMEM_LIMIT_V7X)
    if sem is not None:
        kw["dimension_semantics"] = sem
    return pltpu.CompilerParams(**kw)


def _dot(a, b):
    return jnp.dot(a, b, preferred_element_type=F32)


def _dot_nt(a, b):
    return lax.dot_general(a, b, NT, preferred_element_type=F32)


def _split_dot(x, m):
    hi = x.astype(BF16)
    lo = (x - hi.astype(F32)).astype(BF16)
    return _dot(hi, m) + _dot(lo, m)


def _dot_split(m, x):
    hi = x.astype(BF16)
    lo = (x - hi.astype(F32)).astype(BF16)
    return _dot(m, hi) + _dot(m, lo)


def _iota(shape, dim):
    return lax.broadcasted_iota(jnp.int32, shape, dim)


def _rowsum(x):
    return jnp.sum(x, axis=1, keepdims=True)


def _colsum(x):
    return jnp.sum(x, axis=0, keepdims=True)


def _sigmoid(x):
    return 1.0 / (1.0 + jnp.exp(-x))


def _softplus(x):
    return jnp.maximum(x, 0.0) + jnp.log(1.0 + jnp.exp(-jnp.abs(x)))


def _inproj(x2, norm_w, w_pad):
    T, D = x2.shape
    P = w_pad.shape[1]
    tm = min(512, T)
    tn = 1024 if P % 1024 == 0 else 512

    def body(x_ref, nw_ref, w_ref, proj_ref, hn_ref):
        @pl.when(pl.program_id(1) == 0)
        def _():
            xv = x_ref[...]
            r = lax.rsqrt(jnp.mean(xv * xv, axis=-1, keepdims=True) + EPS)
            hn_ref[...] = (xv * r * nw_ref[...]).astype(BF16)

        proj_ref[...] = _dot(hn_ref[...], w_ref[...])

    return pl.pallas_call(
        body,
        name="inproj",
        grid=(T // tm, P // tn),
        in_specs=[
            pl.BlockSpec((tm, D), lambda i, j: (i, 0)),
            pl.BlockSpec((1, D), lambda i, j: (0, 0)),
            pl.BlockSpec((D, tn), lambda i, j: (0, j)),
        ],
        out_specs=[
            pl.BlockSpec((tm, tn), lambda i, j: (i, j)),
            pl.BlockSpec((tm, D), lambda i, j: (i, 0)),
        ],
        out_shape=[jax.ShapeDtypeStruct((T, P), F32), jax.ShapeDtypeStruct((T, D), BF16)],
        compiler_params=_params(("parallel", "arbitrary")),
    )(x2, norm_w, w_pad)


def _pair_rms(v, lo):
    sq = v * v
    s0 = _rowsum(jnp.where(lo, sq, 0.0))
    s1 = _rowsum(jnp.where(lo, 0.0, sq))
    return lax.rsqrt(jnp.where(lo, s0, s1) * (1.0 / HEAD_DIM) + EPS)


def _pair_mean(v, lo):
    s0 = _rowsum(jnp.where(lo, v, 0.0))
    s1 = _rowsum(jnp.where(lo, 0.0, v))
    return jnp.where(lo, s0, s1) * (1.0 / HEAD_DIM)


def _suffix_ones():
    ri = _iota((BLK, 2 * BLK), 0)
    ci = _iota((BLK, 2 * BLK), 1)
    return jnp.where((ci >= BLK) | (ri > ci), 1.0, 0.0).astype(BF16)


def _sb_tile(qm, km, rest_carry, uo, diag):
    z = _dot_nt(qm, km)
    e = jnp.exp(-jnp.abs(z))
    a = jnp.minimum(z, 0.0) - jnp.log(1.0 + e)
    lk = a - z
    if diag is not None:
        lk = jnp.where(diag, lk, 0.0)
    cs = _split_dot(lk, uo)
    w = jnp.exp(a + rest_carry + cs[:, :BLK])
    if diag is not None:
        w = jnp.where(diag, w, 0.0)
    return z, e, w, rest_carry + cs[:, BLK:]


def _attn_prep(src_ref, w_ref, dst_s, n_blocks, scale):
    lo = _iota((BLK, PAIR), 1) < HEAD_DIM

    def step(i, carry):
        r0 = pl.multiple_of(i * BLK, BLK)
        v = src_ref[0, pl.ds(r0, BLK), :]
        if w_ref is not None:
            v = v * _pair_rms(v, lo) * w_ref[...]
        if scale != 1.0:
            v = v * scale
        dst_s[0, pl.ds(r0, BLK), :] = jnp.where(lo, v, 0.0).astype(BF16)
        dst_s[1, pl.ds(r0, BLK), :] = jnp.where(lo, 0.0, v).astype(BF16)
        return carry

    lax.fori_loop(0, n_blocks, step, 0)


def _attn_fwd(proj3, qw2, kw2, D):
    Bl, L, _ = proj3.shape
    n_pair = D // PAIR
    nq = L // BLK
    scale = 1.0 / math.sqrt(HEAD_DIM)

    def body(q_ref, k_ref, v_ref, qw_ref, kw_ref, o_ref, qm_s, km_s, vm_s):
        uo = _suffix_ones()
        diag = _iota((BLK, BLK), 1) < _iota((BLK, BLK), 0)
        _attn_prep(q_ref, qw_ref, qm_s, nq, scale)
        _attn_prep(k_ref, kw_ref, km_s, nq, 1.0)
        _attn_prep(v_ref, None, vm_s, nq, 1.0)

        def qblock(qi, carry):
            r0 = pl.multiple_of(qi * BLK, BLK)
            acc = jnp.zeros((BLK, PAIR), F32)
            for h in range(2):
                qm = qm_s[h, pl.ds(r0, BLK), :]
                _, _, w, rc = _sb_tile(qm, km_s[h, pl.ds(r0, BLK), :], jnp.zeros((BLK, BLK), F32), uo, diag)
                acc = acc + _dot(w.astype(BF16), vm_s[h, pl.ds(r0, BLK), :])

                def kblock(t, c2):
                    rc2, acc2 = c2
                    c0 = pl.multiple_of((qi - 1 - t) * BLK, BLK)
                    _, _, w2, rc2 = _sb_tile(qm, km_s[h, pl.ds(c0, BLK), :], rc2, uo, None)
                    return rc2, acc2 + _dot(w2.astype(BF16), vm_s[h, pl.ds(c0, BLK), :])

                rc, acc = lax.fori_loop(0, qi, kblock, (rc, acc))
            o_ref[0, pl.ds(r0, BLK), :] = acc
            return carry

        lax.fori_loop(0, nq, qblock, 0)

    blk = lambda off: pl.BlockSpec((1, L, PAIR), lambda b, p: (b, 0, off + p))
    wspec = pl.BlockSpec((1, PAIR), lambda b, p: (0, 0))
    return pl.pallas_call(
        body,
        name="sb_attn_fwd",
        grid=(Bl, n_pair),
        in_specs=[blk(0), blk(n_pair), blk(2 * n_pair), wspec, wspec],
        out_specs=pl.BlockSpec((1, L, PAIR), lambda b, p: (b, 0, p)),
        out_shape=jax.ShapeDtypeStruct((Bl, L, D), F32),
        scratch_shapes=[pltpu.VMEM((2, L, PAIR), BF16)] * 3,
        compiler_params=_params(("parallel", "parallel")),
    )(proj3, proj3, proj3, qw2, kw2)


def _attn_bwd(proj3, o3, do3, qw2, kw2, D):
    Bl, L, _ = proj3.shape
    n_pair = D // PAIR
    nq = L // BLK
    scale = 1.0 / math.sqrt(HEAD_DIM)

    def body(q_ref, k_ref, v_ref, o_ref, do_ref, qw_ref, kw_ref, dq_ref, dk_ref, dv_ref, dw_ref,
             qm_s, km_s, vm_s, dom_s, dq_s, dk_s, dv_s):
        uo = _suffix_ones()
        diag = _iota((BLK, BLK), 1) < _iota((BLK, BLK), 0)
        lo = _iota((BLK, PAIR), 1) < HEAD_DIM
        _attn_prep(q_ref, qw_ref, qm_s, nq, scale)
        _attn_prep(k_ref, kw_ref, km_s, nq, 1.0)
        _attn_prep(v_ref, None, vm_s, nq, 1.0)
        _attn_prep(do_ref, None, dom_s, nq, 1.0)

        @pl.when((pl.program_id(0) == 0) & (pl.program_id(1) == 0))
        def _():
            dw_ref[...] = jnp.zeros_like(dw_ref)

        def zero(i, carry):
            r0 = pl.multiple_of(i * BLK, BLK)
            dk_s[pl.ds(r0, BLK), :] = jnp.zeros((BLK, PAIR), F32)
            dv_s[pl.ds(r0, BLK), :] = jnp.zeros((BLK, PAIR), F32)
            return carry

        lax.fori_loop(0, nq, zero, 0)

        def tile(h, qm, dom, delta, c0, rc, gc, dqa, mask):
            km = km_s[h, pl.ds(c0, BLK), :]
            z, e, w, rc_new = _sb_tile(qm, km, rc, uo, mask)
            wf = w.astype(BF16).astype(F32)
            g = wf * _dot_nt(dom, vm_s[h, pl.ds(c0, BLK), :])
            gs = _split_dot(g, uo)
            g_before = delta - (gc + gs[:, :BLK] + g)
            r = 1.0 / (1.0 + e)
            er = e * r
            pos = z >= 0.0
            s = jnp.where(pos, r, er)
            dz = g * jnp.where(pos, er, r) - g_before * s
            if mask is not None:
                dz = jnp.where(mask, dz, 0.0)
            dv_s[pl.ds(c0, BLK), :] += _dot(wf.T.astype(BF16), dom)
            dk_s[pl.ds(c0, BLK), :] += _dot(dz.T.astype(BF16), qm)
            return rc_new, gc + gs[:, BLK:], dqa + _dot(dz.astype(BF16), km)

        def qblock(qi, carry):
            r0 = pl.multiple_of(qi * BLK, BLK)
            o_blk = o_ref[0, pl.ds(r0, BLK), :]
            dqa = jnp.zeros((BLK, PAIR), F32)
            for h in range(2):
                qm = qm_s[h, pl.ds(r0, BLK), :]
                dom = dom_s[h, pl.ds(r0, BLK), :]
                delta = _rowsum(dom.astype(F32) * o_blk)
                zero_c = jnp.zeros((BLK, BLK), F32)
                rc, gc, dqa = tile(h, qm, dom, delta, r0, zero_c, zero_c, dqa, diag)

                def kblock(t, c2):
                    c0 = pl.multiple_of((qi - 1 - t) * BLK, BLK)
                    return tile(h, qm, dom, delta, c0, c2[0], c2[1], c2[2], None)

                rc, gc, dqa = lax.fori_loop(0, qi, kblock, (rc, gc, dqa))
            dq_s[pl.ds(r0, BLK), :] = dqa * scale
            return carry

        lax.fori_loop(0, nq, qblock, 0)

        def finish(i, carry):
            r0 = pl.multiple_of(i * BLK, BLK)
            dwq, dwk = carry
            out = []
            for src_ref, w_ref, d_s in ((q_ref, qw_ref, dq_s), (k_ref, kw_ref, dk_s)):
                v = src_ref[0, pl.ds(r0, BLK), :]
                r = _pair_rms(v, lo)
                vh = v * r
                dy = d_s[pl.ds(r0, BLK), :]
                dvh = dy * w_ref[...]
                out.append((r * (dvh - vh * _pair_mean(dvh * vh, lo)), _colsum(dy * vh)))
            dq_ref[0, pl.ds(r0, BLK), :] = out[0][0].astype(BF16)
            dk_ref[0, pl.ds(r0, BLK), :] = out[1][0].astype(BF16)
            dv_ref[0, pl.ds(r0, BLK), :] = dv_s[pl.ds(r0, BLK), :].astype(BF16)
            return dwq + out[0][1], dwk + out[1][1]

        zrow = jnp.zeros((1, PAIR), F32)
        dwq, dwk = lax.fori_loop(0, nq, finish, (zrow, zrow))
        dw_ref[0:1, :] += dwq
        dw_ref[1:2, :] += dwk

    blk = lambda off: pl.BlockSpec((1, L, PAIR), lambda b, p: (b, 0, off + p))
    wspec = pl.BlockSpec((1, PAIR), lambda b, p: (0, 0))
    oblk = pl.BlockSpec((1, L, PAIR), lambda b, p: (b, 0, p))
    return pl.pallas_call(
        body,
        name="sb_attn_bwd",
        grid=(Bl, n_pair),
        in_specs=[blk(0), blk(n_pair), blk(2 * n_pair), oblk, oblk, wspec, wspec],
        out_specs=[oblk, oblk, oblk, pl.BlockSpec((8, PAIR), lambda b, p: (0, 0))],
        out_shape=[jax.ShapeDtypeStruct((Bl, L, D), BF16)] * 3 + [jax.ShapeDtypeStruct((8, PAIR), F32)],
        scratch_shapes=[pltpu.VMEM((2, L, PAIR), BF16)] * 4 + [pltpu.VMEM((L, PAIR), F32)] * 3,
        compiler_params=_params(("arbitrary", "arbitrary")),
    )(proj3, proj3, proj3, o3, do3, qw2, kw2)


def _conv_pre(ext_s, halo_ref, raw_ref, w_ref, b_ref, first):
    ext_s[0:HALO, :] = jnp.where(first, 0.0, halo_ref[0])
    ext_s[HALO:HALO + BLK, :] = raw_ref[0]
    pre = b_ref[...]
    for i in range(CONV_K):
        pre = pre + ext_s[pl.ds(HALO - (CONV_K - 1 - i), BLK), :] * w_ref[i:i + 1, :]
    return pre


def _lane_col(m, lane, h):
    return _rowsum(jnp.where(lane == h, m, 0.0))


def _half_sums(row, lo1):
    return _rowsum(jnp.where(lo1, row, 0.0)), _rowsum(jnp.where(lo1, 0.0, row))


def _ssd_specs(Bl, L, D, rev):
    nc = L // BLK
    rows_per = BLK // HALO
    cidx = (lambda c: nc - 1 - c) if rev else (lambda c: c)
    xoff = 5
    boff = (6 * D) // 512
    doff = (6 * D + 512) // LANES
    prev = lambda c: jnp.maximum(cidx(c) * rows_per - 1, 0)
    specs = [
        pl.BlockSpec((1, BLK, D), lambda b, c: (b, cidx(c), xoff)),
        pl.BlockSpec((1, BLK, 512), lambda b, c: (b, cidx(c), boff)),
        pl.BlockSpec((1, HALO, D), lambda b, c: (b, prev(c), xoff)),
        pl.BlockSpec((1, HALO, 512), lambda b, c: (b, prev(c), boff)),
        pl.BlockSpec((1, BLK, LANES), lambda b, c: (b, cidx(c), doff)),
    ]
    full = lambda shape: pl.BlockSpec(shape, lambda b, c: (0,) * len(shape))
    specs += [full((CONV_K, D)), full((CONV_K, 512)), full((1, D)), full((1, 512)),
              full((1, LANES)), full((1, LANES)), full((1, LANES))]
    return specs, cidx


def _ssd_common(dtr_ref, dtb_ref, alog_ref, acs_s, acsT_s):
    ltri = jnp.where(_iota((BLK, BLK), 1) <= _iota((BLK, BLK), 0), 1.0, 0.0).astype(BF16)
    dtv = _softplus(dtr_ref[0] + dtb_ref[...])
    a = -jnp.exp(alog_ref[...])
    acs = _dot_split(ltri, dtv * a)
    acs_s[...] = acs
    acsT_s[...] = acs.T
    return dtv, a, acs


def _pair_terms(pr, acs, dtv, acs_s, lane, lo, lane1, lo1):
    h0, h1 = 2 * pr, 2 * pr + 1
    c0, c1 = _lane_col(acs, lane, h0), _lane_col(acs, lane, h1)
    d0, d1 = _lane_col(dtv, lane, h0), _lane_col(dtv, lane, h1)
    lastv = acs_s[BLK - 1:BLK, :]
    l0, l1 = _lane_col(lastv, lane1, h0), _lane_col(lastv, lane1, h1)
    return dict(h=(h0, h1), c=(c0, c1), last=(l0, l1), acs_p=jnp.where(lo, c0, c1), dt_p=jnp.where(lo, d0, d1),
                last_p=jnp.where(lo1, l0, l1))


def _decay_tiles(cc, row, tri, want_t):
    lm = jnp.where(tri, jnp.exp(jnp.where(tri, cc - row, 0.0)), 0.0)
    if not want_t:
        return lm, None
    tri_t = _iota((BLK, BLK), 1) >= _iota((BLK, BLK), 0)
    return lm, jnp.where(tri_t, jnp.exp(jnp.where(tri_t, row - cc, 0.0)), 0.0)


def _ssd_fwd(proj3, cwx, cwb, cbx, cbb, dtb, alog, dsk, D):
    Bl, L, _ = proj3.shape
    nc = L // BLK
    n_pair = D // PAIR
    pairs_per_group = n_pair // SSD_GROUPS
    specs, _ = _ssd_specs(Bl, L, D, False)

    def body(xr_ref, bcr_ref, xh_ref, bch_ref, dtr_ref, cwx_ref, cwb_ref, cbx_ref, cbb_ref, dtb_ref, alog_ref,
             dsk_ref, y_ref, sin_ref, st_s, extx_s, extb_s, acs_s, acsT_s):
        first = pl.program_id(1) == 0

        @pl.when(first)
        def _():
            st_s[...] = jnp.zeros_like(st_s)

        lane, lane1 = _iota((BLK, LANES), 1), _iota((1, LANES), 1)
        lo, lo1 = lane < HEAD_DIM, lane1 < HEAD_DIM
        tri = _iota((BLK, BLK), 1) <= _iota((BLK, BLK), 0)
        pre = _conv_pre(extx_s, xh_ref, xr_ref, cwx_ref, cbx_ref, first)
        ux = pre * _sigmoid(pre)
        pre = _conv_pre(extb_s, bch_ref, bcr_ref, cwb_ref, cbb_ref, first)
        ub = pre * _sigmoid(pre)
        dtv, a, acs = _ssd_common(dtr_ref, dtb_ref, alog_ref, acs_s, acsT_s)
        for g in range(SSD_GROUPS):
            bg = ub[:, g * SSD_STATE:(g + 1) * SSD_STATE]
            cb_ = ub[:, (SSD_GROUPS + g) * SSD_STATE:(SSD_GROUPS + g + 1) * SSD_STATE].astype(BF16)
            cbm = _dot_nt(cb_, bg.astype(BF16))
            btb = bg.T.astype(BF16)
            for pr in range(g * pairs_per_group, (g + 1) * pairs_per_group):
                t = _pair_terms(pr, acs, dtv, acs_s, lane, lo, lane1, lo1)
                xs_p = ux[:, pr * PAIR:(pr + 1) * PAIR]
                x_p = xs_p * t["dt_p"]
                st = st_s[pr]
                sin_ref[0, 0, pr] = st
                y = _dot(cb_, st.astype(BF16)) * jnp.exp(t["acs_p"])
                for k in range(2):
                    row = acsT_s[t["h"][k]:t["h"][k] + 1, :]
                    lm, _ = _decay_tiles(t["c"][k], row, tri, False)
                    xm = jnp.where(lo if k == 0 else ~lo, x_p, 0.0).astype(BF16)
                    y = y + _dot((cbm * lm).astype(BF16), xm)
                d_p = jnp.where(lo1, _lane_col(dsk_ref[...], lane1, t["h"][0]), _lane_col(dsk_ref[...], lane1, t["h"][1]))
                y_ref[0, :, pr * PAIR:(pr + 1) * PAIR] = y + d_p * xs_p
                xd = (x_p * jnp.exp(t["last_p"] - t["acs_p"])).astype(BF16)
                st_s[pr] = st * jnp.exp(t["last_p"]) + _dot(btb, xd)

    return pl.pallas_call(
        body,
        name="ssd_fwd",
        grid=(Bl, nc),
        in_specs=specs,
        out_specs=[
            pl.BlockSpec((1, BLK, D), lambda b, c: (b, c, 0)),
            pl.BlockSpec((1, 1, n_pair, SSD_STATE, PAIR), lambda b, c: (b, c, 0, 0, 0)),
        ],
        out_shape=[jax.ShapeDtypeStruct((Bl, L, D), F32),
                   jax.ShapeDtypeStruct((Bl, nc, n_pair, SSD_STATE, PAIR), F32)],
        scratch_shapes=[pltpu.VMEM((n_pair, SSD_STATE, PAIR), F32), pltpu.VMEM((HALO + BLK, D), F32),
                        pltpu.VMEM((HALO + BLK, 512), F32), pltpu.VMEM((BLK, LANES), F32),
                        pltpu.VMEM((LANES, BLK), F32)],
        compiler_params=_params(("arbitrary", "arbitrary")),
    )(proj3, proj3, proj3, proj3, proj3, cwx, cwb, cbx, cbb, dtb, alog, dsk)


def _ssd_bwd(proj3, s_in, dy3, cwx, cwb, cbx, cbb, dtb, alog, dsk, D):
    Bl, L, _ = proj3.shape
    nc = L // BLK
    n_pair = D // PAIR
    n_heads = 2 * n_pair
    pairs_per_group = n_pair // SSD_GROUPS
    specs, cidx = _ssd_specs(Bl, L, D, True)
    specs = specs + [
        pl.BlockSpec((1, 1, n_pair, SSD_STATE, PAIR), lambda b, c: (b, cidx(c), 0, 0, 0)),
        pl.BlockSpec((1, BLK, D), lambda b, c: (b, cidx(c), 0)),
    ]

    def body(xr_ref, bcr_ref, xh_ref, bch_ref, dtr_ref, cwx_ref, cwb_ref, cbx_ref, cbb_ref, dtb_ref, alog_ref,
             dsk_ref, sin_ref, dy_ref, dxbc_ref, ddt_ref, dcwx_ref, dcwb_ref, dcbx_ref, dcbb_ref, misc_ref,
             dst_s, extx_s, extb_s, acs_s, acsT_s, dux_s, dub_s, e2x_s, e2b_s, nxx_s, nxb_s):
        step = pl.program_id(1)
        first = step == nc - 1
        last = step == 0

        @pl.when(last)
        def _():
            dst_s[...] = jnp.zeros_like(dst_s)
            nxx_s[...] = jnp.zeros_like(nxx_s)
            nxb_s[...] = jnp.zeros_like(nxb_s)

        @pl.when(last & (pl.program_id(0) == 0))
        def _():
            for r in (dcwx_ref, dcwb_ref, dcbx_ref, dcbb_ref, misc_ref):
                r[...] = jnp.zeros_like(r)

        lane, lane1 = _iota((BLK, LANES), 1), _iota((1, LANES), 1)
        lo, lo1 = lane < HEAD_DIM, lane1 < HEAD_DIM
        tri = _iota((BLK, BLK), 1) <= _iota((BLK, BLK), 0)
        prex = _conv_pre(extx_s, xh_ref, xr_ref, cwx_ref, cbx_ref, first)
        sgx = _sigmoid(prex)
        ux = prex * sgx
        preb = _conv_pre(extb_s, bch_ref, bcr_ref, cwb_ref, cbb_ref, first)
        sgb = _sigmoid(preb)
        ub = preb * sgb
        dtv, a, acs = _ssd_common(dtr_ref, dtb_ref, alog_ref, acs_s, acsT_s)
        dacs = jnp.zeros((BLK, LANES), F32)
        dlast = jnp.zeros((1, LANES), F32)
        ddt = jnp.zeros((BLK, LANES), F32)
        dd = jnp.zeros((1, LANES), F32)
        for g in range(SSD_GROUPS):
            bg = ub[:, g * SSD_STATE:(g + 1) * SSD_STATE]
            cg = ub[:, (SSD_GROUPS + g) * SSD_STATE:(SSD_GROUPS + g + 1) * SSD_STATE]
            bb, cb_ = bg.astype(BF16), cg.astype(BF16)
            cbm = _dot_nt(cb_, bb)
            cbt = _dot_nt(bb, cb_)
            ctb = cg.T.astype(BF16)
            dbg = jnp.zeros((BLK, SSD_STATE), F32)
            dcg = jnp.zeros((BLK, SSD_STATE), F32)
            for pr in range(g * pairs_per_group, (g + 1) * pairs_per_group):
                t = _pair_terms(pr, acs, dtv, acs_s, lane, lo, lane1, lo1)
                h0, h1 = t["h"]
                xs_p = ux[:, pr * PAIR:(pr + 1) * PAIR]
                dy_p = dy_ref[0, :, pr * PAIR:(pr + 1) * PAIR]
                x_p = xs_p * t["dt_p"]
                ea_p = jnp.exp(t["acs_p"])
                dte_p = jnp.exp(t["last_p"] - t["acs_p"])
                cd_p = jnp.exp(t["last_p"])
                st = sin_ref[0, 0, pr]
                dst = dst_s[pr]
                stb, dstb = st.astype(BF16), dst.astype(BF16)
                s0, s1 = _half_sums(_colsum(dy_p * xs_p), lo1)
                dd = dd + jnp.where(lane1 == h0, s0, 0.0) + jnp.where(lane1 == h1, s1, 0.0)
                d_p = jnp.where(lo1, _lane_col(dsk_ref[...], lane1, h0), _lane_col(dsk_ref[...], lane1, h1))
                dxs_p = d_p * dy_p
                dp = dy_p * ea_p
                dpb = dp.astype(BF16)
                yo = dp * _dot(cb_, stb)
                dcg = dcg + _dot_nt(dpb, stb)
                dst_off = _dot(ctb, dpb)
                dac = [_rowsum(jnp.where(lo, yo, 0.0)), _rowsum(jnp.where(lo, 0.0, yo))]
                s0, s1 = _half_sums(_colsum(dst * st), lo1)
                dl = [s0 * jnp.exp(t["last"][0]), s1 * jnp.exp(t["last"][1])]
                dxd = _dot(bb, dstb)
                dx_p = dxd * dte_p
                tt = dxd * x_p
                dbg = dbg + _dot_nt((x_p * dte_p).astype(BF16), dstb)
                for k, ddte in enumerate((_rowsum(jnp.where(lo, tt, 0.0)), _rowsum(jnp.where(lo, 0.0, tt)))):
                    ek = ddte * jnp.exp(t["last"][k] - t["c"][k])
                    dl[k] = dl[k] + _colsum(ek)
                    dac[k] = dac[k] - ek
                for k in range(2):
                    row = acsT_s[t["h"][k]:t["h"][k] + 1, :]
                    lm, lmt = _decay_tiles(t["c"][k], row, tri, True)
                    msk = lo if k == 0 else ~lo
                    xm = jnp.where(msk, x_p, 0.0).astype(BF16)
                    dym = jnp.where(msk, dy_p, 0.0).astype(BF16)
                    dm = _dot_nt(dym, xm)
                    dmt = _dot_nt(xm, dym)
                    mt = cbt * lmt
                    dx_p = dx_p + _dot(mt.astype(BF16), dym)
                    dac[k] = dac[k] + _rowsum(dm * (cbm * lm)) - _rowsum(dmt * mt)
                    dcg = dcg + _dot((dm * lm).astype(BF16), bb)
                    dbg = dbg + _dot((dmt * lmt).astype(BF16), cb_)
                dacs = dacs + jnp.where(lane == h0, dac[0], 0.0) + jnp.where(lane == h1, dac[1], 0.0)
                dlast = dlast + jnp.where(lane1 == h0, dl[0], 0.0) + jnp.where(lane1 == h1, dl[1], 0.0)
                dxs_p = dxs_p + dx_p * t["dt_p"]
                t3 = dx_p * xs_p
                ddt = ddt + jnp.where(lane == h0, _rowsum(jnp.where(lo, t3, 0.0)), 0.0) \
                    + jnp.where(lane == h1, _rowsum(jnp.where(lo, 0.0, t3)), 0.0)
                dux_s[:, pr * PAIR:(pr + 1) * PAIR] = dxs_p
                dst_s[pr] = dst * cd_p + dst_off
            dub_s[:, g * SSD_STATE:(g + 1) * SSD_STATE] = dbg
            dub_s[:, (SSD_GROUPS + g) * SSD_STATE:(SSD_GROUPS + g + 1) * SSD_STATE] = dcg
        dacs = dacs + jnp.where(_iota((BLK, LANES), 0) == BLK - 1, dlast, 0.0)
        utri = jnp.where(_iota((BLK, BLK), 1) >= _iota((BLK, BLK), 0), 1.0, 0.0).astype(BF16)
        dda = _dot_split(utri, dacs)
        ddt = ddt + dda * a
        ddtr = jnp.where(lane < n_heads, ddt * _sigmoid(dtr_ref[0] + dtb_ref[...]), 0.0)
        ddt_ref[0] = ddtr.astype(BF16)
        misc_ref[0:1, :] += _colsum(ddtr)
        misc_ref[1:2, :] += jnp.where(lane1 < n_heads, _colsum(dda * dtv) * a, 0.0)
        misc_ref[2:3, :] += dd
        for (du_s, pre, sg, ext_s, e2_s, nx_s, w_ref, dcw_ref, dcb_ref, c0, width) in (
                (dux_s, prex, sgx, extx_s, e2x_s, nxx_s, cwx_ref, dcwx_ref, dcbx_ref, 0, D),
                (dub_s, preb, sgb, extb_s, e2b_s, nxb_s, cwb_ref, dcwb_ref, dcbb_ref, D, 512)):
            dpre = du_s[...] * (sg * (1.0 + pre * (1.0 - sg)))
            dcb_ref[...] += _colsum(dpre)
            for i in range(CONV_K):
                dcw_ref[i:i + 1, :] += _colsum(dpre * ext_s[pl.ds(HALO - (CONV_K - 1 - i), BLK), :])
            e2_s[0:BLK, :] = dpre
            e2_s[BLK:BLK + HALO, :] = nx_s[...]
            dxr = jnp.zeros((BLK, width), F32)
            for i in range(CONV_K):
                dxr = dxr + e2_s[pl.ds(CONV_K - 1 - i, BLK), :] * w_ref[i:i + 1, :]
            dxbc_ref[0, :, c0:c0 + width] = dxr.astype(BF16)
            nx_s[...] = e2_s[0:HALO, :]

    full = lambda shape: pl.BlockSpec(shape, lambda b, c: (0,) * len(shape))
    return pl.pallas_call(
        body,
        name="ssd_bwd",
        grid=(Bl, nc),
        in_specs=specs,
        out_specs=[
            pl.BlockSpec((1, BLK, D + 512), lambda b, c: (b, cidx(c), 0)),
            pl.BlockSpec((1, BLK, LANES), lambda b, c: (b, cidx(c), 0)),
            full((CONV_K, D)), full((CONV_K, 512)), full((1, D)), full((1, 512)), full((8, LANES)),
        ],
        out_shape=[
            jax.ShapeDtypeStruct((Bl, L, D + 512), BF16), jax.ShapeDtypeStruct((Bl, L, LANES), BF16),
            jax.ShapeDtypeStruct((CONV_K, D), F32), jax.ShapeDtypeStruct((CONV_K, 512), F32),
            jax.ShapeDtypeStruct((1, D), F32), jax.ShapeDtypeStruct((1, 512), F32),
            jax.ShapeDtypeStruct((8, LANES), F32),
        ],
        scratch_shapes=[
            pltpu.VMEM((n_pair, SSD_STATE, PAIR), F32),
            pltpu.VMEM((HALO + BLK, D), F32), pltpu.VMEM((HALO + BLK, 512), F32),
            pltpu.VMEM((BLK, LANES), F32), pltpu.VMEM((LANES, BLK), F32),
            pltpu.VMEM((BLK, D), F32), pltpu.VMEM((BLK, 512), F32),
            pltpu.VMEM((BLK + HALO, D), F32), pltpu.VMEM((BLK + HALO, 512), F32),
            pltpu.VMEM((HALO, D), F32), pltpu.VMEM((HALO, 512), F32),
        ],
        compiler_params=_params(("arbitrary", "arbitrary")),
    )(proj3, proj3, proj3, proj3, proj3, cwx, cwb, cbx, cbb, dtb, alog, dsk, s_in, dy3)


def _gate_out(x2, tgt2, o2, proj2, y2, sbw, ssw, w_out_bf):
    T, D = x2.shape
    tm = min(256, T)

    def body(x_ref, t_ref, o_ref, zs_ref, y_ref, zy_ref, sbw_ref, ssw_ref, wo_ref,
             dout_ref, doutb_ref, mix_ref, do_ref, dy_ref, dz_ref, dnw_ref, loss_ref):
        @pl.when(pl.program_id(0) == 0)
        def _():
            dnw_ref[...] = jnp.zeros_like(dnw_ref)
            loss_ref[...] = jnp.zeros_like(loss_ref)

        def fwd(o, z, w):
            sg = _sigmoid(z)
            sl = z * sg
            g = o * sl
            r = lax.rsqrt(jnp.mean(g * g, axis=-1, keepdims=True) + EPS)
            n = g * r
            return sg, sl, r, n, (n * w).astype(BF16)

        def bwd(dy, o, z, w, sg, sl, r, n):
            dn = dy * w
            dg = r * (dn - n * jnp.mean(dn * n, axis=-1, keepdims=True))
            return dg * sl, dg * o * (sg * (1.0 + z * (1.0 - sg))), _colsum(dy * n)

        o1, z1, w1 = o_ref[...], zs_ref[...], sbw_ref[...]
        o2_, z2, w2 = y_ref[...], zy_ref[...], ssw_ref[...]
        sg1, sl1, r1, n1, y1b = fwd(o1, z1, w1)
        sg2, sl2, r2, n2, y2b = fwd(o2_, z2, w2)
        mix_ref[:, 0:D] = y1b
        mix_ref[:, D:2 * D] = y2b
        out = x_ref[...] + (_dot(y1b, wo_ref[0:D, :]) + _dot(y2b, wo_ref[D:2 * D, :]))
        err = out - t_ref[...]
        loss_ref[...] += jnp.sum(err * err) * (0.5 / D)
        dout = err * (1.0 / D)
        dout_ref[...] = dout
        doutb = dout.astype(BF16)
        doutb_ref[...] = doutb
        do1, dz1, dw1 = bwd(_dot_nt(doutb, wo_ref[0:D, :]), o1, z1, w1, sg1, sl1, r1, n1)
        do2, dz2, dw2 = bwd(_dot_nt(doutb, wo_ref[D:2 * D, :]), o2_, z2, w2, sg2, sl2, r2, n2)
        do_ref[...] = do1
        dy_ref[...] = do2
        dz_ref[:, 0:D] = dz1.astype(BF16)
        dz_ref[:, D:2 * D] = dz2.astype(BF16)
        dnw_ref[0:1, :] += dw1
        dnw_ref[1:2, :] += dw2

    row = lambda col: pl.BlockSpec((tm, D), lambda i: (i, col))
    full = lambda shape: pl.BlockSpec(shape, lambda i: (0,) * len(shape))
    wide = pl.BlockSpec((tm, 2 * D), lambda i: (i, 0))
    return pl.pallas_call(
        body,
        name="gate_out",
        grid=(T // tm,),
        in_specs=[row(0), row(0), row(0), row(3), row(0), row(4), full((1, D)), full((1, D)), full((2 * D, D))],
        out_specs=[row(0), row(0), wide, row(0), row(0), wide, full((8, D)), full((8, LANES))],
        out_shape=[
            jax.ShapeDtypeStruct((T, D), F32), jax.ShapeDtypeStruct((T, D), BF16),
            jax.ShapeDtypeStruct((T, 2 * D), BF16), jax.ShapeDtypeStruct((T, D), F32),
            jax.ShapeDtypeStruct((T, D), F32), jax.ShapeDtypeStruct((T, 2 * D), BF16),
            jax.ShapeDtypeStruct((8, D), F32), jax.ShapeDtypeStruct((8, LANES), F32),
        ],
        compiler_params=_params(("arbitrary",)),
    )(x2, tgt2, o2, proj2, y2, proj2, sbw, ssw, w_out_bf)


def _dhn(dproj, w_pad, x2, dout, norm_w):
    T, D = x2.shape
    P = w_pad.shape[1]
    tm = min(512, T)
    tk = 1024 if P % 1024 == 0 else 512
    nk = P // tk

    def body(dp_ref, w_ref, x_ref, dout_ref, nw_ref, gx_ref, dnw_ref, acc_s):
        k = pl.program_id(1)

        @pl.when((pl.program_id(0) == 0) & (k == 0))
        def _():
            dnw_ref[...] = jnp.zeros_like(dnw_ref)

        @pl.when(k == 0)
        def _():
            acc_s[...] = jnp.zeros_like(acc_s)

        acc_s[...] += _dot_nt(dp_ref[...], w_ref[...])

        @pl.when(k == nk - 1)
        def _():
            xv = x_ref[...]
            r = lax.rsqrt(jnp.mean(xv * xv, axis=-1, keepdims=True) + EPS)
            xh = xv * r
            dhn = acc_s[...]
            dxh = dhn * nw_ref[...]
            gx_ref[...] = dout_ref[...] + r * (dxh - xh * jnp.mean(dxh * xh, axis=-1, keepdims=True))
            dnw_ref[0:1, :] += _colsum(dhn * xh)

    return pl.pallas_call(
        body,
        name="dhn",
        grid=(T // tm, nk),
        in_specs=[
            pl.BlockSpec((tm, tk), lambda i, k: (i, k)),
            pl.BlockSpec((D, tk), lambda i, k: (0, k)),
            pl.BlockSpec((tm, D), lambda i, k: (i, 0)),
            pl.BlockSpec((tm, D), lambda i, k: (i, 0)),
            pl.BlockSpec((1, D), lambda i, k: (0, 0)),
        ],
        out_specs=[pl.BlockSpec((tm, D), lambda i, k: (i, 0)), pl.BlockSpec((8, D), lambda i, k: (0, 0))],
        out_shape=[jax.ShapeDtypeStruct((T, D), F32), jax.ShapeDtypeStruct((8, D), F32)],
        scratch_shapes=[pltpu.VMEM((tm, D), F32)],
        compiler_params=_params(("arbitrary", "arbitrary")),
    )(dproj, w_pad, x2, dout, norm_w)


def _matmul(a, b, name):
    M, K = a.shape
    N = b.shape[1]
    tm = min(1024, M)
    tn = 1024 if N % 1024 == 0 else (512 if N % 512 == 0 else N)
    tk = min(512, K)

    def body(a_ref, b_ref, o_ref):
        @pl.when(pl.program_id(2) == 0)
        def _():
            o_ref[...] = jnp.zeros_like(o_ref)

        o_ref[...] += _dot(a_ref[...], b_ref[...])

    return pl.pallas_call(
        body,
        name=name,
        grid=(M // tm, N // tn, K // tk),
        in_specs=[pl.BlockSpec((tm, tk), lambda i, j, k: (i, k)), pl.BlockSpec((tk, tn), lambda i, j, k: (k, j))],
        out_specs=pl.BlockSpec((tm, tn), lambda i, j, k: (i, j)),
        out_shape=jax.ShapeDtypeStruct((M, N), F32),
        compiler_params=_params(("parallel", "parallel", "arbitrary")),
    )(a, b)


def _adamw(w, g, m, v, name):
    R, C = w.shape
    tr = 256 if R % 256 == 0 else R
    c1 = 1.0 - ADAM_B1 ** ADAM_STEP
    c2 = 1.0 - ADAM_B2 ** ADAM_STEP

    def body(w_ref, g_ref, m_ref, v_ref, d_ref, nm_ref, nv_ref):
        gv = g_ref[...]
        m_new = ADAM_B1 * m_ref[...] + (1.0 - ADAM_B1) * gv
        v_new = ADAM_B2 * v_ref[...] + (1.0 - ADAM_B2) * (gv * gv)
        d_ref[...] = -ADAM_LR * ((m_new / c1) / (jnp.sqrt(v_new / c2) + ADAM_EPS) + ADAM_WD * w_ref[...])
        nm_ref[...] = m_new
        nv_ref[...] = v_new

    spec = pl.BlockSpec((tr, C), lambda i: (i, 0))
    return pl.pallas_call(
        body,
        name=name,
        grid=(R // tr,),
        in_specs=[spec] * 4,
        out_specs=[spec] * 3,
        out_shape=[jax.ShapeDtypeStruct((R, C), F32)] * 3,
        compiler_params=_params(("parallel",)),
    )(w, g, m, v)


def _add_core_half(a, recv, core, name):
    _, n, h, S = a.shape
    th = 256 if h % 256 == 0 else h

    def body(c_ref, a_ref, r_ref, o_ref):
        o_ref[...] = a_ref[...] + r_ref[...]

    return pl.pallas_call(
        body,
        name=name,
        grid_spec=pltpu.PrefetchScalarGridSpec(
            num_scalar_prefetch=1,
            grid=(n, h // th),
            in_specs=[
                pl.BlockSpec((None, None, th, S), lambda p, i, c: (c[0], p, i, 0)),
                pl.BlockSpec((None, th, S), lambda p, i, c: (p, i, 0)),
            ],
            out_specs=pl.BlockSpec((None, th, S), lambda p, i, c: (p, i, 0)),
        ),
        out_shape=jax.ShapeDtypeStruct((n, h, S), F32),
        compiler_params=_params(("parallel", "parallel")),
    )(core, a, recv)


def _add_chips(hsum, recv, chip, name):
    _, h, S = hsum.shape
    th = 256 if h % 256 == 0 else h

    def body(c_ref, a_ref, r_ref, o_ref):
        o_ref[...] = ((a_ref[...] + r_ref[0]) + r_ref[1]) + r_ref[2]

    return pl.pallas_call(
        body,
        name=name,
        grid_spec=pltpu.PrefetchScalarGridSpec(
            num_scalar_prefetch=1,
            grid=(h // th,),
            in_specs=[
                pl.BlockSpec((None, th, S), lambda i, c: (c[0], i, 0)),
                pl.BlockSpec((N_CHIPS - 1, th, S), lambda i, c: (0, i, 0)),
            ],
            out_specs=pl.BlockSpec((th, S), lambda i, c: (i, 0)),
        ),
        out_shape=jax.ShapeDtypeStruct((h, S), F32),
        compiler_params=_params(("parallel",)),
    )(chip, hsum, recv)


def _place():
    x, y, c = lax.axis_index("x"), lax.axis_index("y"), lax.axis_index("c")
    other_chips = [(1 - x, y), (x, 1 - y), (1 - x, 1 - y)]
    return x, y, c, other_chips


def _allgather_weights(w_in_bf, w_out_bf, conv_w):
    D, S = w_in_bf.shape
    R = w_out_bf.shape[0]
    n_ici, n_fwd = 3 * (N_CHIPS - 1), 2 * (N_CHIPS - 1)

    def body(win, wout, cw, gin, gout, gcw, send_sems, recv_sems, local_sems):
        x, y, c, chips = _place()
        me = 2 * x + y
        sibling = (x, y, 1 - c)
        hin, hout = D // 2, R // 2

        def halves(chip_idx):
            return (gin.at[chip_idx, pl.ds(c * hin, hin)], gout.at[chip_idx, pl.ds(c * hout, hout)])

        def rcopy(k, src, dst, to):
            return pltpu.make_async_remote_copy(src_ref=src, dst_ref=dst, send_sem=send_sems.at[k],
                                                recv_sem=recv_sems.at[k], device_id=to, device_id_type=MESH)

        local = [pltpu.make_async_copy(win, gin.at[me], local_sems.at[0]),
                 pltpu.make_async_copy(wout, gout.at[me], local_sems.at[1]),
                 pltpu.make_async_copy(cw, gcw.at[me], local_sems.at[2])]
        for cp in local:
            cp.start()
        my_in, my_out = halves(me)
        src_in, src_out = win.at[pl.ds(c * hin, hin)], wout.at[pl.ds(c * hout, hout)]
        sends = []
        for j, chip in enumerate(chips):
            to = (*chip, c)
            sends += [rcopy(3 * j, src_in, my_in, to), rcopy(3 * j + 1, src_out, my_out, to),
                      rcopy(3 * j + 2, cw, gcw.at[me], to)]
        for cp in sends:
            cp.start()
        passed = []
        for j, (px, py) in enumerate(chips):
            their_in, their_out = halves(2 * px + py)
            rcopy(3 * j, their_in, their_in, sibling).wait_recv()
            rcopy(3 * j + 1, their_out, their_out, sibling).wait_recv()
            rcopy(3 * j + 2, cw, gcw.at[2 * px + py], sibling).wait_recv()
            fw = [rcopy(n_ici + 2 * j, their_in, their_in, sibling), rcopy(n_ici + 2 * j + 1, their_out, their_out, sibling)]
            for cp in fw:
                cp.start()
            passed += fw
        for j, (px, py) in enumerate(chips):
            oin = gin.at[2 * px + py, pl.ds((1 - c) * hin, hin)]
            oout = gout.at[2 * px + py, pl.ds((1 - c) * hout, hout)]
            rcopy(n_ici + 2 * j, oin, oin, sibling).wait_recv()
            rcopy(n_ici + 2 * j + 1, oout, oout, sibling).wait_recv()
        for cp in sends + passed:
            cp.wait_send()
        for cp in local:
            cp.wait()

    return pl.pallas_call(
        body,
        name="allgather_weights",
        in_specs=[ANY, ANY, ANY],
        out_specs=[ANY, ANY, ANY],
        out_shape=[jax.ShapeDtypeStruct((N_CHIPS, D, S), BF16), jax.ShapeDtypeStruct((N_CHIPS, R, D), BF16),
                   jax.ShapeDtypeStruct((N_CHIPS,) + conv_w.shape, F32)],
        scratch_shapes=[pltpu.SemaphoreType.DMA((n_ici + n_fwd,)), pltpu.SemaphoreType.DMA((n_ici + n_fwd,)),
                        pltpu.SemaphoreType.DMA((3,))],
    )(w_in_bf, w_out_bf, conv_w)


def _allreduce_small(packed):
    R = packed.shape[0]
    n_dev = 2 * N_CHIPS

    def body(p_ref, o_ref, buf, send_sems, recv_sems):
        x, y, c, _ = _place()
        me = 4 * x + 2 * y + c
        buf[me] = p_ref[...]
        copies = []
        for k in range(1, n_dev):
            px = 1 - x if k & 4 else x
            py = 1 - y if k & 2 else y
            pc = 1 - c if k & 1 else c
            copies.append((pltpu.make_async_remote_copy(
                src_ref=buf.at[me], dst_ref=buf.at[me], send_sem=send_sems.at[k - 1], recv_sem=recv_sems.at[k - 1],
                device_id=(px, py, pc), device_id_type=MESH), 4 * px + 2 * py + pc, (px, py, pc)))
        for cp, _, _ in copies:
            cp.start()
        for k, (_, peer, to) in enumerate(copies):
            pltpu.make_async_remote_copy(
                src_ref=buf.at[peer], dst_ref=buf.at[peer], send_sem=send_sems.at[k], recv_sem=recv_sems.at[k],
                device_id=to, device_id_type=MESH).wait_recv()
        for cp, _, _ in copies:
            cp.wait_send()
        acc = buf[0]
        for d in range(1, n_dev):
            acc = acc + buf[d]
        o_ref[...] = acc

    vm = pl.BlockSpec(memory_space=pltpu.VMEM)
    return pl.pallas_call(
        body,
        name="allreduce_small",
        in_specs=[vm],
        out_specs=vm,
        out_shape=jax.ShapeDtypeStruct((R, LANES), F32),
        scratch_shapes=[pltpu.VMEM((n_dev, R, LANES), F32), pltpu.SemaphoreType.DMA((n_dev - 1,)),
                        pltpu.SemaphoreType.DMA((n_dev - 1,))],
    )(packed)


def _swap_core_halves(a_in, a_out):
    def body(ain, aout, rin, rout, send_sems, recv_sems):
        x, y, c, _ = _place()
        cps = [pltpu.make_async_remote_copy(src_ref=s.at[1 - c], dst_ref=d, send_sem=send_sems.at[k],
                                            recv_sem=recv_sems.at[k], device_id=(x, y, 1 - c), device_id_type=MESH)
               for k, (s, d) in enumerate(((ain, rin), (aout, rout)))]
        for cp in cps:
            cp.start()
        for cp in cps:
            cp.wait()

    return pl.pallas_call(
        body,
        name="reduce_core_swap",
        in_specs=[ANY, ANY],
        out_specs=[ANY, ANY],
        out_shape=[jax.ShapeDtypeStruct(a_in.shape[1:], F32), jax.ShapeDtypeStruct(a_out.shape[1:], F32)],
        scratch_shapes=[pltpu.SemaphoreType.DMA((2,)), pltpu.SemaphoreType.DMA((2,))],
    )(a_in, a_out)


def _scatter_to_chips(h_in, h_out):
    def body(hin, hout, rin, rout, send_sems, recv_sems):
        x, y, c, chips = _place()
        cps = []
        for j, (px, py) in enumerate(chips):
            for k, (s, d) in enumerate(((hin, rin), (hout, rout))):
                cps.append(pltpu.make_async_remote_copy(
                    src_ref=s.at[2 * px + py], dst_ref=d.at[j], send_sem=send_sems.at[2 * j + k],
                    recv_sem=recv_sems.at[2 * j + k], device_id=(px, py, c), device_id_type=MESH))
        for cp in cps:
            cp.start()
        for cp in cps:
            cp.wait()

    n = 2 * (N_CHIPS - 1)
    return pl.pallas_call(
        body,
        name="reduce_chip_scatter",
        in_specs=[ANY, ANY],
        out_specs=[ANY, ANY],
        out_shape=[jax.ShapeDtypeStruct((N_CHIPS - 1,) + h_in.shape[1:], F32),
                   jax.ShapeDtypeStruct((N_CHIPS - 1,) + h_out.shape[1:], F32)],
        scratch_shapes=[pltpu.SemaphoreType.DMA((n,)), pltpu.SemaphoreType.DMA((n,))],
    )(h_in, h_out)


def _join_core_halves(g_in, g_out):
    def body(gin, gout, fin, fout, send_sems, recv_sems, local_sems):
        x, y, c, _ = _place()
        local = [pltpu.make_async_copy(s, d.at[c], local_sems.at[k]) for k, (s, d) in enumerate(((gin, fin), (gout, fout)))]
        cps = [pltpu.make_async_remote_copy(src_ref=s, dst_ref=d.at[c], send_sem=send_sems.at[k],
                                            recv_sem=recv_sems.at[k], device_id=(x, y, 1 - c), device_id_type=MESH)
               for k, (s, d) in enumerate(((gin, fin), (gout, fout)))]
        for cp in local + cps:
            cp.start()
        for k, (s, d) in enumerate(((gin, fin), (gout, fout))):
            pltpu.make_async_remote_copy(src_ref=s, dst_ref=d.at[1 - c], send_sem=send_sems.at[k],
                                         recv_sem=recv_sems.at[k], device_id=(x, y, 1 - c),
                                         device_id_type=MESH).wait_recv()
        for cp in cps:
            cp.wait_send()
        for cp in local:
            cp.wait()

    return pl.pallas_call(
        body,
        name="reduce_core_join",
        in_specs=[ANY, ANY],
        out_specs=[ANY, ANY],
        out_shape=[jax.ShapeDtypeStruct((2,) + g_in.shape, F32), jax.ShapeDtypeStruct((2,) + g_out.shape, F32)],
        scratch_shapes=[pltpu.SemaphoreType.DMA((2,)), pltpu.SemaphoreType.DMA((2,)), pltpu.SemaphoreType.DMA((2,))],
    )(g_in, g_out)


def _pack(arrays):
    rows = []
    for a in arrays:
        flat = a.reshape(-1).astype(F32)
        n = -(-flat.shape[0] // LANES) * LANES
        rows.append(jnp.pad(flat, (0, n - flat.shape[0])).reshape(-1, LANES))
    out = jnp.concatenate(rows, axis=0)
    return jnp.pad(out, ((0, -out.shape[0] % 8), (0, 0)))


def _unpack(packed, shapes):
    out, r = [], 0
    for shp in shapes:
        n = math.prod(shp)
        nr = -(-n // LANES)
        out.append(packed[r:r + nr].reshape(-1)[:n].reshape(shp))
        r += nr
    return out


def _pad_lanes(a):
    return jnp.pad(a, ((0, 0), (0, LANES - a.shape[1])))


def kernel(x, norm_w, w_in, q_norm_w, k_norm_w, conv_w, conv_b, dt_bias, A_log, D_skip, sb_norm_w, ssd_norm_w, w_out, loss_target, m_norm_w, m_w_in, m_q_norm_w, m_k_norm_w, m_conv_w, m_conv_b, m_dt_bias, m_A_log, m_D_skip, m_sb_norm_w, m_ssd_norm_w, m_w_out, v_norm_w, v_w_in, v_q_norm_w, v_k_norm_w, v_conv_w, v_conv_b, v_dt_bias, v_A_log, v_D_skip, v_sb_norm_w, v_ssd_norm_w, v_w_out):
    Bl, L, D = x.shape
    T = Bl * L
    S = w_in.shape[2]
    R = w_out.shape[1]
    CW = conv_w.shape[2]
    n_in = N_CHIPS * S
    CD = D + 2 * SSD_GROUPS * SSD_STATE
    H = D // HEAD_DIM
    n_main = 6 * D + 512
    P = -(-(n_main + LANES) // 1024) * 1024
    assert n_in == n_main + H and CD == N_CHIPS * CW and 2 * D == N_CHIPS * R and CD == D + 512
    chip = (2 * lax.axis_index("x") + lax.axis_index("y")).astype(jnp.int32)
    core = lax.axis_index("c").astype(jnp.int32)

    g_in, g_out, g_cw = _allgather_weights(w_in[0].astype(BF16), w_out[0].astype(BF16), conv_w[0])
    w_pad = jnp.pad(g_in.transpose(1, 0, 2).reshape(D, n_in), ((0, 0), (0, P - n_in)))
    w_out_bf = g_out.reshape(2 * D, D)
    conv_full = g_cw.transpose(1, 0, 2).reshape(CONV_K, CD)
    cwx, cwb = conv_full[:, :D], conv_full[:, D:]
    cbx, cbb = conv_b[:, :D], conv_b[:, D:]
    dtb, alog, dsk = _pad_lanes(dt_bias), _pad_lanes(A_log), _pad_lanes(D_skip)
    qw2, kw2 = jnp.tile(q_norm_w, (1, 2)), jnp.tile(k_norm_w, (1, 2))

    x2 = x.reshape(T, D)
    proj, hn = _inproj(x2, norm_w, w_pad)
    proj3 = proj.reshape(Bl, L, P)
    o_sb = _attn_fwd(proj3, qw2, kw2, D)
    y_ssd, s_in = _ssd_fwd(proj3, cwx, cwb, cbx, cbb, dtb, alog, dsk, D)
    dout, dout_bf, mixed_bf, do_sb, dy_ssd, dz_bf, dnw_out, loss_blk = _gate_out(
        x2, loss_target.reshape(T, D), o_sb.reshape(T, D), proj, y_ssd.reshape(T, D), sb_norm_w, ssd_norm_w, w_out_bf)

    dq, dk, dv, dqkw = _attn_bwd(proj3, o_sb, do_sb.reshape(Bl, L, D), qw2, kw2, D)
    dxbc, ddt, dcwx, dcwb, dcbx, dcbb, misc = _ssd_bwd(
        proj3, s_in, dy_ssd.reshape(Bl, L, D), cwx, cwb, cbx, cbb, dtb, alog, dsk, D)
    dproj = jnp.concatenate(
        [dq.reshape(T, D), dk.reshape(T, D), dv.reshape(T, D), dz_bf, dxbc.reshape(T, CD), ddt.reshape(T, LANES),
         jnp.zeros((T, P - n_main - LANES), BF16)], axis=1)
    grad_x2, dnw_in = _dhn(dproj, w_pad, x2, dout, norm_w)
    gw_in = _matmul(hn.T, dproj, "grad_w_in")[:, :n_in]
    gw_out = _matmul(mixed_bf.T, dout_bf, "grad_w_out")

    small_shapes = [(1, D), (1, D), (1, D), (1, CD), (1, HEAD_DIM), (1, HEAD_DIM), (1, H), (1, H), (1, H)]
    g_small_local = [dnw_in[0:1], dnw_out[0:1], dnw_out[1:2], jnp.concatenate([dcbx, dcbb], axis=1),
                     dqkw[0:1, :HEAD_DIM] + dqkw[0:1, HEAD_DIM:], dqkw[1:2, :HEAD_DIM] + dqkw[1:2, HEAD_DIM:],
                     misc[0:1, :H], misc[1:2, :H], misc[2:3, :H]]
    packed = _pack(g_small_local + [jnp.concatenate([dcwx, dcwb], axis=1), loss_blk[0:1, 0:1]])
    red = _allreduce_small(packed)
    g_small = _unpack(red, small_shapes + [(CONV_K, CD), (1, 1)])
    g_conv_w = lax.dynamic_slice_in_dim(g_small[9], chip * CW, CW, axis=1)
    loss = g_small[10][0, 0]

    a_in = gw_in.reshape(2, D // 2, N_CHIPS, S).transpose(0, 2, 1, 3)
    a_out = gw_out.reshape(N_CHIPS, 2, R // 2, D).transpose(1, 0, 2, 3)
    r_in, r_out = _swap_core_halves(a_in, a_out)
    core1, chip1 = core.reshape(1), chip.reshape(1)
    h_in = _add_core_half(a_in, r_in, core1, "sum_cores_w_in")
    h_out = _add_core_half(a_out, r_out, core1, "sum_cores_w_out")
    s_in_, s_out_ = _scatter_to_chips(h_in, h_out)
    gh_in = _add_chips(h_in, s_in_, chip1, "sum_chips_w_in")
    gh_out = _add_chips(h_out, s_out_, chip1, "sum_chips_w_out")
    f_in, f_out = _join_core_halves(gh_in, gh_out)
    g_w_in = f_in.reshape(D, S)
    g_w_out = f_out.reshape(R, D)

    d_in, nm_in, nv_in = _adamw(w_in[0], g_w_in, m_w_in[0], v_w_in[0], "adamw_w_in")
    d_out, nm_out, nv_out = _adamw(w_out[0], g_w_out, m_w_out[0], v_w_out[0], "adamw_w_out")
    d_cw, nm_cw, nv_cw = _adamw(conv_w[0], g_conv_w, m_conv_w[0], v_conv_w[0], "adamw_conv_w")
    small_w = [norm_w, sb_norm_w, ssd_norm_w, conv_b, q_norm_w, k_norm_w, dt_bias, A_log, D_skip]
    small_m = [m_norm_w, m_sb_norm_w, m_ssd_norm_w, m_conv_b, m_q_norm_w, m_k_norm_w, m_dt_bias, m_A_log, m_D_skip]
    small_v = [v_norm_w, v_sb_norm_w, v_ssd_norm_w, v_conv_b, v_q_norm_w, v_k_norm_w, v_dt_bias, v_A_log, v_D_skip]
    d_s, nm_s, nv_s = _adamw(_pack(small_w), _pack(g_small[:9]), _pack(small_m), _pack(small_v), "adamw_small")
    d_s, nm_s, nv_s = (_unpack(t, small_shapes) for t in (d_s, nm_s, nv_s))

    def ordered(s, w_in_, conv_w_, w_out_):
        return [s[0], w_in_[None], s[4], s[5], conv_w_[None], s[3], s[6], s[7], s[8], s[1], s[2], w_out_[None]]

    return (loss, grad_x2.reshape(Bl, L, D),
            *ordered(g_small[:9], g_w_in, g_conv_w, g_w_out),
            *ordered(d_s, d_in, d_cw, d_out),
            *ordered(nm_s, nm_in, nm_cw, nm_out),
            *ordered(nv_s, nv_in, nv_cw, nv_out))
```

```python
import functools
import math

import jax
import jax.numpy as jnp
from jax import lax
from jax.experimental import pallas as pl
from jax.experimental.pallas import tpu as pltpu

F32 = jnp.float32
BF16 = jnp.bfloat16
EPS = 1e-6
HEAD_DIM = 64
PAIR = 2 * HEAD_DIM
LANES = 128
SSD_STATE = 128
SSD_GROUPS = 2
BLK = 128
UNDERFLOW = -105.0
CONV_K = 4
HALO = 8
N_CHIPS = 4
ADAM_LR, ADAM_B1, ADAM_B2, ADAM_EPS, ADAM_WD, ADAM_STEP = 0.001, 0.9, 0.999, 1e-08, 0.01, 10
VMEM_LIMIT_V7X = 56 * 1024 * 1024
MESH = pl.DeviceIdType.MESH
ANY = pl.BlockSpec(memory_space=pl.ANY)
NT = (((1,), (1,)), ((), ()))


def _params(sem=None):
    kw = dict(vmem_limit_bytes=VMEM_LIMIT_V7X)
    if sem is not None:
        kw["dimension_semantics"] = sem
    return pltpu.CompilerParams(**kw)


def _dot(a, b):
    return jnp.dot(a, b, preferred_element_type=F32)


def _dot_nt(a, b):
    return lax.dot_general(a, b, NT, preferred_element_type=F32)


def _split_dot(x, m):
    hi = x.astype(BF16)
    lo = (x - hi.astype(F32)).astype(BF16)
    return _dot(hi, m) + _dot(lo, m)


def _dot_split(m, x):
    hi = x.astype(BF16)
    lo = (x - hi.astype(F32)).astype(BF16)
    return _dot(m, hi) + _dot(m, lo)


def _iota(shape, dim):
    return lax.broadcasted_iota(jnp.int32, shape, dim)


def _rowsum(x):
    return jnp.sum(x, axis=1, keepdims=True)


def _colsum(x):
    return jnp.sum(x, axis=0, keepdims=True)


def _sigmoid(x):
    return 1.0 / (1.0 + jnp.exp(-x))


def _softplus(x):
    return jnp.maximum(x, 0.0) + jnp.log(1.0 + jnp.exp(-jnp.abs(x)))


def _inproj(x2, norm_w, w_pad):
    T, D = x2.shape
    P = w_pad.shape[1]
    tm = min(512, T)
    tn = 1024 if P % 1024 == 0 else 512

    def body(x_ref, nw_ref, w_ref, proj_ref, hn_ref):
        @pl.when(pl.program_id(1) == 0)
        def _():
            xv = x_ref[...]
            r = lax.rsqrt(jnp.mean(xv * xv, axis=-1, keepdims=True) + EPS)
            hn_ref[...] = (xv * r * nw_ref[...]).astype(BF16)

        proj_ref[...] = _dot(hn_ref[...], w_ref[...])

    return pl.pallas_call(
        body,
        name="inproj",
        grid=(T // tm, P // tn),
        in_specs=[
            pl.BlockSpec((tm, D), lambda i, j: (i, 0)),
            pl.BlockSpec((1, D), lambda i, j: (0, 0)),
            pl.BlockSpec((D, tn), lambda i, j: (0, j)),
        ],
        out_specs=[
            pl.BlockSpec((tm, tn), lambda i, j: (i, j)),
            pl.BlockSpec((tm, D), lambda i, j: (i, 0)),
        ],
        out_shape=[jax.ShapeDtypeStruct((T, P), F32), jax.ShapeDtypeStruct((T, D), BF16)],
        compiler_params=_params(("parallel", "arbitrary")),
    )(x2, norm_w, w_pad)


def _pair_rms(v, lo):
    sq = v * v
    s0 = _rowsum(jnp.where(lo, sq, 0.0))
    s1 = _rowsum(jnp.where(lo, 0.0, sq))
    return lax.rsqrt(jnp.where(lo, s0, s1) * (1.0 / HEAD_DIM) + EPS)


def _pair_mean(v, lo):
    s0 = _rowsum(jnp.where(lo, v, 0.0))
    s1 = _rowsum(jnp.where(lo, 0.0, v))
    return jnp.where(lo, s0, s1) * (1.0 / HEAD_DIM)


def _suffix_ones():
    ri = _iota((BLK, 2 * BLK), 0)
    ci = _iota((BLK, 2 * BLK), 1)
    return jnp.where((ci >= BLK) | (ri > ci), 1.0, 0.0).astype(BF16)


def _sb_tile(qm, km, rest_carry, uo, diag):
    z = _dot_nt(qm, km)
    e = jnp.exp(-jnp.abs(z))
    a = jnp.minimum(z, 0.0) - jnp.log(1.0 + e)
    lk = a - z
    if diag is not None:
        lk = jnp.where(diag, lk, 0.0)
    cs = _split_dot(lk, uo)
    w = jnp.exp(a + rest_carry + cs[:, :BLK])
    if diag is not None:
        w = jnp.where(diag, w, 0.0)
    return z, e, w, rest_carry + cs[:, BLK:]


def _attn_prep(src_ref, w_ref, dst_s, n_blocks, scale):
    lo = _iota((BLK, PAIR), 1) < HEAD_DIM

    def step(i, carry):
        r0 = pl.multiple_of(i * BLK, BLK)
        v = src_ref[0, pl.ds(r0, BLK), :]
        if w_ref is not None:
            v = v * _pair_rms(v, lo) * w_ref[...]
        if scale != 1.0:
            v = v * scale
        dst_s[0, pl.ds(r0, BLK), :] = jnp.where(lo, v, 0.0).astype(BF16)
        dst_s[1, pl.ds(r0, BLK), :] = jnp.where(lo, 0.0, v).astype(BF16)
        return carry

    lax.fori_loop(0, n_blocks, step, 0)


def _attn_fwd(proj3, qw2, kw2, D):
    Bl, L, _ = proj3.shape
    n_pair = D // PAIR
    nq = L // BLK
    scale = 1.0 / math.sqrt(HEAD_DIM)

    def body(q_ref, k_ref, v_ref, qw_ref, kw_ref, o_ref, qm_s, km_s, vm_s):
        uo = _suffix_ones()
        diag = _iota((BLK, BLK), 1) < _iota((BLK, BLK), 0)
        _attn_prep(q_ref, qw_ref, qm_s, nq, scale)
        _attn_prep(k_ref, kw_ref, km_s, nq, 1.0)
        _attn_prep(v_ref, None, vm_s, nq, 1.0)

        def qblock(qi, carry):
            r0 = pl.multiple_of(qi * BLK, BLK)
            qms = [qm_s[h, pl.ds(r0, BLK), :] for h in range(2)]

            def sweep(state, c0s, mask):
                kb, rc0, rc1, acc, _ = state
                rc = [rc0, rc1]
                for c0 in c0s:
                    for h in range(2):
                        _, _, w, rc[h] = _sb_tile(qms[h], km_s[h, pl.ds(c0, BLK), :], rc[h], uo, mask)
                        acc = acc + _dot(w.astype(BF16), vm_s[h, pl.ds(c0, BLK), :])
                return kb - len(c0s), rc[0], rc[1], acc, jnp.maximum(jnp.max(rc[0]), jnp.max(rc[1]))

            def left(state, n):
                return [pl.multiple_of((state[0] - u) * BLK, BLK) for u in range(n)]

            zero_c = jnp.zeros((BLK, BLK), F32)
            state = sweep((qi, zero_c, zero_c, jnp.zeros((BLK, PAIR), F32), 0.0), [r0], diag)
            state = lax.while_loop(lambda s: (s[0] >= 1) & (s[4] >= UNDERFLOW), lambda s: sweep(s, left(s, 2), None), state)
            state = lax.while_loop(lambda s: (s[0] >= 0) & (s[4] >= UNDERFLOW), lambda s: sweep(s, left(s, 1), None), state)
            o_ref[0, pl.ds(r0, BLK), :] = state[3]
            return carry

        lax.fori_loop(0, nq, qblock, 0)

    blk = lambda off: pl.BlockSpec((1, L, PAIR), lambda b, p: (b, 0, off + p))
    wspec = pl.BlockSpec((1, PAIR), lambda b, p: (0, 0))
    return pl.pallas_call(
        body,
        name="sb_attn_fwd",
        grid=(Bl, n_pair),
        in_specs=[blk(0), blk(n_pair), blk(2 * n_pair), wspec, wspec],
        out_specs=pl.BlockSpec((1, L, PAIR), lambda b, p: (b, 0, p)),
        out_shape=jax.ShapeDtypeStruct((Bl, L, D), F32),
        scratch_shapes=[pltpu.VMEM((2, L, PAIR), BF16)] * 3,
        compiler_params=_params(("parallel", "parallel")),
    )(proj3, proj3, proj3, qw2, kw2)


def _attn_bwd(proj3, o3, do3, qw2, kw2, D):
    Bl, L, _ = proj3.shape
    n_pair = D // PAIR
    nq = L // BLK
    scale = 1.0 / math.sqrt(HEAD_DIM)

    def body(q_ref, k_ref, v_ref, o_ref, do_ref, qw_ref, kw_ref, dq_ref, dk_ref, dv_ref, dw_ref,
             qm_s, km_s, vm_s, dom_s, dq_s, dk_s, dv_s):
        uo = _suffix_ones()
        diag = _iota((BLK, BLK), 1) < _iota((BLK, BLK), 0)
        lo = _iota((BLK, PAIR), 1) < HEAD_DIM
        _attn_prep(q_ref, qw_ref, qm_s, nq, scale)
        _attn_prep(k_ref, kw_ref, km_s, nq, 1.0)
        _attn_prep(v_ref, None, vm_s, nq, 1.0)
        _attn_prep(do_ref, None, dom_s, nq, 1.0)

        @pl.when((pl.program_id(0) == 0) & (pl.program_id(1) == 0))
        def _():
            dw_ref[...] = jnp.zeros_like(dw_ref)

        def zero(i, carry):
            r0 = pl.multiple_of(i * BLK, BLK)
            dk_s[pl.ds(r0, BLK), :] = jnp.zeros((BLK, PAIR), F32)
            dv_s[pl.ds(r0, BLK), :] = jnp.zeros((BLK, PAIR), F32)
            return carry

        lax.fori_loop(0, nq, zero, 0)

        def tile(h, qm, dom, delta, c0, rc, gc, dqa, mask):
            km = km_s[h, pl.ds(c0, BLK), :]
            z, e, w, rc_new = _sb_tile(qm, km, rc, uo, mask)
            wf = w.astype(BF16).astype(F32)
            g = wf * _dot_nt(dom, vm_s[h, pl.ds(c0, BLK), :])
            gs = _split_dot(g, uo)
            g_before = delta - (gc + gs[:, :BLK] + g)
            r = 1.0 / (1.0 + e)
            er = e * r
            pos = z >= 0.0
            s = jnp.where(pos, r, er)
            dz = g * jnp.where(pos, er, r) - g_before * s
            if mask is not None:
                dz = jnp.where(mask, dz, 0.0)
            dv_s[pl.ds(c0, BLK), :] += _dot(wf.T.astype(BF16), dom)
            dk_s[pl.ds(c0, BLK), :] += _dot(dz.T.astype(BF16), qm)
            return rc_new, gc + gs[:, BLK:], dqa + _dot(dz.astype(BF16), km)

        def qblock(qi, carry):
            r0 = pl.multiple_of(qi * BLK, BLK)
            o_blk = o_ref[0, pl.ds(r0, BLK), :]
            qms = [qm_s[h, pl.ds(r0, BLK), :] for h in range(2)]
            doms = [dom_s[h, pl.ds(r0, BLK), :] for h in range(2)]
            deltas = [_rowsum(doms[h].astype(F32) * o_blk) for h in range(2)]

            def sweep(state, c0s, mask):
                kb, rc0, rc1, gc0, gc1, dqa, _ = state
                rc, gc = [rc0, rc1], [gc0, gc1]
                for c0 in c0s:
                    for h in range(2):
                        rc[h], gc[h], dqa = tile(h, qms[h], doms[h], deltas[h], c0, rc[h], gc[h], dqa, mask)
                return kb - len(c0s), rc[0], rc[1], gc[0], gc[1], dqa, jnp.maximum(jnp.max(rc[0]), jnp.max(rc[1]))

            def left(state, n):
                return [pl.multiple_of((state[0] - u) * BLK, BLK) for u in range(n)]

            zero_c = jnp.zeros((BLK, BLK), F32)
            state = sweep((qi, zero_c, zero_c, zero_c, zero_c, jnp.zeros((BLK, PAIR), F32), 0.0), [r0], diag)
            state = lax.while_loop(lambda s: (s[0] >= 1) & (s[6] >= UNDERFLOW), lambda s: sweep(s, left(s, 2), None), state)
            state = lax.while_loop(lambda s: (s[0] >= 0) & (s[6] >= UNDERFLOW), lambda s: sweep(s, left(s, 1), None), state)
            dq_s[pl.ds(r0, BLK), :] = state[5] * scale
            return carry

        lax.fori_loop(0, nq, qblock, 0)

        def finish(i, carry):
            r0 = pl.multiple_of(i * BLK, BLK)
            dwq, dwk = carry
            out = []
            for src_ref, w_ref, d_s in ((q_ref, qw_ref, dq_s), (k_ref, kw_ref, dk_s)):
                v = src_ref[0, pl.ds(r0, BLK), :]
                r = _pair_rms(v, lo)
                vh = v * r
                dy = d_s[pl.ds(r0, BLK), :]
                dvh = dy * w_ref[...]
                out.append((r * (dvh - vh * _pair_mean(dvh * vh, lo)), _colsum(dy * vh)))
            dq_ref[0, pl.ds(r0, BLK), :] = out[0][0].astype(BF16)
            dk_ref[0, pl.ds(r0, BLK), :] = out[1][0].astype(BF16)
            dv_ref[0, pl.ds(r0, BLK), :] = dv_s[pl.ds(r0, BLK), :].astype(BF16)
            return dwq + out[0][1], dwk + out[1][1]

        zrow = jnp.zeros((1, PAIR), F32)
        dwq, dwk = lax.fori_loop(0, nq, finish, (zrow, zrow))
        dw_ref[0:1, :] += dwq
        dw_ref[1:2, :] += dwk

    blk = lambda off: pl.BlockSpec((1, L, PAIR), lambda b, p: (b, 0, off + p))
    wspec = pl.BlockSpec((1, PAIR), lambda b, p: (0, 0))
    oblk = pl.BlockSpec((1, L, PAIR), lambda b, p: (b, 0, p))
    return pl.pallas_call(
        body,
        name="sb_attn_bwd",
        grid=(Bl, n_pair),
        in_specs=[blk(0), blk(n_pair), blk(2 * n_pair), oblk, oblk, wspec, wspec],
        out_specs=[oblk, oblk, oblk, pl.BlockSpec((8, PAIR), lambda b, p: (0, 0))],
        out_shape=[jax.ShapeDtypeStruct((Bl, L, D), BF16)] * 3 + [jax.ShapeDtypeStruct((8, PAIR), F32)],
        scratch_shapes=[pltpu.VMEM((2, L, PAIR), BF16)] * 4 + [pltpu.VMEM((L, PAIR), F32)] * 3,
        compiler_params=_params(("arbitrary", "arbitrary")),
    )(proj3, proj3, proj3, o3, do3, qw2, kw2)


def _conv_pre(ext_s, halo_ref, raw_ref, w_ref, b_ref, first):
    ext_s[0:HALO, :] = jnp.where(first, 0.0, halo_ref[0])
    ext_s[HALO:HALO + BLK, :] = raw_ref[0]
    pre = b_ref[...]
    for i in range(CONV_K):
        pre = pre + ext_s[pl.ds(HALO - (CONV_K - 1 - i), BLK), :] * w_ref[i:i + 1, :]
    return pre


def _lane_col(m, lane, h):
    return _rowsum(jnp.where(lane == h, m, 0.0))


def _half_sums(row, lo1):
    return _rowsum(jnp.where(lo1, row, 0.0)), _rowsum(jnp.where(lo1, 0.0, row))


def _ssd_specs(Bl, L, D, rev):
    nc = L // BLK
    rows_per = BLK // HALO
    cidx = (lambda c: nc - 1 - c) if rev else (lambda c: c)
    xoff = 5
    boff = (6 * D) // 512
    doff = (6 * D + 512) // LANES
    prev = lambda c: jnp.maximum(cidx(c) * rows_per - 1, 0)
    specs = [
        pl.BlockSpec((1, BLK, D), lambda b, c: (b, cidx(c), xoff)),
        pl.BlockSpec((1, BLK, 512), lambda b, c: (b, cidx(c), boff)),
        pl.BlockSpec((1, HALO, D), lambda b, c: (b, prev(c), xoff)),
        pl.BlockSpec((1, HALO, 512), lambda b, c: (b, prev(c), boff)),
        pl.BlockSpec((1, BLK, LANES), lambda b, c: (b, cidx(c), doff)),
    ]
    full = lambda shape: pl.BlockSpec(shape, lambda b, c: (0,) * len(shape))
    specs += [full((CONV_K, D)), full((CONV_K, 512)), full((1, D)), full((1, 512)),
              full((1, LANES)), full((1, LANES)), full((1, LANES))]
    return specs, cidx


def _ssd_common(dtr_ref, dtb_ref, alog_ref, acs_s, acsT_s):
    ltri = jnp.where(_iota((BLK, BLK), 1) <= _iota((BLK, BLK), 0), 1.0, 0.0).astype(BF16)
    dtv = _softplus(dtr_ref[0] + dtb_ref[...])
    a = -jnp.exp(alog_ref[...])
    acs = _dot_split(ltri, dtv * a)
    acs_s[...] = acs
    acsT_s[...] = acs.T
    return dtv, a, acs


def _pair_terms(pr, acs, dtv, acs_s, lane, lo, lane1, lo1):
    h0, h1 = 2 * pr, 2 * pr + 1
    c0, c1 = _lane_col(acs, lane, h0), _lane_col(acs, lane, h1)
    d0, d1 = _lane_col(dtv, lane, h0), _lane_col(dtv, lane, h1)
    lastv = acs_s[BLK - 1:BLK, :]
    l0, l1 = _lane_col(lastv, lane1, h0), _lane_col(lastv, lane1, h1)
    return dict(h=(h0, h1), c=(c0, c1), last=(l0, l1), acs_p=jnp.where(lo, c0, c1), dt_p=jnp.where(lo, d0, d1),
                last_p=jnp.where(lo1, l0, l1))


def _decay_tiles(cc, row, tri, want_t):
    lm = jnp.where(tri, jnp.exp(jnp.where(tri, cc - row, 0.0)), 0.0)
    if not want_t:
        return lm, None
    tri_t = _iota((BLK, BLK), 1) >= _iota((BLK, BLK), 0)
    return lm, jnp.where(tri_t, jnp.exp(jnp.where(tri_t, row - cc, 0.0)), 0.0)


def _ssd_fwd(proj3, cwx, cwb, cbx, cbb, dtb, alog, dsk, D):
    Bl, L, _ = proj3.shape
    nc = L // BLK
    n_pair = D // PAIR
    pairs_per_group = n_pair // SSD_GROUPS
    specs, _ = _ssd_specs(Bl, L, D, False)

    def body(xr_ref, bcr_ref, xh_ref, bch_ref, dtr_ref, cwx_ref, cwb_ref, cbx_ref, cbb_ref, dtb_ref, alog_ref,
             dsk_ref, y_ref, sin_ref, st_s, extx_s, extb_s, acs_s, acsT_s):
        first = pl.program_id(1) == 0

        @pl.when(first)
        def _():
            st_s[...] = jnp.zeros_like(st_s)

        lane, lane1 = _iota((BLK, LANES), 1), _iota((1, LANES), 1)
        lo, lo1 = lane < HEAD_DIM, lane1 < HEAD_DIM
        tri = _iota((BLK, BLK), 1) <= _iota((BLK, BLK), 0)
        pre = _conv_pre(extx_s, xh_ref, xr_ref, cwx_ref, cbx_ref, first)
        ux = pre * _sigmoid(pre)
        pre = _conv_pre(extb_s, bch_ref, bcr_ref, cwb_ref, cbb_ref, first)
        ub = pre * _sigmoid(pre)
        dtv, a, acs = _ssd_common(dtr_ref, dtb_ref, alog_ref, acs_s, acsT_s)
        for g in range(SSD_GROUPS):
            bg = ub[:, g * SSD_STATE:(g + 1) * SSD_STATE]
            cb_ = ub[:, (SSD_GROUPS + g) * SSD_STATE:(SSD_GROUPS + g + 1) * SSD_STATE].astype(BF16)
            cbm = _dot_nt(cb_, bg.astype(BF16))
            btb = bg.T.astype(BF16)
            for pr in range(g * pairs_per_group, (g + 1) * pairs_per_group):
                t = _pair_terms(pr, acs, dtv, acs_s, lane, lo, lane1, lo1)
                xs_p = ux[:, pr * PAIR:(pr + 1) * PAIR]
                x_p = xs_p * t["dt_p"]
                st = st_s[pr]
                sin_ref[0, 0, pr] = st
                y = _dot(cb_, st.astype(BF16)) * jnp.exp(t["acs_p"])
                for k in range(2):
                    row = acsT_s[t["h"][k]:t["h"][k] + 1, :]
                    lm, _ = _decay_tiles(t["c"][k], row, tri, False)
                    xm = jnp.where(lo if k == 0 else ~lo, x_p, 0.0).astype(BF16)
                    y = y + _dot((cbm * lm).astype(BF16), xm)
                d_p = jnp.where(lo1, _lane_col(dsk_ref[...], lane1, t["h"][0]), _lane_col(dsk_ref[...], lane1, t["h"][1]))
                y_ref[0, :, pr * PAIR:(pr + 1) * PAIR] = y + d_p * xs_p
                xd = (x_p * jnp.exp(t["last_p"] - t["acs_p"])).astype(BF16)
                st_s[pr] = st * jnp.exp(t["last_p"]) + _dot(btb, xd)

    return pl.pallas_call(
        body,
        name="ssd_fwd",
        grid=(Bl, nc),
        in_specs=specs,
        out_specs=[
            pl.BlockSpec((1, BLK, D), lambda b, c: (b, c, 0)),
            pl.BlockSpec((1, 1, n_pair, SSD_STATE, PAIR), lambda b, c: (b, c, 0, 0, 0)),
        ],
        out_shape=[jax.ShapeDtypeStruct((Bl, L, D), F32),
                   jax.ShapeDtypeStruct((Bl, nc, n_pair, SSD_STATE, PAIR), F32)],
        scratch_shapes=[pltpu.VMEM((n_pair, SSD_STATE, PAIR), F32), pltpu.VMEM((HALO + BLK, D), F32),
                        pltpu.VMEM((HALO + BLK, 512), F32), pltpu.VMEM((BLK, LANES), F32),
                        pltpu.VMEM((LANES, BLK), F32)],
        compiler_params=_params(("arbitrary", "arbitrary")),
    )(proj3, proj3, proj3, proj3, proj3, cwx, cwb, cbx, cbb, dtb, alog, dsk)


def _ssd_bwd(proj3, s_in, dy3, cwx, cwb, cbx, cbb, dtb, alog, dsk, D):
    Bl, L, _ = proj3.shape
    nc = L // BLK
    n_pair = D // PAIR
    n_heads = 2 * n_pair
    pairs_per_group = n_pair // SSD_GROUPS
    specs, cidx = _ssd_specs(Bl, L, D, True)
    specs = specs + [
        pl.BlockSpec((1, 1, n_pair, SSD_STATE, PAIR), lambda b, c: (b, cidx(c), 0, 0, 0)),
        pl.BlockSpec((1, BLK, D), lambda b, c: (b, cidx(c), 0)),
    ]

    def body(xr_ref, bcr_ref, xh_ref, bch_ref, dtr_ref, cwx_ref, cwb_ref, cbx_ref, cbb_ref, dtb_ref, alog_ref,
             dsk_ref, sin_ref, dy_ref, dxbc_ref, ddt_ref, dcwx_ref, dcwb_ref, dcbx_ref, dcbb_ref, misc_ref,
             dst_s, extx_s, extb_s, acs_s, acsT_s, dux_s, dub_s, e2x_s, e2b_s, nxx_s, nxb_s):
        step = pl.program_id(1)
        first = step == nc - 1
        last = step == 0

        @pl.when(last)
        def _():
            dst_s[...] = jnp.zeros_like(dst_s)
            nxx_s[...] = jnp.zeros_like(nxx_s)
            nxb_s[...] = jnp.zeros_like(nxb_s)

        @pl.when(last & (pl.program_id(0) == 0))
        def _():
            for r in (dcwx_ref, dcwb_ref, dcbx_ref, dcbb_ref, misc_ref):
                r[...] = jnp.zeros_like(r)

        lane, lane1 = _iota((BLK, LANES), 1), _iota((1, LANES), 1)
        lo, lo1 = lane < HEAD_DIM, lane1 < HEAD_DIM
        tri = _iota((BLK, BLK), 1) <= _iota((BLK, BLK), 0)
        prex = _conv_pre(extx_s, xh_ref, xr_ref, cwx_ref, cbx_ref, first)
        sgx = _sigmoid(prex)
        ux = prex * sgx
        preb = _conv_pre(extb_s, bch_ref, bcr_ref, cwb_ref, cbb_ref, first)
        sgb = _sigmoid(preb)
        ub = preb * sgb
        dtv, a, acs = _ssd_common(dtr_ref, dtb_ref, alog_ref, acs_s, acsT_s)
        dacs = jnp.zeros((BLK, LANES), F32)
        dlast = jnp.zeros((1, LANES), F32)
        ddt = jnp.zeros((BLK, LANES), F32)
        dd = jnp.zeros((1, LANES), F32)
        for g in range(SSD_GROUPS):
            bg = ub[:, g * SSD_STATE:(g + 1) * SSD_STATE]
            cg = ub[:, (SSD_GROUPS + g) * SSD_STATE:(SSD_GROUPS + g + 1) * SSD_STATE]
            bb, cb_ = bg.astype(BF16), cg.astype(BF16)
            cbm = _dot_nt(cb_, bb)
            cbt = _dot_nt(bb, cb_)
            ctb = cg.T.astype(BF16)
            dbg = jnp.zeros((BLK, SSD_STATE), F32)
            dcg = jnp.zeros((BLK, SSD_STATE), F32)
            for pr in range(g * pairs_per_group, (g + 1) * pairs_per_group):
                t = _pair_terms(pr, acs, dtv, acs_s, lane, lo, lane1, lo1)
                h0, h1 = t["h"]
                xs_p = ux[:, pr * PAIR:(pr + 1) * PAIR]
                dy_p = dy_ref[0, :, pr * PAIR:(pr + 1) * PAIR]
                x_p = xs_p * t["dt_p"]
                ea_p = jnp.exp(t["acs_p"])
                dte_p = jnp.exp(t["last_p"] - t["acs_p"])
                cd_p = jnp.exp(t["last_p"])
                st = sin_ref[0, 0, pr]
                dst = dst_s[pr]
                stb, dstb = st.astype(BF16), dst.astype(BF16)
                s0, s1 = _half_sums(_colsum(dy_p * xs_p), lo1)
                dd = dd + jnp.where(lane1 == h0, s0, 0.0) + jnp.where(lane1 == h1, s1, 0.0)
                d_p = jnp.where(lo1, _lane_col(dsk_ref[...], lane1, h0), _lane_col(dsk_ref[...], lane1, h1))
                dxs_p = d_p * dy_p
                dp = dy_p * ea_p
                dpb = dp.astype(BF16)
                yo = dp * _dot(cb_, stb)
                dcg = dcg + _dot_nt(dpb, stb)
                dst_off = _dot(ctb, dpb)
                dac = [_rowsum(jnp.where(lo, yo, 0.0)), _rowsum(jnp.where(lo, 0.0, yo))]
                s0, s1 = _half_sums(_colsum(dst * st), lo1)
                dl = [s0 * jnp.exp(t["last"][0]), s1 * jnp.exp(t["last"][1])]
                dxd = _dot(bb, dstb)
                dx_p = dxd * dte_p
                tt = dxd * x_p
                dbg = dbg + _dot_nt((x_p * dte_p).astype(BF16), dstb)
                for k, ddte in enumerate((_rowsum(jnp.where(lo, tt, 0.0)), _rowsum(jnp.where(lo, 0.0, tt)))):
                    ek = ddte * jnp.exp(t["last"][k] - t["c"][k])
                    dl[k] = dl[k] + _colsum(ek)
                    dac[k] = dac[k] - ek
                for k in range(2):
                    row = acsT_s[t["h"][k]:t["h"][k] + 1, :]
                    lm, lmt = _decay_tiles(t["c"][k], row, tri, True)
                    msk = lo if k == 0 else ~lo
                    xm = jnp.where(msk, x_p, 0.0).astype(BF16)
                    dym = jnp.where(msk, dy_p, 0.0).astype(BF16)
                    dm = _dot_nt(dym, xm)
                    dmt = _dot_nt(xm, dym)
                    mt = cbt * lmt
                    dx_p = dx_p + _dot(mt.astype(BF16), dym)
                    dac[k] = dac[k] + _rowsum(dm * (cbm * lm)) - _rowsum(dmt * mt)
                    dcg = dcg + _dot((dm * lm).astype(BF16), bb)
                    dbg = dbg + _dot((dmt * lmt).astype(BF16), cb_)
                dacs = dacs + jnp.where(lane == h0, dac[0], 0.0) + jnp.where(lane == h1, dac[1], 0.0)
                dlast = dlast + jnp.where(lane1 == h0, dl[0], 0.0) + jnp.where(lane1 == h1, dl[1], 0.0)
                dxs_p = dxs_p + dx_p * t["dt_p"]
                t3 = dx_p * xs_p
                ddt = ddt + jnp.where(lane == h0, _rowsum(jnp.where(lo, t3, 0.0)), 0.0) \
                    + jnp.where(lane == h1, _rowsum(jnp.where(lo, 0.0, t3)), 0.0)
                dux_s[:, pr * PAIR:(pr + 1) * PAIR] = dxs_p
                dst_s[pr] = dst * cd_p + dst_off
            dub_s[:, g * SSD_STATE:(g + 1) * SSD_STATE] = dbg
            dub_s[:, (SSD_GROUPS + g) * SSD_STATE:(SSD_GROUPS + g + 1) * SSD_STATE] = dcg
        dacs = dacs + jnp.where(_iota((BLK, LANES), 0) == BLK - 1, dlast, 0.0)
        utri = jnp.where(_iota((BLK, BLK), 1) >= _iota((BLK, BLK), 0), 1.0, 0.0).astype(BF16)
        dda = _dot_split(utri, dacs)
        ddt = ddt + dda * a
        ddtr = jnp.where(lane < n_heads, ddt * _sigmoid(dtr_ref[0] + dtb_ref[...]), 0.0)
        ddt_ref[0] = ddtr.astype(BF16)
        misc_ref[0:1, :] += _colsum(ddtr)
        misc_ref[1:2, :] += jnp.where(lane1 < n_heads, _colsum(dda * dtv) * a, 0.0)
        misc_ref[2:3, :] += dd
        for (du_s, pre, sg, ext_s, e2_s, nx_s, w_ref, dcw_ref, dcb_ref, c0, width) in (
                (dux_s, prex, sgx, extx_s, e2x_s, nxx_s, cwx_ref, dcwx_ref, dcbx_ref, 0, D),
                (dub_s, preb, sgb, extb_s, e2b_s, nxb_s, cwb_ref, dcwb_ref, dcbb_ref, D, 512)):
            dpre = du_s[...] * (sg * (1.0 + pre * (1.0 - sg)))
            dcb_ref[...] += _colsum(dpre)
            for i in range(CONV_K):
                dcw_ref[i:i + 1, :] += _colsum(dpre * ext_s[pl.ds(HALO - (CONV_K - 1 - i), BLK), :])
            e2_s[0:BLK, :] = dpre
            e2_s[BLK:BLK + HALO, :] = nx_s[...]
            dxr = jnp.zeros((BLK, width), F32)
            for i in range(CONV_K):
                dxr = dxr + e2_s[pl.ds(CONV_K - 1 - i, BLK), :] * w_ref[i:i + 1, :]
            dxbc_ref[0, :, c0:c0 + width] = dxr.astype(BF16)
            nx_s[...] = e2_s[0:HALO, :]

    full = lambda shape: pl.BlockSpec(shape, lambda b, c: (0,) * len(shape))
    return pl.pallas_call(
        body,
        name="ssd_bwd",
        grid=(Bl, nc),
        in_specs=specs,
        out_specs=[
            pl.BlockSpec((1, BLK, D + 512), lambda b, c: (b, cidx(c), 0)),
            pl.BlockSpec((1, BLK, LANES), lambda b, c: (b, cidx(c), 0)),
            full((CONV_K, D)), full((CONV_K, 512)), full((1, D)), full((1, 512)), full((8, LANES)),
        ],
        out_shape=[
            jax.ShapeDtypeStruct((Bl, L, D + 512), BF16), jax.ShapeDtypeStruct((Bl, L, LANES), BF16),
            jax.ShapeDtypeStruct((CONV_K, D), F32), jax.ShapeDtypeStruct((CONV_K, 512), F32),
            jax.ShapeDtypeStruct((1, D), F32), jax.ShapeDtypeStruct((1, 512), F32),
            jax.ShapeDtypeStruct((8, LANES), F32),
        ],
        scratch_shapes=[
            pltpu.VMEM((n_pair, SSD_STATE, PAIR), F32),
            pltpu.VMEM((HALO + BLK, D), F32), pltpu.VMEM((HALO + BLK, 512), F32),
            pltpu.VMEM((BLK, LANES), F32), pltpu.VMEM((LANES, BLK), F32),
            pltpu.VMEM((BLK, D), F32), pltpu.VMEM((BLK, 512), F32),
            pltpu.VMEM((BLK + HALO, D), F32), pltpu.VMEM((BLK + HALO, 512), F32),
            pltpu.VMEM((HALO, D), F32), pltpu.VMEM((HALO, 512), F32),
        ],
        compiler_params=_params(("arbitrary", "arbitrary")),
    )(proj3, proj3, proj3, proj3, proj3, cwx, cwb, cbx, cbb, dtb, alog, dsk, s_in, dy3)


def _gate_out(x2, tgt2, o2, proj2, y2, sbw, ssw, w_out_bf):
    T, D = x2.shape
    tm = min(256, T)

    def body(x_ref, t_ref, o_ref, zs_ref, y_ref, zy_ref, sbw_ref, ssw_ref, wo_ref,
             dout_ref, doutb_ref, mix_ref, do_ref, dy_ref, dz_ref, dnw_ref, loss_ref):
        @pl.when(pl.program_id(0) == 0)
        def _():
            dnw_ref[...] = jnp.zeros_like(dnw_ref)
            loss_ref[...] = jnp.zeros_like(loss_ref)

        def fwd(o, z, w):
            sg = _sigmoid(z)
            sl = z * sg
            g = o * sl
            r = lax.rsqrt(jnp.mean(g * g, axis=-1, keepdims=True) + EPS)
            n = g * r
            return sg, sl, r, n, (n * w).astype(BF16)

        def bwd(dy, o, z, w, sg, sl, r, n):
            dn = dy * w
            dg = r * (dn - n * jnp.mean(dn * n, axis=-1, keepdims=True))
            return dg * sl, dg * o * (sg * (1.0 + z * (1.0 - sg))), _colsum(dy * n)

        o1, z1, w1 = o_ref[...], zs_ref[...], sbw_ref[...]
        o2_, z2, w2 = y_ref[...], zy_ref[...], ssw_ref[...]
        sg1, sl1, r1, n1, y1b = fwd(o1, z1, w1)
        sg2, sl2, r2, n2, y2b = fwd(o2_, z2, w2)
        mix_ref[:, 0:D] = y1b
        mix_ref[:, D:2 * D] = y2b
        out = x_ref[...] + (_dot(y1b, wo_ref[0:D, :]) + _dot(y2b, wo_ref[D:2 * D, :]))
        err = out - t_ref[...]
        loss_ref[...] += jnp.sum(err * err) * (0.5 / D)
        dout = err * (1.0 / D)
        dout_ref[...] = dout
        doutb = dout.astype(BF16)
        doutb_ref[...] = doutb
        do1, dz1, dw1 = bwd(_dot_nt(doutb, wo_ref[0:D, :]), o1, z1, w1, sg1, sl1, r1, n1)
        do2, dz2, dw2 = bwd(_dot_nt(doutb, wo_ref[D:2 * D, :]), o2_, z2, w2, sg2, sl2, r2, n2)
        do_ref[...] = do1
        dy_ref[...] = do2
        dz_ref[:, 0:D] = dz1.astype(BF16)
        dz_ref[:, D:2 * D] = dz2.astype(BF16)
        dnw_ref[0:1, :] += dw1
        dnw_ref[1:2, :] += dw2

    row = lambda col: pl.BlockSpec((tm, D), lambda i: (i, col))
    full = lambda shape: pl.BlockSpec(shape, lambda i: (0,) * len(shape))
    wide = pl.BlockSpec((tm, 2 * D), lambda i: (i, 0))
    return pl.pallas_call(
        body,
        name="gate_out",
        grid=(T // tm,),
        in_specs=[row(0), row(0), row(0), row(3), row(0), row(4), full((1, D)), full((1, D)), full((2 * D, D))],
        out_specs=[row(0), row(0), wide, row(0), row(0), wide, full((8, D)), full((8, LANES))],
        out_shape=[
            jax.ShapeDtypeStruct((T, D), F32), jax.ShapeDtypeStruct((T, D), BF16),
            jax.ShapeDtypeStruct((T, 2 * D), BF16), jax.ShapeDtypeStruct((T, D), F32),
            jax.ShapeDtypeStruct((T, D), F32), jax.ShapeDtypeStruct((T, 2 * D), BF16),
            jax.ShapeDtypeStruct((8, D), F32), jax.ShapeDtypeStruct((8, LANES), F32),
        ],
        compiler_params=_params(("arbitrary",)),
    )(x2, tgt2, o2, proj2, y2, proj2, sbw, ssw, w_out_bf)


def _dhn(dproj, w_pad, x2, dout, norm_w):
    T, D = x2.shape
    P = w_pad.shape[1]
    tm = min(512, T)
    tk = 1024 if P % 1024 == 0 else 512
    nk = P // tk

    def body(dp_ref, w_ref, x_ref, dout_ref, nw_ref, gx_ref, dnw_ref, acc_s):
        k = pl.program_id(1)

        @pl.when((pl.program_id(0) == 0) & (k == 0))
        def _():
            dnw_ref[...] = jnp.zeros_like(dnw_ref)

        @pl.when(k == 0)
        def _():
            acc_s[...] = jnp.zeros_like(acc_s)

        acc_s[...] += _dot_nt(dp_ref[...], w_ref[...])

        @pl.when(k == nk - 1)
        def _():
            xv = x_ref[...]
            r = lax.rsqrt(jnp.mean(xv * xv, axis=-1, keepdims=True) + EPS)
            xh = xv * r
            dhn = acc_s[...]
            dxh = dhn * nw_ref[...]
            gx_ref[...] = dout_ref[...] + r * (dxh - xh * jnp.mean(dxh * xh, axis=-1, keepdims=True))
            dnw_ref[0:1, :] += _colsum(dhn * xh)

    return pl.pallas_call(
        body,
        name="dhn",
        grid=(T // tm, nk),
        in_specs=[
            pl.BlockSpec((tm, tk), lambda i, k: (i, k)),
            pl.BlockSpec((D, tk), lambda i, k: (0, k)),
            pl.BlockSpec((tm, D), lambda i, k: (i, 0)),
            pl.BlockSpec((tm, D), lambda i, k: (i, 0)),
            pl.BlockSpec((1, D), lambda i, k: (0, 0)),
        ],
        out_specs=[pl.BlockSpec((tm, D), lambda i, k: (i, 0)), pl.BlockSpec((8, D), lambda i, k: (0, 0))],
        out_shape=[jax.ShapeDtypeStruct((T, D), F32), jax.ShapeDtypeStruct((8, D), F32)],
        scratch_shapes=[pltpu.VMEM((tm, D), F32)],
        compiler_params=_params(("arbitrary", "arbitrary")),
    )(dproj, w_pad, x2, dout, norm_w)


def _matmul(a, b, name):
    M, K = a.shape
    N = b.shape[1]
    tm = min(1024, M)
    tn = 1024 if N % 1024 == 0 else (512 if N % 512 == 0 else N)
    tk = min(512, K)

    def body(a_ref, b_ref, o_ref):
        @pl.when(pl.program_id(2) == 0)
        def _():
            o_ref[...] = jnp.zeros_like(o_ref)

        o_ref[...] += _dot(a_ref[...], b_ref[...])

    return pl.pallas_call(
        body,
        name=name,
        grid=(M // tm, N // tn, K // tk),
        in_specs=[pl.BlockSpec((tm, tk), lambda i, j, k: (i, k)), pl.BlockSpec((tk, tn), lambda i, j, k: (k, j))],
        out_specs=pl.BlockSpec((tm, tn), lambda i, j, k: (i, j)),
        out_shape=jax.ShapeDtypeStruct((M, N), F32),
        compiler_params=_params(("parallel", "parallel", "arbitrary")),
    )(a, b)


def _adamw(w, g, m, v, name):
    R, C = w.shape
    tr = 256 if R % 256 == 0 else R
    c1 = 1.0 - ADAM_B1 ** ADAM_STEP
    c2 = 1.0 - ADAM_B2 ** ADAM_STEP

    def body(w_ref, g_ref, m_ref, v_ref, d_ref, nm_ref, nv_ref):
        gv = g_ref[...]
        m_new = ADAM_B1 * m_ref[...] + (1.0 - ADAM_B1) * gv
        v_new = ADAM_B2 * v_ref[...] + (1.0 - ADAM_B2) * (gv * gv)
        d_ref[...] = -ADAM_LR * ((m_new / c1) / (jnp.sqrt(v_new / c2) + ADAM_EPS) + ADAM_WD * w_ref[...])
        nm_ref[...] = m_new
        nv_ref[...] = v_new

    spec = pl.BlockSpec((tr, C), lambda i: (i, 0))
    return pl.pallas_call(
        body,
        name=name,
        grid=(R // tr,),
        in_specs=[spec] * 4,
        out_specs=[spec] * 3,
        out_shape=[jax.ShapeDtypeStruct((R, C), F32)] * 3,
        compiler_params=_params(("parallel",)),
    )(w, g, m, v)


def _add_core_half(a, recv, core, name):
    _, n, h, S = a.shape
    th = 256 if h % 256 == 0 else h

    def body(c_ref, a_ref, r_ref, o_ref):
        o_ref[...] = a_ref[...] + r_ref[...]

    return pl.pallas_call(
        body,
        name=name,
        grid_spec=pltpu.PrefetchScalarGridSpec(
            num_scalar_prefetch=1,
            grid=(n, h // th),
            in_specs=[
                pl.BlockSpec((None, None, th, S), lambda p, i, c: (c[0], p, i, 0)),
                pl.BlockSpec((None, th, S), lambda p, i, c: (p, i, 0)),
            ],
            out_specs=pl.BlockSpec((None, th, S), lambda p, i, c: (p, i, 0)),
        ),
        out_shape=jax.ShapeDtypeStruct((n, h, S), F32),
        compiler_params=_params(("parallel", "parallel")),
    )(core, a, recv)


def _add_chips(hsum, recv, chip, name):
    _, h, S = hsum.shape
    th = 256 if h % 256 == 0 else h

    def body(c_ref, a_ref, r_ref, o_ref):
        o_ref[...] = ((a_ref[...] + r_ref[0]) + r_ref[1]) + r_ref[2]

    return pl.pallas_call(
        body,
        name=name,
        grid_spec=pltpu.PrefetchScalarGridSpec(
            num_scalar_prefetch=1,
            grid=(h // th,),
            in_specs=[
                pl.BlockSpec((None, th, S), lambda i, c: (c[0], i, 0)),
                pl.BlockSpec((N_CHIPS - 1, th, S), lambda i, c: (0, i, 0)),
            ],
            out_specs=pl.BlockSpec((th, S), lambda i, c: (i, 0)),
        ),
        out_shape=jax.ShapeDtypeStruct((h, S), F32),
        compiler_params=_params(("parallel",)),
    )(chip, hsum, recv)


def _place():
    x, y, c = lax.axis_index("x"), lax.axis_index("y"), lax.axis_index("c")
    other_chips = [(1 - x, y), (x, 1 - y), (1 - x, 1 - y)]
    return x, y, c, other_chips


def _allgather_weights(w_in_bf, w_out_bf, conv_w):
    D, S = w_in_bf.shape
    R = w_out_bf.shape[0]
    n_ici, n_fwd = 3 * (N_CHIPS - 1), 2 * (N_CHIPS - 1)

    def body(win, wout, cw, gin, gout, gcw, send_sems, recv_sems, local_sems):
        x, y, c, chips = _place()
        me = 2 * x + y
        sibling = (x, y, 1 - c)
        hin, hout = D // 2, R // 2

        def halves(chip_idx):
            return (gin.at[chip_idx, pl.ds(c * hin, hin)], gout.at[chip_idx, pl.ds(c * hout, hout)])

        def rcopy(k, src, dst, to):
            return pltpu.make_async_remote_copy(src_ref=src, dst_ref=dst, send_sem=send_sems.at[k],
                                                recv_sem=recv_sems.at[k], device_id=to, device_id_type=MESH)

        local = [pltpu.make_async_copy(win, gin.at[me], local_sems.at[0]),
                 pltpu.make_async_copy(wout, gout.at[me], local_sems.at[1]),
                 pltpu.make_async_copy(cw, gcw.at[me], local_sems.at[2])]
        for cp in local:
            cp.start()
        my_in, my_out = halves(me)
        src_in, src_out = win.at[pl.ds(c * hin, hin)], wout.at[pl.ds(c * hout, hout)]
        sends = []
        for j, chip in enumerate(chips):
            to = (*chip, c)
            sends += [rcopy(3 * j, src_in, my_in, to), rcopy(3 * j + 1, src_out, my_out, to),
                      rcopy(3 * j + 2, cw, gcw.at[me], to)]
        for cp in sends:
            cp.start()
        passed = []
        for j, (px, py) in enumerate(chips):
            their_in, their_out = halves(2 * px + py)
            rcopy(3 * j, their_in, their_in, sibling).wait_recv()
            rcopy(3 * j + 1, their_out, their_out, sibling).wait_recv()
            rcopy(3 * j + 2, cw, gcw.at[2 * px + py], sibling).wait_recv()
            fw = [rcopy(n_ici + 2 * j, their_in, their_in, sibling), rcopy(n_ici + 2 * j + 1, their_out, their_out, sibling)]
            for cp in fw:
                cp.start()
            passed += fw
        for j, (px, py) in enumerate(chips):
            oin = gin.at[2 * px + py, pl.ds((1 - c) * hin, hin)]
            oout = gout.at[2 * px + py, pl.ds((1 - c) * hout, hout)]
            rcopy(n_ici + 2 * j, oin, oin, sibling).wait_recv()
            rcopy(n_ici + 2 * j + 1, oout, oout, sibling).wait_recv()
        for cp in sends + passed:
            cp.wait_send()
        for cp in local:
            cp.wait()

    return pl.pallas_call(
        body,
        name="allgather_weights",
        in_specs=[ANY, ANY, ANY],
        out_specs=[ANY, ANY, ANY],
        out_shape=[jax.ShapeDtypeStruct((N_CHIPS, D, S), BF16), jax.ShapeDtypeStruct((N_CHIPS, R, D), BF16),
                   jax.ShapeDtypeStruct((N_CHIPS,) + conv_w.shape, F32)],
        scratch_shapes=[pltpu.SemaphoreType.DMA((n_ici + n_fwd,)), pltpu.SemaphoreType.DMA((n_ici + n_fwd,)),
                        pltpu.SemaphoreType.DMA((3,))],
    )(w_in_bf, w_out_bf, conv_w)


def _allreduce_small(packed):
    R = packed.shape[0]
    n_dev = 2 * N_CHIPS

    def body(p_ref, o_ref, buf, send_sems, recv_sems):
        x, y, c, _ = _place()
        me = 4 * x + 2 * y + c
        buf[me] = p_ref[...]
        copies = []
        for k in range(1, n_dev):
            px = 1 - x if k & 4 else x
            py = 1 - y if k & 2 else y
            pc = 1 - c if k & 1 else c
            copies.append((pltpu.make_async_remote_copy(
                src_ref=buf.at[me], dst_ref=buf.at[me], send_sem=send_sems.at[k - 1], recv_sem=recv_sems.at[k - 1],
                device_id=(px, py, pc), device_id_type=MESH), 4 * px + 2 * py + pc, (px, py, pc)))
        for cp, _, _ in copies:
            cp.start()
        for k, (_, peer, to) in enumerate(copies):
            pltpu.make_async_remote_copy(
                src_ref=buf.at[peer], dst_ref=buf.at[peer], send_sem=send_sems.at[k], recv_sem=recv_sems.at[k],
                device_id=to, device_id_type=MESH).wait_recv()
        for cp, _, _ in copies:
            cp.wait_send()
        acc = buf[0]
        for d in range(1, n_dev):
            acc = acc + buf[d]
        o_ref[...] = acc

    vm = pl.BlockSpec(memory_space=pltpu.VMEM)
    return pl.pallas_call(
        body,
        name="allreduce_small",
        in_specs=[vm],
        out_specs=vm,
        out_shape=jax.ShapeDtypeStruct((R, LANES), F32),
        scratch_shapes=[pltpu.VMEM((n_dev, R, LANES), F32), pltpu.SemaphoreType.DMA((n_dev - 1,)),
                        pltpu.SemaphoreType.DMA((n_dev - 1,))],
    )(packed)


def _swap_core_halves(a_in, a_out):
    def body(ain, aout, rin, rout, send_sems, recv_sems):
        x, y, c, _ = _place()
        cps = [pltpu.make_async_remote_copy(src_ref=s.at[1 - c], dst_ref=d, send_sem=send_sems.at[k],
                                            recv_sem=recv_sems.at[k], device_id=(x, y, 1 - c), device_id_type=MESH)
               for k, (s, d) in enumerate(((ain, rin), (aout, rout)))]
        for cp in cps:
            cp.start()
        for cp in cps:
            cp.wait()

    return pl.pallas_call(
        body,
        name="reduce_core_swap",
        in_specs=[ANY, ANY],
        out_specs=[ANY, ANY],
        out_shape=[jax.ShapeDtypeStruct(a_in.shape[1:], F32), jax.ShapeDtypeStruct(a_out.shape[1:], F32)],
        scratch_shapes=[pltpu.SemaphoreType.DMA((2,)), pltpu.SemaphoreType.DMA((2,))],
    )(a_in, a_out)


def _scatter_to_chips(h_in, h_out):
    def body(hin, hout, rin, rout, send_sems, recv_sems):
        x, y, c, chips = _place()
        cps = []
        for j, (px, py) in enumerate(chips):
            for k, (s, d) in enumerate(((hin, rin), (hout, rout))):
                cps.append(pltpu.make_async_remote_copy(
                    src_ref=s.at[2 * px + py], dst_ref=d.at[j], send_sem=send_sems.at[2 * j + k],
                    recv_sem=recv_sems.at[2 * j + k], device_id=(px, py, c), device_id_type=MESH))
        for cp in cps:
            cp.start()
        for cp in cps:
            cp.wait()

    n = 2 * (N_CHIPS - 1)
    return pl.pallas_call(
        body,
        name="reduce_chip_scatter",
        in_specs=[ANY, ANY],
        out_specs=[ANY, ANY],
        out_shape=[jax.ShapeDtypeStruct((N_CHIPS - 1,) + h_in.shape[1:], F32),
                   jax.ShapeDtypeStruct((N_CHIPS - 1,) + h_out.shape[1:], F32)],
        scratch_shapes=[pltpu.SemaphoreType.DMA((n,)), pltpu.SemaphoreType.DMA((n,))],
    )(h_in, h_out)


def _join_core_halves(g_in, g_out):
    def body(gin, gout, fin, fout, send_sems, recv_sems, local_sems):
        x, y, c, _ = _place()
        local = [pltpu.make_async_copy(s, d.at[c], local_sems.at[k]) for k, (s, d) in enumerate(((gin, fin), (gout, fout)))]
        cps = [pltpu.make_async_remote_copy(src_ref=s, dst_ref=d.at[c], send_sem=send_sems.at[k],
                                            recv_sem=recv_sems.at[k], device_id=(x, y, 1 - c), device_id_type=MESH)
               for k, (s, d) in enumerate(((gin, fin), (gout, fout)))]
        for cp in local + cps:
            cp.start()
        for k, (s, d) in enumerate(((gin, fin), (gout, fout))):
            pltpu.make_async_remote_copy(src_ref=s, dst_ref=d.at[1 - c], send_sem=send_sems.at[k],
                                         recv_sem=recv_sems.at[k], device_id=(x, y, 1 - c),
                                         device_id_type=MESH).wait_recv()
        for cp in cps:
            cp.wait_send()
        for cp in local:
            cp.wait()

    return pl.pallas_call(
        body,
        name="reduce_core_join",
        in_specs=[ANY, ANY],
        out_specs=[ANY, ANY],
        out_shape=[jax.ShapeDtypeStruct((2,) + g_in.shape, F32), jax.ShapeDtypeStruct((2,) + g_out.shape, F32)],
        scratch_shapes=[pltpu.SemaphoreType.DMA((2,)), pltpu.SemaphoreType.DMA((2,)), pltpu.SemaphoreType.DMA((2,))],
    )(g_in, g_out)


def _pack(arrays):
    rows = []
    for a in arrays:
        flat = a.reshape(-1).astype(F32)
        n = -(-flat.shape[0] // LANES) * LANES
        rows.append(jnp.pad(flat, (0, n - flat.shape[0])).reshape(-1, LANES))
    out = jnp.concatenate(rows, axis=0)
    return jnp.pad(out, ((0, -out.shape[0] % 8), (0, 0)))


def _unpack(packed, shapes):
    out, r = [], 0
    for shp in shapes:
        n = math.prod(shp)
        nr = -(-n // LANES)
        out.append(packed[r:r + nr].reshape(-1)[:n].reshape(shp))
        r += nr
    return out


def _pad_lanes(a):
    return jnp.pad(a, ((0, 0), (0, LANES - a.shape[1])))


def kernel(x, norm_w, w_in, q_norm_w, k_norm_w, conv_w, conv_b, dt_bias, A_log, D_skip, sb_norm_w, ssd_norm_w, w_out, loss_target, m_norm_w, m_w_in, m_q_norm_w, m_k_norm_w, m_conv_w, m_conv_b, m_dt_bias, m_A_log, m_D_skip, m_sb_norm_w, m_ssd_norm_w, m_w_out, v_norm_w, v_w_in, v_q_norm_w, v_k_norm_w, v_conv_w, v_conv_b, v_dt_bias, v_A_log, v_D_skip, v_sb_norm_w, v_ssd_norm_w, v_w_out):
    Bl, L, D = x.shape
    T = Bl * L
    S = w_in.shape[2]
    R = w_out.shape[1]
    CW = conv_w.shape[2]
    n_in = N_CHIPS * S
    CD = D + 2 * SSD_GROUPS * SSD_STATE
    H = D // HEAD_DIM
    n_main = 6 * D + 512
    P = -(-(n_main + LANES) // 1024) * 1024
    assert n_in == n_main + H and CD == N_CHIPS * CW and 2 * D == N_CHIPS * R and CD == D + 512
    chip = (2 * lax.axis_index("x") + lax.axis_index("y")).astype(jnp.int32)
    core = lax.axis_index("c").astype(jnp.int32)

    g_in, g_out, g_cw = _allgather_weights(w_in[0].astype(BF16), w_out[0].astype(BF16), conv_w[0])
    w_pad = jnp.pad(g_in.transpose(1, 0, 2).reshape(D, n_in), ((0, 0), (0, P - n_in)))
    w_out_bf = g_out.reshape(2 * D, D)
    conv_full = g_cw.transpose(1, 0, 2).reshape(CONV_K, CD)
    cwx, cwb = conv_full[:, :D], conv_full[:, D:]
    cbx, cbb = conv_b[:, :D], conv_b[:, D:]
    dtb, alog, dsk = _pad_lanes(dt_bias), _pad_lanes(A_log), _pad_lanes(D_skip)
    qw2, kw2 = jnp.tile(q_norm_w, (1, 2)), jnp.tile(k_norm_w, (1, 2))

    x2 = x.reshape(T, D)
    proj, hn = _inproj(x2, norm_w, w_pad)
    proj3 = proj.reshape(Bl, L, P)
    o_sb = _attn_fwd(proj3, qw2, kw2, D)
    y_ssd, s_in = _ssd_fwd(proj3, cwx, cwb, cbx, cbb, dtb, alog, dsk, D)
    dout, dout_bf, mixed_bf, do_sb, dy_ssd, dz_bf, dnw_out, loss_blk = _gate_out(
        x2, loss_target.reshape(T, D), o_sb.reshape(T, D), proj, y_ssd.reshape(T, D), sb_norm_w, ssd_norm_w, w_out_bf)

    dq, dk, dv, dqkw = _attn_bwd(proj3, o_sb, do_sb.reshape(Bl, L, D), qw2, kw2, D)
    dxbc, ddt, dcwx, dcwb, dcbx, dcbb, misc = _ssd_bwd(
        proj3, s_in, dy_ssd.reshape(Bl, L, D), cwx, cwb, cbx, cbb, dtb, alog, dsk, D)
    dproj = jnp.concatenate(
        [dq.reshape(T, D), dk.reshape(T, D), dv.reshape(T, D), dz_bf, dxbc.reshape(T, CD), ddt.reshape(T, LANES),
         jnp.zeros((T, P - n_main - LANES), BF16)], axis=1)
    grad_x2, dnw_in = _dhn(dproj, w_pad, x2, dout, norm_w)
    gw_in = _matmul(hn.T, dproj, "grad_w_in")[:, :n_in]
    gw_out = _matmul(mixed_bf.T, dout_bf, "grad_w_out")

    small_shapes = [(1, D), (1, D), (1, D), (1, CD), (1, HEAD_DIM), (1, HEAD_DIM), (1, H), (1, H), (1, H)]
    g_small_local = [dnw_in[0:1], dnw_out[0:1], dnw_out[1:2], jnp.concatenate([dcbx, dcbb], axis=1),
                     dqkw[0:1, :HEAD_DIM] + dqkw[0:1, HEAD_DIM:], dqkw[1:2, :HEAD_DIM] + dqkw[1:2, HEAD_DIM:],
                     misc[0:1, :H], misc[1:2, :H], misc[2:3, :H]]
    packed = _pack(g_small_local + [jnp.concatenate([dcwx, dcwb], axis=1), loss_blk[0:1, 0:1]])
    red = _allreduce_small(packed)
    g_small = _unpack(red, small_shapes + [(CONV_K, CD), (1, 1)])
    g_conv_w = lax.dynamic_slice_in_dim(g_small[9], chip * CW, CW, axis=1)
    loss = g_small[10][0, 0]

    a_in = gw_in.reshape(2, D // 2, N_CHIPS, S).transpose(0, 2, 1, 3)
    a_out = gw_out.reshape(N_CHIPS, 2, R // 2, D).transpose(1, 0, 2, 3)
    r_in, r_out = _swap_core_halves(a_in, a_out)
    core1, chip1 = core.reshape(1), chip.reshape(1)
    h_in = _add_core_half(a_in, r_in, core1, "sum_cores_w_in")
    h_out = _add_core_half(a_out, r_out, core1, "sum_cores_w_out")
    s_in_, s_out_ = _scatter_to_chips(h_in, h_out)
    gh_in = _add_chips(h_in, s_in_, chip1, "sum_chips_w_in")
    gh_out = _add_chips(h_out, s_out_, chip1, "sum_chips_w_out")
    f_in, f_out = _join_core_halves(gh_in, gh_out)
    g_w_in = f_in.reshape(D, S)
    g_w_out = f_out.reshape(R, D)

    d_in, nm_in, nv_in = _adamw(w_in[0], g_w_in, m_w_in[0], v_w_in[0], "adamw_w_in")
    d_out, nm_out, nv_out = _adamw(w_out[0], g_w_out, m_w_out[0], v_w_out[0], "adamw_w_out")
    d_cw, nm_cw, nv_cw = _adamw(conv_w[0], g_conv_w, m_conv_w[0], v_conv_w[0], "adamw_conv_w")
    small_w = [norm_w, sb_norm_w, ssd_norm_w, conv_b, q_norm_w, k_norm_w, dt_bias, A_log, D_skip]
    small_m = [m_norm_w, m_sb_norm_w, m_ssd_norm_w, m_conv_b, m_q_norm_w, m_k_norm_w, m_dt_bias, m_A_log, m_D_skip]
    small_v = [v_norm_w, v_sb_norm_w, v_ssd_norm_w, v_conv_b, v_q_norm_w, v_k_norm_w, v_dt_bias, v_A_log, v_D_skip]
    d_s, nm_s, nv_s = _adamw(_pack(small_w), _pack(g_small[:9]), _pack(small_m), _pack(small_v), "adamw_small")
    d_s, nm_s, nv_s = (_unpack(t, small_shapes) for t in (d_s, nm_s, nv_s))

    def ordered(s, w_in_, conv_w_, w_out_):
        return [s[0], w_in_[None], s[4], s[5], conv_w_[None], s[3], s[6], s[7], s[8], s[1], s[2], w_out_[None]]

    return (loss, grad_x2.reshape(Bl, L, D),
            *ordered(g_small[:9], g_w_in, g_conv_w, g_w_out),
            *ordered(d_s, d_in, d_cw, d_out),
            *ordered(nm_s, nm_in, nm_cw, nm_out),
            *ordered(nv_s, nv_in, nv_cw, nv_out))
```

```python
import functools
import math

import jax
import jax.numpy as jnp
from jax import lax
from jax.experimental import pallas as pl
from jax.experimental.pallas import tpu as pltpu

F32 = jnp.float32
BF16 = jnp.bfloat16
EPS = 1e-6
HEAD_DIM = 64
PAIR = 2 * HEAD_DIM
LANES = 128
SSD_STATE = 128
SSD_GROUPS = 2
BLK = 128
FIRST_LEFT = 2
UNDERFLOW = -105.0
CONV_K = 4
HALO = 8
N_CHIPS = 4
ADAM_LR, ADAM_B1, ADAM_B2, ADAM_EPS, ADAM_WD, ADAM_STEP = 0.001, 0.9, 0.999, 1e-08, 0.01, 10
VMEM_LIMIT_V7X = 56 * 1024 * 1024
MESH = pl.DeviceIdType.MESH
ANY = pl.BlockSpec(memory_space=pl.ANY)
NT = (((1,), (1,)), ((), ()))


def _params(sem=None):
    kw = dict(vmem_limit_bytes=VMEM_LIMIT_V7X)
    if sem is not None:
        kw["dimension_semantics"] = sem
    return pltpu.CompilerParams(**kw)


def _dot(a, b):
    return jnp.dot(a, b, preferred_element_type=F32)


def _dot_nt(a, b):
    return lax.dot_general(a, b, NT, preferred_element_type=F32)


def _dot_split(m, x):
    hi = x.astype(BF16)
    lo = (x - hi.astype(F32)).astype(BF16)
    return _dot(m, hi) + _dot(m, lo)


def _iota(shape, dim):
    return lax.broadcasted_iota(jnp.int32, shape, dim)


def _rowsum(x):
    return jnp.sum(x, axis=1, keepdims=True)


def _colsum(x):
    return jnp.sum(x, axis=0, keepdims=True)


def _sigmoid(x):
    return 1.0 / (1.0 + jnp.exp(-x))


def _softplus(x):
    return jnp.maximum(x, 0.0) + jnp.log(1.0 + jnp.exp(-jnp.abs(x)))


def _inproj(x2, norm_w, w_pad):
    T, D = x2.shape
    P = w_pad.shape[1]
    tm = min(512, T)
    tn = 1024 if P % 1024 == 0 else 512

    def body(x_ref, nw_ref, w_ref, proj_ref, hn_ref):
        @pl.when(pl.program_id(1) == 0)
        def _():
            xv = x_ref[...]
            r = lax.rsqrt(jnp.mean(xv * xv, axis=-1, keepdims=True) + EPS)
            hn_ref[...] = (xv * r * nw_ref[...]).astype(BF16)

        proj_ref[...] = _dot(hn_ref[...], w_ref[...])

    return pl.pallas_call(
        body,
        name="inproj",
        grid=(T // tm, P // tn),
        in_specs=[
            pl.BlockSpec((tm, D), lambda i, j: (i, 0)),
            pl.BlockSpec((1, D), lambda i, j: (0, 0)),
            pl.BlockSpec((D, tn), lambda i, j: (0, j)),
        ],
        out_specs=[
            pl.BlockSpec((tm, tn), lambda i, j: (i, j)),
            pl.BlockSpec((tm, D), lambda i, j: (i, 0)),
        ],
        out_shape=[jax.ShapeDtypeStruct((T, P), F32), jax.ShapeDtypeStruct((T, D), BF16)],
        compiler_params=_params(("parallel", "arbitrary")),
    )(x2, norm_w, w_pad)


def _pair_rms(v, lo):
    sq = v * v
    s0 = _rowsum(jnp.where(lo, sq, 0.0))
    s1 = _rowsum(jnp.where(lo, 0.0, sq))
    return lax.rsqrt(jnp.where(lo, s0, s1) * (1.0 / HEAD_DIM) + EPS)


def _pair_mean(v, lo):
    s0 = _rowsum(jnp.where(lo, v, 0.0))
    s1 = _rowsum(jnp.where(lo, 0.0, v))
    return jnp.where(lo, s0, s1) * (1.0 / HEAD_DIM)


def _suffix_ones():
    ri = _iota((BLK, 2 * BLK), 0)
    ci = _iota((BLK, 2 * BLK), 1)
    return jnp.where((ci >= BLK) | (ri > ci), 1.0, 0.0).astype(BF16)


def _split_dots(xs, m):
    his = [x.astype(BF16) for x in xs]
    los = [(x - hi.astype(F32)).astype(BF16) for x, hi in zip(xs, his)]
    return [_dot(hi, m) + _dot(lo, m) for hi, lo in zip(his, los)]


def _sb_tiles(qms, km_s, c0s, rest, uo, masks):
    tiles = [(u, h) for u in range(len(c0s)) for h in range(2)]
    zs = [_dot_nt(qms[h], km_s[h, pl.ds(c0s[u], BLK), :]) for u, h in tiles]
    es = [jnp.exp(-jnp.abs(z)) for z in zs]
    las = [jnp.minimum(z, 0.0) - jnp.log(1.0 + e) for z, e in zip(zs, es)]
    lks = [a - z for a, z in zip(las, zs)]
    lks = [lk if masks[u] is None else jnp.where(masks[u], lk, 0.0) for (u, h), lk in zip(tiles, lks)]
    css = _split_dots(lks, uo)
    rest = list(rest)
    ws = []
    for (u, h), a, cs in zip(tiles, las, css):
        w = jnp.exp(a + rest[h] + cs[:, :BLK])
        ws.append(w if masks[u] is None else jnp.where(masks[u], w, 0.0))
        rest[h] = rest[h] + cs[:, BLK:]
    return tiles, zs, es, ws, rest


def _first_blocks(qi, diag):
    c0s = [pl.multiple_of(jnp.maximum(qi - u, 0) * BLK, BLK) for u in range(1 + FIRST_LEFT)]
    masks = [diag] + [jnp.broadcast_to(qi - u >= 0, (BLK, BLK)) for u in range(1, 1 + FIRST_LEFT)]
    return c0s, masks


def _attn_prep(src_ref, w_ref, dst_s, n_blocks, scale):
    lo = _iota((BLK, PAIR), 1) < HEAD_DIM

    def step(i, carry):
        r0 = pl.multiple_of(i * BLK, BLK)
        v = src_ref[0, pl.ds(r0, BLK), :]
        if w_ref is not None:
            v = v * _pair_rms(v, lo) * w_ref[...]
        if scale != 1.0:
            v = v * scale
        dst_s[0, pl.ds(r0, BLK), :] = jnp.where(lo, v, 0.0).astype(BF16)
        dst_s[1, pl.ds(r0, BLK), :] = jnp.where(lo, 0.0, v).astype(BF16)
        return carry

    lax.fori_loop(0, n_blocks, step, 0)


def _attn_fwd(proj3, qw2, kw2, D):
    Bl, L, _ = proj3.shape
    n_pair = D // PAIR
    nq = L // BLK
    scale = 1.0 / math.sqrt(HEAD_DIM)

    def body(q_ref, k_ref, v_ref, qw_ref, kw_ref, o_ref, qm_s, km_s, vm_s):
        uo = _suffix_ones()
        diag = _iota((BLK, BLK), 1) < _iota((BLK, BLK), 0)
        _attn_prep(q_ref, qw_ref, qm_s, nq, scale)
        _attn_prep(k_ref, kw_ref, km_s, nq, 1.0)
        _attn_prep(v_ref, None, vm_s, nq, 1.0)

        def qblock(qi, carry):
            r0 = pl.multiple_of(qi * BLK, BLK)
            qms = [qm_s[h, pl.ds(r0, BLK), :] for h in range(2)]

            def sweep(state, c0s, masks):
                kb, rc0, rc1, acc, _ = state
                tiles, _, _, ws, rc = _sb_tiles(qms, km_s, c0s, [rc0, rc1], uo, masks)
                wbs = [w.astype(BF16) for w in ws]
                for (u, h), wb in zip(tiles, wbs):
                    acc = acc + _dot(wb, vm_s[h, pl.ds(c0s[u], BLK), :])
                return kb - len(c0s), rc[0], rc[1], acc, jnp.maximum(jnp.max(rc[0]), jnp.max(rc[1]))

            def left(state, n):
                return [pl.multiple_of((state[0] - u) * BLK, BLK) for u in range(n)]

            zero_c = jnp.zeros((BLK, BLK), F32)
            state = sweep((qi, zero_c, zero_c, jnp.zeros((BLK, PAIR), F32), 0.0), *_first_blocks(qi, diag))
            state = lax.while_loop(lambda s: (s[0] >= 1) & (s[4] >= UNDERFLOW), lambda s: sweep(s, left(s, 2), [None] * 2), state)
            state = lax.while_loop(lambda s: (s[0] >= 0) & (s[4] >= UNDERFLOW), lambda s: sweep(s, left(s, 1), [None]), state)
            o_ref[0, pl.ds(r0, BLK), :] = state[3]
            return carry

        lax.fori_loop(0, nq, qblock, 0)

    blk = lambda off: pl.BlockSpec((1, L, PAIR), lambda b, p: (b, 0, off + p))
    wspec = pl.BlockSpec((1, PAIR), lambda b, p: (0, 0))
    return pl.pallas_call(
        body,
        name="sb_attn_fwd",
        grid=(Bl, n_pair),
        in_specs=[blk(0), blk(n_pair), blk(2 * n_pair), wspec, wspec],
        out_specs=pl.BlockSpec((1, L, PAIR), lambda b, p: (b, 0, p)),
        out_shape=jax.ShapeDtypeStruct((Bl, L, D), F32),
        scratch_shapes=[pltpu.VMEM((2, L, PAIR), BF16)] * 3,
        compiler_params=_params(("parallel", "parallel")),
    )(proj3, proj3, proj3, qw2, kw2)


def _attn_bwd(proj3, o3, do3, qw2, kw2, D):
    Bl, L, _ = proj3.shape
    n_pair = D // PAIR
    nq = L // BLK
    scale = 1.0 / math.sqrt(HEAD_DIM)

    def body(q_ref, k_ref, v_ref, o_ref, do_ref, qw_ref, kw_ref, dq_ref, dk_ref, dv_ref, dw_ref,
             qm_s, km_s, vm_s, dom_s, dq_s, dk_s, dv_s):
        uo = _suffix_ones()
        diag = _iota((BLK, BLK), 1) < _iota((BLK, BLK), 0)
        lo = _iota((BLK, PAIR), 1) < HEAD_DIM
        _attn_prep(q_ref, qw_ref, qm_s, nq, scale)
        _attn_prep(k_ref, kw_ref, km_s, nq, 1.0)
        _attn_prep(v_ref, None, vm_s, nq, 1.0)
        _attn_prep(do_ref, None, dom_s, nq, 1.0)

        @pl.when((pl.program_id(0) == 0) & (pl.program_id(1) == 0))
        def _():
            dw_ref[...] = jnp.zeros_like(dw_ref)

        def zero(i, carry):
            r0 = pl.multiple_of(i * BLK, BLK)
            dk_s[pl.ds(r0, BLK), :] = jnp.zeros((BLK, PAIR), F32)
            dv_s[pl.ds(r0, BLK), :] = jnp.zeros((BLK, PAIR), F32)
            return carry

        lax.fori_loop(0, nq, zero, 0)

        def tiles_bwd(qms, doms, deltas, c0s, rc, gc, dqa, masks):
            dws = [_dot_nt(doms[h], vm_s[h, pl.ds(c0s[u], BLK), :]) for u in range(len(c0s)) for h in range(2)]
            tiles, zs, es, ws, rc = _sb_tiles(qms, km_s, c0s, rc, uo, masks)
            wfs = [w.astype(BF16).astype(F32) for w in ws]
            gs = [wf * dw for wf, dw in zip(wfs, dws)]
            gss = _split_dots(gs, uo)
            gc = list(gc)
            dzs = []
            for (u, h), z, e, g, gsum in zip(tiles, zs, es, gs, gss):
                g_before = deltas[h] - (gc[h] + gsum[:, :BLK] + g)
                gc[h] = gc[h] + gsum[:, BLK:]
                r = 1.0 / (1.0 + e)
                er = e * r
                pos = z >= 0.0
                dz = g * jnp.where(pos, er, r) - g_before * jnp.where(pos, r, er)
                dzs.append(dz if masks[u] is None else jnp.where(masks[u], dz, 0.0))
            wts = [wf.T.astype(BF16) for wf in wfs]
            dzts = [dz.T.astype(BF16) for dz in dzs]
            dzbs = [dz.astype(BF16) for dz in dzs]
            for u, c0 in enumerate(c0s):
                dv_s[pl.ds(c0, BLK), :] += _dot(wts[2 * u], doms[0]) + _dot(wts[2 * u + 1], doms[1])
                dk_s[pl.ds(c0, BLK), :] += _dot(dzts[2 * u], qms[0]) + _dot(dzts[2 * u + 1], qms[1])
            for (u, h), dzb in zip(tiles, dzbs):
                dqa = dqa + _dot(dzb, km_s[h, pl.ds(c0s[u], BLK), :])
            return rc, gc, dqa

        def qblock(qi, carry):
            r0 = pl.multiple_of(qi * BLK, BLK)
            o_blk = o_ref[0, pl.ds(r0, BLK), :]
            qms = [qm_s[h, pl.ds(r0, BLK), :] for h in range(2)]
            doms = [dom_s[h, pl.ds(r0, BLK), :] for h in range(2)]
            deltas = [_rowsum(doms[h].astype(F32) * o_blk) for h in range(2)]

            def sweep(state, c0s, masks):
                kb, rc0, rc1, gc0, gc1, dqa, _ = state
                rc, gc, dqa = tiles_bwd(qms, doms, deltas, c0s, [rc0, rc1], [gc0, gc1], dqa, masks)
                return kb - len(c0s), rc[0], rc[1], gc[0], gc[1], dqa, jnp.maximum(jnp.max(rc[0]), jnp.max(rc[1]))

            def left(state, n):
                return [pl.multiple_of((state[0] - u) * BLK, BLK) for u in range(n)]

            zero_c = jnp.zeros((BLK, BLK), F32)
            state = sweep((qi, zero_c, zero_c, zero_c, zero_c, jnp.zeros((BLK, PAIR), F32), 0.0), *_first_blocks(qi, diag))
            state = lax.while_loop(lambda s: (s[0] >= 1) & (s[6] >= UNDERFLOW), lambda s: sweep(s, left(s, 2), [None] * 2), state)
            state = lax.while_loop(lambda s: (s[0] >= 0) & (s[6] >= UNDERFLOW), lambda s: sweep(s, left(s, 1), [None]), state)
            dq_s[pl.ds(r0, BLK), :] = state[5] * scale
            return carry

        lax.fori_loop(0, nq, qblock, 0)

        def finish(i, carry):
            r0 = pl.multiple_of(i * BLK, BLK)
            dwq, dwk = carry
            out = []
            for src_ref, w_ref, d_s in ((q_ref, qw_ref, dq_s), (k_ref, kw_ref, dk_s)):
                v = src_ref[0, pl.ds(r0, BLK), :]
                r = _pair_rms(v, lo)
                vh = v * r
                dy = d_s[pl.ds(r0, BLK), :]
                dvh = dy * w_ref[...]
                out.append((r * (dvh - vh * _pair_mean(dvh * vh, lo)), _colsum(dy * vh)))
            dq_ref[0, pl.ds(r0, BLK), :] = out[0][0].astype(BF16)
            dk_ref[0, pl.ds(r0, BLK), :] = out[1][0].astype(BF16)
            dv_ref[0, pl.ds(r0, BLK), :] = dv_s[pl.ds(r0, BLK), :].astype(BF16)
            return dwq + out[0][1], dwk + out[1][1]

        zrow = jnp.zeros((1, PAIR), F32)
        dwq, dwk = lax.fori_loop(0, nq, finish, (zrow, zrow))
        dw_ref[0:1, :] += dwq
        dw_ref[1:2, :] += dwk

    blk = lambda off: pl.BlockSpec((1, L, PAIR), lambda b, p: (b, 0, off + p))
    wspec = pl.BlockSpec((1, PAIR), lambda b, p: (0, 0))
    oblk = pl.BlockSpec((1, L, PAIR), lambda b, p: (b, 0, p))
    return pl.pallas_call(
        body,
        name="sb_attn_bwd",
        grid=(Bl, n_pair),
        in_specs=[blk(0), blk(n_pair), blk(2 * n_pair), oblk, oblk, wspec, wspec],
        out_specs=[oblk, oblk, oblk, pl.BlockSpec((8, PAIR), lambda b, p: (0, 0))],
        out_shape=[jax.ShapeDtypeStruct((Bl, L, D), BF16)] * 3 + [jax.ShapeDtypeStruct((8, PAIR), F32)],
        scratch_shapes=[pltpu.VMEM((2, L, PAIR), BF16)] * 4 + [pltpu.VMEM((L, PAIR), F32)] * 3,
        compiler_params=_params(("arbitrary", "arbitrary")),
    )(proj3, proj3, proj3, o3, do3, qw2, kw2)


def _conv_pre(ext_s, halo_ref, raw_ref, w_ref, b_ref, first):
    ext_s[0:HALO, :] = jnp.where(first, 0.0, halo_ref[0])
    ext_s[HALO:HALO + BLK, :] = raw_ref[0]
    pre = b_ref[...]
    for i in range(CONV_K):
        pre = pre + ext_s[pl.ds(HALO - (CONV_K - 1 - i), BLK), :] * w_ref[i:i + 1, :]
    return pre


def _lane_col(m, lane, h):
    return _rowsum(jnp.where(lane == h, m, 0.0))


def _half_sums(row, lo1):
    return _rowsum(jnp.where(lo1, row, 0.0)), _rowsum(jnp.where(lo1, 0.0, row))


def _ssd_specs(Bl, L, D, rev):
    nc = L // BLK
    rows_per = BLK // HALO
    cidx = (lambda c: nc - 1 - c) if rev else (lambda c: c)
    xoff = 5
    boff = (6 * D) // 512
    doff = (6 * D + 512) // LANES
    prev = lambda c: jnp.maximum(cidx(c) * rows_per - 1, 0)
    specs = [
        pl.BlockSpec((1, BLK, D), lambda b, c: (b, cidx(c), xoff)),
        pl.BlockSpec((1, BLK, 512), lambda b, c: (b, cidx(c), boff)),
        pl.BlockSpec((1, HALO, D), lambda b, c: (b, prev(c), xoff)),
        pl.BlockSpec((1, HALO, 512), lambda b, c: (b, prev(c), boff)),
        pl.BlockSpec((1, BLK, LANES), lambda b, c: (b, cidx(c), doff)),
    ]
    full = lambda shape: pl.BlockSpec(shape, lambda b, c: (0,) * len(shape))
    specs += [full((CONV_K, D)), full((CONV_K, 512)), full((1, D)), full((1, 512)),
              full((1, LANES)), full((1, LANES)), full((1, LANES))]
    return specs, cidx


def _ssd_common(dtr_ref, dtb_ref, alog_ref, acs_s, acsT_s):
    ltri = jnp.where(_iota((BLK, BLK), 1) <= _iota((BLK, BLK), 0), 1.0, 0.0).astype(BF16)
    dtv = _softplus(dtr_ref[0] + dtb_ref[...])
    a = -jnp.exp(alog_ref[...])
    acs = _dot_split(ltri, dtv * a)
    acs_s[...] = acs
    acsT_s[...] = acs.T
    return dtv, a, acs


def _pair_terms(pr, acs, dtv, acs_s, lane, lo, lane1, lo1):
    h0, h1 = 2 * pr, 2 * pr + 1
    c0, c1 = _lane_col(acs, lane, h0), _lane_col(acs, lane, h1)
    d0, d1 = _lane_col(dtv, lane, h0), _lane_col(dtv, lane, h1)
    lastv = acs_s[BLK - 1:BLK, :]
    l0, l1 = _lane_col(lastv, lane1, h0), _lane_col(lastv, lane1, h1)
    return dict(h=(h0, h1), c=(c0, c1), last=(l0, l1), acs_p=jnp.where(lo, c0, c1), dt_p=jnp.where(lo, d0, d1),
                last_p=jnp.where(lo1, l0, l1))


def _decay_tiles(cc, row, tri, want_t):
    lm = jnp.where(tri, jnp.exp(jnp.where(tri, cc - row, 0.0)), 0.0)
    if not want_t:
        return lm, None
    tri_t = _iota((BLK, BLK), 1) >= _iota((BLK, BLK), 0)
    return lm, jnp.where(tri_t, jnp.exp(jnp.where(tri_t, row - cc, 0.0)), 0.0)


def _ssd_fwd(proj3, cwx, cwb, cbx, cbb, dtb, alog, dsk, D):
    Bl, L, _ = proj3.shape
    nc = L // BLK
    n_pair = D // PAIR
    pairs_per_group = n_pair // SSD_GROUPS
    specs, _ = _ssd_specs(Bl, L, D, False)

    def body(xr_ref, bcr_ref, xh_ref, bch_ref, dtr_ref, cwx_ref, cwb_ref, cbx_ref, cbb_ref, dtb_ref, alog_ref,
             dsk_ref, y_ref, sin_ref, st_s, extx_s, extb_s, acs_s, acsT_s):
        first = pl.program_id(1) == 0

        @pl.when(first)
        def _():
            st_s[...] = jnp.zeros_like(st_s)

        lane, lane1 = _iota((BLK, LANES), 1), _iota((1, LANES), 1)
        lo, lo1 = lane < HEAD_DIM, lane1 < HEAD_DIM
        tri = _iota((BLK, BLK), 1) <= _iota((BLK, BLK), 0)
        pre = _conv_pre(extx_s, xh_ref, xr_ref, cwx_ref, cbx_ref, first)
        ux = pre * _sigmoid(pre)
        pre = _conv_pre(extb_s, bch_ref, bcr_ref, cwb_ref, cbb_ref, first)
        ub = pre * _sigmoid(pre)
        dtv, a, acs = _ssd_common(dtr_ref, dtb_ref, alog_ref, acs_s, acsT_s)
        for g in range(SSD_GROUPS):
            bg = ub[:, g * SSD_STATE:(g + 1) * SSD_STATE]
            cb_ = ub[:, (SSD_GROUPS + g) * SSD_STATE:(SSD_GROUPS + g + 1) * SSD_STATE].astype(BF16)
            cbm = _dot_nt(cb_, bg.astype(BF16))
            btb = bg.T.astype(BF16)
            for pr in range(g * pairs_per_group, (g + 1) * pairs_per_group):
                t = _pair_terms(pr, acs, dtv, acs_s, lane, lo, lane1, lo1)
                xs_p = ux[:, pr * PAIR:(pr + 1) * PAIR]
                x_p = xs_p * t["dt_p"]
                st = st_s[pr]
                sin_ref[0, 0, pr] = st
                y = _dot(cb_, st.astype(BF16)) * jnp.exp(t["acs_p"])
                for k in range(2):
                    row = acsT_s[t["h"][k]:t["h"][k] + 1, :]
                    lm, _ = _decay_tiles(t["c"][k], row, tri, False)
                    xm = jnp.where(lo if k == 0 else ~lo, x_p, 0.0).astype(BF16)
                    y = y + _dot((cbm * lm).astype(BF16), xm)
                d_p = jnp.where(lo1, _lane_col(dsk_ref[...], lane1, t["h"][0]), _lane_col(dsk_ref[...], lane1, t["h"][1]))
                y_ref[0, :, pr * PAIR:(pr + 1) * PAIR] = y + d_p * xs_p
                xd = (x_p * jnp.exp(t["last_p"] - t["acs_p"])).astype(BF16)
                st_s[pr] = st * jnp.exp(t["last_p"]) + _dot(btb, xd)

    return pl.pallas_call(
        body,
        name="ssd_fwd",
        grid=(Bl, nc),
        in_specs=specs,
        out_specs=[
            pl.BlockSpec((1, BLK, D), lambda b, c: (b, c, 0)),
            pl.BlockSpec((1, 1, n_pair, SSD_STATE, PAIR), lambda b, c: (b, c, 0, 0, 0)),
        ],
        out_shape=[jax.ShapeDtypeStruct((Bl, L, D), F32),
                   jax.ShapeDtypeStruct((Bl, nc, n_pair, SSD_STATE, PAIR), F32)],
        scratch_shapes=[pltpu.VMEM((n_pair, SSD_STATE, PAIR), F32), pltpu.VMEM((HALO + BLK, D), F32),
                        pltpu.VMEM((HALO + BLK, 512), F32), pltpu.VMEM((BLK, LANES), F32),
                        pltpu.VMEM((LANES, BLK), F32)],
        compiler_params=_params(("arbitrary", "arbitrary")),
    )(proj3, proj3, proj3, proj3, proj3, cwx, cwb, cbx, cbb, dtb, alog, dsk)


def _ssd_bwd(proj3, s_in, dy3, cwx, cwb, cbx, cbb, dtb, alog, dsk, D):
    Bl, L, _ = proj3.shape
    nc = L // BLK
    n_pair = D // PAIR
    n_heads = 2 * n_pair
    pairs_per_group = n_pair // SSD_GROUPS
    specs, cidx = _ssd_specs(Bl, L, D, True)
    specs = specs + [
        pl.BlockSpec((1, 1, n_pair, SSD_STATE, PAIR), lambda b, c: (b, cidx(c), 0, 0, 0)),
        pl.BlockSpec((1, BLK, D), lambda b, c: (b, cidx(c), 0)),
    ]

    def body(xr_ref, bcr_ref, xh_ref, bch_ref, dtr_ref, cwx_ref, cwb_ref, cbx_ref, cbb_ref, dtb_ref, alog_ref,
             dsk_ref, sin_ref, dy_ref, dxbc_ref, ddt_ref, dcwx_ref, dcwb_ref, dcbx_ref, dcbb_ref, misc_ref,
             dst_s, extx_s, extb_s, acs_s, acsT_s, dux_s, dub_s, e2x_s, e2b_s, nxx_s, nxb_s):
        step = pl.program_id(1)
        first = step == nc - 1
        last = step == 0

        @pl.when(last)
        def _():
            dst_s[...] = jnp.zeros_like(dst_s)
            nxx_s[...] = jnp.zeros_like(nxx_s)
            nxb_s[...] = jnp.zeros_like(nxb_s)

        @pl.when(last & (pl.program_id(0) == 0))
        def _():
            for r in (dcwx_ref, dcwb_ref, dcbx_ref, dcbb_ref, misc_ref):
                r[...] = jnp.zeros_like(r)

        lane, lane1 = _iota((BLK, LANES), 1), _iota((1, LANES), 1)
        lo, lo1 = lane < HEAD_DIM, lane1 < HEAD_DIM
        tri = _iota((BLK, BLK), 1) <= _iota((BLK, BLK), 0)
        prex = _conv_pre(extx_s, xh_ref, xr_ref, cwx_ref, cbx_ref, first)
        sgx = _sigmoid(prex)
        ux = prex * sgx
        preb = _conv_pre(extb_s, bch_ref, bcr_ref, cwb_ref, cbb_ref, first)
        sgb = _sigmoid(preb)
        ub = preb * sgb
        dtv, a, acs = _ssd_common(dtr_ref, dtb_ref, alog_ref, acs_s, acsT_s)
        dacs = jnp.zeros((BLK, LANES), F32)
        dlast = jnp.zeros((1, LANES), F32)
        ddt = jnp.zeros((BLK, LANES), F32)
        dd = jnp.zeros((1, LANES), F32)
        for g in range(SSD_GROUPS):
            bg = ub[:, g * SSD_STATE:(g + 1) * SSD_STATE]
            cg = ub[:, (SSD_GROUPS + g) * SSD_STATE:(SSD_GROUPS + g + 1) * SSD_STATE]
            bb, cb_ = bg.astype(BF16), cg.astype(BF16)
            cbm = _dot_nt(cb_, bb)
            cbt = _dot_nt(bb, cb_)
            ctb = cg.T.astype(BF16)
            dbg = jnp.zeros((BLK, SSD_STATE), F32)
            dcg = jnp.zeros((BLK, SSD_STATE), F32)
            for pr in range(g * pairs_per_group, (g + 1) * pairs_per_group):
                t = _pair_terms(pr, acs, dtv, acs_s, lane, lo, lane1, lo1)
                h0, h1 = t["h"]
                xs_p = ux[:, pr * PAIR:(pr + 1) * PAIR]
                dy_p = dy_ref[0, :, pr * PAIR:(pr + 1) * PAIR]
                x_p = xs_p * t["dt_p"]
                ea_p = jnp.exp(t["acs_p"])
                dte_p = jnp.exp(t["last_p"] - t["acs_p"])
                cd_p = jnp.exp(t["last_p"])
                st = sin_ref[0, 0, pr]
                dst = dst_s[pr]
                stb, dstb = st.astype(BF16), dst.astype(BF16)
                s0, s1 = _half_sums(_colsum(dy_p * xs_p), lo1)
                dd = dd + jnp.where(lane1 == h0, s0, 0.0) + jnp.where(lane1 == h1, s1, 0.0)
                d_p = jnp.where(lo1, _lane_col(dsk_ref[...], lane1, h0), _lane_col(dsk_ref[...], lane1, h1))
                dxs_p = d_p * dy_p
                dp = dy_p * ea_p
                dpb = dp.astype(BF16)
                yo = dp * _dot(cb_, stb)
                dcg = dcg + _dot_nt(dpb, stb)
                dst_off = _dot(ctb, dpb)
                dac = [_rowsum(jnp.where(lo, yo, 0.0)), _rowsum(jnp.where(lo, 0.0, yo))]
                s0, s1 = _half_sums(_colsum(dst * st), lo1)
                dl = [s0 * jnp.exp(t["last"][0]), s1 * jnp.exp(t["last"][1])]
                dxd = _dot(bb, dstb)
                dx_p = dxd * dte_p
                tt = dxd * x_p
                dbg = dbg + _dot_nt((x_p * dte_p).astype(BF16), dstb)
                for k, ddte in enumerate((_rowsum(jnp.where(lo, tt, 0.0)), _rowsum(jnp.where(lo, 0.0, tt)))):
                    ek = ddte * jnp.exp(t["last"][k] - t["c"][k])
                    dl[k] = dl[k] + _colsum(ek)
                    dac[k] = dac[k] - ek
                for k in range(2):
                    row = acsT_s[t["h"][k]:t["h"][k] + 1, :]
                    lm, lmt = _decay_tiles(t["c"][k], row, tri, True)
                    msk = lo if k == 0 else ~lo
                    xm = jnp.where(msk, x_p, 0.0).astype(BF16)
                    dym = jnp.where(msk, dy_p, 0.0).astype(BF16)
                    dm = _dot_nt(dym, xm)
                    dmt = _dot_nt(xm, dym)
                    mt = cbt * lmt
                    dx_p = dx_p + _dot(mt.astype(BF16), dym)
                    dac[k] = dac[k] + _rowsum(dm * (cbm * lm)) - _rowsum(dmt * mt)
                    dcg = dcg + _dot((dm * lm).astype(BF16), bb)
                    dbg = dbg + _dot((dmt * lmt).astype(BF16), cb_)
                dacs = dacs + jnp.where(lane == h0, dac[0], 0.0) + jnp.where(lane == h1, dac[1], 0.0)
                dlast = dlast + jnp.where(lane1 == h0, dl[0], 0.0) + jnp.where(lane1 == h1, dl[1], 0.0)
                dxs_p = dxs_p + dx_p * t["dt_p"]
                t3 = dx_p * xs_p
                ddt = ddt + jnp.where(lane == h0, _rowsum(jnp.where(lo, t3, 0.0)), 0.0) \
                    + jnp.where(lane == h1, _rowsum(jnp.where(lo, 0.0, t3)), 0.0)
                dux_s[:, pr * PAIR:(pr + 1) * PAIR] = dxs_p
                dst_s[pr] = dst * cd_p + dst_off
            dub_s[:, g * SSD_STATE:(g + 1) * SSD_STATE] = dbg
            dub_s[:, (SSD_GROUPS + g) * SSD_STATE:(SSD_GROUPS + g + 1) * SSD_STATE] = dcg
        dacs = dacs + jnp.where(_iota((BLK, LANES), 0) == BLK - 1, dlast, 0.0)
        utri = jnp.where(_iota((BLK, BLK), 1) >= _iota((BLK, BLK), 0), 1.0, 0.0).astype(BF16)
        dda = _dot_split(utri, dacs)
        ddt = ddt + dda * a
        ddtr = jnp.where(lane < n_heads, ddt * _sigmoid(dtr_ref[0] + dtb_ref[...]), 0.0)
        ddt_ref[0] = ddtr.astype(BF16)
        misc_ref[0:1, :] += _colsum(ddtr)
        misc_ref[1:2, :] += jnp.where(lane1 < n_heads, _colsum(dda * dtv) * a, 0.0)
        misc_ref[2:3, :] += dd
        for (du_s, pre, sg, ext_s, e2_s, nx_s, w_ref, dcw_ref, dcb_ref, c0, width) in (
                (dux_s, prex, sgx, extx_s, e2x_s, nxx_s, cwx_ref, dcwx_ref, dcbx_ref, 0, D),
                (dub_s, preb, sgb, extb_s, e2b_s, nxb_s, cwb_ref, dcwb_ref, dcbb_ref, D, 512)):
            dpre = du_s[...] * (sg * (1.0 + pre * (1.0 - sg)))
            dcb_ref[...] += _colsum(dpre)
            for i in range(CONV_K):
                dcw_ref[i:i + 1, :] += _colsum(dpre * ext_s[pl.ds(HALO - (CONV_K - 1 - i), BLK), :])
            e2_s[0:BLK, :] = dpre
            e2_s[BLK:BLK + HALO, :] = nx_s[...]
            dxr = jnp.zeros((BLK, width), F32)
            for i in range(CONV_K):
                dxr = dxr + e2_s[pl.ds(CONV_K - 1 - i, BLK), :] * w_ref[i:i + 1, :]
            dxbc_ref[0, :, c0:c0 + width] = dxr.astype(BF16)
            nx_s[...] = e2_s[0:HALO, :]

    full = lambda shape: pl.BlockSpec(shape, lambda b, c: (0,) * len(shape))
    return pl.pallas_call(
        body,
        name="ssd_bwd",
        grid=(Bl, nc),
        in_specs=specs,
        out_specs=[
            pl.BlockSpec((1, BLK, D + 512), lambda b, c: (b, cidx(c), 0)),
            pl.BlockSpec((1, BLK, LANES), lambda b, c: (b, cidx(c), 0)),
            full((CONV_K, D)), full((CONV_K, 512)), full((1, D)), full((1, 512)), full((8, LANES)),
        ],
        out_shape=[
            jax.ShapeDtypeStruct((Bl, L, D + 512), BF16), jax.ShapeDtypeStruct((Bl, L, LANES), BF16),
            jax.ShapeDtypeStruct((CONV_K, D), F32), jax.ShapeDtypeStruct((CONV_K, 512), F32),
            jax.ShapeDtypeStruct((1, D), F32), jax.ShapeDtypeStruct((1, 512), F32),
            jax.ShapeDtypeStruct((8, LANES), F32),
        ],
        scratch_shapes=[
            pltpu.VMEM((n_pair, SSD_STATE, PAIR), F32),
            pltpu.VMEM((HALO + BLK, D), F32), pltpu.VMEM((HALO + BLK, 512), F32),
            pltpu.VMEM((BLK, LANES), F32), pltpu.VMEM((LANES, BLK), F32),
            pltpu.VMEM((BLK, D), F32), pltpu.VMEM((BLK, 512), F32),
            pltpu.VMEM((BLK + HALO, D), F32), pltpu.VMEM((BLK + HALO, 512), F32),
            pltpu.VMEM((HALO, D), F32), pltpu.VMEM((HALO, 512), F32),
        ],
        compiler_params=_params(("arbitrary", "arbitrary")),
    )(proj3, proj3, proj3, proj3, proj3, cwx, cwb, cbx, cbb, dtb, alog, dsk, s_in, dy3)


def _gate_out(x2, tgt2, o2, proj2, y2, sbw, ssw, w_out_bf):
    T, D = x2.shape
    tm = min(256, T)

    def body(x_ref, t_ref, o_ref, zs_ref, y_ref, zy_ref, sbw_ref, ssw_ref, wo_ref,
             dout_ref, doutb_ref, mix_ref, do_ref, dy_ref, dz_ref, dnw_ref, loss_ref):
        @pl.when(pl.program_id(0) == 0)
        def _():
            dnw_ref[...] = jnp.zeros_like(dnw_ref)
            loss_ref[...] = jnp.zeros_like(loss_ref)

        def fwd(o, z, w):
            sg = _sigmoid(z)
            sl = z * sg
            g = o * sl
            r = lax.rsqrt(jnp.mean(g * g, axis=-1, keepdims=True) + EPS)
            n = g * r
            return sg, sl, r, n, (n * w).astype(BF16)

        def bwd(dy, o, z, w, sg, sl, r, n):
            dn = dy * w
            dg = r * (dn - n * jnp.mean(dn * n, axis=-1, keepdims=True))
            return dg * sl, dg * o * (sg * (1.0 + z * (1.0 - sg))), _colsum(dy * n)

        o1, z1, w1 = o_ref[...], zs_ref[...], sbw_ref[...]
        o2_, z2, w2 = y_ref[...], zy_ref[...], ssw_ref[...]
        sg1, sl1, r1, n1, y1b = fwd(o1, z1, w1)
        sg2, sl2, r2, n2, y2b = fwd(o2_, z2, w2)
        mix_ref[:, 0:D] = y1b
        mix_ref[:, D:2 * D] = y2b
        out = x_ref[...] + (_dot(y1b, wo_ref[0:D, :]) + _dot(y2b, wo_ref[D:2 * D, :]))
        err = out - t_ref[...]
        loss_ref[...] += jnp.sum(err * err) * (0.5 / D)
        dout = err * (1.0 / D)
        dout_ref[...] = dout
        doutb = dout.astype(BF16)
        doutb_ref[...] = doutb
        do1, dz1, dw1 = bwd(_dot_nt(doutb, wo_ref[0:D, :]), o1, z1, w1, sg1, sl1, r1, n1)
        do2, dz2, dw2 = bwd(_dot_nt(doutb, wo_ref[D:2 * D, :]), o2_, z2, w2, sg2, sl2, r2, n2)
        do_ref[...] = do1
        dy_ref[...] = do2
        dz_ref[:, 0:D] = dz1.astype(BF16)
        dz_ref[:, D:2 * D] = dz2.astype(BF16)
        dnw_ref[0:1, :] += dw1
        dnw_ref[1:2, :] += dw2

    row = lambda col: pl.BlockSpec((tm, D), lambda i: (i, col))
    full = lambda shape: pl.BlockSpec(shape, lambda i: (0,) * len(shape))
    wide = pl.BlockSpec((tm, 2 * D), lambda i: (i, 0))
    return pl.pallas_call(
        body,
        name="gate_out",
        grid=(T // tm,),
        in_specs=[row(0), row(0), row(0), row(3), row(0), row(4), full((1, D)), full((1, D)), full((2 * D, D))],
        out_specs=[row(0), row(0), wide, row(0), row(0), wide, full((8, D)), full((8, LANES))],
        out_shape=[
            jax.ShapeDtypeStruct((T, D), F32), jax.ShapeDtypeStruct((T, D), BF16),
            jax.ShapeDtypeStruct((T, 2 * D), BF16), jax.ShapeDtypeStruct((T, D), F32),
            jax.ShapeDtypeStruct((T, D), F32), jax.ShapeDtypeStruct((T, 2 * D), BF16),
            jax.ShapeDtypeStruct((8, D), F32), jax.ShapeDtypeStruct((8, LANES), F32),
        ],
        compiler_params=_params(("arbitrary",)),
    )(x2, tgt2, o2, proj2, y2, proj2, sbw, ssw, w_out_bf)


def _dhn(dproj, w_pad, x2, dout, norm_w):
    T, D = x2.shape
    P = w_pad.shape[1]
    tm = min(512, T)
    tk = 1024 if P % 1024 == 0 else 512
    nk = P // tk

    def body(dp_ref, w_ref, x_ref, dout_ref, nw_ref, gx_ref, dnw_ref, acc_s):
        k = pl.program_id(1)

        @pl.when((pl.program_id(0) == 0) & (k == 0))
        def _():
            dnw_ref[...] = jnp.zeros_like(dnw_ref)

        @pl.when(k == 0)
        def _():
            acc_s[...] = jnp.zeros_like(acc_s)

        acc_s[...] += _dot_nt(dp_ref[...], w_ref[...])

        @pl.when(k == nk - 1)
        def _():
            xv = x_ref[...]
            r = lax.rsqrt(jnp.mean(xv * xv, axis=-1, keepdims=True) + EPS)
            xh = xv * r
            dhn = acc_s[...]
            dxh = dhn * nw_ref[...]
            gx_ref[...] = dout_ref[...] + r * (dxh - xh * jnp.mean(dxh * xh, axis=-1, keepdims=True))
            dnw_ref[0:1, :] += _colsum(dhn * xh)

    return pl.pallas_call(
        body,
        name="dhn",
        grid=(T // tm, nk),
        in_specs=[
            pl.BlockSpec((tm, tk), lambda i, k: (i, k)),
            pl.BlockSpec((D, tk), lambda i, k: (0, k)),
            pl.BlockSpec((tm, D), lambda i, k: (i, 0)),
            pl.BlockSpec((tm, D), lambda i, k: (i, 0)),
            pl.BlockSpec((1, D), lambda i, k: (0, 0)),
        ],
        out_specs=[pl.BlockSpec((tm, D), lambda i, k: (i, 0)), pl.BlockSpec((8, D), lambda i, k: (0, 0))],
        out_shape=[jax.ShapeDtypeStruct((T, D), F32), jax.ShapeDtypeStruct((8, D), F32)],
        scratch_shapes=[pltpu.VMEM((tm, D), F32)],
        compiler_params=_params(("arbitrary", "arbitrary")),
    )(dproj, w_pad, x2, dout, norm_w)


def _matmul(a, b, name):
    M, K = a.shape
    N = b.shape[1]
    tm = min(1024, M)
    tn = 1024 if N % 1024 == 0 else (512 if N % 512 == 0 else N)
    tk = min(512, K)

    def body(a_ref, b_ref, o_ref):
        @pl.when(pl.program_id(2) == 0)
        def _():
            o_ref[...] = jnp.zeros_like(o_ref)

        o_ref[...] += _dot(a_ref[...], b_ref[...])

    return pl.pallas_call(
        body,
        name=name,
        grid=(M // tm, N // tn, K // tk),
        in_specs=[pl.BlockSpec((tm, tk), lambda i, j, k: (i, k)), pl.BlockSpec((tk, tn), lambda i, j, k: (k, j))],
        out_specs=pl.BlockSpec((tm, tn), lambda i, j, k: (i, j)),
        out_shape=jax.ShapeDtypeStruct((M, N), F32),
        compiler_params=_params(("parallel", "parallel", "arbitrary")),
    )(a, b)


def _adamw(w, g, m, v, name):
    R, C = w.shape
    tr = 256 if R % 256 == 0 else R
    c1 = 1.0 - ADAM_B1 ** ADAM_STEP
    c2 = 1.0 - ADAM_B2 ** ADAM_STEP

    def body(w_ref, g_ref, m_ref, v_ref, d_ref, nm_ref, nv_ref):
        gv = g_ref[...]
        m_new = ADAM_B1 * m_ref[...] + (1.0 - ADAM_B1) * gv
        v_new = ADAM_B2 * v_ref[...] + (1.0 - ADAM_B2) * (gv * gv)
        d_ref[...] = -ADAM_LR * ((m_new / c1) / (jnp.sqrt(v_new / c2) + ADAM_EPS) + ADAM_WD * w_ref[...])
        nm_ref[...] = m_new
        nv_ref[...] = v_new

    spec = pl.BlockSpec((tr, C), lambda i: (i, 0))
    return pl.pallas_call(
        body,
        name=name,
        grid=(R // tr,),
        in_specs=[spec] * 4,
        out_specs=[spec] * 3,
        out_shape=[jax.ShapeDtypeStruct((R, C), F32)] * 3,
        compiler_params=_params(("parallel",)),
    )(w, g, m, v)


def _add_core_half(a, recv, core, name):
    _, n, h, S = a.shape
    th = 256 if h % 256 == 0 else h

    def body(c_ref, a_ref, r_ref, o_ref):
        o_ref[...] = a_ref[...] + r_ref[...]

    return pl.pallas_call(
        body,
        name=name,
        grid_spec=pltpu.PrefetchScalarGridSpec(
            num_scalar_prefetch=1,
            grid=(n, h // th),
            in_specs=[
                pl.BlockSpec((None, None, th, S), lambda p, i, c: (c[0], p, i, 0)),
                pl.BlockSpec((None, th, S), lambda p, i, c: (p, i, 0)),
            ],
            out_specs=pl.BlockSpec((None, th, S), lambda p, i, c: (p, i, 0)),
        ),
        out_shape=jax.ShapeDtypeStruct((n, h, S), F32),
        compiler_params=_params(("parallel", "parallel")),
    )(core, a, recv)


def _add_chips(hsum, recv, chip, name):
    _, h, S = hsum.shape
    th = 256 if h % 256 == 0 else h

    def body(c_ref, a_ref, r_ref, o_ref):
        o_ref[...] = ((a_ref[...] + r_ref[0]) + r_ref[1]) + r_ref[2]

    return pl.pallas_call(
        body,
        name=name,
        grid_spec=pltpu.PrefetchScalarGridSpec(
            num_scalar_prefetch=1,
            grid=(h // th,),
            in_specs=[
                pl.BlockSpec((None, th, S), lambda i, c: (c[0], i, 0)),
                pl.BlockSpec((N_CHIPS - 1, th, S), lambda i, c: (0, i, 0)),
            ],
            out_specs=pl.BlockSpec((th, S), lambda i, c: (i, 0)),
        ),
        out_shape=jax.ShapeDtypeStruct((h, S), F32),
        compiler_params=_params(("parallel",)),
    )(chip, hsum, recv)


def _place():
    x, y, c = lax.axis_index("x"), lax.axis_index("y"), lax.axis_index("c")
    other_chips = [(1 - x, y), (x, 1 - y), (1 - x, 1 - y)]
    return x, y, c, other_chips


def _allgather_weights(w_in_bf, w_out_bf, conv_w):
    D, S = w_in_bf.shape
    R = w_out_bf.shape[0]
    n_ici, n_fwd = 3 * (N_CHIPS - 1), 2 * (N_CHIPS - 1)

    def body(win, wout, cw, gin, gout, gcw, send_sems, recv_sems, local_sems):
        x, y, c, chips = _place()
        me = 2 * x + y
        sibling = (x, y, 1 - c)
        hin, hout = D // 2, R // 2

        def halves(chip_idx):
            return (gin.at[chip_idx, pl.ds(c * hin, hin)], gout.at[chip_idx, pl.ds(c * hout, hout)])

        def rcopy(k, src, dst, to):
            return pltpu.make_async_remote_copy(src_ref=src, dst_ref=dst, send_sem=send_sems.at[k],
                                                recv_sem=recv_sems.at[k], device_id=to, device_id_type=MESH)

        local = [pltpu.make_async_copy(win, gin.at[me], local_sems.at[0]),
                 pltpu.make_async_copy(wout, gout.at[me], local_sems.at[1]),
                 pltpu.make_async_copy(cw, gcw.at[me], local_sems.at[2])]
        for cp in local:
            cp.start()
        my_in, my_out = halves(me)
        src_in, src_out = win.at[pl.ds(c * hin, hin)], wout.at[pl.ds(c * hout, hout)]
        sends = []
        for j, chip in enumerate(chips):
            to = (*chip, c)
            sends += [rcopy(3 * j, src_in, my_in, to), rcopy(3 * j + 1, src_out, my_out, to),
                      rcopy(3 * j + 2, cw, gcw.at[me], to)]
        for cp in sends:
            cp.start()
        passed = []
        for j, (px, py) in enumerate(chips):
            their_in, their_out = halves(2 * px + py)
            rcopy(3 * j, their_in, their_in, sibling).wait_recv()
            rcopy(3 * j + 1, their_out, their_out, sibling).wait_recv()
            rcopy(3 * j + 2, cw, gcw.at[2 * px + py], sibling).wait_recv()
            fw = [rcopy(n_ici + 2 * j, their_in, their_in, sibling), rcopy(n_ici + 2 * j + 1, their_out, their_out, sibling)]
            for cp in fw:
                cp.start()
            passed += fw
        for j, (px, py) in enumerate(chips):
            oin = gin.at[2 * px + py, pl.ds((1 - c) * hin, hin)]
            oout = gout.at[2 * px + py, pl.ds((1 - c) * hout, hout)]
            rcopy(n_ici + 2 * j, oin, oin, sibling).wait_recv()
            rcopy(n_ici + 2 * j + 1, oout, oout, sibling).wait_recv()
        for cp in sends + passed:
            cp.wait_send()
        for cp in local:
            cp.wait()

    return pl.pallas_call(
        body,
        name="allgather_weights",
        in_specs=[ANY, ANY, ANY],
        out_specs=[ANY, ANY, ANY],
        out_shape=[jax.ShapeDtypeStruct((N_CHIPS, D, S), BF16), jax.ShapeDtypeStruct((N_CHIPS, R, D), BF16),
                   jax.ShapeDtypeStruct((N_CHIPS,) + conv_w.shape, F32)],
        scratch_shapes=[pltpu.SemaphoreType.DMA((n_ici + n_fwd,)), pltpu.SemaphoreType.DMA((n_ici + n_fwd,)),
                        pltpu.SemaphoreType.DMA((3,))],
    )(w_in_bf, w_out_bf, conv_w)


def _allreduce_small(packed):
    R = packed.shape[0]
    n_dev = 2 * N_CHIPS

    def body(p_ref, o_ref, buf, send_sems, recv_sems):
        x, y, c, _ = _place()
        me = 4 * x + 2 * y + c
        buf[me] = p_ref[...]
        copies = []
        for k in range(1, n_dev):
            px = 1 - x if k & 4 else x
            py = 1 - y if k & 2 else y
            pc = 1 - c if k & 1 else c
            copies.append((pltpu.make_async_remote_copy(
                src_ref=buf.at[me], dst_ref=buf.at[me], send_sem=send_sems.at[k - 1], recv_sem=recv_sems.at[k - 1],
                device_id=(px, py, pc), device_id_type=MESH), 4 * px + 2 * py + pc, (px, py, pc)))
        for cp, _, _ in copies:
            cp.start()
        for k, (_, peer, to) in enumerate(copies):
            pltpu.make_async_remote_copy(
                src_ref=buf.at[peer], dst_ref=buf.at[peer], send_sem=send_sems.at[k], recv_sem=recv_sems.at[k],
                device_id=to, device_id_type=MESH).wait_recv()
        for cp, _, _ in copies:
            cp.wait_send()
        acc = buf[0]
        for d in range(1, n_dev):
            acc = acc + buf[d]
        o_ref[...] = acc

    vm = pl.BlockSpec(memory_space=pltpu.VMEM)
    return pl.pallas_call(
        body,
        name="allreduce_small",
        in_specs=[vm],
        out_specs=vm,
        out_shape=jax.ShapeDtypeStruct((R, LANES), F32),
        scratch_shapes=[pltpu.VMEM((n_dev, R, LANES), F32), pltpu.SemaphoreType.DMA((n_dev - 1,)),
                        pltpu.SemaphoreType.DMA((n_dev - 1,))],
    )(packed)


def _swap_core_halves(a_in, a_out):
    def body(ain, aout, rin, rout, send_sems, recv_sems):
        x, y, c, _ = _place()
        cps = [pltpu.make_async_remote_copy(src_ref=s.at[1 - c], dst_ref=d, send_sem=send_sems.at[k],
                                            recv_sem=recv_sems.at[k], device_id=(x, y, 1 - c), device_id_type=MESH)
               for k, (s, d) in enumerate(((ain, rin), (aout, rout)))]
        for cp in cps:
            cp.start()
        for cp in cps:
            cp.wait()

    return pl.pallas_call(
        body,
        name="reduce_core_swap",
        in_specs=[ANY, ANY],
        out_specs=[ANY, ANY],
        out_shape=[jax.ShapeDtypeStruct(a_in.shape[1:], F32), jax.ShapeDtypeStruct(a_out.shape[1:], F32)],
        scratch_shapes=[pltpu.SemaphoreType.DMA((2,)), pltpu.SemaphoreType.DMA((2,))],
    )(a_in, a_out)


def _scatter_to_chips(h_in, h_out):
    def body(hin, hout, rin, rout, send_sems, recv_sems):
        x, y, c, chips = _place()
        cps = []
        for j, (px, py) in enumerate(chips):
            for k, (s, d) in enumerate(((hin, rin), (hout, rout))):
                cps.append(pltpu.make_async_remote_copy(
                    src_ref=s.at[2 * px + py], dst_ref=d.at[j], send_sem=send_sems.at[2 * j + k],
                    recv_sem=recv_sems.at[2 * j + k], device_id=(px, py, c), device_id_type=MESH))
        for cp in cps:
            cp.start()
        for cp in cps:
            cp.wait()

    n = 2 * (N_CHIPS - 1)
    return pl.pallas_call(
        body,
        name="reduce_chip_scatter",
        in_specs=[ANY, ANY],
        out_specs=[ANY, ANY],
        out_shape=[jax.ShapeDtypeStruct((N_CHIPS - 1,) + h_in.shape[1:], F32),
                   jax.ShapeDtypeStruct((N_CHIPS - 1,) + h_out.shape[1:], F32)],
        scratch_shapes=[pltpu.SemaphoreType.DMA((n,)), pltpu.SemaphoreType.DMA((n,))],
    )(h_in, h_out)


def _join_core_halves(g_in, g_out):
    def body(gin, gout, fin, fout, send_sems, recv_sems, local_sems):
        x, y, c, _ = _place()
        local = [pltpu.make_async_copy(s, d.at[c], local_sems.at[k]) for k, (s, d) in enumerate(((gin, fin), (gout, fout)))]
        cps = [pltpu.make_async_remote_copy(src_ref=s, dst_ref=d.at[c], send_sem=send_sems.at[k],
                                            recv_sem=recv_sems.at[k], device_id=(x, y, 1 - c), device_id_type=MESH)
               for k, (s, d) in enumerate(((gin, fin), (gout, fout)))]
        for cp in local + cps:
            cp.start()
        for k, (s, d) in enumerate(((gin, fin), (gout, fout))):
            pltpu.make_async_remote_copy(src_ref=s, dst_ref=d.at[1 - c], send_sem=send_sems.at[k],
                                         recv_sem=recv_sems.at[k], device_id=(x, y, 1 - c),
                                         device_id_type=MESH).wait_recv()
        for cp in cps:
            cp.wait_send()
        for cp in local:
            cp.wait()

    return pl.pallas_call(
        body,
        name="reduce_core_join",
        in_specs=[ANY, ANY],
        out_specs=[ANY, ANY],
        out_shape=[jax.ShapeDtypeStruct((2,) + g_in.shape, F32), jax.ShapeDtypeStruct((2,) + g_out.shape, F32)],
        scratch_shapes=[pltpu.SemaphoreType.DMA((2,)), pltpu.SemaphoreType.DMA((2,)), pltpu.SemaphoreType.DMA((2,))],
    )(g_in, g_out)


def _pack(arrays):
    rows = []
    for a in arrays:
        flat = a.reshape(-1).astype(F32)
        n = -(-flat.shape[0] // LANES) * LANES
        rows.append(jnp.pad(flat, (0, n - flat.shape[0])).reshape(-1, LANES))
    out = jnp.concatenate(rows, axis=0)
    return jnp.pad(out, ((0, -out.shape[0] % 8), (0, 0)))


def _unpack(packed, shapes):
    out, r = [], 0
    for shp in shapes:
        n = math.prod(shp)
        nr = -(-n // LANES)
        out.append(packed[r:r + nr].reshape(-1)[:n].reshape(shp))
        r += nr
    return out


def _pad_lanes(a):
    return jnp.pad(a, ((0, 0), (0, LANES - a.shape[1])))


def kernel(x, norm_w, w_in, q_norm_w, k_norm_w, conv_w, conv_b, dt_bias, A_log, D_skip, sb_norm_w, ssd_norm_w, w_out, loss_target, m_norm_w, m_w_in, m_q_norm_w, m_k_norm_w, m_conv_w, m_conv_b, m_dt_bias, m_A_log, m_D_skip, m_sb_norm_w, m_ssd_norm_w, m_w_out, v_norm_w, v_w_in, v_q_norm_w, v_k_norm_w, v_conv_w, v_conv_b, v_dt_bias, v_A_log, v_D_skip, v_sb_norm_w, v_ssd_norm_w, v_w_out):
    Bl, L, D = x.shape
    T = Bl * L
    S = w_in.shape[2]
    R = w_out.shape[1]
    CW = conv_w.shape[2]
    n_in = N_CHIPS * S
    CD = D + 2 * SSD_GROUPS * SSD_STATE
    H = D // HEAD_DIM
    n_main = 6 * D + 512
    P = -(-(n_main + LANES) // 1024) * 1024
    assert n_in == n_main + H and CD == N_CHIPS * CW and 2 * D == N_CHIPS * R and CD == D + 512
    chip = (2 * lax.axis_index("x") + lax.axis_index("y")).astype(jnp.int32)
    core = lax.axis_index("c").astype(jnp.int32)

    g_in, g_out, g_cw = _allgather_weights(w_in[0].astype(BF16), w_out[0].astype(BF16), conv_w[0])
    w_pad = jnp.pad(g_in.transpose(1, 0, 2).reshape(D, n_in), ((0, 0), (0, P - n_in)))
    w_out_bf = g_out.reshape(2 * D, D)
    conv_full = g_cw.transpose(1, 0, 2).reshape(CONV_K, CD)
    cwx, cwb = conv_full[:, :D], conv_full[:, D:]
    cbx, cbb = conv_b[:, :D], conv_b[:, D:]
    dtb, alog, dsk = _pad_lanes(dt_bias), _pad_lanes(A_log), _pad_lanes(D_skip)
    qw2, kw2 = jnp.tile(q_norm_w, (1, 2)), jnp.tile(k_norm_w, (1, 2))

    x2 = x.reshape(T, D)
    proj, hn = _inproj(x2, norm_w, w_pad)
    proj3 = proj.reshape(Bl, L, P)
    o_sb = _attn_fwd(proj3, qw2, kw2, D)
    y_ssd, s_in = _ssd_fwd(proj3, cwx, cwb, cbx, cbb, dtb, alog, dsk, D)
    dout, dout_bf, mixed_bf, do_sb, dy_ssd, dz_bf, dnw_out, loss_blk = _gate_out(
        x2, loss_target.reshape(T, D), o_sb.reshape(T, D), proj, y_ssd.reshape(T, D), sb_norm_w, ssd_norm_w, w_out_bf)

    dq, dk, dv, dqkw = _attn_bwd(proj3, o_sb, do_sb.reshape(Bl, L, D), qw2, kw2, D)
    dxbc, ddt, dcwx, dcwb, dcbx, dcbb, misc = _ssd_bwd(
        proj3, s_in, dy_ssd.reshape(Bl, L, D), cwx, cwb, cbx, cbb, dtb, alog, dsk, D)
    dproj = jnp.concatenate(
        [dq.reshape(T, D), dk.reshape(T, D), dv.reshape(T, D), dz_bf, dxbc.reshape(T, CD), ddt.reshape(T, LANES),
         jnp.zeros((T, P - n_main - LANES), BF16)], axis=1)
    grad_x2, dnw_in = _dhn(dproj, w_pad, x2, dout, norm_w)
    gw_in = _matmul(hn.T, dproj, "grad_w_in")[:, :n_in]
    gw_out = _matmul(mixed_bf.T, dout_bf, "grad_w_out")

    small_shapes = [(1, D), (1, D), (1, D), (1, CD), (1, HEAD_DIM), (1, HEAD_DIM), (1, H), (1, H), (1, H)]
    g_small_local = [dnw_in[0:1], dnw_out[0:1], dnw_out[1:2], jnp.concatenate([dcbx, dcbb], axis=1),
                     dqkw[0:1, :HEAD_DIM] + dqkw[0:1, HEAD_DIM:], dqkw[1:2, :HEAD_DIM] + dqkw[1:2, HEAD_DIM:],
                     misc[0:1, :H], misc[1:2, :H], misc[2:3, :H]]
    packed = _pack(g_small_local + [jnp.concatenate([dcwx, dcwb], axis=1), loss_blk[0:1, 0:1]])
    red = _allreduce_small(packed)
    g_small = _unpack(red, small_shapes + [(CONV_K, CD), (1, 1)])
    g_conv_w = lax.dynamic_slice_in_dim(g_small[9], chip * CW, CW, axis=1)
    loss = g_small[10][0, 0]

    a_in = gw_in.reshape(2, D // 2, N_CHIPS, S).transpose(0, 2, 1, 3)
    a_out = gw_out.reshape(N_CHIPS, 2, R // 2, D).transpose(1, 0, 2, 3)
    r_in, r_out = _swap_core_halves(a_in, a_out)
    core1, chip1 = core.reshape(1), chip.reshape(1)
    h_in = _add_core_half(a_in, r_in, core1, "sum_cores_w_in")
    h_out = _add_core_half(a_out, r_out, core1, "sum_cores_w_out")
    s_in_, s_out_ = _scatter_to_chips(h_in, h_out)
    gh_in = _add_chips(h_in, s_in_, chip1, "sum_chips_w_in")
    gh_out = _add_chips(h_out, s_out_, chip1, "sum_chips_w_out")
    f_in, f_out = _join_core_halves(gh_in, gh_out)
    g_w_in = f_in.reshape(D, S)
    g_w_out = f_out.reshape(R, D)

    d_in, nm_in, nv_in = _adamw(w_in[0], g_w_in, m_w_in[0], v_w_in[0], "adamw_w_in")
    d_out, nm_out, nv_out = _adamw(w_out[0], g_w_out, m_w_out[0], v_w_out[0], "adamw_w_out")
    d_cw, nm_cw, nv_cw = _adamw(conv_w[0], g_conv_w, m_conv_w[0], v_conv_w[0], "adamw_conv_w")
    small_w = [norm_w, sb_norm_w, ssd_norm_w, conv_b, q_norm_w, k_norm_w, dt_bias, A_log, D_skip]
    small_m = [m_norm_w, m_sb_norm_w, m_ssd_norm_w, m_conv_b, m_q_norm_w, m_k_norm_w, m_dt_bias, m_A_log, m_D_skip]
    small_v = [v_norm_w, v_sb_norm_w, v_ssd_norm_w, v_conv_b, v_q_norm_w, v_k_norm_w, v_dt_bias, v_A_log, v_D_skip]
    d_s, nm_s, nv_s = _adamw(_pack(small_w), _pack(g_small[:9]), _pack(small_m), _pack(small_v), "adamw_small")
    d_s, nm_s, nv_s = (_unpack(t, small_shapes) for t in (d_s, nm_s, nv_s))

    def ordered(s, w_in_, conv_w_, w_out_):
        return [s[0], w_in_[None], s[4], s[5], conv_w_[None], s[3], s[6], s[7], s[8], s[1], s[2], w_out_[None]]

    return (loss, grad_x2.reshape(Bl, L, D),
            *ordered(g_small[:9], g_w_in, g_conv_w, g_w_out),
            *ordered(d_s, d_in, d_cw, d_out),
            *ordered(nm_s, nm_in, nm_cw, nm_out),
            *ordered(nv_s, nv_in, nv_cw, nv_out))
```

```python
import functools
import math

import jax
import jax.numpy as jnp
from jax import lax
from jax.experimental import pallas as pl
from jax.experimental.pallas import tpu as pltpu

F32 = jnp.float32
BF16 = jnp.bfloat16
EPS = 1e-6
HEAD_DIM = 64
PAIR = 2 * HEAD_DIM
LANES = 128
SSD_STATE = 128
SSD_GROUPS = 2
BLK = 128
FIRST_LEFT = 2
UNDERFLOW = -105.0
CONV_K = 4
HALO = 8
N_CHIPS = 4
ADAM_LR, ADAM_B1, ADAM_B2, ADAM_EPS, ADAM_WD, ADAM_STEP = 0.001, 0.9, 0.999, 1e-08, 0.01, 10
VMEM_LIMIT_V7X = 56 * 1024 * 1024
MESH = pl.DeviceIdType.MESH
ANY = pl.BlockSpec(memory_space=pl.ANY)
NT = (((1,), (1,)), ((), ()))


def _params(sem=None):
    kw = dict(vmem_limit_bytes=VMEM_LIMIT_V7X)
    if sem is not None:
        kw["dimension_semantics"] = sem
    return pltpu.CompilerParams(**kw)


def _dot(a, b):
    return jnp.dot(a, b, preferred_element_type=F32)


def _dot_nt(a, b):
    return lax.dot_general(a, b, NT, preferred_element_type=F32)


def _dot_split(m, x):
    hi = x.astype(BF16)
    lo = (x - hi.astype(F32)).astype(BF16)
    return _dot(m, hi) + _dot(m, lo)


def _iota(shape, dim):
    return lax.broadcasted_iota(jnp.int32, shape, dim)


def _rowsum(x):
    return jnp.sum(x, axis=1, keepdims=True)


def _colsum(x):
    return jnp.sum(x, axis=0, keepdims=True)


def _sigmoid(x):
    return 1.0 / (1.0 + jnp.exp(-x))


def _softplus(x):
    return jnp.maximum(x, 0.0) + jnp.log(1.0 + jnp.exp(-jnp.abs(x)))


def _inproj(x2, norm_w, w_pad):
    T, D = x2.shape
    P = w_pad.shape[1]
    tm = min(512, T)
    tn = 1024 if P % 1024 == 0 else 512

    def body(x_ref, nw_ref, w_ref, proj_ref, hnt_ref, hn_s):
        @pl.when(pl.program_id(1) == 0)
        def _():
            xv = x_ref[...]
            r = lax.rsqrt(jnp.mean(xv * xv, axis=-1, keepdims=True) + EPS)
            hn = xv * r * nw_ref[...]
            hn_s[...] = hn.astype(BF16)
            hnt_ref[...] = hn.T.astype(BF16)

        proj_ref[...] = _dot(hn_s[...], w_ref[...])

    return pl.pallas_call(
        body,
        name="inproj",
        grid=(T // tm, P // tn),
        in_specs=[
            pl.BlockSpec((tm, D), lambda i, j: (i, 0)),
            pl.BlockSpec((1, D), lambda i, j: (0, 0)),
            pl.BlockSpec((D, tn), lambda i, j: (0, j)),
        ],
        out_specs=[
            pl.BlockSpec((tm, tn), lambda i, j: (i, j)),
            pl.BlockSpec((D, tm), lambda i, j: (0, i)),
        ],
        out_shape=[jax.ShapeDtypeStruct((T, P), F32), jax.ShapeDtypeStruct((D, T), BF16)],
        scratch_shapes=[pltpu.VMEM((tm, D), BF16)],
        compiler_params=_params(("parallel", "arbitrary")),
    )(x2, norm_w, w_pad)


def _pair_ones():
    ri = (_iota((PAIR, PAIR), 0) >= HEAD_DIM).astype(jnp.int32)
    ci = (_iota((PAIR, PAIR), 1) >= HEAD_DIM).astype(jnp.int32)
    return jnp.where(ri == ci, 1.0, 0.0).astype(BF16)


def _pair_rms(v, ones2):
    return lax.rsqrt(_split_dots([v * v], ones2)[0] * (1.0 / HEAD_DIM) + EPS)


def _pair_mean(v, ones2):
    return _split_dots([v], ones2)[0] * (1.0 / HEAD_DIM)


def _suffix_ones():
    ri = _iota((BLK, 2 * BLK), 0)
    ci = _iota((BLK, 2 * BLK), 1)
    return jnp.where((ci >= BLK) | (ri > ci), 1.0, 0.0).astype(BF16)


def _split_dots(xs, m):
    his = [x.astype(BF16) for x in xs]
    los = [(x - hi.astype(F32)).astype(BF16) for x, hi in zip(xs, his)]
    return [_dot(hi, m) + _dot(lo, m) for hi, lo in zip(his, los)]


def _sb_tiles(qms, km_s, c0s, rest, uo, masks):
    tiles = [(u, h) for u in range(len(c0s)) for h in range(2)]
    zs = [_dot_nt(qms[h], km_s[h, pl.ds(c0s[u], BLK), :]) for u, h in tiles]
    es = [jnp.exp(-jnp.abs(z)) for z in zs]
    las = [jnp.minimum(z, 0.0) - jnp.log(1.0 + e) for z, e in zip(zs, es)]
    lks = [a - z for a, z in zip(las, zs)]
    lks = [lk if masks[u] is None else jnp.where(masks[u], lk, 0.0) for (u, h), lk in zip(tiles, lks)]
    css = _split_dots(lks, uo)
    rest = list(rest)
    ws = []
    for (u, h), a, cs in zip(tiles, las, css):
        w = jnp.exp(a + rest[h] + cs[:, :BLK])
        ws.append(w if masks[u] is None else jnp.where(masks[u], w, 0.0))
        rest[h] = rest[h] + cs[:, BLK:]
    return tiles, zs, es, ws, rest


def _first_blocks(qi, diag):
    c0s = [pl.multiple_of(jnp.maximum(qi - u, 0) * BLK, BLK) for u in range(1 + FIRST_LEFT)]
    masks = [diag] + [jnp.broadcast_to(qi - u >= 0, (BLK, BLK)) for u in range(1, 1 + FIRST_LEFT)]
    return c0s, masks


def _attn_prep(src_ref, w_ref, dst_s, n_blocks, scale):
    lo = _iota((BLK, PAIR), 1) < HEAD_DIM
    ones2 = _pair_ones()

    def step(i, carry):
        r0 = pl.multiple_of(i * BLK, BLK)
        v = src_ref[0, pl.ds(r0, BLK), :]
        if w_ref is not None:
            v = v * _pair_rms(v, ones2) * w_ref[...]
        if scale != 1.0:
            v = v * scale
        dst_s[0, pl.ds(r0, BLK), :] = jnp.where(lo, v, 0.0).astype(BF16)
        dst_s[1, pl.ds(r0, BLK), :] = jnp.where(lo, 0.0, v).astype(BF16)
        return carry

    lax.fori_loop(0, n_blocks, step, 0)


def _attn_fwd(proj3, qw2, kw2, D):
    Bl, L, _ = proj3.shape
    n_pair = D // PAIR
    nq = L // BLK
    scale = 1.0 / math.sqrt(HEAD_DIM)

    def body(q_ref, k_ref, v_ref, qw_ref, kw_ref, o_ref, qm_s, km_s, vm_s):
        uo = _suffix_ones()
        diag = _iota((BLK, BLK), 1) < _iota((BLK, BLK), 0)
        _attn_prep(q_ref, qw_ref, qm_s, nq, scale)
        _attn_prep(k_ref, kw_ref, km_s, nq, 1.0)
        _attn_prep(v_ref, None, vm_s, nq, 1.0)

        def qblock(qi, carry):
            r0 = pl.multiple_of(qi * BLK, BLK)
            qms = [qm_s[h, pl.ds(r0, BLK), :] for h in range(2)]

            def sweep(state, c0s, masks):
                kb, rc0, rc1, acc, _ = state
                tiles, _, _, ws, rc = _sb_tiles(qms, km_s, c0s, [rc0, rc1], uo, masks)
                wbs = [w.astype(BF16) for w in ws]
                for (u, h), wb in zip(tiles, wbs):
                    acc = acc + _dot(wb, vm_s[h, pl.ds(c0s[u], BLK), :])
                return kb - len(c0s), rc[0], rc[1], acc, jnp.maximum(jnp.max(rc[0]), jnp.max(rc[1]))

            def left(state, n):
                return [pl.multiple_of((state[0] - u) * BLK, BLK) for u in range(n)]

            zero_c = jnp.zeros((BLK, BLK), F32)
            state = sweep((qi, zero_c, zero_c, jnp.zeros((BLK, PAIR), F32), 0.0), *_first_blocks(qi, diag))
            state = lax.while_loop(lambda s: (s[0] >= 1) & (s[4] >= UNDERFLOW), lambda s: sweep(s, left(s, 2), [None] * 2), state)
            state = lax.while_loop(lambda s: (s[0] >= 0) & (s[4] >= UNDERFLOW), lambda s: sweep(s, left(s, 1), [None]), state)
            o_ref[0, pl.ds(r0, BLK), :] = state[3]
            return carry

        lax.fori_loop(0, nq, qblock, 0)

    blk = lambda off: pl.BlockSpec((1, L, PAIR), lambda b, p: (b, 0, off + p))
    wspec = pl.BlockSpec((1, PAIR), lambda b, p: (0, 0))
    return pl.pallas_call(
        body,
        name="sb_attn_fwd",
        grid=(Bl, n_pair),
        in_specs=[blk(0), blk(n_pair), blk(2 * n_pair), wspec, wspec],
        out_specs=pl.BlockSpec((1, L, PAIR), lambda b, p: (b, 0, p)),
        out_shape=jax.ShapeDtypeStruct((Bl, L, D), F32),
        scratch_shapes=[pltpu.VMEM((2, L, PAIR), BF16)] * 3,
        compiler_params=_params(("parallel", "parallel")),
    )(proj3, proj3, proj3, qw2, kw2)


def _attn_bwd(proj3, o3, do3, qw2, kw2, D):
    Bl, L, _ = proj3.shape
    n_pair = D // PAIR
    nq = L // BLK
    scale = 1.0 / math.sqrt(HEAD_DIM)

    def body(q_ref, k_ref, v_ref, o_ref, do_ref, qw_ref, kw_ref, dq_ref, dk_ref, dv_ref, dw_ref,
             qm_s, km_s, vm_s, dom_s, dq_s, dk_s, dv_s):
        uo = _suffix_ones()
        diag = _iota((BLK, BLK), 1) < _iota((BLK, BLK), 0)
        ones2 = _pair_ones()
        _attn_prep(q_ref, qw_ref, qm_s, nq, scale)
        _attn_prep(k_ref, kw_ref, km_s, nq, 1.0)
        _attn_prep(v_ref, None, vm_s, nq, 1.0)
        _attn_prep(do_ref, None, dom_s, nq, 1.0)

        @pl.when((pl.program_id(0) == 0) & (pl.program_id(1) == 0))
        def _():
            dw_ref[...] = jnp.zeros_like(dw_ref)

        def zero(i, carry):
            r0 = pl.multiple_of(i * BLK, BLK)
            dk_s[pl.ds(r0, BLK), :] = jnp.zeros((BLK, PAIR), F32)
            dv_s[pl.ds(r0, BLK), :] = jnp.zeros((BLK, PAIR), F32)
            return carry

        lax.fori_loop(0, nq, zero, 0)

        def tiles_bwd(qms, doms, deltas, c0s, rc, gc, dqa, masks):
            dws = [_dot_nt(doms[h], vm_s[h, pl.ds(c0s[u], BLK), :]) for u in range(len(c0s)) for h in range(2)]
            tiles, zs, es, ws, rc = _sb_tiles(qms, km_s, c0s, rc, uo, masks)
            wfs = [w.astype(BF16).astype(F32) for w in ws]
            gs = [wf * dw for wf, dw in zip(wfs, dws)]
            gss = _split_dots(gs, uo)
            gc = list(gc)
            dzs = []
            for (u, h), z, e, g, gsum in zip(tiles, zs, es, gs, gss):
                g_before = deltas[h] - (gc[h] + gsum[:, :BLK] + g)
                gc[h] = gc[h] + gsum[:, BLK:]
                r = 1.0 / (1.0 + e)
                er = e * r
                pos = z >= 0.0
                dz = g * jnp.where(pos, er, r) - g_before * jnp.where(pos, r, er)
                dzs.append(dz if masks[u] is None else jnp.where(masks[u], dz, 0.0))
            wts = [wf.T.astype(BF16) for wf in wfs]
            dzts = [dz.T.astype(BF16) for dz in dzs]
            dzbs = [dz.astype(BF16) for dz in dzs]
            for u, c0 in enumerate(c0s):
                dv_s[pl.ds(c0, BLK), :] += _dot(wts[2 * u], doms[0]) + _dot(wts[2 * u + 1], doms[1])
                dk_s[pl.ds(c0, BLK), :] += _dot(dzts[2 * u], qms[0]) + _dot(dzts[2 * u + 1], qms[1])
            for (u, h), dzb in zip(tiles, dzbs):
                dqa = dqa + _dot(dzb, km_s[h, pl.ds(c0s[u], BLK), :])
            return rc, gc, dqa

        def qblock(qi, carry):
            r0 = pl.multiple_of(qi * BLK, BLK)
            o_blk = o_ref[0, pl.ds(r0, BLK), :]
            qms = [qm_s[h, pl.ds(r0, BLK), :] for h in range(2)]
            doms = [dom_s[h, pl.ds(r0, BLK), :] for h in range(2)]
            deltas = [_rowsum(doms[h].astype(F32) * o_blk) for h in range(2)]

            def sweep(state, c0s, masks):
                kb, rc0, rc1, gc0, gc1, dqa, _ = state
                rc, gc, dqa = tiles_bwd(qms, doms, deltas, c0s, [rc0, rc1], [gc0, gc1], dqa, masks)
                return kb - len(c0s), rc[0], rc[1], gc[0], gc[1], dqa, jnp.maximum(jnp.max(rc[0]), jnp.max(rc[1]))

            def left(state, n):
                return [pl.multiple_of((state[0] - u) * BLK, BLK) for u in range(n)]

            zero_c = jnp.zeros((BLK, BLK), F32)
            state = sweep((qi, zero_c, zero_c, zero_c, zero_c, jnp.zeros((BLK, PAIR), F32), 0.0), *_first_blocks(qi, diag))
            state = lax.while_loop(lambda s: (s[0] >= 1) & (s[6] >= UNDERFLOW), lambda s: sweep(s, left(s, 2), [None] * 2), state)
            state = lax.while_loop(lambda s: (s[0] >= 0) & (s[6] >= UNDERFLOW), lambda s: sweep(s, left(s, 1), [None]), state)
            dq_s[pl.ds(r0, BLK), :] = state[5] * scale
            return carry

        lax.fori_loop(0, nq, qblock, 0)

        def finish(i, carry):
            r0 = pl.multiple_of(i * BLK, BLK)
            dwq, dwk = carry
            out = []
            for src_ref, w_ref, d_s in ((q_ref, qw_ref, dq_s), (k_ref, kw_ref, dk_s)):
                v = src_ref[0, pl.ds(r0, BLK), :]
                r = _pair_rms(v, ones2)
                vh = v * r
                dy = d_s[pl.ds(r0, BLK), :]
                dvh = dy * w_ref[...]
                out.append((r * (dvh - vh * _pair_mean(dvh * vh, ones2)), _colsum(dy * vh)))
            dq_ref[0, pl.ds(r0, BLK), :] = out[0][0].astype(BF16)
            dk_ref[0, pl.ds(r0, BLK), :] = out[1][0].astype(BF16)
            dv_ref[0, pl.ds(r0, BLK), :] = dv_s[pl.ds(r0, BLK), :].astype(BF16)
            return dwq + out[0][1], dwk + out[1][1]

        zrow = jnp.zeros((1, PAIR), F32)
        dwq, dwk = lax.fori_loop(0, nq, finish, (zrow, zrow))
        dw_ref[0:1, :] += dwq
        dw_ref[1:2, :] += dwk

    blk = lambda off: pl.BlockSpec((1, L, PAIR), lambda b, p: (b, 0, off + p))
    wspec = pl.BlockSpec((1, PAIR), lambda b, p: (0, 0))
    oblk = pl.BlockSpec((1, L, PAIR), lambda b, p: (b, 0, p))
    return pl.pallas_call(
        body,
        name="sb_attn_bwd",
        grid=(Bl, n_pair),
        in_specs=[blk(0), blk(n_pair), blk(2 * n_pair), oblk, oblk, wspec, wspec],
        out_specs=[oblk, oblk, oblk, pl.BlockSpec((8, PAIR), lambda b, p: (0, 0))],
        out_shape=[jax.ShapeDtypeStruct((Bl, L, D), BF16)] * 3 + [jax.ShapeDtypeStruct((8, PAIR), F32)],
        scratch_shapes=[pltpu.VMEM((2, L, PAIR), BF16)] * 4 + [pltpu.VMEM((L, PAIR), F32)] * 3,
        compiler_params=_params(("arbitrary", "arbitrary")),
    )(proj3, proj3, proj3, o3, do3, qw2, kw2)


def _conv_pre(ext_s, halo_ref, raw_ref, w_ref, b_ref, first):
    ext_s[0:HALO, :] = jnp.where(first, 0.0, halo_ref[0])
    ext_s[HALO:HALO + BLK, :] = raw_ref[0]
    pre = b_ref[...]
    for i in range(CONV_K):
        pre = pre + ext_s[pl.ds(HALO - (CONV_K - 1 - i), BLK), :] * w_ref[i:i + 1, :]
    return pre


def _lane_col(m, lane, h):
    return _rowsum(jnp.where(lane == h, m, 0.0))


def _half_sums(row, lo1):
    return _rowsum(jnp.where(lo1, row, 0.0)), _rowsum(jnp.where(lo1, 0.0, row))


def _ssd_specs(Bl, L, D, rev):
    nc = L // BLK
    rows_per = BLK // HALO
    cidx = (lambda c: nc - 1 - c) if rev else (lambda c: c)
    xoff = 5
    boff = (6 * D) // 512
    doff = (6 * D + 512) // LANES
    prev = lambda c: jnp.maximum(cidx(c) * rows_per - 1, 0)
    specs = [
        pl.BlockSpec((1, BLK, D), lambda b, c: (b, cidx(c), xoff)),
        pl.BlockSpec((1, BLK, 512), lambda b, c: (b, cidx(c), boff)),
        pl.BlockSpec((1, HALO, D), lambda b, c: (b, prev(c), xoff)),
        pl.BlockSpec((1, HALO, 512), lambda b, c: (b, prev(c), boff)),
        pl.BlockSpec((1, BLK, LANES), lambda b, c: (b, cidx(c), doff)),
    ]
    full = lambda shape: pl.BlockSpec(shape, lambda b, c: (0,) * len(shape))
    specs += [full((CONV_K, D)), full((CONV_K, 512)), full((1, D)), full((1, 512)),
              full((1, LANES)), full((1, LANES)), full((1, LANES))]
    return specs, cidx


def _ssd_common(dtr_ref, dtb_ref, alog_ref, acs_s, acsT_s):
    ltri = jnp.where(_iota((BLK, BLK), 1) <= _iota((BLK, BLK), 0), 1.0, 0.0).astype(BF16)
    dtv = _softplus(dtr_ref[0] + dtb_ref[...])
    a = -jnp.exp(alog_ref[...])
    acs = _dot_split(ltri, dtv * a)
    acs_s[...] = acs
    acsT_s[...] = acs.T
    return dtv, a, acs


def _pair_terms(pr, acs, dtv, acs_s, lane, lo, lane1, lo1):
    h0, h1 = 2 * pr, 2 * pr + 1
    c0, c1 = _lane_col(acs, lane, h0), _lane_col(acs, lane, h1)
    d0, d1 = _lane_col(dtv, lane, h0), _lane_col(dtv, lane, h1)
    lastv = acs_s[BLK - 1:BLK, :]
    l0, l1 = _lane_col(lastv, lane1, h0), _lane_col(lastv, lane1, h1)
    return dict(h=(h0, h1), c=(c0, c1), last=(l0, l1), acs_p=jnp.where(lo, c0, c1), dt_p=jnp.where(lo, d0, d1),
                last_p=jnp.where(lo1, l0, l1))


def _decay_tiles(cc, row, tri, want_t):
    lm = jnp.where(tri, jnp.exp(jnp.where(tri, cc - row, 0.0)), 0.0)
    if not want_t:
        return lm, None
    tri_t = _iota((BLK, BLK), 1) >= _iota((BLK, BLK), 0)
    return lm, jnp.where(tri_t, jnp.exp(jnp.where(tri_t, row - cc, 0.0)), 0.0)


def _ssd_fwd(proj3, cwx, cwb, cbx, cbb, dtb, alog, dsk, D):
    Bl, L, _ = proj3.shape
    nc = L // BLK
    n_pair = D // PAIR
    pairs_per_group = n_pair // SSD_GROUPS
    specs, _ = _ssd_specs(Bl, L, D, False)

    def body(xr_ref, bcr_ref, xh_ref, bch_ref, dtr_ref, cwx_ref, cwb_ref, cbx_ref, cbb_ref, dtb_ref, alog_ref,
             dsk_ref, y_ref, sin_ref, st_s, extx_s, extb_s, acs_s, acsT_s):
        first = pl.program_id(1) == 0

        @pl.when(first)
        def _():
            st_s[...] = jnp.zeros_like(st_s)

        lane, lane1 = _iota((BLK, LANES), 1), _iota((1, LANES), 1)
        lo, lo1 = lane < HEAD_DIM, lane1 < HEAD_DIM
        tri = _iota((BLK, BLK), 1) <= _iota((BLK, BLK), 0)
        pre = _conv_pre(extx_s, xh_ref, xr_ref, cwx_ref, cbx_ref, first)
        ux = pre * _sigmoid(pre)
        pre = _conv_pre(extb_s, bch_ref, bcr_ref, cwb_ref, cbb_ref, first)
        ub = pre * _sigmoid(pre)
        dtv, a, acs = _ssd_common(dtr_ref, dtb_ref, alog_ref, acs_s, acsT_s)
        for g in range(SSD_GROUPS):
            bg = ub[:, g * SSD_STATE:(g + 1) * SSD_STATE]
            cb_ = ub[:, (SSD_GROUPS + g) * SSD_STATE:(SSD_GROUPS + g + 1) * SSD_STATE].astype(BF16)
            cbm = _dot_nt(cb_, bg.astype(BF16))
            btb = bg.T.astype(BF16)
            for pr in range(g * pairs_per_group, (g + 1) * pairs_per_group):
                t = _pair_terms(pr, acs, dtv, acs_s, lane, lo, lane1, lo1)
                xs_p = ux[:, pr * PAIR:(pr + 1) * PAIR]
                x_p = xs_p * t["dt_p"]
                st = st_s[pr]
                sin_ref[0, 0, pr] = st
                y = _dot(cb_, st.astype(BF16)) * jnp.exp(t["acs_p"])
                for k in range(2):
                    row = acsT_s[t["h"][k]:t["h"][k] + 1, :]
                    lm, _ = _decay_tiles(t["c"][k], row, tri, False)
                    xm = jnp.where(lo if k == 0 else ~lo, x_p, 0.0).astype(BF16)
                    y = y + _dot((cbm * lm).astype(BF16), xm)
                d_p = jnp.where(lo1, _lane_col(dsk_ref[...], lane1, t["h"][0]), _lane_col(dsk_ref[...], lane1, t["h"][1]))
                y_ref[0, :, pr * PAIR:(pr + 1) * PAIR] = y + d_p * xs_p
                xd = (x_p * jnp.exp(t["last_p"] - t["acs_p"])).astype(BF16)
                st_s[pr] = st * jnp.exp(t["last_p"]) + _dot(btb, xd)

    return pl.pallas_call(
        body,
        name="ssd_fwd",
        grid=(Bl, nc),
        in_specs=specs,
        out_specs=[
            pl.BlockSpec((1, BLK, D), lambda b, c: (b, c, 0)),
            pl.BlockSpec((1, 1, n_pair, SSD_STATE, PAIR), lambda b, c: (b, c, 0, 0, 0)),
        ],
        out_shape=[jax.ShapeDtypeStruct((Bl, L, D), F32),
                   jax.ShapeDtypeStruct((Bl, nc, n_pair, SSD_STATE, PAIR), F32)],
        scratch_shapes=[pltpu.VMEM((n_pair, SSD_STATE, PAIR), F32), pltpu.VMEM((HALO + BLK, D), F32),
                        pltpu.VMEM((HALO + BLK, 512), F32), pltpu.VMEM((BLK, LANES), F32),
                        pltpu.VMEM((LANES, BLK), F32)],
        compiler_params=_params(("arbitrary", "arbitrary")),
    )(proj3, proj3, proj3, proj3, proj3, cwx, cwb, cbx, cbb, dtb, alog, dsk)


def _ssd_bwd(proj3, s_in, dy3, cwx, cwb, cbx, cbb, dtb, alog, dsk, D):
    Bl, L, _ = proj3.shape
    nc = L // BLK
    n_pair = D // PAIR
    n_heads = 2 * n_pair
    pairs_per_group = n_pair // SSD_GROUPS
    specs, cidx = _ssd_specs(Bl, L, D, True)
    specs = specs + [
        pl.BlockSpec((1, 1, n_pair, SSD_STATE, PAIR), lambda b, c: (b, cidx(c), 0, 0, 0)),
        pl.BlockSpec((1, BLK, D), lambda b, c: (b, cidx(c), 0)),
    ]

    def body(xr_ref, bcr_ref, xh_ref, bch_ref, dtr_ref, cwx_ref, cwb_ref, cbx_ref, cbb_ref, dtb_ref, alog_ref,
             dsk_ref, sin_ref, dy_ref, dxbc_ref, ddt_ref, dcwx_ref, dcwb_ref, dcbx_ref, dcbb_ref, misc_ref,
             dst_s, extx_s, extb_s, acs_s, acsT_s, dux_s, dub_s, e2x_s, e2b_s, nxx_s, nxb_s):
        step = pl.program_id(1)
        first = step == nc - 1
        last = step == 0

        @pl.when(last)
        def _():
            dst_s[...] = jnp.zeros_like(dst_s)
            nxx_s[...] = jnp.zeros_like(nxx_s)
            nxb_s[...] = jnp.zeros_like(nxb_s)

        @pl.when(last & (pl.program_id(0) == 0))
        def _():
            for r in (dcwx_ref, dcwb_ref, dcbx_ref, dcbb_ref, misc_ref):
                r[...] = jnp.zeros_like(r)

        lane, lane1 = _iota((BLK, LANES), 1), _iota((1, LANES), 1)
        lo, lo1 = lane < HEAD_DIM, lane1 < HEAD_DIM
        tri = _iota((BLK, BLK), 1) <= _iota((BLK, BLK), 0)
        prex = _conv_pre(extx_s, xh_ref, xr_ref, cwx_ref, cbx_ref, first)
        sgx = _sigmoid(prex)
        ux = prex * sgx
        preb = _conv_pre(extb_s, bch_ref, bcr_ref, cwb_ref, cbb_ref, first)
        sgb = _sigmoid(preb)
        ub = preb * sgb
        dtv, a, acs = _ssd_common(dtr_ref, dtb_ref, alog_ref, acs_s, acsT_s)
        dacs = jnp.zeros((BLK, LANES), F32)
        dlast = jnp.zeros((1, LANES), F32)
        ddt = jnp.zeros((BLK, LANES), F32)
        dd = jnp.zeros((1, LANES), F32)
        for g in range(SSD_GROUPS):
            bg = ub[:, g * SSD_STATE:(g + 1) * SSD_STATE]
            cg = ub[:, (SSD_GROUPS + g) * SSD_STATE:(SSD_GROUPS + g + 1) * SSD_STATE]
            bb, cb_ = bg.astype(BF16), cg.astype(BF16)
            cbm = _dot_nt(cb_, bb)
            cbt = _dot_nt(bb, cb_)
            ctb = cg.T.astype(BF16)
            dbg = jnp.zeros((BLK, SSD_STATE), F32)
            dcg = jnp.zeros((BLK, SSD_STATE), F32)
            for pr in range(g * pairs_per_group, (g + 1) * pairs_per_group):
                t = _pair_terms(pr, acs, dtv, acs_s, lane, lo, lane1, lo1)
                h0, h1 = t["h"]
                xs_p = ux[:, pr * PAIR:(pr + 1) * PAIR]
                dy_p = dy_ref[0, :, pr * PAIR:(pr + 1) * PAIR]
                x_p = xs_p * t["dt_p"]
                ea_p = jnp.exp(t["acs_p"])
                dte_p = jnp.exp(t["last_p"] - t["acs_p"])
                cd_p = jnp.exp(t["last_p"])
                st = sin_ref[0, 0, pr]
                dst = dst_s[pr]
                stb, dstb = st.astype(BF16), dst.astype(BF16)
                s0, s1 = _half_sums(_colsum(dy_p * xs_p), lo1)
                dd = dd + jnp.where(lane1 == h0, s0, 0.0) + jnp.where(lane1 == h1, s1, 0.0)
                d_p = jnp.where(lo1, _lane_col(dsk_ref[...], lane1, h0), _lane_col(dsk_ref[...], lane1, h1))
                dxs_p = d_p * dy_p
                dp = dy_p * ea_p
                dpb = dp.astype(BF16)
                yo = dp * _dot(cb_, stb)
                dcg = dcg + _dot_nt(dpb, stb)
                dst_off = _dot(ctb, dpb)
                dac = [_rowsum(jnp.where(lo, yo, 0.0)), _rowsum(jnp.where(lo, 0.0, yo))]
                s0, s1 = _half_sums(_colsum(dst * st), lo1)
                dl = [s0 * jnp.exp(t["last"][0]), s1 * jnp.exp(t["last"][1])]
                dxd = _dot(bb, dstb)
                dx_p = dxd * dte_p
                tt = dxd * x_p
                dbg = dbg + _dot_nt((x_p * dte_p).astype(BF16), dstb)
                for k, ddte in enumerate((_rowsum(jnp.where(lo, tt, 0.0)), _rowsum(jnp.where(lo, 0.0, tt)))):
                    ek = ddte * jnp.exp(t["last"][k] - t["c"][k])
                    dl[k] = dl[k] + _colsum(ek)
                    dac[k] = dac[k] - ek
                for k in range(2):
                    row = acsT_s[t["h"][k]:t["h"][k] + 1, :]
                    lm, lmt = _decay_tiles(t["c"][k], row, tri, True)
                    msk = lo if k == 0 else ~lo
                    xm = jnp.where(msk, x_p, 0.0).astype(BF16)
                    dym = jnp.where(msk, dy_p, 0.0).astype(BF16)
                    dm = _dot_nt(dym, xm)
                    dmt = _dot_nt(xm, dym)
                    mt = cbt * lmt
                    dx_p = dx_p + _dot(mt.astype(BF16), dym)
                    dac[k] = dac[k] + _rowsum(dm * (cbm * lm)) - _rowsum(dmt * mt)
                    dcg = dcg + _dot((dm * lm).astype(BF16), bb)
                    dbg = dbg + _dot((dmt * lmt).astype(BF16), cb_)
                dacs = dacs + jnp.where(lane == h0, dac[0], 0.0) + jnp.where(lane == h1, dac[1], 0.0)
                dlast = dlast + jnp.where(lane1 == h0, dl[0], 0.0) + jnp.where(lane1 == h1, dl[1], 0.0)
                dxs_p = dxs_p + dx_p * t["dt_p"]
                t3 = dx_p * xs_p
                ddt = ddt + jnp.where(lane == h0, _rowsum(jnp.where(lo, t3, 0.0)), 0.0) \
                    + jnp.where(lane == h1, _rowsum(jnp.where(lo, 0.0, t3)), 0.0)
                dux_s[:, pr * PAIR:(pr + 1) * PAIR] = dxs_p
                dst_s[pr] = dst * cd_p + dst_off
            dub_s[:, g * SSD_STATE:(g + 1) * SSD_STATE] = dbg
            dub_s[:, (SSD_GROUPS + g) * SSD_STATE:(SSD_GROUPS + g + 1) * SSD_STATE] = dcg
        dacs = dacs + jnp.where(_iota((BLK, LANES), 0) == BLK - 1, dlast, 0.0)
        utri = jnp.where(_iota((BLK, BLK), 1) >= _iota((BLK, BLK), 0), 1.0, 0.0).astype(BF16)
        dda = _dot_split(utri, dacs)
        ddt = ddt + dda * a
        ddtr = jnp.where(lane < n_heads, ddt * _sigmoid(dtr_ref[0] + dtb_ref[...]), 0.0)
        ddt_ref[0] = ddtr.astype(BF16)
        misc_ref[0:1, :] += _colsum(ddtr)
        misc_ref[1:2, :] += jnp.where(lane1 < n_heads, _colsum(dda * dtv) * a, 0.0)
        misc_ref[2:3, :] += dd
        for (du_s, pre, sg, ext_s, e2_s, nx_s, w_ref, dcw_ref, dcb_ref, c0, width) in (
                (dux_s, prex, sgx, extx_s, e2x_s, nxx_s, cwx_ref, dcwx_ref, dcbx_ref, 0, D),
                (dub_s, preb, sgb, extb_s, e2b_s, nxb_s, cwb_ref, dcwb_ref, dcbb_ref, D, 512)):
            dpre = du_s[...] * (sg * (1.0 + pre * (1.0 - sg)))
            dcb_ref[...] += _colsum(dpre)
            for i in range(CONV_K):
                dcw_ref[i:i + 1, :] += _colsum(dpre * ext_s[pl.ds(HALO - (CONV_K - 1 - i), BLK), :])
            e2_s[0:BLK, :] = dpre
            e2_s[BLK:BLK + HALO, :] = nx_s[...]
            dxr = jnp.zeros((BLK, width), F32)
            for i in range(CONV_K):
                dxr = dxr + e2_s[pl.ds(CONV_K - 1 - i, BLK), :] * w_ref[i:i + 1, :]
            dxbc_ref[0, :, c0:c0 + width] = dxr.astype(BF16)
            nx_s[...] = e2_s[0:HALO, :]

    full = lambda shape: pl.BlockSpec(shape, lambda b, c: (0,) * len(shape))
    return pl.pallas_call(
        body,
        name="ssd_bwd",
        grid=(Bl, nc),
        in_specs=specs,
        out_specs=[
            pl.BlockSpec((1, BLK, D + 512), lambda b, c: (b, cidx(c), 0)),
            pl.BlockSpec((1, BLK, LANES), lambda b, c: (b, cidx(c), 0)),
            full((CONV_K, D)), full((CONV_K, 512)), full((1, D)), full((1, 512)), full((8, LANES)),
        ],
        out_shape=[
            jax.ShapeDtypeStruct((Bl, L, D + 512), BF16), jax.ShapeDtypeStruct((Bl, L, LANES), BF16),
            jax.ShapeDtypeStruct((CONV_K, D), F32), jax.ShapeDtypeStruct((CONV_K, 512), F32),
            jax.ShapeDtypeStruct((1, D), F32), jax.ShapeDtypeStruct((1, 512), F32),
            jax.ShapeDtypeStruct((8, LANES), F32),
        ],
        scratch_shapes=[
            pltpu.VMEM((n_pair, SSD_STATE, PAIR), F32),
            pltpu.VMEM((HALO + BLK, D), F32), pltpu.VMEM((HALO + BLK, 512), F32),
            pltpu.VMEM((BLK, LANES), F32), pltpu.VMEM((LANES, BLK), F32),
            pltpu.VMEM((BLK, D), F32), pltpu.VMEM((BLK, 512), F32),
            pltpu.VMEM((BLK + HALO, D), F32), pltpu.VMEM((BLK + HALO, 512), F32),
            pltpu.VMEM((HALO, D), F32), pltpu.VMEM((HALO, 512), F32),
        ],
        compiler_params=_params(("arbitrary", "arbitrary")),
    )(proj3, proj3, proj3, proj3, proj3, cwx, cwb, cbx, cbb, dtb, alog, dsk, s_in, dy3)


def _gate_out(x2, tgt2, o2, proj2, y2, sbw, ssw, w_out_bf):
    T, D = x2.shape
    tm = min(256, T)

    def body(x_ref, t_ref, o_ref, zs_ref, y_ref, zy_ref, sbw_ref, ssw_ref, wo_ref,
             dout_ref, doutb_ref, mixt_ref, do_ref, dy_ref, dz_ref, dnw_ref, loss_ref):
        @pl.when(pl.program_id(0) == 0)
        def _():
            dnw_ref[...] = jnp.zeros_like(dnw_ref)
            loss_ref[...] = jnp.zeros_like(loss_ref)

        def fwd(o, z, w):
            sg = _sigmoid(z)
            sl = z * sg
            g = o * sl
            r = lax.rsqrt(jnp.mean(g * g, axis=-1, keepdims=True) + EPS)
            n = g * r
            return sg, sl, r, n, n * w

        def bwd(dy, o, z, w, sg, sl, r, n):
            dn = dy * w
            dg = r * (dn - n * jnp.mean(dn * n, axis=-1, keepdims=True))
            return dg * sl, dg * o * (sg * (1.0 + z * (1.0 - sg))), _colsum(dy * n)

        o1, z1, w1 = o_ref[...], zs_ref[...], sbw_ref[...]
        o2_, z2, w2 = y_ref[...], zy_ref[...], ssw_ref[...]
        sg1, sl1, r1, n1, y1 = fwd(o1, z1, w1)
        sg2, sl2, r2, n2, y2_ = fwd(o2_, z2, w2)
        y1b, y2b = y1.astype(BF16), y2_.astype(BF16)
        mixt_ref[0:D, :] = y1.T.astype(BF16)
        mixt_ref[D:2 * D, :] = y2_.T.astype(BF16)
        out = x_ref[...] + (_dot(y1b, wo_ref[0:D, :]) + _dot(y2b, wo_ref[D:2 * D, :]))
        err = out - t_ref[...]
        loss_ref[...] += jnp.sum(err * err) * (0.5 / D)
        dout = err * (1.0 / D)
        dout_ref[...] = dout
        doutb = dout.astype(BF16)
        doutb_ref[...] = doutb
        do1, dz1, dw1 = bwd(_dot_nt(doutb, wo_ref[0:D, :]), o1, z1, w1, sg1, sl1, r1, n1)
        do2, dz2, dw2 = bwd(_dot_nt(doutb, wo_ref[D:2 * D, :]), o2_, z2, w2, sg2, sl2, r2, n2)
        do_ref[...] = do1
        dy_ref[...] = do2
        dz_ref[:, 0:D] = dz1.astype(BF16)
        dz_ref[:, D:2 * D] = dz2.astype(BF16)
        dnw_ref[0:1, :] += dw1
        dnw_ref[1:2, :] += dw2

    row = lambda col: pl.BlockSpec((tm, D), lambda i: (i, col))
    full = lambda shape: pl.BlockSpec(shape, lambda i: (0,) * len(shape))
    wide = pl.BlockSpec((tm, 2 * D), lambda i: (i, 0))
    return pl.pallas_call(
        body,
        name="gate_out",
        grid=(T // tm,),
        in_specs=[row(0), row(0), row(0), row(3), row(0), row(4), full((1, D)), full((1, D)), full((2 * D, D))],
        out_specs=[row(0), row(0), pl.BlockSpec((2 * D, tm), lambda i: (0, i)), row(0), row(0), wide,
                   full((8, D)), full((8, LANES))],
        out_shape=[
            jax.ShapeDtypeStruct((T, D), F32), jax.ShapeDtypeStruct((T, D), BF16),
            jax.ShapeDtypeStruct((2 * D, T), BF16), jax.ShapeDtypeStruct((T, D), F32),
            jax.ShapeDtypeStruct((T, D), F32), jax.ShapeDtypeStruct((T, 2 * D), BF16),
            jax.ShapeDtypeStruct((8, D), F32), jax.ShapeDtypeStruct((8, LANES), F32),
        ],
        compiler_params=_params(("arbitrary",)),
    )(x2, tgt2, o2, proj2, y2, proj2, sbw, ssw, w_out_bf)


def _dhn(dproj, w_pad, x2, dout, norm_w):
    T, D = x2.shape
    P = w_pad.shape[1]
    tm = min(512, T)
    tk = 1024 if P % 1024 == 0 else 512
    nk = P // tk

    def body(dp_ref, w_ref, x_ref, dout_ref, nw_ref, gx_ref, dnw_ref, acc_s):
        k = pl.program_id(1)

        @pl.when((pl.program_id(0) == 0) & (k == 0))
        def _():
            dnw_ref[...] = jnp.zeros_like(dnw_ref)

        @pl.when(k == 0)
        def _():
            acc_s[...] = jnp.zeros_like(acc_s)

        acc_s[...] += _dot_nt(dp_ref[...], w_ref[...])

        @pl.when(k == nk - 1)
        def _():
            xv = x_ref[...]
            r = lax.rsqrt(jnp.mean(xv * xv, axis=-1, keepdims=True) + EPS)
            xh = xv * r
            dhn = acc_s[...]
            dxh = dhn * nw_ref[...]
            gx_ref[...] = dout_ref[...] + r * (dxh - xh * jnp.mean(dxh * xh, axis=-1, keepdims=True))
            dnw_ref[0:1, :] += _colsum(dhn * xh)

    return pl.pallas_call(
        body,
        name="dhn",
        grid=(T // tm, nk),
        in_specs=[
            pl.BlockSpec((tm, tk), lambda i, k: (i, k)),
            pl.BlockSpec((D, tk), lambda i, k: (0, k)),
            pl.BlockSpec((tm, D), lambda i, k: (i, 0)),
            pl.BlockSpec((tm, D), lambda i, k: (i, 0)),
            pl.BlockSpec((1, D), lambda i, k: (0, 0)),
        ],
        out_specs=[pl.BlockSpec((tm, D), lambda i, k: (i, 0)), pl.BlockSpec((8, D), lambda i, k: (0, 0))],
        out_shape=[jax.ShapeDtypeStruct((T, D), F32), jax.ShapeDtypeStruct((8, D), F32)],
        scratch_shapes=[pltpu.VMEM((tm, D), F32)],
        compiler_params=_params(("arbitrary", "arbitrary")),
    )(dproj, w_pad, x2, dout, norm_w)


def _matmul(a, b, name):
    M, K = a.shape
    N = b.shape[1]
    tm = min(1024, M)
    tn = 1024 if N % 1024 == 0 else (512 if N % 512 == 0 else N)
    tk = min(512, K)

    def body(a_ref, b_ref, o_ref):
        @pl.when(pl.program_id(2) == 0)
        def _():
            o_ref[...] = jnp.zeros_like(o_ref)

        o_ref[...] += _dot(a_ref[...], b_ref[...])

    return pl.pallas_call(
        body,
        name=name,
        grid=(M // tm, N // tn, K // tk),
        in_specs=[pl.BlockSpec((tm, tk), lambda i, j, k: (i, k)), pl.BlockSpec((tk, tn), lambda i, j, k: (k, j))],
        out_specs=pl.BlockSpec((tm, tn), lambda i, j, k: (i, j)),
        out_shape=jax.ShapeDtypeStruct((M, N), F32),
        compiler_params=_params(("parallel", "parallel", "arbitrary")),
    )(a, b)


def _adamw(w, g, m, v, name):
    R, C = w.shape
    tr = 256 if R % 256 == 0 else R
    c1 = 1.0 - ADAM_B1 ** ADAM_STEP
    c2 = 1.0 - ADAM_B2 ** ADAM_STEP

    def body(w_ref, g_ref, m_ref, v_ref, d_ref, nm_ref, nv_ref):
        gv = g_ref[...]
        m_new = ADAM_B1 * m_ref[...] + (1.0 - ADAM_B1) * gv
        v_new = ADAM_B2 * v_ref[...] + (1.0 - ADAM_B2) * (gv * gv)
        d_ref[...] = -ADAM_LR * ((m_new / c1) / (jnp.sqrt(v_new / c2) + ADAM_EPS) + ADAM_WD * w_ref[...])
        nm_ref[...] = m_new
        nv_ref[...] = v_new

    spec = pl.BlockSpec((tr, C), lambda i: (i, 0))
    return pl.pallas_call(
        body,
        name=name,
        grid=(R // tr,),
        in_specs=[spec] * 4,
        out_specs=[spec] * 3,
        out_shape=[jax.ShapeDtypeStruct((R, C), F32)] * 3,
        compiler_params=_params(("parallel",)),
    )(w, g, m, v)


def _add_core_half(a, recv, core, name):
    _, n, h, S = a.shape
    th = 256 if h % 256 == 0 else h

    def body(c_ref, a_ref, r_ref, o_ref):
        o_ref[...] = a_ref[...] + r_ref[...]

    return pl.pallas_call(
        body,
        name=name,
        grid_spec=pltpu.PrefetchScalarGridSpec(
            num_scalar_prefetch=1,
            grid=(n, h // th),
            in_specs=[
                pl.BlockSpec((None, None, th, S), lambda p, i, c: (c[0], p, i, 0)),
                pl.BlockSpec((None, th, S), lambda p, i, c: (p, i, 0)),
            ],
            out_specs=pl.BlockSpec((None, th, S), lambda p, i, c: (p, i, 0)),
        ),
        out_shape=jax.ShapeDtypeStruct((n, h, S), F32),
        compiler_params=_params(("parallel", "parallel")),
    )(core, a, recv)


def _add_chips(hsum, recv, chip, name):
    _, h, S = hsum.shape
    th = 256 if h % 256 == 0 else h

    def body(c_ref, a_ref, r_ref, o_ref):
        o_ref[...] = ((a_ref[...] + r_ref[0]) + r_ref[1]) + r_ref[2]

    return pl.pallas_call(
        body,
        name=name,
        grid_spec=pltpu.PrefetchScalarGridSpec(
            num_scalar_prefetch=1,
            grid=(h // th,),
            in_specs=[
                pl.BlockSpec((None, th, S), lambda i, c: (c[0], i, 0)),
                pl.BlockSpec((N_CHIPS - 1, th, S), lambda i, c: (0, i, 0)),
            ],
            out_specs=pl.BlockSpec((th, S), lambda i, c: (i, 0)),
        ),
        out_shape=jax.ShapeDtypeStruct((h, S), F32),
        compiler_params=_params(("parallel",)),
    )(chip, hsum, recv)


def _place():
    x, y, c = lax.axis_index("x"), lax.axis_index("y"), lax.axis_index("c")
    other_chips = [(1 - x, y), (x, 1 - y), (1 - x, 1 - y)]
    return x, y, c, other_chips


def _allgather_weights(w_in_bf, w_out_bf, conv_w):
    D, S = w_in_bf.shape
    R = w_out_bf.shape[0]
    n_ici, n_fwd = 3 * (N_CHIPS - 1), 2 * (N_CHIPS - 1)

    def body(win, wout, cw, gin, gout, gcw, send_sems, recv_sems):
        x, y, c, chips = _place()
        me = 2 * x + y
        sibling = (x, y, 1 - c)
        hin, hout = D // 2, R // 2

        def halves(chip_idx):
            return (gin.at[chip_idx, pl.ds(c * hin, hin)], gout.at[chip_idx, pl.ds(c * hout, hout)])

        def rcopy(k, src, dst, to):
            return pltpu.make_async_remote_copy(src_ref=src, dst_ref=dst, send_sem=send_sems.at[k],
                                                recv_sem=recv_sems.at[k], device_id=to, device_id_type=MESH)

        my_in, my_out = halves(me)
        src_in, src_out = win.at[pl.ds(c * hin, hin)], wout.at[pl.ds(c * hout, hout)]
        sends = []
        for j, chip in enumerate(chips):
            to = (*chip, c)
            sends += [rcopy(3 * j, src_in, my_in, to), rcopy(3 * j + 1, src_out, my_out, to),
                      rcopy(3 * j + 2, cw, gcw.at[me], to)]
        for cp in sends:
            cp.start()
        passed = []
        for j, (px, py) in enumerate(chips):
            their_in, their_out = halves(2 * px + py)
            rcopy(3 * j, their_in, their_in, sibling).wait_recv()
            rcopy(3 * j + 1, their_out, their_out, sibling).wait_recv()
            rcopy(3 * j + 2, cw, gcw.at[2 * px + py], sibling).wait_recv()
            fw = [rcopy(n_ici + 2 * j, their_in, their_in, sibling), rcopy(n_ici + 2 * j + 1, their_out, their_out, sibling)]
            for cp in fw:
                cp.start()
            passed += fw
        for j, (px, py) in enumerate(chips):
            oin = gin.at[2 * px + py, pl.ds((1 - c) * hin, hin)]
            oout = gout.at[2 * px + py, pl.ds((1 - c) * hout, hout)]
            rcopy(n_ici + 2 * j, oin, oin, sibling).wait_recv()
            rcopy(n_ici + 2 * j + 1, oout, oout, sibling).wait_recv()
        for cp in sends + passed:
            cp.wait_send()

    return pl.pallas_call(
        body,
        name="allgather_weights",
        in_specs=[ANY, ANY, ANY],
        out_specs=[ANY, ANY, ANY],
        out_shape=[jax.ShapeDtypeStruct((N_CHIPS, D, S), BF16), jax.ShapeDtypeStruct((N_CHIPS, R, D), BF16),
                   jax.ShapeDtypeStruct((N_CHIPS,) + conv_w.shape, F32)],
        scratch_shapes=[pltpu.SemaphoreType.DMA((n_ici + n_fwd,)), pltpu.SemaphoreType.DMA((n_ici + n_fwd,))],
    )(w_in_bf, w_out_bf, conv_w)


def _allreduce_small(packed):
    R = packed.shape[0]
    n_dev = 2 * N_CHIPS

    def body(p_ref, o_ref, buf, send_sems, recv_sems):
        x, y, c, _ = _place()
        me = 4 * x + 2 * y + c
        buf[me] = p_ref[...]
        copies = []
        for k in range(1, n_dev):
            px = 1 - x if k & 4 else x
            py = 1 - y if k & 2 else y
            pc = 1 - c if k & 1 else c
            copies.append((pltpu.make_async_remote_copy(
                src_ref=buf.at[me], dst_ref=buf.at[me], send_sem=send_sems.at[k - 1], recv_sem=recv_sems.at[k - 1],
                device_id=(px, py, pc), device_id_type=MESH), 4 * px + 2 * py + pc, (px, py, pc)))
        for cp, _, _ in copies:
            cp.start()
        for k, (_, peer, to) in enumerate(copies):
            pltpu.make_async_remote_copy(
                src_ref=buf.at[peer], dst_ref=buf.at[peer], send_sem=send_sems.at[k], recv_sem=recv_sems.at[k],
                device_id=to, device_id_type=MESH).wait_recv()
        for cp, _, _ in copies:
            cp.wait_send()
        acc = buf[0]
        for d in range(1, n_dev):
            acc = acc + buf[d]
        o_ref[...] = acc

    vm = pl.BlockSpec(memory_space=pltpu.VMEM)
    return pl.pallas_call(
        body,
        name="allreduce_small",
        in_specs=[vm],
        out_specs=vm,
        out_shape=jax.ShapeDtypeStruct((R, LANES), F32),
        scratch_shapes=[pltpu.VMEM((n_dev, R, LANES), F32), pltpu.SemaphoreType.DMA((n_dev - 1,)),
                        pltpu.SemaphoreType.DMA((n_dev - 1,))],
    )(packed)


def _swap_core_halves(a_in, a_out):
    def body(ain, aout, rin, rout, send_sems, recv_sems):
        x, y, c, _ = _place()
        cps = [pltpu.make_async_remote_copy(src_ref=s.at[1 - c], dst_ref=d, send_sem=send_sems.at[k],
                                            recv_sem=recv_sems.at[k], device_id=(x, y, 1 - c), device_id_type=MESH)
               for k, (s, d) in enumerate(((ain, rin), (aout, rout)))]
        for cp in cps:
            cp.start()
        for cp in cps:
            cp.wait()

    return pl.pallas_call(
        body,
        name="reduce_core_swap",
        in_specs=[ANY, ANY],
        out_specs=[ANY, ANY],
        out_shape=[jax.ShapeDtypeStruct(a_in.shape[1:], F32), jax.ShapeDtypeStruct(a_out.shape[1:], F32)],
        scratch_shapes=[pltpu.SemaphoreType.DMA((2,)), pltpu.SemaphoreType.DMA((2,))],
    )(a_in, a_out)


def _scatter_to_chips(h_in, h_out):
    def body(hin, hout, rin, rout, send_sems, recv_sems):
        x, y, c, chips = _place()
        cps = []
        for j, (px, py) in enumerate(chips):
            for k, (s, d) in enumerate(((hin, rin), (hout, rout))):
                cps.append(pltpu.make_async_remote_copy(
                    src_ref=s.at[2 * px + py], dst_ref=d.at[j], send_sem=send_sems.at[2 * j + k],
                    recv_sem=recv_sems.at[2 * j + k], device_id=(px, py, c), device_id_type=MESH))
        for cp in cps:
            cp.start()
        for cp in cps:
            cp.wait()

    n = 2 * (N_CHIPS - 1)
    return pl.pallas_call(
        body,
        name="reduce_chip_scatter",
        in_specs=[ANY, ANY],
        out_specs=[ANY, ANY],
        out_shape=[jax.ShapeDtypeStruct((N_CHIPS - 1,) + h_in.shape[1:], F32),
                   jax.ShapeDtypeStruct((N_CHIPS - 1,) + h_out.shape[1:], F32)],
        scratch_shapes=[pltpu.SemaphoreType.DMA((n,)), pltpu.SemaphoreType.DMA((n,))],
    )(h_in, h_out)


def _join_core_halves(g_in, g_out):
    def body(gin, gout, fin, fout, send_sems, recv_sems):
        x, y, c, _ = _place()
        cps = [pltpu.make_async_remote_copy(src_ref=s, dst_ref=d.at[c], send_sem=send_sems.at[k],
                                            recv_sem=recv_sems.at[k], device_id=(x, y, 1 - c), device_id_type=MESH)
               for k, (s, d) in enumerate(((gin, fin), (gout, fout)))]
        for cp in cps:
            cp.start()
        for k, (s, d) in enumerate(((gin, fin), (gout, fout))):
            pltpu.make_async_remote_copy(src_ref=s, dst_ref=d.at[1 - c], send_sem=send_sems.at[k],
                                         recv_sem=recv_sems.at[k], device_id=(x, y, 1 - c),
                                         device_id_type=MESH).wait_recv()
        for cp in cps:
            cp.wait_send()

    return pl.pallas_call(
        body,
        name="reduce_core_join",
        in_specs=[ANY, ANY],
        out_specs=[ANY, ANY],
        out_shape=[jax.ShapeDtypeStruct((2,) + g_in.shape, F32), jax.ShapeDtypeStruct((2,) + g_out.shape, F32)],
        scratch_shapes=[pltpu.SemaphoreType.DMA((2,)), pltpu.SemaphoreType.DMA((2,))],
    )(g_in, g_out)


def _pack(arrays):
    rows = []
    for a in arrays:
        flat = a.reshape(-1).astype(F32)
        n = -(-flat.shape[0] // LANES) * LANES
        rows.append(jnp.pad(flat, (0, n - flat.shape[0])).reshape(-1, LANES))
    out = jnp.concatenate(rows, axis=0)
    return jnp.pad(out, ((0, -out.shape[0] % 8), (0, 0)))


def _unpack(packed, shapes):
    out, r = [], 0
    for shp in shapes:
        n = math.prod(shp)
        nr = -(-n // LANES)
        out.append(packed[r:r + nr].reshape(-1)[:n].reshape(shp))
        r += nr
    return out


def _pad_lanes(a):
    return jnp.pad(a, ((0, 0), (0, LANES - a.shape[1])))


def kernel(x, norm_w, w_in, q_norm_w, k_norm_w, conv_w, conv_b, dt_bias, A_log, D_skip, sb_norm_w, ssd_norm_w, w_out, loss_target, m_norm_w, m_w_in, m_q_norm_w, m_k_norm_w, m_conv_w, m_conv_b, m_dt_bias, m_A_log, m_D_skip, m_sb_norm_w, m_ssd_norm_w, m_w_out, v_norm_w, v_w_in, v_q_norm_w, v_k_norm_w, v_conv_w, v_conv_b, v_dt_bias, v_A_log, v_D_skip, v_sb_norm_w, v_ssd_norm_w, v_w_out):
    Bl, L, D = x.shape
    T = Bl * L
    S = w_in.shape[2]
    R = w_out.shape[1]
    CW = conv_w.shape[2]
    n_in = N_CHIPS * S
    CD = D + 2 * SSD_GROUPS * SSD_STATE
    H = D // HEAD_DIM
    n_main = 6 * D + 512
    P = -(-(n_main + LANES) // 1024) * 1024
    assert n_in == n_main + H and CD == N_CHIPS * CW and 2 * D == N_CHIPS * R and CD == D + 512
    chip = (2 * lax.axis_index("x") + lax.axis_index("y")).astype(jnp.int32)
    core = lax.axis_index("c").astype(jnp.int32)

    w_in_bf, w_out_shard_bf = w_in[0].astype(BF16), w_out[0].astype(BF16)
    g_in, g_out, g_cw = _allgather_weights(w_in_bf, w_out_shard_bf, conv_w[0])
    g_in = lax.dynamic_update_slice(g_in, w_in_bf[None], (chip, 0, 0))
    g_out = lax.dynamic_update_slice(g_out, w_out_shard_bf[None], (chip, 0, 0))
    g_cw = lax.dynamic_update_slice(g_cw, conv_w, (chip, 0, 0))
    w_pad = jnp.pad(g_in.transpose(1, 0, 2).reshape(D, n_in), ((0, 0), (0, P - n_in)))
    w_out_bf = g_out.reshape(2 * D, D)
    conv_full = g_cw.transpose(1, 0, 2).reshape(CONV_K, CD)
    cwx, cwb = conv_full[:, :D], conv_full[:, D:]
    cbx, cbb = conv_b[:, :D], conv_b[:, D:]
    dtb, alog, dsk = _pad_lanes(dt_bias), _pad_lanes(A_log), _pad_lanes(D_skip)
    qw2, kw2 = jnp.tile(q_norm_w, (1, 2)), jnp.tile(k_norm_w, (1, 2))

    x2 = x.reshape(T, D)
    proj, hn_t = _inproj(x2, norm_w, w_pad)
    proj3 = proj.reshape(Bl, L, P)
    o_sb = _attn_fwd(proj3, qw2, kw2, D)
    y_ssd, s_in = _ssd_fwd(proj3, cwx, cwb, cbx, cbb, dtb, alog, dsk, D)
    dout, dout_bf, mixed_t, do_sb, dy_ssd, dz_bf, dnw_out, loss_blk = _gate_out(
        x2, loss_target.reshape(T, D), o_sb.reshape(T, D), proj, y_ssd.reshape(T, D), sb_norm_w, ssd_norm_w, w_out_bf)

    dq, dk, dv, dqkw = _attn_bwd(proj3, o_sb, do_sb.reshape(Bl, L, D), qw2, kw2, D)
    dxbc, ddt, dcwx, dcwb, dcbx, dcbb, misc = _ssd_bwd(
        proj3, s_in, dy_ssd.reshape(Bl, L, D), cwx, cwb, cbx, cbb, dtb, alog, dsk, D)
    dproj = jnp.concatenate(
        [dq.reshape(T, D), dk.reshape(T, D), dv.reshape(T, D), dz_bf, dxbc.reshape(T, CD), ddt.reshape(T, LANES),
         jnp.zeros((T, P - n_main - LANES), BF16)], axis=1)
    grad_x2, dnw_in = _dhn(dproj, w_pad, x2, dout, norm_w)
    gw_in = _matmul(hn_t, dproj, "grad_w_in")[:, :n_in]
    gw_out = _matmul(mixed_t, dout_bf, "grad_w_out")

    small_shapes = [(1, D), (1, D), (1, D), (1, CD), (1, HEAD_DIM), (1, HEAD_DIM), (1, H), (1, H), (1, H)]
    g_small_local = [dnw_in[0:1], dnw_out[0:1], dnw_out[1:2], jnp.concatenate([dcbx, dcbb], axis=1),
                     dqkw[0:1, :HEAD_DIM] + dqkw[0:1, HEAD_DIM:], dqkw[1:2, :HEAD_DIM] + dqkw[1:2, HEAD_DIM:],
                     misc[0:1, :H], misc[1:2, :H], misc[2:3, :H]]
    packed = _pack(g_small_local + [jnp.concatenate([dcwx, dcwb], axis=1), loss_blk[0:1, 0:1]])
    red = _allreduce_small(packed)
    g_small = _unpack(red, small_shapes + [(CONV_K, CD), (1, 1)])
    g_conv_w = lax.dynamic_slice_in_dim(g_small[9], chip * CW, CW, axis=1)
    loss = g_small[10][0, 0]

    a_in = gw_in.reshape(2, D // 2, N_CHIPS, S).transpose(0, 2, 1, 3)
    a_out = gw_out.reshape(N_CHIPS, 2, R // 2, D).transpose(1, 0, 2, 3)
    r_in, r_out = _swap_core_halves(a_in, a_out)
    core1, chip1 = core.reshape(1), chip.reshape(1)
    h_in = _add_core_half(a_in, r_in, core1, "sum_cores_w_in")
    h_out = _add_core_half(a_out, r_out, core1, "sum_cores_w_out")
    s_in_, s_out_ = _scatter_to_chips(h_in, h_out)
    gh_in = _add_chips(h_in, s_in_, chip1, "sum_chips_w_in")
    gh_out = _add_chips(h_out, s_out_, chip1, "sum_chips_w_out")
    f_in, f_out = _join_core_halves(gh_in, gh_out)
    g_w_in = lax.dynamic_update_slice(f_in, gh_in[None], (core, 0, 0)).reshape(D, S)
    g_w_out = lax.dynamic_update_slice(f_out, gh_out[None], (core, 0, 0)).reshape(R, D)

    d_in, nm_in, nv_in = _adamw(w_in[0], g_w_in, m_w_in[0], v_w_in[0], "adamw_w_in")
    d_out, nm_out, nv_out = _adamw(w_out[0], g_w_out, m_w_out[0], v_w_out[0], "adamw_w_out")
    d_cw, nm_cw, nv_cw = _adamw(conv_w[0], g_conv_w, m_conv_w[0], v_conv_w[0], "adamw_conv_w")
    small_w = [norm_w, sb_norm_w, ssd_norm_w, conv_b, q_norm_w, k_norm_w, dt_bias, A_log, D_skip]
    small_m = [m_norm_w, m_sb_norm_w, m_ssd_norm_w, m_conv_b, m_q_norm_w, m_k_norm_w, m_dt_bias, m_A_log, m_D_skip]
    small_v = [v_norm_w, v_sb_norm_w, v_ssd_norm_w, v_conv_b, v_q_norm_w, v_k_norm_w, v_dt_bias, v_A_log, v_D_skip]
    d_s, nm_s, nv_s = _adamw(_pack(small_w), _pack(g_small[:9]), _pack(small_m), _pack(small_v), "adamw_small")
    d_s, nm_s, nv_s = (_unpack(t, small_shapes) for t in (d_s, nm_s, nv_s))

    def ordered(s, w_in_, conv_w_, w_out_):
        return [s[0], w_in_[None], s[4], s[5], conv_w_[None], s[3], s[6], s[7], s[8], s[1], s[2], w_out_[None]]

    return (loss, grad_x2.reshape(Bl, L, D),
            *ordered(g_small[:9], g_w_in, g_conv_w, g_w_out),
            *ordered(d_s, d_in, d_cw, d_out),
            *ordered(nm_s, nm_in, nm_cw, nm_out),
            *ordered(nv_s, nv_in, nv_cw, nv_out))
```

```python
import functools
import math

import jax
import jax.numpy as jnp
from jax import lax
from jax.experimental import pallas as pl
from jax.experimental.pallas import tpu as pltpu

F32 = jnp.float32
BF16 = jnp.bfloat16
EPS = 1e-6
HEAD_DIM = 64
PAIR = 2 * HEAD_DIM
LANES = 128
SSD_STATE = 128
SSD_GROUPS = 2
BLK = 128
PREP_BLOCKS = 4
FIRST_LEFT = 2
UNDERFLOW = -105.0
CONV_K = 4
HALO = 8
N_CHIPS = 4
ADAM_LR, ADAM_B1, ADAM_B2, ADAM_EPS, ADAM_WD, ADAM_STEP = 0.001, 0.9, 0.999, 1e-08, 0.01, 10
VMEM_LIMIT_V7X = 56 * 1024 * 1024
MESH = pl.DeviceIdType.MESH
ANY = pl.BlockSpec(memory_space=pl.ANY)
NT = (((1,), (1,)), ((), ()))


def _params(sem=None):
    kw = dict(vmem_limit_bytes=VMEM_LIMIT_V7X)
    if sem is not None:
        kw["dimension_semantics"] = sem
    return pltpu.CompilerParams(**kw)


def _dot(a, b):
    return jnp.dot(a, b, preferred_element_type=F32)


def _dot_nt(a, b):
    return lax.dot_general(a, b, NT, preferred_element_type=F32)


def _dot_split(m, x):
    hi = x.astype(BF16)
    lo = (x - hi.astype(F32)).astype(BF16)
    return _dot(m, hi) + _dot(m, lo)


def _iota(shape, dim):
    return lax.broadcasted_iota(jnp.int32, shape, dim)


def _rowsum(x):
    return jnp.sum(x, axis=1, keepdims=True)


def _colsum(x):
    return jnp.sum(x, axis=0, keepdims=True)


def _sigmoid(x):
    return 1.0 / (1.0 + jnp.exp(-x))


def _softplus(x):
    return jnp.maximum(x, 0.0) + jnp.log(1.0 + jnp.exp(-jnp.abs(x)))


def _inproj(x2, norm_w, w_pad):
    T, D = x2.shape
    P = w_pad.shape[1]
    tm = min(512, T)
    tn = 1024 if P % 1024 == 0 else 512

    def body(x_ref, nw_ref, w_ref, proj_ref, hnt_ref, hn_s):
        @pl.when(pl.program_id(1) == 0)
        def _():
            xv = x_ref[...]
            r = lax.rsqrt(jnp.mean(xv * xv, axis=-1, keepdims=True) + EPS)
            hn = xv * r * nw_ref[...]
            hn_s[...] = hn.astype(BF16)
            hnt_ref[...] = hn.T.astype(BF16)

        proj_ref[...] = _dot(hn_s[...], w_ref[...])

    return pl.pallas_call(
        body,
        name="inproj",
        grid=(T // tm, P // tn),
        in_specs=[
            pl.BlockSpec((tm, D), lambda i, j: (i, 0)),
            pl.BlockSpec((1, D), lambda i, j: (0, 0)),
            pl.BlockSpec((D, tn), lambda i, j: (0, j)),
        ],
        out_specs=[
            pl.BlockSpec((tm, tn), lambda i, j: (i, j)),
            pl.BlockSpec((D, tm), lambda i, j: (0, i)),
        ],
        out_shape=[jax.ShapeDtypeStruct((T, P), F32), jax.ShapeDtypeStruct((D, T), BF16)],
        scratch_shapes=[pltpu.VMEM((tm, D), BF16)],
        compiler_params=_params(("parallel", "arbitrary")),
    )(x2, norm_w, w_pad)


def _pair_ones():
    ri = (_iota((PAIR, PAIR), 0) >= HEAD_DIM).astype(jnp.int32)
    ci = (_iota((PAIR, PAIR), 1) >= HEAD_DIM).astype(jnp.int32)
    return jnp.where(ri == ci, 1.0, 0.0).astype(BF16)


def _pair_rms(v, ones2):
    return lax.rsqrt(_split_dots([v * v], ones2)[0] * (1.0 / HEAD_DIM) + EPS)


def _pair_mean(v, ones2):
    return _split_dots([v], ones2)[0] * (1.0 / HEAD_DIM)


def _suffix_ones():
    ri = _iota((BLK, 2 * BLK), 0)
    ci = _iota((BLK, 2 * BLK), 1)
    return jnp.where((ci >= BLK) | (ri > ci), 1.0, 0.0).astype(BF16)


def _split_dots(xs, m):
    his = [x.astype(BF16) for x in xs]
    los = [(x - hi.astype(F32)).astype(BF16) for x, hi in zip(xs, his)]
    return [_dot(hi, m) + _dot(lo, m) for hi, lo in zip(his, los)]


def _sb_tiles(qms, km_s, c0s, rest, uo, masks):
    tiles = [(u, h) for u in range(len(c0s)) for h in range(2)]
    zs = [_dot_nt(qms[h], km_s[h, pl.ds(c0s[u], BLK), :]) for u, h in tiles]
    es = [jnp.exp(-jnp.abs(z)) for z in zs]
    las = [jnp.minimum(z, 0.0) - jnp.log(1.0 + e) for z, e in zip(zs, es)]
    lks = [a - z for a, z in zip(las, zs)]
    lks = [lk if masks[u] is None else jnp.where(masks[u], lk, 0.0) for (u, h), lk in zip(tiles, lks)]
    css = _split_dots(lks, uo)
    rest = list(rest)
    ws = []
    for (u, h), a, cs in zip(tiles, las, css):
        w = jnp.exp(a + rest[h] + cs[:, :BLK])
        ws.append(w if masks[u] is None else jnp.where(masks[u], w, 0.0))
        rest[h] = rest[h] + cs[:, BLK:]
    return tiles, zs, es, ws, rest


def _first_blocks(qi, diag):
    c0s = [pl.multiple_of(jnp.maximum(qi - u, 0) * BLK, BLK) for u in range(1 + FIRST_LEFT)]
    masks = [diag] + [jnp.broadcast_to(qi - u >= 0, (BLK, BLK)) for u in range(1, 1 + FIRST_LEFT)]
    return c0s, masks


def _attn_prep(src_ref, w_ref, dst_s, n_blocks, scale):
    per = math.gcd(PREP_BLOCKS, n_blocks)
    rows = per * BLK
    lo = _iota((rows, PAIR), 1) < HEAD_DIM
    ones2 = _pair_ones()

    def step(i, carry):
        r0 = pl.multiple_of(i * rows, rows)
        v = src_ref[0, pl.ds(r0, rows), :]
        if w_ref is not None:
            v = v * _pair_rms(v, ones2) * w_ref[...]
        if scale != 1.0:
            v = v * scale
        dst_s[0, pl.ds(r0, rows), :] = jnp.where(lo, v, 0.0).astype(BF16)
        dst_s[1, pl.ds(r0, rows), :] = jnp.where(lo, 0.0, v).astype(BF16)
        return carry

    lax.fori_loop(0, n_blocks // per, step, 0)


def _attn_fwd(proj3, qw2, kw2, D):
    Bl, L, _ = proj3.shape
    n_pair = D // PAIR
    nq = L // BLK
    scale = 1.0 / math.sqrt(HEAD_DIM)

    def body(q_ref, k_ref, v_ref, qw_ref, kw_ref, o_ref, qm_s, km_s, vm_s):
        uo = _suffix_ones()
        diag = _iota((BLK, BLK), 1) < _iota((BLK, BLK), 0)
        _attn_prep(q_ref, qw_ref, qm_s, nq, scale)
        _attn_prep(k_ref, kw_ref, km_s, nq, 1.0)
        _attn_prep(v_ref, None, vm_s, nq, 1.0)

        def qblock(qi, carry):
            r0 = pl.multiple_of(qi * BLK, BLK)
            qms = [qm_s[h, pl.ds(r0, BLK), :] for h in range(2)]

            def sweep(state, c0s, masks):
                kb, rc0, rc1, acc, _ = state
                tiles, _, _, ws, rc = _sb_tiles(qms, km_s, c0s, [rc0, rc1], uo, masks)
                wbs = [w.astype(BF16) for w in ws]
                for (u, h), wb in zip(tiles, wbs):
                    acc = acc + _dot(wb, vm_s[h, pl.ds(c0s[u], BLK), :])
                return kb - len(c0s), rc[0], rc[1], acc, jnp.maximum(jnp.max(rc[0]), jnp.max(rc[1]))

            def left(state, n):
                return [pl.multiple_of((state[0] - u) * BLK, BLK) for u in range(n)]

            zero_c = jnp.zeros((BLK, BLK), F32)
            state = sweep((qi, zero_c, zero_c, jnp.zeros((BLK, PAIR), F32), 0.0), *_first_blocks(qi, diag))
            state = lax.while_loop(lambda s: (s[0] >= 1) & (s[4] >= UNDERFLOW), lambda s: sweep(s, left(s, 2), [None] * 2), state)
            state = lax.while_loop(lambda s: (s[0] >= 0) & (s[4] >= UNDERFLOW), lambda s: sweep(s, left(s, 1), [None]), state)
            o_ref[0, pl.ds(r0, BLK), :] = state[3]
            return carry

        lax.fori_loop(0, nq, qblock, 0)

    blk = lambda off: pl.BlockSpec((1, L, PAIR), lambda b, p: (b, 0, off + p))
    wspec = pl.BlockSpec((1, PAIR), lambda b, p: (0, 0))
    return pl.pallas_call(
        body,
        name="sb_attn_fwd",
        grid=(Bl, n_pair),
        in_specs=[blk(0), blk(n_pair), blk(2 * n_pair), wspec, wspec],
        out_specs=pl.BlockSpec((1, L, PAIR), lambda b, p: (b, 0, p)),
        out_shape=jax.ShapeDtypeStruct((Bl, L, D), F32),
        scratch_shapes=[pltpu.VMEM((2, L, PAIR), BF16)] * 3,
        compiler_params=_params(("parallel", "parallel")),
    )(proj3, proj3, proj3, qw2, kw2)


def _attn_bwd(proj3, o3, do3, qw2, kw2, D):
    Bl, L, _ = proj3.shape
    n_pair = D // PAIR
    nq = L // BLK
    scale = 1.0 / math.sqrt(HEAD_DIM)

    def body(q_ref, k_ref, v_ref, o_ref, do_ref, qw_ref, kw_ref, dq_ref, dk_ref, dv_ref, dw_ref,
             qm_s, km_s, vm_s, dom_s, dq_s, dk_s, dv_s):
        uo = _suffix_ones()
        diag = _iota((BLK, BLK), 1) < _iota((BLK, BLK), 0)
        ones2 = _pair_ones()
        _attn_prep(q_ref, qw_ref, qm_s, nq, scale)
        _attn_prep(k_ref, kw_ref, km_s, nq, 1.0)
        _attn_prep(v_ref, None, vm_s, nq, 1.0)
        _attn_prep(do_ref, None, dom_s, nq, 1.0)

        @pl.when((pl.program_id(0) == 0) & (pl.program_id(1) == 0))
        def _():
            dw_ref[...] = jnp.zeros_like(dw_ref)

        def zero(i, carry):
            r0 = pl.multiple_of(i * BLK, BLK)
            dk_s[pl.ds(r0, BLK), :] = jnp.zeros((BLK, PAIR), F32)
            dv_s[pl.ds(r0, BLK), :] = jnp.zeros((BLK, PAIR), F32)
            return carry

        lax.fori_loop(0, nq, zero, 0)

        def tiles_bwd(qms, doms, deltas, c0s, rc, gc, dqa, masks):
            dws = [_dot_nt(doms[h], vm_s[h, pl.ds(c0s[u], BLK), :]) for u in range(len(c0s)) for h in range(2)]
            tiles, zs, es, ws, rc = _sb_tiles(qms, km_s, c0s, rc, uo, masks)
            wfs = [w.astype(BF16).astype(F32) for w in ws]
            gs = [wf * dw for wf, dw in zip(wfs, dws)]
            gss = _split_dots(gs, uo)
            gc = list(gc)
            dzs = []
            for (u, h), z, e, g, gsum in zip(tiles, zs, es, gs, gss):
                g_before = deltas[h] - (gc[h] + gsum[:, :BLK] + g)
                gc[h] = gc[h] + gsum[:, BLK:]
                r = 1.0 / (1.0 + e)
                er = e * r
                pos = z >= 0.0
                dz = g * jnp.where(pos, er, r) - g_before * jnp.where(pos, r, er)
                dzs.append(dz if masks[u] is None else jnp.where(masks[u], dz, 0.0))
            wts = [wf.T.astype(BF16) for wf in wfs]
            dzts = [dz.T.astype(BF16) for dz in dzs]
            dzbs = [dz.astype(BF16) for dz in dzs]
            for u, c0 in enumerate(c0s):
                dv_s[pl.ds(c0, BLK), :] += _dot(wts[2 * u], doms[0]) + _dot(wts[2 * u + 1], doms[1])
                dk_s[pl.ds(c0, BLK), :] += _dot(dzts[2 * u], qms[0]) + _dot(dzts[2 * u + 1], qms[1])
            for (u, h), dzb in zip(tiles, dzbs):
                dqa = dqa + _dot(dzb, km_s[h, pl.ds(c0s[u], BLK), :])
            return rc, gc, dqa

        def qblock(qi, carry):
            r0 = pl.multiple_of(qi * BLK, BLK)
            o_blk = o_ref[0, pl.ds(r0, BLK), :]
            qms = [qm_s[h, pl.ds(r0, BLK), :] for h in range(2)]
            doms = [dom_s[h, pl.ds(r0, BLK), :] for h in range(2)]
            deltas = [_rowsum(doms[h].astype(F32) * o_blk) for h in range(2)]

            def sweep(state, c0s, masks):
                kb, rc0, rc1, gc0, gc1, dqa, _ = state
                rc, gc, dqa = tiles_bwd(qms, doms, deltas, c0s, [rc0, rc1], [gc0, gc1], dqa, masks)
                return kb - len(c0s), rc[0], rc[1], gc[0], gc[1], dqa, jnp.maximum(jnp.max(rc[0]), jnp.max(rc[1]))

            def left(state, n):
                return [pl.multiple_of((state[0] - u) * BLK, BLK) for u in range(n)]

            zero_c = jnp.zeros((BLK, BLK), F32)
            state = sweep((qi, zero_c, zero_c, zero_c, zero_c, jnp.zeros((BLK, PAIR), F32), 0.0), *_first_blocks(qi, diag))
            state = lax.while_loop(lambda s: (s[0] >= 1) & (s[6] >= UNDERFLOW), lambda s: sweep(s, left(s, 2), [None] * 2), state)
            state = lax.while_loop(lambda s: (s[0] >= 0) & (s[6] >= UNDERFLOW), lambda s: sweep(s, left(s, 1), [None]), state)
            dq_s[pl.ds(r0, BLK), :] = state[5] * scale
            return carry

        lax.fori_loop(0, nq, qblock, 0)

        per = math.gcd(PREP_BLOCKS, nq)
        rows = per * BLK

        def finish(i, carry):
            r0 = pl.multiple_of(i * rows, rows)
            dwq, dwk = carry
            out = []
            for src_ref, w_ref, d_s in ((q_ref, qw_ref, dq_s), (k_ref, kw_ref, dk_s)):
                v = src_ref[0, pl.ds(r0, rows), :]
                r = _pair_rms(v, ones2)
                vh = v * r
                dy = d_s[pl.ds(r0, rows), :]
                dvh = dy * w_ref[...]
                out.append((r * (dvh - vh * _pair_mean(dvh * vh, ones2)), _colsum(dy * vh)))
            dq_ref[0, pl.ds(r0, rows), :] = out[0][0].astype(BF16)
            dk_ref[0, pl.ds(r0, rows), :] = out[1][0].astype(BF16)
            dv_ref[0, pl.ds(r0, rows), :] = dv_s[pl.ds(r0, rows), :].astype(BF16)
            return dwq + out[0][1], dwk + out[1][1]

        zrow = jnp.zeros((1, PAIR), F32)
        dwq, dwk = lax.fori_loop(0, nq // per, finish, (zrow, zrow))
        dw_ref[0:1, :] += dwq
        dw_ref[1:2, :] += dwk

    blk = lambda off: pl.BlockSpec((1, L, PAIR), lambda b, p: (b, 0, off + p))
    wspec = pl.BlockSpec((1, PAIR), lambda b, p: (0, 0))
    oblk = pl.BlockSpec((1, L, PAIR), lambda b, p: (b, 0, p))
    return pl.pallas_call(
        body,
        name="sb_attn_bwd",
        grid=(Bl, n_pair),
        in_specs=[blk(0), blk(n_pair), blk(2 * n_pair), oblk, oblk, wspec, wspec],
        out_specs=[oblk, oblk, oblk, pl.BlockSpec((8, PAIR), lambda b, p: (0, 0))],
        out_shape=[jax.ShapeDtypeStruct((Bl, L, D), BF16)] * 3 + [jax.ShapeDtypeStruct((8, PAIR), F32)],
        scratch_shapes=[pltpu.VMEM((2, L, PAIR), BF16)] * 4 + [pltpu.VMEM((L, PAIR), F32)] * 3,
        compiler_params=_params(("arbitrary", "arbitrary")),
    )(proj3, proj3, proj3, o3, do3, qw2, kw2)


def _conv_pre(ext_s, halo_ref, raw_ref, w_ref, b_ref, first):
    ext_s[0:HALO, :] = jnp.where(first, 0.0, halo_ref[0])
    ext_s[HALO:HALO + BLK, :] = raw_ref[0]
    pre = b_ref[...]
    for i in range(CONV_K):
        pre = pre + ext_s[pl.ds(HALO - (CONV_K - 1 - i), BLK), :] * w_ref[i:i + 1, :]
    return pre


def _lane_col(m, lane, h):
    return _rowsum(jnp.where(lane == h, m, 0.0))


def _half_sums(row, lo1):
    return _rowsum(jnp.where(lo1, row, 0.0)), _rowsum(jnp.where(lo1, 0.0, row))


def _ssd_specs(Bl, L, D, rev):
    nc = L // BLK
    rows_per = BLK // HALO
    cidx = (lambda c: nc - 1 - c) if rev else (lambda c: c)
    xoff = 5
    boff = (6 * D) // 512
    doff = (6 * D + 512) // LANES
    prev = lambda c: jnp.maximum(cidx(c) * rows_per - 1, 0)
    specs = [
        pl.BlockSpec((1, BLK, D), lambda b, c: (b, cidx(c), xoff)),
        pl.BlockSpec((1, BLK, 512), lambda b, c: (b, cidx(c), boff)),
        pl.BlockSpec((1, HALO, D), lambda b, c: (b, prev(c), xoff)),
        pl.BlockSpec((1, HALO, 512), lambda b, c: (b, prev(c), boff)),
        pl.BlockSpec((1, BLK, LANES), lambda b, c: (b, cidx(c), doff)),
    ]
    full = lambda shape: pl.BlockSpec(shape, lambda b, c: (0,) * len(shape))
    specs += [full((CONV_K, D)), full((CONV_K, 512)), full((1, D)), full((1, 512)),
              full((1, LANES)), full((1, LANES)), full((1, LANES))]
    return specs, cidx


def _ssd_common(dtr_ref, dtb_ref, alog_ref, acs_s, acsT_s):
    ltri = jnp.where(_iota((BLK, BLK), 1) <= _iota((BLK, BLK), 0), 1.0, 0.0).astype(BF16)
    dtv = _softplus(dtr_ref[0] + dtb_ref[...])
    a = -jnp.exp(alog_ref[...])
    acs = _dot_split(ltri, dtv * a)
    acs_s[...] = acs
    acsT_s[...] = acs.T
    return dtv, a, acs


def _pair_terms(pr, acs, dtv, acs_s, lane, lo, lane1, lo1):
    h0, h1 = 2 * pr, 2 * pr + 1
    c0, c1 = _lane_col(acs, lane, h0), _lane_col(acs, lane, h1)
    d0, d1 = _lane_col(dtv, lane, h0), _lane_col(dtv, lane, h1)
    lastv = acs_s[BLK - 1:BLK, :]
    l0, l1 = _lane_col(lastv, lane1, h0), _lane_col(lastv, lane1, h1)
    return dict(h=(h0, h1), c=(c0, c1), last=(l0, l1), acs_p=jnp.where(lo, c0, c1), dt_p=jnp.where(lo, d0, d1),
                last_p=jnp.where(lo1, l0, l1))


def _decay_tiles(cc, row, tri, want_t):
    lm = jnp.where(tri, jnp.exp(jnp.where(tri, cc - row, 0.0)), 0.0)
    if not want_t:
        return lm, None
    tri_t = _iota((BLK, BLK), 1) >= _iota((BLK, BLK), 0)
    return lm, jnp.where(tri_t, jnp.exp(jnp.where(tri_t, row - cc, 0.0)), 0.0)


def _ssd_fwd(proj3, cwx, cwb, cbx, cbb, dtb, alog, dsk, D):
    Bl, L, _ = proj3.shape
    nc = L // BLK
    n_pair = D // PAIR
    pairs_per_group = n_pair // SSD_GROUPS
    specs, _ = _ssd_specs(Bl, L, D, False)

    def body(xr_ref, bcr_ref, xh_ref, bch_ref, dtr_ref, cwx_ref, cwb_ref, cbx_ref, cbb_ref, dtb_ref, alog_ref,
             dsk_ref, y_ref, sin_ref, st_s, extx_s, extb_s, acs_s, acsT_s):
        first = pl.program_id(1) == 0

        @pl.when(first)
        def _():
            st_s[...] = jnp.zeros_like(st_s)

        lane, lane1 = _iota((BLK, LANES), 1), _iota((1, LANES), 1)
        lo, lo1 = lane < HEAD_DIM, lane1 < HEAD_DIM
        tri = _iota((BLK, BLK), 1) <= _iota((BLK, BLK), 0)
        pre = _conv_pre(extx_s, xh_ref, xr_ref, cwx_ref, cbx_ref, first)
        ux = pre * _sigmoid(pre)
        pre = _conv_pre(extb_s, bch_ref, bcr_ref, cwb_ref, cbb_ref, first)
        ub = pre * _sigmoid(pre)
        dtv, a, acs = _ssd_common(dtr_ref, dtb_ref, alog_ref, acs_s, acsT_s)
        for g in range(SSD_GROUPS):
            bg = ub[:, g * SSD_STATE:(g + 1) * SSD_STATE]
            cb_ = ub[:, (SSD_GROUPS + g) * SSD_STATE:(SSD_GROUPS + g + 1) * SSD_STATE].astype(BF16)
            cbm = _dot_nt(cb_, bg.astype(BF16))
            btb = bg.T.astype(BF16)
            for pr in range(g * pairs_per_group, (g + 1) * pairs_per_group):
                t = _pair_terms(pr, acs, dtv, acs_s, lane, lo, lane1, lo1)
                xs_p = ux[:, pr * PAIR:(pr + 1) * PAIR]
                x_p = xs_p * t["dt_p"]
                st = st_s[pr]
                sin_ref[0, 0, pr] = st
                y = _dot(cb_, st.astype(BF16)) * jnp.exp(t["acs_p"])
                for k in range(2):
                    row = acsT_s[t["h"][k]:t["h"][k] + 1, :]
                    lm, _ = _decay_tiles(t["c"][k], row, tri, False)
                    xm = jnp.where(lo if k == 0 else ~lo, x_p, 0.0).astype(BF16)
                    y = y + _dot((cbm * lm).astype(BF16), xm)
                d_p = jnp.where(lo1, _lane_col(dsk_ref[...], lane1, t["h"][0]), _lane_col(dsk_ref[...], lane1, t["h"][1]))
                y_ref[0, :, pr * PAIR:(pr + 1) * PAIR] = y + d_p * xs_p
                xd = (x_p * jnp.exp(t["last_p"] - t["acs_p"])).astype(BF16)
                st_s[pr] = st * jnp.exp(t["last_p"]) + _dot(btb, xd)

    return pl.pallas_call(
        body,
        name="ssd_fwd",
        grid=(Bl, nc),
        in_specs=specs,
        out_specs=[
            pl.BlockSpec((1, BLK, D), lambda b, c: (b, c, 0)),
            pl.BlockSpec((1, 1, n_pair, SSD_STATE, PAIR), lambda b, c: (b, c, 0, 0, 0)),
        ],
        out_shape=[jax.ShapeDtypeStruct((Bl, L, D), F32),
                   jax.ShapeDtypeStruct((Bl, nc, n_pair, SSD_STATE, PAIR), F32)],
        scratch_shapes=[pltpu.VMEM((n_pair, SSD_STATE, PAIR), F32), pltpu.VMEM((HALO + BLK, D), F32),
                        pltpu.VMEM((HALO + BLK, 512), F32), pltpu.VMEM((BLK, LANES), F32),
                        pltpu.VMEM((LANES, BLK), F32)],
        compiler_params=_params(("arbitrary", "arbitrary")),
    )(proj3, proj3, proj3, proj3, proj3, cwx, cwb, cbx, cbb, dtb, alog, dsk)


def _ssd_bwd(proj3, s_in, dy3, cwx, cwb, cbx, cbb, dtb, alog, dsk, D, tail):
    Bl, L, _ = proj3.shape
    CD = D + 512
    nc = L // BLK
    n_pair = D // PAIR
    n_heads = 2 * n_pair
    pairs_per_group = n_pair // SSD_GROUPS
    specs, cidx = _ssd_specs(Bl, L, D, True)
    specs = specs + [
        pl.BlockSpec((1, 1, n_pair, SSD_STATE, PAIR), lambda b, c: (b, cidx(c), 0, 0, 0)),
        pl.BlockSpec((1, BLK, D), lambda b, c: (b, cidx(c), 0)),
    ]

    def body(xr_ref, bcr_ref, xh_ref, bch_ref, dtr_ref, cwx_ref, cwb_ref, cbx_ref, cbb_ref, dtb_ref, alog_ref,
             dsk_ref, sin_ref, dy_ref, dxbc_ref, dcwx_ref, dcwb_ref, dcbx_ref, dcbb_ref, misc_ref,
             dst_s, extx_s, extb_s, acs_s, acsT_s, dux_s, dub_s, e2x_s, e2b_s, nxx_s, nxb_s):
        step = pl.program_id(1)
        first = step == nc - 1
        last = step == 0

        @pl.when(last)
        def _():
            dst_s[...] = jnp.zeros_like(dst_s)
            nxx_s[...] = jnp.zeros_like(nxx_s)
            nxb_s[...] = jnp.zeros_like(nxb_s)

        @pl.when(last & (pl.program_id(0) == 0))
        def _():
            for r in (dcwx_ref, dcwb_ref, dcbx_ref, dcbb_ref, misc_ref):
                r[...] = jnp.zeros_like(r)

        lane, lane1 = _iota((BLK, LANES), 1), _iota((1, LANES), 1)
        lo, lo1 = lane < HEAD_DIM, lane1 < HEAD_DIM
        tri = _iota((BLK, BLK), 1) <= _iota((BLK, BLK), 0)
        prex = _conv_pre(extx_s, xh_ref, xr_ref, cwx_ref, cbx_ref, first)
        sgx = _sigmoid(prex)
        ux = prex * sgx
        preb = _conv_pre(extb_s, bch_ref, bcr_ref, cwb_ref, cbb_ref, first)
        sgb = _sigmoid(preb)
        ub = preb * sgb
        dtv, a, acs = _ssd_common(dtr_ref, dtb_ref, alog_ref, acs_s, acsT_s)
        dacs = jnp.zeros((BLK, LANES), F32)
        dlast = jnp.zeros((1, LANES), F32)
        ddt = jnp.zeros((BLK, LANES), F32)
        dd = jnp.zeros((1, LANES), F32)
        for g in range(SSD_GROUPS):
            bg = ub[:, g * SSD_STATE:(g + 1) * SSD_STATE]
            cg = ub[:, (SSD_GROUPS + g) * SSD_STATE:(SSD_GROUPS + g + 1) * SSD_STATE]
            bb, cb_ = bg.astype(BF16), cg.astype(BF16)
            cbm = _dot_nt(cb_, bb)
            cbt = _dot_nt(bb, cb_)
            ctb = cg.T.astype(BF16)
            dbg = jnp.zeros((BLK, SSD_STATE), F32)
            dcg = jnp.zeros((BLK, SSD_STATE), F32)
            for pr in range(g * pairs_per_group, (g + 1) * pairs_per_group):
                t = _pair_terms(pr, acs, dtv, acs_s, lane, lo, lane1, lo1)
                h0, h1 = t["h"]
                xs_p = ux[:, pr * PAIR:(pr + 1) * PAIR]
                dy_p = dy_ref[0, :, pr * PAIR:(pr + 1) * PAIR]
                x_p = xs_p * t["dt_p"]
                ea_p = jnp.exp(t["acs_p"])
                dte_p = jnp.exp(t["last_p"] - t["acs_p"])
                cd_p = jnp.exp(t["last_p"])
                st = sin_ref[0, 0, pr]
                dst = dst_s[pr]
                stb, dstb = st.astype(BF16), dst.astype(BF16)
                s0, s1 = _half_sums(_colsum(dy_p * xs_p), lo1)
                dd = dd + jnp.where(lane1 == h0, s0, 0.0) + jnp.where(lane1 == h1, s1, 0.0)
                d_p = jnp.where(lo1, _lane_col(dsk_ref[...], lane1, h0), _lane_col(dsk_ref[...], lane1, h1))
                dxs_p = d_p * dy_p
                dp = dy_p * ea_p
                dpb = dp.astype(BF16)
                yo = dp * _dot(cb_, stb)
                dcg = dcg + _dot_nt(dpb, stb)
                dst_off = _dot(ctb, dpb)
                dac = [_rowsum(jnp.where(lo, yo, 0.0)), _rowsum(jnp.where(lo, 0.0, yo))]
                s0, s1 = _half_sums(_colsum(dst * st), lo1)
                dl = [s0 * jnp.exp(t["last"][0]), s1 * jnp.exp(t["last"][1])]
                dxd = _dot(bb, dstb)
                dx_p = dxd * dte_p
                tt = dxd * x_p
                dbg = dbg + _dot_nt((x_p * dte_p).astype(BF16), dstb)
                for k, ddte in enumerate((_rowsum(jnp.where(lo, tt, 0.0)), _rowsum(jnp.where(lo, 0.0, tt)))):
                    ek = ddte * jnp.exp(t["last"][k] - t["c"][k])
                    dl[k] = dl[k] + _colsum(ek)
                    dac[k] = dac[k] - ek
                for k in range(2):
                    row = acsT_s[t["h"][k]:t["h"][k] + 1, :]
                    lm, lmt = _decay_tiles(t["c"][k], row, tri, True)
                    msk = lo if k == 0 else ~lo
                    xm = jnp.where(msk, x_p, 0.0).astype(BF16)
                    dym = jnp.where(msk, dy_p, 0.0).astype(BF16)
                    dm = _dot_nt(dym, xm)
                    dmt = _dot_nt(xm, dym)
                    mt = cbt * lmt
                    dx_p = dx_p + _dot(mt.astype(BF16), dym)
                    dac[k] = dac[k] + _rowsum(dm * (cbm * lm)) - _rowsum(dmt * mt)
                    dcg = dcg + _dot((dm * lm).astype(BF16), bb)
                    dbg = dbg + _dot((dmt * lmt).astype(BF16), cb_)
                dacs = dacs + jnp.where(lane == h0, dac[0], 0.0) + jnp.where(lane == h1, dac[1], 0.0)
                dlast = dlast + jnp.where(lane1 == h0, dl[0], 0.0) + jnp.where(lane1 == h1, dl[1], 0.0)
                dxs_p = dxs_p + dx_p * t["dt_p"]
                t3 = dx_p * xs_p
                ddt = ddt + jnp.where(lane == h0, _rowsum(jnp.where(lo, t3, 0.0)), 0.0) \
                    + jnp.where(lane == h1, _rowsum(jnp.where(lo, 0.0, t3)), 0.0)
                dux_s[:, pr * PAIR:(pr + 1) * PAIR] = dxs_p
                dst_s[pr] = dst * cd_p + dst_off
            dub_s[:, g * SSD_STATE:(g + 1) * SSD_STATE] = dbg
            dub_s[:, (SSD_GROUPS + g) * SSD_STATE:(SSD_GROUPS + g + 1) * SSD_STATE] = dcg
        dacs = dacs + jnp.where(_iota((BLK, LANES), 0) == BLK - 1, dlast, 0.0)
        utri = jnp.where(_iota((BLK, BLK), 1) >= _iota((BLK, BLK), 0), 1.0, 0.0).astype(BF16)
        dda = _dot_split(utri, dacs)
        ddt = ddt + dda * a
        ddtr = jnp.where(lane < n_heads, ddt * _sigmoid(dtr_ref[0] + dtb_ref[...]), 0.0)
        dxbc_ref[0, :, CD:CD + LANES] = ddtr.astype(BF16)
        dxbc_ref[0, :, CD + LANES:tail] = jnp.zeros((BLK, tail - CD - LANES), BF16)
        misc_ref[0:1, :] += _colsum(ddtr)
        misc_ref[1:2, :] += jnp.where(lane1 < n_heads, _colsum(dda * dtv) * a, 0.0)
        misc_ref[2:3, :] += dd
        for (du_s, pre, sg, ext_s, e2_s, nx_s, w_ref, dcw_ref, dcb_ref, c0, width) in (
                (dux_s, prex, sgx, extx_s, e2x_s, nxx_s, cwx_ref, dcwx_ref, dcbx_ref, 0, D),
                (dub_s, preb, sgb, extb_s, e2b_s, nxb_s, cwb_ref, dcwb_ref, dcbb_ref, D, 512)):
            dpre = du_s[...] * (sg * (1.0 + pre * (1.0 - sg)))
            dcb_ref[...] += _colsum(dpre)
            for i in range(CONV_K):
                dcw_ref[i:i + 1, :] += _colsum(dpre * ext_s[pl.ds(HALO - (CONV_K - 1 - i), BLK), :])
            e2_s[0:BLK, :] = dpre
            e2_s[BLK:BLK + HALO, :] = nx_s[...]
            dxr = jnp.zeros((BLK, width), F32)
            for i in range(CONV_K):
                dxr = dxr + e2_s[pl.ds(CONV_K - 1 - i, BLK), :] * w_ref[i:i + 1, :]
            dxbc_ref[0, :, c0:c0 + width] = dxr.astype(BF16)
            nx_s[...] = e2_s[0:HALO, :]

    full = lambda shape: pl.BlockSpec(shape, lambda b, c: (0,) * len(shape))
    return pl.pallas_call(
        body,
        name="ssd_bwd",
        grid=(Bl, nc),
        in_specs=specs,
        out_specs=[
            pl.BlockSpec((1, BLK, tail), lambda b, c: (b, cidx(c), 0)),
            full((CONV_K, D)), full((CONV_K, 512)), full((1, D)), full((1, 512)), full((8, LANES)),
        ],
        out_shape=[
            jax.ShapeDtypeStruct((Bl, L, tail), BF16),
            jax.ShapeDtypeStruct((CONV_K, D), F32), jax.ShapeDtypeStruct((CONV_K, 512), F32),
            jax.ShapeDtypeStruct((1, D), F32), jax.ShapeDtypeStruct((1, 512), F32),
            jax.ShapeDtypeStruct((8, LANES), F32),
        ],
        scratch_shapes=[
            pltpu.VMEM((n_pair, SSD_STATE, PAIR), F32),
            pltpu.VMEM((HALO + BLK, D), F32), pltpu.VMEM((HALO + BLK, 512), F32),
            pltpu.VMEM((BLK, LANES), F32), pltpu.VMEM((LANES, BLK), F32),
            pltpu.VMEM((BLK, D), F32), pltpu.VMEM((BLK, 512), F32),
            pltpu.VMEM((BLK + HALO, D), F32), pltpu.VMEM((BLK + HALO, 512), F32),
            pltpu.VMEM((HALO, D), F32), pltpu.VMEM((HALO, 512), F32),
        ],
        compiler_params=_params(("arbitrary", "arbitrary")),
    )(proj3, proj3, proj3, proj3, proj3, cwx, cwb, cbx, cbb, dtb, alog, dsk, s_in, dy3)


def _gate_out(x2, tgt2, o2, proj2, y2, sbw, ssw, w_out_bf):
    T, D = x2.shape
    tm = min(256, T)

    def body(x_ref, t_ref, o_ref, zs_ref, y_ref, zy_ref, sbw_ref, ssw_ref, wo_ref,
             dout_ref, doutb_ref, mixt_ref, do_ref, dy_ref, dz_ref, dnw_ref, loss_ref):
        @pl.when(pl.program_id(0) == 0)
        def _():
            dnw_ref[...] = jnp.zeros_like(dnw_ref)
            loss_ref[...] = jnp.zeros_like(loss_ref)

        def fwd(o, z, w):
            sg = _sigmoid(z)
            sl = z * sg
            g = o * sl
            r = lax.rsqrt(jnp.mean(g * g, axis=-1, keepdims=True) + EPS)
            n = g * r
            return sg, sl, r, n, n * w

        def bwd(dy, o, z, w, sg, sl, r, n):
            dn = dy * w
            dg = r * (dn - n * jnp.mean(dn * n, axis=-1, keepdims=True))
            return dg * sl, dg * o * (sg * (1.0 + z * (1.0 - sg))), _colsum(dy * n)

        o1, z1, w1 = o_ref[...], zs_ref[...], sbw_ref[...]
        o2_, z2, w2 = y_ref[...], zy_ref[...], ssw_ref[...]
        sg1, sl1, r1, n1, y1 = fwd(o1, z1, w1)
        sg2, sl2, r2, n2, y2_ = fwd(o2_, z2, w2)
        y1b, y2b = y1.astype(BF16), y2_.astype(BF16)
        mixt_ref[0:D, :] = y1.T.astype(BF16)
        mixt_ref[D:2 * D, :] = y2_.T.astype(BF16)
        out = x_ref[...] + (_dot(y1b, wo_ref[0:D, :]) + _dot(y2b, wo_ref[D:2 * D, :]))
        err = out - t_ref[...]
        loss_ref[...] += jnp.sum(err * err) * (0.5 / D)
        dout = err * (1.0 / D)
        dout_ref[...] = dout
        doutb = dout.astype(BF16)
        doutb_ref[...] = doutb
        do1, dz1, dw1 = bwd(_dot_nt(doutb, wo_ref[0:D, :]), o1, z1, w1, sg1, sl1, r1, n1)
        do2, dz2, dw2 = bwd(_dot_nt(doutb, wo_ref[D:2 * D, :]), o2_, z2, w2, sg2, sl2, r2, n2)
        do_ref[...] = do1
        dy_ref[...] = do2
        dz_ref[:, 0:D] = dz1.astype(BF16)
        dz_ref[:, D:2 * D] = dz2.astype(BF16)
        dnw_ref[0:1, :] += dw1
        dnw_ref[1:2, :] += dw2

    row = lambda col: pl.BlockSpec((tm, D), lambda i: (i, col))
    full = lambda shape: pl.BlockSpec(shape, lambda i: (0,) * len(shape))
    wide = pl.BlockSpec((tm, 2 * D), lambda i: (i, 0))
    return pl.pallas_call(
        body,
        name="gate_out",
        grid=(T // tm,),
        in_specs=[row(0), row(0), row(0), row(3), row(0), row(4), full((1, D)), full((1, D)), full((2 * D, D))],
        out_specs=[row(0), row(0), pl.BlockSpec((2 * D, tm), lambda i: (0, i)), row(0), row(0), wide,
                   full((8, D)), full((8, LANES))],
        out_shape=[
            jax.ShapeDtypeStruct((T, D), F32), jax.ShapeDtypeStruct((T, D), BF16),
            jax.ShapeDtypeStruct((2 * D, T), BF16), jax.ShapeDtypeStruct((T, D), F32),
            jax.ShapeDtypeStruct((T, D), F32), jax.ShapeDtypeStruct((T, 2 * D), BF16),
            jax.ShapeDtypeStruct((8, D), F32), jax.ShapeDtypeStruct((8, LANES), F32),
        ],
        compiler_params=_params(("arbitrary",)),
    )(x2, tgt2, o2, proj2, y2, proj2, sbw, ssw, w_out_bf)


def _piece_blocks(pieces, D):
    counts = [p.shape[1] // D for p in pieces]
    return [sum(counts[:i]) for i in range(len(counts))], counts


def _dhn(pieces, w_pad, x2, dout, norm_w):
    T, D = x2.shape
    tm = min(512, T)
    starts, counts = _piece_blocks(pieces, D)
    nk = sum(counts)
    assert nk * D == w_pad.shape[1]

    def body(*refs):
        p_refs = refs[:len(pieces)]
        w_ref, x_ref, dout_ref, nw_ref, gx_ref, dnw_ref, acc_s = refs[len(pieces):]
        k = pl.program_id(1)

        @pl.when((pl.program_id(0) == 0) & (k == 0))
        def _():
            dnw_ref[...] = jnp.zeros_like(dnw_ref)

        @pl.when(k == 0)
        def _():
            acc_s[...] = jnp.zeros_like(acc_s)

        for p_ref, s, n in zip(p_refs, starts, counts):
            @pl.when((k >= s) & (k < s + n))
            def _(p_ref=p_ref):
                acc_s[...] += _dot_nt(p_ref[...], w_ref[...])

        @pl.when(k == nk - 1)
        def _():
            xv = x_ref[...]
            r = lax.rsqrt(jnp.mean(xv * xv, axis=-1, keepdims=True) + EPS)
            xh = xv * r
            dhn = acc_s[...]
            dxh = dhn * nw_ref[...]
            gx_ref[...] = dout_ref[...] + r * (dxh - xh * jnp.mean(dxh * xh, axis=-1, keepdims=True))
            dnw_ref[0:1, :] += _colsum(dhn * xh)

    return pl.pallas_call(
        body,
        name="dhn",
        grid=(T // tm, nk),
        in_specs=[pl.BlockSpec((tm, D), lambda i, k, s=s, n=n: (i, jnp.clip(k - s, 0, n - 1)))
                  for s, n in zip(starts, counts)] + [
            pl.BlockSpec((D, D), lambda i, k: (0, k)),
            pl.BlockSpec((tm, D), lambda i, k: (i, 0)),
            pl.BlockSpec((tm, D), lambda i, k: (i, 0)),
            pl.BlockSpec((1, D), lambda i, k: (0, 0)),
        ],
        out_specs=[pl.BlockSpec((tm, D), lambda i, k: (i, 0)), pl.BlockSpec((8, D), lambda i, k: (0, 0))],
        out_shape=[jax.ShapeDtypeStruct((T, D), F32), jax.ShapeDtypeStruct((8, D), F32)],
        scratch_shapes=[pltpu.VMEM((tm, D), F32)],
        compiler_params=_params(("arbitrary", "arbitrary")),
    )(*pieces, w_pad, x2, dout, norm_w)


def _grad_w_in(hn_t, pieces):
    D, T = hn_t.shape
    tk = min(512, T)
    starts, counts = _piece_blocks(pieces, D)

    def body(*refs):
        a_ref, p_refs, o_ref = refs[0], refs[1:-1], refs[-1]
        j = pl.program_id(0)

        @pl.when(pl.program_id(1) == 0)
        def _():
            o_ref[...] = jnp.zeros_like(o_ref)

        for p_ref, s, n in zip(p_refs, starts, counts):
            @pl.when((j >= s) & (j < s + n))
            def _(p_ref=p_ref):
                o_ref[...] += _dot(a_ref[...], p_ref[...])

    def piece_spec(s, n):
        return pl.BlockSpec((tk, D), lambda j, k: (jnp.where((j >= s) & (j < s + n), k, 0), jnp.clip(j - s, 0, n - 1)))

    return pl.pallas_call(
        body,
        name="grad_w_in",
        grid=(sum(counts), T // tk),
        in_specs=[pl.BlockSpec((D, tk), lambda j, k: (0, k))] + [piece_spec(s, n) for s, n in zip(starts, counts)],
        out_specs=pl.BlockSpec((D, D), lambda j, k: (0, j)),
        out_shape=jax.ShapeDtypeStruct((D, sum(counts) * D), F32),
        compiler_params=_params(("parallel", "arbitrary")),
    )(hn_t, *pieces)


def _matmul(a, b, name):
    M, K = a.shape
    N = b.shape[1]
    tm = min(1024, M)
    tn = 1024 if N % 1024 == 0 else (512 if N % 512 == 0 else N)
    tk = min(512, K)

    def body(a_ref, b_ref, o_ref):
        @pl.when(pl.program_id(2) == 0)
        def _():
            o_ref[...] = jnp.zeros_like(o_ref)

        o_ref[...] += _dot(a_ref[...], b_ref[...])

    return pl.pallas_call(
        body,
        name=name,
        grid=(M // tm, N // tn, K // tk),
        in_specs=[pl.BlockSpec((tm, tk), lambda i, j, k: (i, k)), pl.BlockSpec((tk, tn), lambda i, j, k: (k, j))],
        out_specs=pl.BlockSpec((tm, tn), lambda i, j, k: (i, j)),
        out_shape=jax.ShapeDtypeStruct((M, N), F32),
        compiler_params=_params(("parallel", "parallel", "arbitrary")),
    )(a, b)


def _adamw(w, g, m, v, name):
    R, C = w.shape
    tr = 256 if R % 256 == 0 else R
    c1 = 1.0 - ADAM_B1 ** ADAM_STEP
    c2 = 1.0 - ADAM_B2 ** ADAM_STEP

    def body(w_ref, g_ref, m_ref, v_ref, d_ref, nm_ref, nv_ref):
        gv = g_ref[...]
        m_new = ADAM_B1 * m_ref[...] + (1.0 - ADAM_B1) * gv
        v_new = ADAM_B2 * v_ref[...] + (1.0 - ADAM_B2) * (gv * gv)
        d_ref[...] = -ADAM_LR * ((m_new / c1) / (jnp.sqrt(v_new / c2) + ADAM_EPS) + ADAM_WD * w_ref[...])
        nm_ref[...] = m_new
        nv_ref[...] = v_new

    spec = pl.BlockSpec((tr, C), lambda i: (i, 0))
    return pl.pallas_call(
        body,
        name=name,
        grid=(R // tr,),
        in_specs=[spec] * 4,
        out_specs=[spec] * 3,
        out_shape=[jax.ShapeDtypeStruct((R, C), F32)] * 3,
        compiler_params=_params(("parallel",)),
    )(w, g, m, v)


def _add_core_half(a, recv, core, name):
    _, n, h, S = a.shape
    th = 256 if h % 256 == 0 else h

    def body(c_ref, a_ref, r_ref, o_ref):
        o_ref[...] = a_ref[...] + r_ref[...]

    return pl.pallas_call(
        body,
        name=name,
        grid_spec=pltpu.PrefetchScalarGridSpec(
            num_scalar_prefetch=1,
            grid=(n, h // th),
            in_specs=[
                pl.BlockSpec((None, None, th, S), lambda p, i, c: (c[0], p, i, 0)),
                pl.BlockSpec((None, th, S), lambda p, i, c: (p, i, 0)),
            ],
            out_specs=pl.BlockSpec((None, th, S), lambda p, i, c: (p, i, 0)),
        ),
        out_shape=jax.ShapeDtypeStruct((n, h, S), F32),
        compiler_params=_params(("parallel", "parallel")),
    )(core, a, recv)


def _add_chips(hsum, recv, chip, name):
    _, h, S = hsum.shape
    th = 256 if h % 256 == 0 else h

    def body(c_ref, a_ref, r_ref, o_ref):
        o_ref[...] = ((a_ref[...] + r_ref[0]) + r_ref[1]) + r_ref[2]

    return pl.pallas_call(
        body,
        name=name,
        grid_spec=pltpu.PrefetchScalarGridSpec(
            num_scalar_prefetch=1,
            grid=(h // th,),
            in_specs=[
                pl.BlockSpec((None, th, S), lambda i, c: (c[0], i, 0)),
                pl.BlockSpec((N_CHIPS - 1, th, S), lambda i, c: (0, i, 0)),
            ],
            out_specs=pl.BlockSpec((th, S), lambda i, c: (i, 0)),
        ),
        out_shape=jax.ShapeDtypeStruct((h, S), F32),
        compiler_params=_params(("parallel",)),
    )(chip, hsum, recv)


def _place():
    x, y, c = lax.axis_index("x"), lax.axis_index("y"), lax.axis_index("c")
    other_chips = [(1 - x, y), (x, 1 - y), (1 - x, 1 - y)]
    return x, y, c, other_chips


def _allgather_weights(w_in_bf, w_out_bf, conv_w):
    D, S = w_in_bf.shape
    R = w_out_bf.shape[0]
    n_ici, n_fwd = 3 * (N_CHIPS - 1), 2 * (N_CHIPS - 1)

    def body(win, wout, cw, gin, gout, gcw, send_sems, recv_sems):
        x, y, c, chips = _place()
        me = 2 * x + y
        sibling = (x, y, 1 - c)
        hin, hout = D // 2, R // 2

        def halves(chip_idx):
            return (gin.at[chip_idx, pl.ds(c * hin, hin)], gout.at[chip_idx, pl.ds(c * hout, hout)])

        def rcopy(k, src, dst, to):
            return pltpu.make_async_remote_copy(src_ref=src, dst_ref=dst, send_sem=send_sems.at[k],
                                                recv_sem=recv_sems.at[k], device_id=to, device_id_type=MESH)

        my_in, my_out = halves(me)
        src_in, src_out = win.at[pl.ds(c * hin, hin)], wout.at[pl.ds(c * hout, hout)]
        sends = []
        for j, chip in enumerate(chips):
            to = (*chip, c)
            sends += [rcopy(3 * j, src_in, my_in, to), rcopy(3 * j + 1, src_out, my_out, to),
                      rcopy(3 * j + 2, cw, gcw.at[me], to)]
        for cp in sends:
            cp.start()
        passed = []
        for j, (px, py) in enumerate(chips):
            their_in, their_out = halves(2 * px + py)
            rcopy(3 * j, their_in, their_in, sibling).wait_recv()
            rcopy(3 * j + 1, their_out, their_out, sibling).wait_recv()
            rcopy(3 * j + 2, cw, gcw.at[2 * px + py], sibling).wait_recv()
            fw = [rcopy(n_ici + 2 * j, their_in, their_in, sibling), rcopy(n_ici + 2 * j + 1, their_out, their_out, sibling)]
            for cp in fw:
                cp.start()
            passed += fw
        for j, (px, py) in enumerate(chips):
            oin = gin.at[2 * px + py, pl.ds((1 - c) * hin, hin)]
            oout = gout.at[2 * px + py, pl.ds((1 - c) * hout, hout)]
            rcopy(n_ici + 2 * j, oin, oin, sibling).wait_recv()
            rcopy(n_ici + 2 * j + 1, oout, oout, sibling).wait_recv()
        for cp in sends + passed:
            cp.wait_send()

    return pl.pallas_call(
        body,
        name="allgather_weights",
        in_specs=[ANY, ANY, ANY],
        out_specs=[ANY, ANY, ANY],
        out_shape=[jax.ShapeDtypeStruct((N_CHIPS, D, S), BF16), jax.ShapeDtypeStruct((N_CHIPS, R, D), BF16),
                   jax.ShapeDtypeStruct((N_CHIPS,) + conv_w.shape, F32)],
        scratch_shapes=[pltpu.SemaphoreType.DMA((n_ici + n_fwd,)), pltpu.SemaphoreType.DMA((n_ici + n_fwd,))],
    )(w_in_bf, w_out_bf, conv_w)


def _allreduce_small(packed):
    R = packed.shape[0]
    n_dev = 2 * N_CHIPS

    def body(p_ref, o_ref, buf, send_sems, recv_sems):
        x, y, c, _ = _place()
        me = 4 * x + 2 * y + c
        buf[me] = p_ref[...]
        copies = []
        for k in range(1, n_dev):
            px = 1 - x if k & 4 else x
            py = 1 - y if k & 2 else y
            pc = 1 - c if k & 1 else c
            copies.append((pltpu.make_async_remote_copy(
                src_ref=buf.at[me], dst_ref=buf.at[me], send_sem=send_sems.at[k - 1], recv_sem=recv_sems.at[k - 1],
                device_id=(px, py, pc), device_id_type=MESH), 4 * px + 2 * py + pc, (px, py, pc)))
        for cp, _, _ in copies:
            cp.start()
        for k, (_, peer, to) in enumerate(copies):
            pltpu.make_async_remote_copy(
                src_ref=buf.at[peer], dst_ref=buf.at[peer], send_sem=send_sems.at[k], recv_sem=recv_sems.at[k],
                device_id=to, device_id_type=MESH).wait_recv()
        for cp, _, _ in copies:
            cp.wait_send()
        acc = buf[0]
        for d in range(1, n_dev):
            acc = acc + buf[d]
        o_ref[...] = acc

    vm = pl.BlockSpec(memory_space=pltpu.VMEM)
    return pl.pallas_call(
        body,
        name="allreduce_small",
        in_specs=[vm],
        out_specs=vm,
        out_shape=jax.ShapeDtypeStruct((R, LANES), F32),
        scratch_shapes=[pltpu.VMEM((n_dev, R, LANES), F32), pltpu.SemaphoreType.DMA((n_dev - 1,)),
                        pltpu.SemaphoreType.DMA((n_dev - 1,))],
    )(packed)


def _swap_core_halves(a_in, a_out):
    def body(ain, aout, rin, rout, send_sems, recv_sems):
        x, y, c, _ = _place()
        cps = [pltpu.make_async_remote_copy(src_ref=s.at[1 - c], dst_ref=d, send_sem=send_sems.at[k],
                                            recv_sem=recv_sems.at[k], device_id=(x, y, 1 - c), device_id_type=MESH)
               for k, (s, d) in enumerate(((ain, rin), (aout, rout)))]
        for cp in cps:
            cp.start()
        for cp in cps:
            cp.wait()

    return pl.pallas_call(
        body,
        name="reduce_core_swap",
        in_specs=[ANY, ANY],
        out_specs=[ANY, ANY],
        out_shape=[jax.ShapeDtypeStruct(a_in.shape[1:], F32), jax.ShapeDtypeStruct(a_out.shape[1:], F32)],
        scratch_shapes=[pltpu.SemaphoreType.DMA((2,)), pltpu.SemaphoreType.DMA((2,))],
    )(a_in, a_out)


def _scatter_to_chips(h_in, h_out):
    def body(hin, hout, rin, rout, send_sems, recv_sems):
        x, y, c, chips = _place()
        cps = []
        for j, (px, py) in enumerate(chips):
            for k, (s, d) in enumerate(((hin, rin), (hout, rout))):
                cps.append(pltpu.make_async_remote_copy(
                    src_ref=s.at[2 * px + py], dst_ref=d.at[j], send_sem=send_sems.at[2 * j + k],
                    recv_sem=recv_sems.at[2 * j + k], device_id=(px, py, c), device_id_type=MESH))
        for cp in cps:
            cp.start()
        for cp in cps:
            cp.wait()

    n = 2 * (N_CHIPS - 1)
    return pl.pallas_call(
        body,
        name="reduce_chip_scatter",
        in_specs=[ANY, ANY],
        out_specs=[ANY, ANY],
        out_shape=[jax.ShapeDtypeStruct((N_CHIPS - 1,) + h_in.shape[1:], F32),
                   jax.ShapeDtypeStruct((N_CHIPS - 1,) + h_out.shape[1:], F32)],
        scratch_shapes=[pltpu.SemaphoreType.DMA((n,)), pltpu.SemaphoreType.DMA((n,))],
    )(h_in, h_out)


def _join_core_halves(g_in, g_out):
    def body(gin, gout, fin, fout, send_sems, recv_sems):
        x, y, c, _ = _place()
        cps = [pltpu.make_async_remote_copy(src_ref=s, dst_ref=d.at[c], send_sem=send_sems.at[k],
                                            recv_sem=recv_sems.at[k], device_id=(x, y, 1 - c), device_id_type=MESH)
               for k, (s, d) in enumerate(((gin, fin), (gout, fout)))]
        for cp in cps:
            cp.start()
        for k, (s, d) in enumerate(((gin, fin), (gout, fout))):
            pltpu.make_async_remote_copy(src_ref=s, dst_ref=d.at[1 - c], send_sem=send_sems.at[k],
                                         recv_sem=recv_sems.at[k], device_id=(x, y, 1 - c),
                                         device_id_type=MESH).wait_recv()
        for cp in cps:
            cp.wait_send()

    return pl.pallas_call(
        body,
        name="reduce_core_join",
        in_specs=[ANY, ANY],
        out_specs=[ANY, ANY],
        out_shape=[jax.ShapeDtypeStruct((2,) + g_in.shape, F32), jax.ShapeDtypeStruct((2,) + g_out.shape, F32)],
        scratch_shapes=[pltpu.SemaphoreType.DMA((2,)), pltpu.SemaphoreType.DMA((2,))],
    )(g_in, g_out)


def _pack(arrays):
    rows = []
    for a in arrays:
        flat = a.reshape(-1).astype(F32)
        n = -(-flat.shape[0] // LANES) * LANES
        rows.append(jnp.pad(flat, (0, n - flat.shape[0])).reshape(-1, LANES))
    out = jnp.concatenate(rows, axis=0)
    return jnp.pad(out, ((0, -out.shape[0] % 8), (0, 0)))


def _unpack(packed, shapes):
    out, r = [], 0
    for shp in shapes:
        n = math.prod(shp)
        nr = -(-n // LANES)
        out.append(packed[r:r + nr].reshape(-1)[:n].reshape(shp))
        r += nr
    return out


def _pad_lanes(a):
    return jnp.pad(a, ((0, 0), (0, LANES - a.shape[1])))


def kernel(x, norm_w, w_in, q_norm_w, k_norm_w, conv_w, conv_b, dt_bias, A_log, D_skip, sb_norm_w, ssd_norm_w, w_out, loss_target, m_norm_w, m_w_in, m_q_norm_w, m_k_norm_w, m_conv_w, m_conv_b, m_dt_bias, m_A_log, m_D_skip, m_sb_norm_w, m_ssd_norm_w, m_w_out, v_norm_w, v_w_in, v_q_norm_w, v_k_norm_w, v_conv_w, v_conv_b, v_dt_bias, v_A_log, v_D_skip, v_sb_norm_w, v_ssd_norm_w, v_w_out):
    Bl, L, D = x.shape
    T = Bl * L
    S = w_in.shape[2]
    R = w_out.shape[1]
    CW = conv_w.shape[2]
    n_in = N_CHIPS * S
    CD = D + 2 * SSD_GROUPS * SSD_STATE
    H = D // HEAD_DIM
    n_main = 6 * D + 512
    P = -(-(n_main + LANES) // 1024) * 1024
    assert n_in == n_main + H and CD == N_CHIPS * CW and 2 * D == N_CHIPS * R and CD == D + 512
    chip = (2 * lax.axis_index("x") + lax.axis_index("y")).astype(jnp.int32)
    core = lax.axis_index("c").astype(jnp.int32)

    w_in_bf, w_out_shard_bf = w_in[0].astype(BF16), w_out[0].astype(BF16)
    g_in, g_out, g_cw = _allgather_weights(w_in_bf, w_out_shard_bf, conv_w[0])
    g_in = lax.dynamic_update_slice(g_in, w_in_bf[None], (chip, 0, 0))
    g_out = lax.dynamic_update_slice(g_out, w_out_shard_bf[None], (chip, 0, 0))
    g_cw = lax.dynamic_update_slice(g_cw, conv_w, (chip, 0, 0))
    w_pad = jnp.pad(g_in.transpose(1, 0, 2).reshape(D, n_in), ((0, 0), (0, P - n_in)))
    w_out_bf = g_out.reshape(2 * D, D)
    conv_full = g_cw.transpose(1, 0, 2).reshape(CONV_K, CD)
    cwx, cwb = conv_full[:, :D], conv_full[:, D:]
    cbx, cbb = conv_b[:, :D], conv_b[:, D:]
    dtb, alog, dsk = _pad_lanes(dt_bias), _pad_lanes(A_log), _pad_lanes(D_skip)
    qw2, kw2 = jnp.tile(q_norm_w, (1, 2)), jnp.tile(k_norm_w, (1, 2))

    x2 = x.reshape(T, D)
    proj, hn_t = _inproj(x2, norm_w, w_pad)
    proj3 = proj.reshape(Bl, L, P)
    o_sb = _attn_fwd(proj3, qw2, kw2, D)
    y_ssd, s_in = _ssd_fwd(proj3, cwx, cwb, cbx, cbb, dtb, alog, dsk, D)
    dout, dout_bf, mixed_t, do_sb, dy_ssd, dz_bf, dnw_out, loss_blk = _gate_out(
        x2, loss_target.reshape(T, D), o_sb.reshape(T, D), proj, y_ssd.reshape(T, D), sb_norm_w, ssd_norm_w, w_out_bf)

    dq, dk, dv, dqkw = _attn_bwd(proj3, o_sb, do_sb.reshape(Bl, L, D), qw2, kw2, D)
    dtail, dcwx, dcwb, dcbx, dcbb, misc = _ssd_bwd(
        proj3, s_in, dy_ssd.reshape(Bl, L, D), cwx, cwb, cbx, cbb, dtb, alog, dsk, D, P - 5 * D)
    dproj = [dq.reshape(T, D), dk.reshape(T, D), dv.reshape(T, D), dz_bf, dtail.reshape(T, P - 5 * D)]
    grad_x2, dnw_in = _dhn(dproj, w_pad, x2, dout, norm_w)
    gw_in = _grad_w_in(hn_t, dproj)[:, :n_in]
    gw_out = _matmul(mixed_t, dout_bf, "grad_w_out")

    small_shapes = [(1, D), (1, D), (1, D), (1, CD), (1, HEAD_DIM), (1, HEAD_DIM), (1, H), (1, H), (1, H)]
    g_small_local = [dnw_in[0:1], dnw_out[0:1], dnw_out[1:2], jnp.concatenate([dcbx, dcbb], axis=1),
                     dqkw[0:1, :HEAD_DIM] + dqkw[0:1, HEAD_DIM:], dqkw[1:2, :HEAD_DIM] + dqkw[1:2, HEAD_DIM:],
                     misc[0:1, :H], misc[1:2, :H], misc[2:3, :H]]
    packed = _pack(g_small_local + [jnp.concatenate([dcwx, dcwb], axis=1), loss_blk[0:1, 0:1]])
    red = _allreduce_small(packed)
    g_small = _unpack(red, small_shapes + [(CONV_K, CD), (1, 1)])
    g_conv_w = lax.dynamic_slice_in_dim(g_small[9], chip * CW, CW, axis=1)
    loss = g_small[10][0, 0]

    a_in = gw_in.reshape(2, D // 2, N_CHIPS, S).transpose(0, 2, 1, 3)
    a_out = gw_out.reshape(N_CHIPS, 2, R // 2, D).transpose(1, 0, 2, 3)
    r_in, r_out = _swap_core_halves(a_in, a_out)
    core1, chip1 = core.reshape(1), chip.reshape(1)
    h_in = _add_core_half(a_in, r_in, core1, "sum_cores_w_in")
    h_out = _add_core_half(a_out, r_out, core1, "sum_cores_w_out")
    s_in_, s_out_ = _scatter_to_chips(h_in, h_out)
    gh_in = _add_chips(h_in, s_in_, chip1, "sum_chips_w_in")
    gh_out = _add_chips(h_out, s_out_, chip1, "sum_chips_w_out")
    f_in, f_out = _join_core_halves(gh_in, gh_out)
    g_w_in = lax.dynamic_update_slice(f_in, gh_in[None], (core, 0, 0)).reshape(D, S)
    g_w_out = lax.dynamic_update_slice(f_out, gh_out[None], (core, 0, 0)).reshape(R, D)

    d_in, nm_in, nv_in = _adamw(w_in[0], g_w_in, m_w_in[0], v_w_in[0], "adamw_w_in")
    d_out, nm_out, nv_out = _adamw(w_out[0], g_w_out, m_w_out[0], v_w_out[0], "adamw_w_out")
    d_cw, nm_cw, nv_cw = _adamw(conv_w[0], g_conv_w, m_conv_w[0], v_conv_w[0], "adamw_conv_w")
    small_w = [norm_w, sb_norm_w, ssd_norm_w, conv_b, q_norm_w, k_norm_w, dt_bias, A_log, D_skip]
    small_m = [m_norm_w, m_sb_norm_w, m_ssd_norm_w, m_conv_b, m_q_norm_w, m_k_norm_w, m_dt_bias, m_A_log, m_D_skip]
    small_v = [v_norm_w, v_sb_norm_w, v_ssd_norm_w, v_conv_b, v_q_norm_w, v_k_norm_w, v_dt_bias, v_A_log, v_D_skip]
    d_s, nm_s, nv_s = _adamw(_pack(small_w), _pack(g_small[:9]), _pack(small_m), _pack(small_v), "adamw_small")
    d_s, nm_s, nv_s = (_unpack(t, small_shapes) for t in (d_s, nm_s, nv_s))

    def ordered(s, w_in_, conv_w_, w_out_):
        return [s[0], w_in_[None], s[4], s[5], conv_w_[None], s[3], s[6], s[7], s[8], s[1], s[2], w_out_[None]]

    return (loss, grad_x2.reshape(Bl, L, D),
            *ordered(g_small[:9], g_w_in, g_conv_w, g_w_out),
            *ordered(d_s, d_in, d_cw, d_out),
            *ordered(nm_s, nm_in, nm_cw, nm_out),
            *ordered(nv_s, nv_in, nv_cw, nv_out))
```

```python
import functools
import math

import jax
import jax.numpy as jnp
from jax import lax
from jax.experimental import pallas as pl
from jax.experimental.pallas import tpu as pltpu

F32 = jnp.float32
BF16 = jnp.bfloat16
EPS = 1e-6
HEAD_DIM = 64
PAIR = 2 * HEAD_DIM
LANES = 128
SSD_STATE = 128
SSD_GROUPS = 2
BLK = 128
PREP_BLOCKS = 4
FIRST_LEFT = 2
UNDERFLOW = -105.0
CONV_K = 4
HALO = 8
N_CHIPS = 4
ADAM_LR, ADAM_B1, ADAM_B2, ADAM_EPS, ADAM_WD, ADAM_STEP = 0.001, 0.9, 0.999, 1e-08, 0.01, 10
VMEM_LIMIT_V7X = 56 * 1024 * 1024
MESH = pl.DeviceIdType.MESH
ANY = pl.BlockSpec(memory_space=pl.ANY)
NT = (((1,), (1,)), ((), ()))


def _params(sem=None):
    kw = dict(vmem_limit_bytes=VMEM_LIMIT_V7X)
    if sem is not None:
        kw["dimension_semantics"] = sem
    return pltpu.CompilerParams(**kw)


def _dot(a, b):
    return jnp.dot(a, b, preferred_element_type=F32)


def _dot_nt(a, b):
    return lax.dot_general(a, b, NT, preferred_element_type=F32)


def _dot_split(m, x):
    hi = x.astype(BF16)
    lo = (x - hi.astype(F32)).astype(BF16)
    return _dot(m, hi) + _dot(m, lo)


def _iota(shape, dim):
    return lax.broadcasted_iota(jnp.int32, shape, dim)


def _rowsum(x):
    return jnp.sum(x, axis=1, keepdims=True)


def _colsum(x):
    return jnp.sum(x, axis=0, keepdims=True)


def _sigmoid(x):
    return 1.0 / (1.0 + jnp.exp(-x))


def _softplus(x):
    return jnp.maximum(x, 0.0) + jnp.log(1.0 + jnp.exp(-jnp.abs(x)))


def _inproj(x2, norm_w, w_pad):
    T, D = x2.shape
    P = w_pad.shape[1]
    tm = min(1024, T)
    tn = 1024 if P % 1024 == 0 else 512

    def body(x_ref, nw_ref, w_ref, proj_ref, hnt_ref, hn_s):
        @pl.when(pl.program_id(1) == 0)
        def _():
            xv = x_ref[...]
            r = lax.rsqrt(jnp.mean(xv * xv, axis=-1, keepdims=True) + EPS)
            hn = xv * r * nw_ref[...]
            hn_s[...] = hn.astype(BF16)
            hnt_ref[...] = hn.T.astype(BF16)

        proj_ref[...] = _dot(hn_s[...], w_ref[...])

    return pl.pallas_call(
        body,
        name="inproj",
        grid=(T // tm, P // tn),
        in_specs=[
            pl.BlockSpec((tm, D), lambda i, j: (i, 0)),
            pl.BlockSpec((1, D), lambda i, j: (0, 0)),
            pl.BlockSpec((D, tn), lambda i, j: (0, j)),
        ],
        out_specs=[
            pl.BlockSpec((tm, tn), lambda i, j: (i, j)),
            pl.BlockSpec((D, tm), lambda i, j: (0, i)),
        ],
        out_shape=[jax.ShapeDtypeStruct((T, P), F32), jax.ShapeDtypeStruct((D, T), BF16)],
        scratch_shapes=[pltpu.VMEM((tm, D), BF16)],
        compiler_params=_params(("parallel", "arbitrary")),
    )(x2, norm_w, w_pad)


def _pair_ones():
    ri = (_iota((PAIR, PAIR), 0) >= HEAD_DIM).astype(jnp.int32)
    ci = (_iota((PAIR, PAIR), 1) >= HEAD_DIM).astype(jnp.int32)
    return jnp.where(ri == ci, 1.0, 0.0).astype(BF16)


def _pair_rms(v, ones2):
    return lax.rsqrt(_split_dots([v * v], ones2)[0] * (1.0 / HEAD_DIM) + EPS)


def _pair_mean(v, ones2):
    return _split_dots([v], ones2)[0] * (1.0 / HEAD_DIM)


def _suffix_ones():
    ri = _iota((BLK, 2 * BLK), 0)
    ci = _iota((BLK, 2 * BLK), 1)
    return jnp.where((ci >= BLK) | (ri > ci), 1.0, 0.0).astype(BF16)


def _split_dots(xs, m):
    his = [x.astype(BF16) for x in xs]
    los = [(x - hi.astype(F32)).astype(BF16) for x, hi in zip(xs, his)]
    return [_dot(hi, m) + _dot(lo, m) for hi, lo in zip(his, los)]


def _sb_tiles(qms, km_s, c0s, rest, uo, masks):
    tiles = [(u, h) for u in range(len(c0s)) for h in range(2)]
    zs = [_dot_nt(qms[h], km_s[h, pl.ds(c0s[u], BLK), :]) for u, h in tiles]
    es = [jnp.exp(-jnp.abs(z)) for z in zs]
    las = [jnp.minimum(z, 0.0) - jnp.log(1.0 + e) for z, e in zip(zs, es)]
    lks = [a - z for a, z in zip(las, zs)]
    lks = [lk if masks[u] is None else jnp.where(masks[u], lk, 0.0) for (u, h), lk in zip(tiles, lks)]
    css = _split_dots(lks, uo)
    rest = list(rest)
    ws = []
    for (u, h), a, cs in zip(tiles, las, css):
        w = jnp.exp(a + rest[h] + cs[:, :BLK])
        ws.append(w if masks[u] is None else jnp.where(masks[u], w, 0.0))
        rest[h] = rest[h] + cs[:, BLK:]
    return tiles, zs, es, ws, rest


def _first_blocks(qi, diag):
    c0s = [pl.multiple_of(jnp.maximum(qi - u, 0) * BLK, BLK) for u in range(1 + FIRST_LEFT)]
    masks = [diag] + [jnp.broadcast_to(qi - u >= 0, (BLK, BLK)) for u in range(1, 1 + FIRST_LEFT)]
    return c0s, masks


def _attn_prep(src_ref, w_ref, dst_s, n_blocks, scale):
    per = math.gcd(PREP_BLOCKS, n_blocks)
    rows = per * BLK
    lo = _iota((rows, PAIR), 1) < HEAD_DIM
    ones2 = _pair_ones()

    def step(i, carry):
        r0 = pl.multiple_of(i * rows, rows)
        v = src_ref[0, pl.ds(r0, rows), :]
        if w_ref is not None:
            v = v * _pair_rms(v, ones2) * w_ref[...]
        if scale != 1.0:
            v = v * scale
        dst_s[0, pl.ds(r0, rows), :] = jnp.where(lo, v, 0.0).astype(BF16)
        dst_s[1, pl.ds(r0, rows), :] = jnp.where(lo, 0.0, v).astype(BF16)
        return carry

    lax.fori_loop(0, n_blocks // per, step, 0)


def _attn_fwd(proj3, qw2, kw2, D):
    Bl, L, _ = proj3.shape
    n_pair = D // PAIR
    nq = L // BLK
    scale = 1.0 / math.sqrt(HEAD_DIM)

    def body(q_ref, k_ref, v_ref, qw_ref, kw_ref, o_ref, qm_s, km_s, vm_s):
        uo = _suffix_ones()
        diag = _iota((BLK, BLK), 1) < _iota((BLK, BLK), 0)
        _attn_prep(q_ref, qw_ref, qm_s, nq, scale)
        _attn_prep(k_ref, kw_ref, km_s, nq, 1.0)
        _attn_prep(v_ref, None, vm_s, nq, 1.0)

        def qblock(qi, carry):
            r0 = pl.multiple_of(qi * BLK, BLK)
            qms = [qm_s[h, pl.ds(r0, BLK), :] for h in range(2)]

            def sweep(state, c0s, masks):
                kb, rc0, rc1, acc, _ = state
                tiles, _, _, ws, rc = _sb_tiles(qms, km_s, c0s, [rc0, rc1], uo, masks)
                wbs = [w.astype(BF16) for w in ws]
                for (u, h), wb in zip(tiles, wbs):
                    acc = acc + _dot(wb, vm_s[h, pl.ds(c0s[u], BLK), :])
                return kb - len(c0s), rc[0], rc[1], acc, jnp.maximum(jnp.max(rc[0]), jnp.max(rc[1]))

            def left(state, n):
                return [pl.multiple_of((state[0] - u) * BLK, BLK) for u in range(n)]

            zero_c = jnp.zeros((BLK, BLK), F32)
            state = sweep((qi, zero_c, zero_c, jnp.zeros((BLK, PAIR), F32), 0.0), *_first_blocks(qi, diag))
            state = lax.while_loop(lambda s: (s[0] >= 1) & (s[4] >= UNDERFLOW), lambda s: sweep(s, left(s, 2), [None] * 2), state)
            state = lax.while_loop(lambda s: (s[0] >= 0) & (s[4] >= UNDERFLOW), lambda s: sweep(s, left(s, 1), [None]), state)
            o_ref[0, pl.ds(r0, BLK), :] = state[3]
            return carry

        lax.fori_loop(0, nq, qblock, 0)

    blk = lambda off: pl.BlockSpec((1, L, PAIR), lambda b, p: (b, 0, off + p))
    wspec = pl.BlockSpec((1, PAIR), lambda b, p: (0, 0))
    return pl.pallas_call(
        body,
        name="sb_attn_fwd",
        grid=(Bl, n_pair),
        in_specs=[blk(0), blk(n_pair), blk(2 * n_pair), wspec, wspec],
        out_specs=pl.BlockSpec((1, L, PAIR), lambda b, p: (b, 0, p)),
        out_shape=jax.ShapeDtypeStruct((Bl, L, D), F32),
        scratch_shapes=[pltpu.VMEM((2, L, PAIR), BF16)] * 3,
        compiler_params=_params(("parallel", "parallel")),
    )(proj3, proj3, proj3, qw2, kw2)


def _attn_bwd(proj3, o3, do3, qw2, kw2, D):
    Bl, L, _ = proj3.shape
    n_pair = D // PAIR
    nq = L // BLK
    scale = 1.0 / math.sqrt(HEAD_DIM)

    def body(q_ref, k_ref, v_ref, o_ref, do_ref, qw_ref, kw_ref, dq_ref, dk_ref, dv_ref, dw_ref,
             qm_s, km_s, vm_s, dom_s, dq_s, dk_s, dv_s):
        uo = _suffix_ones()
        diag = _iota((BLK, BLK), 1) < _iota((BLK, BLK), 0)
        ones2 = _pair_ones()
        _attn_prep(q_ref, qw_ref, qm_s, nq, scale)
        _attn_prep(k_ref, kw_ref, km_s, nq, 1.0)
        _attn_prep(v_ref, None, vm_s, nq, 1.0)
        _attn_prep(do_ref, None, dom_s, nq, 1.0)

        @pl.when((pl.program_id(0) == 0) & (pl.program_id(1) == 0))
        def _():
            dw_ref[...] = jnp.zeros_like(dw_ref)

        def zero(i, carry):
            r0 = pl.multiple_of(i * BLK, BLK)
            dk_s[pl.ds(r0, BLK), :] = jnp.zeros((BLK, PAIR), F32)
            dv_s[pl.ds(r0, BLK), :] = jnp.zeros((BLK, PAIR), F32)
            return carry

        lax.fori_loop(0, nq, zero, 0)

        def tiles_bwd(qms, doms, deltas, c0s, rc, gc, dqa, masks):
            dws = [_dot_nt(doms[h], vm_s[h, pl.ds(c0s[u], BLK), :]) for u in range(len(c0s)) for h in range(2)]
            tiles, zs, es, ws, rc = _sb_tiles(qms, km_s, c0s, rc, uo, masks)
            wfs = [w.astype(BF16).astype(F32) for w in ws]
            gs = [wf * dw for wf, dw in zip(wfs, dws)]
            gss = _split_dots(gs, uo)
            gc = list(gc)
            dzs = []
            for (u, h), z, e, g, gsum in zip(tiles, zs, es, gs, gss):
                g_before = deltas[h] - (gc[h] + gsum[:, :BLK] + g)
                gc[h] = gc[h] + gsum[:, BLK:]
                r = 1.0 / (1.0 + e)
                er = e * r
                pos = z >= 0.0
                dz = g * jnp.where(pos, er, r) - g_before * jnp.where(pos, r, er)
                dzs.append(dz if masks[u] is None else jnp.where(masks[u], dz, 0.0))
            wts = [wf.T.astype(BF16) for wf in wfs]
            dzts = [dz.T.astype(BF16) for dz in dzs]
            dzbs = [dz.astype(BF16) for dz in dzs]
            for u, c0 in enumerate(c0s):
                dv_s[pl.ds(c0, BLK), :] += _dot(wts[2 * u], doms[0]) + _dot(wts[2 * u + 1], doms[1])
                dk_s[pl.ds(c0, BLK), :] += _dot(dzts[2 * u], qms[0]) + _dot(dzts[2 * u + 1], qms[1])
            for (u, h), dzb in zip(tiles, dzbs):
                dqa = dqa + _dot(dzb, km_s[h, pl.ds(c0s[u], BLK), :])
            return rc, gc, dqa

        def qblock(qi, carry):
            r0 = pl.multiple_of(qi * BLK, BLK)
            o_blk = o_ref[0, pl.ds(r0, BLK), :]
            qms = [qm_s[h, pl.ds(r0, BLK), :] for h in range(2)]
            doms = [dom_s[h, pl.ds(r0, BLK), :] for h in range(2)]
            deltas = [_rowsum(doms[h].astype(F32) * o_blk) for h in range(2)]

            def sweep(state, c0s, masks):
                kb, rc0, rc1, gc0, gc1, dqa, _ = state
                rc, gc, dqa = tiles_bwd(qms, doms, deltas, c0s, [rc0, rc1], [gc0, gc1], dqa, masks)
                return kb - len(c0s), rc[0], rc[1], gc[0], gc[1], dqa, jnp.maximum(jnp.max(rc[0]), jnp.max(rc[1]))

            def left(state, n):
                return [pl.multiple_of((state[0] - u) * BLK, BLK) for u in range(n)]

            zero_c = jnp.zeros((BLK, BLK), F32)
            state = sweep((qi, zero_c, zero_c, zero_c, zero_c, jnp.zeros((BLK, PAIR), F32), 0.0), *_first_blocks(qi, diag))
            state = lax.while_loop(lambda s: (s[0] >= 1) & (s[6] >= UNDERFLOW), lambda s: sweep(s, left(s, 2), [None] * 2), state)
            state = lax.while_loop(lambda s: (s[0] >= 0) & (s[6] >= UNDERFLOW), lambda s: sweep(s, left(s, 1), [None]), state)
            dq_s[pl.ds(r0, BLK), :] = state[5] * scale
            return carry

        lax.fori_loop(0, nq, qblock, 0)

        per = math.gcd(PREP_BLOCKS, nq)
        rows = per * BLK

        def finish(i, carry):
            r0 = pl.multiple_of(i * rows, rows)
            dwq, dwk = carry
            out = []
            for src_ref, w_ref, d_s in ((q_ref, qw_ref, dq_s), (k_ref, kw_ref, dk_s)):
                v = src_ref[0, pl.ds(r0, rows), :]
                r = _pair_rms(v, ones2)
                vh = v * r
                dy = d_s[pl.ds(r0, rows), :]
                dvh = dy * w_ref[...]
                out.append((r * (dvh - vh * _pair_mean(dvh * vh, ones2)), _colsum(dy * vh)))
            dq_ref[0, pl.ds(r0, rows), :] = out[0][0].astype(BF16)
            dk_ref[0, pl.ds(r0, rows), :] = out[1][0].astype(BF16)
            dv_ref[0, pl.ds(r0, rows), :] = dv_s[pl.ds(r0, rows), :].astype(BF16)
            return dwq + out[0][1], dwk + out[1][1]

        zrow = jnp.zeros((1, PAIR), F32)
        dwq, dwk = lax.fori_loop(0, nq // per, finish, (zrow, zrow))
        dw_ref[0:1, :] += dwq
        dw_ref[1:2, :] += dwk

    blk = lambda off: pl.BlockSpec((1, L, PAIR), lambda b, p: (b, 0, off + p))
    wspec = pl.BlockSpec((1, PAIR), lambda b, p: (0, 0))
    oblk = pl.BlockSpec((1, L, PAIR), lambda b, p: (b, 0, p))
    return pl.pallas_call(
        body,
        name="sb_attn_bwd",
        grid=(Bl, n_pair),
        in_specs=[blk(0), blk(n_pair), blk(2 * n_pair), oblk, oblk, wspec, wspec],
        out_specs=[oblk, oblk, oblk, pl.BlockSpec((8, PAIR), lambda b, p: (0, 0))],
        out_shape=[jax.ShapeDtypeStruct((Bl, L, D), BF16)] * 3 + [jax.ShapeDtypeStruct((8, PAIR), F32)],
        scratch_shapes=[pltpu.VMEM((2, L, PAIR), BF16)] * 4 + [pltpu.VMEM((L, PAIR), F32)] * 3,
        compiler_params=_params(("arbitrary", "arbitrary")),
    )(proj3, proj3, proj3, o3, do3, qw2, kw2)


def _conv_pre(ext_s, halo_ref, raw_ref, w_ref, b_ref, first):
    ext_s[0:HALO, :] = jnp.where(first, 0.0, halo_ref[0])
    ext_s[HALO:HALO + BLK, :] = raw_ref[0]
    pre = b_ref[...]
    for i in range(CONV_K):
        pre = pre + ext_s[pl.ds(HALO - (CONV_K - 1 - i), BLK), :] * w_ref[i:i + 1, :]
    return pre


def _lane_col(m, lane, h):
    return _rowsum(jnp.where(lane == h, m, 0.0))


def _half_sums(row, lo1):
    return _rowsum(jnp.where(lo1, row, 0.0)), _rowsum(jnp.where(lo1, 0.0, row))


def _ssd_specs(Bl, L, D, rev):
    nc = L // BLK
    rows_per = BLK // HALO
    cidx = (lambda c: nc - 1 - c) if rev else (lambda c: c)
    xoff = 5
    boff = (6 * D) // 512
    doff = (6 * D + 512) // LANES
    prev = lambda c: jnp.maximum(cidx(c) * rows_per - 1, 0)
    specs = [
        pl.BlockSpec((1, BLK, D), lambda b, c: (b, cidx(c), xoff)),
        pl.BlockSpec((1, BLK, 512), lambda b, c: (b, cidx(c), boff)),
        pl.BlockSpec((1, HALO, D), lambda b, c: (b, prev(c), xoff)),
        pl.BlockSpec((1, HALO, 512), lambda b, c: (b, prev(c), boff)),
        pl.BlockSpec((1, BLK, LANES), lambda b, c: (b, cidx(c), doff)),
    ]
    full = lambda shape: pl.BlockSpec(shape, lambda b, c: (0,) * len(shape))
    specs += [full((CONV_K, D)), full((CONV_K, 512)), full((1, D)), full((1, 512)),
              full((1, LANES)), full((1, LANES)), full((1, LANES))]
    return specs, cidx


def _ssd_common(dtr_ref, dtb_ref, alog_ref, acs_s, acsT_s):
    ltri = jnp.where(_iota((BLK, BLK), 1) <= _iota((BLK, BLK), 0), 1.0, 0.0).astype(BF16)
    dtv = _softplus(dtr_ref[0] + dtb_ref[...])
    a = -jnp.exp(alog_ref[...])
    acs = _dot_split(ltri, dtv * a)
    acs_s[...] = acs
    acsT_s[...] = acs.T
    return dtv, a, acs


def _pair_terms(pr, acs, dtv, acs_s, lane, lo, lane1, lo1):
    h0, h1 = 2 * pr, 2 * pr + 1
    c0, c1 = _lane_col(acs, lane, h0), _lane_col(acs, lane, h1)
    d0, d1 = _lane_col(dtv, lane, h0), _lane_col(dtv, lane, h1)
    lastv = acs_s[BLK - 1:BLK, :]
    l0, l1 = _lane_col(lastv, lane1, h0), _lane_col(lastv, lane1, h1)
    return dict(h=(h0, h1), c=(c0, c1), last=(l0, l1), acs_p=jnp.where(lo, c0, c1), dt_p=jnp.where(lo, d0, d1),
                last_p=jnp.where(lo1, l0, l1))


def _decay_tiles(cc, row, tri, want_t):
    lm = jnp.where(tri, jnp.exp(jnp.where(tri, cc - row, 0.0)), 0.0)
    if not want_t:
        return lm, None
    tri_t = _iota((BLK, BLK), 1) >= _iota((BLK, BLK), 0)
    return lm, jnp.where(tri_t, jnp.exp(jnp.where(tri_t, row - cc, 0.0)), 0.0)


def _ssd_fwd(proj3, cwx, cwb, cbx, cbb, dtb, alog, dsk, D):
    Bl, L, _ = proj3.shape
    nc = L // BLK
    n_pair = D // PAIR
    pairs_per_group = n_pair // SSD_GROUPS
    specs, _ = _ssd_specs(Bl, L, D, False)

    def body(xr_ref, bcr_ref, xh_ref, bch_ref, dtr_ref, cwx_ref, cwb_ref, cbx_ref, cbb_ref, dtb_ref, alog_ref,
             dsk_ref, y_ref, sin_ref, st_s, extx_s, extb_s, acs_s, acsT_s):
        first = pl.program_id(1) == 0

        @pl.when(first)
        def _():
            st_s[...] = jnp.zeros_like(st_s)

        lane, lane1 = _iota((BLK, LANES), 1), _iota((1, LANES), 1)
        lo, lo1 = lane < HEAD_DIM, lane1 < HEAD_DIM
        tri = _iota((BLK, BLK), 1) <= _iota((BLK, BLK), 0)
        pre = _conv_pre(extx_s, xh_ref, xr_ref, cwx_ref, cbx_ref, first)
        ux = pre * _sigmoid(pre)
        pre = _conv_pre(extb_s, bch_ref, bcr_ref, cwb_ref, cbb_ref, first)
        ub = pre * _sigmoid(pre)
        dtv, a, acs = _ssd_common(dtr_ref, dtb_ref, alog_ref, acs_s, acsT_s)
        for g in range(SSD_GROUPS):
            bg = ub[:, g * SSD_STATE:(g + 1) * SSD_STATE]
            cb_ = ub[:, (SSD_GROUPS + g) * SSD_STATE:(SSD_GROUPS + g + 1) * SSD_STATE].astype(BF16)
            cbm = _dot_nt(cb_, bg.astype(BF16))
            btb = bg.T.astype(BF16)
            for pr in range(g * pairs_per_group, (g + 1) * pairs_per_group):
                t = _pair_terms(pr, acs, dtv, acs_s, lane, lo, lane1, lo1)
                xs_p = ux[:, pr * PAIR:(pr + 1) * PAIR]
                x_p = xs_p * t["dt_p"]
                st = st_s[pr]
                sin_ref[0, 0, pr] = st
                y = _dot(cb_, st.astype(BF16)) * jnp.exp(t["acs_p"])
                for k in range(2):
                    row = acsT_s[t["h"][k]:t["h"][k] + 1, :]
                    lm, _ = _decay_tiles(t["c"][k], row, tri, False)
                    xm = jnp.where(lo if k == 0 else ~lo, x_p, 0.0).astype(BF16)
                    y = y + _dot((cbm * lm).astype(BF16), xm)
                d_p = jnp.where(lo1, _lane_col(dsk_ref[...], lane1, t["h"][0]), _lane_col(dsk_ref[...], lane1, t["h"][1]))
                y_ref[0, :, pr * PAIR:(pr + 1) * PAIR] = y + d_p * xs_p
                xd = (x_p * jnp.exp(t["last_p"] - t["acs_p"])).astype(BF16)
                st_s[pr] = st * jnp.exp(t["last_p"]) + _dot(btb, xd)

    return pl.pallas_call(
        body,
        name="ssd_fwd",
        grid=(Bl, nc),
        in_specs=specs,
        out_specs=[
            pl.BlockSpec((1, BLK, D), lambda b, c: (b, c, 0)),
            pl.BlockSpec((1, 1, n_pair, SSD_STATE, PAIR), lambda b, c: (b, c, 0, 0, 0)),
        ],
        out_shape=[jax.ShapeDtypeStruct((Bl, L, D), F32),
                   jax.ShapeDtypeStruct((Bl, nc, n_pair, SSD_STATE, PAIR), F32)],
        scratch_shapes=[pltpu.VMEM((n_pair, SSD_STATE, PAIR), F32), pltpu.VMEM((HALO + BLK, D), F32),
                        pltpu.VMEM((HALO + BLK, 512), F32), pltpu.VMEM((BLK, LANES), F32),
                        pltpu.VMEM((LANES, BLK), F32)],
        compiler_params=_params(("arbitrary", "arbitrary")),
    )(proj3, proj3, proj3, proj3, proj3, cwx, cwb, cbx, cbb, dtb, alog, dsk)


def _ssd_bwd(proj3, s_in, dy3, cwx, cwb, cbx, cbb, dtb, alog, dsk, D, tail):
    Bl, L, _ = proj3.shape
    CD = D + 512
    nc = L // BLK
    n_pair = D // PAIR
    n_heads = 2 * n_pair
    pairs_per_group = n_pair // SSD_GROUPS
    specs, cidx = _ssd_specs(Bl, L, D, True)
    specs = specs + [
        pl.BlockSpec((1, 1, n_pair, SSD_STATE, PAIR), lambda b, c: (b, cidx(c), 0, 0, 0)),
        pl.BlockSpec((1, BLK, D), lambda b, c: (b, cidx(c), 0)),
    ]

    def body(xr_ref, bcr_ref, xh_ref, bch_ref, dtr_ref, cwx_ref, cwb_ref, cbx_ref, cbb_ref, dtb_ref, alog_ref,
             dsk_ref, sin_ref, dy_ref, dxbc_ref, dcwx_ref, dcwb_ref, dcbx_ref, dcbb_ref, misc_ref,
             dst_s, extx_s, extb_s, acs_s, acsT_s, dux_s, dub_s, e2x_s, e2b_s, nxx_s, nxb_s):
        step = pl.program_id(1)
        first = step == nc - 1
        last = step == 0

        @pl.when(last)
        def _():
            dst_s[...] = jnp.zeros_like(dst_s)
            nxx_s[...] = jnp.zeros_like(nxx_s)
            nxb_s[...] = jnp.zeros_like(nxb_s)

        @pl.when(last & (pl.program_id(0) == 0))
        def _():
            for r in (dcwx_ref, dcwb_ref, dcbx_ref, dcbb_ref, misc_ref):
                r[...] = jnp.zeros_like(r)

        lane, lane1 = _iota((BLK, LANES), 1), _iota((1, LANES), 1)
        lo, lo1 = lane < HEAD_DIM, lane1 < HEAD_DIM
        tri = _iota((BLK, BLK), 1) <= _iota((BLK, BLK), 0)
        prex = _conv_pre(extx_s, xh_ref, xr_ref, cwx_ref, cbx_ref, first)
        sgx = _sigmoid(prex)
        ux = prex * sgx
        preb = _conv_pre(extb_s, bch_ref, bcr_ref, cwb_ref, cbb_ref, first)
        sgb = _sigmoid(preb)
        ub = preb * sgb
        dtv, a, acs = _ssd_common(dtr_ref, dtb_ref, alog_ref, acs_s, acsT_s)
        dacs = jnp.zeros((BLK, LANES), F32)
        dlast = jnp.zeros((1, LANES), F32)
        ddt = jnp.zeros((BLK, LANES), F32)
        dd = jnp.zeros((1, LANES), F32)
        for g in range(SSD_GROUPS):
            bg = ub[:, g * SSD_STATE:(g + 1) * SSD_STATE]
            cg = ub[:, (SSD_GROUPS + g) * SSD_STATE:(SSD_GROUPS + g + 1) * SSD_STATE]
            bb, cb_ = bg.astype(BF16), cg.astype(BF16)
            cbm = _dot_nt(cb_, bb)
            cbt = _dot_nt(bb, cb_)
            ctb = cg.T.astype(BF16)
            dbg = jnp.zeros((BLK, SSD_STATE), F32)
            dcg = jnp.zeros((BLK, SSD_STATE), F32)
            for pr in range(g * pairs_per_group, (g + 1) * pairs_per_group):
                t = _pair_terms(pr, acs, dtv, acs_s, lane, lo, lane1, lo1)
                h0, h1 = t["h"]
                xs_p = ux[:, pr * PAIR:(pr + 1) * PAIR]
                dy_p = dy_ref[0, :, pr * PAIR:(pr + 1) * PAIR]
                x_p = xs_p * t["dt_p"]
                ea_p = jnp.exp(t["acs_p"])
                dte_p = jnp.exp(t["last_p"] - t["acs_p"])
                cd_p = jnp.exp(t["last_p"])
                st = sin_ref[0, 0, pr]
                dst = dst_s[pr]
                stb, dstb = st.astype(BF16), dst.astype(BF16)
                s0, s1 = _half_sums(_colsum(dy_p * xs_p), lo1)
                dd = dd + jnp.where(lane1 == h0, s0, 0.0) + jnp.where(lane1 == h1, s1, 0.0)
                d_p = jnp.where(lo1, _lane_col(dsk_ref[...], lane1, h0), _lane_col(dsk_ref[...], lane1, h1))
                dxs_p = d_p * dy_p
                dp = dy_p * ea_p
                dpb = dp.astype(BF16)
                yo = dp * _dot(cb_, stb)
                dcg = dcg + _dot_nt(dpb, stb)
                dst_off = _dot(ctb, dpb)
                dac = [_rowsum(jnp.where(lo, yo, 0.0)), _rowsum(jnp.where(lo, 0.0, yo))]
                s0, s1 = _half_sums(_colsum(dst * st), lo1)
                dl = [s0 * jnp.exp(t["last"][0]), s1 * jnp.exp(t["last"][1])]
                dxd = _dot(bb, dstb)
                dx_p = dxd * dte_p
                tt = dxd * x_p
                dbg = dbg + _dot_nt((x_p * dte_p).astype(BF16), dstb)
                for k, ddte in enumerate((_rowsum(jnp.where(lo, tt, 0.0)), _rowsum(jnp.where(lo, 0.0, tt)))):
                    ek = ddte * jnp.exp(t["last"][k] - t["c"][k])
                    dl[k] = dl[k] + _colsum(ek)
                    dac[k] = dac[k] - ek
                for k in range(2):
                    row = acsT_s[t["h"][k]:t["h"][k] + 1, :]
                    lm, lmt = _decay_tiles(t["c"][k], row, tri, True)
                    msk = lo if k == 0 else ~lo
                    xm = jnp.where(msk, x_p, 0.0).astype(BF16)
                    dym = jnp.where(msk, dy_p, 0.0).astype(BF16)
                    dm = _dot_nt(dym, xm)
                    dmt = _dot_nt(xm, dym)
                    mt = cbt * lmt
                    dx_p = dx_p + _dot(mt.astype(BF16), dym)
                    dac[k] = dac[k] + _rowsum(dm * (cbm * lm)) - _rowsum(dmt * mt)
                    dcg = dcg + _dot((dm * lm).astype(BF16), bb)
                    dbg = dbg + _dot((dmt * lmt).astype(BF16), cb_)
                dacs = dacs + jnp.where(lane == h0, dac[0], 0.0) + jnp.where(lane == h1, dac[1], 0.0)
                dlast = dlast + jnp.where(lane1 == h0, dl[0], 0.0) + jnp.where(lane1 == h1, dl[1], 0.0)
                dxs_p = dxs_p + dx_p * t["dt_p"]
                t3 = dx_p * xs_p
                ddt = ddt + jnp.where(lane == h0, _rowsum(jnp.where(lo, t3, 0.0)), 0.0) \
                    + jnp.where(lane == h1, _rowsum(jnp.where(lo, 0.0, t3)), 0.0)
                dux_s[:, pr * PAIR:(pr + 1) * PAIR] = dxs_p
                dst_s[pr] = dst * cd_p + dst_off
            dub_s[:, g * SSD_STATE:(g + 1) * SSD_STATE] = dbg
            dub_s[:, (SSD_GROUPS + g) * SSD_STATE:(SSD_GROUPS + g + 1) * SSD_STATE] = dcg
        dacs = dacs + jnp.where(_iota((BLK, LANES), 0) == BLK - 1, dlast, 0.0)
        utri = jnp.where(_iota((BLK, BLK), 1) >= _iota((BLK, BLK), 0), 1.0, 0.0).astype(BF16)
        dda = _dot_split(utri, dacs)
        ddt = ddt + dda * a
        ddtr = jnp.where(lane < n_heads, ddt * _sigmoid(dtr_ref[0] + dtb_ref[...]), 0.0)
        dxbc_ref[0, :, CD:CD + LANES] = ddtr.astype(BF16)
        dxbc_ref[0, :, CD + LANES:tail] = jnp.zeros((BLK, tail - CD - LANES), BF16)
        misc_ref[0:1, :] += _colsum(ddtr)
        misc_ref[1:2, :] += jnp.where(lane1 < n_heads, _colsum(dda * dtv) * a, 0.0)
        misc_ref[2:3, :] += dd
        for (du_s, pre, sg, ext_s, e2_s, nx_s, w_ref, dcw_ref, dcb_ref, c0, width) in (
                (dux_s, prex, sgx, extx_s, e2x_s, nxx_s, cwx_ref, dcwx_ref, dcbx_ref, 0, D),
                (dub_s, preb, sgb, extb_s, e2b_s, nxb_s, cwb_ref, dcwb_ref, dcbb_ref, D, 512)):
            dpre = du_s[...] * (sg * (1.0 + pre * (1.0 - sg)))
            dcb_ref[...] += _colsum(dpre)
            for i in range(CONV_K):
                dcw_ref[i:i + 1, :] += _colsum(dpre * ext_s[pl.ds(HALO - (CONV_K - 1 - i), BLK), :])
            e2_s[0:BLK, :] = dpre
            e2_s[BLK:BLK + HALO, :] = nx_s[...]
            dxr = jnp.zeros((BLK, width), F32)
            for i in range(CONV_K):
                dxr = dxr + e2_s[pl.ds(CONV_K - 1 - i, BLK), :] * w_ref[i:i + 1, :]
            dxbc_ref[0, :, c0:c0 + width] = dxr.astype(BF16)
            nx_s[...] = e2_s[0:HALO, :]

    full = lambda shape: pl.BlockSpec(shape, lambda b, c: (0,) * len(shape))
    return pl.pallas_call(
        body,
        name="ssd_bwd",
        grid=(Bl, nc),
        in_specs=specs,
        out_specs=[
            pl.BlockSpec((1, BLK, tail), lambda b, c: (b, cidx(c), 0)),
            full((CONV_K, D)), full((CONV_K, 512)), full((1, D)), full((1, 512)), full((8, LANES)),
        ],
        out_shape=[
            jax.ShapeDtypeStruct((Bl, L, tail), BF16),
            jax.ShapeDtypeStruct((CONV_K, D), F32), jax.ShapeDtypeStruct((CONV_K, 512), F32),
            jax.ShapeDtypeStruct((1, D), F32), jax.ShapeDtypeStruct((1, 512), F32),
            jax.ShapeDtypeStruct((8, LANES), F32),
        ],
        scratch_shapes=[
            pltpu.VMEM((n_pair, SSD_STATE, PAIR), F32),
            pltpu.VMEM((HALO + BLK, D), F32), pltpu.VMEM((HALO + BLK, 512), F32),
            pltpu.VMEM((BLK, LANES), F32), pltpu.VMEM((LANES, BLK), F32),
            pltpu.VMEM((BLK, D), F32), pltpu.VMEM((BLK, 512), F32),
            pltpu.VMEM((BLK + HALO, D), F32), pltpu.VMEM((BLK + HALO, 512), F32),
            pltpu.VMEM((HALO, D), F32), pltpu.VMEM((HALO, 512), F32),
        ],
        compiler_params=_params(("arbitrary", "arbitrary")),
    )(proj3, proj3, proj3, proj3, proj3, cwx, cwb, cbx, cbb, dtb, alog, dsk, s_in, dy3)


def _gate_out(x2, tgt2, o2, proj2, y2, sbw, ssw, w_out_bf):
    T, D = x2.shape
    tm = min(256, T)

    def body(x_ref, t_ref, o_ref, zs_ref, y_ref, zy_ref, sbw_ref, ssw_ref, wo_ref,
             dout_ref, doutb_ref, mixt_ref, do_ref, dy_ref, dz_ref, dnw_ref, loss_ref):
        @pl.when(pl.program_id(0) == 0)
        def _():
            dnw_ref[...] = jnp.zeros_like(dnw_ref)
            loss_ref[...] = jnp.zeros_like(loss_ref)

        def fwd(o, z, w):
            sg = _sigmoid(z)
            sl = z * sg
            g = o * sl
            r = lax.rsqrt(jnp.mean(g * g, axis=-1, keepdims=True) + EPS)
            n = g * r
            return sg, sl, r, n, n * w

        def bwd(dy, o, z, w, sg, sl, r, n):
            dn = dy * w
            dg = r * (dn - n * jnp.mean(dn * n, axis=-1, keepdims=True))
            return dg * sl, dg * o * (sg * (1.0 + z * (1.0 - sg))), _colsum(dy * n)

        o1, z1, w1 = o_ref[...], zs_ref[...], sbw_ref[...]
        o2_, z2, w2 = y_ref[...], zy_ref[...], ssw_ref[...]
        sg1, sl1, r1, n1, y1 = fwd(o1, z1, w1)
        sg2, sl2, r2, n2, y2_ = fwd(o2_, z2, w2)
        y1b, y2b = y1.astype(BF16), y2_.astype(BF16)
        mixt_ref[0:D, :] = y1.T.astype(BF16)
        mixt_ref[D:2 * D, :] = y2_.T.astype(BF16)
        out = x_ref[...] + (_dot(y1b, wo_ref[0:D, :]) + _dot(y2b, wo_ref[D:2 * D, :]))
        err = out - t_ref[...]
        loss_ref[...] += jnp.sum(err * err) * (0.5 / D)
        dout = err * (1.0 / D)
        dout_ref[...] = dout
        doutb = dout.astype(BF16)
        doutb_ref[...] = doutb
        do1, dz1, dw1 = bwd(_dot_nt(doutb, wo_ref[0:D, :]), o1, z1, w1, sg1, sl1, r1, n1)
        do2, dz2, dw2 = bwd(_dot_nt(doutb, wo_ref[D:2 * D, :]), o2_, z2, w2, sg2, sl2, r2, n2)
        do_ref[...] = do1
        dy_ref[...] = do2
        dz_ref[:, 0:D] = dz1.astype(BF16)
        dz_ref[:, D:2 * D] = dz2.astype(BF16)
        dnw_ref[0:1, :] += dw1
        dnw_ref[1:2, :] += dw2

    row = lambda col: pl.BlockSpec((tm, D), lambda i: (i, col))
    full = lambda shape: pl.BlockSpec(shape, lambda i: (0,) * len(shape))
    wide = pl.BlockSpec((tm, 2 * D), lambda i: (i, 0))
    return pl.pallas_call(
        body,
        name="gate_out",
        grid=(T // tm,),
        in_specs=[row(0), row(0), row(0), row(3), row(0), row(4), full((1, D)), full((1, D)), full((2 * D, D))],
        out_specs=[row(0), row(0), pl.BlockSpec((2 * D, tm), lambda i: (0, i)), row(0), row(0), wide,
                   full((8, D)), full((8, LANES))],
        out_shape=[
            jax.ShapeDtypeStruct((T, D), F32), jax.ShapeDtypeStruct((T, D), BF16),
            jax.ShapeDtypeStruct((2 * D, T), BF16), jax.ShapeDtypeStruct((T, D), F32),
            jax.ShapeDtypeStruct((T, D), F32), jax.ShapeDtypeStruct((T, 2 * D), BF16),
            jax.ShapeDtypeStruct((8, D), F32), jax.ShapeDtypeStruct((8, LANES), F32),
        ],
        compiler_params=_params(("arbitrary",)),
    )(x2, tgt2, o2, proj2, y2, proj2, sbw, ssw, w_out_bf)


def _piece_blocks(pieces, D):
    counts = [p.shape[1] // D for p in pieces]
    return [sum(counts[:i]) for i in range(len(counts))], counts


def _dhn(pieces, w_pad, x2, dout, norm_w, h_in, h_out):
    T, D = x2.shape
    tm = min(512, T)
    starts, counts = _piece_blocks(pieces, D)
    nk = sum(counts)
    ni = T // tm
    n_sem = 2 * (N_CHIPS - 1)
    assert nk * D == w_pad.shape[1]

    def body(*refs):
        p_refs = refs[:len(pieces)]
        (w_ref, x_ref, dout_ref, nw_ref, hin, hout, gx_ref, dnw_ref, rin, rout,
         acc_s, send_sems, recv_sems) = refs[len(pieces):]
        i, k = pl.program_id(0), pl.program_id(1)

        def scatter():
            x, y, c, chips = _place()
            return [pltpu.make_async_remote_copy(
                src_ref=s.at[2 * px + py], dst_ref=d.at[j], send_sem=send_sems.at[2 * j + m],
                recv_sem=recv_sems.at[2 * j + m], device_id=(px, py, c), device_id_type=MESH)
                for j, (px, py) in enumerate(chips) for m, (s, d) in enumerate(((hin, rin), (hout, rout)))]

        @pl.when((i == 0) & (k == 0))
        def _():
            for cp in scatter():
                cp.start()

        @pl.when((i == ni - 1) & (k == nk - 1))
        def _():
            for cp in scatter():
                cp.wait()

        @pl.when((i == 0) & (k == 0))
        def _():
            dnw_ref[...] = jnp.zeros_like(dnw_ref)

        @pl.when(k == 0)
        def _():
            acc_s[...] = jnp.zeros_like(acc_s)

        for p_ref, s, n in zip(p_refs, starts, counts):
            @pl.when((k >= s) & (k < s + n))
            def _(p_ref=p_ref):
                acc_s[...] += _dot_nt(p_ref[...], w_ref[...])

        @pl.when(k == nk - 1)
        def _():
            xv = x_ref[...]
            r = lax.rsqrt(jnp.mean(xv * xv, axis=-1, keepdims=True) + EPS)
            xh = xv * r
            dhn = acc_s[...]
            dxh = dhn * nw_ref[...]
            gx_ref[...] = dout_ref[...] + r * (dxh - xh * jnp.mean(dxh * xh, axis=-1, keepdims=True))
            dnw_ref[0:1, :] += _colsum(dhn * xh)

    return pl.pallas_call(
        body,
        name="dhn",
        grid=(T // tm, nk),
        in_specs=[pl.BlockSpec((tm, D), lambda i, k, s=s, n=n: (i, jnp.clip(k - s, 0, n - 1)))
                  for s, n in zip(starts, counts)] + [
            pl.BlockSpec((D, D), lambda i, k: (0, k)),
            pl.BlockSpec((tm, D), lambda i, k: (i, 0)),
            pl.BlockSpec((tm, D), lambda i, k: (i, 0)),
            pl.BlockSpec((1, D), lambda i, k: (0, 0)),
            ANY, ANY,
        ],
        out_specs=[pl.BlockSpec((tm, D), lambda i, k: (i, 0)), pl.BlockSpec((8, D), lambda i, k: (0, 0)), ANY, ANY],
        out_shape=[jax.ShapeDtypeStruct((T, D), F32), jax.ShapeDtypeStruct((8, D), F32),
                   jax.ShapeDtypeStruct((N_CHIPS - 1,) + h_in.shape[1:], F32),
                   jax.ShapeDtypeStruct((N_CHIPS - 1,) + h_out.shape[1:], F32)],
        scratch_shapes=[pltpu.VMEM((tm, D), F32), pltpu.SemaphoreType.DMA((n_sem,)), pltpu.SemaphoreType.DMA((n_sem,))],
        compiler_params=_params(("arbitrary", "arbitrary")),
    )(*pieces, w_pad, x2, dout, norm_w, h_in, h_out)


def _grad_w_in(hn_t, pieces):
    D, T = hn_t.shape
    tk = min(1024, T)
    starts, counts = _piece_blocks(pieces, D)

    def body(*refs):
        a_ref, p_refs, o_ref = refs[0], refs[1:-1], refs[-1]
        j = pl.program_id(0)

        @pl.when(pl.program_id(1) == 0)
        def _():
            o_ref[...] = jnp.zeros_like(o_ref)

        for p_ref, s, n in zip(p_refs, starts, counts):
            @pl.when((j >= s) & (j < s + n))
            def _(p_ref=p_ref):
                o_ref[...] += _dot(a_ref[...], p_ref[...])

    def piece_spec(s, n):
        return pl.BlockSpec((tk, D), lambda j, k: (jnp.where((j >= s) & (j < s + n), k, 0), jnp.clip(j - s, 0, n - 1)))

    return pl.pallas_call(
        body,
        name="grad_w_in",
        grid=(sum(counts), T // tk),
        in_specs=[pl.BlockSpec((D, tk), lambda j, k: (0, k))] + [piece_spec(s, n) for s, n in zip(starts, counts)],
        out_specs=pl.BlockSpec((D, D), lambda j, k: (0, j)),
        out_shape=jax.ShapeDtypeStruct((D, sum(counts) * D), F32),
        compiler_params=_params(("parallel", "arbitrary")),
    )(hn_t, *pieces)


def _matmul(a, b, name):
    M, K = a.shape
    N = b.shape[1]
    tm = min(1024, M)
    tn = 1024 if N % 1024 == 0 else (512 if N % 512 == 0 else N)
    tk = min(512, K)

    def body(a_ref, b_ref, o_ref):
        @pl.when(pl.program_id(2) == 0)
        def _():
            o_ref[...] = jnp.zeros_like(o_ref)

        o_ref[...] += _dot(a_ref[...], b_ref[...])

    return pl.pallas_call(
        body,
        name=name,
        grid=(M // tm, N // tn, K // tk),
        in_specs=[pl.BlockSpec((tm, tk), lambda i, j, k: (i, k)), pl.BlockSpec((tk, tn), lambda i, j, k: (k, j))],
        out_specs=pl.BlockSpec((tm, tn), lambda i, j, k: (i, j)),
        out_shape=jax.ShapeDtypeStruct((M, N), F32),
        compiler_params=_params(("parallel", "parallel", "arbitrary")),
    )(a, b)


def _adamw(w, g, m, v, name):
    R, C = w.shape
    tr = 256 if R % 256 == 0 else R
    c1 = 1.0 - ADAM_B1 ** ADAM_STEP
    c2 = 1.0 - ADAM_B2 ** ADAM_STEP

    def body(w_ref, g_ref, m_ref, v_ref, d_ref, nm_ref, nv_ref):
        gv = g_ref[...]
        m_new = ADAM_B1 * m_ref[...] + (1.0 - ADAM_B1) * gv
        v_new = ADAM_B2 * v_ref[...] + (1.0 - ADAM_B2) * (gv * gv)
        d_ref[...] = -ADAM_LR * ((m_new / c1) / (jnp.sqrt(v_new / c2) + ADAM_EPS) + ADAM_WD * w_ref[...])
        nm_ref[...] = m_new
        nv_ref[...] = v_new

    spec = pl.BlockSpec((tr, C), lambda i: (i, 0))
    return pl.pallas_call(
        body,
        name=name,
        grid=(R // tr,),
        in_specs=[spec] * 4,
        out_specs=[spec] * 3,
        out_shape=[jax.ShapeDtypeStruct((R, C), F32)] * 3,
        compiler_params=_params(("parallel",)),
    )(w, g, m, v)


def _add_core_half(a, recv, core, name):
    _, n, h, S = a.shape
    th = 256 if h % 256 == 0 else h

    def body(c_ref, a_ref, r_ref, o_ref):
        o_ref[...] = a_ref[...] + r_ref[...]

    return pl.pallas_call(
        body,
        name=name,
        grid_spec=pltpu.PrefetchScalarGridSpec(
            num_scalar_prefetch=1,
            grid=(n, h // th),
            in_specs=[
                pl.BlockSpec((None, None, th, S), lambda p, i, c: (c[0], p, i, 0)),
                pl.BlockSpec((None, th, S), lambda p, i, c: (p, i, 0)),
            ],
            out_specs=pl.BlockSpec((None, th, S), lambda p, i, c: (p, i, 0)),
        ),
        out_shape=jax.ShapeDtypeStruct((n, h, S), F32),
        compiler_params=_params(("parallel", "parallel")),
    )(core, a, recv)


def _add_chips(hsum, recv, chip, name):
    _, h, S = hsum.shape
    th = 256 if h % 256 == 0 else h

    def body(c_ref, a_ref, r_ref, o_ref):
        o_ref[...] = ((a_ref[...] + r_ref[0]) + r_ref[1]) + r_ref[2]

    return pl.pallas_call(
        body,
        name=name,
        grid_spec=pltpu.PrefetchScalarGridSpec(
            num_scalar_prefetch=1,
            grid=(h // th,),
            in_specs=[
                pl.BlockSpec((None, th, S), lambda i, c: (c[0], i, 0)),
                pl.BlockSpec((N_CHIPS - 1, th, S), lambda i, c: (0, i, 0)),
            ],
            out_specs=pl.BlockSpec((th, S), lambda i, c: (i, 0)),
        ),
        out_shape=jax.ShapeDtypeStruct((h, S), F32),
        compiler_params=_params(("parallel",)),
    )(chip, hsum, recv)


def _place():
    x, y, c = lax.axis_index("x"), lax.axis_index("y"), lax.axis_index("c")
    other_chips = [(1 - x, y), (x, 1 - y), (1 - x, 1 - y)]
    return x, y, c, other_chips


def _allgather_weights(w_in_bf, w_out_bf, conv_w):
    D, S = w_in_bf.shape
    R = w_out_bf.shape[0]
    n_ici, n_fwd = 3 * (N_CHIPS - 1), 2 * (N_CHIPS - 1)

    def body(win, wout, cw, gin, gout, gcw, send_sems, recv_sems):
        x, y, c, chips = _place()
        me = 2 * x + y
        sibling = (x, y, 1 - c)
        hin, hout = D // 2, R // 2

        def halves(chip_idx):
            return (gin.at[chip_idx, pl.ds(c * hin, hin)], gout.at[chip_idx, pl.ds(c * hout, hout)])

        def rcopy(k, src, dst, to):
            return pltpu.make_async_remote_copy(src_ref=src, dst_ref=dst, send_sem=send_sems.at[k],
                                                recv_sem=recv_sems.at[k], device_id=to, device_id_type=MESH)

        my_in, my_out = halves(me)
        src_in, src_out = win.at[pl.ds(c * hin, hin)], wout.at[pl.ds(c * hout, hout)]
        sends = []
        for j, chip in enumerate(chips):
            to = (*chip, c)
            sends += [rcopy(3 * j, src_in, my_in, to), rcopy(3 * j + 1, src_out, my_out, to),
                      rcopy(3 * j + 2, cw, gcw.at[me], to)]
        for cp in sends:
            cp.start()
        passed = []
        for j, (px, py) in enumerate(chips):
            their_in, their_out = halves(2 * px + py)
            rcopy(3 * j, their_in, their_in, sibling).wait_recv()
            rcopy(3 * j + 1, their_out, their_out, sibling).wait_recv()
            rcopy(3 * j + 2, cw, gcw.at[2 * px + py], sibling).wait_recv()
            fw = [rcopy(n_ici + 2 * j, their_in, their_in, sibling), rcopy(n_ici + 2 * j + 1, their_out, their_out, sibling)]
            for cp in fw:
                cp.start()
            passed += fw
        for j, (px, py) in enumerate(chips):
            oin = gin.at[2 * px + py, pl.ds((1 - c) * hin, hin)]
            oout = gout.at[2 * px + py, pl.ds((1 - c) * hout, hout)]
            rcopy(n_ici + 2 * j, oin, oin, sibling).wait_recv()
            rcopy(n_ici + 2 * j + 1, oout, oout, sibling).wait_recv()
        for cp in sends + passed:
            cp.wait_send()

    return pl.pallas_call(
        body,
        name="allgather_weights",
        in_specs=[ANY, ANY, ANY],
        out_specs=[ANY, ANY, ANY],
        out_shape=[jax.ShapeDtypeStruct((N_CHIPS, D, S), BF16), jax.ShapeDtypeStruct((N_CHIPS, R, D), BF16),
                   jax.ShapeDtypeStruct((N_CHIPS,) + conv_w.shape, F32)],
        scratch_shapes=[pltpu.SemaphoreType.DMA((n_ici + n_fwd,)), pltpu.SemaphoreType.DMA((n_ici + n_fwd,))],
    )(w_in_bf, w_out_bf, conv_w)


def _allreduce_small(packed):
    R = packed.shape[0]
    n_dev = 2 * N_CHIPS

    def body(p_ref, o_ref, buf, send_sems, recv_sems):
        x, y, c, _ = _place()
        me = 4 * x + 2 * y + c
        buf[me] = p_ref[...]
        copies = []
        for k in range(1, n_dev):
            px = 1 - x if k & 4 else x
            py = 1 - y if k & 2 else y
            pc = 1 - c if k & 1 else c
            copies.append((pltpu.make_async_remote_copy(
                src_ref=buf.at[me], dst_ref=buf.at[me], send_sem=send_sems.at[k - 1], recv_sem=recv_sems.at[k - 1],
                device_id=(px, py, pc), device_id_type=MESH), 4 * px + 2 * py + pc, (px, py, pc)))
        for cp, _, _ in copies:
            cp.start()
        for k, (_, peer, to) in enumerate(copies):
            pltpu.make_async_remote_copy(
                src_ref=buf.at[peer], dst_ref=buf.at[peer], send_sem=send_sems.at[k], recv_sem=recv_sems.at[k],
                device_id=to, device_id_type=MESH).wait_recv()
        for cp, _, _ in copies:
            cp.wait_send()
        acc = buf[0]
        for d in range(1, n_dev):
            acc = acc + buf[d]
        o_ref[...] = acc

    vm = pl.BlockSpec(memory_space=pltpu.VMEM)
    return pl.pallas_call(
        body,
        name="allreduce_small",
        in_specs=[vm],
        out_specs=vm,
        out_shape=jax.ShapeDtypeStruct((R, LANES), F32),
        scratch_shapes=[pltpu.VMEM((n_dev, R, LANES), F32), pltpu.SemaphoreType.DMA((n_dev - 1,)),
                        pltpu.SemaphoreType.DMA((n_dev - 1,))],
    )(packed)


def _swap_core_halves(a_in, a_out):
    def body(ain, aout, rin, rout, send_sems, recv_sems):
        x, y, c, _ = _place()
        cps = [pltpu.make_async_remote_copy(src_ref=s.at[1 - c], dst_ref=d, send_sem=send_sems.at[k],
                                            recv_sem=recv_sems.at[k], device_id=(x, y, 1 - c), device_id_type=MESH)
               for k, (s, d) in enumerate(((ain, rin), (aout, rout)))]
        for cp in cps:
            cp.start()
        for cp in cps:
            cp.wait()

    return pl.pallas_call(
        body,
        name="reduce_core_swap",
        in_specs=[ANY, ANY],
        out_specs=[ANY, ANY],
        out_shape=[jax.ShapeDtypeStruct(a_in.shape[1:], F32), jax.ShapeDtypeStruct(a_out.shape[1:], F32)],
        scratch_shapes=[pltpu.SemaphoreType.DMA((2,)), pltpu.SemaphoreType.DMA((2,))],
    )(a_in, a_out)


def _join_core_halves(g_in, g_out):
    def body(gin, gout, fin, fout, send_sems, recv_sems):
        x, y, c, _ = _place()
        cps = [pltpu.make_async_remote_copy(src_ref=s, dst_ref=d.at[c], send_sem=send_sems.at[k],
                                            recv_sem=recv_sems.at[k], device_id=(x, y, 1 - c), device_id_type=MESH)
               for k, (s, d) in enumerate(((gin, fin), (gout, fout)))]
        for cp in cps:
            cp.start()
        for k, (s, d) in enumerate(((gin, fin), (gout, fout))):
            pltpu.make_async_remote_copy(src_ref=s, dst_ref=d.at[1 - c], send_sem=send_sems.at[k],
                                         recv_sem=recv_sems.at[k], device_id=(x, y, 1 - c),
                                         device_id_type=MESH).wait_recv()
        for cp in cps:
            cp.wait_send()

    return pl.pallas_call(
        body,
        name="reduce_core_join",
        in_specs=[ANY, ANY],
        out_specs=[ANY, ANY],
        out_shape=[jax.ShapeDtypeStruct((2,) + g_in.shape, F32), jax.ShapeDtypeStruct((2,) + g_out.shape, F32)],
        scratch_shapes=[pltpu.SemaphoreType.DMA((2,)), pltpu.SemaphoreType.DMA((2,))],
    )(g_in, g_out)


def _pack(arrays):
    rows = []
    for a in arrays:
        flat = a.reshape(-1).astype(F32)
        n = -(-flat.shape[0] // LANES) * LANES
        rows.append(jnp.pad(flat, (0, n - flat.shape[0])).reshape(-1, LANES))
    out = jnp.concatenate(rows, axis=0)
    return jnp.pad(out, ((0, -out.shape[0] % 8), (0, 0)))


def _unpack(packed, shapes):
    out, r = [], 0
    for shp in shapes:
        n = math.prod(shp)
        nr = -(-n // LANES)
        out.append(packed[r:r + nr].reshape(-1)[:n].reshape(shp))
        r += nr
    return out


def _pad_lanes(a):
    return jnp.pad(a, ((0, 0), (0, LANES - a.shape[1])))


def kernel(x, norm_w, w_in, q_norm_w, k_norm_w, conv_w, conv_b, dt_bias, A_log, D_skip, sb_norm_w, ssd_norm_w, w_out, loss_target, m_norm_w, m_w_in, m_q_norm_w, m_k_norm_w, m_conv_w, m_conv_b, m_dt_bias, m_A_log, m_D_skip, m_sb_norm_w, m_ssd_norm_w, m_w_out, v_norm_w, v_w_in, v_q_norm_w, v_k_norm_w, v_conv_w, v_conv_b, v_dt_bias, v_A_log, v_D_skip, v_sb_norm_w, v_ssd_norm_w, v_w_out):
    Bl, L, D = x.shape
    T = Bl * L
    S = w_in.shape[2]
    R = w_out.shape[1]
    CW = conv_w.shape[2]
    n_in = N_CHIPS * S
    CD = D + 2 * SSD_GROUPS * SSD_STATE
    H = D // HEAD_DIM
    n_main = 6 * D + 512
    P = -(-(n_main + LANES) // 1024) * 1024
    assert n_in == n_main + H and CD == N_CHIPS * CW and 2 * D == N_CHIPS * R and CD == D + 512
    chip = (2 * lax.axis_index("x") + lax.axis_index("y")).astype(jnp.int32)
    core = lax.axis_index("c").astype(jnp.int32)

    w_in_bf, w_out_shard_bf = w_in[0].astype(BF16), w_out[0].astype(BF16)
    g_in, g_out, g_cw = _allgather_weights(w_in_bf, w_out_shard_bf, conv_w[0])
    g_in = lax.dynamic_update_slice(g_in, w_in_bf[None], (chip, 0, 0))
    g_out = lax.dynamic_update_slice(g_out, w_out_shard_bf[None], (chip, 0, 0))
    g_cw = lax.dynamic_update_slice(g_cw, conv_w, (chip, 0, 0))
    w_pad = jnp.pad(g_in.transpose(1, 0, 2).reshape(D, n_in), ((0, 0), (0, P - n_in)))
    w_out_bf = g_out.reshape(2 * D, D)
    conv_full = g_cw.transpose(1, 0, 2).reshape(CONV_K, CD)
    cwx, cwb = conv_full[:, :D], conv_full[:, D:]
    cbx, cbb = conv_b[:, :D], conv_b[:, D:]
    dtb, alog, dsk = _pad_lanes(dt_bias), _pad_lanes(A_log), _pad_lanes(D_skip)
    qw2, kw2 = jnp.tile(q_norm_w, (1, 2)), jnp.tile(k_norm_w, (1, 2))

    x2 = x.reshape(T, D)
    proj, hn_t = _inproj(x2, norm_w, w_pad)
    proj3 = proj.reshape(Bl, L, P)
    o_sb = _attn_fwd(proj3, qw2, kw2, D)
    y_ssd, s_in = _ssd_fwd(proj3, cwx, cwb, cbx, cbb, dtb, alog, dsk, D)
    dout, dout_bf, mixed_t, do_sb, dy_ssd, dz_bf, dnw_out, loss_blk = _gate_out(
        x2, loss_target.reshape(T, D), o_sb.reshape(T, D), proj, y_ssd.reshape(T, D), sb_norm_w, ssd_norm_w, w_out_bf)

    dq, dk, dv, dqkw = _attn_bwd(proj3, o_sb, do_sb.reshape(Bl, L, D), qw2, kw2, D)
    dtail, dcwx, dcwb, dcbx, dcbb, misc = _ssd_bwd(
        proj3, s_in, dy_ssd.reshape(Bl, L, D), cwx, cwb, cbx, cbb, dtb, alog, dsk, D, P - 5 * D)
    dproj = [dq.reshape(T, D), dk.reshape(T, D), dv.reshape(T, D), dz_bf, dtail.reshape(T, P - 5 * D)]
    gw_in = _grad_w_in(hn_t, dproj)[:, :n_in]
    gw_out = _matmul(mixed_t, dout_bf, "grad_w_out")

    a_in = gw_in.reshape(2, D // 2, N_CHIPS, S).transpose(0, 2, 1, 3)
    a_out = gw_out.reshape(N_CHIPS, 2, R // 2, D).transpose(1, 0, 2, 3)
    r_in, r_out = _swap_core_halves(a_in, a_out)
    core1, chip1 = core.reshape(1), chip.reshape(1)
    h_in = _add_core_half(a_in, r_in, core1, "sum_cores_w_in")
    h_out = _add_core_half(a_out, r_out, core1, "sum_cores_w_out")
    grad_x2, dnw_in, s_in_, s_out_ = _dhn(dproj, w_pad, x2, dout, norm_w, h_in, h_out)
    gh_in = _add_chips(h_in, s_in_, chip1, "sum_chips_w_in")
    gh_out = _add_chips(h_out, s_out_, chip1, "sum_chips_w_out")
    f_in, f_out = _join_core_halves(gh_in, gh_out)
    g_w_in = lax.dynamic_update_slice(f_in, gh_in[None], (core, 0, 0)).reshape(D, S)
    g_w_out = lax.dynamic_update_slice(f_out, gh_out[None], (core, 0, 0)).reshape(R, D)

    small_shapes = [(1, D), (1, D), (1, D), (1, CD), (1, HEAD_DIM), (1, HEAD_DIM), (1, H), (1, H), (1, H)]
    g_small_local = [dnw_in[0:1], dnw_out[0:1], dnw_out[1:2], jnp.concatenate([dcbx, dcbb], axis=1),
                     dqkw[0:1, :HEAD_DIM] + dqkw[0:1, HEAD_DIM:], dqkw[1:2, :HEAD_DIM] + dqkw[1:2, HEAD_DIM:],
                     misc[0:1, :H], misc[1:2, :H], misc[2:3, :H]]
    packed = _pack(g_small_local + [jnp.concatenate([dcwx, dcwb], axis=1), loss_blk[0:1, 0:1]])
    red = _allreduce_small(packed)
    g_small = _unpack(red, small_shapes + [(CONV_K, CD), (1, 1)])
    g_conv_w = lax.dynamic_slice_in_dim(g_small[9], chip * CW, CW, axis=1)
    loss = g_small[10][0, 0]

    d_in, nm_in, nv_in = _adamw(w_in[0], g_w_in, m_w_in[0], v_w_in[0], "adamw_w_in")
    d_out, nm_out, nv_out = _adamw(w_out[0], g_w_out, m_w_out[0], v_w_out[0], "adamw_w_out")
    d_cw, nm_cw, nv_cw = _adamw(conv_w[0], g_conv_w, m_conv_w[0], v_conv_w[0], "adamw_conv_w")
    small_w = [norm_w, sb_norm_w, ssd_norm_w, conv_b, q_norm_w, k_norm_w, dt_bias, A_log, D_skip]
    small_m = [m_norm_w, m_sb_norm_w, m_ssd_norm_w, m_conv_b, m_q_norm_w, m_k_norm_w, m_dt_bias, m_A_log, m_D_skip]
    small_v = [v_norm_w, v_sb_norm_w, v_ssd_norm_w, v_conv_b, v_q_norm_w, v_k_norm_w, v_dt_bias, v_A_log, v_D_skip]
    d_s, nm_s, nv_s = _adamw(_pack(small_w), _pack(g_small[:9]), _pack(small_m), _pack(small_v), "adamw_small")
    d_s, nm_s, nv_s = (_unpack(t, small_shapes) for t in (d_s, nm_s, nv_s))

    def ordered(s, w_in_, conv_w_, w_out_):
        return [s[0], w_in_[None], s[4], s[5], conv_w_[None], s[3], s[6], s[7], s[8], s[1], s[2], w_out_[None]]

    return (loss, grad_x2.reshape(Bl, L, D),
            *ordered(g_small[:9], g_w_in, g_conv_w, g_w_out),
            *ordered(d_s, d_in, d_cw, d_out),
            *ordered(nm_s, nm_in, nm_cw, nm_out),
            *ordered(nv_s, nv_in, nv_cw, nv_out))
```

```python
import functools
import math

import jax
import jax.numpy as jnp
from jax import lax
from jax.experimental import pallas as pl
from jax.experimental.pallas import tpu as pltpu

F32 = jnp.float32
BF16 = jnp.bfloat16
EPS = 1e-6
HEAD_DIM = 64
PAIR = 2 * HEAD_DIM
LANES = 128
SSD_STATE = 128
SSD_GROUPS = 2
BLK = 128
PREP_BLOCKS = 4
Q_TOGETHER = 2
FIRST_LEFT = 2
UNDERFLOW = -105.0
CONV_K = 4
HALO = 8
N_CHIPS = 4
ADAM_LR, ADAM_B1, ADAM_B2, ADAM_EPS, ADAM_WD, ADAM_STEP = 0.001, 0.9, 0.999, 1e-08, 0.01, 10
VMEM_LIMIT_V7X = 56 * 1024 * 1024
MESH = pl.DeviceIdType.MESH
ANY = pl.BlockSpec(memory_space=pl.ANY)
NT = (((1,), (1,)), ((), ()))


def _params(sem=None):
    kw = dict(vmem_limit_bytes=VMEM_LIMIT_V7X)
    if sem is not None:
        kw["dimension_semantics"] = sem
    return pltpu.CompilerParams(**kw)


def _dot(a, b):
    return jnp.dot(a, b, preferred_element_type=F32)


def _dot_nt(a, b):
    return lax.dot_general(a, b, NT, preferred_element_type=F32)


def _dot_split(m, x):
    hi = x.astype(BF16)
    lo = (x - hi.astype(F32)).astype(BF16)
    return _dot(m, hi) + _dot(m, lo)


def _iota(shape, dim):
    return lax.broadcasted_iota(jnp.int32, shape, dim)


def _rowsum(x):
    return jnp.sum(x, axis=1, keepdims=True)


def _colsum(x):
    return jnp.sum(x, axis=0, keepdims=True)


def _sigmoid(x):
    return 1.0 / (1.0 + jnp.exp(-x))


def _softplus(x):
    return jnp.maximum(x, 0.0) + jnp.log(1.0 + jnp.exp(-jnp.abs(x)))


def _inproj(x2, norm_w, w_pad):
    T, D = x2.shape
    P = w_pad.shape[1]
    tm = min(1024, T)
    tn = 1024 if P % 1024 == 0 else 512

    def body(x_ref, nw_ref, w_ref, proj_ref, hnt_ref, hn_s):
        @pl.when(pl.program_id(1) == 0)
        def _():
            xv = x_ref[...]
            r = lax.rsqrt(jnp.mean(xv * xv, axis=-1, keepdims=True) + EPS)
            hn = xv * r * nw_ref[...]
            hn_s[...] = hn.astype(BF16)
            hnt_ref[...] = hn.T.astype(BF16)

        proj_ref[...] = _dot(hn_s[...], w_ref[...])

    return pl.pallas_call(
        body,
        name="inproj",
        grid=(T // tm, P // tn),
        in_specs=[
            pl.BlockSpec((tm, D), lambda i, j: (i, 0)),
            pl.BlockSpec((1, D), lambda i, j: (0, 0)),
            pl.BlockSpec((D, tn), lambda i, j: (0, j)),
        ],
        out_specs=[
            pl.BlockSpec((tm, tn), lambda i, j: (i, j)),
            pl.BlockSpec((D, tm), lambda i, j: (0, i)),
        ],
        out_shape=[jax.ShapeDtypeStruct((T, P), F32), jax.ShapeDtypeStruct((D, T), BF16)],
        scratch_shapes=[pltpu.VMEM((tm, D), BF16)],
        compiler_params=_params(("parallel", "arbitrary")),
    )(x2, norm_w, w_pad)


def _pair_ones():
    ri = ((_iota((2 * PAIR, PAIR), 0) % PAIR) >= HEAD_DIM).astype(jnp.int32)
    ci = (_iota((2 * PAIR, PAIR), 1) >= HEAD_DIM).astype(jnp.int32)
    return jnp.where(ri == ci, 1.0, 0.0).astype(BF16)


def _pair_rms(v, ones2):
    return lax.rsqrt(_split_dots([v * v], ones2)[0] * (1.0 / HEAD_DIM) + EPS)


def _pair_mean(v, ones2):
    return _split_dots([v], ones2)[0] * (1.0 / HEAD_DIM)


def _suffix_ones():
    ri = _iota((2 * BLK, 2 * BLK), 0) % BLK
    ci = _iota((2 * BLK, 2 * BLK), 1)
    return jnp.where((ci >= BLK) | (ri > ci), 1.0, 0.0).astype(BF16)


def _split_dots(xs, m2):
    his = [x.astype(BF16) for x in xs]
    los = [(x - hi.astype(F32)).astype(BF16) for x, hi in zip(xs, his)]
    return [_dot(jnp.concatenate([hi, lo], axis=1), m2) for hi, lo in zip(his, los)]


def _sb_tiles(streams, km_s, uo):
    tiles = [(s, u, h) for s, st in enumerate(streams) for u in range(len(st["kbs"])) for h in range(2)]
    z2s = {(s, u): _dot_nt(st["q"], km_s[kb]) for s, st in enumerate(streams) for u, kb in enumerate(st["kbs"])}
    zs = [z2s[s, u][:, h * BLK:(h + 1) * BLK] for s, u, h in tiles]
    es = [jnp.exp(-jnp.abs(z)) for z in zs]
    las = [jnp.minimum(z, 0.0) - jnp.log(1.0 + e) for z, e in zip(zs, es)]
    lks = [a - z for a, z in zip(las, zs)]
    masks = [streams[s]["masks"][u] for s, u, h in tiles]
    lks = [lk if m is None else jnp.where(m, lk, 0.0) for m, lk in zip(masks, lks)]
    css = _split_dots(lks, uo)
    rests = [list(st["rest"]) for st in streams]
    ws = []
    for (s, u, h), m, a, cs in zip(tiles, masks, las, css):
        w = jnp.exp(a + rests[s][h] + cs[:, :BLK])
        ws.append(w if m is None else jnp.where(m, w, 0.0))
        rests[s][h] = rests[s][h] + cs[:, BLK:]
    return tiles, zs, es, ws, rests


def _stream(q_pair, qi, n_left, diag, zero):
    return dict(q=q_pair, kbs=[qi - u for u in range(n_left + 1)], masks=[diag] + [None] * n_left, rest=[zero, zero])


def _row0(block):
    return block * BLK if isinstance(block, int) else pl.multiple_of(block * BLK, BLK)


def _pair_of(vals, tiles, s, u):
    return [v for v, t in zip(vals, tiles) if t[0] == s and t[1] == u]


def _block_groups(nq):
    assert nq % Q_TOGETHER == 0 and Q_TOGETHER >= FIRST_LEFT
    return list(range(Q_TOGETHER)), nq // Q_TOGETHER


def _attn_prep(src_ref, w_ref, dst_s, n_blocks, scale):
    per = math.gcd(PREP_BLOCKS, n_blocks)
    rows = per * BLK
    lo = _iota((rows, PAIR), 1) < HEAD_DIM
    ones2 = _pair_ones()

    def step(i, carry):
        r0 = pl.multiple_of(i * rows, rows)
        v = src_ref[0, pl.ds(r0, rows), :]
        if w_ref is not None:
            v = v * _pair_rms(v, ones2) * w_ref[...]
        if scale != 1.0:
            v = v * scale
        v0, v1 = jnp.where(lo, v, 0.0).astype(BF16), jnp.where(lo, 0.0, v).astype(BF16)
        for b in range(per):
            dst_s[i * per + b, 0:BLK, :] = v0[b * BLK:(b + 1) * BLK]
            dst_s[i * per + b, BLK:2 * BLK, :] = v1[b * BLK:(b + 1) * BLK]
        return carry

    lax.fori_loop(0, n_blocks // per, step, 0)


def _attn_fwd(proj3, qw2, kw2, D):
    Bl, L, _ = proj3.shape
    n_pair = D // PAIR
    nq = L // BLK
    scale = 1.0 / math.sqrt(HEAD_DIM)

    def body(q_ref, k_ref, v_ref, qw_ref, kw_ref, o_ref, qm_s, km_s, vm_s):
        uo = _suffix_ones()
        diag = _iota((BLK, BLK), 1) < _iota((BLK, BLK), 0)
        _attn_prep(q_ref, qw_ref, qm_s, nq, scale)
        _attn_prep(k_ref, kw_ref, km_s, nq, 1.0)
        _attn_prep(v_ref, None, vm_s, nq, 1.0)

        zero_c = jnp.zeros((BLK, BLK), F32)

        def q_of(qi):
            return qm_s[qi, 0:BLK, :] + qm_s[qi, BLK:2 * BLK, :]

        def values(streams, accs):
            tiles, _, _, ws, rests = _sb_tiles(streams, km_s, uo)
            wbs = [w.astype(BF16) for w in ws]
            accs = list(accs)
            for s, st in enumerate(streams):
                for u, kb in enumerate(st["kbs"]):
                    accs[s] = accs[s] + _dot(jnp.concatenate(_pair_of(wbs, tiles, s, u), axis=1), vm_s[kb])
            return accs, rests

        def group(qis, n_lefts):
            streams = [_stream(q_of(qi), qi, n, diag, zero_c) for qi, n in zip(qis, n_lefts)]
            accs, rests = values(streams, [jnp.zeros((BLK, PAIR), F32)] * len(qis))
            for qi, n, q, acc, rc in zip(qis, n_lefts, [st["q"] for st in streams], accs, rests):

                def sweep(state, n_blocks, q=q):
                    kb, rc0, rc1, acc1, _ = state
                    st = dict(q=q, kbs=[kb - u for u in range(n_blocks)], masks=[None] * n_blocks, rest=[rc0, rc1])
                    (acc1,), (r,) = values([st], [acc1])
                    return kb - n_blocks, r[0], r[1], acc1, jnp.maximum(jnp.max(r[0]), jnp.max(r[1]))

                state = (jnp.asarray(qi - n - 1, jnp.int32), rc[0], rc[1], acc, jnp.maximum(jnp.max(rc[0]), jnp.max(rc[1])))
                state = lax.while_loop(lambda t: (t[0] >= 1) & (t[4] >= UNDERFLOW), lambda t: sweep(t, 2), state)
                state = lax.while_loop(lambda t: (t[0] >= 0) & (t[4] >= UNDERFLOW), lambda t: sweep(t, 1), state)
                o_ref[0, pl.ds(_row0(qi), BLK), :] = state[3]

        head, n_groups = _block_groups(nq)
        group(head, [min(qi, FIRST_LEFT) for qi in head])

        def groups(g, carry):
            group([g * Q_TOGETHER + j for j in range(Q_TOGETHER)], [FIRST_LEFT] * Q_TOGETHER)
            return carry

        lax.fori_loop(1, n_groups, groups, 0)

    blk = lambda off: pl.BlockSpec((1, L, PAIR), lambda b, p: (b, 0, off + p))
    wspec = pl.BlockSpec((1, PAIR), lambda b, p: (0, 0))
    return pl.pallas_call(
        body,
        name="sb_attn_fwd",
        grid=(Bl, n_pair),
        in_specs=[blk(0), blk(n_pair), blk(2 * n_pair), wspec, wspec],
        out_specs=pl.BlockSpec((1, L, PAIR), lambda b, p: (b, 0, p)),
        out_shape=jax.ShapeDtypeStruct((Bl, L, D), F32),
        scratch_shapes=[pltpu.VMEM((nq, 2 * BLK, PAIR), BF16)] * 3,
        compiler_params=_params(("parallel", "parallel")),
    )(proj3, proj3, proj3, qw2, kw2)


def _attn_bwd(proj3, o3, do3, qw2, kw2, D):
    Bl, L, _ = proj3.shape
    n_pair = D // PAIR
    nq = L // BLK
    scale = 1.0 / math.sqrt(HEAD_DIM)

    def body(q_ref, k_ref, v_ref, o_ref, do_ref, qw_ref, kw_ref, dq_ref, dk_ref, dv_ref, dw_ref,
             qm_s, km_s, vm_s, dom_s, dq_s, dk_s, dv_s):
        uo = _suffix_ones()
        diag = _iota((BLK, BLK), 1) < _iota((BLK, BLK), 0)
        ones2 = _pair_ones()
        _attn_prep(q_ref, qw_ref, qm_s, nq, scale)
        _attn_prep(k_ref, kw_ref, km_s, nq, 1.0)
        _attn_prep(v_ref, None, vm_s, nq, 1.0)
        _attn_prep(do_ref, None, dom_s, nq, 1.0)

        @pl.when((pl.program_id(0) == 0) & (pl.program_id(1) == 0))
        def _():
            dw_ref[...] = jnp.zeros_like(dw_ref)

        def zero(i, carry):
            r0 = pl.multiple_of(i * BLK, BLK)
            dk_s[pl.ds(r0, BLK), :] = jnp.zeros((BLK, PAIR), F32)
            dv_s[pl.ds(r0, BLK), :] = jnp.zeros((BLK, PAIR), F32)
            return carry

        lax.fori_loop(0, nq, zero, 0)

        zero_c = jnp.zeros((BLK, BLK), F32)

        def tiles_bwd(streams, dqas):
            dw2s = {(s, u): _dot_nt(st["do"], vm_s[kb]) for s, st in enumerate(streams) for u, kb in enumerate(st["kbs"])}
            tiles, zs, es, ws, rests = _sb_tiles(streams, km_s, uo)
            dws = [dw2s[s, u][:, h * BLK:(h + 1) * BLK] for s, u, h in tiles]
            wfs = [w.astype(BF16).astype(F32) for w in ws]
            gs = [wf * dw for wf, dw in zip(wfs, dws)]
            gss = _split_dots(gs, uo)
            gcs = [list(st["g_rest"]) for st in streams]
            dzs = []
            for (s, u, h), z, e, g, gsum in zip(tiles, zs, es, gs, gss):
                g_before = streams[s]["delta"][h] - (gcs[s][h] + gsum[:, :BLK] + g)
                gcs[s][h] = gcs[s][h] + gsum[:, BLK:]
                r = 1.0 / (1.0 + e)
                er = e * r
                pos = z >= 0.0
                dz = g * jnp.where(pos, er, r) - g_before * jnp.where(pos, r, er)
                m = streams[s]["masks"][u]
                dzs.append(dz if m is None else jnp.where(m, dz, 0.0))
            wts = [wf.T.astype(BF16) for wf in wfs]
            dzts = [dz.T.astype(BF16) for dz in dzs]
            dzbs = [dz.astype(BF16) for dz in dzs]
            dqas = list(dqas)
            for s, st in enumerate(streams):
                for u, kb in enumerate(st["kbs"]):
                    c0 = _row0(kb)
                    dv_s[pl.ds(c0, BLK), :] += _dot(jnp.concatenate(_pair_of(wts, tiles, s, u), axis=1), dom_s[st["qi"]])
                    dk_s[pl.ds(c0, BLK), :] += _dot(jnp.concatenate(_pair_of(dzts, tiles, s, u), axis=1), qm_s[st["qi"]])
                    dqas[s] = dqas[s] + _dot(jnp.concatenate(_pair_of(dzbs, tiles, s, u), axis=1), km_s[kb])
            return dqas, rests, gcs

        def group(qis, n_lefts):
            streams = []
            for qi, n in zip(qis, n_lefts):
                o_blk = o_ref[0, pl.ds(_row0(qi), BLK), :]
                doms = [dom_s[qi, 0:BLK, :], dom_s[qi, BLK:2 * BLK, :]]
                st = _stream(qm_s[qi, 0:BLK, :] + qm_s[qi, BLK:2 * BLK, :], qi, n, diag, zero_c)
                st.update(qi=qi, do=doms[0] + doms[1], delta=[_rowsum(d.astype(F32) * o_blk) for d in doms],
                          g_rest=[zero_c, zero_c])
                streams.append(st)
            dqas, rests, gcs = tiles_bwd(streams, [jnp.zeros((BLK, PAIR), F32)] * len(qis))
            for qi, n, st0, dqa, rc, gc in zip(qis, n_lefts, streams, dqas, rests, gcs):

                def sweep(state, n_blocks, st0=st0):
                    kb, rc0, rc1, gc0, gc1, dqa1, _ = state
                    st = dict(st0, kbs=[kb - u for u in range(n_blocks)], masks=[None] * n_blocks, rest=[rc0, rc1],
                              g_rest=[gc0, gc1])
                    (dqa1,), (r,), (g,) = tiles_bwd([st], [dqa1])
                    return kb - n_blocks, r[0], r[1], g[0], g[1], dqa1, jnp.maximum(jnp.max(r[0]), jnp.max(r[1]))

                state = (jnp.asarray(qi - n - 1, jnp.int32), rc[0], rc[1], gc[0], gc[1], dqa,
                         jnp.maximum(jnp.max(rc[0]), jnp.max(rc[1])))
                state = lax.while_loop(lambda t: (t[0] >= 1) & (t[6] >= UNDERFLOW), lambda t: sweep(t, 2), state)
                state = lax.while_loop(lambda t: (t[0] >= 0) & (t[6] >= UNDERFLOW), lambda t: sweep(t, 1), state)
                dq_s[pl.ds(_row0(qi), BLK), :] = state[5] * scale

        head, n_groups = _block_groups(nq)
        group(head, [min(qi, FIRST_LEFT) for qi in head])

        def groups(g, carry):
            group([g * Q_TOGETHER + j for j in range(Q_TOGETHER)], [FIRST_LEFT] * Q_TOGETHER)
            return carry

        lax.fori_loop(1, n_groups, groups, 0)

        per = math.gcd(PREP_BLOCKS, nq)
        rows = per * BLK

        def finish(i, carry):
            r0 = pl.multiple_of(i * rows, rows)
            dwq, dwk = carry
            out = []
            for src_ref, w_ref, d_s in ((q_ref, qw_ref, dq_s), (k_ref, kw_ref, dk_s)):
                v = src_ref[0, pl.ds(r0, rows), :]
                r = _pair_rms(v, ones2)
                vh = v * r
                dy = d_s[pl.ds(r0, rows), :]
                dvh = dy * w_ref[...]
                out.append((r * (dvh - vh * _pair_mean(dvh * vh, ones2)), _colsum(dy * vh)))
            dq_ref[0, pl.ds(r0, rows), :] = out[0][0].astype(BF16)
            dk_ref[0, pl.ds(r0, rows), :] = out[1][0].astype(BF16)
            dv_ref[0, pl.ds(r0, rows), :] = dv_s[pl.ds(r0, rows), :].astype(BF16)
            return dwq + out[0][1], dwk + out[1][1]

        zrow = jnp.zeros((1, PAIR), F32)
        dwq, dwk = lax.fori_loop(0, nq // per, finish, (zrow, zrow))
        dw_ref[0:1, :] += dwq
        dw_ref[1:2, :] += dwk

    blk = lambda off: pl.BlockSpec((1, L, PAIR), lambda b, p: (b, 0, off + p))
    wspec = pl.BlockSpec((1, PAIR), lambda b, p: (0, 0))
    oblk = pl.BlockSpec((1, L, PAIR), lambda b, p: (b, 0, p))
    return pl.pallas_call(
        body,
        name="sb_attn_bwd",
        grid=(Bl, n_pair),
        in_specs=[blk(0), blk(n_pair), blk(2 * n_pair), oblk, oblk, wspec, wspec],
        out_specs=[oblk, oblk, oblk, pl.BlockSpec((8, PAIR), lambda b, p: (0, 0))],
        out_shape=[jax.ShapeDtypeStruct((Bl, L, D), BF16)] * 3 + [jax.ShapeDtypeStruct((8, PAIR), F32)],
        scratch_shapes=[pltpu.VMEM((nq, 2 * BLK, PAIR), BF16)] * 4 + [pltpu.VMEM((L, PAIR), F32)] * 3,
        compiler_params=_params(("arbitrary", "arbitrary")),
    )(proj3, proj3, proj3, o3, do3, qw2, kw2)


def _conv_pre(ext_s, halo_ref, raw_ref, w_ref, b_ref, first):
    ext_s[0:HALO, :] = jnp.where(first, 0.0, halo_ref[0])
    ext_s[HALO:HALO + BLK, :] = raw_ref[0]
    pre = b_ref[...]
    for i in range(CONV_K):
        pre = pre + ext_s[pl.ds(HALO - (CONV_K - 1 - i), BLK), :] * w_ref[i:i + 1, :]
    return pre


def _lane_col(m, lane, h):
    return _rowsum(jnp.where(lane == h, m, 0.0))


def _half_sums(row, lo1):
    return _rowsum(jnp.where(lo1, row, 0.0)), _rowsum(jnp.where(lo1, 0.0, row))


def _ssd_specs(Bl, L, D, rev):
    nc = L // BLK
    rows_per = BLK // HALO
    cidx = (lambda c: nc - 1 - c) if rev else (lambda c: c)
    xoff = 5
    boff = (6 * D) // 512
    doff = (6 * D + 512) // LANES
    prev = lambda c: jnp.maximum(cidx(c) * rows_per - 1, 0)
    specs = [
        pl.BlockSpec((1, BLK, D), lambda b, c: (b, cidx(c), xoff)),
        pl.BlockSpec((1, BLK, 512), lambda b, c: (b, cidx(c), boff)),
        pl.BlockSpec((1, HALO, D), lambda b, c: (b, prev(c), xoff)),
        pl.BlockSpec((1, HALO, 512), lambda b, c: (b, prev(c), boff)),
        pl.BlockSpec((1, BLK, LANES), lambda b, c: (b, cidx(c), doff)),
    ]
    full = lambda shape: pl.BlockSpec(shape, lambda b, c: (0,) * len(shape))
    specs += [full((CONV_K, D)), full((CONV_K, 512)), full((1, D)), full((1, 512)),
              full((1, LANES)), full((1, LANES)), full((1, LANES))]
    return specs, cidx


def _ssd_common(dtr_ref, dtb_ref, alog_ref, acs_s, acsT_s):
    ltri = jnp.where(_iota((BLK, BLK), 1) <= _iota((BLK, BLK), 0), 1.0, 0.0).astype(BF16)
    dtv = _softplus(dtr_ref[0] + dtb_ref[...])
    a = -jnp.exp(alog_ref[...])
    acs = _dot_split(ltri, dtv * a)
    acs_s[...] = acs
    acsT_s[...] = acs.T
    return dtv, a, acs


def _pair_terms(pr, acs, dtv, acs_s, lane, lo, lane1, lo1):
    h0, h1 = 2 * pr, 2 * pr + 1
    c0, c1 = _lane_col(acs, lane, h0), _lane_col(acs, lane, h1)
    d0, d1 = _lane_col(dtv, lane, h0), _lane_col(dtv, lane, h1)
    lastv = acs_s[BLK - 1:BLK, :]
    l0, l1 = _lane_col(lastv, lane1, h0), _lane_col(lastv, lane1, h1)
    return dict(h=(h0, h1), c=(c0, c1), last=(l0, l1), acs_p=jnp.where(lo, c0, c1), dt_p=jnp.where(lo, d0, d1),
                last_p=jnp.where(lo1, l0, l1))


def _decay_tiles(cc, row, tri, want_t):
    lm = jnp.where(tri, jnp.exp(jnp.where(tri, cc - row, 0.0)), 0.0)
    if not want_t:
        return lm, None
    tri_t = _iota((BLK, BLK), 1) >= _iota((BLK, BLK), 0)
    return lm, jnp.where(tri_t, jnp.exp(jnp.where(tri_t, row - cc, 0.0)), 0.0)


def _ssd_fwd(proj3, cwx, cwb, cbx, cbb, dtb, alog, dsk, D):
    Bl, L, _ = proj3.shape
    nc = L // BLK
    n_pair = D // PAIR
    pairs_per_group = n_pair // SSD_GROUPS
    specs, _ = _ssd_specs(Bl, L, D, False)

    def body(xr_ref, bcr_ref, xh_ref, bch_ref, dtr_ref, cwx_ref, cwb_ref, cbx_ref, cbb_ref, dtb_ref, alog_ref,
             dsk_ref, y_ref, sin_ref, st_s, extx_s, extb_s, acs_s, acsT_s):
        first = pl.program_id(1) == 0

        @pl.when(first)
        def _():
            st_s[...] = jnp.zeros_like(st_s)

        lane, lane1 = _iota((BLK, LANES), 1), _iota((1, LANES), 1)
        lo, lo1 = lane < HEAD_DIM, lane1 < HEAD_DIM
        tri = _iota((BLK, BLK), 1) <= _iota((BLK, BLK), 0)
        pre = _conv_pre(extx_s, xh_ref, xr_ref, cwx_ref, cbx_ref, first)
        ux = pre * _sigmoid(pre)
        pre = _conv_pre(extb_s, bch_ref, bcr_ref, cwb_ref, cbb_ref, first)
        ub = pre * _sigmoid(pre)
        dtv, a, acs = _ssd_common(dtr_ref, dtb_ref, alog_ref, acs_s, acsT_s)
        for g in range(SSD_GROUPS):
            bg = ub[:, g * SSD_STATE:(g + 1) * SSD_STATE]
            cb_ = ub[:, (SSD_GROUPS + g) * SSD_STATE:(SSD_GROUPS + g + 1) * SSD_STATE].astype(BF16)
            cbm = _dot_nt(cb_, bg.astype(BF16))
            btb = bg.T.astype(BF16)
            for pr in range(g * pairs_per_group, (g + 1) * pairs_per_group):
                t = _pair_terms(pr, acs, dtv, acs_s, lane, lo, lane1, lo1)
                xs_p = ux[:, pr * PAIR:(pr + 1) * PAIR]
                x_p = xs_p * t["dt_p"]
                st = st_s[pr]
                sin_ref[0, 0, pr] = st
                y = _dot(cb_, st.astype(BF16)) * jnp.exp(t["acs_p"])
                for k in range(2):
                    row = acsT_s[t["h"][k]:t["h"][k] + 1, :]
                    lm, _ = _decay_tiles(t["c"][k], row, tri, False)
                    xm = jnp.where(lo if k == 0 else ~lo, x_p, 0.0).astype(BF16)
                    y = y + _dot((cbm * lm).astype(BF16), xm)
                d_p = jnp.where(lo1, _lane_col(dsk_ref[...], lane1, t["h"][0]), _lane_col(dsk_ref[...], lane1, t["h"][1]))
                y_ref[0, :, pr * PAIR:(pr + 1) * PAIR] = y + d_p * xs_p
                xd = (x_p * jnp.exp(t["last_p"] - t["acs_p"])).astype(BF16)
                st_s[pr] = st * jnp.exp(t["last_p"]) + _dot(btb, xd)

    return pl.pallas_call(
        body,
        name="ssd_fwd",
        grid=(Bl, nc),
        in_specs=specs,
        out_specs=[
            pl.BlockSpec((1, BLK, D), lambda b, c: (b, c, 0)),
            pl.BlockSpec((1, 1, n_pair, SSD_STATE, PAIR), lambda b, c: (b, c, 0, 0, 0)),
        ],
        out_shape=[jax.ShapeDtypeStruct((Bl, L, D), F32),
                   jax.ShapeDtypeStruct((Bl, nc, n_pair, SSD_STATE, PAIR), F32)],
        scratch_shapes=[pltpu.VMEM((n_pair, SSD_STATE, PAIR), F32), pltpu.VMEM((HALO + BLK, D), F32),
                        pltpu.VMEM((HALO + BLK, 512), F32), pltpu.VMEM((BLK, LANES), F32),
                        pltpu.VMEM((LANES, BLK), F32)],
        compiler_params=_params(("arbitrary", "arbitrary")),
    )(proj3, proj3, proj3, proj3, proj3, cwx, cwb, cbx, cbb, dtb, alog, dsk)


def _ssd_bwd(proj3, s_in, dy3, cwx, cwb, cbx, cbb, dtb, alog, dsk, D, tail):
    Bl, L, _ = proj3.shape
    CD = D + 512
    nc = L // BLK
    n_pair = D // PAIR
    n_heads = 2 * n_pair
    pairs_per_group = n_pair // SSD_GROUPS
    specs, cidx = _ssd_specs(Bl, L, D, True)
    specs = specs + [
        pl.BlockSpec((1, 1, n_pair, SSD_STATE, PAIR), lambda b, c: (b, cidx(c), 0, 0, 0)),
        pl.BlockSpec((1, BLK, D), lambda b, c: (b, cidx(c), 0)),
    ]

    def body(xr_ref, bcr_ref, xh_ref, bch_ref, dtr_ref, cwx_ref, cwb_ref, cbx_ref, cbb_ref, dtb_ref, alog_ref,
             dsk_ref, sin_ref, dy_ref, dxbc_ref, dcwx_ref, dcwb_ref, dcbx_ref, dcbb_ref, misc_ref,
             dst_s, extx_s, extb_s, acs_s, acsT_s, dux_s, dub_s, e2x_s, e2b_s, nxx_s, nxb_s):
        step = pl.program_id(1)
        first = step == nc - 1
        last = step == 0

        @pl.when(last)
        def _():
            dst_s[...] = jnp.zeros_like(dst_s)
            nxx_s[...] = jnp.zeros_like(nxx_s)
            nxb_s[...] = jnp.zeros_like(nxb_s)

        @pl.when(last & (pl.program_id(0) == 0))
        def _():
            for r in (dcwx_ref, dcwb_ref, dcbx_ref, dcbb_ref, misc_ref):
                r[...] = jnp.zeros_like(r)

        lane, lane1 = _iota((BLK, LANES), 1), _iota((1, LANES), 1)
        lo, lo1 = lane < HEAD_DIM, lane1 < HEAD_DIM
        tri = _iota((BLK, BLK), 1) <= _iota((BLK, BLK), 0)
        prex = _conv_pre(extx_s, xh_ref, xr_ref, cwx_ref, cbx_ref, first)
        sgx = _sigmoid(prex)
        ux = prex * sgx
        preb = _conv_pre(extb_s, bch_ref, bcr_ref, cwb_ref, cbb_ref, first)
        sgb = _sigmoid(preb)
        ub = preb * sgb
        dtv, a, acs = _ssd_common(dtr_ref, dtb_ref, alog_ref, acs_s, acsT_s)
        dacs = jnp.zeros((BLK, LANES), F32)
        dlast = jnp.zeros((1, LANES), F32)
        ddt = jnp.zeros((BLK, LANES), F32)
        dd = jnp.zeros((1, LANES), F32)
        for g in range(SSD_GROUPS):
            bg = ub[:, g * SSD_STATE:(g + 1) * SSD_STATE]
            cg = ub[:, (SSD_GROUPS + g) * SSD_STATE:(SSD_GROUPS + g + 1) * SSD_STATE]
            bb, cb_ = bg.astype(BF16), cg.astype(BF16)
            cbm = _dot_nt(cb_, bb)
            cbt = _dot_nt(bb, cb_)
            ctb = cg.T.astype(BF16)
            dbg = jnp.zeros((BLK, SSD_STATE), F32)
            dcg = jnp.zeros((BLK, SSD_STATE), F32)
            for pr in range(g * pairs_per_group, (g + 1) * pairs_per_group):
                t = _pair_terms(pr, acs, dtv, acs_s, lane, lo, lane1, lo1)
                h0, h1 = t["h"]
                xs_p = ux[:, pr * PAIR:(pr + 1) * PAIR]
                dy_p = dy_ref[0, :, pr * PAIR:(pr + 1) * PAIR]
                x_p = xs_p * t["dt_p"]
                ea_p = jnp.exp(t["acs_p"])
                dte_p = jnp.exp(t["last_p"] - t["acs_p"])
                cd_p = jnp.exp(t["last_p"])
                st = sin_ref[0, 0, pr]
                dst = dst_s[pr]
                stb, dstb = st.astype(BF16), dst.astype(BF16)
                s0, s1 = _half_sums(_colsum(dy_p * xs_p), lo1)
                dd = dd + jnp.where(lane1 == h0, s0, 0.0) + jnp.where(lane1 == h1, s1, 0.0)
                d_p = jnp.where(lo1, _lane_col(dsk_ref[...], lane1, h0), _lane_col(dsk_ref[...], lane1, h1))
                dxs_p = d_p * dy_p
                dp = dy_p * ea_p
                dpb = dp.astype(BF16)
                yo = dp * _dot(cb_, stb)
                dcg = dcg + _dot_nt(dpb, stb)
                dst_off = _dot(ctb, dpb)
                dac = [_rowsum(jnp.where(lo, yo, 0.0)), _rowsum(jnp.where(lo, 0.0, yo))]
                s0, s1 = _half_sums(_colsum(dst * st), lo1)
                dl = [s0 * jnp.exp(t["last"][0]), s1 * jnp.exp(t["last"][1])]
                dxd = _dot(bb, dstb)
                dx_p = dxd * dte_p
                tt = dxd * x_p
                dbg = dbg + _dot_nt((x_p * dte_p).astype(BF16), dstb)
                for k, ddte in enumerate((_rowsum(jnp.where(lo, tt, 0.0)), _rowsum(jnp.where(lo, 0.0, tt)))):
                    ek = ddte * jnp.exp(t["last"][k] - t["c"][k])
                    dl[k] = dl[k] + _colsum(ek)
                    dac[k] = dac[k] - ek
                for k in range(2):
                    row = acsT_s[t["h"][k]:t["h"][k] + 1, :]
                    lm, lmt = _decay_tiles(t["c"][k], row, tri, True)
                    msk = lo if k == 0 else ~lo
                    xm = jnp.where(msk, x_p, 0.0).astype(BF16)
                    dym = jnp.where(msk, dy_p, 0.0).astype(BF16)
                    dm = _dot_nt(dym, xm)
                    dmt = _dot_nt(xm, dym)
                    mt = cbt * lmt
                    dx_p = dx_p + _dot(mt.astype(BF16), dym)
                    dac[k] = dac[k] + _rowsum(dm * (cbm * lm)) - _rowsum(dmt * mt)
                    dcg = dcg + _dot((dm * lm).astype(BF16), bb)
                    dbg = dbg + _dot((dmt * lmt).astype(BF16), cb_)
                dacs = dacs + jnp.where(lane == h0, dac[0], 0.0) + jnp.where(lane == h1, dac[1], 0.0)
                dlast = dlast + jnp.where(lane1 == h0, dl[0], 0.0) + jnp.where(lane1 == h1, dl[1], 0.0)
                dxs_p = dxs_p + dx_p * t["dt_p"]
                t3 = dx_p * xs_p
                ddt = ddt + jnp.where(lane == h0, _rowsum(jnp.where(lo, t3, 0.0)), 0.0) \
                    + jnp.where(lane == h1, _rowsum(jnp.where(lo, 0.0, t3)), 0.0)
                dux_s[:, pr * PAIR:(pr + 1) * PAIR] = dxs_p
                dst_s[pr] = dst * cd_p + dst_off
            dub_s[:, g * SSD_STATE:(g + 1) * SSD_STATE] = dbg
            dub_s[:, (SSD_GROUPS + g) * SSD_STATE:(SSD_GROUPS + g + 1) * SSD_STATE] = dcg
        dacs = dacs + jnp.where(_iota((BLK, LANES), 0) == BLK - 1, dlast, 0.0)
        utri = jnp.where(_iota((BLK, BLK), 1) >= _iota((BLK, BLK), 0), 1.0, 0.0).astype(BF16)
        dda = _dot_split(utri, dacs)
        ddt = ddt + dda * a
        ddtr = jnp.where(lane < n_heads, ddt * _sigmoid(dtr_ref[0] + dtb_ref[...]), 0.0)
        dxbc_ref[0, :, CD:CD + LANES] = ddtr.astype(BF16)
        dxbc_ref[0, :, CD + LANES:tail] = jnp.zeros((BLK, tail - CD - LANES), BF16)
        misc_ref[0:1, :] += _colsum(ddtr)
        misc_ref[1:2, :] += jnp.where(lane1 < n_heads, _colsum(dda * dtv) * a, 0.0)
        misc_ref[2:3, :] += dd
        for (du_s, pre, sg, ext_s, e2_s, nx_s, w_ref, dcw_ref, dcb_ref, c0, width) in (
                (dux_s, prex, sgx, extx_s, e2x_s, nxx_s, cwx_ref, dcwx_ref, dcbx_ref, 0, D),
                (dub_s, preb, sgb, extb_s, e2b_s, nxb_s, cwb_ref, dcwb_ref, dcbb_ref, D, 512)):
            dpre = du_s[...] * (sg * (1.0 + pre * (1.0 - sg)))
            dcb_ref[...] += _colsum(dpre)
            for i in range(CONV_K):
                dcw_ref[i:i + 1, :] += _colsum(dpre * ext_s[pl.ds(HALO - (CONV_K - 1 - i), BLK), :])
            e2_s[0:BLK, :] = dpre
            e2_s[BLK:BLK + HALO, :] = nx_s[...]
            dxr = jnp.zeros((BLK, width), F32)
            for i in range(CONV_K):
                dxr = dxr + e2_s[pl.ds(CONV_K - 1 - i, BLK), :] * w_ref[i:i + 1, :]
            dxbc_ref[0, :, c0:c0 + width] = dxr.astype(BF16)
            nx_s[...] = e2_s[0:HALO, :]

    full = lambda shape: pl.BlockSpec(shape, lambda b, c: (0,) * len(shape))
    return pl.pallas_call(
        body,
        name="ssd_bwd",
        grid=(Bl, nc),
        in_specs=specs,
        out_specs=[
            pl.BlockSpec((1, BLK, tail), lambda b, c: (b, cidx(c), 0)),
            full((CONV_K, D)), full((CONV_K, 512)), full((1, D)), full((1, 512)), full((8, LANES)),
        ],
        out_shape=[
            jax.ShapeDtypeStruct((Bl, L, tail), BF16),
            jax.ShapeDtypeStruct((CONV_K, D), F32), jax.ShapeDtypeStruct((CONV_K, 512), F32),
            jax.ShapeDtypeStruct((1, D), F32), jax.ShapeDtypeStruct((1, 512), F32),
            jax.ShapeDtypeStruct((8, LANES), F32),
        ],
        scratch_shapes=[
            pltpu.VMEM((n_pair, SSD_STATE, PAIR), F32),
            pltpu.VMEM((HALO + BLK, D), F32), pltpu.VMEM((HALO + BLK, 512), F32),
            pltpu.VMEM((BLK, LANES), F32), pltpu.VMEM((LANES, BLK), F32),
            pltpu.VMEM((BLK, D), F32), pltpu.VMEM((BLK, 512), F32),
            pltpu.VMEM((BLK + HALO, D), F32), pltpu.VMEM((BLK + HALO, 512), F32),
            pltpu.VMEM((HALO, D), F32), pltpu.VMEM((HALO, 512), F32),
        ],
        compiler_params=_params(("arbitrary", "arbitrary")),
    )(proj3, proj3, proj3, proj3, proj3, cwx, cwb, cbx, cbb, dtb, alog, dsk, s_in, dy3)


def _gate_out(x2, tgt2, o2, proj2, y2, sbw, ssw, w_out_bf):
    T, D = x2.shape
    tm = min(256, T)

    def body(x_ref, t_ref, o_ref, zs_ref, y_ref, zy_ref, sbw_ref, ssw_ref, wo_ref,
             dout_ref, doutb_ref, mixt_ref, do_ref, dy_ref, dz_ref, dnw_ref, loss_ref):
        @pl.when(pl.program_id(0) == 0)
        def _():
            dnw_ref[...] = jnp.zeros_like(dnw_ref)
            loss_ref[...] = jnp.zeros_like(loss_ref)

        def fwd(o, z, w):
            sg = _sigmoid(z)
            sl = z * sg
            g = o * sl
            r = lax.rsqrt(jnp.mean(g * g, axis=-1, keepdims=True) + EPS)
            n = g * r
            return sg, sl, r, n, n * w

        def bwd(dy, o, z, w, sg, sl, r, n):
            dn = dy * w
            dg = r * (dn - n * jnp.mean(dn * n, axis=-1, keepdims=True))
            return dg * sl, dg * o * (sg * (1.0 + z * (1.0 - sg))), _colsum(dy * n)

        o1, z1, w1 = o_ref[...], zs_ref[...], sbw_ref[...]
        o2_, z2, w2 = y_ref[...], zy_ref[...], ssw_ref[...]
        sg1, sl1, r1, n1, y1 = fwd(o1, z1, w1)
        sg2, sl2, r2, n2, y2_ = fwd(o2_, z2, w2)
        y1b, y2b = y1.astype(BF16), y2_.astype(BF16)
        mixt_ref[0:D, :] = y1.T.astype(BF16)
        mixt_ref[D:2 * D, :] = y2_.T.astype(BF16)
        out = x_ref[...] + (_dot(y1b, wo_ref[0:D, :]) + _dot(y2b, wo_ref[D:2 * D, :]))
        err = out - t_ref[...]
        loss_ref[...] += jnp.sum(err * err) * (0.5 / D)
        dout = err * (1.0 / D)
        dout_ref[...] = dout
        doutb = dout.astype(BF16)
        doutb_ref[...] = doutb
        do1, dz1, dw1 = bwd(_dot_nt(doutb, wo_ref[0:D, :]), o1, z1, w1, sg1, sl1, r1, n1)
        do2, dz2, dw2 = bwd(_dot_nt(doutb, wo_ref[D:2 * D, :]), o2_, z2, w2, sg2, sl2, r2, n2)
        do_ref[...] = do1
        dy_ref[...] = do2
        dz_ref[:, 0:D] = dz1.astype(BF16)
        dz_ref[:, D:2 * D] = dz2.astype(BF16)
        dnw_ref[0:1, :] += dw1
        dnw_ref[1:2, :] += dw2

    row = lambda col: pl.BlockSpec((tm, D), lambda i: (i, col))
    full = lambda shape: pl.BlockSpec(shape, lambda i: (0,) * len(shape))
    wide = pl.BlockSpec((tm, 2 * D), lambda i: (i, 0))
    return pl.pallas_call(
        body,
        name="gate_out",
        grid=(T // tm,),
        in_specs=[row(0), row(0), row(0), row(3), row(0), row(4), full((1, D)), full((1, D)), full((2 * D, D))],
        out_specs=[row(0), row(0), pl.BlockSpec((2 * D, tm), lambda i: (0, i)), row(0), row(0), wide,
                   full((8, D)), full((8, LANES))],
        out_shape=[
            jax.ShapeDtypeStruct((T, D), F32), jax.ShapeDtypeStruct((T, D), BF16),
            jax.ShapeDtypeStruct((2 * D, T), BF16), jax.ShapeDtypeStruct((T, D), F32),
            jax.ShapeDtypeStruct((T, D), F32), jax.ShapeDtypeStruct((T, 2 * D), BF16),
            jax.ShapeDtypeStruct((8, D), F32), jax.ShapeDtypeStruct((8, LANES), F32),
        ],
        compiler_params=_params(("arbitrary",)),
    )(x2, tgt2, o2, proj2, y2, proj2, sbw, ssw, w_out_bf)


def _piece_blocks(pieces, D):
    counts = [p.shape[1] // D for p in pieces]
    return [sum(counts[:i]) for i in range(len(counts))], counts


def _dhn(pieces, w_pad, x2, dout, norm_w, h_in, h_out):
    T, D = x2.shape
    tm = min(512, T)
    starts, counts = _piece_blocks(pieces, D)
    nk = sum(counts)
    ni = T // tm
    n_sem = 2 * (N_CHIPS - 1)
    assert nk * D == w_pad.shape[1]

    def body(*refs):
        p_refs = refs[:len(pieces)]
        (w_ref, x_ref, dout_ref, nw_ref, hin, hout, gx_ref, dnw_ref, rin, rout,
         acc_s, send_sems, recv_sems) = refs[len(pieces):]
        i, k = pl.program_id(0), pl.program_id(1)

        def scatter():
            x, y, c, chips = _place()
            return [pltpu.make_async_remote_copy(
                src_ref=s.at[2 * px + py], dst_ref=d.at[j], send_sem=send_sems.at[2 * j + m],
                recv_sem=recv_sems.at[2 * j + m], device_id=(px, py, c), device_id_type=MESH)
                for j, (px, py) in enumerate(chips) for m, (s, d) in enumerate(((hin, rin), (hout, rout)))]

        @pl.when((i == 0) & (k == 0))
        def _():
            for cp in scatter():
                cp.start()

        @pl.when((i == ni - 1) & (k == nk - 1))
        def _():
            for cp in scatter():
                cp.wait()

        @pl.when((i == 0) & (k == 0))
        def _():
            dnw_ref[...] = jnp.zeros_like(dnw_ref)

        @pl.when(k == 0)
        def _():
            acc_s[...] = jnp.zeros_like(acc_s)

        for p_ref, s, n in zip(p_refs, starts, counts):
            @pl.when((k >= s) & (k < s + n))
            def _(p_ref=p_ref):
                acc_s[...] += _dot_nt(p_ref[...], w_ref[...])

        @pl.when(k == nk - 1)
        def _():
            xv = x_ref[...]
            r = lax.rsqrt(jnp.mean(xv * xv, axis=-1, keepdims=True) + EPS)
            xh = xv * r
            dhn = acc_s[...]
            dxh = dhn * nw_ref[...]
            gx_ref[...] = dout_ref[...] + r * (dxh - xh * jnp.mean(dxh * xh, axis=-1, keepdims=True))
            dnw_ref[0:1, :] += _colsum(dhn * xh)

    return pl.pallas_call(
        body,
        name="dhn",
        grid=(T // tm, nk),
        in_specs=[pl.BlockSpec((tm, D), lambda i, k, s=s, n=n: (i, jnp.clip(k - s, 0, n - 1)))
                  for s, n in zip(starts, counts)] + [
            pl.BlockSpec((D, D), lambda i, k: (0, k)),
            pl.BlockSpec((tm, D), lambda i, k: (i, 0)),
            pl.BlockSpec((tm, D), lambda i, k: (i, 0)),
            pl.BlockSpec((1, D), lambda i, k: (0, 0)),
            ANY, ANY,
        ],
        out_specs=[pl.BlockSpec((tm, D), lambda i, k: (i, 0)), pl.BlockSpec((8, D), lambda i, k: (0, 0)), ANY, ANY],
        out_shape=[jax.ShapeDtypeStruct((T, D), F32), jax.ShapeDtypeStruct((8, D), F32),
                   jax.ShapeDtypeStruct((N_CHIPS - 1,) + h_in.shape[1:], F32),
                   jax.ShapeDtypeStruct((N_CHIPS - 1,) + h_out.shape[1:], F32)],
        scratch_shapes=[pltpu.VMEM((tm, D), F32), pltpu.SemaphoreType.DMA((n_sem,)), pltpu.SemaphoreType.DMA((n_sem,))],
        compiler_params=_params(("arbitrary", "arbitrary")),
    )(*pieces, w_pad, x2, dout, norm_w, h_in, h_out)


def _grad_w_in(hn_t, pieces):
    D, T = hn_t.shape
    tk = min(1024, T)
    starts, counts = _piece_blocks(pieces, D)

    def body(*refs):
        a_ref, p_refs, o_ref = refs[0], refs[1:-1], refs[-1]
        j = pl.program_id(0)

        @pl.when(pl.program_id(1) == 0)
        def _():
            o_ref[...] = jnp.zeros_like(o_ref)

        for p_ref, s, n in zip(p_refs, starts, counts):
            @pl.when((j >= s) & (j < s + n))
            def _(p_ref=p_ref):
                o_ref[...] += _dot(a_ref[...], p_ref[...])

    def piece_spec(s, n):
        return pl.BlockSpec((tk, D), lambda j, k: (jnp.where((j >= s) & (j < s + n), k, 0), jnp.clip(j - s, 0, n - 1)))

    return pl.pallas_call(
        body,
        name="grad_w_in",
        grid=(sum(counts), T // tk),
        in_specs=[pl.BlockSpec((D, tk), lambda j, k: (0, k))] + [piece_spec(s, n) for s, n in zip(starts, counts)],
        out_specs=pl.BlockSpec((D, D), lambda j, k: (0, j)),
        out_shape=jax.ShapeDtypeStruct((D, sum(counts) * D), F32),
        compiler_params=_params(("parallel", "arbitrary")),
    )(hn_t, *pieces)


def _matmul(a, b, name):
    M, K = a.shape
    N = b.shape[1]
    tm = min(1024, M)
    tn = 1024 if N % 1024 == 0 else (512 if N % 512 == 0 else N)
    tk = min(512, K)

    def body(a_ref, b_ref, o_ref):
        @pl.when(pl.program_id(2) == 0)
        def _():
            o_ref[...] = jnp.zeros_like(o_ref)

        o_ref[...] += _dot(a_ref[...], b_ref[...])

    return pl.pallas_call(
        body,
        name=name,
        grid=(M // tm, N // tn, K // tk),
        in_specs=[pl.BlockSpec((tm, tk), lambda i, j, k: (i, k)), pl.BlockSpec((tk, tn), lambda i, j, k: (k, j))],
        out_specs=pl.BlockSpec((tm, tn), lambda i, j, k: (i, j)),
        out_shape=jax.ShapeDtypeStruct((M, N), F32),
        compiler_params=_params(("parallel", "parallel", "arbitrary")),
    )(a, b)


def _adamw(w, g, m, v, name):
    R, C = w.shape
    tr = 256 if R % 256 == 0 else R
    c1 = 1.0 - ADAM_B1 ** ADAM_STEP
    c2 = 1.0 - ADAM_B2 ** ADAM_STEP

    def body(w_ref, g_ref, m_ref, v_ref, d_ref, nm_ref, nv_ref):
        gv = g_ref[...]
        m_new = ADAM_B1 * m_ref[...] + (1.0 - ADAM_B1) * gv
        v_new = ADAM_B2 * v_ref[...] + (1.0 - ADAM_B2) * (gv * gv)
        d_ref[...] = -ADAM_LR * ((m_new / c1) / (jnp.sqrt(v_new / c2) + ADAM_EPS) + ADAM_WD * w_ref[...])
        nm_ref[...] = m_new
        nv_ref[...] = v_new

    spec = pl.BlockSpec((tr, C), lambda i: (i, 0))
    return pl.pallas_call(
        body,
        name=name,
        grid=(R // tr,),
        in_specs=[spec] * 4,
        out_specs=[spec] * 3,
        out_shape=[jax.ShapeDtypeStruct((R, C), F32)] * 3,
        compiler_params=_params(("parallel",)),
    )(w, g, m, v)


def _add_core_half(a, recv, core, name):
    _, n, h, S = a.shape
    th = 256 if h % 256 == 0 else h

    def body(c_ref, a_ref, r_ref, o_ref):
        o_ref[...] = a_ref[...] + r_ref[...]

    return pl.pallas_call(
        body,
        name=name,
        grid_spec=pltpu.PrefetchScalarGridSpec(
            num_scalar_prefetch=1,
            grid=(n, h // th),
            in_specs=[
                pl.BlockSpec((None, None, th, S), lambda p, i, c: (c[0], p, i, 0)),
                pl.BlockSpec((None, th, S), lambda p, i, c: (p, i, 0)),
            ],
            out_specs=pl.BlockSpec((None, th, S), lambda p, i, c: (p, i, 0)),
        ),
        out_shape=jax.ShapeDtypeStruct((n, h, S), F32),
        compiler_params=_params(("parallel", "parallel")),
    )(core, a, recv)


def _add_chips(hsum, recv, chip, name):
    _, h, S = hsum.shape
    th = 256 if h % 256 == 0 else h

    def body(c_ref, a_ref, r_ref, o_ref):
        o_ref[...] = ((a_ref[...] + r_ref[0]) + r_ref[1]) + r_ref[2]

    return pl.pallas_call(
        body,
        name=name,
        grid_spec=pltpu.PrefetchScalarGridSpec(
            num_scalar_prefetch=1,
            grid=(h // th,),
            in_specs=[
                pl.BlockSpec((None, th, S), lambda i, c: (c[0], i, 0)),
                pl.BlockSpec((N_CHIPS - 1, th, S), lambda i, c: (0, i, 0)),
            ],
            out_specs=pl.BlockSpec((th, S), lambda i, c: (i, 0)),
        ),
        out_shape=jax.ShapeDtypeStruct((h, S), F32),
        compiler_params=_params(("parallel",)),
    )(chip, hsum, recv)


def _place():
    x, y, c = lax.axis_index("x"), lax.axis_index("y"), lax.axis_index("c")
    other_chips = [(1 - x, y), (x, 1 - y), (1 - x, 1 - y)]
    return x, y, c, other_chips


def _allgather_weights(w_in_bf, w_out_bf, conv_w):
    D, S = w_in_bf.shape
    R = w_out_bf.shape[0]
    n_ici, n_fwd = 3 * (N_CHIPS - 1), 2 * (N_CHIPS - 1)

    def body(win, wout, cw, gin, gout, gcw, send_sems, recv_sems):
        x, y, c, chips = _place()
        me = 2 * x + y
        sibling = (x, y, 1 - c)
        hin, hout = D // 2, R // 2

        def halves(chip_idx):
            return (gin.at[chip_idx, pl.ds(c * hin, hin)], gout.at[chip_idx, pl.ds(c * hout, hout)])

        def rcopy(k, src, dst, to):
            return pltpu.make_async_remote_copy(src_ref=src, dst_ref=dst, send_sem=send_sems.at[k],
                                                recv_sem=recv_sems.at[k], device_id=to, device_id_type=MESH)

        my_in, my_out = halves(me)
        src_in, src_out = win.at[pl.ds(c * hin, hin)], wout.at[pl.ds(c * hout, hout)]
        sends = []
        for j, chip in enumerate(chips):
            to = (*chip, c)
            sends += [rcopy(3 * j, src_in, my_in, to), rcopy(3 * j + 1, src_out, my_out, to),
                      rcopy(3 * j + 2, cw, gcw.at[me], to)]
        for cp in sends:
            cp.start()
        passed = []
        for j, (px, py) in enumerate(chips):
            their_in, their_out = halves(2 * px + py)
            rcopy(3 * j, their_in, their_in, sibling).wait_recv()
            rcopy(3 * j + 1, their_out, their_out, sibling).wait_recv()
            rcopy(3 * j + 2, cw, gcw.at[2 * px + py], sibling).wait_recv()
            fw = [rcopy(n_ici + 2 * j, their_in, their_in, sibling), rcopy(n_ici + 2 * j + 1, their_out, their_out, sibling)]
            for cp in fw:
                cp.start()
            passed += fw
        for j, (px, py) in enumerate(chips):
            oin = gin.at[2 * px + py, pl.ds((1 - c) * hin, hin)]
            oout = gout.at[2 * px + py, pl.ds((1 - c) * hout, hout)]
            rcopy(n_ici + 2 * j, oin, oin, sibling).wait_recv()
            rcopy(n_ici + 2 * j + 1, oout, oout, sibling).wait_recv()
        for cp in sends + passed:
            cp.wait_send()

    return pl.pallas_call(
        body,
        name="allgather_weights",
        in_specs=[ANY, ANY, ANY],
        out_specs=[ANY, ANY, ANY],
        out_shape=[jax.ShapeDtypeStruct((N_CHIPS, D, S), BF16), jax.ShapeDtypeStruct((N_CHIPS, R, D), BF16),
                   jax.ShapeDtypeStruct((N_CHIPS,) + conv_w.shape, F32)],
        scratch_shapes=[pltpu.SemaphoreType.DMA((n_ici + n_fwd,)), pltpu.SemaphoreType.DMA((n_ici + n_fwd,))],
    )(w_in_bf, w_out_bf, conv_w)


def _allreduce_small(packed):
    R = packed.shape[0]
    n_dev = 2 * N_CHIPS

    def body(p_ref, o_ref, buf, send_sems, recv_sems):
        x, y, c, _ = _place()
        me = 4 * x + 2 * y + c
        buf[me] = p_ref[...]
        copies = []
        for k in range(1, n_dev):
            px = 1 - x if k & 4 else x
            py = 1 - y if k & 2 else y
            pc = 1 - c if k & 1 else c
            copies.append((pltpu.make_async_remote_copy(
                src_ref=buf.at[me], dst_ref=buf.at[me], send_sem=send_sems.at[k - 1], recv_sem=recv_sems.at[k - 1],
                device_id=(px, py, pc), device_id_type=MESH), 4 * px + 2 * py + pc, (px, py, pc)))
        for cp, _, _ in copies:
            cp.start()
        for k, (_, peer, to) in enumerate(copies):
            pltpu.make_async_remote_copy(
                src_ref=buf.at[peer], dst_ref=buf.at[peer], send_sem=send_sems.at[k], recv_sem=recv_sems.at[k],
                device_id=to, device_id_type=MESH).wait_recv()
        for cp, _, _ in copies:
            cp.wait_send()
        acc = buf[0]
        for d in range(1, n_dev):
            acc = acc + buf[d]
        o_ref[...] = acc

    vm = pl.BlockSpec(memory_space=pltpu.VMEM)
    return pl.pallas_call(
        body,
        name="allreduce_small",
        in_specs=[vm],
        out_specs=vm,
        out_shape=jax.ShapeDtypeStruct((R, LANES), F32),
        scratch_shapes=[pltpu.VMEM((n_dev, R, LANES), F32), pltpu.SemaphoreType.DMA((n_dev - 1,)),
                        pltpu.SemaphoreType.DMA((n_dev - 1,))],
    )(packed)


def _swap_core_halves(a_in, a_out):
    def body(ain, aout, rin, rout, send_sems, recv_sems):
        x, y, c, _ = _place()
        cps = [pltpu.make_async_remote_copy(src_ref=s.at[1 - c], dst_ref=d, send_sem=send_sems.at[k],
                                            recv_sem=recv_sems.at[k], device_id=(x, y, 1 - c), device_id_type=MESH)
               for k, (s, d) in enumerate(((ain, rin), (aout, rout)))]
        for cp in cps:
            cp.start()
        for cp in cps:
            cp.wait()

    return pl.pallas_call(
        body,
        name="reduce_core_swap",
        in_specs=[ANY, ANY],
        out_specs=[ANY, ANY],
        out_shape=[jax.ShapeDtypeStruct(a_in.shape[1:], F32), jax.ShapeDtypeStruct(a_out.shape[1:], F32)],
        scratch_shapes=[pltpu.SemaphoreType.DMA((2,)), pltpu.SemaphoreType.DMA((2,))],
    )(a_in, a_out)


def _join_core_halves(g_in, g_out):
    def body(gin, gout, fin, fout, send_sems, recv_sems):
        x, y, c, _ = _place()
        cps = [pltpu.make_async_remote_copy(src_ref=s, dst_ref=d.at[c], send_sem=send_sems.at[k],
                                            recv_sem=recv_sems.at[k], device_id=(x, y, 1 - c), device_id_type=MESH)
               for k, (s, d) in enumerate(((gin, fin), (gout, fout)))]
        for cp in cps:
            cp.start()
        for k, (s, d) in enumerate(((gin, fin), (gout, fout))):
            pltpu.make_async_remote_copy(src_ref=s, dst_ref=d.at[1 - c], send_sem=send_sems.at[k],
                                         recv_sem=recv_sems.at[k], device_id=(x, y, 1 - c),
                                         device_id_type=MESH).wait_recv()
        for cp in cps:
            cp.wait_send()

    return pl.pallas_call(
        body,
        name="reduce_core_join",
        in_specs=[ANY, ANY],
        out_specs=[ANY, ANY],
        out_shape=[jax.ShapeDtypeStruct((2,) + g_in.shape, F32), jax.ShapeDtypeStruct((2,) + g_out.shape, F32)],
        scratch_shapes=[pltpu.SemaphoreType.DMA((2,)), pltpu.SemaphoreType.DMA((2,))],
    )(g_in, g_out)


def _pack(arrays):
    rows = []
    for a in arrays:
        flat = a.reshape(-1).astype(F32)
        n = -(-flat.shape[0] // LANES) * LANES
        rows.append(jnp.pad(flat, (0, n - flat.shape[0])).reshape(-1, LANES))
    out = jnp.concatenate(rows, axis=0)
    return jnp.pad(out, ((0, -out.shape[0] % 8), (0, 0)))


def _unpack(packed, shapes):
    out, r = [], 0
    for shp in shapes:
        n = math.prod(shp)
        nr = -(-n // LANES)
        out.append(packed[r:r + nr].reshape(-1)[:n].reshape(shp))
        r += nr
    return out


def _pad_lanes(a):
    return jnp.pad(a, ((0, 0), (0, LANES - a.shape[1])))


def kernel(x, norm_w, w_in, q_norm_w, k_norm_w, conv_w, conv_b, dt_bias, A_log, D_skip, sb_norm_w, ssd_norm_w, w_out, loss_target, m_norm_w, m_w_in, m_q_norm_w, m_k_norm_w, m_conv_w, m_conv_b, m_dt_bias, m_A_log, m_D_skip, m_sb_norm_w, m_ssd_norm_w, m_w_out, v_norm_w, v_w_in, v_q_norm_w, v_k_norm_w, v_conv_w, v_conv_b, v_dt_bias, v_A_log, v_D_skip, v_sb_norm_w, v_ssd_norm_w, v_w_out):
    Bl, L, D = x.shape
    T = Bl * L
    S = w_in.shape[2]
    R = w_out.shape[1]
    CW = conv_w.shape[2]
    n_in = N_CHIPS * S
    CD = D + 2 * SSD_GROUPS * SSD_STATE
    H = D // HEAD_DIM
    n_main = 6 * D + 512
    P = -(-(n_main + LANES) // 1024) * 1024
    assert n_in == n_main + H and CD == N_CHIPS * CW and 2 * D == N_CHIPS * R and CD == D + 512
    chip = (2 * lax.axis_index("x") + lax.axis_index("y")).astype(jnp.int32)
    core = lax.axis_index("c").astype(jnp.int32)

    w_in_bf, w_out_shard_bf = w_in[0].astype(BF16), w_out[0].astype(BF16)
    g_in, g_out, g_cw = _allgather_weights(w_in_bf, w_out_shard_bf, conv_w[0])
    g_in = lax.dynamic_update_slice(g_in, w_in_bf[None], (chip, 0, 0))
    g_out = lax.dynamic_update_slice(g_out, w_out_shard_bf[None], (chip, 0, 0))
    g_cw = lax.dynamic_update_slice(g_cw, conv_w, (chip, 0, 0))
    w_pad = jnp.pad(g_in.transpose(1, 0, 2).reshape(D, n_in), ((0, 0), (0, P - n_in)))
    w_out_bf = g_out.reshape(2 * D, D)
    conv_full = g_cw.transpose(1, 0, 2).reshape(CONV_K, CD)
    cwx, cwb = conv_full[:, :D], conv_full[:, D:]
    cbx, cbb = conv_b[:, :D], conv_b[:, D:]
    dtb, alog, dsk = _pad_lanes(dt_bias), _pad_lanes(A_log), _pad_lanes(D_skip)
    qw2, kw2 = jnp.tile(q_norm_w, (1, 2)), jnp.tile(k_norm_w, (1, 2))

    x2 = x.reshape(T, D)
    proj, hn_t = _inproj(x2, norm_w, w_pad)
    proj3 = proj.reshape(Bl, L, P)
    o_sb = _attn_fwd(proj3, qw2, kw2, D)
    y_ssd, s_in = _ssd_fwd(proj3, cwx, cwb, cbx, cbb, dtb, alog, dsk, D)
    dout, dout_bf, mixed_t, do_sb, dy_ssd, dz_bf, dnw_out, loss_blk = _gate_out(
        x2, loss_target.reshape(T, D), o_sb.reshape(T, D), proj, y_ssd.reshape(T, D), sb_norm_w, ssd_norm_w, w_out_bf)

    dq, dk, dv, dqkw = _attn_bwd(proj3, o_sb, do_sb.reshape(Bl, L, D), qw2, kw2, D)
    dtail, dcwx, dcwb, dcbx, dcbb, misc = _ssd_bwd(
        proj3, s_in, dy_ssd.reshape(Bl, L, D), cwx, cwb, cbx, cbb, dtb, alog, dsk, D, P - 5 * D)
    dproj = [dq.reshape(T, D), dk.reshape(T, D), dv.reshape(T, D), dz_bf, dtail.reshape(T, P - 5 * D)]
    gw_in = _grad_w_in(hn_t, dproj)[:, :n_in]
    gw_out = _matmul(mixed_t, dout_bf, "grad_w_out")

    a_in = gw_in.reshape(2, D // 2, N_CHIPS, S).transpose(0, 2, 1, 3)
    a_out = gw_out.reshape(N_CHIPS, 2, R // 2, D).transpose(1, 0, 2, 3)
    r_in, r_out = _swap_core_halves(a_in, a_out)
    core1, chip1 = core.reshape(1), chip.reshape(1)
    h_in = _add_core_half(a_in, r_in, core1, "sum_cores_w_in")
    h_out = _add_core_half(a_out, r_out, core1, "sum_cores_w_out")
    grad_x2, dnw_in, s_in_, s_out_ = _dhn(dproj, w_pad, x2, dout, norm_w, h_in, h_out)
    gh_in = _add_chips(h_in, s_in_, chip1, "sum_chips_w_in")
    gh_out = _add_chips(h_out, s_out_, chip1, "sum_chips_w_out")
    f_in, f_out = _join_core_halves(gh_in, gh_out)
    g_w_in = lax.dynamic_update_slice(f_in, gh_in[None], (core, 0, 0)).reshape(D, S)
    g_w_out = lax.dynamic_update_slice(f_out, gh_out[None], (core, 0, 0)).reshape(R, D)

    small_shapes = [(1, D), (1, D), (1, D), (1, CD), (1, HEAD_DIM), (1, HEAD_DIM), (1, H), (1, H), (1, H)]
    g_small_local = [dnw_in[0:1], dnw_out[0:1], dnw_out[1:2], jnp.concatenate([dcbx, dcbb], axis=1),
                     dqkw[0:1, :HEAD_DIM] + dqkw[0:1, HEAD_DIM:], dqkw[1:2, :HEAD_DIM] + dqkw[1:2, HEAD_DIM:],
                     misc[0:1, :H], misc[1:2, :H], misc[2:3, :H]]
    packed = _pack(g_small_local + [jnp.concatenate([dcwx, dcwb], axis=1), loss_blk[0:1, 0:1]])
    red = _allreduce_small(packed)
    g_small = _unpack(red, small_shapes + [(CONV_K, CD), (1, 1)])
    g_conv_w = lax.dynamic_slice_in_dim(g_small[9], chip * CW, CW, axis=1)
    loss = g_small[10][0, 0]

    d_in, nm_in, nv_in = _adamw(w_in[0], g_w_in, m_w_in[0], v_w_in[0], "adamw_w_in")
    d_out, nm_out, nv_out = _adamw(w_out[0], g_w_out, m_w_out[0], v_w_out[0], "adamw_w_out")
    d_cw, nm_cw, nv_cw = _adamw(conv_w[0], g_conv_w, m_conv_w[0], v_conv_w[0], "adamw_conv_w")
    small_w = [norm_w, sb_norm_w, ssd_norm_w, conv_b, q_norm_w, k_norm_w, dt_bias, A_log, D_skip]
    small_m = [m_norm_w, m_sb_norm_w, m_ssd_norm_w, m_conv_b, m_q_norm_w, m_k_norm_w, m_dt_bias, m_A_log, m_D_skip]
    small_v = [v_norm_w, v_sb_norm_w, v_ssd_norm_w, v_conv_b, v_q_norm_w, v_k_norm_w, v_dt_bias, v_A_log, v_D_skip]
    d_s, nm_s, nv_s = _adamw(_pack(small_w), _pack(g_small[:9]), _pack(small_m), _pack(small_v), "adamw_small")
    d_s, nm_s, nv_s = (_unpack(t, small_shapes) for t in (d_s, nm_s, nv_s))

    def ordered(s, w_in_, conv_w_, w_out_):
        return [s[0], w_in_[None], s[4], s[5], conv_w_[None], s[3], s[6], s[7], s[8], s[1], s[2], w_out_[None]]

    return (loss, grad_x2.reshape(Bl, L, D),
            *ordered(g_small[:9], g_w_in, g_conv_w, g_w_out),
            *ordered(d_s, d_in, d_cw, d_out),
            *ordered(nm_s, nm_in, nm_cw, nm_out),
            *ordered(nv_s, nv_in, nv_cw, nv_out))
```

```python
import functools
import math

import jax
import jax.numpy as jnp
from jax import lax
from jax.experimental import pallas as pl
from jax.experimental.pallas import tpu as pltpu

F32 = jnp.float32
BF16 = jnp.bfloat16
EPS = 1e-6
HEAD_DIM = 64
PAIR = 2 * HEAD_DIM
LANES = 128
SSD_STATE = 128
SSD_GROUPS = 2
BLK = 128
PREP_BLOCKS = 4
Q_TOGETHER = 2
FIRST_LEFT = 2
UNDERFLOW = -105.0
CONV_K = 4
HALO = 8
N_CHIPS = 4
ADAM_LR, ADAM_B1, ADAM_B2, ADAM_EPS, ADAM_WD, ADAM_STEP = 0.001, 0.9, 0.999, 1e-08, 0.01, 10
VMEM_LIMIT_V7X = 56 * 1024 * 1024
MESH = pl.DeviceIdType.MESH
ANY = pl.BlockSpec(memory_space=pl.ANY)
NT = (((1,), (1,)), ((), ()))


def _params(sem=None):
    kw = dict(vmem_limit_bytes=VMEM_LIMIT_V7X)
    if sem is not None:
        kw["dimension_semantics"] = sem
    return pltpu.CompilerParams(**kw)


def _dot(a, b):
    return jnp.dot(a, b, preferred_element_type=F32)


def _dot_nt(a, b):
    return lax.dot_general(a, b, NT, preferred_element_type=F32)


def _dot_split(m, x):
    hi = x.astype(BF16)
    lo = (x - hi.astype(F32)).astype(BF16)
    return _dot(m, hi) + _dot(m, lo)


def _iota(shape, dim):
    return lax.broadcasted_iota(jnp.int32, shape, dim)


def _rowsum(x):
    return jnp.sum(x, axis=1, keepdims=True)


def _colsum(x):
    return jnp.sum(x, axis=0, keepdims=True)


def _sigmoid(x):
    return 1.0 / (1.0 + jnp.exp(-x))


def _softplus(x):
    return jnp.maximum(x, 0.0) + jnp.log(1.0 + jnp.exp(-jnp.abs(x)))


def _inproj(x2, norm_w, w_pad):
    T, D = x2.shape
    P = w_pad.shape[1]
    tm = min(1024, T)
    tn = 1024 if P % 1024 == 0 else 512
    nj = P // tn

    def body(x_ref, nw_ref, w_ref, proj_ref, hnt_ref, wt_ref, hn_s):
        @pl.when(pl.program_id(1) == 0)
        def _():
            xv = x_ref[...]
            r = lax.rsqrt(jnp.mean(xv * xv, axis=-1, keepdims=True) + EPS)
            hn = xv * r * nw_ref[...]
            hn_s[...] = hn.astype(BF16)
            hnt_ref[...] = hn.T.astype(BF16)

        @pl.when(pl.program_id(0) == 0)
        def _():
            wt_ref[...] = w_ref[...].astype(F32).T.astype(BF16)

        proj_ref[...] = _dot(hn_s[...], w_ref[...])

    return pl.pallas_call(
        body,
        name="inproj",
        grid=(T // tm, P // tn),
        in_specs=[
            pl.BlockSpec((tm, D), lambda i, j: (i, 0)),
            pl.BlockSpec((1, D), lambda i, j: (0, 0)),
            pl.BlockSpec((D, tn), lambda i, j: (0, j)),
        ],
        out_specs=[
            pl.BlockSpec((tm, tn), lambda i, j: (i, j)),
            pl.BlockSpec((D, tm), lambda i, j: (0, i)),
            pl.BlockSpec((tn, D), lambda i, j: (jnp.where(i == 0, j, nj - 1), 0)),
        ],
        out_shape=[jax.ShapeDtypeStruct((T, P), F32), jax.ShapeDtypeStruct((D, T), BF16),
                   jax.ShapeDtypeStruct((P, D), BF16)],
        scratch_shapes=[pltpu.VMEM((tm, D), BF16)],
        compiler_params=_params(("arbitrary", "arbitrary")),
    )(x2, norm_w, w_pad)


def _pair_ones():
    ri = ((_iota((2 * PAIR, PAIR), 0) % PAIR) >= HEAD_DIM).astype(jnp.int32)
    ci = (_iota((2 * PAIR, PAIR), 1) >= HEAD_DIM).astype(jnp.int32)
    return jnp.where(ri == ci, 1.0, 0.0).astype(BF16)


def _pair_rms(v, ones2):
    return lax.rsqrt(_split_dots([v * v], ones2)[0] * (1.0 / HEAD_DIM) + EPS)


def _pair_mean(v, ones2):
    return _split_dots([v], ones2)[0] * (1.0 / HEAD_DIM)


def _suffix_ones():
    ri = _iota((2 * BLK, 2 * BLK), 0) % BLK
    ci = _iota((2 * BLK, 2 * BLK), 1)
    return jnp.where((ci >= BLK) | (ri > ci), 1.0, 0.0).astype(BF16)


def _split_dots(xs, m2):
    his = [x.astype(BF16) for x in xs]
    los = [(x - hi.astype(F32)).astype(BF16) for x, hi in zip(xs, his)]
    return [_dot(jnp.concatenate([hi, lo], axis=1), m2) for hi, lo in zip(his, los)]


def _sb_tiles(streams, km_s, uo):
    tiles = [(s, u, h) for s, st in enumerate(streams) for u in range(len(st["kbs"])) for h in range(2)]
    z2s = {(s, u): _dot_nt(st["q"], km_s[kb]) for s, st in enumerate(streams) for u, kb in enumerate(st["kbs"])}
    zs = [z2s[s, u][:, h * BLK:(h + 1) * BLK] for s, u, h in tiles]
    es = [jnp.exp(-jnp.abs(z)) for z in zs]
    las = [jnp.minimum(z, 0.0) - jnp.log(1.0 + e) for z, e in zip(zs, es)]
    lks = [a - z for a, z in zip(las, zs)]
    masks = [streams[s]["masks"][u] for s, u, h in tiles]
    lks = [lk if m is None else jnp.where(m, lk, 0.0) for m, lk in zip(masks, lks)]
    css = _split_dots(lks, uo)
    rests = [list(st["rest"]) for st in streams]
    ws = []
    for (s, u, h), m, a, cs in zip(tiles, masks, las, css):
        w = jnp.exp(a + rests[s][h] + cs[:, :BLK])
        ws.append(w if m is None else jnp.where(m, w, 0.0))
        rests[s][h] = rests[s][h] + cs[:, BLK:]
    return tiles, zs, es, ws, rests


def _stream(q_pair, qi, n_left, diag, zero):
    return dict(q=q_pair, kbs=[qi - u for u in range(n_left + 1)], masks=[diag] + [None] * n_left, rest=[zero, zero])


def _row0(block):
    return block * BLK if isinstance(block, int) else pl.multiple_of(block * BLK, BLK)


def _pair_of(vals, tiles, s, u):
    return [v for v, t in zip(vals, tiles) if t[0] == s and t[1] == u]


def _block_groups(nq):
    assert nq % Q_TOGETHER == 0 and Q_TOGETHER >= FIRST_LEFT
    return list(range(Q_TOGETHER)), nq // Q_TOGETHER


def _attn_prep(src_ref, w_ref, dst_s, n_blocks, scale):
    per = math.gcd(PREP_BLOCKS, n_blocks)
    rows = per * BLK
    lo = _iota((rows, PAIR), 1) < HEAD_DIM
    ones2 = _pair_ones()

    def step(i, carry):
        r0 = pl.multiple_of(i * rows, rows)
        v = src_ref[0, pl.ds(r0, rows), :]
        if w_ref is not None:
            v = v * _pair_rms(v, ones2) * w_ref[...]
        if scale != 1.0:
            v = v * scale
        v0, v1 = jnp.where(lo, v, 0.0).astype(BF16), jnp.where(lo, 0.0, v).astype(BF16)
        for b in range(per):
            dst_s[i * per + b, 0:BLK, :] = v0[b * BLK:(b + 1) * BLK]
            dst_s[i * per + b, BLK:2 * BLK, :] = v1[b * BLK:(b + 1) * BLK]
        return carry

    lax.fori_loop(0, n_blocks // per, step, 0)


def _attn_fwd(proj3, qw2, kw2, D):
    Bl, L, _ = proj3.shape
    n_pair = D // PAIR
    nq = L // BLK
    scale = 1.0 / math.sqrt(HEAD_DIM)

    def body(q_ref, k_ref, v_ref, qw_ref, kw_ref, o_ref, qm_s, km_s, vm_s):
        uo = _suffix_ones()
        diag = _iota((BLK, BLK), 1) < _iota((BLK, BLK), 0)
        _attn_prep(q_ref, qw_ref, qm_s, nq, scale)
        _attn_prep(k_ref, kw_ref, km_s, nq, 1.0)
        _attn_prep(v_ref, None, vm_s, nq, 1.0)

        zero_c = jnp.zeros((BLK, BLK), F32)

        def q_of(qi):
            return qm_s[qi, 0:BLK, :] + qm_s[qi, BLK:2 * BLK, :]

        def values(streams, accs):
            tiles, _, _, ws, rests = _sb_tiles(streams, km_s, uo)
            wbs = [w.astype(BF16) for w in ws]
            accs = list(accs)
            for s, st in enumerate(streams):
                for u, kb in enumerate(st["kbs"]):
                    accs[s] = accs[s] + _dot(jnp.concatenate(_pair_of(wbs, tiles, s, u), axis=1), vm_s[kb])
            return accs, rests

        def group(qis, n_lefts):
            streams = [_stream(q_of(qi), qi, n, diag, zero_c) for qi, n in zip(qis, n_lefts)]
            accs, rests = values(streams, [jnp.zeros((BLK, PAIR), F32)] * len(qis))
            for qi, n, q, acc, rc in zip(qis, n_lefts, [st["q"] for st in streams], accs, rests):

                def sweep(state, n_blocks, q=q):
                    kb, rc0, rc1, acc1, _ = state
                    st = dict(q=q, kbs=[kb - u for u in range(n_blocks)], masks=[None] * n_blocks, rest=[rc0, rc1])
                    (acc1,), (r,) = values([st], [acc1])
                    return kb - n_blocks, r[0], r[1], acc1, jnp.maximum(jnp.max(r[0]), jnp.max(r[1]))

                state = (jnp.asarray(qi - n - 1, jnp.int32), rc[0], rc[1], acc, jnp.maximum(jnp.max(rc[0]), jnp.max(rc[1])))
                state = lax.while_loop(lambda t: (t[0] >= 1) & (t[4] >= UNDERFLOW), lambda t: sweep(t, 2), state)
                state = lax.while_loop(lambda t: (t[0] >= 0) & (t[4] >= UNDERFLOW), lambda t: sweep(t, 1), state)
                o_ref[0, pl.ds(_row0(qi), BLK), :] = state[3]

        head, n_groups = _block_groups(nq)
        group(head, [min(qi, FIRST_LEFT) for qi in head])

        def groups(g, carry):
            group([g * Q_TOGETHER + j for j in range(Q_TOGETHER)], [FIRST_LEFT] * Q_TOGETHER)
            return carry

        lax.fori_loop(1, n_groups, groups, 0)

    blk = lambda off: pl.BlockSpec((1, L, PAIR), lambda b, p: (b, 0, off + p))
    wspec = pl.BlockSpec((1, PAIR), lambda b, p: (0, 0))
    return pl.pallas_call(
        body,
        name="sb_attn_fwd",
        grid=(Bl, n_pair),
        in_specs=[blk(0), blk(n_pair), blk(2 * n_pair), wspec, wspec],
        out_specs=pl.BlockSpec((1, L, PAIR), lambda b, p: (b, 0, p)),
        out_shape=jax.ShapeDtypeStruct((Bl, L, D), F32),
        scratch_shapes=[pltpu.VMEM((nq, 2 * BLK, PAIR), BF16)] * 3,
        compiler_params=_params(("parallel", "parallel")),
    )(proj3, proj3, proj3, qw2, kw2)


def _attn_bwd(proj3, o3, do3, qw2, kw2, D):
    Bl, L, _ = proj3.shape
    n_pair = D // PAIR
    nq = L // BLK
    scale = 1.0 / math.sqrt(HEAD_DIM)

    def body(q_ref, k_ref, v_ref, o_ref, do_ref, qw_ref, kw_ref, dq_ref, dk_ref, dv_ref, dw_ref,
             qm_s, km_s, vm_s, dom_s, dq_s, dk_s, dv_s):
        uo = _suffix_ones()
        diag = _iota((BLK, BLK), 1) < _iota((BLK, BLK), 0)
        ones2 = _pair_ones()
        _attn_prep(q_ref, qw_ref, qm_s, nq, scale)
        _attn_prep(k_ref, kw_ref, km_s, nq, 1.0)
        _attn_prep(v_ref, None, vm_s, nq, 1.0)
        _attn_prep(do_ref, None, dom_s, nq, 1.0)

        @pl.when((pl.program_id(0) == 0) & (pl.program_id(1) == 0))
        def _():
            dw_ref[...] = jnp.zeros_like(dw_ref)

        def zero(i, carry):
            r0 = pl.multiple_of(i * BLK, BLK)
            dk_s[pl.ds(r0, BLK), :] = jnp.zeros((BLK, PAIR), F32)
            dv_s[pl.ds(r0, BLK), :] = jnp.zeros((BLK, PAIR), F32)
            return carry

        lax.fori_loop(0, nq, zero, 0)

        zero_c = jnp.zeros((BLK, BLK), F32)

        def tiles_bwd(streams, dqas):
            dw2s = {(s, u): _dot_nt(st["do"], vm_s[kb]) for s, st in enumerate(streams) for u, kb in enumerate(st["kbs"])}
            tiles, zs, es, ws, rests = _sb_tiles(streams, km_s, uo)
            dws = [dw2s[s, u][:, h * BLK:(h + 1) * BLK] for s, u, h in tiles]
            wfs = [w.astype(BF16).astype(F32) for w in ws]
            gs = [wf * dw for wf, dw in zip(wfs, dws)]
            gss = _split_dots(gs, uo)
            gcs = [list(st["g_rest"]) for st in streams]
            dzs = []
            for (s, u, h), z, e, g, gsum in zip(tiles, zs, es, gs, gss):
                g_before = streams[s]["delta"][h] - (gcs[s][h] + gsum[:, :BLK] + g)
                gcs[s][h] = gcs[s][h] + gsum[:, BLK:]
                r = 1.0 / (1.0 + e)
                er = e * r
                pos = z >= 0.0
                dz = g * jnp.where(pos, er, r) - g_before * jnp.where(pos, r, er)
                m = streams[s]["masks"][u]
                dzs.append(dz if m is None else jnp.where(m, dz, 0.0))
            wts = [wf.T.astype(BF16) for wf in wfs]
            dzts = [dz.T.astype(BF16) for dz in dzs]
            dzbs = [dz.astype(BF16) for dz in dzs]
            dqas = list(dqas)
            for s, st in enumerate(streams):
                for u, kb in enumerate(st["kbs"]):
                    c0 = _row0(kb)
                    dv_s[pl.ds(c0, BLK), :] += _dot(jnp.concatenate(_pair_of(wts, tiles, s, u), axis=1), dom_s[st["qi"]])
                    dk_s[pl.ds(c0, BLK), :] += _dot(jnp.concatenate(_pair_of(dzts, tiles, s, u), axis=1), qm_s[st["qi"]])
                    dqas[s] = dqas[s] + _dot(jnp.concatenate(_pair_of(dzbs, tiles, s, u), axis=1), km_s[kb])
            return dqas, rests, gcs

        def group(qis, n_lefts):
            streams = []
            for qi, n in zip(qis, n_lefts):
                o_blk = o_ref[0, pl.ds(_row0(qi), BLK), :]
                doms = [dom_s[qi, 0:BLK, :], dom_s[qi, BLK:2 * BLK, :]]
                st = _stream(qm_s[qi, 0:BLK, :] + qm_s[qi, BLK:2 * BLK, :], qi, n, diag, zero_c)
                st.update(qi=qi, do=doms[0] + doms[1], delta=[_rowsum(d.astype(F32) * o_blk) for d in doms],
                          g_rest=[zero_c, zero_c])
                streams.append(st)
            dqas, rests, gcs = tiles_bwd(streams, [jnp.zeros((BLK, PAIR), F32)] * len(qis))
            for qi, n, st0, dqa, rc, gc in zip(qis, n_lefts, streams, dqas, rests, gcs):

                def sweep(state, n_blocks, st0=st0):
                    kb, rc0, rc1, gc0, gc1, dqa1, _ = state
                    st = dict(st0, kbs=[kb - u for u in range(n_blocks)], masks=[None] * n_blocks, rest=[rc0, rc1],
                              g_rest=[gc0, gc1])
                    (dqa1,), (r,), (g,) = tiles_bwd([st], [dqa1])
                    return kb - n_blocks, r[0], r[1], g[0], g[1], dqa1, jnp.maximum(jnp.max(r[0]), jnp.max(r[1]))

                state = (jnp.asarray(qi - n - 1, jnp.int32), rc[0], rc[1], gc[0], gc[1], dqa,
                         jnp.maximum(jnp.max(rc[0]), jnp.max(rc[1])))
                state = lax.while_loop(lambda t: (t[0] >= 1) & (t[6] >= UNDERFLOW), lambda t: sweep(t, 2), state)
                state = lax.while_loop(lambda t: (t[0] >= 0) & (t[6] >= UNDERFLOW), lambda t: sweep(t, 1), state)
                dq_s[pl.ds(_row0(qi), BLK), :] = state[5] * scale

        head, n_groups = _block_groups(nq)
        group(head, [min(qi, FIRST_LEFT) for qi in head])

        def groups(g, carry):
            group([g * Q_TOGETHER + j for j in range(Q_TOGETHER)], [FIRST_LEFT] * Q_TOGETHER)
            return carry

        lax.fori_loop(1, n_groups, groups, 0)

        per = math.gcd(PREP_BLOCKS, nq)
        rows = per * BLK

        def finish(i, carry):
            r0 = pl.multiple_of(i * rows, rows)
            dwq, dwk = carry
            out = []
            for src_ref, w_ref, d_s in ((q_ref, qw_ref, dq_s), (k_ref, kw_ref, dk_s)):
                v = src_ref[0, pl.ds(r0, rows), :]
                r = _pair_rms(v, ones2)
                vh = v * r
                dy = d_s[pl.ds(r0, rows), :]
                dvh = dy * w_ref[...]
                out.append((r * (dvh - vh * _pair_mean(dvh * vh, ones2)), _colsum(dy * vh)))
            dq_ref[0, pl.ds(r0, rows), :] = out[0][0].astype(BF16)
            dk_ref[0, pl.ds(r0, rows), :] = out[1][0].astype(BF16)
            dv_ref[0, pl.ds(r0, rows), :] = dv_s[pl.ds(r0, rows), :].astype(BF16)
            return dwq + out[0][1], dwk + out[1][1]

        zrow = jnp.zeros((1, PAIR), F32)
        dwq, dwk = lax.fori_loop(0, nq // per, finish, (zrow, zrow))
        dw_ref[0:1, :] += dwq
        dw_ref[1:2, :] += dwk

    blk = lambda off: pl.BlockSpec((1, L, PAIR), lambda b, p: (b, 0, off + p))
    wspec = pl.BlockSpec((1, PAIR), lambda b, p: (0, 0))
    oblk = pl.BlockSpec((1, L, PAIR), lambda b, p: (b, 0, p))
    return pl.pallas_call(
        body,
        name="sb_attn_bwd",
        grid=(Bl, n_pair),
        in_specs=[blk(0), blk(n_pair), blk(2 * n_pair), oblk, oblk, wspec, wspec],
        out_specs=[oblk, oblk, oblk, pl.BlockSpec((8, PAIR), lambda b, p: (0, 0))],
        out_shape=[jax.ShapeDtypeStruct((Bl, L, D), BF16)] * 3 + [jax.ShapeDtypeStruct((8, PAIR), F32)],
        scratch_shapes=[pltpu.VMEM((nq, 2 * BLK, PAIR), BF16)] * 4 + [pltpu.VMEM((L, PAIR), F32)] * 3,
        compiler_params=_params(("arbitrary", "arbitrary")),
    )(proj3, proj3, proj3, o3, do3, qw2, kw2)


def _conv_pre(ext_s, halo_ref, raw_ref, w_ref, b_ref, first):
    ext_s[0:HALO, :] = jnp.where(first, 0.0, halo_ref[0])
    ext_s[HALO:HALO + BLK, :] = raw_ref[0]
    pre = b_ref[...]
    for i in range(CONV_K):
        pre = pre + ext_s[pl.ds(HALO - (CONV_K - 1 - i), BLK), :] * w_ref[i:i + 1, :]
    return pre


def _lane_col(m, lane, h):
    return _rowsum(jnp.where(lane == h, m, 0.0))


def _half_sums(row, lo1):
    return _rowsum(jnp.where(lo1, row, 0.0)), _rowsum(jnp.where(lo1, 0.0, row))


def _ssd_specs(Bl, L, D, rev):
    nc = L // BLK
    rows_per = BLK // HALO
    cidx = (lambda c: nc - 1 - c) if rev else (lambda c: c)
    xoff = 5
    boff = (6 * D) // 512
    doff = (6 * D + 512) // LANES
    prev = lambda c: jnp.maximum(cidx(c) * rows_per - 1, 0)
    specs = [
        pl.BlockSpec((1, BLK, D), lambda b, c: (b, cidx(c), xoff)),
        pl.BlockSpec((1, BLK, 512), lambda b, c: (b, cidx(c), boff)),
        pl.BlockSpec((1, HALO, D), lambda b, c: (b, prev(c), xoff)),
        pl.BlockSpec((1, HALO, 512), lambda b, c: (b, prev(c), boff)),
        pl.BlockSpec((1, BLK, LANES), lambda b, c: (b, cidx(c), doff)),
    ]
    full = lambda shape: pl.BlockSpec(shape, lambda b, c: (0,) * len(shape))
    specs += [full((CONV_K, D)), full((CONV_K, 512)), full((1, D)), full((1, 512)),
              full((1, LANES)), full((1, LANES)), full((1, LANES))]
    return specs, cidx


def _ssd_common(dtr_ref, dtb_ref, alog_ref, acs_s, acsT_s):
    ltri = jnp.where(_iota((BLK, BLK), 1) <= _iota((BLK, BLK), 0), 1.0, 0.0).astype(BF16)
    dtv = _softplus(dtr_ref[0] + dtb_ref[...])
    a = -jnp.exp(alog_ref[...])
    acs = _dot_split(ltri, dtv * a)
    acs_s[...] = acs
    acsT_s[...] = acs.T
    return dtv, a, acs


def _pair_terms(pr, acs, dtv, acs_s, lane, lo, lane1, lo1):
    h0, h1 = 2 * pr, 2 * pr + 1
    c0, c1 = _lane_col(acs, lane, h0), _lane_col(acs, lane, h1)
    d0, d1 = _lane_col(dtv, lane, h0), _lane_col(dtv, lane, h1)
    lastv = acs_s[BLK - 1:BLK, :]
    l0, l1 = _lane_col(lastv, lane1, h0), _lane_col(lastv, lane1, h1)
    return dict(h=(h0, h1), c=(c0, c1), last=(l0, l1), acs_p=jnp.where(lo, c0, c1), dt_p=jnp.where(lo, d0, d1),
                last_p=jnp.where(lo1, l0, l1))


def _decay_tiles(cc, row, tri, want_t):
    lm = jnp.where(tri, jnp.exp(cc - row), 0.0)
    if not want_t:
        return lm, None
    tri_t = _iota((BLK, BLK), 1) >= _iota((BLK, BLK), 0)
    return lm, jnp.where(tri_t, jnp.exp(row - cc), 0.0)


def _ssd_fwd(proj3, cwx, cwb, cbx, cbb, dtb, alog, dsk, D):
    Bl, L, _ = proj3.shape
    nc = L // BLK
    n_pair = D // PAIR
    pairs_per_group = n_pair // SSD_GROUPS
    specs, _ = _ssd_specs(Bl, L, D, False)

    def body(xr_ref, bcr_ref, xh_ref, bch_ref, dtr_ref, cwx_ref, cwb_ref, cbx_ref, cbb_ref, dtb_ref, alog_ref,
             dsk_ref, y_ref, sin_ref, st_s, extx_s, extb_s, acs_s, acsT_s):
        first = pl.program_id(1) == 0

        @pl.when(first)
        def _():
            st_s[...] = jnp.zeros_like(st_s)

        lane, lane1 = _iota((BLK, LANES), 1), _iota((1, LANES), 1)
        lo, lo1 = lane < HEAD_DIM, lane1 < HEAD_DIM
        tri = _iota((BLK, BLK), 1) <= _iota((BLK, BLK), 0)
        pre = _conv_pre(extx_s, xh_ref, xr_ref, cwx_ref, cbx_ref, first)
        ux = pre * _sigmoid(pre)
        pre = _conv_pre(extb_s, bch_ref, bcr_ref, cwb_ref, cbb_ref, first)
        ub = pre * _sigmoid(pre)
        dtv, a, acs = _ssd_common(dtr_ref, dtb_ref, alog_ref, acs_s, acsT_s)
        for g in range(SSD_GROUPS):
            bg = ub[:, g * SSD_STATE:(g + 1) * SSD_STATE]
            cb_ = ub[:, (SSD_GROUPS + g) * SSD_STATE:(SSD_GROUPS + g + 1) * SSD_STATE].astype(BF16)
            cbm = _dot_nt(cb_, bg.astype(BF16))
            btb = bg.T.astype(BF16)
            for pr in range(g * pairs_per_group, (g + 1) * pairs_per_group):
                t = _pair_terms(pr, acs, dtv, acs_s, lane, lo, lane1, lo1)
                xs_p = ux[:, pr * PAIR:(pr + 1) * PAIR]
                x_p = xs_p * t["dt_p"]
                st = st_s[pr]
                sin_ref[0, 0, pr] = st
                y = _dot(cb_, st.astype(BF16)) * jnp.exp(t["acs_p"])
                for k in range(2):
                    row = acsT_s[t["h"][k]:t["h"][k] + 1, :]
                    lm, _ = _decay_tiles(t["c"][k], row, tri, False)
                    xm = jnp.where(lo if k == 0 else ~lo, x_p, 0.0).astype(BF16)
                    y = y + _dot((cbm * lm).astype(BF16), xm)
                d_p = jnp.where(lo1, _lane_col(dsk_ref[...], lane1, t["h"][0]), _lane_col(dsk_ref[...], lane1, t["h"][1]))
                y_ref[0, :, pr * PAIR:(pr + 1) * PAIR] = y + d_p * xs_p
                xd = (x_p * jnp.exp(t["last_p"] - t["acs_p"])).astype(BF16)
                st_s[pr] = st * jnp.exp(t["last_p"]) + _dot(btb, xd)

    return pl.pallas_call(
        body,
        name="ssd_fwd",
        grid=(Bl, nc),
        in_specs=specs,
        out_specs=[
            pl.BlockSpec((1, BLK, D), lambda b, c: (b, c, 0)),
            pl.BlockSpec((1, 1, n_pair, SSD_STATE, PAIR), lambda b, c: (b, c, 0, 0, 0)),
        ],
        out_shape=[jax.ShapeDtypeStruct((Bl, L, D), F32),
                   jax.ShapeDtypeStruct((Bl, nc, n_pair, SSD_STATE, PAIR), F32)],
        scratch_shapes=[pltpu.VMEM((n_pair, SSD_STATE, PAIR), F32), pltpu.VMEM((HALO + BLK, D), F32),
                        pltpu.VMEM((HALO + BLK, 512), F32), pltpu.VMEM((BLK, LANES), F32),
                        pltpu.VMEM((LANES, BLK), F32)],
        compiler_params=_params(("arbitrary", "arbitrary")),
    )(proj3, proj3, proj3, proj3, proj3, cwx, cwb, cbx, cbb, dtb, alog, dsk)


def _ssd_bwd(proj3, s_in, dy3, cwx, cwb, cbx, cbb, dtb, alog, dsk, D, tail):
    Bl, L, _ = proj3.shape
    CD = D + 512
    nc = L // BLK
    n_pair = D // PAIR
    n_heads = 2 * n_pair
    pairs_per_group = n_pair // SSD_GROUPS
    specs, cidx = _ssd_specs(Bl, L, D, True)
    specs = specs + [
        pl.BlockSpec((1, 1, n_pair, SSD_STATE, PAIR), lambda b, c: (b, cidx(c), 0, 0, 0)),
        pl.BlockSpec((1, BLK, D), lambda b, c: (b, cidx(c), 0)),
    ]

    def body(xr_ref, bcr_ref, xh_ref, bch_ref, dtr_ref, cwx_ref, cwb_ref, cbx_ref, cbb_ref, dtb_ref, alog_ref,
             dsk_ref, sin_ref, dy_ref, dxbc_ref, dcwx_ref, dcwb_ref, dcbx_ref, dcbb_ref, misc_ref,
             dst_s, extx_s, extb_s, acs_s, acsT_s, dux_s, dub_s, e2x_s, e2b_s, nxx_s, nxb_s):
        step = pl.program_id(1)
        first = step == nc - 1
        last = step == 0

        @pl.when(last)
        def _():
            dst_s[...] = jnp.zeros_like(dst_s)
            nxx_s[...] = jnp.zeros_like(nxx_s)
            nxb_s[...] = jnp.zeros_like(nxb_s)

        @pl.when(last & (pl.program_id(0) == 0))
        def _():
            for r in (dcwx_ref, dcwb_ref, dcbx_ref, dcbb_ref, misc_ref):
                r[...] = jnp.zeros_like(r)

        lane, lane1 = _iota((BLK, LANES), 1), _iota((1, LANES), 1)
        lo, lo1 = lane < HEAD_DIM, lane1 < HEAD_DIM
        tri = _iota((BLK, BLK), 1) <= _iota((BLK, BLK), 0)
        prex = _conv_pre(extx_s, xh_ref, xr_ref, cwx_ref, cbx_ref, first)
        sgx = _sigmoid(prex)
        ux = prex * sgx
        preb = _conv_pre(extb_s, bch_ref, bcr_ref, cwb_ref, cbb_ref, first)
        sgb = _sigmoid(preb)
        ub = preb * sgb
        dtv, a, acs = _ssd_common(dtr_ref, dtb_ref, alog_ref, acs_s, acsT_s)
        dacs = jnp.zeros((BLK, LANES), F32)
        dlast = jnp.zeros((1, LANES), F32)
        ddt = jnp.zeros((BLK, LANES), F32)
        dd = jnp.zeros((1, LANES), F32)
        for g in range(SSD_GROUPS):
            bg = ub[:, g * SSD_STATE:(g + 1) * SSD_STATE]
            cg = ub[:, (SSD_GROUPS + g) * SSD_STATE:(SSD_GROUPS + g + 1) * SSD_STATE]
            bb, cb_ = bg.astype(BF16), cg.astype(BF16)
            cbm = _dot_nt(cb_, bb)
            cbt = _dot_nt(bb, cb_)
            ctb = cg.T.astype(BF16)
            dbg = jnp.zeros((BLK, SSD_STATE), F32)
            dcg = jnp.zeros((BLK, SSD_STATE), F32)
            for pr in range(g * pairs_per_group, (g + 1) * pairs_per_group):
                t = _pair_terms(pr, acs, dtv, acs_s, lane, lo, lane1, lo1)
                h0, h1 = t["h"]
                xs_p = ux[:, pr * PAIR:(pr + 1) * PAIR]
                dy_p = dy_ref[0, :, pr * PAIR:(pr + 1) * PAIR]
                x_p = xs_p * t["dt_p"]
                ea_p = jnp.exp(t["acs_p"])
                dte_p = jnp.exp(t["last_p"] - t["acs_p"])
                cd_p = jnp.exp(t["last_p"])
                st = sin_ref[0, 0, pr]
                dst = dst_s[pr]
                stb, dstb = st.astype(BF16), dst.astype(BF16)
                s0, s1 = _half_sums(_colsum(dy_p * xs_p), lo1)
                dd = dd + jnp.where(lane1 == h0, s0, 0.0) + jnp.where(lane1 == h1, s1, 0.0)
                d_p = jnp.where(lo1, _lane_col(dsk_ref[...], lane1, h0), _lane_col(dsk_ref[...], lane1, h1))
                dxs_p = d_p * dy_p
                dp = dy_p * ea_p
                dpb = dp.astype(BF16)
                yo = dp * _dot(cb_, stb)
                dcg = dcg + _dot_nt(dpb, stb)
                dst_off = _dot(ctb, dpb)
                dac = [_rowsum(jnp.where(lo, yo, 0.0)), _rowsum(jnp.where(lo, 0.0, yo))]
                s0, s1 = _half_sums(_colsum(dst * st), lo1)
                dl = [s0 * jnp.exp(t["last"][0]), s1 * jnp.exp(t["last"][1])]
                dxd = _dot(bb, dstb)
                dx_p = dxd * dte_p
                tt = dxd * x_p
                dbg = dbg + _dot_nt((x_p * dte_p).astype(BF16), dstb)
                for k, ddte in enumerate((_rowsum(jnp.where(lo, tt, 0.0)), _rowsum(jnp.where(lo, 0.0, tt)))):
                    ek = ddte * jnp.exp(t["last"][k] - t["c"][k])
                    dl[k] = dl[k] + _colsum(ek)
                    dac[k] = dac[k] - ek
                for k in range(2):
                    row = acsT_s[t["h"][k]:t["h"][k] + 1, :]
                    lm, lmt = _decay_tiles(t["c"][k], row, tri, True)
                    msk = lo if k == 0 else ~lo
                    xm = jnp.where(msk, x_p, 0.0).astype(BF16)
                    dym = jnp.where(msk, dy_p, 0.0).astype(BF16)
                    dm = _dot_nt(dym, xm)
                    dmt = _dot_nt(xm, dym)
                    mt = cbt * lmt
                    dx_p = dx_p + _dot(mt.astype(BF16), dym)
                    dac[k] = dac[k] + _rowsum(dm * (cbm * lm)) - _rowsum(dmt * mt)
                    dcg = dcg + _dot((dm * lm).astype(BF16), bb)
                    dbg = dbg + _dot((dmt * lmt).astype(BF16), cb_)
                dacs = dacs + jnp.where(lane == h0, dac[0], 0.0) + jnp.where(lane == h1, dac[1], 0.0)
                dlast = dlast + jnp.where(lane1 == h0, dl[0], 0.0) + jnp.where(lane1 == h1, dl[1], 0.0)
                dxs_p = dxs_p + dx_p * t["dt_p"]
                t3 = dx_p * xs_p
                ddt = ddt + jnp.where(lane == h0, _rowsum(jnp.where(lo, t3, 0.0)), 0.0) \
                    + jnp.where(lane == h1, _rowsum(jnp.where(lo, 0.0, t3)), 0.0)
                dux_s[:, pr * PAIR:(pr + 1) * PAIR] = dxs_p
                dst_s[pr] = dst * cd_p + dst_off
            dub_s[:, g * SSD_STATE:(g + 1) * SSD_STATE] = dbg
            dub_s[:, (SSD_GROUPS + g) * SSD_STATE:(SSD_GROUPS + g + 1) * SSD_STATE] = dcg
        dacs = dacs + jnp.where(_iota((BLK, LANES), 0) == BLK - 1, dlast, 0.0)
        utri = jnp.where(_iota((BLK, BLK), 1) >= _iota((BLK, BLK), 0), 1.0, 0.0).astype(BF16)
        dda = _dot_split(utri, dacs)
        ddt = ddt + dda * a
        ddtr = jnp.where(lane < n_heads, ddt * _sigmoid(dtr_ref[0] + dtb_ref[...]), 0.0)
        dxbc_ref[0, :, CD:CD + LANES] = ddtr.astype(BF16)
        dxbc_ref[0, :, CD + LANES:tail] = jnp.zeros((BLK, tail - CD - LANES), BF16)
        misc_ref[0:1, :] += _colsum(ddtr)
        misc_ref[1:2, :] += jnp.where(lane1 < n_heads, _colsum(dda * dtv) * a, 0.0)
        misc_ref[2:3, :] += dd
        for (du_s, pre, sg, ext_s, e2_s, nx_s, w_ref, dcw_ref, dcb_ref, c0, width) in (
                (dux_s, prex, sgx, extx_s, e2x_s, nxx_s, cwx_ref, dcwx_ref, dcbx_ref, 0, D),
                (dub_s, preb, sgb, extb_s, e2b_s, nxb_s, cwb_ref, dcwb_ref, dcbb_ref, D, 512)):
            dpre = du_s[...] * (sg * (1.0 + pre * (1.0 - sg)))
            dcb_ref[...] += _colsum(dpre)
            for i in range(CONV_K):
                dcw_ref[i:i + 1, :] += _colsum(dpre * ext_s[pl.ds(HALO - (CONV_K - 1 - i), BLK), :])
            e2_s[0:BLK, :] = dpre
            e2_s[BLK:BLK + HALO, :] = nx_s[...]
            dxr = jnp.zeros((BLK, width), F32)
            for i in range(CONV_K):
                dxr = dxr + e2_s[pl.ds(CONV_K - 1 - i, BLK), :] * w_ref[i:i + 1, :]
            dxbc_ref[0, :, c0:c0 + width] = dxr.astype(BF16)
            nx_s[...] = e2_s[0:HALO, :]

    full = lambda shape: pl.BlockSpec(shape, lambda b, c: (0,) * len(shape))
    return pl.pallas_call(
        body,
        name="ssd_bwd",
        grid=(Bl, nc),
        in_specs=specs,
        out_specs=[
            pl.BlockSpec((1, BLK, tail), lambda b, c: (b, cidx(c), 0)),
            full((CONV_K, D)), full((CONV_K, 512)), full((1, D)), full((1, 512)), full((8, LANES)),
        ],
        out_shape=[
            jax.ShapeDtypeStruct((Bl, L, tail), BF16),
            jax.ShapeDtypeStruct((CONV_K, D), F32), jax.ShapeDtypeStruct((CONV_K, 512), F32),
            jax.ShapeDtypeStruct((1, D), F32), jax.ShapeDtypeStruct((1, 512), F32),
            jax.ShapeDtypeStruct((8, LANES), F32),
        ],
        scratch_shapes=[
            pltpu.VMEM((n_pair, SSD_STATE, PAIR), F32),
            pltpu.VMEM((HALO + BLK, D), F32), pltpu.VMEM((HALO + BLK, 512), F32),
            pltpu.VMEM((BLK, LANES), F32), pltpu.VMEM((LANES, BLK), F32),
            pltpu.VMEM((BLK, D), F32), pltpu.VMEM((BLK, 512), F32),
            pltpu.VMEM((BLK + HALO, D), F32), pltpu.VMEM((BLK + HALO, 512), F32),
            pltpu.VMEM((HALO, D), F32), pltpu.VMEM((HALO, 512), F32),
        ],
        compiler_params=_params(("arbitrary", "arbitrary")),
    )(proj3, proj3, proj3, proj3, proj3, cwx, cwb, cbx, cbb, dtb, alog, dsk, s_in, dy3)


def _gate_out(x2, tgt2, o2, proj2, y2, sbw, ssw, w_out_bf, w_out_t):
    T, D = x2.shape
    tm = min(256, T)

    def body(x_ref, t_ref, o_ref, zs_ref, y_ref, zy_ref, sbw_ref, ssw_ref, wo_ref, wot_ref,
             dout_ref, doutb_ref, mixt_ref, do_ref, dy_ref, dz_ref, dnw_ref, loss_ref):
        @pl.when(pl.program_id(0) == 0)
        def _():
            dnw_ref[...] = jnp.zeros_like(dnw_ref)
            loss_ref[...] = jnp.zeros_like(loss_ref)

        def fwd(o, z, w):
            sg = _sigmoid(z)
            sl = z * sg
            g = o * sl
            r = lax.rsqrt(jnp.mean(g * g, axis=-1, keepdims=True) + EPS)
            n = g * r
            return sg, sl, r, n, n * w

        def bwd(dy, o, z, w, sg, sl, r, n):
            dn = dy * w
            dg = r * (dn - n * jnp.mean(dn * n, axis=-1, keepdims=True))
            return dg * sl, dg * o * (sg * (1.0 + z * (1.0 - sg))), _colsum(dy * n)

        o1, z1, w1 = o_ref[...], zs_ref[...], sbw_ref[...]
        o2_, z2, w2 = y_ref[...], zy_ref[...], ssw_ref[...]
        sg1, sl1, r1, n1, y1 = fwd(o1, z1, w1)
        sg2, sl2, r2, n2, y2_ = fwd(o2_, z2, w2)
        y1b, y2b = y1.astype(BF16), y2_.astype(BF16)
        mixt_ref[0:D, :] = y1.T.astype(BF16)
        mixt_ref[D:2 * D, :] = y2_.T.astype(BF16)
        out = x_ref[...] + (_dot(y1b, wo_ref[0:D, :]) + _dot(y2b, wo_ref[D:2 * D, :]))
        err = out - t_ref[...]
        loss_ref[...] += jnp.sum(err * err) * (0.5 / D)
        dout = err * (1.0 / D)
        dout_ref[...] = dout
        doutb = dout.astype(BF16)
        doutb_ref[...] = doutb
        do1, dz1, dw1 = bwd(_dot(doutb, wot_ref[:, 0:D]), o1, z1, w1, sg1, sl1, r1, n1)
        do2, dz2, dw2 = bwd(_dot(doutb, wot_ref[:, D:2 * D]), o2_, z2, w2, sg2, sl2, r2, n2)
        do_ref[...] = do1
        dy_ref[...] = do2
        dz_ref[:, 0:D] = dz1.astype(BF16)
        dz_ref[:, D:2 * D] = dz2.astype(BF16)
        dnw_ref[0:1, :] += dw1
        dnw_ref[1:2, :] += dw2

    row = lambda col: pl.BlockSpec((tm, D), lambda i: (i, col))
    full = lambda shape: pl.BlockSpec(shape, lambda i: (0,) * len(shape))
    wide = pl.BlockSpec((tm, 2 * D), lambda i: (i, 0))
    return pl.pallas_call(
        body,
        name="gate_out",
        grid=(T // tm,),
        in_specs=[row(0), row(0), row(0), row(3), row(0), row(4), full((1, D)), full((1, D)), full((2 * D, D)),
                  full((D, 2 * D))],
        out_specs=[row(0), row(0), pl.BlockSpec((2 * D, tm), lambda i: (0, i)), row(0), row(0), wide,
                   full((8, D)), full((8, LANES))],
        out_shape=[
            jax.ShapeDtypeStruct((T, D), F32), jax.ShapeDtypeStruct((T, D), BF16),
            jax.ShapeDtypeStruct((2 * D, T), BF16), jax.ShapeDtypeStruct((T, D), F32),
            jax.ShapeDtypeStruct((T, D), F32), jax.ShapeDtypeStruct((T, 2 * D), BF16),
            jax.ShapeDtypeStruct((8, D), F32), jax.ShapeDtypeStruct((8, LANES), F32),
        ],
        compiler_params=_params(("arbitrary",)),
    )(x2, tgt2, o2, proj2, y2, proj2, sbw, ssw, w_out_bf, w_out_t)


def _piece_blocks(pieces, D):
    counts = [p.shape[1] // D for p in pieces]
    return [sum(counts[:i]) for i in range(len(counts))], counts


def _dhn(pieces, w_pad_t, x2, dout, norm_w, h_in, h_out):
    T, D = x2.shape
    tm = min(512, T)
    starts, counts = _piece_blocks(pieces, D)
    nk = sum(counts)
    ni = T // tm
    n_sem = 2 * (N_CHIPS - 1)
    assert nk * D == w_pad_t.shape[0]

    def body(*refs):
        p_refs = refs[:len(pieces)]
        (w_ref, x_ref, dout_ref, nw_ref, hin, hout, gx_ref, dnw_ref, rin, rout,
         acc_s, send_sems, recv_sems) = refs[len(pieces):]
        i, k = pl.program_id(0), pl.program_id(1)

        def scatter():
            x, y, c, chips = _place()
            return [pltpu.make_async_remote_copy(
                src_ref=s.at[2 * px + py], dst_ref=d.at[j], send_sem=send_sems.at[2 * j + m],
                recv_sem=recv_sems.at[2 * j + m], device_id=(px, py, c), device_id_type=MESH)
                for j, (px, py) in enumerate(chips) for m, (s, d) in enumerate(((hin, rin), (hout, rout)))]

        @pl.when((i == 0) & (k == 0))
        def _():
            for cp in scatter():
                cp.start()

        @pl.when((i == ni - 1) & (k == nk - 1))
        def _():
            for cp in scatter():
                cp.wait()

        @pl.when((i == 0) & (k == 0))
        def _():
            dnw_ref[...] = jnp.zeros_like(dnw_ref)

        @pl.when(k == 0)
        def _():
            acc_s[...] = jnp.zeros_like(acc_s)

        for p_ref, s, n in zip(p_refs, starts, counts):
            @pl.when((k >= s) & (k < s + n))
            def _(p_ref=p_ref):
                acc_s[...] += _dot(p_ref[...], w_ref[...])

        @pl.when(k == nk - 1)
        def _():
            xv = x_ref[...]
            r = lax.rsqrt(jnp.mean(xv * xv, axis=-1, keepdims=True) + EPS)
            xh = xv * r
            dhn = acc_s[...]
            dxh = dhn * nw_ref[...]
            gx_ref[...] = dout_ref[...] + r * (dxh - xh * jnp.mean(dxh * xh, axis=-1, keepdims=True))
            dnw_ref[0:1, :] += _colsum(dhn * xh)

    return pl.pallas_call(
        body,
        name="dhn",
        grid=(T // tm, nk),
        in_specs=[pl.BlockSpec((tm, D), lambda i, k, s=s, n=n: (i, jnp.clip(k - s, 0, n - 1)))
                  for s, n in zip(starts, counts)] + [
            pl.BlockSpec((D, D), lambda i, k: (k, 0)),
            pl.BlockSpec((tm, D), lambda i, k: (i, 0)),
            pl.BlockSpec((tm, D), lambda i, k: (i, 0)),
            pl.BlockSpec((1, D), lambda i, k: (0, 0)),
            ANY, ANY,
        ],
        out_specs=[pl.BlockSpec((tm, D), lambda i, k: (i, 0)), pl.BlockSpec((8, D), lambda i, k: (0, 0)), ANY, ANY],
        out_shape=[jax.ShapeDtypeStruct((T, D), F32), jax.ShapeDtypeStruct((8, D), F32),
                   jax.ShapeDtypeStruct((N_CHIPS - 1,) + h_in.shape[1:], F32),
                   jax.ShapeDtypeStruct((N_CHIPS - 1,) + h_out.shape[1:], F32)],
        scratch_shapes=[pltpu.VMEM((tm, D), F32), pltpu.SemaphoreType.DMA((n_sem,)), pltpu.SemaphoreType.DMA((n_sem,))],
        compiler_params=_params(("arbitrary", "arbitrary")),
    )(*pieces, w_pad_t, x2, dout, norm_w, h_in, h_out)


def _grad_w_in(hn_t, pieces):
    D, T = hn_t.shape
    tk = min(1024, T)
    starts, counts = _piece_blocks(pieces, D)

    def body(*refs):
        a_ref, p_refs, o_ref = refs[0], refs[1:-1], refs[-1]
        j = pl.program_id(0)

        @pl.when(pl.program_id(1) == 0)
        def _():
            o_ref[...] = jnp.zeros_like(o_ref)

        for p_ref, s, n in zip(p_refs, starts, counts):
            @pl.when((j >= s) & (j < s + n))
            def _(p_ref=p_ref):
                o_ref[...] += _dot(a_ref[...], p_ref[...])

    def piece_spec(s, n):
        return pl.BlockSpec((tk, D), lambda j, k: (jnp.where((j >= s) & (j < s + n), k, 0), jnp.clip(j - s, 0, n - 1)))

    return pl.pallas_call(
        body,
        name="grad_w_in",
        grid=(sum(counts), T // tk),
        in_specs=[pl.BlockSpec((D, tk), lambda j, k: (0, k))] + [piece_spec(s, n) for s, n in zip(starts, counts)],
        out_specs=pl.BlockSpec((D, D), lambda j, k: (0, j)),
        out_shape=jax.ShapeDtypeStruct((D, sum(counts) * D), F32),
        compiler_params=_params(("parallel", "arbitrary")),
    )(hn_t, *pieces)


def _matmul(a, b, name):
    M, K = a.shape
    N = b.shape[1]
    tm = min(1024, M)
    tn = 1024 if N % 1024 == 0 else (512 if N % 512 == 0 else N)
    tk = min(512, K)

    def body(a_ref, b_ref, o_ref):
        @pl.when(pl.program_id(2) == 0)
        def _():
            o_ref[...] = jnp.zeros_like(o_ref)

        o_ref[...] += _dot(a_ref[...], b_ref[...])

    return pl.pallas_call(
        body,
        name=name,
        grid=(M // tm, N // tn, K // tk),
        in_specs=[pl.BlockSpec((tm, tk), lambda i, j, k: (i, k)), pl.BlockSpec((tk, tn), lambda i, j, k: (k, j))],
        out_specs=pl.BlockSpec((tm, tn), lambda i, j, k: (i, j)),
        out_shape=jax.ShapeDtypeStruct((M, N), F32),
        compiler_params=_params(("parallel", "parallel", "arbitrary")),
    )(a, b)


def _adamw(w, g, m, v, name):
    R, C = w.shape
    tr = 256 if R % 256 == 0 else R
    c1 = 1.0 - ADAM_B1 ** ADAM_STEP
    c2 = 1.0 - ADAM_B2 ** ADAM_STEP

    def body(w_ref, g_ref, m_ref, v_ref, d_ref, nm_ref, nv_ref):
        gv = g_ref[...]
        m_new = ADAM_B1 * m_ref[...] + (1.0 - ADAM_B1) * gv
        v_new = ADAM_B2 * v_ref[...] + (1.0 - ADAM_B2) * (gv * gv)
        d_ref[...] = -ADAM_LR * ((m_new / c1) / (jnp.sqrt(v_new / c2) + ADAM_EPS) + ADAM_WD * w_ref[...])
        nm_ref[...] = m_new
        nv_ref[...] = v_new

    spec = pl.BlockSpec((tr, C), lambda i: (i, 0))
    return pl.pallas_call(
        body,
        name=name,
        grid=(R // tr,),
        in_specs=[spec] * 4,
        out_specs=[spec] * 3,
        out_shape=[jax.ShapeDtypeStruct((R, C), F32)] * 3,
        compiler_params=_params(("parallel",)),
    )(w, g, m, v)


def _add_core_half(a, recv, core, name):
    _, n, h, S = a.shape
    th = 256 if h % 256 == 0 else h

    def body(c_ref, a_ref, r_ref, o_ref):
        o_ref[...] = a_ref[...] + r_ref[...]

    return pl.pallas_call(
        body,
        name=name,
        grid_spec=pltpu.PrefetchScalarGridSpec(
            num_scalar_prefetch=1,
            grid=(n, h // th),
            in_specs=[
                pl.BlockSpec((None, None, th, S), lambda p, i, c: (c[0], p, i, 0)),
                pl.BlockSpec((None, th, S), lambda p, i, c: (p, i, 0)),
            ],
            out_specs=pl.BlockSpec((None, th, S), lambda p, i, c: (p, i, 0)),
        ),
        out_shape=jax.ShapeDtypeStruct((n, h, S), F32),
        compiler_params=_params(("parallel", "parallel")),
    )(core, a, recv)


def _add_chips(hsum, recv, chip, name):
    _, h, S = hsum.shape
    th = 256 if h % 256 == 0 else h

    def body(c_ref, a_ref, r_ref, o_ref):
        o_ref[...] = ((a_ref[...] + r_ref[0]) + r_ref[1]) + r_ref[2]

    return pl.pallas_call(
        body,
        name=name,
        grid_spec=pltpu.PrefetchScalarGridSpec(
            num_scalar_prefetch=1,
            grid=(h // th,),
            in_specs=[
                pl.BlockSpec((None, th, S), lambda i, c: (c[0], i, 0)),
                pl.BlockSpec((N_CHIPS - 1, th, S), lambda i, c: (0, i, 0)),
            ],
            out_specs=pl.BlockSpec((th, S), lambda i, c: (i, 0)),
        ),
        out_shape=jax.ShapeDtypeStruct((h, S), F32),
        compiler_params=_params(("parallel",)),
    )(chip, hsum, recv)


def _place():
    x, y, c = lax.axis_index("x"), lax.axis_index("y"), lax.axis_index("c")
    other_chips = [(1 - x, y), (x, 1 - y), (1 - x, 1 - y)]
    return x, y, c, other_chips


def _allgather_weights(w_in_bf, w_out_bf, conv_w):
    D, S = w_in_bf.shape
    R = w_out_bf.shape[0]
    n_ici, n_fwd = 3 * (N_CHIPS - 1), 2 * (N_CHIPS - 1)

    def body(win, wout, cw, gin, gout, gcw, send_sems, recv_sems):
        x, y, c, chips = _place()
        me = 2 * x + y
        sibling = (x, y, 1 - c)
        hin, hout = D // 2, R // 2

        def halves(chip_idx):
            return (gin.at[chip_idx, pl.ds(c * hin, hin)], gout.at[chip_idx, pl.ds(c * hout, hout)])

        def rcopy(k, src, dst, to):
            return pltpu.make_async_remote_copy(src_ref=src, dst_ref=dst, send_sem=send_sems.at[k],
                                                recv_sem=recv_sems.at[k], device_id=to, device_id_type=MESH)

        my_in, my_out = halves(me)
        src_in, src_out = win.at[pl.ds(c * hin, hin)], wout.at[pl.ds(c * hout, hout)]
        sends = []
        for j, chip in enumerate(chips):
            to = (*chip, c)
            sends += [rcopy(3 * j, src_in, my_in, to), rcopy(3 * j + 1, src_out, my_out, to),
                      rcopy(3 * j + 2, cw, gcw.at[me], to)]
        for cp in sends:
            cp.start()
        passed = []
        for j, (px, py) in enumerate(chips):
            their_in, their_out = halves(2 * px + py)
            rcopy(3 * j, their_in, their_in, sibling).wait_recv()
            rcopy(3 * j + 1, their_out, their_out, sibling).wait_recv()
            rcopy(3 * j + 2, cw, gcw.at[2 * px + py], sibling).wait_recv()
            fw = [rcopy(n_ici + 2 * j, their_in, their_in, sibling), rcopy(n_ici + 2 * j + 1, their_out, their_out, sibling)]
            for cp in fw:
                cp.start()
            passed += fw
        for j, (px, py) in enumerate(chips):
            oin = gin.at[2 * px + py, pl.ds((1 - c) * hin, hin)]
            oout = gout.at[2 * px + py, pl.ds((1 - c) * hout, hout)]
            rcopy(n_ici + 2 * j, oin, oin, sibling).wait_recv()
            rcopy(n_ici + 2 * j + 1, oout, oout, sibling).wait_recv()
        for cp in sends + passed:
            cp.wait_send()

    return pl.pallas_call(
        body,
        name="allgather_weights",
        in_specs=[ANY, ANY, ANY],
        out_specs=[ANY, ANY, ANY],
        out_shape=[jax.ShapeDtypeStruct((N_CHIPS, D, S), BF16), jax.ShapeDtypeStruct((N_CHIPS, R, D), BF16),
                   jax.ShapeDtypeStruct((N_CHIPS,) + conv_w.shape, F32)],
        scratch_shapes=[pltpu.SemaphoreType.DMA((n_ici + n_fwd,)), pltpu.SemaphoreType.DMA((n_ici + n_fwd,))],
    )(w_in_bf, w_out_bf, conv_w)


def _allreduce_small(packed):
    R = packed.shape[0]
    n_dev = 2 * N_CHIPS

    def body(p_ref, o_ref, buf, send_sems, recv_sems):
        x, y, c, _ = _place()
        me = 4 * x + 2 * y + c
        buf[me] = p_ref[...]
        copies = []
        for k in range(1, n_dev):
            px = 1 - x if k & 4 else x
            py = 1 - y if k & 2 else y
            pc = 1 - c if k & 1 else c
            copies.append((pltpu.make_async_remote_copy(
                src_ref=buf.at[me], dst_ref=buf.at[me], send_sem=send_sems.at[k - 1], recv_sem=recv_sems.at[k - 1],
                device_id=(px, py, pc), device_id_type=MESH), 4 * px + 2 * py + pc, (px, py, pc)))
        for cp, _, _ in copies:
            cp.start()
        for k, (_, peer, to) in enumerate(copies):
            pltpu.make_async_remote_copy(
                src_ref=buf.at[peer], dst_ref=buf.at[peer], send_sem=send_sems.at[k], recv_sem=recv_sems.at[k],
                device_id=to, device_id_type=MESH).wait_recv()
        for cp, _, _ in copies:
            cp.wait_send()
        acc = buf[0]
        for d in range(1, n_dev):
            acc = acc + buf[d]
        o_ref[...] = acc

    vm = pl.BlockSpec(memory_space=pltpu.VMEM)
    return pl.pallas_call(
        body,
        name="allreduce_small",
        in_specs=[vm],
        out_specs=vm,
        out_shape=jax.ShapeDtypeStruct((R, LANES), F32),
        scratch_shapes=[pltpu.VMEM((n_dev, R, LANES), F32), pltpu.SemaphoreType.DMA((n_dev - 1,)),
                        pltpu.SemaphoreType.DMA((n_dev - 1,))],
    )(packed)


def _swap_core_halves(a_in, a_out):
    def body(ain, aout, rin, rout, send_sems, recv_sems):
        x, y, c, _ = _place()
        cps = [pltpu.make_async_remote_copy(src_ref=s.at[1 - c], dst_ref=d, send_sem=send_sems.at[k],
                                            recv_sem=recv_sems.at[k], device_id=(x, y, 1 - c), device_id_type=MESH)
               for k, (s, d) in enumerate(((ain, rin), (aout, rout)))]
        for cp in cps:
            cp.start()
        for cp in cps:
            cp.wait()

    return pl.pallas_call(
        body,
        name="reduce_core_swap",
        in_specs=[ANY, ANY],
        out_specs=[ANY, ANY],
        out_shape=[jax.ShapeDtypeStruct(a_in.shape[1:], F32), jax.ShapeDtypeStruct(a_out.shape[1:], F32)],
        scratch_shapes=[pltpu.SemaphoreType.DMA((2,)), pltpu.SemaphoreType.DMA((2,))],
    )(a_in, a_out)


def _join_core_halves(g_in, g_out):
    def body(gin, gout, fin, fout, send_sems, recv_sems):
        x, y, c, _ = _place()
        cps = [pltpu.make_async_remote_copy(src_ref=s, dst_ref=d.at[c], send_sem=send_sems.at[k],
                                            recv_sem=recv_sems.at[k], device_id=(x, y, 1 - c), device_id_type=MESH)
               for k, (s, d) in enumerate(((gin, fin), (gout, fout)))]
        for cp in cps:
            cp.start()
        for k, (s, d) in enumerate(((gin, fin), (gout, fout))):
            pltpu.make_async_remote_copy(src_ref=s, dst_ref=d.at[1 - c], send_sem=send_sems.at[k],
                                         recv_sem=recv_sems.at[k], device_id=(x, y, 1 - c),
                                         device_id_type=MESH).wait_recv()
        for cp in cps:
            cp.wait_send()

    return pl.pallas_call(
        body,
        name="reduce_core_join",
        in_specs=[ANY, ANY],
        out_specs=[ANY, ANY],
        out_shape=[jax.ShapeDtypeStruct((2,) + g_in.shape, F32), jax.ShapeDtypeStruct((2,) + g_out.shape, F32)],
        scratch_shapes=[pltpu.SemaphoreType.DMA((2,)), pltpu.SemaphoreType.DMA((2,))],
    )(g_in, g_out)


def _pack(arrays):
    rows = []
    for a in arrays:
        flat = a.reshape(-1).astype(F32)
        n = -(-flat.shape[0] // LANES) * LANES
        rows.append(jnp.pad(flat, (0, n - flat.shape[0])).reshape(-1, LANES))
    out = jnp.concatenate(rows, axis=0)
    return jnp.pad(out, ((0, -out.shape[0] % 8), (0, 0)))


def _unpack(packed, shapes):
    out, r = [], 0
    for shp in shapes:
        n = math.prod(shp)
        nr = -(-n // LANES)
        out.append(packed[r:r + nr].reshape(-1)[:n].reshape(shp))
        r += nr
    return out


def _pad_lanes(a):
    return jnp.pad(a, ((0, 0), (0, LANES - a.shape[1])))


def kernel(x, norm_w, w_in, q_norm_w, k_norm_w, conv_w, conv_b, dt_bias, A_log, D_skip, sb_norm_w, ssd_norm_w, w_out, loss_target, m_norm_w, m_w_in, m_q_norm_w, m_k_norm_w, m_conv_w, m_conv_b, m_dt_bias, m_A_log, m_D_skip, m_sb_norm_w, m_ssd_norm_w, m_w_out, v_norm_w, v_w_in, v_q_norm_w, v_k_norm_w, v_conv_w, v_conv_b, v_dt_bias, v_A_log, v_D_skip, v_sb_norm_w, v_ssd_norm_w, v_w_out):
    Bl, L, D = x.shape
    T = Bl * L
    S = w_in.shape[2]
    R = w_out.shape[1]
    CW = conv_w.shape[2]
    n_in = N_CHIPS * S
    CD = D + 2 * SSD_GROUPS * SSD_STATE
    H = D // HEAD_DIM
    n_main = 6 * D + 512
    P = -(-(n_main + LANES) // 1024) * 1024
    assert n_in == n_main + H and CD == N_CHIPS * CW and 2 * D == N_CHIPS * R and CD == D + 512
    chip = (2 * lax.axis_index("x") + lax.axis_index("y")).astype(jnp.int32)
    core = lax.axis_index("c").astype(jnp.int32)

    w_in_bf, w_out_shard_bf = w_in[0].astype(BF16), w_out[0].astype(BF16)
    g_in, g_out, g_cw = _allgather_weights(w_in_bf, w_out_shard_bf, conv_w[0])
    g_in = lax.dynamic_update_slice(g_in, w_in_bf[None], (chip, 0, 0))
    g_out = lax.dynamic_update_slice(g_out, w_out_shard_bf[None], (chip, 0, 0))
    g_cw = lax.dynamic_update_slice(g_cw, conv_w, (chip, 0, 0))
    w_pad = jnp.pad(g_in.transpose(1, 0, 2).reshape(D, n_in), ((0, 0), (0, P - n_in)))
    w_out_bf = g_out.reshape(2 * D, D)
    conv_full = g_cw.transpose(1, 0, 2).reshape(CONV_K, CD)
    cwx, cwb = conv_full[:, :D], conv_full[:, D:]
    cbx, cbb = conv_b[:, :D], conv_b[:, D:]
    dtb, alog, dsk = _pad_lanes(dt_bias), _pad_lanes(A_log), _pad_lanes(D_skip)
    qw2, kw2 = jnp.tile(q_norm_w, (1, 2)), jnp.tile(k_norm_w, (1, 2))

    x2 = x.reshape(T, D)
    proj, hn_t, w_pad_t = _inproj(x2, norm_w, w_pad)
    proj3 = proj.reshape(Bl, L, P)
    o_sb = _attn_fwd(proj3, qw2, kw2, D)
    y_ssd, s_in = _ssd_fwd(proj3, cwx, cwb, cbx, cbb, dtb, alog, dsk, D)
    dout, dout_bf, mixed_t, do_sb, dy_ssd, dz_bf, dnw_out, loss_blk = _gate_out(
        x2, loss_target.reshape(T, D), o_sb.reshape(T, D), proj, y_ssd.reshape(T, D), sb_norm_w, ssd_norm_w, w_out_bf,
        w_out_bf.T)

    dq, dk, dv, dqkw = _attn_bwd(proj3, o_sb, do_sb.reshape(Bl, L, D), qw2, kw2, D)
    dtail, dcwx, dcwb, dcbx, dcbb, misc = _ssd_bwd(
        proj3, s_in, dy_ssd.reshape(Bl, L, D), cwx, cwb, cbx, cbb, dtb, alog, dsk, D, P - 5 * D)
    dproj = [dq.reshape(T, D), dk.reshape(T, D), dv.reshape(T, D), dz_bf, dtail.reshape(T, P - 5 * D)]
    gw_in = _grad_w_in(hn_t, dproj)[:, :n_in]
    gw_out = _matmul(mixed_t, dout_bf, "grad_w_out")

    a_in = gw_in.reshape(2, D // 2, N_CHIPS, S).transpose(0, 2, 1, 3)
    a_out = gw_out.reshape(N_CHIPS, 2, R // 2, D).transpose(1, 0, 2, 3)
    r_in, r_out = _swap_core_halves(a_in, a_out)
    core1, chip1 = core.reshape(1), chip.reshape(1)
    h_in = _add_core_half(a_in, r_in, core1, "sum_cores_w_in")
    h_out = _add_core_half(a_out, r_out, core1, "sum_cores_w_out")
    grad_x2, dnw_in, s_in_, s_out_ = _dhn(dproj, w_pad_t, x2, dout, norm_w, h_in, h_out)
    gh_in = _add_chips(h_in, s_in_, chip1, "sum_chips_w_in")
    gh_out = _add_chips(h_out, s_out_, chip1, "sum_chips_w_out")
    f_in, f_out = _join_core_halves(gh_in, gh_out)
    g_w_in = lax.dynamic_update_slice(f_in, gh_in[None], (core, 0, 0)).reshape(D, S)
    g_w_out = lax.dynamic_update_slice(f_out, gh_out[None], (core, 0, 0)).reshape(R, D)

    small_shapes = [(1, D), (1, D), (1, D), (1, CD), (1, HEAD_DIM), (1, HEAD_DIM), (1, H), (1, H), (1, H)]
    g_small_local = [dnw_in[0:1], dnw_out[0:1], dnw_out[1:2], jnp.concatenate([dcbx, dcbb], axis=1),
                     dqkw[0:1, :HEAD_DIM] + dqkw[0:1, HEAD_DIM:], dqkw[1:2, :HEAD_DIM] + dqkw[1:2, HEAD_DIM:],
                     misc[0:1, :H], misc[1:2, :H], misc[2:3, :H]]
    packed = _pack(g_small_local + [jnp.concatenate([dcwx, dcwb], axis=1), loss_blk[0:1, 0:1]])
    red = _allreduce_small(packed)
    g_small = _unpack(red, small_shapes + [(CONV_K, CD), (1, 1)])
    g_conv_w = lax.dynamic_slice_in_dim(g_small[9], chip * CW, CW, axis=1)
    loss = g_small[10][0, 0]

    d_in, nm_in, nv_in = _adamw(w_in[0], g_w_in, m_w_in[0], v_w_in[0], "adamw_w_in")
    d_out, nm_out, nv_out = _adamw(w_out[0], g_w_out, m_w_out[0], v_w_out[0], "adamw_w_out")
    d_cw, nm_cw, nv_cw = _adamw(conv_w[0], g_conv_w, m_conv_w[0], v_conv_w[0], "adamw_conv_w")
    small_w = [norm_w, sb_norm_w, ssd_norm_w, conv_b, q_norm_w, k_norm_w, dt_bias, A_log, D_skip]
    small_m = [m_norm_w, m_sb_norm_w, m_ssd_norm_w, m_conv_b, m_q_norm_w, m_k_norm_w, m_dt_bias, m_A_log, m_D_skip]
    small_v = [v_norm_w, v_sb_norm_w, v_ssd_norm_w, v_conv_b, v_q_norm_w, v_k_norm_w, v_dt_bias, v_A_log, v_D_skip]
    d_s, nm_s, nv_s = _adamw(_pack(small_w), _pack(g_small[:9]), _pack(small_m), _pack(small_v), "adamw_small")
    d_s, nm_s, nv_s = (_unpack(t, small_shapes) for t in (d_s, nm_s, nv_s))

    def ordered(s, w_in_, conv_w_, w_out_):
        return [s[0], w_in_[None], s[4], s[5], conv_w_[None], s[3], s[6], s[7], s[8], s[1], s[2], w_out_[None]]

    return (loss, grad_x2.reshape(Bl, L, D),
            *ordered(g_small[:9], g_w_in, g_conv_w, g_w_out),
            *ordered(d_s, d_in, d_cw, d_out),
            *ordered(nm_s, nm_in, nm_cw, nm_out),
            *ordered(nv_s, nv_in, nv_cw, nv_out))
```

```python
import functools
import math

import jax
import jax.numpy as jnp
from jax import lax
from jax.experimental import pallas as pl
from jax.experimental.pallas import tpu as pltpu

F32 = jnp.float32
BF16 = jnp.bfloat16
EPS = 1e-6
HEAD_DIM = 64
PAIR = 2 * HEAD_DIM
LANES = 128
SSD_STATE = 128
SSD_GROUPS = 2
BLK = 128
PREP_BLOCKS = 4
Q_TOGETHER = 2
FIRST_LEFT = 2
UNDERFLOW = -105.0
CONV_K = 4
HALO = 8
N_CHIPS = 4
ADAM_LR, ADAM_B1, ADAM_B2, ADAM_EPS, ADAM_WD, ADAM_STEP = 0.001, 0.9, 0.999, 1e-08, 0.01, 10
VMEM_LIMIT_V7X = 56 * 1024 * 1024
MESH = pl.DeviceIdType.MESH
ANY = pl.BlockSpec(memory_space=pl.ANY)
NT = (((1,), (1,)), ((), ()))


def _params(sem=None):
    kw = dict(vmem_limit_bytes=VMEM_LIMIT_V7X)
    if sem is not None:
        kw["dimension_semantics"] = sem
    return pltpu.CompilerParams(**kw)


def _dot(a, b):
    return jnp.dot(a, b, preferred_element_type=F32)


def _dot_nt(a, b):
    return lax.dot_general(a, b, NT, preferred_element_type=F32)


def _dot_split(m, x):
    hi = x.astype(BF16)
    lo = (x - hi.astype(F32)).astype(BF16)
    return _dot(m, hi) + _dot(m, lo)


def _iota(shape, dim):
    return lax.broadcasted_iota(jnp.int32, shape, dim)


def _rowsum(x):
    return jnp.sum(x, axis=1, keepdims=True)


def _colsum(x):
    return jnp.sum(x, axis=0, keepdims=True)


def _sigmoid(x):
    return 1.0 / (1.0 + jnp.exp(-x))


def _softplus(x):
    return jnp.maximum(x, 0.0) + jnp.log(1.0 + jnp.exp(-jnp.abs(x)))


def _inproj(x2, norm_w, w_pad):
    T, D = x2.shape
    P = w_pad.shape[1]
    tm = min(1024, T)
    tn = 1024 if P % 1024 == 0 else 512
    nj = P // tn

    def body(x_ref, nw_ref, w_ref, proj_ref, hnt_ref, wt_ref, hn_s):
        @pl.when(pl.program_id(1) == 0)
        def _():
            xv = x_ref[...]
            r = lax.rsqrt(jnp.mean(xv * xv, axis=-1, keepdims=True) + EPS)
            hn = xv * r * nw_ref[...]
            hn_s[...] = hn.astype(BF16)
            hnt_ref[...] = hn.T.astype(BF16)

        @pl.when(pl.program_id(0) == 0)
        def _():
            wt_ref[...] = w_ref[...].astype(F32).T.astype(BF16)

        proj_ref[...] = _dot(hn_s[...], w_ref[...])

    return pl.pallas_call(
        body,
        name="inproj",
        grid=(T // tm, P // tn),
        in_specs=[
            pl.BlockSpec((tm, D), lambda i, j: (i, 0)),
            pl.BlockSpec((1, D), lambda i, j: (0, 0)),
            pl.BlockSpec((D, tn), lambda i, j: (0, j)),
        ],
        out_specs=[
            pl.BlockSpec((tm, tn), lambda i, j: (i, j)),
            pl.BlockSpec((D, tm), lambda i, j: (0, i)),
            pl.BlockSpec((tn, D), lambda i, j: (jnp.where(i == 0, j, nj - 1), 0)),
        ],
        out_shape=[jax.ShapeDtypeStruct((T, P), F32), jax.ShapeDtypeStruct((D, T), BF16),
                   jax.ShapeDtypeStruct((P, D), BF16)],
        scratch_shapes=[pltpu.VMEM((tm, D), BF16)],
        compiler_params=_params(("arbitrary", "arbitrary")),
    )(x2, norm_w, w_pad)


def _pair_ones():
    ri = ((_iota((2 * PAIR, PAIR), 0) % PAIR) >= HEAD_DIM).astype(jnp.int32)
    ci = (_iota((2 * PAIR, PAIR), 1) >= HEAD_DIM).astype(jnp.int32)
    return jnp.where(ri == ci, 1.0, 0.0).astype(BF16)


def _pair_rms(v, ones2):
    return lax.rsqrt(_split_dots([v * v], ones2)[0] * (1.0 / HEAD_DIM) + EPS)


def _pair_mean(v, ones2):
    return _split_dots([v], ones2)[0] * (1.0 / HEAD_DIM)


def _suffix_ones():
    ri = _iota((2 * BLK, 2 * BLK), 0) % BLK
    ci = _iota((2 * BLK, 2 * BLK), 1)
    return jnp.where((ci >= BLK) | (ri > ci), 1.0, 0.0).astype(BF16)


def _split_dots(xs, m2):
    his = [x.astype(BF16) for x in xs]
    los = [(x - hi.astype(F32)).astype(BF16) for x, hi in zip(xs, his)]
    return [_dot(jnp.concatenate([hi, lo], axis=1), m2) for hi, lo in zip(his, los)]


def _sb_tiles(streams, km_s, uo):
    tiles = [(s, u, h) for s, st in enumerate(streams) for u in range(len(st["kbs"])) for h in range(2)]
    z2s = {(s, u): _dot_nt(st["q"], km_s[kb]) for s, st in enumerate(streams) for u, kb in enumerate(st["kbs"])}
    zs = [z2s[s, u][:, h * BLK:(h + 1) * BLK] for s, u, h in tiles]
    es = [jnp.exp(-jnp.abs(z)) for z in zs]
    las = [jnp.minimum(z, 0.0) - jnp.log(1.0 + e) for z, e in zip(zs, es)]
    lns = [a - z for a, z in zip(las, zs)]
    masks = [streams[s]["masks"][u] for s, u, h in tiles]
    lks = [lk if m is None else jnp.where(m, lk, 0.0) for m, lk in zip(masks, lns)]
    css = _split_dots(lks, uo)
    rests = [list(st["rest"]) for st in streams]
    ws = []
    for (s, u, h), m, a, cs in zip(tiles, masks, las, css):
        w = jnp.exp(a + rests[s][h] + cs[:, :BLK])
        ws.append(w if m is None else jnp.where(m, w, 0.0))
        rests[s][h] = rests[s][h] + cs[:, BLK:]
    return tiles, las, lns, ws, rests


def _stream(q_pair, qi, n_left, diag, zero):
    return dict(q=q_pair, kbs=[qi - u for u in range(n_left + 1)], masks=[diag] + [None] * n_left, rest=[zero, zero])


def _row0(block):
    return block * BLK if isinstance(block, int) else pl.multiple_of(block * BLK, BLK)


def _pair_of(vals, tiles, s, u):
    return [v for v, t in zip(vals, tiles) if t[0] == s and t[1] == u]


def _block_groups(nq):
    assert nq % Q_TOGETHER == 0 and Q_TOGETHER >= FIRST_LEFT
    return list(range(Q_TOGETHER)), nq // Q_TOGETHER


def _attn_prep(src_ref, w_ref, dst_s, n_blocks, scale):
    per = math.gcd(PREP_BLOCKS, n_blocks)
    rows = per * BLK
    lo = _iota((rows, PAIR), 1) < HEAD_DIM
    ones2 = _pair_ones()

    def step(i, carry):
        r0 = pl.multiple_of(i * rows, rows)
        v = src_ref[0, pl.ds(r0, rows), :]
        if w_ref is not None:
            v = v * _pair_rms(v, ones2) * w_ref[...]
        if scale != 1.0:
            v = v * scale
        v0, v1 = jnp.where(lo, v, 0.0).astype(BF16), jnp.where(lo, 0.0, v).astype(BF16)
        for b in range(per):
            dst_s[i * per + b, 0:BLK, :] = v0[b * BLK:(b + 1) * BLK]
            dst_s[i * per + b, BLK:2 * BLK, :] = v1[b * BLK:(b + 1) * BLK]
        return carry

    lax.fori_loop(0, n_blocks // per, step, 0)


def _attn_fwd(proj3, qw2, kw2, D):
    Bl, L, _ = proj3.shape
    n_pair = D // PAIR
    nq = L // BLK
    scale = 1.0 / math.sqrt(HEAD_DIM)

    def body(q_ref, k_ref, v_ref, qw_ref, kw_ref, o_ref, qm_s, km_s, vm_s):
        uo = _suffix_ones()
        diag = _iota((BLK, BLK), 1) < _iota((BLK, BLK), 0)
        _attn_prep(q_ref, qw_ref, qm_s, nq, scale)
        _attn_prep(k_ref, kw_ref, km_s, nq, 1.0)
        _attn_prep(v_ref, None, vm_s, nq, 1.0)

        zero_c = jnp.zeros((BLK, BLK), F32)

        def q_of(qi):
            return qm_s[qi, 0:BLK, :] + qm_s[qi, BLK:2 * BLK, :]

        def values(streams, accs):
            tiles, _, _, ws, rests = _sb_tiles(streams, km_s, uo)
            wbs = [w.astype(BF16) for w in ws]
            accs = list(accs)
            for s, st in enumerate(streams):
                for u, kb in enumerate(st["kbs"]):
                    accs[s] = accs[s] + _dot(jnp.concatenate(_pair_of(wbs, tiles, s, u), axis=1), vm_s[kb])
            return accs, rests

        def group(qis, n_lefts):
            streams = [_stream(q_of(qi), qi, n, diag, zero_c) for qi, n in zip(qis, n_lefts)]
            accs, rests = values(streams, [jnp.zeros((BLK, PAIR), F32)] * len(qis))
            for qi, n, q, acc, rc in zip(qis, n_lefts, [st["q"] for st in streams], accs, rests):

                def sweep(state, n_blocks, q=q):
                    kb, rc0, rc1, acc1, _ = state
                    st = dict(q=q, kbs=[kb - u for u in range(n_blocks)], masks=[None] * n_blocks, rest=[rc0, rc1])
                    (acc1,), (r,) = values([st], [acc1])
                    return kb - n_blocks, r[0], r[1], acc1, jnp.maximum(jnp.max(r[0]), jnp.max(r[1]))

                state = (jnp.asarray(qi - n - 1, jnp.int32), rc[0], rc[1], acc, jnp.maximum(jnp.max(rc[0]), jnp.max(rc[1])))
                state = lax.while_loop(lambda t: (t[0] >= 1) & (t[4] >= UNDERFLOW), lambda t: sweep(t, 2), state)
                state = lax.while_loop(lambda t: (t[0] >= 0) & (t[4] >= UNDERFLOW), lambda t: sweep(t, 1), state)
                o_ref[0, pl.ds(_row0(qi), BLK), :] = state[3]

        head, n_groups = _block_groups(nq)
        group(head, [min(qi, FIRST_LEFT) for qi in head])

        def groups(g, carry):
            group([g * Q_TOGETHER + j for j in range(Q_TOGETHER)], [FIRST_LEFT] * Q_TOGETHER)
            return carry

        lax.fori_loop(1, n_groups, groups, 0)

    blk = lambda off: pl.BlockSpec((1, L, PAIR), lambda b, p: (b, 0, off + p))
    wspec = pl.BlockSpec((1, PAIR), lambda b, p: (0, 0))
    return pl.pallas_call(
        body,
        name="sb_attn_fwd",
        grid=(Bl, n_pair),
        in_specs=[blk(0), blk(n_pair), blk(2 * n_pair), wspec, wspec],
        out_specs=pl.BlockSpec((1, L, PAIR), lambda b, p: (b, 0, p)),
        out_shape=jax.ShapeDtypeStruct((Bl, L, D), F32),
        scratch_shapes=[pltpu.VMEM((nq, 2 * BLK, PAIR), BF16)] * 3,
        compiler_params=_params(("parallel", "parallel")),
    )(proj3, proj3, proj3, qw2, kw2)


def _attn_bwd(proj3, o3, do3, qw2, kw2, D):
    Bl, L, _ = proj3.shape
    n_pair = D // PAIR
    nq = L // BLK
    scale = 1.0 / math.sqrt(HEAD_DIM)

    def body(q_ref, k_ref, v_ref, o_ref, do_ref, qw_ref, kw_ref, dq_ref, dk_ref, dv_ref, dw_ref,
             qm_s, km_s, vm_s, dom_s, dq_s, dk_s, dv_s):
        uo = _suffix_ones()
        diag = _iota((BLK, BLK), 1) < _iota((BLK, BLK), 0)
        ones2 = _pair_ones()
        _attn_prep(q_ref, qw_ref, qm_s, nq, scale)
        _attn_prep(k_ref, kw_ref, km_s, nq, 1.0)
        _attn_prep(v_ref, None, vm_s, nq, 1.0)
        _attn_prep(do_ref, None, dom_s, nq, 1.0)

        @pl.when((pl.program_id(0) == 0) & (pl.program_id(1) == 0))
        def _():
            dw_ref[...] = jnp.zeros_like(dw_ref)

        def zero(i, carry):
            r0 = pl.multiple_of(i * BLK, BLK)
            dk_s[pl.ds(r0, BLK), :] = jnp.zeros((BLK, PAIR), F32)
            dv_s[pl.ds(r0, BLK), :] = jnp.zeros((BLK, PAIR), F32)
            return carry

        lax.fori_loop(0, nq, zero, 0)

        zero_c = jnp.zeros((BLK, BLK), F32)

        def tiles_bwd(streams, dqas):
            dw2s = {(s, u): _dot_nt(st["do"], vm_s[kb]) for s, st in enumerate(streams) for u, kb in enumerate(st["kbs"])}
            tiles, las, lns, ws, rests = _sb_tiles(streams, km_s, uo)
            dws = [dw2s[s, u][:, h * BLK:(h + 1) * BLK] for s, u, h in tiles]
            wfs = [w.astype(BF16).astype(F32) for w in ws]
            gs = [wf * dw for wf, dw in zip(wfs, dws)]
            gss = _split_dots(gs, uo)
            gcs = [list(st["g_rest"]) for st in streams]
            dzs = []
            for (s, u, h), a, ln, g, gsum in zip(tiles, las, lns, gs, gss):
                g_before = streams[s]["delta"][h] - (gcs[s][h] + gsum[:, :BLK] + g)
                gcs[s][h] = gcs[s][h] + gsum[:, BLK:]
                dz = g * jnp.exp(ln) - g_before * jnp.exp(a)
                m = streams[s]["masks"][u]
                dzs.append(dz if m is None else jnp.where(m, dz, 0.0))
            wts = [wf.T.astype(BF16) for wf in wfs]
            dzts = [dz.T.astype(BF16) for dz in dzs]
            dzbs = [dz.astype(BF16) for dz in dzs]
            dqas = list(dqas)
            for s, st in enumerate(streams):
                for u, kb in enumerate(st["kbs"]):
                    c0 = _row0(kb)
                    dv_s[pl.ds(c0, BLK), :] += _dot(jnp.concatenate(_pair_of(wts, tiles, s, u), axis=1), dom_s[st["qi"]])
                    dk_s[pl.ds(c0, BLK), :] += _dot(jnp.concatenate(_pair_of(dzts, tiles, s, u), axis=1), qm_s[st["qi"]])
                    dqas[s] = dqas[s] + _dot(jnp.concatenate(_pair_of(dzbs, tiles, s, u), axis=1), km_s[kb])
            return dqas, rests, gcs

        def group(qis, n_lefts):
            streams = []
            for qi, n in zip(qis, n_lefts):
                o_blk = o_ref[0, pl.ds(_row0(qi), BLK), :]
                doms = [dom_s[qi, 0:BLK, :], dom_s[qi, BLK:2 * BLK, :]]
                st = _stream(qm_s[qi, 0:BLK, :] + qm_s[qi, BLK:2 * BLK, :], qi, n, diag, zero_c)
                st.update(qi=qi, do=doms[0] + doms[1], delta=[_rowsum(d.astype(F32) * o_blk) for d in doms],
                          g_rest=[zero_c, zero_c])
                streams.append(st)
            dqas, rests, gcs = tiles_bwd(streams, [jnp.zeros((BLK, PAIR), F32)] * len(qis))
            for qi, n, st0, dqa, rc, gc in zip(qis, n_lefts, streams, dqas, rests, gcs):

                def sweep(state, n_blocks, st0=st0):
                    kb, rc0, rc1, gc0, gc1, dqa1, _ = state
                    st = dict(st0, kbs=[kb - u for u in range(n_blocks)], masks=[None] * n_blocks, rest=[rc0, rc1],
                              g_rest=[gc0, gc1])
                    (dqa1,), (r,), (g,) = tiles_bwd([st], [dqa1])
                    return kb - n_blocks, r[0], r[1], g[0], g[1], dqa1, jnp.maximum(jnp.max(r[0]), jnp.max(r[1]))

                state = (jnp.asarray(qi - n - 1, jnp.int32), rc[0], rc[1], gc[0], gc[1], dqa,
                         jnp.maximum(jnp.max(rc[0]), jnp.max(rc[1])))
                state = lax.while_loop(lambda t: (t[0] >= 1) & (t[6] >= UNDERFLOW), lambda t: sweep(t, 2), state)
                state = lax.while_loop(lambda t: (t[0] >= 0) & (t[6] >= UNDERFLOW), lambda t: sweep(t, 1), state)
                dq_s[pl.ds(_row0(qi), BLK), :] = state[5] * scale

        head, n_groups = _block_groups(nq)
        group(head, [min(qi, FIRST_LEFT) for qi in head])

        def groups(g, carry):
            group([g * Q_TOGETHER + j for j in range(Q_TOGETHER)], [FIRST_LEFT] * Q_TOGETHER)
            return carry

        lax.fori_loop(1, n_groups, groups, 0)

        per = math.gcd(PREP_BLOCKS, nq)
        rows = per * BLK

        def finish(i, carry):
            r0 = pl.multiple_of(i * rows, rows)
            dwq, dwk = carry
            out = []
            for src_ref, w_ref, d_s in ((q_ref, qw_ref, dq_s), (k_ref, kw_ref, dk_s)):
                v = src_ref[0, pl.ds(r0, rows), :]
                r = _pair_rms(v, ones2)
                vh = v * r
                dy = d_s[pl.ds(r0, rows), :]
                dvh = dy * w_ref[...]
                out.append((r * (dvh - vh * _pair_mean(dvh * vh, ones2)), _colsum(dy * vh)))
            dq_ref[0, pl.ds(r0, rows), :] = out[0][0].astype(BF16)
            dk_ref[0, pl.ds(r0, rows), :] = out[1][0].astype(BF16)
            dv_ref[0, pl.ds(r0, rows), :] = dv_s[pl.ds(r0, rows), :].astype(BF16)
            return dwq + out[0][1], dwk + out[1][1]

        zrow = jnp.zeros((1, PAIR), F32)
        dwq, dwk = lax.fori_loop(0, nq // per, finish, (zrow, zrow))
        dw_ref[0:1, :] += dwq
        dw_ref[1:2, :] += dwk

    blk = lambda off: pl.BlockSpec((1, L, PAIR), lambda b, p: (b, 0, off + p))
    wspec = pl.BlockSpec((1, PAIR), lambda b, p: (0, 0))
    oblk = pl.BlockSpec((1, L, PAIR), lambda b, p: (b, 0, p))
    return pl.pallas_call(
        body,
        name="sb_attn_bwd",
        grid=(Bl, n_pair),
        in_specs=[blk(0), blk(n_pair), blk(2 * n_pair), oblk, oblk, wspec, wspec],
        out_specs=[oblk, oblk, oblk, pl.BlockSpec((8, PAIR), lambda b, p: (0, 0))],
        out_shape=[jax.ShapeDtypeStruct((Bl, L, D), BF16)] * 3 + [jax.ShapeDtypeStruct((8, PAIR), F32)],
        scratch_shapes=[pltpu.VMEM((nq, 2 * BLK, PAIR), BF16)] * 4 + [pltpu.VMEM((L, PAIR), F32)] * 3,
        compiler_params=_params(("arbitrary", "arbitrary")),
    )(proj3, proj3, proj3, o3, do3, qw2, kw2)


def _conv_pre(ext_s, halo_ref, raw_ref, w_ref, b_ref, first):
    ext_s[0:HALO, :] = jnp.where(first, 0.0, halo_ref[0])
    ext_s[HALO:HALO + BLK, :] = raw_ref[0]
    pre = b_ref[...]
    for i in range(CONV_K):
        pre = pre + ext_s[pl.ds(HALO - (CONV_K - 1 - i), BLK), :] * w_ref[i:i + 1, :]
    return pre


def _lane_col(m, lane, h):
    return _rowsum(jnp.where(lane == h, m, 0.0))


def _half_sums(row, lo1):
    return _rowsum(jnp.where(lo1, row, 0.0)), _rowsum(jnp.where(lo1, 0.0, row))


def _ssd_specs(Bl, L, D, rev):
    nc = L // BLK
    rows_per = BLK // HALO
    cidx = (lambda c: nc - 1 - c) if rev else (lambda c: c)
    xoff = 5
    boff = (6 * D) // 512
    doff = (6 * D + 512) // LANES
    prev = lambda c: jnp.maximum(cidx(c) * rows_per - 1, 0)
    specs = [
        pl.BlockSpec((1, BLK, D), lambda b, c: (b, cidx(c), xoff)),
        pl.BlockSpec((1, BLK, 512), lambda b, c: (b, cidx(c), boff)),
        pl.BlockSpec((1, HALO, D), lambda b, c: (b, prev(c), xoff)),
        pl.BlockSpec((1, HALO, 512), lambda b, c: (b, prev(c), boff)),
        pl.BlockSpec((1, BLK, LANES), lambda b, c: (b, cidx(c), doff)),
    ]
    full = lambda shape: pl.BlockSpec(shape, lambda b, c: (0,) * len(shape))
    specs += [full((CONV_K, D)), full((CONV_K, 512)), full((1, D)), full((1, 512)),
              full((1, LANES)), full((1, LANES)), full((1, LANES))]
    return specs, cidx


def _ssd_common(dtr_ref, dtb_ref, alog_ref, acs_s, acsT_s):
    ltri = jnp.where(_iota((BLK, BLK), 1) <= _iota((BLK, BLK), 0), 1.0, 0.0).astype(BF16)
    dtv = _softplus(dtr_ref[0] + dtb_ref[...])
    a = -jnp.exp(alog_ref[...])
    acs = _dot_split(ltri, dtv * a)
    acs_s[...] = acs
    acsT_s[...] = acs.T
    return dtv, a, acs


def _pair_terms(pr, acs, dtv, acs_s, lane, lo, lane1, lo1):
    h0, h1 = 2 * pr, 2 * pr + 1
    c0, c1 = _lane_col(acs, lane, h0), _lane_col(acs, lane, h1)
    d0, d1 = _lane_col(dtv, lane, h0), _lane_col(dtv, lane, h1)
    lastv = acs_s[BLK - 1:BLK, :]
    l0, l1 = _lane_col(lastv, lane1, h0), _lane_col(lastv, lane1, h1)
    return dict(h=(h0, h1), c=(c0, c1), last=(l0, l1), acs_p=jnp.where(lo, c0, c1), dt_p=jnp.where(lo, d0, d1),
                last_p=jnp.where(lo1, l0, l1))


def _decay_tiles(cc, row, tri, want_t):
    lm = jnp.where(tri, jnp.exp(cc - row), 0.0)
    if not want_t:
        return lm, None
    tri_t = _iota((BLK, BLK), 1) >= _iota((BLK, BLK), 0)
    return lm, jnp.where(tri_t, jnp.exp(row - cc), 0.0)


def _ssd_fwd(proj3, cwx, cwb, cbx, cbb, dtb, alog, dsk, D):
    Bl, L, _ = proj3.shape
    nc = L // BLK
    n_pair = D // PAIR
    pairs_per_group = n_pair // SSD_GROUPS
    specs, _ = _ssd_specs(Bl, L, D, False)

    def body(xr_ref, bcr_ref, xh_ref, bch_ref, dtr_ref, cwx_ref, cwb_ref, cbx_ref, cbb_ref, dtb_ref, alog_ref,
             dsk_ref, y_ref, sin_ref, st_s, extx_s, extb_s, acs_s, acsT_s):
        first = pl.program_id(1) == 0

        @pl.when(first)
        def _():
            st_s[...] = jnp.zeros_like(st_s)

        lane, lane1 = _iota((BLK, LANES), 1), _iota((1, LANES), 1)
        lo, lo1 = lane < HEAD_DIM, lane1 < HEAD_DIM
        tri = _iota((BLK, BLK), 1) <= _iota((BLK, BLK), 0)
        pre = _conv_pre(extx_s, xh_ref, xr_ref, cwx_ref, cbx_ref, first)
        ux = pre * _sigmoid(pre)
        pre = _conv_pre(extb_s, bch_ref, bcr_ref, cwb_ref, cbb_ref, first)
        ub = pre * _sigmoid(pre)
        dtv, a, acs = _ssd_common(dtr_ref, dtb_ref, alog_ref, acs_s, acsT_s)
        for g in range(SSD_GROUPS):
            bg = ub[:, g * SSD_STATE:(g + 1) * SSD_STATE]
            cb_ = ub[:, (SSD_GROUPS + g) * SSD_STATE:(SSD_GROUPS + g + 1) * SSD_STATE].astype(BF16)
            cbm = _dot_nt(cb_, bg.astype(BF16))
            btb = bg.T.astype(BF16)
            for pr in range(g * pairs_per_group, (g + 1) * pairs_per_group):
                t = _pair_terms(pr, acs, dtv, acs_s, lane, lo, lane1, lo1)
                xs_p = ux[:, pr * PAIR:(pr + 1) * PAIR]
                x_p = xs_p * t["dt_p"]
                st = st_s[pr]
                sin_ref[0, 0, pr] = st
                y = _dot(cb_, st.astype(BF16)) * jnp.exp(t["acs_p"])
                for k in range(2):
                    row = acsT_s[t["h"][k]:t["h"][k] + 1, :]
                    lm, _ = _decay_tiles(t["c"][k], row, tri, False)
                    xm = jnp.where(lo if k == 0 else ~lo, x_p, 0.0).astype(BF16)
                    y = y + _dot((cbm * lm).astype(BF16), xm)
                d_p = jnp.where(lo1, _lane_col(dsk_ref[...], lane1, t["h"][0]), _lane_col(dsk_ref[...], lane1, t["h"][1]))
                y_ref[0, :, pr * PAIR:(pr + 1) * PAIR] = y + d_p * xs_p
                xd = (x_p * jnp.exp(t["last_p"] - t["acs_p"])).astype(BF16)
                st_s[pr] = st * jnp.exp(t["last_p"]) + _dot(btb, xd)

    return pl.pallas_call(
        body,
        name="ssd_fwd",
        grid=(Bl, nc),
        in_specs=specs,
        out_specs=[
            pl.BlockSpec((1, BLK, D), lambda b, c: (b, c, 0)),
            pl.BlockSpec((1, 1, n_pair, SSD_STATE, PAIR), lambda b, c: (b, c, 0, 0, 0)),
        ],
        out_shape=[jax.ShapeDtypeStruct((Bl, L, D), F32),
                   jax.ShapeDtypeStruct((Bl, nc, n_pair, SSD_STATE, PAIR), F32)],
        scratch_shapes=[pltpu.VMEM((n_pair, SSD_STATE, PAIR), F32), pltpu.VMEM((HALO + BLK, D), F32),
                        pltpu.VMEM((HALO + BLK, 512), F32), pltpu.VMEM((BLK, LANES), F32),
                        pltpu.VMEM((LANES, BLK), F32)],
        compiler_params=_params(("arbitrary", "arbitrary")),
    )(proj3, proj3, proj3, proj3, proj3, cwx, cwb, cbx, cbb, dtb, alog, dsk)


def _ssd_bwd(proj3, s_in, dy3, cwx, cwb, cbx, cbb, dtb, alog, dsk, D, tail):
    Bl, L, _ = proj3.shape
    CD = D + 512
    nc = L // BLK
    n_pair = D // PAIR
    n_heads = 2 * n_pair
    pairs_per_group = n_pair // SSD_GROUPS
    specs, cidx = _ssd_specs(Bl, L, D, True)
    specs = specs + [
        pl.BlockSpec((1, 1, n_pair, SSD_STATE, PAIR), lambda b, c: (b, cidx(c), 0, 0, 0)),
        pl.BlockSpec((1, BLK, D), lambda b, c: (b, cidx(c), 0)),
    ]

    def body(xr_ref, bcr_ref, xh_ref, bch_ref, dtr_ref, cwx_ref, cwb_ref, cbx_ref, cbb_ref, dtb_ref, alog_ref,
             dsk_ref, sin_ref, dy_ref, dxbc_ref, dcwx_ref, dcwb_ref, dcbx_ref, dcbb_ref, misc_ref,
             dst_s, extx_s, extb_s, acs_s, acsT_s, dux_s, dub_s, e2x_s, e2b_s, nxx_s, nxb_s):
        step = pl.program_id(1)
        first = step == nc - 1
        last = step == 0

        @pl.when(last)
        def _():
            dst_s[...] = jnp.zeros_like(dst_s)
            nxx_s[...] = jnp.zeros_like(nxx_s)
            nxb_s[...] = jnp.zeros_like(nxb_s)

        @pl.when(last & (pl.program_id(0) == 0))
        def _():
            for r in (dcwx_ref, dcwb_ref, dcbx_ref, dcbb_ref, misc_ref):
                r[...] = jnp.zeros_like(r)

        lane, lane1 = _iota((BLK, LANES), 1), _iota((1, LANES), 1)
        lo, lo1 = lane < HEAD_DIM, lane1 < HEAD_DIM
        tri = _iota((BLK, BLK), 1) <= _iota((BLK, BLK), 0)
        prex = _conv_pre(extx_s, xh_ref, xr_ref, cwx_ref, cbx_ref, first)
        sgx = _sigmoid(prex)
        ux = prex * sgx
        preb = _conv_pre(extb_s, bch_ref, bcr_ref, cwb_ref, cbb_ref, first)
        sgb = _sigmoid(preb)
        ub = preb * sgb
        dtv, a, acs = _ssd_common(dtr_ref, dtb_ref, alog_ref, acs_s, acsT_s)
        dacs = jnp.zeros((BLK, LANES), F32)
        dlast = jnp.zeros((1, LANES), F32)
        ddt = jnp.zeros((BLK, LANES), F32)
        dd = jnp.zeros((1, LANES), F32)
        for g in range(SSD_GROUPS):
            bg = ub[:, g * SSD_STATE:(g + 1) * SSD_STATE]
            cg = ub[:, (SSD_GROUPS + g) * SSD_STATE:(SSD_GROUPS + g + 1) * SSD_STATE]
            bb, cb_ = bg.astype(BF16), cg.astype(BF16)
            cbm = _dot_nt(cb_, bb)
            cbt = _dot_nt(bb, cb_)
            ctb = cg.T.astype(BF16)
            dbg = jnp.zeros((BLK, SSD_STATE), F32)
            dcg = jnp.zeros((BLK, SSD_STATE), F32)
            for pr in range(g * pairs_per_group, (g + 1) * pairs_per_group):
                t = _pair_terms(pr, acs, dtv, acs_s, lane, lo, lane1, lo1)
                h0, h1 = t["h"]
                xs_p = ux[:, pr * PAIR:(pr + 1) * PAIR]
                dy_p = dy_ref[0, :, pr * PAIR:(pr + 1) * PAIR]
                x_p = xs_p * t["dt_p"]
                ea_p = jnp.exp(t["acs_p"])
                dte_p = jnp.exp(t["last_p"] - t["acs_p"])
                cd_p = jnp.exp(t["last_p"])
                st = sin_ref[0, 0, pr]
                dst = dst_s[pr]
                stb, dstb = st.astype(BF16), dst.astype(BF16)
                s0, s1 = _half_sums(_colsum(dy_p * xs_p), lo1)
                dd = dd + jnp.where(lane1 == h0, s0, 0.0) + jnp.where(lane1 == h1, s1, 0.0)
                d_p = jnp.where(lo1, _lane_col(dsk_ref[...], lane1, h0), _lane_col(dsk_ref[...], lane1, h1))
                dxs_p = d_p * dy_p
                dp = dy_p * ea_p
                dpb = dp.astype(BF16)
                yo = dp * _dot(cb_, stb)
                dcg = dcg + _dot_nt(dpb, stb)
                dst_off = _dot(ctb, dpb)
                dac = [_rowsum(jnp.where(lo, yo, 0.0)), _rowsum(jnp.where(lo, 0.0, yo))]
                s0, s1 = _half_sums(_colsum(dst * st), lo1)
                dl = [s0 * jnp.exp(t["last"][0]), s1 * jnp.exp(t["last"][1])]
                dxd = _dot(bb, dstb)
                dx_p = dxd * dte_p
                tt = dxd * x_p
                dbg = dbg + _dot_nt((x_p * dte_p).astype(BF16), dstb)
                for k, ddte in enumerate((_rowsum(jnp.where(lo, tt, 0.0)), _rowsum(jnp.where(lo, 0.0, tt)))):
                    ek = ddte * jnp.exp(t["last"][k] - t["c"][k])
                    dl[k] = dl[k] + _colsum(ek)
                    dac[k] = dac[k] - ek
                for k in range(2):
                    row = acsT_s[t["h"][k]:t["h"][k] + 1, :]
                    lm, lmt = _decay_tiles(t["c"][k], row, tri, True)
                    msk = lo if k == 0 else ~lo
                    xm = jnp.where(msk, x_p, 0.0).astype(BF16)
                    dym = jnp.where(msk, dy_p, 0.0).astype(BF16)
                    dm = _dot_nt(dym, xm)
                    dmt = _dot_nt(xm, dym)
                    mt = cbt * lmt
                    dx_p = dx_p + _dot(mt.astype(BF16), dym)
                    dac[k] = dac[k] + _rowsum(dm * (cbm * lm)) - _rowsum(dmt * mt)
                    dcg = dcg + _dot((dm * lm).astype(BF16), bb)
                    dbg = dbg + _dot((dmt * lmt).astype(BF16), cb_)
                dacs = dacs + jnp.where(lane == h0, dac[0], 0.0) + jnp.where(lane == h1, dac[1], 0.0)
                dlast = dlast + jnp.where(lane1 == h0, dl[0], 0.0) + jnp.where(lane1 == h1, dl[1], 0.0)
                dxs_p = dxs_p + dx_p * t["dt_p"]
                t3 = dx_p * xs_p
                ddt = ddt + jnp.where(lane == h0, _rowsum(jnp.where(lo, t3, 0.0)), 0.0) \
                    + jnp.where(lane == h1, _rowsum(jnp.where(lo, 0.0, t3)), 0.0)
                dux_s[:, pr * PAIR:(pr + 1) * PAIR] = dxs_p
                dst_s[pr] = dst * cd_p + dst_off
            dub_s[:, g * SSD_STATE:(g + 1) * SSD_STATE] = dbg
            dub_s[:, (SSD_GROUPS + g) * SSD_STATE:(SSD_GROUPS + g + 1) * SSD_STATE] = dcg
        dacs = dacs + jnp.where(_iota((BLK, LANES), 0) == BLK - 1, dlast, 0.0)
        utri = jnp.where(_iota((BLK, BLK), 1) >= _iota((BLK, BLK), 0), 1.0, 0.0).astype(BF16)
        dda = _dot_split(utri, dacs)
        ddt = ddt + dda * a
        ddtr = jnp.where(lane < n_heads, ddt * _sigmoid(dtr_ref[0] + dtb_ref[...]), 0.0)
        dxbc_ref[0, :, CD:CD + LANES] = ddtr.astype(BF16)
        dxbc_ref[0, :, CD + LANES:tail] = jnp.zeros((BLK, tail - CD - LANES), BF16)
        misc_ref[0:1, :] += _colsum(ddtr)
        misc_ref[1:2, :] += jnp.where(lane1 < n_heads, _colsum(dda * dtv) * a, 0.0)
        misc_ref[2:3, :] += dd
        for (du_s, pre, sg, ext_s, e2_s, nx_s, w_ref, dcw_ref, dcb_ref, c0, width) in (
                (dux_s, prex, sgx, extx_s, e2x_s, nxx_s, cwx_ref, dcwx_ref, dcbx_ref, 0, D),
                (dub_s, preb, sgb, extb_s, e2b_s, nxb_s, cwb_ref, dcwb_ref, dcbb_ref, D, 512)):
            dpre = du_s[...] * (sg * (1.0 + pre * (1.0 - sg)))
            dcb_ref[...] += _colsum(dpre)
            for i in range(CONV_K):
                dcw_ref[i:i + 1, :] += _colsum(dpre * ext_s[pl.ds(HALO - (CONV_K - 1 - i), BLK), :])
            e2_s[0:BLK, :] = dpre
            e2_s[BLK:BLK + HALO, :] = nx_s[...]
            dxr = jnp.zeros((BLK, width), F32)
            for i in range(CONV_K):
                dxr = dxr + e2_s[pl.ds(CONV_K - 1 - i, BLK), :] * w_ref[i:i + 1, :]
            dxbc_ref[0, :, c0:c0 + width] = dxr.astype(BF16)
            nx_s[...] = e2_s[0:HALO, :]

    full = lambda shape: pl.BlockSpec(shape, lambda b, c: (0,) * len(shape))
    return pl.pallas_call(
        body,
        name="ssd_bwd",
        grid=(Bl, nc),
        in_specs=specs,
        out_specs=[
            pl.BlockSpec((1, BLK, tail), lambda b, c: (b, cidx(c), 0)),
            full((CONV_K, D)), full((CONV_K, 512)), full((1, D)), full((1, 512)), full((8, LANES)),
        ],
        out_shape=[
            jax.ShapeDtypeStruct((Bl, L, tail), BF16),
            jax.ShapeDtypeStruct((CONV_K, D), F32), jax.ShapeDtypeStruct((CONV_K, 512), F32),
            jax.ShapeDtypeStruct((1, D), F32), jax.ShapeDtypeStruct((1, 512), F32),
            jax.ShapeDtypeStruct((8, LANES), F32),
        ],
        scratch_shapes=[
            pltpu.VMEM((n_pair, SSD_STATE, PAIR), F32),
            pltpu.VMEM((HALO + BLK, D), F32), pltpu.VMEM((HALO + BLK, 512), F32),
            pltpu.VMEM((BLK, LANES), F32), pltpu.VMEM((LANES, BLK), F32),
            pltpu.VMEM((BLK, D), F32), pltpu.VMEM((BLK, 512), F32),
            pltpu.VMEM((BLK + HALO, D), F32), pltpu.VMEM((BLK + HALO, 512), F32),
            pltpu.VMEM((HALO, D), F32), pltpu.VMEM((HALO, 512), F32),
        ],
        compiler_params=_params(("arbitrary", "arbitrary")),
    )(proj3, proj3, proj3, proj3, proj3, cwx, cwb, cbx, cbb, dtb, alog, dsk, s_in, dy3)


def _gate_out(x2, tgt2, o2, proj2, y2, sbw, ssw, w_out_bf, w_out_t):
    T, D = x2.shape
    tm = min(256, T)

    def body(x_ref, t_ref, o_ref, zs_ref, y_ref, zy_ref, sbw_ref, ssw_ref, wo_ref, wot_ref,
             dout_ref, doutb_ref, mixt_ref, do_ref, dy_ref, dz_ref, dnw_ref, loss_ref):
        @pl.when(pl.program_id(0) == 0)
        def _():
            dnw_ref[...] = jnp.zeros_like(dnw_ref)
            loss_ref[...] = jnp.zeros_like(loss_ref)

        def fwd(o, z, w):
            sg = _sigmoid(z)
            sl = z * sg
            g = o * sl
            r = lax.rsqrt(jnp.mean(g * g, axis=-1, keepdims=True) + EPS)
            n = g * r
            return sg, sl, r, n, n * w

        def bwd(dy, o, z, w, sg, sl, r, n):
            dn = dy * w
            dg = r * (dn - n * jnp.mean(dn * n, axis=-1, keepdims=True))
            return dg * sl, dg * o * (sg * (1.0 + z * (1.0 - sg))), _colsum(dy * n)

        o1, z1, w1 = o_ref[...], zs_ref[...], sbw_ref[...]
        o2_, z2, w2 = y_ref[...], zy_ref[...], ssw_ref[...]
        sg1, sl1, r1, n1, y1 = fwd(o1, z1, w1)
        sg2, sl2, r2, n2, y2_ = fwd(o2_, z2, w2)
        y1b, y2b = y1.astype(BF16), y2_.astype(BF16)
        mixt_ref[0:D, :] = y1.T.astype(BF16)
        mixt_ref[D:2 * D, :] = y2_.T.astype(BF16)
        out = x_ref[...] + (_dot(y1b, wo_ref[0:D, :]) + _dot(y2b, wo_ref[D:2 * D, :]))
        err = out - t_ref[...]
        loss_ref[...] += jnp.sum(err * err) * (0.5 / D)
        dout = err * (1.0 / D)
        dout_ref[...] = dout
        doutb = dout.astype(BF16)
        doutb_ref[...] = doutb
        do1, dz1, dw1 = bwd(_dot(doutb, wot_ref[:, 0:D]), o1, z1, w1, sg1, sl1, r1, n1)
        do2, dz2, dw2 = bwd(_dot(doutb, wot_ref[:, D:2 * D]), o2_, z2, w2, sg2, sl2, r2, n2)
        do_ref[...] = do1
        dy_ref[...] = do2
        dz_ref[:, 0:D] = dz1.astype(BF16)
        dz_ref[:, D:2 * D] = dz2.astype(BF16)
        dnw_ref[0:1, :] += dw1
        dnw_ref[1:2, :] += dw2

    row = lambda col: pl.BlockSpec((tm, D), lambda i: (i, col))
    full = lambda shape: pl.BlockSpec(shape, lambda i: (0,) * len(shape))
    wide = pl.BlockSpec((tm, 2 * D), lambda i: (i, 0))
    return pl.pallas_call(
        body,
        name="gate_out",
        grid=(T // tm,),
        in_specs=[row(0), row(0), row(0), row(3), row(0), row(4), full((1, D)), full((1, D)), full((2 * D, D)),
                  full((D, 2 * D))],
        out_specs=[row(0), row(0), pl.BlockSpec((2 * D, tm), lambda i: (0, i)), row(0), row(0), wide,
                   full((8, D)), full((8, LANES))],
        out_shape=[
            jax.ShapeDtypeStruct((T, D), F32), jax.ShapeDtypeStruct((T, D), BF16),
            jax.ShapeDtypeStruct((2 * D, T), BF16), jax.ShapeDtypeStruct((T, D), F32),
            jax.ShapeDtypeStruct((T, D), F32), jax.ShapeDtypeStruct((T, 2 * D), BF16),
            jax.ShapeDtypeStruct((8, D), F32), jax.ShapeDtypeStruct((8, LANES), F32),
        ],
        compiler_params=_params(("arbitrary",)),
    )(x2, tgt2, o2, proj2, y2, proj2, sbw, ssw, w_out_bf, w_out_t)


def _piece_blocks(pieces, D):
    counts = [p.shape[1] // D for p in pieces]
    return [sum(counts[:i]) for i in range(len(counts))], counts


def _dhn(pieces, w_pad_t, x2, dout, norm_w, h_in, h_out):
    T, D = x2.shape
    tm = min(1024, T)
    starts, counts = _piece_blocks(pieces, D)
    nk = sum(counts)
    ni = T // tm
    n_sem = 2 * (N_CHIPS - 1)
    assert nk * D == w_pad_t.shape[0]

    def body(*refs):
        p_refs = refs[:len(pieces)]
        (w_ref, x_hbm, dout_hbm, nw_ref, hin, hout, gx_ref, dnw_ref, rin, rout,
         acc_s, x_s, dout_s, send_sems, recv_sems, row_sems) = refs[len(pieces):]
        i, k = pl.program_id(0), pl.program_id(1)

        def rows():
            r0 = pl.multiple_of(i * tm, tm)
            return [pltpu.make_async_copy(src.at[pl.ds(r0, tm)], dst, row_sems.at[n])
                    for n, (src, dst) in enumerate(((x_hbm, x_s), (dout_hbm, dout_s)))]

        @pl.when(k == 0)
        def _():
            for cp in rows():
                cp.start()

        def scatter():
            x, y, c, chips = _place()
            return [pltpu.make_async_remote_copy(
                src_ref=s.at[2 * px + py], dst_ref=d.at[j], send_sem=send_sems.at[2 * j + m],
                recv_sem=recv_sems.at[2 * j + m], device_id=(px, py, c), device_id_type=MESH)
                for j, (px, py) in enumerate(chips) for m, (s, d) in enumerate(((hin, rin), (hout, rout)))]

        @pl.when((i == 0) & (k == 0))
        def _():
            for cp in scatter():
                cp.start()

        @pl.when((i == ni - 1) & (k == nk - 1))
        def _():
            for cp in scatter():
                cp.wait()

        @pl.when((i == 0) & (k == 0))
        def _():
            dnw_ref[...] = jnp.zeros_like(dnw_ref)

        @pl.when(k == 0)
        def _():
            acc_s[...] = jnp.zeros_like(acc_s)

        for p_ref, s, n in zip(p_refs, starts, counts):
            @pl.when((k >= s) & (k < s + n))
            def _(p_ref=p_ref):
                acc_s[...] += _dot(p_ref[...], w_ref[...])

        @pl.when(k == nk - 1)
        def _():
            for cp in rows():
                cp.wait()
            xv = x_s[...]
            r = lax.rsqrt(jnp.mean(xv * xv, axis=-1, keepdims=True) + EPS)
            xh = xv * r
            dhn = acc_s[...]
            dxh = dhn * nw_ref[...]
            gx_ref[...] = dout_s[...] + r * (dxh - xh * jnp.mean(dxh * xh, axis=-1, keepdims=True))
            dnw_ref[0:1, :] += _colsum(dhn * xh)

    return pl.pallas_call(
        body,
        name="dhn",
        grid=(T // tm, nk),
        in_specs=[pl.BlockSpec((tm, D), lambda i, k, s=s, n=n: (i, jnp.clip(k - s, 0, n - 1)))
                  for s, n in zip(starts, counts)] + [
            pl.BlockSpec((D, D), lambda i, k: (k, 0)),
            ANY, ANY,
            pl.BlockSpec((1, D), lambda i, k: (0, 0)),
            ANY, ANY,
        ],
        out_specs=[pl.BlockSpec((tm, D), lambda i, k: (i, 0)), pl.BlockSpec((8, D), lambda i, k: (0, 0)), ANY, ANY],
        out_shape=[jax.ShapeDtypeStruct((T, D), F32), jax.ShapeDtypeStruct((8, D), F32),
                   jax.ShapeDtypeStruct((N_CHIPS - 1,) + h_in.shape[1:], F32),
                   jax.ShapeDtypeStruct((N_CHIPS - 1,) + h_out.shape[1:], F32)],
        scratch_shapes=[pltpu.VMEM((tm, D), F32)] * 3 + [pltpu.SemaphoreType.DMA((n_sem,)), pltpu.SemaphoreType.DMA((n_sem,)),
                                                      pltpu.SemaphoreType.DMA((2,))],
        compiler_params=_params(("arbitrary", "arbitrary")),
    )(*pieces, w_pad_t, x2, dout, norm_w, h_in, h_out)


def _grad_w_in(hn_t, pieces):
    D, T = hn_t.shape
    tk = min(1024, T)
    starts, counts = _piece_blocks(pieces, D)

    def body(*refs):
        a_ref, p_refs, o_ref = refs[0], refs[1:-1], refs[-1]
        j = pl.program_id(0)

        @pl.when(pl.program_id(1) == 0)
        def _():
            o_ref[...] = jnp.zeros_like(o_ref)

        for p_ref, s, n in zip(p_refs, starts, counts):
            @pl.when((j >= s) & (j < s + n))
            def _(p_ref=p_ref):
                o_ref[...] += _dot(a_ref[...], p_ref[...])

    def piece_spec(s, n):
        return pl.BlockSpec((tk, D), lambda j, k: (jnp.where((j >= s) & (j < s + n), k, 0), jnp.clip(j - s, 0, n - 1)))

    return pl.pallas_call(
        body,
        name="grad_w_in",
        grid=(sum(counts), T // tk),
        in_specs=[pl.BlockSpec((D, tk), lambda j, k: (0, k))] + [piece_spec(s, n) for s, n in zip(starts, counts)],
        out_specs=pl.BlockSpec((D, D), lambda j, k: (0, j)),
        out_shape=jax.ShapeDtypeStruct((D, sum(counts) * D), F32),
        compiler_params=_params(("parallel", "arbitrary")),
    )(hn_t, *pieces)


def _matmul(a, b, name):
    M, K = a.shape
    N = b.shape[1]
    tm = min(1024, M)
    tn = 1024 if N % 1024 == 0 else (512 if N % 512 == 0 else N)
    tk = min(512, K)

    def body(a_ref, b_ref, o_ref):
        @pl.when(pl.program_id(2) == 0)
        def _():
            o_ref[...] = jnp.zeros_like(o_ref)

        o_ref[...] += _dot(a_ref[...], b_ref[...])

    return pl.pallas_call(
        body,
        name=name,
        grid=(M // tm, N // tn, K // tk),
        in_specs=[pl.BlockSpec((tm, tk), lambda i, j, k: (i, k)), pl.BlockSpec((tk, tn), lambda i, j, k: (k, j))],
        out_specs=pl.BlockSpec((tm, tn), lambda i, j, k: (i, j)),
        out_shape=jax.ShapeDtypeStruct((M, N), F32),
        compiler_params=_params(("parallel", "parallel", "arbitrary")),
    )(a, b)


def _adamw(w, g, m, v, name):
    R, C = w.shape
    tr = 256 if R % 256 == 0 else R
    c1 = 1.0 - ADAM_B1 ** ADAM_STEP
    c2 = 1.0 - ADAM_B2 ** ADAM_STEP

    def body(w_ref, g_ref, m_ref, v_ref, d_ref, nm_ref, nv_ref):
        gv = g_ref[...]
        m_new = ADAM_B1 * m_ref[...] + (1.0 - ADAM_B1) * gv
        v_new = ADAM_B2 * v_ref[...] + (1.0 - ADAM_B2) * (gv * gv)
        d_ref[...] = -ADAM_LR * ((m_new / c1) / (jnp.sqrt(v_new / c2) + ADAM_EPS) + ADAM_WD * w_ref[...])
        nm_ref[...] = m_new
        nv_ref[...] = v_new

    spec = pl.BlockSpec((tr, C), lambda i: (i, 0))
    return pl.pallas_call(
        body,
        name=name,
        grid=(R // tr,),
        in_specs=[spec] * 4,
        out_specs=[spec] * 3,
        out_shape=[jax.ShapeDtypeStruct((R, C), F32)] * 3,
        compiler_params=_params(("parallel",)),
    )(w, g, m, v)


def _add_core_half(a, recv, core, name):
    _, n, h, S = a.shape
    th = 256 if h % 256 == 0 else h

    def body(c_ref, a_ref, r_ref, o_ref):
        o_ref[...] = a_ref[...] + r_ref[...]

    return pl.pallas_call(
        body,
        name=name,
        grid_spec=pltpu.PrefetchScalarGridSpec(
            num_scalar_prefetch=1,
            grid=(n, h // th),
            in_specs=[
                pl.BlockSpec((None, None, th, S), lambda p, i, c: (c[0], p, i, 0)),
                pl.BlockSpec((None, th, S), lambda p, i, c: (p, i, 0)),
            ],
            out_specs=pl.BlockSpec((None, th, S), lambda p, i, c: (p, i, 0)),
        ),
        out_shape=jax.ShapeDtypeStruct((n, h, S), F32),
        compiler_params=_params(("parallel", "parallel")),
    )(core, a, recv)


def _add_chips(hsum, recv, chip, name):
    _, h, S = hsum.shape
    th = 256 if h % 256 == 0 else h

    def body(c_ref, a_ref, r_ref, o_ref):
        o_ref[...] = ((a_ref[...] + r_ref[0]) + r_ref[1]) + r_ref[2]

    return pl.pallas_call(
        body,
        name=name,
        grid_spec=pltpu.PrefetchScalarGridSpec(
            num_scalar_prefetch=1,
            grid=(h // th,),
            in_specs=[
                pl.BlockSpec((None, th, S), lambda i, c: (c[0], i, 0)),
                pl.BlockSpec((N_CHIPS - 1, th, S), lambda i, c: (0, i, 0)),
            ],
            out_specs=pl.BlockSpec((th, S), lambda i, c: (i, 0)),
        ),
        out_shape=jax.ShapeDtypeStruct((h, S), F32),
        compiler_params=_params(("parallel",)),
    )(chip, hsum, recv)


def _place():
    x, y, c = lax.axis_index("x"), lax.axis_index("y"), lax.axis_index("c")
    other_chips = [(1 - x, y), (x, 1 - y), (1 - x, 1 - y)]
    return x, y, c, other_chips


def _allgather_weights(w_in_bf, w_out_bf, conv_w):
    D, S = w_in_bf.shape
    R = w_out_bf.shape[0]
    n_ici, n_fwd = 3 * (N_CHIPS - 1), 2 * (N_CHIPS - 1)

    def body(win, wout, cw, gin, gout, gcw, send_sems, recv_sems):
        x, y, c, chips = _place()
        me = 2 * x + y
        sibling = (x, y, 1 - c)
        hin, hout = D // 2, R // 2

        def halves(chip_idx):
            return (gin.at[chip_idx, pl.ds(c * hin, hin)], gout.at[chip_idx, pl.ds(c * hout, hout)])

        def rcopy(k, src, dst, to):
            return pltpu.make_async_remote_copy(src_ref=src, dst_ref=dst, send_sem=send_sems.at[k],
                                                recv_sem=recv_sems.at[k], device_id=to, device_id_type=MESH)

        my_in, my_out = halves(me)
        src_in, src_out = win.at[pl.ds(c * hin, hin)], wout.at[pl.ds(c * hout, hout)]
        sends = []
        for j, chip in enumerate(chips):
            to = (*chip, c)
            sends += [rcopy(3 * j, src_in, my_in, to), rcopy(3 * j + 1, src_out, my_out, to),
                      rcopy(3 * j + 2, cw, gcw.at[me], to)]
        for cp in sends:
            cp.start()
        passed = []
        for j, (px, py) in enumerate(chips):
            their_in, their_out = halves(2 * px + py)
            rcopy(3 * j, their_in, their_in, sibling).wait_recv()
            rcopy(3 * j + 1, their_out, their_out, sibling).wait_recv()
            rcopy(3 * j + 2, cw, gcw.at[2 * px + py], sibling).wait_recv()
            fw = [rcopy(n_ici + 2 * j, their_in, their_in, sibling), rcopy(n_ici + 2 * j + 1, their_out, their_out, sibling)]
            for cp in fw:
                cp.start()
            passed += fw
        for j, (px, py) in enumerate(chips):
            oin = gin.at[2 * px + py, pl.ds((1 - c) * hin, hin)]
            oout = gout.at[2 * px + py, pl.ds((1 - c) * hout, hout)]
            rcopy(n_ici + 2 * j, oin, oin, sibling).wait_recv()
            rcopy(n_ici + 2 * j + 1, oout, oout, sibling).wait_recv()
        for cp in sends + passed:
            cp.wait_send()

    return pl.pallas_call(
        body,
        name="allgather_weights",
        in_specs=[ANY, ANY, ANY],
        out_specs=[ANY, ANY, ANY],
        out_shape=[jax.ShapeDtypeStruct((N_CHIPS, D, S), BF16), jax.ShapeDtypeStruct((N_CHIPS, R, D), BF16),
                   jax.ShapeDtypeStruct((N_CHIPS,) + conv_w.shape, F32)],
        scratch_shapes=[pltpu.SemaphoreType.DMA((n_ici + n_fwd,)), pltpu.SemaphoreType.DMA((n_ici + n_fwd,))],
    )(w_in_bf, w_out_bf, conv_w)


def _allreduce_small(packed):
    R = packed.shape[0]
    n_dev = 2 * N_CHIPS

    def body(p_ref, o_ref, buf, send_sems, recv_sems):
        x, y, c, _ = _place()
        me = 4 * x + 2 * y + c
        buf[me] = p_ref[...]
        copies = []
        for k in range(1, n_dev):
            px = 1 - x if k & 4 else x
            py = 1 - y if k & 2 else y
            pc = 1 - c if k & 1 else c
            copies.append((pltpu.make_async_remote_copy(
                src_ref=buf.at[me], dst_ref=buf.at[me], send_sem=send_sems.at[k - 1], recv_sem=recv_sems.at[k - 1],
                device_id=(px, py, pc), device_id_type=MESH), 4 * px + 2 * py + pc, (px, py, pc)))
        for cp, _, _ in copies:
            cp.start()
        for k, (_, peer, to) in enumerate(copies):
            pltpu.make_async_remote_copy(
                src_ref=buf.at[peer], dst_ref=buf.at[peer], send_sem=send_sems.at[k], recv_sem=recv_sems.at[k],
                device_id=to, device_id_type=MESH).wait_recv()
        for cp, _, _ in copies:
            cp.wait_send()
        acc = buf[0]
        for d in range(1, n_dev):
            acc = acc + buf[d]
        o_ref[...] = acc

    vm = pl.BlockSpec(memory_space=pltpu.VMEM)
    return pl.pallas_call(
        body,
        name="allreduce_small",
        in_specs=[vm],
        out_specs=vm,
        out_shape=jax.ShapeDtypeStruct((R, LANES), F32),
        scratch_shapes=[pltpu.VMEM((n_dev, R, LANES), F32), pltpu.SemaphoreType.DMA((n_dev - 1,)),
                        pltpu.SemaphoreType.DMA((n_dev - 1,))],
    )(packed)


def _swap_core_halves(a_in, a_out):
    def body(ain, aout, rin, rout, send_sems, recv_sems):
        x, y, c, _ = _place()
        cps = [pltpu.make_async_remote_copy(src_ref=s.at[1 - c], dst_ref=d, send_sem=send_sems.at[k],
                                            recv_sem=recv_sems.at[k], device_id=(x, y, 1 - c), device_id_type=MESH)
               for k, (s, d) in enumerate(((ain, rin), (aout, rout)))]
        for cp in cps:
            cp.start()
        for cp in cps:
            cp.wait()

    return pl.pallas_call(
        body,
        name="reduce_core_swap",
        in_specs=[ANY, ANY],
        out_specs=[ANY, ANY],
        out_shape=[jax.ShapeDtypeStruct(a_in.shape[1:], F32), jax.ShapeDtypeStruct(a_out.shape[1:], F32)],
        scratch_shapes=[pltpu.SemaphoreType.DMA((2,)), pltpu.SemaphoreType.DMA((2,))],
    )(a_in, a_out)


def _join_core_halves(g_in, g_out):
    def body(gin, gout, fin, fout, send_sems, recv_sems):
        x, y, c, _ = _place()
        cps = [pltpu.make_async_remote_copy(src_ref=s, dst_ref=d.at[c], send_sem=send_sems.at[k],
                                            recv_sem=recv_sems.at[k], device_id=(x, y, 1 - c), device_id_type=MESH)
               for k, (s, d) in enumerate(((gin, fin), (gout, fout)))]
        for cp in cps:
            cp.start()
        for k, (s, d) in enumerate(((gin, fin), (gout, fout))):
            pltpu.make_async_remote_copy(src_ref=s, dst_ref=d.at[1 - c], send_sem=send_sems.at[k],
                                         recv_sem=recv_sems.at[k], device_id=(x, y, 1 - c),
                                         device_id_type=MESH).wait_recv()
        for cp in cps:
            cp.wait_send()

    return pl.pallas_call(
        body,
        name="reduce_core_join",
        in_specs=[ANY, ANY],
        out_specs=[ANY, ANY],
        out_shape=[jax.ShapeDtypeStruct((2,) + g_in.shape, F32), jax.ShapeDtypeStruct((2,) + g_out.shape, F32)],
        scratch_shapes=[pltpu.SemaphoreType.DMA((2,)), pltpu.SemaphoreType.DMA((2,))],
    )(g_in, g_out)


def _pack(arrays):
    rows = []
    for a in arrays:
        flat = a.reshape(-1).astype(F32)
        n = -(-flat.shape[0] // LANES) * LANES
        rows.append(jnp.pad(flat, (0, n - flat.shape[0])).reshape(-1, LANES))
    out = jnp.concatenate(rows, axis=0)
    return jnp.pad(out, ((0, -out.shape[0] % 8), (0, 0)))


def _unpack(packed, shapes):
    out, r = [], 0
    for shp in shapes:
        n = math.prod(shp)
        nr = -(-n // LANES)
        out.append(packed[r:r + nr].reshape(-1)[:n].reshape(shp))
        r += nr
    return out


def _pad_lanes(a):
    return jnp.pad(a, ((0, 0), (0, LANES - a.shape[1])))


def kernel(x, norm_w, w_in, q_norm_w, k_norm_w, conv_w, conv_b, dt_bias, A_log, D_skip, sb_norm_w, ssd_norm_w, w_out, loss_target, m_norm_w, m_w_in, m_q_norm_w, m_k_norm_w, m_conv_w, m_conv_b, m_dt_bias, m_A_log, m_D_skip, m_sb_norm_w, m_ssd_norm_w, m_w_out, v_norm_w, v_w_in, v_q_norm_w, v_k_norm_w, v_conv_w, v_conv_b, v_dt_bias, v_A_log, v_D_skip, v_sb_norm_w, v_ssd_norm_w, v_w_out):
    Bl, L, D = x.shape
    T = Bl * L
    S = w_in.shape[2]
    R = w_out.shape[1]
    CW = conv_w.shape[2]
    n_in = N_CHIPS * S
    CD = D + 2 * SSD_GROUPS * SSD_STATE
    H = D // HEAD_DIM
    n_main = 6 * D + 512
    P = -(-(n_main + LANES) // 1024) * 1024
    assert n_in == n_main + H and CD == N_CHIPS * CW and 2 * D == N_CHIPS * R and CD == D + 512
    chip = (2 * lax.axis_index("x") + lax.axis_index("y")).astype(jnp.int32)
    core = lax.axis_index("c").astype(jnp.int32)

    w_in_bf, w_out_shard_bf = w_in[0].astype(BF16), w_out[0].astype(BF16)
    g_in, g_out, g_cw = _allgather_weights(w_in_bf, w_out_shard_bf, conv_w[0])
    g_in = lax.dynamic_update_slice(g_in, w_in_bf[None], (chip, 0, 0))
    g_out = lax.dynamic_update_slice(g_out, w_out_shard_bf[None], (chip, 0, 0))
    g_cw = lax.dynamic_update_slice(g_cw, conv_w, (chip, 0, 0))
    w_pad = jnp.pad(g_in.transpose(1, 0, 2).reshape(D, n_in), ((0, 0), (0, P - n_in)))
    w_out_bf = g_out.reshape(2 * D, D)
    conv_full = g_cw.transpose(1, 0, 2).reshape(CONV_K, CD)
    cwx, cwb = conv_full[:, :D], conv_full[:, D:]
    cbx, cbb = conv_b[:, :D], conv_b[:, D:]
    dtb, alog, dsk = _pad_lanes(dt_bias), _pad_lanes(A_log), _pad_lanes(D_skip)
    qw2, kw2 = jnp.tile(q_norm_w, (1, 2)), jnp.tile(k_norm_w, (1, 2))

    x2 = x.reshape(T, D)
    proj, hn_t, w_pad_t = _inproj(x2, norm_w, w_pad)
    proj3 = proj.reshape(Bl, L, P)
    o_sb = _attn_fwd(proj3, qw2, kw2, D)
    y_ssd, s_in = _ssd_fwd(proj3, cwx, cwb, cbx, cbb, dtb, alog, dsk, D)
    dout, dout_bf, mixed_t, do_sb, dy_ssd, dz_bf, dnw_out, loss_blk = _gate_out(
        x2, loss_target.reshape(T, D), o_sb.reshape(T, D), proj, y_ssd.reshape(T, D), sb_norm_w, ssd_norm_w, w_out_bf,
        w_out_bf.T)

    dq, dk, dv, dqkw = _attn_bwd(proj3, o_sb, do_sb.reshape(Bl, L, D), qw2, kw2, D)
    dtail, dcwx, dcwb, dcbx, dcbb, misc = _ssd_bwd(
        proj3, s_in, dy_ssd.reshape(Bl, L, D), cwx, cwb, cbx, cbb, dtb, alog, dsk, D, P - 5 * D)
    dproj = [dq.reshape(T, D), dk.reshape(T, D), dv.reshape(T, D), dz_bf, dtail.reshape(T, P - 5 * D)]
    gw_in = _grad_w_in(hn_t, dproj)[:, :n_in]
    gw_out = _matmul(mixed_t, dout_bf, "grad_w_out")

    a_in = gw_in.reshape(2, D // 2, N_CHIPS, S).transpose(0, 2, 1, 3)
    a_out = gw_out.reshape(N_CHIPS, 2, R // 2, D).transpose(1, 0, 2, 3)
    r_in, r_out = _swap_core_halves(a_in, a_out)
    core1, chip1 = core.reshape(1), chip.reshape(1)
    h_in = _add_core_half(a_in, r_in, core1, "sum_cores_w_in")
    h_out = _add_core_half(a_out, r_out, core1, "sum_cores_w_out")
    grad_x2, dnw_in, s_in_, s_out_ = _dhn(dproj, w_pad_t, x2, dout, norm_w, h_in, h_out)
    gh_in = _add_chips(h_in, s_in_, chip1, "sum_chips_w_in")
    gh_out = _add_chips(h_out, s_out_, chip1, "sum_chips_w_out")
    f_in, f_out = _join_core_halves(gh_in, gh_out)
    g_w_in = lax.dynamic_update_slice(f_in, gh_in[None], (core, 0, 0)).reshape(D, S)
    g_w_out = lax.dynamic_update_slice(f_out, gh_out[None], (core, 0, 0)).reshape(R, D)

    small_shapes = [(1, D), (1, D), (1, D), (1, CD), (1, HEAD_DIM), (1, HEAD_DIM), (1, H), (1, H), (1, H)]
    g_small_local = [dnw_in[0:1], dnw_out[0:1], dnw_out[1:2], jnp.concatenate([dcbx, dcbb], axis=1),
                     dqkw[0:1, :HEAD_DIM] + dqkw[0:1, HEAD_DIM:], dqkw[1:2, :HEAD_DIM] + dqkw[1:2, HEAD_DIM:],
                     misc[0:1, :H], misc[1:2, :H], misc[2:3, :H]]
    packed = _pack(g_small_local + [jnp.concatenate([dcwx, dcwb], axis=1), loss_blk[0:1, 0:1]])
    red = _allreduce_small(packed)
    g_small = _unpack(red, small_shapes + [(CONV_K, CD), (1, 1)])
    g_conv_w = lax.dynamic_slice_in_dim(g_small[9], chip * CW, CW, axis=1)
    loss = g_small[10][0, 0]

    d_in, nm_in, nv_in = _adamw(w_in[0], g_w_in, m_w_in[0], v_w_in[0], "adamw_w_in")
    d_out, nm_out, nv_out = _adamw(w_out[0], g_w_out, m_w_out[0], v_w_out[0], "adamw_w_out")
    d_cw, nm_cw, nv_cw = _adamw(conv_w[0], g_conv_w, m_conv_w[0], v_conv_w[0], "adamw_conv_w")
    small_w = [norm_w, sb_norm_w, ssd_norm_w, conv_b, q_norm_w, k_norm_w, dt_bias, A_log, D_skip]
    small_m = [m_norm_w, m_sb_norm_w, m_ssd_norm_w, m_conv_b, m_q_norm_w, m_k_norm_w, m_dt_bias, m_A_log, m_D_skip]
    small_v = [v_norm_w, v_sb_norm_w, v_ssd_norm_w, v_conv_b, v_q_norm_w, v_k_norm_w, v_dt_bias, v_A_log, v_D_skip]
    d_s, nm_s, nv_s = _adamw(_pack(small_w), _pack(g_small[:9]), _pack(small_m), _pack(small_v), "adamw_small")
    d_s, nm_s, nv_s = (_unpack(t, small_shapes) for t in (d_s, nm_s, nv_s))

    def ordered(s, w_in_, conv_w_, w_out_):
        return [s[0], w_in_[None], s[4], s[5], conv_w_[None], s[3], s[6], s[7], s[8], s[1], s[2], w_out_[None]]

    return (loss, grad_x2.reshape(Bl, L, D),
            *ordered(g_small[:9], g_w_in, g_conv_w, g_w_out),
            *ordered(d_s, d_in, d_cw, d_out),
            *ordered(nm_s, nm_in, nm_cw, nm_out),
            *ordered(nv_s, nv_in, nv_cw, nv_out))
```

```python
import functools
import math

import jax
import jax.numpy as jnp
from jax import lax
from jax.experimental import pallas as pl
from jax.experimental.pallas import tpu as pltpu

F32 = jnp.float32
BF16 = jnp.bfloat16
EPS = 1e-6
HEAD_DIM = 64
PAIR = 2 * HEAD_DIM
LANES = 128
SSD_STATE = 128
SSD_GROUPS = 2
BLK = 128
PREP_BLOCKS = 4
Q_TOGETHER = 2
FIRST_LEFT = 2
UNDERFLOW = -105.0
CONV_K = 4
HALO = 8
N_CHIPS = 4
ADAM_LR, ADAM_B1, ADAM_B2, ADAM_EPS, ADAM_WD, ADAM_STEP = 0.001, 0.9, 0.999, 1e-08, 0.01, 10
VMEM_LIMIT_V7X = 56 * 1024 * 1024
MESH = pl.DeviceIdType.MESH
ANY = pl.BlockSpec(memory_space=pl.ANY)
NT = (((1,), (1,)), ((), ()))


def _params(sem=None):
    kw = dict(vmem_limit_bytes=VMEM_LIMIT_V7X)
    if sem is not None:
        kw["dimension_semantics"] = sem
    return pltpu.CompilerParams(**kw)


def _dot(a, b):
    return jnp.dot(a, b, preferred_element_type=F32)


def _dot_nt(a, b):
    return lax.dot_general(a, b, NT, preferred_element_type=F32)


def _dot_split(m, x):
    hi = x.astype(BF16)
    lo = (x - hi.astype(F32)).astype(BF16)
    return _dot(m, hi) + _dot(m, lo)


def _iota(shape, dim):
    return lax.broadcasted_iota(jnp.int32, shape, dim)


def _rowsum(x):
    return jnp.sum(x, axis=1, keepdims=True)


def _colsum(x):
    return jnp.sum(x, axis=0, keepdims=True)


def _sigmoid(x):
    return 1.0 / (1.0 + jnp.exp(-x))


def _softplus(x):
    return jnp.maximum(x, 0.0) + jnp.log(1.0 + jnp.exp(-jnp.abs(x)))


def _inproj(x2, norm_w, w_pad):
    T, D = x2.shape
    P = w_pad.shape[1]
    tm = min(1024, T)
    tn = 1024 if P % 1024 == 0 else 512
    nj = P // tn

    def body(x_ref, nw_ref, w_ref, proj_ref, hnt_ref, wt_ref, hn_s):
        @pl.when(pl.program_id(1) == 0)
        def _():
            xv = x_ref[...]
            r = lax.rsqrt(jnp.mean(xv * xv, axis=-1, keepdims=True) + EPS)
            hn = xv * r * nw_ref[...]
            hn_s[...] = hn.astype(BF16)
            hnt_ref[...] = hn.T.astype(BF16)

        @pl.when(pl.program_id(0) == 0)
        def _():
            wt_ref[...] = w_ref[...].astype(F32).T.astype(BF16)

        proj_ref[...] = _dot(hn_s[...], w_ref[...])

    return pl.pallas_call(
        body,
        name="inproj",
        grid=(T // tm, P // tn),
        in_specs=[
            pl.BlockSpec((tm, D), lambda i, j: (i, 0)),
            pl.BlockSpec((1, D), lambda i, j: (0, 0)),
            pl.BlockSpec((D, tn), lambda i, j: (0, j)),
        ],
        out_specs=[
            pl.BlockSpec((tm, tn), lambda i, j: (i, j)),
            pl.BlockSpec((D, tm), lambda i, j: (0, i)),
            pl.BlockSpec((tn, D), lambda i, j: (jnp.where(i == 0, j, nj - 1), 0)),
        ],
        out_shape=[jax.ShapeDtypeStruct((T, P), F32), jax.ShapeDtypeStruct((D, T), BF16),
                   jax.ShapeDtypeStruct((P, D), BF16)],
        scratch_shapes=[pltpu.VMEM((tm, D), BF16)],
        compiler_params=_params(("arbitrary", "arbitrary")),
    )(x2, norm_w, w_pad)


def _pair_ones():
    ri = ((_iota((2 * PAIR, PAIR), 0) % PAIR) >= HEAD_DIM).astype(jnp.int32)
    ci = (_iota((2 * PAIR, PAIR), 1) >= HEAD_DIM).astype(jnp.int32)
    return jnp.where(ri == ci, 1.0, 0.0).astype(BF16)


def _pair_rms(v, ones2):
    return lax.rsqrt(_split_dots([v * v], ones2)[0] * (1.0 / HEAD_DIM) + EPS)


def _pair_mean(v, ones2):
    return _split_dots([v], ones2)[0] * (1.0 / HEAD_DIM)


def _suffix_ones():
    ri = _iota((2 * BLK, 2 * BLK), 0) % BLK
    ci = _iota((2 * BLK, 2 * BLK), 1)
    return jnp.where((ci >= BLK) | (ri > ci), 1.0, 0.0).astype(BF16)


def _split_dots(xs, m2):
    his = [x.astype(BF16) for x in xs]
    los = [(x - hi.astype(F32)).astype(BF16) for x, hi in zip(xs, his)]
    return [_dot(jnp.concatenate([hi, lo], axis=1), m2) for hi, lo in zip(his, los)]


def _sb_tiles(streams, km_s, uo):
    tiles = [(s, u, h) for s, st in enumerate(streams) for u in range(len(st["kbs"])) for h in range(2)]
    z2s = {(s, u): _dot_nt(st["q"], km_s[kb]) for s, st in enumerate(streams) for u, kb in enumerate(st["kbs"])}
    zs = [z2s[s, u][:, h * BLK:(h + 1) * BLK] for s, u, h in tiles]
    es = [jnp.exp(-jnp.abs(z)) for z in zs]
    las = [jnp.minimum(z, 0.0) - jnp.log(1.0 + e) for z, e in zip(zs, es)]
    lns = [a - z for a, z in zip(las, zs)]
    masks = [streams[s]["masks"][u] for s, u, h in tiles]
    lks = [lk if m is None else jnp.where(m, lk, 0.0) for m, lk in zip(masks, lns)]
    css = _split_dots(lks, uo)
    rests = [list(st["rest"]) for st in streams]
    ws = []
    for (s, u, h), m, a, cs in zip(tiles, masks, las, css):
        w = jnp.exp(a + rests[s][h] + cs[:, :BLK])
        ws.append(w if m is None else jnp.where(m, w, 0.0))
        rests[s][h] = rests[s][h] + cs[:, BLK:]
    return tiles, las, lns, ws, rests


def _stream(q_pair, qi, n_left, diag, zero):
    return dict(q=q_pair, kbs=[qi - u for u in range(n_left + 1)], masks=[diag] + [None] * n_left, rest=[zero, zero])


def _row0(block):
    return block * BLK if isinstance(block, int) else pl.multiple_of(block * BLK, BLK)


def _pair_of(vals, tiles, s, u):
    return [v for v, t in zip(vals, tiles) if t[0] == s and t[1] == u]


def _block_groups(nq):
    assert nq % Q_TOGETHER == 0 and Q_TOGETHER >= FIRST_LEFT
    return list(range(Q_TOGETHER)), nq // Q_TOGETHER


def _attn_prep(src_ref, w_ref, dst_s, n_blocks, scale):
    per = math.gcd(PREP_BLOCKS, n_blocks)
    rows = per * BLK
    lo = _iota((rows, PAIR), 1) < HEAD_DIM
    ones2 = _pair_ones()

    def step(i, carry):
        r0 = pl.multiple_of(i * rows, rows)
        v = src_ref[0, pl.ds(r0, rows), :]
        if w_ref is not None:
            v = v * _pair_rms(v, ones2) * w_ref[...]
        if scale != 1.0:
            v = v * scale
        v0, v1 = jnp.where(lo, v, 0.0).astype(BF16), jnp.where(lo, 0.0, v).astype(BF16)
        for b in range(per):
            dst_s[i * per + b, 0:BLK, :] = v0[b * BLK:(b + 1) * BLK]
            dst_s[i * per + b, BLK:2 * BLK, :] = v1[b * BLK:(b + 1) * BLK]
        return carry

    lax.fori_loop(0, n_blocks // per, step, 0)


def _attn_fwd(proj3, qw2, kw2, D):
    Bl, L, _ = proj3.shape
    n_pair = D // PAIR
    nq = L // BLK
    scale = 1.0 / math.sqrt(HEAD_DIM)

    def body(q_ref, k_ref, v_ref, qw_ref, kw_ref, o_ref, qm_s, km_s, vm_s):
        uo = _suffix_ones()
        diag = _iota((BLK, BLK), 1) < _iota((BLK, BLK), 0)
        _attn_prep(q_ref, qw_ref, qm_s, nq, scale)
        _attn_prep(k_ref, kw_ref, km_s, nq, 1.0)
        _attn_prep(v_ref, None, vm_s, nq, 1.0)

        zero_c = jnp.zeros((BLK, BLK), F32)

        def q_of(qi):
            return qm_s[qi, 0:BLK, :] + qm_s[qi, BLK:2 * BLK, :]

        def values(streams, accs):
            tiles, _, _, ws, rests = _sb_tiles(streams, km_s, uo)
            wbs = [w.astype(BF16) for w in ws]
            accs = list(accs)
            for s, st in enumerate(streams):
                for u, kb in enumerate(st["kbs"]):
                    accs[s] = accs[s] + _dot(jnp.concatenate(_pair_of(wbs, tiles, s, u), axis=1), vm_s[kb])
            return accs, rests

        def group(qis, n_lefts):
            streams = [_stream(q_of(qi), qi, n, diag, zero_c) for qi, n in zip(qis, n_lefts)]
            accs, rests = values(streams, [jnp.zeros((BLK, PAIR), F32)] * len(qis))
            for qi, n, q, acc, rc in zip(qis, n_lefts, [st["q"] for st in streams], accs, rests):

                def sweep(state, n_blocks, q=q):
                    kb, rc0, rc1, acc1, _ = state
                    st = dict(q=q, kbs=[kb - u for u in range(n_blocks)], masks=[None] * n_blocks, rest=[rc0, rc1])
                    (acc1,), (r,) = values([st], [acc1])
                    return kb - n_blocks, r[0], r[1], acc1, jnp.maximum(jnp.max(r[0]), jnp.max(r[1]))

                state = (jnp.asarray(qi - n - 1, jnp.int32), rc[0], rc[1], acc, jnp.maximum(jnp.max(rc[0]), jnp.max(rc[1])))
                state = lax.while_loop(lambda t: (t[0] >= 1) & (t[4] >= UNDERFLOW), lambda t: sweep(t, 2), state)
                state = lax.while_loop(lambda t: (t[0] >= 0) & (t[4] >= UNDERFLOW), lambda t: sweep(t, 1), state)
                o_ref[0, pl.ds(_row0(qi), BLK), :] = state[3]

        head, n_groups = _block_groups(nq)
        group(head, [min(qi, FIRST_LEFT) for qi in head])

        def groups(g, carry):
            group([g * Q_TOGETHER + j for j in range(Q_TOGETHER)], [FIRST_LEFT] * Q_TOGETHER)
            return carry

        lax.fori_loop(1, n_groups, groups, 0)

    blk = lambda off: pl.BlockSpec((1, L, PAIR), lambda b, p: (b, 0, off + p))
    wspec = pl.BlockSpec((1, PAIR), lambda b, p: (0, 0))
    return pl.pallas_call(
        body,
        name="sb_attn_fwd",
        grid=(Bl, n_pair),
        in_specs=[blk(0), blk(n_pair), blk(2 * n_pair), wspec, wspec],
        out_specs=pl.BlockSpec((1, L, PAIR), lambda b, p: (b, 0, p)),
        out_shape=jax.ShapeDtypeStruct((Bl, L, D), F32),
        scratch_shapes=[pltpu.VMEM((nq, 2 * BLK, PAIR), BF16)] * 3,
        compiler_params=_params(("parallel", "parallel")),
    )(proj3, proj3, proj3, qw2, kw2)


def _attn_bwd(proj3, o3, do3, qw2, kw2, D):
    Bl, L, _ = proj3.shape
    n_pair = D // PAIR
    nq = L // BLK
    scale = 1.0 / math.sqrt(HEAD_DIM)

    def body(q_ref, k_ref, v_ref, o_ref, do_ref, qw_ref, kw_ref, dq_ref, dk_ref, dv_ref, dw_ref,
             qm_s, km_s, vm_s, dom_s, dq_s, dk_s, dv_s):
        uo = _suffix_ones()
        diag = _iota((BLK, BLK), 1) < _iota((BLK, BLK), 0)
        ones2 = _pair_ones()
        _attn_prep(q_ref, qw_ref, qm_s, nq, scale)
        _attn_prep(k_ref, kw_ref, km_s, nq, 1.0)
        _attn_prep(v_ref, None, vm_s, nq, 1.0)
        _attn_prep(do_ref, None, dom_s, nq, 1.0)

        @pl.when((pl.program_id(0) == 0) & (pl.program_id(1) == 0))
        def _():
            dw_ref[...] = jnp.zeros_like(dw_ref)

        def zero(i, carry):
            r0 = pl.multiple_of(i * BLK, BLK)
            dk_s[pl.ds(r0, BLK), :] = jnp.zeros((BLK, PAIR), F32)
            dv_s[pl.ds(r0, BLK), :] = jnp.zeros((BLK, PAIR), F32)
            return carry

        lax.fori_loop(0, nq, zero, 0)

        zero_c = jnp.zeros((BLK, BLK), F32)

        def tiles_bwd(streams, dqas):
            dw2s = {(s, u): _dot_nt(st["do"], vm_s[kb]) for s, st in enumerate(streams) for u, kb in enumerate(st["kbs"])}
            tiles, las, lns, ws, rests = _sb_tiles(streams, km_s, uo)
            dws = [dw2s[s, u][:, h * BLK:(h + 1) * BLK] for s, u, h in tiles]
            wfs = [w.astype(BF16).astype(F32) for w in ws]
            gs = [wf * dw for wf, dw in zip(wfs, dws)]
            gss = _split_dots(gs, uo)
            gcs = [list(st["g_rest"]) for st in streams]
            dzs = []
            for (s, u, h), a, ln, g, gsum in zip(tiles, las, lns, gs, gss):
                g_before = streams[s]["delta"][h] - (gcs[s][h] + gsum[:, :BLK] + g)
                gcs[s][h] = gcs[s][h] + gsum[:, BLK:]
                dz = g * jnp.exp(ln) - g_before * jnp.exp(a)
                m = streams[s]["masks"][u]
                dzs.append(dz if m is None else jnp.where(m, dz, 0.0))
            wts = [wf.T.astype(BF16) for wf in wfs]
            dzts = [dz.T.astype(BF16) for dz in dzs]
            dzbs = [dz.astype(BF16) for dz in dzs]
            dqas = list(dqas)
            for s, st in enumerate(streams):
                for u, kb in enumerate(st["kbs"]):
                    c0 = _row0(kb)
                    dv_s[pl.ds(c0, BLK), :] += _dot(jnp.concatenate(_pair_of(wts, tiles, s, u), axis=1), dom_s[st["qi"]])
                    dk_s[pl.ds(c0, BLK), :] += _dot(jnp.concatenate(_pair_of(dzts, tiles, s, u), axis=1), qm_s[st["qi"]])
                    dqas[s] = dqas[s] + _dot(jnp.concatenate(_pair_of(dzbs, tiles, s, u), axis=1), km_s[kb])
            return dqas, rests, gcs

        def group(qis, n_lefts):
            streams = []
            for qi, n in zip(qis, n_lefts):
                o_blk = o_ref[0, pl.ds(_row0(qi), BLK), :]
                doms = [dom_s[qi, 0:BLK, :], dom_s[qi, BLK:2 * BLK, :]]
                st = _stream(qm_s[qi, 0:BLK, :] + qm_s[qi, BLK:2 * BLK, :], qi, n, diag, zero_c)
                st.update(qi=qi, do=doms[0] + doms[1], delta=[_rowsum(d.astype(F32) * o_blk) for d in doms],
                          g_rest=[zero_c, zero_c])
                streams.append(st)
            dqas, rests, gcs = tiles_bwd(streams, [jnp.zeros((BLK, PAIR), F32)] * len(qis))
            for qi, n, st0, dqa, rc, gc in zip(qis, n_lefts, streams, dqas, rests, gcs):

                def sweep(state, n_blocks, st0=st0):
                    kb, rc0, rc1, gc0, gc1, dqa1, _ = state
                    st = dict(st0, kbs=[kb - u for u in range(n_blocks)], masks=[None] * n_blocks, rest=[rc0, rc1],
                              g_rest=[gc0, gc1])
                    (dqa1,), (r,), (g,) = tiles_bwd([st], [dqa1])
                    return kb - n_blocks, r[0], r[1], g[0], g[1], dqa1, jnp.maximum(jnp.max(r[0]), jnp.max(r[1]))

                state = (jnp.asarray(qi - n - 1, jnp.int32), rc[0], rc[1], gc[0], gc[1], dqa,
                         jnp.maximum(jnp.max(rc[0]), jnp.max(rc[1])))
                state = lax.while_loop(lambda t: (t[0] >= 1) & (t[6] >= UNDERFLOW), lambda t: sweep(t, 2), state)
                state = lax.while_loop(lambda t: (t[0] >= 0) & (t[6] >= UNDERFLOW), lambda t: sweep(t, 1), state)
                dq_s[pl.ds(_row0(qi), BLK), :] = state[5] * scale

        head, n_groups = _block_groups(nq)
        group(head, [min(qi, FIRST_LEFT) for qi in head])

        def groups(g, carry):
            group([g * Q_TOGETHER + j for j in range(Q_TOGETHER)], [FIRST_LEFT] * Q_TOGETHER)
            return carry

        lax.fori_loop(1, n_groups, groups, 0)

        per = math.gcd(PREP_BLOCKS, nq)
        rows = per * BLK

        def finish(i, carry):
            r0 = pl.multiple_of(i * rows, rows)
            dwq, dwk = carry
            out = []
            for src_ref, w_ref, d_s in ((q_ref, qw_ref, dq_s), (k_ref, kw_ref, dk_s)):
                v = src_ref[0, pl.ds(r0, rows), :]
                r = _pair_rms(v, ones2)
                vh = v * r
                dy = d_s[pl.ds(r0, rows), :]
                dvh = dy * w_ref[...]
                out.append((r * (dvh - vh * _pair_mean(dvh * vh, ones2)), _colsum(dy * vh)))
            dq_ref[0, pl.ds(r0, rows), :] = out[0][0].astype(BF16)
            dk_ref[0, pl.ds(r0, rows), :] = out[1][0].astype(BF16)
            dv_ref[0, pl.ds(r0, rows), :] = dv_s[pl.ds(r0, rows), :].astype(BF16)
            return dwq + out[0][1], dwk + out[1][1]

        zrow = jnp.zeros((1, PAIR), F32)
        dwq, dwk = lax.fori_loop(0, nq // per, finish, (zrow, zrow))
        dw_ref[0:1, :] += dwq
        dw_ref[1:2, :] += dwk

    blk = lambda off: pl.BlockSpec((1, L, PAIR), lambda b, p: (b, 0, off + p))
    wspec = pl.BlockSpec((1, PAIR), lambda b, p: (0, 0))
    oblk = pl.BlockSpec((1, L, PAIR), lambda b, p: (b, 0, p))
    return pl.pallas_call(
        body,
        name="sb_attn_bwd",
        grid=(Bl, n_pair),
        in_specs=[blk(0), blk(n_pair), blk(2 * n_pair), oblk, oblk, wspec, wspec],
        out_specs=[oblk, oblk, oblk, pl.BlockSpec((8, PAIR), lambda b, p: (0, 0))],
        out_shape=[jax.ShapeDtypeStruct((Bl, L, D), BF16)] * 3 + [jax.ShapeDtypeStruct((8, PAIR), F32)],
        scratch_shapes=[pltpu.VMEM((nq, 2 * BLK, PAIR), BF16)] * 4 + [pltpu.VMEM((L, PAIR), F32)] * 3,
        compiler_params=_params(("arbitrary", "arbitrary")),
    )(proj3, proj3, proj3, o3, do3, qw2, kw2)


def _conv_pre(ext_s, halo_ref, raw_ref, w_ref, b_ref, first):
    ext_s[0:HALO, :] = jnp.where(first, 0.0, halo_ref[0])
    ext_s[HALO:HALO + BLK, :] = raw_ref[0]
    pre = b_ref[...]
    for i in range(CONV_K):
        pre = pre + ext_s[pl.ds(HALO - (CONV_K - 1 - i), BLK), :] * w_ref[i:i + 1, :]
    return pre


def _lane_col(m, lane, h):
    return _rowsum(jnp.where(lane == h, m, 0.0))


def _half_sums(row, lo1):
    return _rowsum(jnp.where(lo1, row, 0.0)), _rowsum(jnp.where(lo1, 0.0, row))


def _ssd_specs(Bl, L, D, rev):
    nc = L // BLK
    rows_per = BLK // HALO
    cidx = (lambda c: nc - 1 - c) if rev else (lambda c: c)
    xoff = 5
    boff = (6 * D) // 512
    doff = (6 * D + 512) // LANES
    prev = lambda c: jnp.maximum(cidx(c) * rows_per - 1, 0)
    specs = [
        pl.BlockSpec((1, BLK, D), lambda b, c: (b, cidx(c), xoff)),
        pl.BlockSpec((1, BLK, 512), lambda b, c: (b, cidx(c), boff)),
        pl.BlockSpec((1, HALO, D), lambda b, c: (b, prev(c), xoff)),
        pl.BlockSpec((1, HALO, 512), lambda b, c: (b, prev(c), boff)),
        pl.BlockSpec((1, BLK, LANES), lambda b, c: (b, cidx(c), doff)),
    ]
    full = lambda shape: pl.BlockSpec(shape, lambda b, c: (0,) * len(shape))
    specs += [full((CONV_K, D)), full((CONV_K, 512)), full((1, D)), full((1, 512)),
              full((1, LANES)), full((1, LANES)), full((1, LANES))]
    return specs, cidx


def _ssd_common(dtr_ref, dtb_ref, alog_ref, acs_s, acsT_s):
    ltri = jnp.where(_iota((BLK, BLK), 1) <= _iota((BLK, BLK), 0), 1.0, 0.0).astype(BF16)
    dtv = _softplus(dtr_ref[0] + dtb_ref[...])
    a = -jnp.exp(alog_ref[...])
    acs = _dot_split(ltri, dtv * a)
    acs_s[...] = acs
    acsT_s[...] = acs.T
    return dtv, a, acs


def _pair_terms(pr, acs, dtv, acs_s, lane, lo, lane1, lo1):
    h0, h1 = 2 * pr, 2 * pr + 1
    c0, c1 = _lane_col(acs, lane, h0), _lane_col(acs, lane, h1)
    d0, d1 = _lane_col(dtv, lane, h0), _lane_col(dtv, lane, h1)
    lastv = acs_s[BLK - 1:BLK, :]
    l0, l1 = _lane_col(lastv, lane1, h0), _lane_col(lastv, lane1, h1)
    return dict(h=(h0, h1), c=(c0, c1), last=(l0, l1), acs_p=jnp.where(lo, c0, c1), dt_p=jnp.where(lo, d0, d1),
                last_p=jnp.where(lo1, l0, l1))


def _decay_tiles(cc, row, tri, want_t):
    lm = jnp.where(tri, jnp.exp(cc - row), 0.0)
    if not want_t:
        return lm, None
    tri_t = _iota((BLK, BLK), 1) >= _iota((BLK, BLK), 0)
    return lm, jnp.where(tri_t, jnp.exp(row - cc), 0.0)


def _ssd_fwd(proj3, cwx, cwb, cbx, cbb, dtb, alog, dsk, D):
    Bl, L, _ = proj3.shape
    nc = L // BLK
    n_pair = D // PAIR
    pairs_per_group = n_pair // SSD_GROUPS
    specs, _ = _ssd_specs(Bl, L, D, False)

    def body(xr_ref, bcr_ref, xh_ref, bch_ref, dtr_ref, cwx_ref, cwb_ref, cbx_ref, cbb_ref, dtb_ref, alog_ref,
             dsk_ref, y_ref, sin_ref, st_s, extx_s, extb_s, acs_s, acsT_s):
        first = pl.program_id(1) == 0

        @pl.when(first)
        def _():
            st_s[...] = jnp.zeros_like(st_s)

        lane, lane1 = _iota((BLK, LANES), 1), _iota((1, LANES), 1)
        lo, lo1 = lane < HEAD_DIM, lane1 < HEAD_DIM
        tri = _iota((BLK, BLK), 1) <= _iota((BLK, BLK), 0)
        pre = _conv_pre(extx_s, xh_ref, xr_ref, cwx_ref, cbx_ref, first)
        ux = pre * _sigmoid(pre)
        pre = _conv_pre(extb_s, bch_ref, bcr_ref, cwb_ref, cbb_ref, first)
        ub = pre * _sigmoid(pre)
        dtv, a, acs = _ssd_common(dtr_ref, dtb_ref, alog_ref, acs_s, acsT_s)
        for g in range(SSD_GROUPS):
            bg = ub[:, g * SSD_STATE:(g + 1) * SSD_STATE]
            cb_ = ub[:, (SSD_GROUPS + g) * SSD_STATE:(SSD_GROUPS + g + 1) * SSD_STATE].astype(BF16)
            cbm = _dot_nt(cb_, bg.astype(BF16))
            btb = bg.T.astype(BF16)
            for pr in range(g * pairs_per_group, (g + 1) * pairs_per_group):
                t = _pair_terms(pr, acs, dtv, acs_s, lane, lo, lane1, lo1)
                xs_p = ux[:, pr * PAIR:(pr + 1) * PAIR]
                x_p = xs_p * t["dt_p"]
                st = st_s[pr]
                sin_ref[0, 0, pr] = st
                y = _dot(cb_, st.astype(BF16)) * jnp.exp(t["acs_p"])
                for k in range(2):
                    row = acsT_s[t["h"][k]:t["h"][k] + 1, :]
                    lm, _ = _decay_tiles(t["c"][k], row, tri, False)
                    xm = jnp.where(lo if k == 0 else ~lo, x_p, 0.0).astype(BF16)
                    y = y + _dot((cbm * lm).astype(BF16), xm)
                d_p = jnp.where(lo1, _lane_col(dsk_ref[...], lane1, t["h"][0]), _lane_col(dsk_ref[...], lane1, t["h"][1]))
                y_ref[0, :, pr * PAIR:(pr + 1) * PAIR] = y + d_p * xs_p
                xd = (x_p * jnp.exp(t["last_p"] - t["acs_p"])).astype(BF16)
                st_s[pr] = st * jnp.exp(t["last_p"]) + _dot(btb, xd)

    return pl.pallas_call(
        body,
        name="ssd_fwd",
        grid=(Bl, nc),
        in_specs=specs,
        out_specs=[
            pl.BlockSpec((1, BLK, D), lambda b, c: (b, c, 0)),
            pl.BlockSpec((1, 1, n_pair, SSD_STATE, PAIR), lambda b, c: (b, c, 0, 0, 0)),
        ],
        out_shape=[jax.ShapeDtypeStruct((Bl, L, D), F32),
                   jax.ShapeDtypeStruct((Bl, nc, n_pair, SSD_STATE, PAIR), F32)],
        scratch_shapes=[pltpu.VMEM((n_pair, SSD_STATE, PAIR), F32), pltpu.VMEM((HALO + BLK, D), F32),
                        pltpu.VMEM((HALO + BLK, 512), F32), pltpu.VMEM((BLK, LANES), F32),
                        pltpu.VMEM((LANES, BLK), F32)],
        compiler_params=_params(("arbitrary", "arbitrary")),
    )(proj3, proj3, proj3, proj3, proj3, cwx, cwb, cbx, cbb, dtb, alog, dsk)


def _ssd_bwd(proj3, s_in, dy3, cwx, cwb, cbx, cbb, dtb, alog, dsk, D, tail):
    Bl, L, _ = proj3.shape
    CD = D + 512
    nc = L // BLK
    n_pair = D // PAIR
    n_heads = 2 * n_pair
    pairs_per_group = n_pair // SSD_GROUPS
    specs, cidx = _ssd_specs(Bl, L, D, True)
    specs = specs + [
        pl.BlockSpec((1, 1, n_pair, SSD_STATE, PAIR), lambda b, c: (b, cidx(c), 0, 0, 0)),
        pl.BlockSpec((1, BLK, D), lambda b, c: (b, cidx(c), 0)),
    ]

    def body(xr_ref, bcr_ref, xh_ref, bch_ref, dtr_ref, cwx_ref, cwb_ref, cbx_ref, cbb_ref, dtb_ref, alog_ref,
             dsk_ref, sin_ref, dy_ref, dxbc_ref, dcwx_ref, dcwb_ref, dcbx_ref, dcbb_ref, misc_ref,
             dst_s, extx_s, extb_s, acs_s, acsT_s, dux_s, dub_s, e2x_s, e2b_s, nxx_s, nxb_s):
        step = pl.program_id(1)
        first = step == nc - 1
        last = step == 0

        @pl.when(last)
        def _():
            dst_s[...] = jnp.zeros_like(dst_s)
            nxx_s[...] = jnp.zeros_like(nxx_s)
            nxb_s[...] = jnp.zeros_like(nxb_s)

        @pl.when(last & (pl.program_id(0) == 0))
        def _():
            for r in (dcwx_ref, dcwb_ref, dcbx_ref, dcbb_ref, misc_ref):
                r[...] = jnp.zeros_like(r)

        lane, lane1 = _iota((BLK, LANES), 1), _iota((1, LANES), 1)
        lo, lo1 = lane < HEAD_DIM, lane1 < HEAD_DIM
        tri = _iota((BLK, BLK), 1) <= _iota((BLK, BLK), 0)
        prex = _conv_pre(extx_s, xh_ref, xr_ref, cwx_ref, cbx_ref, first)
        sgx = _sigmoid(prex)
        ux = prex * sgx
        preb = _conv_pre(extb_s, bch_ref, bcr_ref, cwb_ref, cbb_ref, first)
        sgb = _sigmoid(preb)
        ub = preb * sgb
        dtv, a, acs = _ssd_common(dtr_ref, dtb_ref, alog_ref, acs_s, acsT_s)
        dacs = jnp.zeros((BLK, LANES), F32)
        dlast = jnp.zeros((1, LANES), F32)
        ddt = jnp.zeros((BLK, LANES), F32)
        dd = jnp.zeros((1, LANES), F32)
        for g in range(SSD_GROUPS):
            bg = ub[:, g * SSD_STATE:(g + 1) * SSD_STATE]
            cg = ub[:, (SSD_GROUPS + g) * SSD_STATE:(SSD_GROUPS + g + 1) * SSD_STATE]
            bb, cb_ = bg.astype(BF16), cg.astype(BF16)
            cbm = _dot_nt(cb_, bb)
            cbt = _dot_nt(bb, cb_)
            ctb = cg.T.astype(BF16)
            dbg = jnp.zeros((BLK, SSD_STATE), F32)
            dcg = jnp.zeros((BLK, SSD_STATE), F32)
            for pr in range(g * pairs_per_group, (g + 1) * pairs_per_group):
                t = _pair_terms(pr, acs, dtv, acs_s, lane, lo, lane1, lo1)
                h0, h1 = t["h"]
                xs_p = ux[:, pr * PAIR:(pr + 1) * PAIR]
                dy_p = dy_ref[0, :, pr * PAIR:(pr + 1) * PAIR]
                x_p = xs_p * t["dt_p"]
                ea_p = jnp.exp(t["acs_p"])
                dte_p = jnp.exp(t["last_p"] - t["acs_p"])
                cd_p = jnp.exp(t["last_p"])
                st = sin_ref[0, 0, pr]
                dst = dst_s[pr]
                stb, dstb = st.astype(BF16), dst.astype(BF16)
                s0, s1 = _half_sums(_colsum(dy_p * xs_p), lo1)
                dd = dd + jnp.where(lane1 == h0, s0, 0.0) + jnp.where(lane1 == h1, s1, 0.0)
                d_p = jnp.where(lo1, _lane_col(dsk_ref[...], lane1, h0), _lane_col(dsk_ref[...], lane1, h1))
                dxs_p = d_p * dy_p
                dp = dy_p * ea_p
                dpb = dp.astype(BF16)
                yo = dp * _dot(cb_, stb)
                dcg = dcg + _dot_nt(dpb, stb)
                dst_off = _dot(ctb, dpb)
                dac = [_rowsum(jnp.where(lo, yo, 0.0)), _rowsum(jnp.where(lo, 0.0, yo))]
                s0, s1 = _half_sums(_colsum(dst * st), lo1)
                dl = [s0 * jnp.exp(t["last"][0]), s1 * jnp.exp(t["last"][1])]
                dxd = _dot(bb, dstb)
                dx_p = dxd * dte_p
                tt = dxd * x_p
                dbg = dbg + _dot_nt((x_p * dte_p).astype(BF16), dstb)
                for k, ddte in enumerate((_rowsum(jnp.where(lo, tt, 0.0)), _rowsum(jnp.where(lo, 0.0, tt)))):
                    ek = ddte * jnp.exp(t["last"][k] - t["c"][k])
                    dl[k] = dl[k] + _colsum(ek)
                    dac[k] = dac[k] - ek
                for k in range(2):
                    row = acsT_s[t["h"][k]:t["h"][k] + 1, :]
                    lm, lmt = _decay_tiles(t["c"][k], row, tri, True)
                    msk = lo if k == 0 else ~lo
                    xm = jnp.where(msk, x_p, 0.0).astype(BF16)
                    dym = jnp.where(msk, dy_p, 0.0).astype(BF16)
                    dm = _dot_nt(dym, xm)
                    dmt = _dot_nt(xm, dym)
                    mt = cbt * lmt
                    dx_p = dx_p + _dot(mt.astype(BF16), dym)
                    dac[k] = dac[k] + _rowsum(dm * (cbm * lm)) - _rowsum(dmt * mt)
                    dcg = dcg + _dot((dm * lm).astype(BF16), bb)
                    dbg = dbg + _dot((dmt * lmt).astype(BF16), cb_)
                dacs = dacs + jnp.where(lane == h0, dac[0], 0.0) + jnp.where(lane == h1, dac[1], 0.0)
                dlast = dlast + jnp.where(lane1 == h0, dl[0], 0.0) + jnp.where(lane1 == h1, dl[1], 0.0)
                dxs_p = dxs_p + dx_p * t["dt_p"]
                t3 = dx_p * xs_p
                ddt = ddt + jnp.where(lane == h0, _rowsum(jnp.where(lo, t3, 0.0)), 0.0) \
                    + jnp.where(lane == h1, _rowsum(jnp.where(lo, 0.0, t3)), 0.0)
                dux_s[:, pr * PAIR:(pr + 1) * PAIR] = dxs_p
                dst_s[pr] = dst * cd_p + dst_off
            dub_s[:, g * SSD_STATE:(g + 1) * SSD_STATE] = dbg
            dub_s[:, (SSD_GROUPS + g) * SSD_STATE:(SSD_GROUPS + g + 1) * SSD_STATE] = dcg
        dacs = dacs + jnp.where(_iota((BLK, LANES), 0) == BLK - 1, dlast, 0.0)
        utri = jnp.where(_iota((BLK, BLK), 1) >= _iota((BLK, BLK), 0), 1.0, 0.0).astype(BF16)
        dda = _dot_split(utri, dacs)
        ddt = ddt + dda * a
        ddtr = jnp.where(lane < n_heads, ddt * _sigmoid(dtr_ref[0] + dtb_ref[...]), 0.0)
        dxbc_ref[0, :, CD:CD + LANES] = ddtr.astype(BF16)
        dxbc_ref[0, :, CD + LANES:tail] = jnp.zeros((BLK, tail - CD - LANES), BF16)
        misc_ref[0:1, :] += _colsum(ddtr)
        misc_ref[1:2, :] += jnp.where(lane1 < n_heads, _colsum(dda * dtv) * a, 0.0)
        misc_ref[2:3, :] += dd
        for (du_s, pre, sg, ext_s, e2_s, nx_s, w_ref, dcw_ref, dcb_ref, c0, width) in (
                (dux_s, prex, sgx, extx_s, e2x_s, nxx_s, cwx_ref, dcwx_ref, dcbx_ref, 0, D),
                (dub_s, preb, sgb, extb_s, e2b_s, nxb_s, cwb_ref, dcwb_ref, dcbb_ref, D, 512)):
            dpre = du_s[...] * (sg * (1.0 + pre * (1.0 - sg)))
            dcb_ref[...] += _colsum(dpre)
            for i in range(CONV_K):
                dcw_ref[i:i + 1, :] += _colsum(dpre * ext_s[pl.ds(HALO - (CONV_K - 1 - i), BLK), :])
            e2_s[0:BLK, :] = dpre
            e2_s[BLK:BLK + HALO, :] = nx_s[...]
            dxr = jnp.zeros((BLK, width), F32)
            for i in range(CONV_K):
                dxr = dxr + e2_s[pl.ds(CONV_K - 1 - i, BLK), :] * w_ref[i:i + 1, :]
            dxbc_ref[0, :, c0:c0 + width] = dxr.astype(BF16)
            nx_s[...] = e2_s[0:HALO, :]

    full = lambda shape: pl.BlockSpec(shape, lambda b, c: (0,) * len(shape))
    return pl.pallas_call(
        body,
        name="ssd_bwd",
        grid=(Bl, nc),
        in_specs=specs,
        out_specs=[
            pl.BlockSpec((1, BLK, tail), lambda b, c: (b, cidx(c), 0)),
            full((CONV_K, D)), full((CONV_K, 512)), full((1, D)), full((1, 512)), full((8, LANES)),
        ],
        out_shape=[
            jax.ShapeDtypeStruct((Bl, L, tail), BF16),
            jax.ShapeDtypeStruct((CONV_K, D), F32), jax.ShapeDtypeStruct((CONV_K, 512), F32),
            jax.ShapeDtypeStruct((1, D), F32), jax.ShapeDtypeStruct((1, 512), F32),
            jax.ShapeDtypeStruct((8, LANES), F32),
        ],
        scratch_shapes=[
            pltpu.VMEM((n_pair, SSD_STATE, PAIR), F32),
            pltpu.VMEM((HALO + BLK, D), F32), pltpu.VMEM((HALO + BLK, 512), F32),
            pltpu.VMEM((BLK, LANES), F32), pltpu.VMEM((LANES, BLK), F32),
            pltpu.VMEM((BLK, D), F32), pltpu.VMEM((BLK, 512), F32),
            pltpu.VMEM((BLK + HALO, D), F32), pltpu.VMEM((BLK + HALO, 512), F32),
            pltpu.VMEM((HALO, D), F32), pltpu.VMEM((HALO, 512), F32),
        ],
        compiler_params=_params(("arbitrary", "arbitrary")),
    )(proj3, proj3, proj3, proj3, proj3, cwx, cwb, cbx, cbb, dtb, alog, dsk, s_in, dy3)


def _gate_out(x2, tgt2, o2, proj2, y2, sbw, ssw, w_out_bf, w_out_t):
    T, D = x2.shape
    tm = min(256, T)

    def body(x_ref, t_ref, o_ref, zs_ref, y_ref, zy_ref, sbw_ref, ssw_ref, wo_ref, wot_ref,
             dout_ref, doutb_ref, mixt_ref, do_ref, dy_ref, dz_ref, dnw_ref, loss_ref):
        @pl.when(pl.program_id(0) == 0)
        def _():
            dnw_ref[...] = jnp.zeros_like(dnw_ref)
            loss_ref[...] = jnp.zeros_like(loss_ref)

        def fwd(o, z, w):
            sg = _sigmoid(z)
            sl = z * sg
            g = o * sl
            r = lax.rsqrt(jnp.mean(g * g, axis=-1, keepdims=True) + EPS)
            n = g * r
            return sg, sl, r, n, n * w

        def bwd(dy, o, z, w, sg, sl, r, n):
            dn = dy * w
            dg = r * (dn - n * jnp.mean(dn * n, axis=-1, keepdims=True))
            return dg * sl, dg * o * (sg * (1.0 + z * (1.0 - sg))), _colsum(dy * n)

        o1, z1, w1 = o_ref[...], zs_ref[...], sbw_ref[...]
        o2_, z2, w2 = y_ref[...], zy_ref[...], ssw_ref[...]
        sg1, sl1, r1, n1, y1 = fwd(o1, z1, w1)
        sg2, sl2, r2, n2, y2_ = fwd(o2_, z2, w2)
        y1b, y2b = y1.astype(BF16), y2_.astype(BF16)
        mixt_ref[0:D, :] = y1.T.astype(BF16)
        mixt_ref[D:2 * D, :] = y2_.T.astype(BF16)
        out = x_ref[...] + (_dot(y1b, wo_ref[0:D, :]) + _dot(y2b, wo_ref[D:2 * D, :]))
        err = out - t_ref[...]
        loss_ref[...] += jnp.sum(err * err) * (0.5 / D)
        dout = err * (1.0 / D)
        dout_ref[...] = dout
        doutb = dout.astype(BF16)
        doutb_ref[...] = doutb
        do1, dz1, dw1 = bwd(_dot(doutb, wot_ref[:, 0:D]), o1, z1, w1, sg1, sl1, r1, n1)
        do2, dz2, dw2 = bwd(_dot(doutb, wot_ref[:, D:2 * D]), o2_, z2, w2, sg2, sl2, r2, n2)
        do_ref[...] = do1
        dy_ref[...] = do2
        dz_ref[:, 0:D] = dz1.astype(BF16)
        dz_ref[:, D:2 * D] = dz2.astype(BF16)
        dnw_ref[0:1, :] += dw1
        dnw_ref[1:2, :] += dw2

    row = lambda col: pl.BlockSpec((tm, D), lambda i: (i, col))
    full = lambda shape: pl.BlockSpec(shape, lambda i: (0,) * len(shape))
    wide = pl.BlockSpec((tm, 2 * D), lambda i: (i, 0))
    return pl.pallas_call(
        body,
        name="gate_out",
        grid=(T // tm,),
        in_specs=[row(0), row(0), row(0), row(3), row(0), row(4), full((1, D)), full((1, D)), full((2 * D, D)),
                  full((D, 2 * D))],
        out_specs=[row(0), row(0), pl.BlockSpec((2 * D, tm), lambda i: (0, i)), row(0), row(0), wide,
                   full((8, D)), full((8, LANES))],
        out_shape=[
            jax.ShapeDtypeStruct((T, D), F32), jax.ShapeDtypeStruct((T, D), BF16),
            jax.ShapeDtypeStruct((2 * D, T), BF16), jax.ShapeDtypeStruct((T, D), F32),
            jax.ShapeDtypeStruct((T, D), F32), jax.ShapeDtypeStruct((T, 2 * D), BF16),
            jax.ShapeDtypeStruct((8, D), F32), jax.ShapeDtypeStruct((8, LANES), F32),
        ],
        compiler_params=_params(("arbitrary",)),
    )(x2, tgt2, o2, proj2, y2, proj2, sbw, ssw, w_out_bf, w_out_t)


def _piece_blocks(pieces, D):
    counts = [p.shape[1] // D for p in pieces]
    return [sum(counts[:i]) for i in range(len(counts))], counts


def _dhn(pieces, w_pad_t, x2, dout, norm_w, h_in, h_out, slab_off, slab_w):
    T, D = x2.shape
    tm = min(1024, T)
    starts, counts = _piece_blocks(pieces, D)
    nk = sum(counts)
    ni = T // tm
    n_sem = 2 * (N_CHIPS - 1)
    assert nk * D == w_pad_t.shape[0]

    def body(*refs):
        p_refs = refs[:len(pieces)]
        (w_ref, x_hbm, dout_hbm, nw_ref, hin, hout, gx_ref, dnw_ref, rin, rout,
         acc_s, x_s, dout_s, send_sems, recv_sems, row_sems, own_sems) = refs[len(pieces):]
        i, k = pl.program_id(0), pl.program_id(1)

        def rows():
            r0 = pl.multiple_of(i * tm, tm)
            return [pltpu.make_async_copy(src.at[pl.ds(r0, tm)], dst, row_sems.at[n])
                    for n, (src, dst) in enumerate(((x_hbm, x_s), (dout_hbm, dout_s)))]

        @pl.when(k == 0)
        def _():
            for cp in rows():
                cp.start()

        def scatter():
            x, y, c, chips = _place()

            def slab(p):
                return hin.at[:, pl.ds(pl.multiple_of(p * slab_off, LANES), slab_w)]

            cps = []
            for j, (px, py) in enumerate(chips):
                for m, (src, dst) in enumerate(((slab(2 * px + py), rin.at[j]), (hout.at[2 * px + py], rout.at[j]))):
                    cps.append(pltpu.make_async_remote_copy(
                        src_ref=src, dst_ref=dst, send_sem=send_sems.at[2 * j + m], recv_sem=recv_sems.at[2 * j + m],
                        device_id=(px, py, c), device_id_type=MESH))
            me = 2 * x + y
            own = [pltpu.make_async_copy(slab(me), rin.at[N_CHIPS - 1], own_sems.at[0]),
                   pltpu.make_async_copy(hout.at[me], rout.at[N_CHIPS - 1], own_sems.at[1])]
            return cps + own

        @pl.when((i == 0) & (k == 0))
        def _():
            for cp in scatter():
                cp.start()

        @pl.when((i == ni - 1) & (k == nk - 1))
        def _():
            for cp in scatter():
                cp.wait()

        @pl.when((i == 0) & (k == 0))
        def _():
            dnw_ref[...] = jnp.zeros_like(dnw_ref)

        @pl.when(k == 0)
        def _():
            acc_s[...] = jnp.zeros_like(acc_s)

        for p_ref, s, n in zip(p_refs, starts, counts):
            @pl.when((k >= s) & (k < s + n))
            def _(p_ref=p_ref):
                acc_s[...] += _dot(p_ref[...], w_ref[...])

        @pl.when(k == nk - 1)
        def _():
            for cp in rows():
                cp.wait()
            xv = x_s[...]
            r = lax.rsqrt(jnp.mean(xv * xv, axis=-1, keepdims=True) + EPS)
            xh = xv * r
            dhn = acc_s[...]
            dxh = dhn * nw_ref[...]
            gx_ref[...] = dout_s[...] + r * (dxh - xh * jnp.mean(dxh * xh, axis=-1, keepdims=True))
            dnw_ref[0:1, :] += _colsum(dhn * xh)

    return pl.pallas_call(
        body,
        name="dhn",
        grid=(T // tm, nk),
        in_specs=[pl.BlockSpec((tm, D), lambda i, k, s=s, n=n: (i, jnp.clip(k - s, 0, n - 1)))
                  for s, n in zip(starts, counts)] + [
            pl.BlockSpec((D, D), lambda i, k: (k, 0)),
            ANY, ANY,
            pl.BlockSpec((1, D), lambda i, k: (0, 0)),
            ANY, ANY,
        ],
        out_specs=[pl.BlockSpec((tm, D), lambda i, k: (i, 0)), pl.BlockSpec((8, D), lambda i, k: (0, 0)), ANY, ANY],
        out_shape=[jax.ShapeDtypeStruct((T, D), F32), jax.ShapeDtypeStruct((8, D), F32),
                   jax.ShapeDtypeStruct((N_CHIPS, h_in.shape[0], slab_w), F32),
                   jax.ShapeDtypeStruct((N_CHIPS,) + h_out.shape[1:], F32)],
        scratch_shapes=[pltpu.VMEM((tm, D), F32)] * 3 + [pltpu.SemaphoreType.DMA((n_sem,)), pltpu.SemaphoreType.DMA((n_sem,)),
                                                      pltpu.SemaphoreType.DMA((2,)), pltpu.SemaphoreType.DMA((2,))],
        compiler_params=_params(("arbitrary", "arbitrary")),
    )(*pieces, w_pad_t, x2, dout, norm_w, h_in, h_out)


def _grad_w_in(hn_t, pieces):
    D, T = hn_t.shape
    tk = min(1024, T)
    starts, counts = _piece_blocks(pieces, D)

    def body(*refs):
        a_ref, p_refs, o_ref = refs[0], refs[1:-1], refs[-1]
        j = pl.program_id(0)

        @pl.when(pl.program_id(1) == 0)
        def _():
            o_ref[...] = jnp.zeros_like(o_ref)

        for p_ref, s, n in zip(p_refs, starts, counts):
            @pl.when((j >= s) & (j < s + n))
            def _(p_ref=p_ref):
                o_ref[...] += _dot(a_ref[...], p_ref[...])

    def piece_spec(s, n):
        return pl.BlockSpec((tk, D), lambda j, k: (jnp.where((j >= s) & (j < s + n), k, 0), jnp.clip(j - s, 0, n - 1)))

    return pl.pallas_call(
        body,
        name="grad_w_in",
        grid=(sum(counts), T // tk),
        in_specs=[pl.BlockSpec((D, tk), lambda j, k: (0, k))] + [piece_spec(s, n) for s, n in zip(starts, counts)],
        out_specs=pl.BlockSpec((D, D), lambda j, k: (0, j)),
        out_shape=jax.ShapeDtypeStruct((D, sum(counts) * D), F32),
        compiler_params=_params(("parallel", "arbitrary")),
    )(hn_t, *pieces)


def _matmul(a, b, name):
    M, K = a.shape
    N = b.shape[1]
    tm = min(1024, M)
    tn = 1024 if N % 1024 == 0 else (512 if N % 512 == 0 else N)
    tk = min(512, K)

    def body(a_ref, b_ref, o_ref):
        @pl.when(pl.program_id(2) == 0)
        def _():
            o_ref[...] = jnp.zeros_like(o_ref)

        o_ref[...] += _dot(a_ref[...], b_ref[...])

    return pl.pallas_call(
        body,
        name=name,
        grid=(M // tm, N // tn, K // tk),
        in_specs=[pl.BlockSpec((tm, tk), lambda i, j, k: (i, k)), pl.BlockSpec((tk, tn), lambda i, j, k: (k, j))],
        out_specs=pl.BlockSpec((tm, tn), lambda i, j, k: (i, j)),
        out_shape=jax.ShapeDtypeStruct((M, N), F32),
        compiler_params=_params(("parallel", "parallel", "arbitrary")),
    )(a, b)


def _adamw(w, g, m, v, name):
    R, C = w.shape
    tr = 256 if R % 256 == 0 else R
    c1 = 1.0 - ADAM_B1 ** ADAM_STEP
    c2 = 1.0 - ADAM_B2 ** ADAM_STEP

    def body(w_ref, g_ref, m_ref, v_ref, d_ref, nm_ref, nv_ref):
        gv = g_ref[...]
        m_new = ADAM_B1 * m_ref[...] + (1.0 - ADAM_B1) * gv
        v_new = ADAM_B2 * v_ref[...] + (1.0 - ADAM_B2) * (gv * gv)
        d_ref[...] = -ADAM_LR * ((m_new / c1) / (jnp.sqrt(v_new / c2) + ADAM_EPS) + ADAM_WD * w_ref[...])
        nm_ref[...] = m_new
        nv_ref[...] = v_new

    spec = pl.BlockSpec((tr, C), lambda i: (i, 0))
    return pl.pallas_call(
        body,
        name=name,
        grid=(R // tr,),
        in_specs=[spec] * 4,
        out_specs=[spec] * 3,
        out_shape=[jax.ShapeDtypeStruct((R, C), F32)] * 3,
        compiler_params=_params(("parallel",)),
    )(w, g, m, v)


def _add_core_rows(g, recv, core, name):
    h, width = recv.shape
    th = 128 if h % 128 == 0 else h

    def body(c_ref, g_ref, r_ref, o_ref):
        o_ref[...] = g_ref[...] + r_ref[...]

    return pl.pallas_call(
        body,
        name=name,
        grid_spec=pltpu.PrefetchScalarGridSpec(
            num_scalar_prefetch=1,
            grid=(h // th,),
            in_specs=[
                pl.BlockSpec((th, width), lambda i, c: (c[0] * (h // th) + i, 0)),
                pl.BlockSpec((th, width), lambda i, c: (i, 0)),
            ],
            out_specs=pl.BlockSpec((th, width), lambda i, c: (i, 0)),
        ),
        out_shape=jax.ShapeDtypeStruct((h, width), F32),
        compiler_params=_params(("parallel",)),
    )(core, g, recv)


def _add_core_blocks(g, recv, core, name):
    n, hb, C = recv.shape

    def body(c_ref, g_ref, r_ref, o_ref):
        o_ref[...] = g_ref[...] + r_ref[...]

    return pl.pallas_call(
        body,
        name=name,
        grid_spec=pltpu.PrefetchScalarGridSpec(
            num_scalar_prefetch=1,
            grid=(n,),
            in_specs=[
                pl.BlockSpec((hb, C), lambda p, c: (2 * p + c[0], 0)),
                pl.BlockSpec((None, hb, C), lambda p, c: (p, 0, 0)),
            ],
            out_specs=pl.BlockSpec((None, hb, C), lambda p, c: (p, 0, 0)),
        ),
        out_shape=jax.ShapeDtypeStruct((n, hb, C), F32),
        compiler_params=_params(("parallel",)),
    )(core, g, recv)


def _add_chips(recv, name):
    _, h, W = recv.shape
    th = 256 if h % 256 == 0 else h

    def body(r_ref, o_ref):
        o_ref[...] = ((r_ref[N_CHIPS - 1] + r_ref[0]) + r_ref[1]) + r_ref[2]

    return pl.pallas_call(
        body,
        name=name,
        grid=(h // th,),
        in_specs=[pl.BlockSpec((N_CHIPS, th, W), lambda i: (0, i, 0))],
        out_specs=pl.BlockSpec((th, W), lambda i: (i, 0)),
        out_shape=jax.ShapeDtypeStruct((h, W), F32),
        compiler_params=_params(("parallel",)),
    )(recv)


def _place():
    x, y, c = lax.axis_index("x"), lax.axis_index("y"), lax.axis_index("c")
    other_chips = [(1 - x, y), (x, 1 - y), (1 - x, 1 - y)]
    return x, y, c, other_chips


def _allgather_weights(w_in_bf, w_out_bf, conv_w):
    D, S = w_in_bf.shape
    R = w_out_bf.shape[0]
    n_ici, n_fwd = 3 * (N_CHIPS - 1), 2 * (N_CHIPS - 1)

    def body(win, wout, cw, gin, gout, gcw, send_sems, recv_sems):
        x, y, c, chips = _place()
        me = 2 * x + y
        sibling = (x, y, 1 - c)
        hin, hout = D // 2, R // 2

        def halves(chip_idx):
            return (gin.at[chip_idx, pl.ds(c * hin, hin)], gout.at[chip_idx, pl.ds(c * hout, hout)])

        def rcopy(k, src, dst, to):
            return pltpu.make_async_remote_copy(src_ref=src, dst_ref=dst, send_sem=send_sems.at[k],
                                                recv_sem=recv_sems.at[k], device_id=to, device_id_type=MESH)

        my_in, my_out = halves(me)
        src_in, src_out = win.at[pl.ds(c * hin, hin)], wout.at[pl.ds(c * hout, hout)]
        sends = []
        for j, chip in enumerate(chips):
            to = (*chip, c)
            sends += [rcopy(3 * j, src_in, my_in, to), rcopy(3 * j + 1, src_out, my_out, to),
                      rcopy(3 * j + 2, cw, gcw.at[me], to)]
        for cp in sends:
            cp.start()
        passed = []
        for j, (px, py) in enumerate(chips):
            their_in, their_out = halves(2 * px + py)
            rcopy(3 * j, their_in, their_in, sibling).wait_recv()
            rcopy(3 * j + 1, their_out, their_out, sibling).wait_recv()
            rcopy(3 * j + 2, cw, gcw.at[2 * px + py], sibling).wait_recv()
            fw = [rcopy(n_ici + 2 * j, their_in, their_in, sibling), rcopy(n_ici + 2 * j + 1, their_out, their_out, sibling)]
            for cp in fw:
                cp.start()
            passed += fw
        for j, (px, py) in enumerate(chips):
            oin = gin.at[2 * px + py, pl.ds((1 - c) * hin, hin)]
            oout = gout.at[2 * px + py, pl.ds((1 - c) * hout, hout)]
            rcopy(n_ici + 2 * j, oin, oin, sibling).wait_recv()
            rcopy(n_ici + 2 * j + 1, oout, oout, sibling).wait_recv()
        for cp in sends + passed:
            cp.wait_send()

    return pl.pallas_call(
        body,
        name="allgather_weights",
        in_specs=[ANY, ANY, ANY],
        out_specs=[ANY, ANY, ANY],
        out_shape=[jax.ShapeDtypeStruct((N_CHIPS, D, S), BF16), jax.ShapeDtypeStruct((N_CHIPS, R, D), BF16),
                   jax.ShapeDtypeStruct((N_CHIPS,) + conv_w.shape, F32)],
        scratch_shapes=[pltpu.SemaphoreType.DMA((n_ici + n_fwd,)), pltpu.SemaphoreType.DMA((n_ici + n_fwd,))],
    )(w_in_bf, w_out_bf, conv_w)


def _allreduce_small(packed):
    R = packed.shape[0]
    n_dev = 2 * N_CHIPS

    def body(p_ref, o_ref, buf, send_sems, recv_sems):
        x, y, c, _ = _place()
        me = 4 * x + 2 * y + c
        buf[me] = p_ref[...]
        copies = []
        for k in range(1, n_dev):
            px = 1 - x if k & 4 else x
            py = 1 - y if k & 2 else y
            pc = 1 - c if k & 1 else c
            copies.append((pltpu.make_async_remote_copy(
                src_ref=buf.at[me], dst_ref=buf.at[me], send_sem=send_sems.at[k - 1], recv_sem=recv_sems.at[k - 1],
                device_id=(px, py, pc), device_id_type=MESH), 4 * px + 2 * py + pc, (px, py, pc)))
        for cp, _, _ in copies:
            cp.start()
        for k, (_, peer, to) in enumerate(copies):
            pltpu.make_async_remote_copy(
                src_ref=buf.at[peer], dst_ref=buf.at[peer], send_sem=send_sems.at[k], recv_sem=recv_sems.at[k],
                device_id=to, device_id_type=MESH).wait_recv()
        for cp, _, _ in copies:
            cp.wait_send()
        acc = buf[0]
        for d in range(1, n_dev):
            acc = acc + buf[d]
        o_ref[...] = acc

    vm = pl.BlockSpec(memory_space=pltpu.VMEM)
    return pl.pallas_call(
        body,
        name="allreduce_small",
        in_specs=[vm],
        out_specs=vm,
        out_shape=jax.ShapeDtypeStruct((R, LANES), F32),
        scratch_shapes=[pltpu.VMEM((n_dev, R, LANES), F32), pltpu.SemaphoreType.DMA((n_dev - 1,)),
                        pltpu.SemaphoreType.DMA((n_dev - 1,))],
    )(packed)


def _swap_core_halves(g_in, g_out, width):
    h = g_in.shape[0] // 2
    hb = g_out.shape[0] // (2 * N_CHIPS)

    def body(gin, gout, rin, rout, send_sems, recv_sems):
        x, y, c, _ = _place()

        def rcopy(k, src, dst):
            return pltpu.make_async_remote_copy(src_ref=src, dst_ref=dst, send_sem=send_sems.at[k], recv_sem=recv_sems.at[k],
                                                device_id=(x, y, 1 - c), device_id_type=MESH)

        cps = [rcopy(0, gin.at[pl.ds((1 - c) * h, h), pl.ds(0, width)], rin)]
        cps += [rcopy(1 + p, gout.at[pl.ds((2 * p + 1 - c) * hb, hb)], rout.at[p]) for p in range(N_CHIPS)]
        for cp in cps:
            cp.start()
        for cp in cps:
            cp.wait()

    n = 1 + N_CHIPS
    return pl.pallas_call(
        body,
        name="reduce_core_swap",
        in_specs=[ANY, ANY],
        out_specs=[ANY, ANY],
        out_shape=[jax.ShapeDtypeStruct((h, width), F32), jax.ShapeDtypeStruct((N_CHIPS, hb, g_out.shape[1]), F32)],
        scratch_shapes=[pltpu.SemaphoreType.DMA((n,)), pltpu.SemaphoreType.DMA((n,))],
    )(g_in, g_out)


def _join_core_halves(g_in, g_out):
    def body(gin, gout, fin, fout, send_sems, recv_sems):
        x, y, c, _ = _place()
        cps = [pltpu.make_async_remote_copy(src_ref=s, dst_ref=d.at[c], send_sem=send_sems.at[k],
                                            recv_sem=recv_sems.at[k], device_id=(x, y, 1 - c), device_id_type=MESH)
               for k, (s, d) in enumerate(((gin, fin), (gout, fout)))]
        for cp in cps:
            cp.start()
        for k, (s, d) in enumerate(((gin, fin), (gout, fout))):
            pltpu.make_async_remote_copy(src_ref=s, dst_ref=d.at[1 - c], send_sem=send_sems.at[k],
                                         recv_sem=recv_sems.at[k], device_id=(x, y, 1 - c),
                                         device_id_type=MESH).wait_recv()
        for cp in cps:
            cp.wait_send()

    return pl.pallas_call(
        body,
        name="reduce_core_join",
        in_specs=[ANY, ANY],
        out_specs=[ANY, ANY],
        out_shape=[jax.ShapeDtypeStruct((2,) + g_in.shape, F32), jax.ShapeDtypeStruct((2,) + g_out.shape, F32)],
        scratch_shapes=[pltpu.SemaphoreType.DMA((2,)), pltpu.SemaphoreType.DMA((2,))],
    )(g_in, g_out)


def _pack(arrays):
    rows = []
    for a in arrays:
        flat = a.reshape(-1).astype(F32)
        n = -(-flat.shape[0] // LANES) * LANES
        rows.append(jnp.pad(flat, (0, n - flat.shape[0])).reshape(-1, LANES))
    out = jnp.concatenate(rows, axis=0)
    return jnp.pad(out, ((0, -out.shape[0] % 8), (0, 0)))


def _unpack(packed, shapes):
    out, r = [], 0
    for shp in shapes:
        n = math.prod(shp)
        nr = -(-n // LANES)
        out.append(packed[r:r + nr].reshape(-1)[:n].reshape(shp))
        r += nr
    return out


def _pad_lanes(a):
    return jnp.pad(a, ((0, 0), (0, LANES - a.shape[1])))


def kernel(x, norm_w, w_in, q_norm_w, k_norm_w, conv_w, conv_b, dt_bias, A_log, D_skip, sb_norm_w, ssd_norm_w, w_out, loss_target, m_norm_w, m_w_in, m_q_norm_w, m_k_norm_w, m_conv_w, m_conv_b, m_dt_bias, m_A_log, m_D_skip, m_sb_norm_w, m_ssd_norm_w, m_w_out, v_norm_w, v_w_in, v_q_norm_w, v_k_norm_w, v_conv_w, v_conv_b, v_dt_bias, v_A_log, v_D_skip, v_sb_norm_w, v_ssd_norm_w, v_w_out):
    Bl, L, D = x.shape
    T = Bl * L
    S = w_in.shape[2]
    R = w_out.shape[1]
    CW = conv_w.shape[2]
    n_in = N_CHIPS * S
    CD = D + 2 * SSD_GROUPS * SSD_STATE
    H = D // HEAD_DIM
    n_main = 6 * D + 512
    P = -(-(n_main + LANES) // 1024) * 1024
    assert n_in == n_main + H and CD == N_CHIPS * CW and 2 * D == N_CHIPS * R and CD == D + 512
    chip = (2 * lax.axis_index("x") + lax.axis_index("y")).astype(jnp.int32)
    core = lax.axis_index("c").astype(jnp.int32)

    w_in_bf, w_out_shard_bf = w_in[0].astype(BF16), w_out[0].astype(BF16)
    g_in, g_out, g_cw = _allgather_weights(w_in_bf, w_out_shard_bf, conv_w[0])
    g_in = lax.dynamic_update_slice(g_in, w_in_bf[None], (chip, 0, 0))
    g_out = lax.dynamic_update_slice(g_out, w_out_shard_bf[None], (chip, 0, 0))
    g_cw = lax.dynamic_update_slice(g_cw, conv_w, (chip, 0, 0))
    w_pad = jnp.pad(g_in.transpose(1, 0, 2).reshape(D, n_in), ((0, 0), (0, P - n_in)))
    w_out_bf = g_out.reshape(2 * D, D)
    conv_full = g_cw.transpose(1, 0, 2).reshape(CONV_K, CD)
    cwx, cwb = conv_full[:, :D], conv_full[:, D:]
    cbx, cbb = conv_b[:, :D], conv_b[:, D:]
    dtb, alog, dsk = _pad_lanes(dt_bias), _pad_lanes(A_log), _pad_lanes(D_skip)
    qw2, kw2 = jnp.tile(q_norm_w, (1, 2)), jnp.tile(k_norm_w, (1, 2))

    x2 = x.reshape(T, D)
    proj, hn_t, w_pad_t = _inproj(x2, norm_w, w_pad)
    proj3 = proj.reshape(Bl, L, P)
    o_sb = _attn_fwd(proj3, qw2, kw2, D)
    y_ssd, s_in = _ssd_fwd(proj3, cwx, cwb, cbx, cbb, dtb, alog, dsk, D)
    dout, dout_bf, mixed_t, do_sb, dy_ssd, dz_bf, dnw_out, loss_blk = _gate_out(
        x2, loss_target.reshape(T, D), o_sb.reshape(T, D), proj, y_ssd.reshape(T, D), sb_norm_w, ssd_norm_w, w_out_bf,
        w_out_bf.T)

    dq, dk, dv, dqkw = _attn_bwd(proj3, o_sb, do_sb.reshape(Bl, L, D), qw2, kw2, D)
    dtail, dcwx, dcwb, dcbx, dcbb, misc = _ssd_bwd(
        proj3, s_in, dy_ssd.reshape(Bl, L, D), cwx, cwb, cbx, cbb, dtb, alog, dsk, D, P - 5 * D)
    dproj = [dq.reshape(T, D), dk.reshape(T, D), dv.reshape(T, D), dz_bf, dtail.reshape(T, P - 5 * D)]
    gw_in = _grad_w_in(hn_t, dproj)
    gw_out = _matmul(mixed_t, dout_bf, "grad_w_out")

    slab_off = S // LANES * LANES
    slab_w = -(-(S + (N_CHIPS - 1) * (S - slab_off)) // LANES) * LANES
    width = (N_CHIPS - 1) * slab_off + slab_w
    assert n_in <= width <= P
    core1 = core.reshape(1)
    r_in, r_out = _swap_core_halves(gw_in, gw_out, width)
    h_in = _add_core_rows(gw_in, r_in, core1, "sum_cores_w_in")
    h_out = _add_core_blocks(gw_out, r_out, core1, "sum_cores_w_out")
    grad_x2, dnw_in, s_in_, s_out_ = _dhn(dproj, w_pad_t, x2, dout, norm_w, h_in, h_out, slab_off, slab_w)
    gh_in = _add_chips(s_in_, "sum_chips_w_in")
    gh_out = _add_chips(s_out_, "sum_chips_w_out")
    f_in, f_out = _join_core_halves(gh_in, gh_out)
    g_slab = lax.dynamic_update_slice(f_in, gh_in[None], (core, 0, 0)).reshape(D, slab_w)
    g_w_in = lax.dynamic_slice(g_slab, (0, chip * (S - slab_off)), (D, S))
    g_w_out = lax.dynamic_update_slice(f_out, gh_out[None], (core, 0, 0)).reshape(R, D)

    small_shapes = [(1, D), (1, D), (1, D), (1, CD), (1, HEAD_DIM), (1, HEAD_DIM), (1, H), (1, H), (1, H)]
    g_small_local = [dnw_in[0:1], dnw_out[0:1], dnw_out[1:2], jnp.concatenate([dcbx, dcbb], axis=1),
                     dqkw[0:1, :HEAD_DIM] + dqkw[0:1, HEAD_DIM:], dqkw[1:2, :HEAD_DIM] + dqkw[1:2, HEAD_DIM:],
                     misc[0:1, :H], misc[1:2, :H], misc[2:3, :H]]
    packed = _pack(g_small_local + [jnp.concatenate([dcwx, dcwb], axis=1), loss_blk[0:1, 0:1]])
    red = _allreduce_small(packed)
    g_small = _unpack(red, small_shapes + [(CONV_K, CD), (1, 1)])
    g_conv_w = lax.dynamic_slice_in_dim(g_small[9], chip * CW, CW, axis=1)
    loss = g_small[10][0, 0]

    d_in, nm_in, nv_in = _adamw(w_in[0], g_w_in, m_w_in[0], v_w_in[0], "adamw_w_in")
    d_out, nm_out, nv_out = _adamw(w_out[0], g_w_out, m_w_out[0], v_w_out[0], "adamw_w_out")
    d_cw, nm_cw, nv_cw = _adamw(conv_w[0], g_conv_w, m_conv_w[0], v_conv_w[0], "adamw_conv_w")
    small_w = [norm_w, sb_norm_w, ssd_norm_w, conv_b, q_norm_w, k_norm_w, dt_bias, A_log, D_skip]
    small_m = [m_norm_w, m_sb_norm_w, m_ssd_norm_w, m_conv_b, m_q_norm_w, m_k_norm_w, m_dt_bias, m_A_log, m_D_skip]
    small_v = [v_norm_w, v_sb_norm_w, v_ssd_norm_w, v_conv_b, v_q_norm_w, v_k_norm_w, v_dt_bias, v_A_log, v_D_skip]
    d_s, nm_s, nv_s = _adamw(_pack(small_w), _pack(g_small[:9]), _pack(small_m), _pack(small_v), "adamw_small")
    d_s, nm_s, nv_s = (_unpack(t, small_shapes) for t in (d_s, nm_s, nv_s))

    def ordered(s, w_in_, conv_w_, w_out_):
        return [s[0], w_in_[None], s[4], s[5], conv_w_[None], s[3], s[6], s[7], s[8], s[1], s[2], w_out_[None]]

    return (loss, grad_x2.reshape(Bl, L, D),
            *ordered(g_small[:9], g_w_in, g_conv_w, g_w_out),
            *ordered(d_s, d_in, d_cw, d_out),
            *ordered(nm_s, nm_in, nm_cw, nm_out),
            *ordered(nv_s, nv_in, nv_cw, nv_out))
```

```python
import functools
import math

import jax
import jax.numpy as jnp
from jax import lax
from jax.experimental import pallas as pl
from jax.experimental.pallas import tpu as pltpu

F32 = jnp.float32
BF16 = jnp.bfloat16
EPS = 1e-6
HEAD_DIM = 64
PAIR = 2 * HEAD_DIM
LANES = 128
SSD_STATE = 128
SSD_GROUPS = 2
BLK = 128
PREP_BLOCKS = 4
Q_TOGETHER_FWD = 2
Q_TOGETHER_BWD = 2
FIRST_LEFT = 2
UNDERFLOW = -105.0
CONV_K = 4
HALO = 8
N_CHIPS = 4
ADAM_LR, ADAM_B1, ADAM_B2, ADAM_EPS, ADAM_WD, ADAM_STEP = 0.001, 0.9, 0.999, 1e-08, 0.01, 10
VMEM_LIMIT_V7X = 56 * 1024 * 1024
MESH = pl.DeviceIdType.MESH
ANY = pl.BlockSpec(memory_space=pl.ANY)
NT = (((1,), (1,)), ((), ()))


def _params(sem=None):
    kw = dict(vmem_limit_bytes=VMEM_LIMIT_V7X)
    if sem is not None:
        kw["dimension_semantics"] = sem
    return pltpu.CompilerParams(**kw)


def _dot(a, b):
    return jnp.dot(a, b, preferred_element_type=F32)


def _dot_nt(a, b):
    return lax.dot_general(a, b, NT, preferred_element_type=F32)


def _dot_split(m, x):
    hi = x.astype(BF16)
    lo = (x - hi.astype(F32)).astype(BF16)
    return _dot(m, hi) + _dot(m, lo)


def _iota(shape, dim):
    return lax.broadcasted_iota(jnp.int32, shape, dim)


def _rowsum(x):
    return jnp.sum(x, axis=1, keepdims=True)


def _colsum(x):
    return jnp.sum(x, axis=0, keepdims=True)


def _sigmoid(x):
    return 0.5 * jnp.tanh(0.5 * x) + 0.5


def _softplus(x):
    return jnp.maximum(x, 0.0) + jnp.log(1.0 + jnp.exp(-jnp.abs(x)))


def _inproj(x2, norm_w, w_pad):
    T, D = x2.shape
    P = w_pad.shape[1]
    tm = min(1024, T)
    tn = 1024 if P % 1024 == 0 else 512
    nj = P // tn

    def body(x_ref, nw_ref, w_ref, proj_ref, hnt_ref, wt_ref, hn_s):
        @pl.when(pl.program_id(1) == 0)
        def _():
            xv = x_ref[...]
            r = lax.rsqrt(jnp.mean(xv * xv, axis=-1, keepdims=True) + EPS)
            hn = xv * r * nw_ref[...]
            hn_s[...] = hn.astype(BF16)
            hnt_ref[...] = hn.T.astype(BF16)

        @pl.when(pl.program_id(0) == 0)
        def _():
            wt_ref[...] = w_ref[...].astype(F32).T.astype(BF16)

        proj_ref[...] = _dot(hn_s[...], w_ref[...])

    return pl.pallas_call(
        body,
        name="inproj",
        grid=(T // tm, P // tn),
        in_specs=[
            pl.BlockSpec((tm, D), lambda i, j: (i, 0)),
            pl.BlockSpec((1, D), lambda i, j: (0, 0)),
            pl.BlockSpec((D, tn), lambda i, j: (0, j)),
        ],
        out_specs=[
            pl.BlockSpec((tm, tn), lambda i, j: (i, j)),
            pl.BlockSpec((D, tm), lambda i, j: (0, i)),
            pl.BlockSpec((tn, D), lambda i, j: (jnp.where(i == 0, j, nj - 1), 0)),
        ],
        out_shape=[jax.ShapeDtypeStruct((T, P), F32), jax.ShapeDtypeStruct((D, T), BF16),
                   jax.ShapeDtypeStruct((P, D), BF16)],
        scratch_shapes=[pltpu.VMEM((tm, D), BF16)],
        compiler_params=_params(("arbitrary", "arbitrary")),
    )(x2, norm_w, w_pad)


def _pair_ones():
    ri = ((_iota((2 * PAIR, PAIR), 0) % PAIR) >= HEAD_DIM).astype(jnp.int32)
    ci = (_iota((2 * PAIR, PAIR), 1) >= HEAD_DIM).astype(jnp.int32)
    return jnp.where(ri == ci, 1.0, 0.0).astype(BF16)


def _pair_rms(v, ones2):
    return lax.rsqrt(_split_dots([v * v], ones2)[0] * (1.0 / HEAD_DIM) + EPS)


def _pair_mean(v, ones2):
    return _split_dots([v], ones2)[0] * (1.0 / HEAD_DIM)


def _suffix_ones():
    ri = _iota((2 * BLK, 2 * BLK), 0) % BLK
    ci = _iota((2 * BLK, 2 * BLK), 1)
    return jnp.where((ci >= BLK) | (ri > ci), 1.0, 0.0).astype(BF16)


def _split_dots(xs, m2):
    his = [x.astype(BF16) for x in xs]
    los = [(x - hi.astype(F32)).astype(BF16) for x, hi in zip(xs, his)]
    return [_dot(jnp.concatenate([hi, lo], axis=1), m2) for hi, lo in zip(his, los)]


def _sb_tiles(streams, km_s, uo):
    tiles = [(s, u, h) for s, st in enumerate(streams) for u in range(len(st["kbs"])) for h in range(2)]
    z2s = {(s, u): _dot_nt(st["q"], km_s[kb]) for s, st in enumerate(streams) for u, kb in enumerate(st["kbs"])}
    zs = [z2s[s, u][:, h * BLK:(h + 1) * BLK] for s, u, h in tiles]
    es = [jnp.exp(-jnp.abs(z)) for z in zs]
    las = [jnp.minimum(z, 0.0) - jnp.log(1.0 + e) for z, e in zip(zs, es)]
    lns = [a - z for a, z in zip(las, zs)]
    masks = [streams[s]["masks"][u] for s, u, h in tiles]
    lks = [lk if m is None else jnp.where(m, lk, 0.0) for m, lk in zip(masks, lns)]
    css = _split_dots(lks, uo)
    rests = [list(st["rest"]) for st in streams]
    ws = []
    for (s, u, h), m, a, cs in zip(tiles, masks, las, css):
        w = jnp.exp(a + rests[s][h] + cs[:, :BLK])
        ws.append(w if m is None else jnp.where(m, w, 0.0))
        rests[s][h] = rests[s][h] + cs[:, BLK:]
    return tiles, las, lns, ws, rests


def _stream(q_pair, qi, n_left, diag, zero):
    return dict(q=q_pair, kbs=[qi - u for u in range(n_left + 1)], masks=[diag] + [None] * n_left, rest=[zero, zero])


def _row0(block):
    return block * BLK if isinstance(block, int) else pl.multiple_of(block * BLK, BLK)


def _pair_of(vals, tiles, s, u):
    return [v for v, t in zip(vals, tiles) if t[0] == s and t[1] == u]


def _block_groups(nq, together):
    n_tog = math.gcd(together, nq)
    assert n_tog >= FIRST_LEFT
    return n_tog, list(range(n_tog)), nq // n_tog


def _attn_prep(src_ref, w_ref, dst_s, n_blocks, scale):
    per = math.gcd(PREP_BLOCKS, n_blocks)
    rows = per * BLK
    lo = _iota((rows, PAIR), 1) < HEAD_DIM
    ones2 = _pair_ones()

    def step(i, carry):
        r0 = pl.multiple_of(i * rows, rows)
        v = src_ref[0, pl.ds(r0, rows), :]
        if w_ref is not None:
            v = v * _pair_rms(v, ones2) * w_ref[...]
        if scale != 1.0:
            v = v * scale
        v0, v1 = jnp.where(lo, v, 0.0).astype(BF16), jnp.where(lo, 0.0, v).astype(BF16)
        for b in range(per):
            dst_s[i * per + b, 0:BLK, :] = v0[b * BLK:(b + 1) * BLK]
            dst_s[i * per + b, BLK:2 * BLK, :] = v1[b * BLK:(b + 1) * BLK]
        return carry

    lax.fori_loop(0, n_blocks // per, step, 0)


def _attn_fwd(proj3, qw2, kw2, D):
    Bl, L, _ = proj3.shape
    n_pair = D // PAIR
    nq = L // BLK
    scale = 1.0 / math.sqrt(HEAD_DIM)

    def body(q_ref, k_ref, v_ref, qw_ref, kw_ref, o_ref, qm_s, km_s, vm_s):
        uo = _suffix_ones()
        diag = _iota((BLK, BLK), 1) < _iota((BLK, BLK), 0)
        _attn_prep(q_ref, qw_ref, qm_s, nq, scale)
        _attn_prep(k_ref, kw_ref, km_s, nq, 1.0)
        _attn_prep(v_ref, None, vm_s, nq, 1.0)

        zero_c = jnp.zeros((BLK, BLK), F32)

        def q_of(qi):
            return qm_s[qi, 0:BLK, :] + qm_s[qi, BLK:2 * BLK, :]

        def values(streams, accs):
            tiles, _, _, ws, rests = _sb_tiles(streams, km_s, uo)
            wbs = [w.astype(BF16) for w in ws]
            accs = list(accs)
            for s, st in enumerate(streams):
                for u, kb in enumerate(st["kbs"]):
                    accs[s] = accs[s] + _dot(jnp.concatenate(_pair_of(wbs, tiles, s, u), axis=1), vm_s[kb])
            return accs, rests

        def group(qis, n_lefts):
            streams = [_stream(q_of(qi), qi, n, diag, zero_c) for qi, n in zip(qis, n_lefts)]
            accs, rests = values(streams, [jnp.zeros((BLK, PAIR), F32)] * len(qis))
            for qi, n, q, acc, rc in zip(qis, n_lefts, [st["q"] for st in streams], accs, rests):

                def sweep(state, n_blocks, q=q):
                    kb, rc0, rc1, acc1, _ = state
                    st = dict(q=q, kbs=[kb - u for u in range(n_blocks)], masks=[None] * n_blocks, rest=[rc0, rc1])
                    (acc1,), (r,) = values([st], [acc1])
                    return kb - n_blocks, r[0], r[1], acc1, jnp.maximum(jnp.max(r[0]), jnp.max(r[1]))

                state = (jnp.asarray(qi - n - 1, jnp.int32), rc[0], rc[1], acc, jnp.maximum(jnp.max(rc[0]), jnp.max(rc[1])))
                state = lax.while_loop(lambda t: (t[0] >= 1) & (t[4] >= UNDERFLOW), lambda t: sweep(t, 2), state)
                state = lax.while_loop(lambda t: (t[0] >= 0) & (t[4] >= UNDERFLOW), lambda t: sweep(t, 1), state)
                o_ref[0, pl.ds(_row0(qi), BLK), :] = state[3]

        n_tog, head, n_groups = _block_groups(nq, Q_TOGETHER_FWD)
        group(head, [min(qi, FIRST_LEFT) for qi in head])

        def groups(g, carry):
            group([g * n_tog + j for j in range(n_tog)], [FIRST_LEFT] * n_tog)
            return carry

        lax.fori_loop(1, n_groups, groups, 0)

    blk = lambda off: pl.BlockSpec((1, L, PAIR), lambda b, p: (b, 0, off + p))
    wspec = pl.BlockSpec((1, PAIR), lambda b, p: (0, 0))
    return pl.pallas_call(
        body,
        name="sb_attn_fwd",
        grid=(Bl, n_pair),
        in_specs=[blk(0), blk(n_pair), blk(2 * n_pair), wspec, wspec],
        out_specs=pl.BlockSpec((1, L, PAIR), lambda b, p: (b, 0, p)),
        out_shape=jax.ShapeDtypeStruct((Bl, L, D), F32),
        scratch_shapes=[pltpu.VMEM((nq, 2 * BLK, PAIR), BF16)] * 3,
        compiler_params=_params(("parallel", "parallel")),
    )(proj3, proj3, proj3, qw2, kw2)


def _attn_bwd(proj3, o3, do3, qw2, kw2, D):
    Bl, L, _ = proj3.shape
    n_pair = D // PAIR
    nq = L // BLK
    scale = 1.0 / math.sqrt(HEAD_DIM)

    def body(q_ref, k_ref, v_ref, o_ref, do_ref, qw_ref, kw_ref, dq_ref, dk_ref, dv_ref, dw_ref,
             qm_s, km_s, vm_s, dom_s, dq_s, dk_s, dv_s):
        uo = _suffix_ones()
        diag = _iota((BLK, BLK), 1) < _iota((BLK, BLK), 0)
        ones2 = _pair_ones()
        _attn_prep(q_ref, qw_ref, qm_s, nq, scale)
        _attn_prep(k_ref, kw_ref, km_s, nq, 1.0)
        _attn_prep(v_ref, None, vm_s, nq, 1.0)
        _attn_prep(do_ref, None, dom_s, nq, 1.0)

        @pl.when((pl.program_id(0) == 0) & (pl.program_id(1) == 0))
        def _():
            dw_ref[...] = jnp.zeros_like(dw_ref)

        def zero(i, carry):
            r0 = pl.multiple_of(i * BLK, BLK)
            dk_s[pl.ds(r0, BLK), :] = jnp.zeros((BLK, PAIR), F32)
            dv_s[pl.ds(r0, BLK), :] = jnp.zeros((BLK, PAIR), F32)
            return carry

        lax.fori_loop(0, nq, zero, 0)

        zero_c = jnp.zeros((BLK, BLK), F32)

        def tiles_bwd(streams, dqas):
            tiles, las, lns, ws, rests = _sb_tiles(streams, km_s, uo)
            dw2s = {(s, u): _dot_nt(st["do"], vm_s[kb]) for s, st in enumerate(streams) for u, kb in enumerate(st["kbs"])}
            dws = [dw2s[s, u][:, h * BLK:(h + 1) * BLK] for s, u, h in tiles]
            wfs = [w.astype(BF16).astype(F32) for w in ws]
            gs = [wf * dw for wf, dw in zip(wfs, dws)]
            gss = _split_dots(gs, uo)
            gcs = [list(st["g_rest"]) for st in streams]
            dzs = []
            for (s, u, h), a, ln, g, gsum in zip(tiles, las, lns, gs, gss):
                g_before = streams[s]["delta"][h] - (gcs[s][h] + gsum[:, :BLK] + g)
                gcs[s][h] = gcs[s][h] + gsum[:, BLK:]
                dz = g * jnp.exp(ln) - g_before * jnp.exp(a)
                m = streams[s]["masks"][u]
                dzs.append(dz if m is None else jnp.where(m, dz, 0.0))
            wts = [wf.T.astype(BF16) for wf in wfs]
            dzts = [dz.T.astype(BF16) for dz in dzs]
            dzbs = [dz.astype(BF16) for dz in dzs]
            dqas = list(dqas)
            for s, st in enumerate(streams):
                for u, kb in enumerate(st["kbs"]):
                    c0 = _row0(kb)
                    dv_s[pl.ds(c0, BLK), :] += _dot(jnp.concatenate(_pair_of(wts, tiles, s, u), axis=1), dom_s[st["qi"]])
                    dk_s[pl.ds(c0, BLK), :] += _dot(jnp.concatenate(_pair_of(dzts, tiles, s, u), axis=1), qm_s[st["qi"]])
                    dqas[s] = dqas[s] + _dot(jnp.concatenate(_pair_of(dzbs, tiles, s, u), axis=1), km_s[kb])
            return dqas, rests, gcs

        def group(qis, n_lefts):
            streams = []
            for qi, n in zip(qis, n_lefts):
                o_blk = o_ref[0, pl.ds(_row0(qi), BLK), :]
                doms = [dom_s[qi, 0:BLK, :], dom_s[qi, BLK:2 * BLK, :]]
                st = _stream(qm_s[qi, 0:BLK, :] + qm_s[qi, BLK:2 * BLK, :], qi, n, diag, zero_c)
                st.update(qi=qi, do=doms[0] + doms[1], delta=[_rowsum(d.astype(F32) * o_blk) for d in doms],
                          g_rest=[zero_c, zero_c])
                streams.append(st)
            dqas, rests, gcs = tiles_bwd(streams, [jnp.zeros((BLK, PAIR), F32)] * len(qis))
            for qi, n, st0, dqa, rc, gc in zip(qis, n_lefts, streams, dqas, rests, gcs):

                def sweep(state, n_blocks, st0=st0):
                    kb, rc0, rc1, gc0, gc1, dqa1, _ = state
                    st = dict(st0, kbs=[kb - u for u in range(n_blocks)], masks=[None] * n_blocks, rest=[rc0, rc1],
                              g_rest=[gc0, gc1])
                    (dqa1,), (r,), (g,) = tiles_bwd([st], [dqa1])
                    return kb - n_blocks, r[0], r[1], g[0], g[1], dqa1, jnp.maximum(jnp.max(r[0]), jnp.max(r[1]))

                state = (jnp.asarray(qi - n - 1, jnp.int32), rc[0], rc[1], gc[0], gc[1], dqa,
                         jnp.maximum(jnp.max(rc[0]), jnp.max(rc[1])))
                state = lax.while_loop(lambda t: (t[0] >= 1) & (t[6] >= UNDERFLOW), lambda t: sweep(t, 2), state)
                state = lax.while_loop(lambda t: (t[0] >= 0) & (t[6] >= UNDERFLOW), lambda t: sweep(t, 1), state)
                dq_s[pl.ds(_row0(qi), BLK), :] = state[5] * scale

        n_tog, head, n_groups = _block_groups(nq, Q_TOGETHER_BWD)
        group(head, [min(qi, FIRST_LEFT) for qi in head])

        def groups(g, carry):
            group([g * n_tog + j for j in range(n_tog)], [FIRST_LEFT] * n_tog)
            return carry

        lax.fori_loop(1, n_groups, groups, 0)

        per = math.gcd(PREP_BLOCKS, nq)
        rows = per * BLK

        def finish(i, carry):
            r0 = pl.multiple_of(i * rows, rows)
            dwq, dwk = carry
            out = []
            for src_ref, w_ref, d_s in ((q_ref, qw_ref, dq_s), (k_ref, kw_ref, dk_s)):
                v = src_ref[0, pl.ds(r0, rows), :]
                r = _pair_rms(v, ones2)
                vh = v * r
                dy = d_s[pl.ds(r0, rows), :]
                dvh = dy * w_ref[...]
                out.append((r * (dvh - vh * _pair_mean(dvh * vh, ones2)), _colsum(dy * vh)))
            dq_ref[0, pl.ds(r0, rows), :] = out[0][0].astype(BF16)
            dk_ref[0, pl.ds(r0, rows), :] = out[1][0].astype(BF16)
            dv_ref[0, pl.ds(r0, rows), :] = dv_s[pl.ds(r0, rows), :].astype(BF16)
            return dwq + out[0][1], dwk + out[1][1]

        zrow = jnp.zeros((1, PAIR), F32)
        dwq, dwk = lax.fori_loop(0, nq // per, finish, (zrow, zrow))
        dw_ref[0:1, :] += dwq
        dw_ref[1:2, :] += dwk

    blk = lambda off: pl.BlockSpec((1, L, PAIR), lambda b, p: (b, 0, off + p))
    wspec = pl.BlockSpec((1, PAIR), lambda b, p: (0, 0))
    oblk = pl.BlockSpec((1, L, PAIR), lambda b, p: (b, 0, p))
    return pl.pallas_call(
        body,
        name="sb_attn_bwd",
        grid=(Bl, n_pair),
        in_specs=[blk(0), blk(n_pair), blk(2 * n_pair), oblk, oblk, wspec, wspec],
        out_specs=[oblk, oblk, oblk, pl.BlockSpec((8, PAIR), lambda b, p: (0, 0))],
        out_shape=[jax.ShapeDtypeStruct((Bl, L, D), BF16)] * 3 + [jax.ShapeDtypeStruct((8, PAIR), F32)],
        scratch_shapes=[pltpu.VMEM((nq, 2 * BLK, PAIR), BF16)] * 4 + [pltpu.VMEM((L, PAIR), F32)] * 3,
        compiler_params=_params(("arbitrary", "arbitrary")),
    )(proj3, proj3, proj3, o3, do3, qw2, kw2)


def _conv_pre(ext_s, halo_ref, raw_ref, w_ref, b_ref, first):
    ext_s[0:HALO, :] = jnp.where(first, 0.0, halo_ref[0])
    ext_s[HALO:HALO + BLK, :] = raw_ref[0]
    pre = b_ref[...]
    for i in range(CONV_K):
        pre = pre + ext_s[pl.ds(HALO - (CONV_K - 1 - i), BLK), :] * w_ref[i:i + 1, :]
    return pre


def _lane_col(m, lane, h):
    return _rowsum(jnp.where(lane == h, m, 0.0))


def _half_sums(row, lo1):
    return _rowsum(jnp.where(lo1, row, 0.0)), _rowsum(jnp.where(lo1, 0.0, row))


def _ssd_specs(Bl, L, D, rev):
    nc = L // BLK
    rows_per = BLK // HALO
    cidx = (lambda c: nc - 1 - c) if rev else (lambda c: c)
    xoff = 5
    boff = (6 * D) // 512
    doff = (6 * D + 512) // LANES
    prev = lambda c: jnp.maximum(cidx(c) * rows_per - 1, 0)
    specs = [
        pl.BlockSpec((1, BLK, D), lambda b, c: (b, cidx(c), xoff)),
        pl.BlockSpec((1, BLK, 512), lambda b, c: (b, cidx(c), boff)),
        pl.BlockSpec((1, HALO, D), lambda b, c: (b, prev(c), xoff)),
        pl.BlockSpec((1, HALO, 512), lambda b, c: (b, prev(c), boff)),
        pl.BlockSpec((1, BLK, LANES), lambda b, c: (b, cidx(c), doff)),
    ]
    full = lambda shape: pl.BlockSpec(shape, lambda b, c: (0,) * len(shape))
    specs += [full((CONV_K, D)), full((CONV_K, 512)), full((1, D)), full((1, 512)),
              full((1, LANES)), full((1, LANES)), full((1, LANES))]
    return specs, cidx


def _ssd_common(dtr_ref, dtb_ref, alog_ref, acs_s, acsT_s):
    ltri = jnp.where(_iota((BLK, BLK), 1) <= _iota((BLK, BLK), 0), 1.0, 0.0).astype(BF16)
    dtv = _softplus(dtr_ref[0] + dtb_ref[...])
    a = -jnp.exp(alog_ref[...])
    acs = _dot_split(ltri, dtv * a)
    acs_s[...] = acs
    acsT_s[...] = acs.T
    return dtv, a, acs


def _pair_terms(pr, acs, dtv, acs_s, lane, lo, lane1, lo1):
    h0, h1 = 2 * pr, 2 * pr + 1
    c0, c1 = _lane_col(acs, lane, h0), _lane_col(acs, lane, h1)
    d0, d1 = _lane_col(dtv, lane, h0), _lane_col(dtv, lane, h1)
    lastv = acs_s[BLK - 1:BLK, :]
    l0, l1 = _lane_col(lastv, lane1, h0), _lane_col(lastv, lane1, h1)
    return dict(h=(h0, h1), c=(c0, c1), last=(l0, l1), acs_p=jnp.where(lo, c0, c1), dt_p=jnp.where(lo, d0, d1),
                last_p=jnp.where(lo1, l0, l1))


def _decay_tiles(cc, row, tri, want_t):
    lm = jnp.where(tri, jnp.exp(cc - row), 0.0)
    return lm, (lm.T if want_t else None)


def _ssd_fwd(proj3, cwx, cwb, cbx, cbb, dtb, alog, dsk, D):
    Bl, L, _ = proj3.shape
    nc = L // BLK
    n_pair = D // PAIR
    pairs_per_group = n_pair // SSD_GROUPS
    specs, _ = _ssd_specs(Bl, L, D, False)

    def body(xr_ref, bcr_ref, xh_ref, bch_ref, dtr_ref, cwx_ref, cwb_ref, cbx_ref, cbb_ref, dtb_ref, alog_ref,
             dsk_ref, y_ref, sin_ref, st_s, extx_s, extb_s, acs_s, acsT_s):
        first = pl.program_id(1) == 0

        @pl.when(first)
        def _():
            st_s[...] = jnp.zeros_like(st_s)

        lane, lane1 = _iota((BLK, LANES), 1), _iota((1, LANES), 1)
        lo, lo1 = lane < HEAD_DIM, lane1 < HEAD_DIM
        tri = _iota((BLK, BLK), 1) <= _iota((BLK, BLK), 0)
        pre = _conv_pre(extx_s, xh_ref, xr_ref, cwx_ref, cbx_ref, first)
        ux = pre * _sigmoid(pre)
        pre = _conv_pre(extb_s, bch_ref, bcr_ref, cwb_ref, cbb_ref, first)
        ub = pre * _sigmoid(pre)
        dtv, a, acs = _ssd_common(dtr_ref, dtb_ref, alog_ref, acs_s, acsT_s)
        for g in range(SSD_GROUPS):
            bg = ub[:, g * SSD_STATE:(g + 1) * SSD_STATE]
            cb_ = ub[:, (SSD_GROUPS + g) * SSD_STATE:(SSD_GROUPS + g + 1) * SSD_STATE].astype(BF16)
            cbm = _dot_nt(cb_, bg.astype(BF16))
            btb = bg.T.astype(BF16)
            for pr in range(g * pairs_per_group, (g + 1) * pairs_per_group):
                t = _pair_terms(pr, acs, dtv, acs_s, lane, lo, lane1, lo1)
                xs_p = ux[:, pr * PAIR:(pr + 1) * PAIR]
                x_p = xs_p * t["dt_p"]
                st = st_s[pr]
                sin_ref[0, 0, pr] = st
                y = _dot(cb_, st.astype(BF16)) * jnp.exp(t["acs_p"])
                for k in range(2):
                    row = acsT_s[t["h"][k]:t["h"][k] + 1, :]
                    lm, _ = _decay_tiles(t["c"][k], row, tri, False)
                    xm = jnp.where(lo if k == 0 else ~lo, x_p, 0.0).astype(BF16)
                    y = y + _dot((cbm * lm).astype(BF16), xm)
                d_p = jnp.where(lo1, _lane_col(dsk_ref[...], lane1, t["h"][0]), _lane_col(dsk_ref[...], lane1, t["h"][1]))
                y_ref[0, :, pr * PAIR:(pr + 1) * PAIR] = y + d_p * xs_p
                xd = (x_p * jnp.exp(t["last_p"] - t["acs_p"])).astype(BF16)
                st_s[pr] = st * jnp.exp(t["last_p"]) + _dot(btb, xd)

    return pl.pallas_call(
        body,
        name="ssd_fwd",
        grid=(Bl, nc),
        in_specs=specs,
        out_specs=[
            pl.BlockSpec((1, BLK, D), lambda b, c: (b, c, 0)),
            pl.BlockSpec((1, 1, n_pair, SSD_STATE, PAIR), lambda b, c: (b, c, 0, 0, 0)),
        ],
        out_shape=[jax.ShapeDtypeStruct((Bl, L, D), F32),
                   jax.ShapeDtypeStruct((Bl, nc, n_pair, SSD_STATE, PAIR), F32)],
        scratch_shapes=[pltpu.VMEM((n_pair, SSD_STATE, PAIR), F32), pltpu.VMEM((HALO + BLK, D), F32),
                        pltpu.VMEM((HALO + BLK, 512), F32), pltpu.VMEM((BLK, LANES), F32),
                        pltpu.VMEM((LANES, BLK), F32)],
        compiler_params=_params(("arbitrary", "arbitrary")),
    )(proj3, proj3, proj3, proj3, proj3, cwx, cwb, cbx, cbb, dtb, alog, dsk)


def _ssd_bwd(proj3, s_in, dy3, cwx, cwb, cbx, cbb, dtb, alog, dsk, D, tail):
    Bl, L, _ = proj3.shape
    CD = D + 512
    nc = L // BLK
    n_pair = D // PAIR
    n_heads = 2 * n_pair
    pairs_per_group = n_pair // SSD_GROUPS
    specs, cidx = _ssd_specs(Bl, L, D, True)
    specs = specs + [
        pl.BlockSpec((1, 1, n_pair, SSD_STATE, PAIR), lambda b, c: (b, cidx(c), 0, 0, 0)),
        pl.BlockSpec((1, BLK, D), lambda b, c: (b, cidx(c), 0)),
    ]

    def body(xr_ref, bcr_ref, xh_ref, bch_ref, dtr_ref, cwx_ref, cwb_ref, cbx_ref, cbb_ref, dtb_ref, alog_ref,
             dsk_ref, sin_ref, dy_ref, dxbc_ref, dcwx_ref, dcwb_ref, dcbx_ref, dcbb_ref, misc_ref,
             dst_s, extx_s, extb_s, acs_s, acsT_s, dux_s, dub_s, e2x_s, e2b_s, nxx_s, nxb_s):
        step = pl.program_id(1)
        first = step == nc - 1
        last = step == 0

        @pl.when(last)
        def _():
            dst_s[...] = jnp.zeros_like(dst_s)
            nxx_s[...] = jnp.zeros_like(nxx_s)
            nxb_s[...] = jnp.zeros_like(nxb_s)

        @pl.when(last & (pl.program_id(0) == 0))
        def _():
            for r in (dcwx_ref, dcwb_ref, dcbx_ref, dcbb_ref, misc_ref):
                r[...] = jnp.zeros_like(r)

        lane, lane1 = _iota((BLK, LANES), 1), _iota((1, LANES), 1)
        lo, lo1 = lane < HEAD_DIM, lane1 < HEAD_DIM
        tri = _iota((BLK, BLK), 1) <= _iota((BLK, BLK), 0)
        prex = _conv_pre(extx_s, xh_ref, xr_ref, cwx_ref, cbx_ref, first)
        sgx = _sigmoid(prex)
        ux = prex * sgx
        preb = _conv_pre(extb_s, bch_ref, bcr_ref, cwb_ref, cbb_ref, first)
        sgb = _sigmoid(preb)
        ub = preb * sgb
        dtv, a, acs = _ssd_common(dtr_ref, dtb_ref, alog_ref, acs_s, acsT_s)
        dacs = jnp.zeros((BLK, LANES), F32)
        dlast = jnp.zeros((1, LANES), F32)
        ddt = jnp.zeros((BLK, LANES), F32)
        dd = jnp.zeros((1, LANES), F32)
        for g in range(SSD_GROUPS):
            bg = ub[:, g * SSD_STATE:(g + 1) * SSD_STATE]
            cg = ub[:, (SSD_GROUPS + g) * SSD_STATE:(SSD_GROUPS + g + 1) * SSD_STATE]
            bb, cb_ = bg.astype(BF16), cg.astype(BF16)
            cbm = _dot_nt(cb_, bb)
            cbt = _dot_nt(bb, cb_)
            ctb = cg.T.astype(BF16)
            dbg = jnp.zeros((BLK, SSD_STATE), F32)
            dcg = jnp.zeros((BLK, SSD_STATE), F32)
            for pr in range(g * pairs_per_group, (g + 1) * pairs_per_group):
                t = _pair_terms(pr, acs, dtv, acs_s, lane, lo, lane1, lo1)
                h0, h1 = t["h"]
                xs_p = ux[:, pr * PAIR:(pr + 1) * PAIR]
                dy_p = dy_ref[0, :, pr * PAIR:(pr + 1) * PAIR]
                x_p = xs_p * t["dt_p"]
                ea_p = jnp.exp(t["acs_p"])
                dte_p = jnp.exp(t["last_p"] - t["acs_p"])
                cd_p = jnp.exp(t["last_p"])
                st = sin_ref[0, 0, pr]
                dst = dst_s[pr]
                stb, dstb = st.astype(BF16), dst.astype(BF16)
                s0, s1 = _half_sums(_colsum(dy_p * xs_p), lo1)
                dd = dd + jnp.where(lane1 == h0, s0, 0.0) + jnp.where(lane1 == h1, s1, 0.0)
                d_p = jnp.where(lo1, _lane_col(dsk_ref[...], lane1, h0), _lane_col(dsk_ref[...], lane1, h1))
                dxs_p = d_p * dy_p
                dp = dy_p * ea_p
                dpb = dp.astype(BF16)
                yo = dp * _dot(cb_, stb)
                dcg = dcg + _dot_nt(dpb, stb)
                dst_off = _dot(ctb, dpb)
                dac = [_rowsum(jnp.where(lo, yo, 0.0)), _rowsum(jnp.where(lo, 0.0, yo))]
                s0, s1 = _half_sums(_colsum(dst * st), lo1)
                dl = [s0 * jnp.exp(t["last"][0]), s1 * jnp.exp(t["last"][1])]
                dxd = _dot(bb, dstb)
                dx_p = dxd * dte_p
                tt = dxd * x_p
                dbg = dbg + _dot_nt((x_p * dte_p).astype(BF16), dstb)
                for k, ddte in enumerate((_rowsum(jnp.where(lo, tt, 0.0)), _rowsum(jnp.where(lo, 0.0, tt)))):
                    ek = ddte * jnp.exp(t["last"][k] - t["c"][k])
                    dl[k] = dl[k] + _colsum(ek)
                    dac[k] = dac[k] - ek
                x_pb = x_p.astype(BF16)
                for k in range(2):
                    row = acsT_s[t["h"][k]:t["h"][k] + 1, :]
                    lm, lmt = _decay_tiles(t["c"][k], row, tri, True)
                    dym = jnp.where(lo if k == 0 else ~lo, dy_p, 0.0).astype(BF16)
                    dm = _dot_nt(dym, x_pb)
                    dmt = _dot_nt(x_pb, dym)
                    mt = cbt * lmt
                    dx_p = dx_p + _dot(mt.astype(BF16), dym)
                    dac[k] = dac[k] + _rowsum(dm * (cbm * lm)) - _rowsum(dmt * mt)
                    dcg = dcg + _dot((dm * lm).astype(BF16), bb)
                    dbg = dbg + _dot((dmt * lmt).astype(BF16), cb_)
                dacs = dacs + jnp.where(lane == h0, dac[0], 0.0) + jnp.where(lane == h1, dac[1], 0.0)
                dlast = dlast + jnp.where(lane1 == h0, dl[0], 0.0) + jnp.where(lane1 == h1, dl[1], 0.0)
                dxs_p = dxs_p + dx_p * t["dt_p"]
                t3 = dx_p * xs_p
                ddt = ddt + jnp.where(lane == h0, _rowsum(jnp.where(lo, t3, 0.0)), 0.0) \
                    + jnp.where(lane == h1, _rowsum(jnp.where(lo, 0.0, t3)), 0.0)
                dux_s[:, pr * PAIR:(pr + 1) * PAIR] = dxs_p
                dst_s[pr] = dst * cd_p + dst_off
            dub_s[:, g * SSD_STATE:(g + 1) * SSD_STATE] = dbg
            dub_s[:, (SSD_GROUPS + g) * SSD_STATE:(SSD_GROUPS + g + 1) * SSD_STATE] = dcg
        dacs = dacs + jnp.where(_iota((BLK, LANES), 0) == BLK - 1, dlast, 0.0)
        utri = jnp.where(_iota((BLK, BLK), 1) >= _iota((BLK, BLK), 0), 1.0, 0.0).astype(BF16)
        dda = _dot_split(utri, dacs)
        ddt = ddt + dda * a
        ddtr = jnp.where(lane < n_heads, ddt * _sigmoid(dtr_ref[0] + dtb_ref[...]), 0.0)
        dxbc_ref[0, :, CD:CD + LANES] = ddtr.astype(BF16)
        dxbc_ref[0, :, CD + LANES:tail] = jnp.zeros((BLK, tail - CD - LANES), BF16)
        misc_ref[0:1, :] += _colsum(ddtr)
        misc_ref[1:2, :] += jnp.where(lane1 < n_heads, _colsum(dda * dtv) * a, 0.0)
        misc_ref[2:3, :] += dd
        for (du_s, pre, sg, ext_s, e2_s, nx_s, w_ref, dcw_ref, dcb_ref, c0, width) in (
                (dux_s, prex, sgx, extx_s, e2x_s, nxx_s, cwx_ref, dcwx_ref, dcbx_ref, 0, D),
                (dub_s, preb, sgb, extb_s, e2b_s, nxb_s, cwb_ref, dcwb_ref, dcbb_ref, D, 512)):
            dpre = du_s[...] * (sg * (1.0 + pre * (1.0 - sg)))
            dcb_ref[...] += _colsum(dpre)
            for i in range(CONV_K):
                dcw_ref[i:i + 1, :] += _colsum(dpre * ext_s[pl.ds(HALO - (CONV_K - 1 - i), BLK), :])
            e2_s[0:BLK, :] = dpre
            e2_s[BLK:BLK + HALO, :] = nx_s[...]
            dxr = jnp.zeros((BLK, width), F32)
            for i in range(CONV_K):
                dxr = dxr + e2_s[pl.ds(CONV_K - 1 - i, BLK), :] * w_ref[i:i + 1, :]
            dxbc_ref[0, :, c0:c0 + width] = dxr.astype(BF16)
            nx_s[...] = e2_s[0:HALO, :]

    full = lambda shape: pl.BlockSpec(shape, lambda b, c: (0,) * len(shape))
    return pl.pallas_call(
        body,
        name="ssd_bwd",
        grid=(Bl, nc),
        in_specs=specs,
        out_specs=[
            pl.BlockSpec((1, BLK, tail), lambda b, c: (b, cidx(c), 0)),
            full((CONV_K, D)), full((CONV_K, 512)), full((1, D)), full((1, 512)), full((8, LANES)),
        ],
        out_shape=[
            jax.ShapeDtypeStruct((Bl, L, tail), BF16),
            jax.ShapeDtypeStruct((CONV_K, D), F32), jax.ShapeDtypeStruct((CONV_K, 512), F32),
            jax.ShapeDtypeStruct((1, D), F32), jax.ShapeDtypeStruct((1, 512), F32),
            jax.ShapeDtypeStruct((8, LANES), F32),
        ],
        scratch_shapes=[
            pltpu.VMEM((n_pair, SSD_STATE, PAIR), F32),
            pltpu.VMEM((HALO + BLK, D), F32), pltpu.VMEM((HALO + BLK, 512), F32),
            pltpu.VMEM((BLK, LANES), F32), pltpu.VMEM((LANES, BLK), F32),
            pltpu.VMEM((BLK, D), F32), pltpu.VMEM((BLK, 512), F32),
            pltpu.VMEM((BLK + HALO, D), F32), pltpu.VMEM((BLK + HALO, 512), F32),
            pltpu.VMEM((HALO, D), F32), pltpu.VMEM((HALO, 512), F32),
        ],
        compiler_params=_params(("arbitrary", "arbitrary")),
    )(proj3, proj3, proj3, proj3, proj3, cwx, cwb, cbx, cbb, dtb, alog, dsk, s_in, dy3)


def _gate_out(x2, tgt2, o2, proj2, y2, sbw, ssw, w_out_bf, w_out_t):
    T, D = x2.shape
    tm = min(256, T)

    def body(x_ref, t_ref, o_ref, zs_ref, y_ref, zy_ref, sbw_ref, ssw_ref, wo_ref, wot_ref,
             dout_ref, doutb_ref, mixt_ref, do_ref, dy_ref, dz_ref, dnw_ref, loss_ref):
        @pl.when(pl.program_id(0) == 0)
        def _():
            dnw_ref[...] = jnp.zeros_like(dnw_ref)
            loss_ref[...] = jnp.zeros_like(loss_ref)

        def fwd(o, z, w):
            sg = _sigmoid(z)
            sl = z * sg
            g = o * sl
            r = lax.rsqrt(jnp.mean(g * g, axis=-1, keepdims=True) + EPS)
            n = g * r
            return sg, sl, r, n, n * w

        def bwd(dy, o, z, w, sg, sl, r, n):
            dn = dy * w
            dg = r * (dn - n * jnp.mean(dn * n, axis=-1, keepdims=True))
            return dg * sl, dg * o * (sg * (1.0 + z * (1.0 - sg))), _colsum(dy * n)

        o1, z1, w1 = o_ref[...], zs_ref[...], sbw_ref[...]
        o2_, z2, w2 = y_ref[...], zy_ref[...], ssw_ref[...]
        sg1, sl1, r1, n1, y1 = fwd(o1, z1, w1)
        sg2, sl2, r2, n2, y2_ = fwd(o2_, z2, w2)
        y1b, y2b = y1.astype(BF16), y2_.astype(BF16)
        mixt_ref[0:D, :] = y1.T.astype(BF16)
        mixt_ref[D:2 * D, :] = y2_.T.astype(BF16)
        out = x_ref[...] + (_dot(y1b, wo_ref[0:D, :]) + _dot(y2b, wo_ref[D:2 * D, :]))
        err = out - t_ref[...]
        loss_ref[...] += jnp.sum(err * err) * (0.5 / D)
        dout = err * (1.0 / D)
        dout_ref[...] = dout
        doutb = dout.astype(BF16)
        doutb_ref[...] = doutb
        do1, dz1, dw1 = bwd(_dot(doutb, wot_ref[:, 0:D]), o1, z1, w1, sg1, sl1, r1, n1)
        do2, dz2, dw2 = bwd(_dot(doutb, wot_ref[:, D:2 * D]), o2_, z2, w2, sg2, sl2, r2, n2)
        do_ref[...] = do1
        dy_ref[...] = do2
        dz_ref[:, 0:D] = dz1.astype(BF16)
        dz_ref[:, D:2 * D] = dz2.astype(BF16)
        dnw_ref[0:1, :] += dw1
        dnw_ref[1:2, :] += dw2

    row = lambda col: pl.BlockSpec((tm, D), lambda i: (i, col))
    full = lambda shape: pl.BlockSpec(shape, lambda i: (0,) * len(shape))
    wide = pl.BlockSpec((tm, 2 * D), lambda i: (i, 0))
    return pl.pallas_call(
        body,
        name="gate_out",
        grid=(T // tm,),
        in_specs=[row(0), row(0), row(0), row(3), row(0), row(4), full((1, D)), full((1, D)), full((2 * D, D)),
                  full((D, 2 * D))],
        out_specs=[row(0), row(0), pl.BlockSpec((2 * D, tm), lambda i: (0, i)), row(0), row(0), wide,
                   full((8, D)), full((8, LANES))],
        out_shape=[
            jax.ShapeDtypeStruct((T, D), F32), jax.ShapeDtypeStruct((T, D), BF16),
            jax.ShapeDtypeStruct((2 * D, T), BF16), jax.ShapeDtypeStruct((T, D), F32),
            jax.ShapeDtypeStruct((T, D), F32), jax.ShapeDtypeStruct((T, 2 * D), BF16),
            jax.ShapeDtypeStruct((8, D), F32), jax.ShapeDtypeStruct((8, LANES), F32),
        ],
        compiler_params=_params(("arbitrary",)),
    )(x2, tgt2, o2, proj2, y2, proj2, sbw, ssw, w_out_bf, w_out_t)


def _piece_blocks(pieces, D):
    counts = [p.shape[1] // D for p in pieces]
    return [sum(counts[:i]) for i in range(len(counts))], counts


def _dhn(pieces, w_pad_t, x2, dout, norm_w, h_in, h_out, slab_off, slab_w):
    T, D = x2.shape
    tm = min(1024, T)
    starts, counts = _piece_blocks(pieces, D)
    nk = sum(counts)
    ni = T // tm
    n_sem = 2 * (N_CHIPS - 1)
    assert nk * D == w_pad_t.shape[0]

    def body(*refs):
        p_refs = refs[:len(pieces)]
        (w_ref, x_hbm, dout_hbm, nw_ref, hin, hout, gx_ref, dnw_ref, rin, rout,
         acc_s, x_s, dout_s, send_sems, recv_sems, row_sems, own_sems) = refs[len(pieces):]
        i, k = pl.program_id(0), pl.program_id(1)

        def rows():
            r0 = pl.multiple_of(i * tm, tm)
            return [pltpu.make_async_copy(src.at[pl.ds(r0, tm)], dst, row_sems.at[n])
                    for n, (src, dst) in enumerate(((x_hbm, x_s), (dout_hbm, dout_s)))]

        @pl.when(k == 0)
        def _():
            for cp in rows():
                cp.start()

        def scatter():
            x, y, c, chips = _place()

            def slab(p):
                return hin.at[:, pl.ds(pl.multiple_of(p * slab_off, LANES), slab_w)]

            cps = []
            for j, (px, py) in enumerate(chips):
                for m, (src, dst) in enumerate(((slab(2 * px + py), rin.at[j]), (hout.at[2 * px + py], rout.at[j]))):
                    cps.append(pltpu.make_async_remote_copy(
                        src_ref=src, dst_ref=dst, send_sem=send_sems.at[2 * j + m], recv_sem=recv_sems.at[2 * j + m],
                        device_id=(px, py, c), device_id_type=MESH))
            me = 2 * x + y
            own = [pltpu.make_async_copy(slab(me), rin.at[N_CHIPS - 1], own_sems.at[0]),
                   pltpu.make_async_copy(hout.at[me], rout.at[N_CHIPS - 1], own_sems.at[1])]
            return cps + own

        @pl.when((i == 0) & (k == 0))
        def _():
            for cp in scatter():
                cp.start()

        @pl.when((i == ni - 1) & (k == nk - 1))
        def _():
            for cp in scatter():
                cp.wait()

        @pl.when((i == 0) & (k == 0))
        def _():
            dnw_ref[...] = jnp.zeros_like(dnw_ref)

        @pl.when(k == 0)
        def _():
            acc_s[...] = jnp.zeros_like(acc_s)

        for p_ref, s, n in zip(p_refs, starts, counts):
            @pl.when((k >= s) & (k < s + n))
            def _(p_ref=p_ref):
                acc_s[...] += _dot(p_ref[...], w_ref[...])

        @pl.when(k == nk - 1)
        def _():
            for cp in rows():
                cp.wait()
            xv = x_s[...]
            r = lax.rsqrt(jnp.mean(xv * xv, axis=-1, keepdims=True) + EPS)
            xh = xv * r
            dhn = acc_s[...]
            dxh = dhn * nw_ref[...]
            gx_ref[...] = dout_s[...] + r * (dxh - xh * jnp.mean(dxh * xh, axis=-1, keepdims=True))
            dnw_ref[0:1, :] += _colsum(dhn * xh)

    return pl.pallas_call(
        body,
        name="dhn",
        grid=(T // tm, nk),
        in_specs=[pl.BlockSpec((tm, D), lambda i, k, s=s, n=n: (i, jnp.clip(k - s, 0, n - 1)))
                  for s, n in zip(starts, counts)] + [
            pl.BlockSpec((D, D), lambda i, k: (k, 0)),
            ANY, ANY,
            pl.BlockSpec((1, D), lambda i, k: (0, 0)),
            ANY, ANY,
        ],
        out_specs=[pl.BlockSpec((tm, D), lambda i, k: (i, 0)), pl.BlockSpec((8, D), lambda i, k: (0, 0)), ANY, ANY],
        out_shape=[jax.ShapeDtypeStruct((T, D), F32), jax.ShapeDtypeStruct((8, D), F32),
                   jax.ShapeDtypeStruct((N_CHIPS, h_in.shape[0], slab_w), F32),
                   jax.ShapeDtypeStruct((N_CHIPS,) + h_out.shape[1:], F32)],
        scratch_shapes=[pltpu.VMEM((tm, D), F32)] * 3 + [pltpu.SemaphoreType.DMA((n_sem,)), pltpu.SemaphoreType.DMA((n_sem,)),
                                                      pltpu.SemaphoreType.DMA((2,)), pltpu.SemaphoreType.DMA((2,))],
        compiler_params=_params(("arbitrary", "arbitrary")),
    )(*pieces, w_pad_t, x2, dout, norm_w, h_in, h_out)


def _grad_w_in(hn_t, pieces):
    D, T = hn_t.shape
    tk = min(1024, T)
    starts, counts = _piece_blocks(pieces, D)

    def body(*refs):
        a_ref, p_refs, o_ref = refs[0], refs[1:-1], refs[-1]
        j = pl.program_id(0)

        @pl.when(pl.program_id(1) == 0)
        def _():
            o_ref[...] = jnp.zeros_like(o_ref)

        for p_ref, s, n in zip(p_refs, starts, counts):
            @pl.when((j >= s) & (j < s + n))
            def _(p_ref=p_ref):
                o_ref[...] += _dot(a_ref[...], p_ref[...])

    def piece_spec(s, n):
        return pl.BlockSpec((tk, D), lambda j, k: (jnp.where((j >= s) & (j < s + n), k, 0), jnp.clip(j - s, 0, n - 1)))

    return pl.pallas_call(
        body,
        name="grad_w_in",
        grid=(sum(counts), T // tk),
        in_specs=[pl.BlockSpec((D, tk), lambda j, k: (0, k))] + [piece_spec(s, n) for s, n in zip(starts, counts)],
        out_specs=pl.BlockSpec((D, D), lambda j, k: (0, j)),
        out_shape=jax.ShapeDtypeStruct((D, sum(counts) * D), F32),
        compiler_params=_params(("parallel", "arbitrary")),
    )(hn_t, *pieces)


def _matmul(a, b, name):
    M, K = a.shape
    N = b.shape[1]
    tm = min(1024, M)
    tn = 1024 if N % 1024 == 0 else (512 if N % 512 == 0 else N)
    tk = min(512, K)

    def body(a_ref, b_ref, o_ref):
        @pl.when(pl.program_id(2) == 0)
        def _():
            o_ref[...] = jnp.zeros_like(o_ref)

        o_ref[...] += _dot(a_ref[...], b_ref[...])

    return pl.pallas_call(
        body,
        name=name,
        grid=(M // tm, N // tn, K // tk),
        in_specs=[pl.BlockSpec((tm, tk), lambda i, j, k: (i, k)), pl.BlockSpec((tk, tn), lambda i, j, k: (k, j))],
        out_specs=pl.BlockSpec((tm, tn), lambda i, j, k: (i, j)),
        out_shape=jax.ShapeDtypeStruct((M, N), F32),
        compiler_params=_params(("parallel", "parallel", "arbitrary")),
    )(a, b)


def _adamw(w, g, m, v, name):
    R, C = w.shape
    tr = 256 if R % 256 == 0 else R
    c1 = 1.0 - ADAM_B1 ** ADAM_STEP
    c2 = 1.0 - ADAM_B2 ** ADAM_STEP

    def body(w_ref, g_ref, m_ref, v_ref, d_ref, nm_ref, nv_ref):
        gv = g_ref[...]
        m_new = ADAM_B1 * m_ref[...] + (1.0 - ADAM_B1) * gv
        v_new = ADAM_B2 * v_ref[...] + (1.0 - ADAM_B2) * (gv * gv)
        d_ref[...] = -ADAM_LR * ((m_new / c1) / (jnp.sqrt(v_new / c2) + ADAM_EPS) + ADAM_WD * w_ref[...])
        nm_ref[...] = m_new
        nv_ref[...] = v_new

    spec = pl.BlockSpec((tr, C), lambda i: (i, 0))
    return pl.pallas_call(
        body,
        name=name,
        grid=(R // tr,),
        in_specs=[spec] * 4,
        out_specs=[spec] * 3,
        out_shape=[jax.ShapeDtypeStruct((R, C), F32)] * 3,
        compiler_params=_params(("parallel",)),
    )(w, g, m, v)


def _add_core_rows(g, recv, core, name):
    h, width = recv.shape
    th = 128 if h % 128 == 0 else h

    def body(c_ref, g_ref, r_ref, o_ref):
        o_ref[...] = g_ref[...] + r_ref[...]

    return pl.pallas_call(
        body,
        name=name,
        grid_spec=pltpu.PrefetchScalarGridSpec(
            num_scalar_prefetch=1,
            grid=(h // th,),
            in_specs=[
                pl.BlockSpec((th, width), lambda i, c: (c[0] * (h // th) + i, 0)),
                pl.BlockSpec((th, width), lambda i, c: (i, 0)),
            ],
            out_specs=pl.BlockSpec((th, width), lambda i, c: (i, 0)),
        ),
        out_shape=jax.ShapeDtypeStruct((h, width), F32),
        compiler_params=_params(("parallel",)),
    )(core, g, recv)


def _add_core_blocks(g, recv, core, name):
    n, hb, C = recv.shape

    def body(c_ref, g_ref, r_ref, o_ref):
        o_ref[...] = g_ref[...] + r_ref[...]

    return pl.pallas_call(
        body,
        name=name,
        grid_spec=pltpu.PrefetchScalarGridSpec(
            num_scalar_prefetch=1,
            grid=(n,),
            in_specs=[
                pl.BlockSpec((hb, C), lambda p, c: (2 * p + c[0], 0)),
                pl.BlockSpec((None, hb, C), lambda p, c: (p, 0, 0)),
            ],
            out_specs=pl.BlockSpec((None, hb, C), lambda p, c: (p, 0, 0)),
        ),
        out_shape=jax.ShapeDtypeStruct((n, hb, C), F32),
        compiler_params=_params(("parallel",)),
    )(core, g, recv)


def _add_chips(recv, name):
    _, h, W = recv.shape
    th = 256 if h % 256 == 0 else h

    def body(r_ref, o_ref):
        o_ref[...] = ((r_ref[N_CHIPS - 1] + r_ref[0]) + r_ref[1]) + r_ref[2]

    return pl.pallas_call(
        body,
        name=name,
        grid=(h // th,),
        in_specs=[pl.BlockSpec((N_CHIPS, th, W), lambda i: (0, i, 0))],
        out_specs=pl.BlockSpec((th, W), lambda i: (i, 0)),
        out_shape=jax.ShapeDtypeStruct((h, W), F32),
        compiler_params=_params(("parallel",)),
    )(recv)


def _place():
    x, y, c = lax.axis_index("x"), lax.axis_index("y"), lax.axis_index("c")
    other_chips = [(1 - x, y), (x, 1 - y), (1 - x, 1 - y)]
    return x, y, c, other_chips


def _allgather_weights(w_in_bf, w_out_bf, conv_w):
    D, S = w_in_bf.shape
    R = w_out_bf.shape[0]
    n_ici, n_fwd = 3 * (N_CHIPS - 1), 2 * (N_CHIPS - 1)

    def body(win, wout, cw, gin, gout, gcw, send_sems, recv_sems):
        x, y, c, chips = _place()
        me = 2 * x + y
        sibling = (x, y, 1 - c)
        hin, hout = D // 2, R // 2

        def halves(chip_idx):
            return (gin.at[chip_idx, pl.ds(c * hin, hin)], gout.at[chip_idx, pl.ds(c * hout, hout)])

        def rcopy(k, src, dst, to):
            return pltpu.make_async_remote_copy(src_ref=src, dst_ref=dst, send_sem=send_sems.at[k],
                                                recv_sem=recv_sems.at[k], device_id=to, device_id_type=MESH)

        my_in, my_out = halves(me)
        src_in, src_out = win.at[pl.ds(c * hin, hin)], wout.at[pl.ds(c * hout, hout)]
        sends = []
        for j, chip in enumerate(chips):
            to = (*chip, c)
            sends += [rcopy(3 * j, src_in, my_in, to), rcopy(3 * j + 1, src_out, my_out, to),
                      rcopy(3 * j + 2, cw, gcw.at[me], to)]
        for cp in sends:
            cp.start()
        passed = []
        for j, (px, py) in enumerate(chips):
            their_in, their_out = halves(2 * px + py)
            rcopy(3 * j, their_in, their_in, sibling).wait_recv()
            rcopy(3 * j + 1, their_out, their_out, sibling).wait_recv()
            rcopy(3 * j + 2, cw, gcw.at[2 * px + py], sibling).wait_recv()
            fw = [rcopy(n_ici + 2 * j, their_in, their_in, sibling), rcopy(n_ici + 2 * j + 1, their_out, their_out, sibling)]
            for cp in fw:
                cp.start()
            passed += fw
        for j, (px, py) in enumerate(chips):
            oin = gin.at[2 * px + py, pl.ds((1 - c) * hin, hin)]
            oout = gout.at[2 * px + py, pl.ds((1 - c) * hout, hout)]
            rcopy(n_ici + 2 * j, oin, oin, sibling).wait_recv()
            rcopy(n_ici + 2 * j + 1, oout, oout, sibling).wait_recv()
        for cp in sends + passed:
            cp.wait_send()

    return pl.pallas_call(
        body,
        name="allgather_weights",
        in_specs=[ANY, ANY, ANY],
        out_specs=[ANY, ANY, ANY],
        out_shape=[jax.ShapeDtypeStruct((N_CHIPS, D, S), BF16), jax.ShapeDtypeStruct((N_CHIPS, R, D), BF16),
                   jax.ShapeDtypeStruct((N_CHIPS,) + conv_w.shape, F32)],
        scratch_shapes=[pltpu.SemaphoreType.DMA((n_ici + n_fwd,)), pltpu.SemaphoreType.DMA((n_ici + n_fwd,))],
    )(w_in_bf, w_out_bf, conv_w)


def _allreduce_small(packed):
    R = packed.shape[0]
    n_dev = 2 * N_CHIPS

    def body(p_ref, o_ref, buf, send_sems, recv_sems):
        x, y, c, _ = _place()
        me = 4 * x + 2 * y + c
        buf[me] = p_ref[...]
        copies = []
        for k in range(1, n_dev):
            px = 1 - x if k & 4 else x
            py = 1 - y if k & 2 else y
            pc = 1 - c if k & 1 else c
            copies.append((pltpu.make_async_remote_copy(
                src_ref=buf.at[me], dst_ref=buf.at[me], send_sem=send_sems.at[k - 1], recv_sem=recv_sems.at[k - 1],
                device_id=(px, py, pc), device_id_type=MESH), 4 * px + 2 * py + pc, (px, py, pc)))
        for cp, _, _ in copies:
            cp.start()
        for k, (_, peer, to) in enumerate(copies):
            pltpu.make_async_remote_copy(
                src_ref=buf.at[peer], dst_ref=buf.at[peer], send_sem=send_sems.at[k], recv_sem=recv_sems.at[k],
                device_id=to, device_id_type=MESH).wait_recv()
        for cp, _, _ in copies:
            cp.wait_send()
        acc = buf[0]
        for d in range(1, n_dev):
            acc = acc + buf[d]
        o_ref[...] = acc

    vm = pl.BlockSpec(memory_space=pltpu.VMEM)
    return pl.pallas_call(
        body,
        name="allreduce_small",
        in_specs=[vm],
        out_specs=vm,
        out_shape=jax.ShapeDtypeStruct((R, LANES), F32),
        scratch_shapes=[pltpu.VMEM((n_dev, R, LANES), F32), pltpu.SemaphoreType.DMA((n_dev - 1,)),
                        pltpu.SemaphoreType.DMA((n_dev - 1,))],
    )(packed)


def _swap_core_halves(g_in, g_out, width):
    h = g_in.shape[0] // 2
    hb = g_out.shape[0] // (2 * N_CHIPS)

    def body(gin, gout, rin, rout, send_sems, recv_sems):
        x, y, c, _ = _place()

        def rcopy(k, src, dst):
            return pltpu.make_async_remote_copy(src_ref=src, dst_ref=dst, send_sem=send_sems.at[k], recv_sem=recv_sems.at[k],
                                                device_id=(x, y, 1 - c), device_id_type=MESH)

        cps = [rcopy(0, gin.at[pl.ds((1 - c) * h, h), pl.ds(0, width)], rin)]
        cps += [rcopy(1 + p, gout.at[pl.ds((2 * p + 1 - c) * hb, hb)], rout.at[p]) for p in range(N_CHIPS)]
        for cp in cps:
            cp.start()
        for cp in cps:
            cp.wait()

    n = 1 + N_CHIPS
    return pl.pallas_call(
        body,
        name="reduce_core_swap",
        in_specs=[ANY, ANY],
        out_specs=[ANY, ANY],
        out_shape=[jax.ShapeDtypeStruct((h, width), F32), jax.ShapeDtypeStruct((N_CHIPS, hb, g_out.shape[1]), F32)],
        scratch_shapes=[pltpu.SemaphoreType.DMA((n,)), pltpu.SemaphoreType.DMA((n,))],
    )(g_in, g_out)


def _join_core_halves(g_in, g_out):
    def body(gin, gout, fin, fout, send_sems, recv_sems):
        x, y, c, _ = _place()
        cps = [pltpu.make_async_remote_copy(src_ref=s, dst_ref=d.at[c], send_sem=send_sems.at[k],
                                            recv_sem=recv_sems.at[k], device_id=(x, y, 1 - c), device_id_type=MESH)
               for k, (s, d) in enumerate(((gin, fin), (gout, fout)))]
        for cp in cps:
            cp.start()
        for k, (s, d) in enumerate(((gin, fin), (gout, fout))):
            pltpu.make_async_remote_copy(src_ref=s, dst_ref=d.at[1 - c], send_sem=send_sems.at[k],
                                         recv_sem=recv_sems.at[k], device_id=(x, y, 1 - c),
                                         device_id_type=MESH).wait_recv()
        for cp in cps:
            cp.wait_send()

    return pl.pallas_call(
        body,
        name="reduce_core_join",
        in_specs=[ANY, ANY],
        out_specs=[ANY, ANY],
        out_shape=[jax.ShapeDtypeStruct((2,) + g_in.shape, F32), jax.ShapeDtypeStruct((2,) + g_out.shape, F32)],
        scratch_shapes=[pltpu.SemaphoreType.DMA((2,)), pltpu.SemaphoreType.DMA((2,))],
    )(g_in, g_out)


def _pack(arrays):
    rows = []
    for a in arrays:
        flat = a.reshape(-1).astype(F32)
        n = -(-flat.shape[0] // LANES) * LANES
        rows.append(jnp.pad(flat, (0, n - flat.shape[0])).reshape(-1, LANES))
    out = jnp.concatenate(rows, axis=0)
    return jnp.pad(out, ((0, -out.shape[0] % 8), (0, 0)))


def _unpack(packed, shapes):
    out, r = [], 0
    for shp in shapes:
        n = math.prod(shp)
        nr = -(-n // LANES)
        out.append(packed[r:r + nr].reshape(-1)[:n].reshape(shp))
        r += nr
    return out


def _pad_lanes(a):
    return jnp.pad(a, ((0, 0), (0, LANES - a.shape[1])))


def kernel(x, norm_w, w_in, q_norm_w, k_norm_w, conv_w, conv_b, dt_bias, A_log, D_skip, sb_norm_w, ssd_norm_w, w_out, loss_target, m_norm_w, m_w_in, m_q_norm_w, m_k_norm_w, m_conv_w, m_conv_b, m_dt_bias, m_A_log, m_D_skip, m_sb_norm_w, m_ssd_norm_w, m_w_out, v_norm_w, v_w_in, v_q_norm_w, v_k_norm_w, v_conv_w, v_conv_b, v_dt_bias, v_A_log, v_D_skip, v_sb_norm_w, v_ssd_norm_w, v_w_out):
    Bl, L, D = x.shape
    T = Bl * L
    S = w_in.shape[2]
    R = w_out.shape[1]
    CW = conv_w.shape[2]
    n_in = N_CHIPS * S
    CD = D + 2 * SSD_GROUPS * SSD_STATE
    H = D // HEAD_DIM
    n_main = 6 * D + 512
    P = -(-(n_main + LANES) // 1024) * 1024
    assert n_in == n_main + H and CD == N_CHIPS * CW and 2 * D == N_CHIPS * R and CD == D + 512
    chip = (2 * lax.axis_index("x") + lax.axis_index("y")).astype(jnp.int32)
    core = lax.axis_index("c").astype(jnp.int32)

    w_in_bf, w_out_shard_bf = w_in[0].astype(BF16), w_out[0].astype(BF16)
    g_in, g_out, g_cw = _allgather_weights(w_in_bf, w_out_shard_bf, conv_w[0])
    g_in = lax.dynamic_update_slice(g_in, w_in_bf[None], (chip, 0, 0))
    g_out = lax.dynamic_update_slice(g_out, w_out_shard_bf[None], (chip, 0, 0))
    g_cw = lax.dynamic_update_slice(g_cw, conv_w, (chip, 0, 0))
    w_pad = jnp.pad(g_in.transpose(1, 0, 2).reshape(D, n_in), ((0, 0), (0, P - n_in)))
    w_out_bf = g_out.reshape(2 * D, D)
    conv_full = g_cw.transpose(1, 0, 2).reshape(CONV_K, CD)
    cwx, cwb = conv_full[:, :D], conv_full[:, D:]
    cbx, cbb = conv_b[:, :D], conv_b[:, D:]
    dtb, alog, dsk = _pad_lanes(dt_bias), _pad_lanes(A_log), _pad_lanes(D_skip)
    qw2, kw2 = jnp.tile(q_norm_w, (1, 2)), jnp.tile(k_norm_w, (1, 2))

    x2 = x.reshape(T, D)
    proj, hn_t, w_pad_t = _inproj(x2, norm_w, w_pad)
    proj3 = proj.reshape(Bl, L, P)
    o_sb = _attn_fwd(proj3, qw2, kw2, D)
    y_ssd, s_in = _ssd_fwd(proj3, cwx, cwb, cbx, cbb, dtb, alog, dsk, D)
    dout, dout_bf, mixed_t, do_sb, dy_ssd, dz_bf, dnw_out, loss_blk = _gate_out(
        x2, loss_target.reshape(T, D), o_sb.reshape(T, D), proj, y_ssd.reshape(T, D), sb_norm_w, ssd_norm_w, w_out_bf,
        w_out_bf.T)

    dq, dk, dv, dqkw = _attn_bwd(proj3, o_sb, do_sb.reshape(Bl, L, D), qw2, kw2, D)
    dtail, dcwx, dcwb, dcbx, dcbb, misc = _ssd_bwd(
        proj3, s_in, dy_ssd.reshape(Bl, L, D), cwx, cwb, cbx, cbb, dtb, alog, dsk, D, P - 5 * D)
    dproj = [dq.reshape(T, D), dk.reshape(T, D), dv.reshape(T, D), dz_bf, dtail.reshape(T, P - 5 * D)]
    gw_in = _grad_w_in(hn_t, dproj)
    gw_out = _matmul(mixed_t, dout_bf, "grad_w_out")

    slab_off = S // LANES * LANES
    slab_w = -(-(S + (N_CHIPS - 1) * (S - slab_off)) // LANES) * LANES
    width = (N_CHIPS - 1) * slab_off + slab_w
    assert n_in <= width <= P
    core1 = core.reshape(1)
    r_in, r_out = _swap_core_halves(gw_in, gw_out, width)
    h_in = _add_core_rows(gw_in, r_in, core1, "sum_cores_w_in")
    h_out = _add_core_blocks(gw_out, r_out, core1, "sum_cores_w_out")
    grad_x2, dnw_in, s_in_, s_out_ = _dhn(dproj, w_pad_t, x2, dout, norm_w, h_in, h_out, slab_off, slab_w)
    gh_in = _add_chips(s_in_, "sum_chips_w_in")
    gh_out = _add_chips(s_out_, "sum_chips_w_out")
    f_in, f_out = _join_core_halves(gh_in, gh_out)
    g_slab = lax.dynamic_update_slice(f_in, gh_in[None], (core, 0, 0)).reshape(D, slab_w)
    g_w_in = lax.dynamic_slice(g_slab, (0, chip * (S - slab_off)), (D, S))
    g_w_out = lax.dynamic_update_slice(f_out, gh_out[None], (core, 0, 0)).reshape(R, D)

    small_shapes = [(1, D), (1, D), (1, D), (1, CD), (1, HEAD_DIM), (1, HEAD_DIM), (1, H), (1, H), (1, H)]
    g_small_local = [dnw_in[0:1], dnw_out[0:1], dnw_out[1:2], jnp.concatenate([dcbx, dcbb], axis=1),
                     dqkw[0:1, :HEAD_DIM] + dqkw[0:1, HEAD_DIM:], dqkw[1:2, :HEAD_DIM] + dqkw[1:2, HEAD_DIM:],
                     misc[0:1, :H], misc[1:2, :H], misc[2:3, :H]]
    packed = _pack(g_small_local + [jnp.concatenate([dcwx, dcwb], axis=1), loss_blk[0:1, 0:1]])
    red = _allreduce_small(packed)
    g_small = _unpack(red, small_shapes + [(CONV_K, CD), (1, 1)])
    g_conv_w = lax.dynamic_slice_in_dim(g_small[9], chip * CW, CW, axis=1)
    loss = g_small[10][0, 0]

    d_in, nm_in, nv_in = _adamw(w_in[0], g_w_in, m_w_in[0], v_w_in[0], "adamw_w_in")
    d_out, nm_out, nv_out = _adamw(w_out[0], g_w_out, m_w_out[0], v_w_out[0], "adamw_w_out")
    d_cw, nm_cw, nv_cw = _adamw(conv_w[0], g_conv_w, m_conv_w[0], v_conv_w[0], "adamw_conv_w")
    small_w = [norm_w, sb_norm_w, ssd_norm_w, conv_b, q_norm_w, k_norm_w, dt_bias, A_log, D_skip]
    small_m = [m_norm_w, m_sb_norm_w, m_ssd_norm_w, m_conv_b, m_q_norm_w, m_k_norm_w, m_dt_bias, m_A_log, m_D_skip]
    small_v = [v_norm_w, v_sb_norm_w, v_ssd_norm_w, v_conv_b, v_q_norm_w, v_k_norm_w, v_dt_bias, v_A_log, v_D_skip]
    d_s, nm_s, nv_s = _adamw(_pack(small_w), _pack(g_small[:9]), _pack(small_m), _pack(small_v), "adamw_small")
    d_s, nm_s, nv_s = (_unpack(t, small_shapes) for t in (d_s, nm_s, nv_s))

    def ordered(s, w_in_, conv_w_, w_out_):
        return [s[0], w_in_[None], s[4], s[5], conv_w_[None], s[3], s[6], s[7], s[8], s[1], s[2], w_out_[None]]

    return (loss, grad_x2.reshape(Bl, L, D),
            *ordered(g_small[:9], g_w_in, g_conv_w, g_w_out),
            *ordered(d_s, d_in, d_cw, d_out),
            *ordered(nm_s, nm_in, nm_cw, nm_out),
            *ordered(nv_s, nv_in, nv_cw, nv_out))
```

```python
import functools
import math

import jax
import jax.numpy as jnp
from jax import lax
from jax.experimental import pallas as pl
from jax.experimental.pallas import tpu as pltpu

F32 = jnp.float32
BF16 = jnp.bfloat16
EPS = 1e-6
HEAD_DIM = 64
PAIR = 2 * HEAD_DIM
LANES = 128
SSD_STATE = 128
SSD_GROUPS = 2
BLK = 128
PREP_BLOCKS = 4
Q_TOGETHER_FWD = 2
Q_TOGETHER_BWD = 2
FIRST_LEFT = 2
UNDERFLOW = -105.0
CONV_K = 4
HALO = 8
N_CHIPS = 4
ADAM_LR, ADAM_B1, ADAM_B2, ADAM_EPS, ADAM_WD, ADAM_STEP = 0.001, 0.9, 0.999, 1e-08, 0.01, 10
VMEM_LIMIT_V7X = 56 * 1024 * 1024
MESH = pl.DeviceIdType.MESH
ANY = pl.BlockSpec(memory_space=pl.ANY)
NT = (((1,), (1,)), ((), ()))


def _params(sem=None):
    kw = dict(vmem_limit_bytes=VMEM_LIMIT_V7X)
    if sem is not None:
        kw["dimension_semantics"] = sem
    return pltpu.CompilerParams(**kw)


def _dot(a, b):
    return jnp.dot(a, b, preferred_element_type=F32)


def _dot_nt(a, b):
    return lax.dot_general(a, b, NT, preferred_element_type=F32)


def _dot_split(m, x):
    hi = x.astype(BF16)
    lo = (x - hi.astype(F32)).astype(BF16)
    return _dot(m, hi) + _dot(m, lo)


def _iota(shape, dim):
    return lax.broadcasted_iota(jnp.int32, shape, dim)


def _rowsum(x):
    return jnp.sum(x, axis=1, keepdims=True)


def _colsum(x):
    return jnp.sum(x, axis=0, keepdims=True)


def _sigmoid(x):
    return 0.5 * jnp.tanh(0.5 * x) + 0.5


def _softplus(x):
    return jnp.maximum(x, 0.0) + jnp.log(1.0 + jnp.exp(-jnp.abs(x)))


def _inproj(hn, w_pad):
    T, D = hn.shape
    P = w_pad.shape[1]
    tm = min(1024, T)
    tn = 1024 if P % 1024 == 0 else 512
    nj = P // tn

    def body(hn_ref, w_ref, proj_ref, wt_ref):
        @pl.when(pl.program_id(0) == 0)
        def _():
            wt_ref[...] = w_ref[...].astype(F32).T.astype(BF16)

        proj_ref[...] = _dot(hn_ref[...], w_ref[...])

    return pl.pallas_call(
        body,
        name="inproj",
        grid=(T // tm, P // tn),
        in_specs=[
            pl.BlockSpec((tm, D), lambda i, j: (i, 0)),
            pl.BlockSpec((D, tn), lambda i, j: (0, j)),
        ],
        out_specs=[
            pl.BlockSpec((tm, tn), lambda i, j: (i, j)),
            pl.BlockSpec((tn, D), lambda i, j: (jnp.where(i == 0, j, nj - 1), 0)),
        ],
        out_shape=[jax.ShapeDtypeStruct((T, P), F32), jax.ShapeDtypeStruct((P, D), BF16)],
        compiler_params=_params(("arbitrary", "arbitrary")),
    )(hn, w_pad)


def _pair_ones():
    ri = ((_iota((2 * PAIR, PAIR), 0) % PAIR) >= HEAD_DIM).astype(jnp.int32)
    ci = (_iota((2 * PAIR, PAIR), 1) >= HEAD_DIM).astype(jnp.int32)
    return jnp.where(ri == ci, 1.0, 0.0).astype(BF16)


def _pair_rms(v, ones2):
    return lax.rsqrt(_split_dots([v * v], ones2)[0] * (1.0 / HEAD_DIM) + EPS)


def _pair_mean(v, ones2):
    return _split_dots([v], ones2)[0] * (1.0 / HEAD_DIM)


def _suffix_ones():
    ri = _iota((2 * BLK, 2 * BLK), 0) % BLK
    ci = _iota((2 * BLK, 2 * BLK), 1)
    return jnp.where((ci >= BLK) | (ri > ci), 1.0, 0.0).astype(BF16)


def _split_dots(xs, m2):
    his = [x.astype(BF16) for x in xs]
    los = [(x - hi.astype(F32)).astype(BF16) for x, hi in zip(xs, his)]
    return [_dot(jnp.concatenate([hi, lo], axis=1), m2) for hi, lo in zip(his, los)]


def _sb_tiles(streams, km_s, uo):
    tiles = [(s, u, h) for s, st in enumerate(streams) for u in range(len(st["kbs"])) for h in range(2)]
    z2s = {(s, u): _dot_nt(st["q"], km_s[kb]) for s, st in enumerate(streams) for u, kb in enumerate(st["kbs"])}
    zs = [z2s[s, u][:, h * BLK:(h + 1) * BLK] for s, u, h in tiles]
    es = [jnp.exp(-jnp.abs(z)) for z in zs]
    las = [jnp.minimum(z, 0.0) - jnp.log(1.0 + e) for z, e in zip(zs, es)]
    lns = [a - z for a, z in zip(las, zs)]
    masks = [streams[s]["masks"][u] for s, u, h in tiles]
    lks = [lk if m is None else jnp.where(m, lk, 0.0) for m, lk in zip(masks, lns)]
    css = _split_dots(lks, uo)
    rests = [list(st["rest"]) for st in streams]
    ws = []
    for (s, u, h), m, a, cs in zip(tiles, masks, las, css):
        w = jnp.exp(a + rests[s][h] + cs[:, :BLK])
        ws.append(w if m is None else jnp.where(m, w, 0.0))
        rests[s][h] = rests[s][h] + cs[:, BLK:]
    return tiles, las, lns, ws, rests


def _stream(q_pair, qi, n_left, diag, zero):
    return dict(q=q_pair, kbs=[qi - u for u in range(n_left + 1)], masks=[diag] + [None] * n_left, rest=[zero, zero])


def _row0(block):
    return block * BLK if isinstance(block, int) else pl.multiple_of(block * BLK, BLK)


def _pair_of(vals, tiles, s, u):
    return [v for v, t in zip(vals, tiles) if t[0] == s and t[1] == u]


def _block_groups(nq, together):
    n_tog = math.gcd(together, nq)
    assert n_tog >= FIRST_LEFT
    return n_tog, list(range(n_tog)), nq // n_tog


def _attn_prep(src_ref, w_ref, dst_s, n_blocks, scale):
    per = math.gcd(PREP_BLOCKS, n_blocks)
    rows = per * BLK
    lo = _iota((rows, PAIR), 1) < HEAD_DIM
    ones2 = _pair_ones()

    def step(i, carry):
        r0 = pl.multiple_of(i * rows, rows)
        v = src_ref[0, pl.ds(r0, rows), :]
        if w_ref is not None:
            v = v * _pair_rms(v, ones2) * w_ref[...]
        if scale != 1.0:
            v = v * scale
        v0, v1 = jnp.where(lo, v, 0.0).astype(BF16), jnp.where(lo, 0.0, v).astype(BF16)
        for b in range(per):
            dst_s[i * per + b, 0:BLK, :] = v0[b * BLK:(b + 1) * BLK]
            dst_s[i * per + b, BLK:2 * BLK, :] = v1[b * BLK:(b + 1) * BLK]
        return carry

    lax.fori_loop(0, n_blocks // per, step, 0)


def _attn_fwd(proj3, qw2, kw2, D):
    Bl, L, _ = proj3.shape
    n_pair = D // PAIR
    nq = L // BLK
    scale = 1.0 / math.sqrt(HEAD_DIM)

    def body(q_ref, k_ref, v_ref, qw_ref, kw_ref, o_ref, qm_s, km_s, vm_s):
        uo = _suffix_ones()
        diag = _iota((BLK, BLK), 1) < _iota((BLK, BLK), 0)
        _attn_prep(q_ref, qw_ref, qm_s, nq, scale)
        _attn_prep(k_ref, kw_ref, km_s, nq, 1.0)
        _attn_prep(v_ref, None, vm_s, nq, 1.0)

        zero_c = jnp.zeros((BLK, BLK), F32)

        def q_of(qi):
            return qm_s[qi, 0:BLK, :] + qm_s[qi, BLK:2 * BLK, :]

        def values(streams, accs):
            tiles, _, _, ws, rests = _sb_tiles(streams, km_s, uo)
            wbs = [w.astype(BF16) for w in ws]
            accs = list(accs)
            for s, st in enumerate(streams):
                for u, kb in enumerate(st["kbs"]):
                    accs[s] = accs[s] + _dot(jnp.concatenate(_pair_of(wbs, tiles, s, u), axis=1), vm_s[kb])
            return accs, rests

        def group(qis, n_lefts):
            streams = [_stream(q_of(qi), qi, n, diag, zero_c) for qi, n in zip(qis, n_lefts)]
            accs, rests = values(streams, [jnp.zeros((BLK, PAIR), F32)] * len(qis))
            for qi, n, q, acc, rc in zip(qis, n_lefts, [st["q"] for st in streams], accs, rests):

                def sweep(state, n_blocks, q=q):
                    kb, rc0, rc1, acc1, _ = state
                    st = dict(q=q, kbs=[kb - u for u in range(n_blocks)], masks=[None] * n_blocks, rest=[rc0, rc1])
                    (acc1,), (r,) = values([st], [acc1])
                    return kb - n_blocks, r[0], r[1], acc1, jnp.maximum(jnp.max(r[0]), jnp.max(r[1]))

                state = (jnp.asarray(qi - n - 1, jnp.int32), rc[0], rc[1], acc, jnp.maximum(jnp.max(rc[0]), jnp.max(rc[1])))
                state = lax.while_loop(lambda t: (t[0] >= 1) & (t[4] >= UNDERFLOW), lambda t: sweep(t, 2), state)
                state = lax.while_loop(lambda t: (t[0] >= 0) & (t[4] >= UNDERFLOW), lambda t: sweep(t, 1), state)
                o_ref[0, pl.ds(_row0(qi), BLK), :] = state[3]

        n_tog, head, n_groups = _block_groups(nq, Q_TOGETHER_FWD)
        group(head, [min(qi, FIRST_LEFT) for qi in head])

        def groups(g, carry):
            group([g * n_tog + j for j in range(n_tog)], [FIRST_LEFT] * n_tog)
            return carry

        lax.fori_loop(1, n_groups, groups, 0)

    blk = lambda off: pl.BlockSpec((1, L, PAIR), lambda b, p: (b, 0, off + p))
    wspec = pl.BlockSpec((1, PAIR), lambda b, p: (0, 0))
    return pl.pallas_call(
        body,
        name="sb_attn_fwd",
        grid=(Bl, n_pair),
        in_specs=[blk(0), blk(n_pair), blk(2 * n_pair), wspec, wspec],
        out_specs=pl.BlockSpec((1, L, PAIR), lambda b, p: (b, 0, p)),
        out_shape=jax.ShapeDtypeStruct((Bl, L, D), F32),
        scratch_shapes=[pltpu.VMEM((nq, 2 * BLK, PAIR), BF16)] * 3,
        compiler_params=_params(("parallel", "parallel")),
    )(proj3, proj3, proj3, qw2, kw2)


def _attn_bwd(proj3, o3, do3, qw2, kw2, D):
    Bl, L, _ = proj3.shape
    n_pair = D // PAIR
    nq = L // BLK
    scale = 1.0 / math.sqrt(HEAD_DIM)

    def body(q_ref, k_ref, v_ref, o_ref, do_ref, qw_ref, kw_ref, dq_ref, dk_ref, dv_ref, dw_ref,
             qm_s, km_s, vm_s, dom_s, dq_s, dk_s, dv_s):
        uo = _suffix_ones()
        diag = _iota((BLK, BLK), 1) < _iota((BLK, BLK), 0)
        ones2 = _pair_ones()
        _attn_prep(q_ref, qw_ref, qm_s, nq, scale)
        _attn_prep(k_ref, kw_ref, km_s, nq, 1.0)
        _attn_prep(v_ref, None, vm_s, nq, 1.0)
        _attn_prep(do_ref, None, dom_s, nq, 1.0)

        @pl.when((pl.program_id(0) == 0) & (pl.program_id(1) == 0))
        def _():
            dw_ref[...] = jnp.zeros_like(dw_ref)

        def zero(i, carry):
            r0 = pl.multiple_of(i * BLK, BLK)
            dk_s[pl.ds(r0, BLK), :] = jnp.zeros((BLK, PAIR), F32)
            dv_s[pl.ds(r0, BLK), :] = jnp.zeros((BLK, PAIR), F32)
            return carry

        lax.fori_loop(0, nq, zero, 0)

        zero_c = jnp.zeros((BLK, BLK), F32)

        def tiles_bwd(streams, dqas):
            tiles, las, lns, ws, rests = _sb_tiles(streams, km_s, uo)
            dw2s = {(s, u): _dot_nt(st["do"], vm_s[kb]) for s, st in enumerate(streams) for u, kb in enumerate(st["kbs"])}
            dws = [dw2s[s, u][:, h * BLK:(h + 1) * BLK] for s, u, h in tiles]
            wfs = [w.astype(BF16).astype(F32) for w in ws]
            gs = [wf * dw for wf, dw in zip(wfs, dws)]
            gss = _split_dots(gs, uo)
            gcs = [list(st["g_rest"]) for st in streams]
            dzs = []
            for (s, u, h), a, ln, g, gsum in zip(tiles, las, lns, gs, gss):
                g_before = streams[s]["delta"][h] - (gcs[s][h] + gsum[:, :BLK] + g)
                gcs[s][h] = gcs[s][h] + gsum[:, BLK:]
                dz = g * jnp.exp(ln) - g_before * jnp.exp(a)
                m = streams[s]["masks"][u]
                dzs.append(dz if m is None else jnp.where(m, dz, 0.0))
            wts = [wf.T.astype(BF16) for wf in wfs]
            dzts = [dz.T.astype(BF16) for dz in dzs]
            dzbs = [dz.astype(BF16) for dz in dzs]
            dqas = list(dqas)
            for s, st in enumerate(streams):
                for u, kb in enumerate(st["kbs"]):
                    c0 = _row0(kb)
                    dv_s[pl.ds(c0, BLK), :] += _dot(jnp.concatenate(_pair_of(wts, tiles, s, u), axis=1), dom_s[st["qi"]])
                    dk_s[pl.ds(c0, BLK), :] += _dot(jnp.concatenate(_pair_of(dzts, tiles, s, u), axis=1), qm_s[st["qi"]])
                    dqas[s] = dqas[s] + _dot(jnp.concatenate(_pair_of(dzbs, tiles, s, u), axis=1), km_s[kb])
            return dqas, rests, gcs

        def group(qis, n_lefts):
            streams = []
            for qi, n in zip(qis, n_lefts):
                o_blk = o_ref[0, pl.ds(_row0(qi), BLK), :]
                doms = [dom_s[qi, 0:BLK, :], dom_s[qi, BLK:2 * BLK, :]]
                st = _stream(qm_s[qi, 0:BLK, :] + qm_s[qi, BLK:2 * BLK, :], qi, n, diag, zero_c)
                st.update(qi=qi, do=doms[0] + doms[1], delta=[_rowsum(d.astype(F32) * o_blk) for d in doms],
                          g_rest=[zero_c, zero_c])
                streams.append(st)
            dqas, rests, gcs = tiles_bwd(streams, [jnp.zeros((BLK, PAIR), F32)] * len(qis))
            for qi, n, st0, dqa, rc, gc in zip(qis, n_lefts, streams, dqas, rests, gcs):

                def sweep(state, n_blocks, st0=st0):
                    kb, rc0, rc1, gc0, gc1, dqa1, _ = state
                    st = dict(st0, kbs=[kb - u for u in range(n_blocks)], masks=[None] * n_blocks, rest=[rc0, rc1],
                              g_rest=[gc0, gc1])
                    (dqa1,), (r,), (g,) = tiles_bwd([st], [dqa1])
                    return kb - n_blocks, r[0], r[1], g[0], g[1], dqa1, jnp.maximum(jnp.max(r[0]), jnp.max(r[1]))

                state = (jnp.asarray(qi - n - 1, jnp.int32), rc[0], rc[1], gc[0], gc[1], dqa,
                         jnp.maximum(jnp.max(rc[0]), jnp.max(rc[1])))
                state = lax.while_loop(lambda t: (t[0] >= 1) & (t[6] >= UNDERFLOW), lambda t: sweep(t, 2), state)
                state = lax.while_loop(lambda t: (t[0] >= 0) & (t[6] >= UNDERFLOW), lambda t: sweep(t, 1), state)
                dq_s[pl.ds(_row0(qi), BLK), :] = state[5] * scale

        n_tog, head, n_groups = _block_groups(nq, Q_TOGETHER_BWD)
        group(head, [min(qi, FIRST_LEFT) for qi in head])

        def groups(g, carry):
            group([g * n_tog + j for j in range(n_tog)], [FIRST_LEFT] * n_tog)
            return carry

        lax.fori_loop(1, n_groups, groups, 0)

        per = math.gcd(PREP_BLOCKS, nq)
        rows = per * BLK

        def finish(i, carry):
            r0 = pl.multiple_of(i * rows, rows)
            dwq, dwk = carry
            out = []
            for src_ref, w_ref, d_s in ((q_ref, qw_ref, dq_s), (k_ref, kw_ref, dk_s)):
                v = src_ref[0, pl.ds(r0, rows), :]
                r = _pair_rms(v, ones2)
                vh = v * r
                dy = d_s[pl.ds(r0, rows), :]
                dvh = dy * w_ref[...]
                out.append((r * (dvh - vh * _pair_mean(dvh * vh, ones2)), _colsum(dy * vh)))
            dq_ref[0, pl.ds(r0, rows), :] = out[0][0].astype(BF16)
            dk_ref[0, pl.ds(r0, rows), :] = out[1][0].astype(BF16)
            dv_ref[0, pl.ds(r0, rows), :] = dv_s[pl.ds(r0, rows), :].astype(BF16)
            return dwq + out[0][1], dwk + out[1][1]

        zrow = jnp.zeros((1, PAIR), F32)
        dwq, dwk = lax.fori_loop(0, nq // per, finish, (zrow, zrow))
        dw_ref[0:1, :] += dwq
        dw_ref[1:2, :] += dwk

    blk = lambda off: pl.BlockSpec((1, L, PAIR), lambda b, p: (b, 0, off + p))
    wspec = pl.BlockSpec((1, PAIR), lambda b, p: (0, 0))
    oblk = pl.BlockSpec((1, L, PAIR), lambda b, p: (b, 0, p))
    return pl.pallas_call(
        body,
        name="sb_attn_bwd",
        grid=(Bl, n_pair),
        in_specs=[blk(0), blk(n_pair), blk(2 * n_pair), oblk, oblk, wspec, wspec],
        out_specs=[oblk, oblk, oblk, pl.BlockSpec((8, PAIR), lambda b, p: (0, 0))],
        out_shape=[jax.ShapeDtypeStruct((Bl, L, D), BF16)] * 3 + [jax.ShapeDtypeStruct((8, PAIR), F32)],
        scratch_shapes=[pltpu.VMEM((nq, 2 * BLK, PAIR), BF16)] * 4 + [pltpu.VMEM((L, PAIR), F32)] * 3,
        compiler_params=_params(("arbitrary", "arbitrary")),
    )(proj3, proj3, proj3, o3, do3, qw2, kw2)


def _conv_pre(ext_s, halo_ref, raw_ref, w_ref, b_ref, first):
    ext_s[0:HALO, :] = jnp.where(first, 0.0, halo_ref[0])
    ext_s[HALO:HALO + BLK, :] = raw_ref[0]
    pre = b_ref[...]
    for i in range(CONV_K):
        pre = pre + ext_s[pl.ds(HALO - (CONV_K - 1 - i), BLK), :] * w_ref[i:i + 1, :]
    return pre


def _lane_col(m, lane, h):
    return _rowsum(jnp.where(lane == h, m, 0.0))


def _half_sums(row, lo1):
    return _rowsum(jnp.where(lo1, row, 0.0)), _rowsum(jnp.where(lo1, 0.0, row))


def _ssd_specs(Bl, L, D, rev):
    nc = L // BLK
    rows_per = BLK // HALO
    cidx = (lambda c: nc - 1 - c) if rev else (lambda c: c)
    xoff = 5
    boff = (6 * D) // 512
    doff = (6 * D + 512) // LANES
    prev = lambda c: jnp.maximum(cidx(c) * rows_per - 1, 0)
    specs = [
        pl.BlockSpec((1, BLK, D), lambda b, c: (b, cidx(c), xoff)),
        pl.BlockSpec((1, BLK, 512), lambda b, c: (b, cidx(c), boff)),
        pl.BlockSpec((1, HALO, D), lambda b, c: (b, prev(c), xoff)),
        pl.BlockSpec((1, HALO, 512), lambda b, c: (b, prev(c), boff)),
        pl.BlockSpec((1, BLK, LANES), lambda b, c: (b, cidx(c), doff)),
    ]
    full = lambda shape: pl.BlockSpec(shape, lambda b, c: (0,) * len(shape))
    specs += [full((CONV_K, D)), full((CONV_K, 512)), full((1, D)), full((1, 512)),
              full((1, LANES)), full((1, LANES)), full((1, LANES))]
    return specs, cidx


def _ssd_common(dtr_ref, dtb_ref, alog_ref, acs_s, acsT_s):
    ltri = jnp.where(_iota((BLK, BLK), 1) <= _iota((BLK, BLK), 0), 1.0, 0.0).astype(BF16)
    dtv = _softplus(dtr_ref[0] + dtb_ref[...])
    a = -jnp.exp(alog_ref[...])
    acs = _dot_split(ltri, dtv * a)
    acs_s[...] = acs
    acsT_s[...] = acs.T
    return dtv, a, acs


def _pair_terms(pr, acs, dtv, acs_s, lane, lo, lane1, lo1):
    h0, h1 = 2 * pr, 2 * pr + 1
    c0, c1 = _lane_col(acs, lane, h0), _lane_col(acs, lane, h1)
    d0, d1 = _lane_col(dtv, lane, h0), _lane_col(dtv, lane, h1)
    lastv = acs_s[BLK - 1:BLK, :]
    l0, l1 = _lane_col(lastv, lane1, h0), _lane_col(lastv, lane1, h1)
    return dict(h=(h0, h1), c=(c0, c1), last=(l0, l1), acs_p=jnp.where(lo, c0, c1), dt_p=jnp.where(lo, d0, d1),
                last_p=jnp.where(lo1, l0, l1))


def _decay_tiles(cc, row, tri, want_t):
    lm = jnp.where(tri, jnp.exp(cc - row), 0.0)
    return lm, (lm.T if want_t else None)


def _ssd_fwd(proj3, cwx, cwb, cbx, cbb, dtb, alog, dsk, D):
    Bl, L, _ = proj3.shape
    nc = L // BLK
    n_pair = D // PAIR
    pairs_per_group = n_pair // SSD_GROUPS
    specs, _ = _ssd_specs(Bl, L, D, False)

    def body(xr_ref, bcr_ref, xh_ref, bch_ref, dtr_ref, cwx_ref, cwb_ref, cbx_ref, cbb_ref, dtb_ref, alog_ref,
             dsk_ref, y_ref, sin_ref, st_s, extx_s, extb_s, acs_s, acsT_s):
        first = pl.program_id(1) == 0

        @pl.when(first)
        def _():
            st_s[...] = jnp.zeros_like(st_s)

        lane, lane1 = _iota((BLK, LANES), 1), _iota((1, LANES), 1)
        lo, lo1 = lane < HEAD_DIM, lane1 < HEAD_DIM
        tri = _iota((BLK, BLK), 1) <= _iota((BLK, BLK), 0)
        pre = _conv_pre(extx_s, xh_ref, xr_ref, cwx_ref, cbx_ref, first)
        ux = pre * _sigmoid(pre)
        pre = _conv_pre(extb_s, bch_ref, bcr_ref, cwb_ref, cbb_ref, first)
        ub = pre * _sigmoid(pre)
        dtv, a, acs = _ssd_common(dtr_ref, dtb_ref, alog_ref, acs_s, acsT_s)
        for g in range(SSD_GROUPS):
            bg = ub[:, g * SSD_STATE:(g + 1) * SSD_STATE]
            cb_ = ub[:, (SSD_GROUPS + g) * SSD_STATE:(SSD_GROUPS + g + 1) * SSD_STATE].astype(BF16)
            cbm = _dot_nt(cb_, bg.astype(BF16))
            btb = bg.T.astype(BF16)
            for pr in range(g * pairs_per_group, (g + 1) * pairs_per_group):
                t = _pair_terms(pr, acs, dtv, acs_s, lane, lo, lane1, lo1)
                xs_p = ux[:, pr * PAIR:(pr + 1) * PAIR]
                x_p = xs_p * t["dt_p"]
                st = st_s[pr]
                sin_ref[0, 0, pr] = st
                y = _dot(cb_, st.astype(BF16)) * jnp.exp(t["acs_p"])
                for k in range(2):
                    row = acsT_s[t["h"][k]:t["h"][k] + 1, :]
                    lm, _ = _decay_tiles(t["c"][k], row, tri, False)
                    xm = jnp.where(lo if k == 0 else ~lo, x_p, 0.0).astype(BF16)
                    y = y + _dot((cbm * lm).astype(BF16), xm)
                d_p = jnp.where(lo1, _lane_col(dsk_ref[...], lane1, t["h"][0]), _lane_col(dsk_ref[...], lane1, t["h"][1]))
                y_ref[0, :, pr * PAIR:(pr + 1) * PAIR] = y + d_p * xs_p
                xd = (x_p * jnp.exp(t["last_p"] - t["acs_p"])).astype(BF16)
                st_s[pr] = st * jnp.exp(t["last_p"]) + _dot(btb, xd)

    return pl.pallas_call(
        body,
        name="ssd_fwd",
        grid=(Bl, nc),
        in_specs=specs,
        out_specs=[
            pl.BlockSpec((1, BLK, D), lambda b, c: (b, c, 0)),
            pl.BlockSpec((1, 1, n_pair, SSD_STATE, PAIR), lambda b, c: (b, c, 0, 0, 0)),
        ],
        out_shape=[jax.ShapeDtypeStruct((Bl, L, D), F32),
                   jax.ShapeDtypeStruct((Bl, nc, n_pair, SSD_STATE, PAIR), F32)],
        scratch_shapes=[pltpu.VMEM((n_pair, SSD_STATE, PAIR), F32), pltpu.VMEM((HALO + BLK, D), F32),
                        pltpu.VMEM((HALO + BLK, 512), F32), pltpu.VMEM((BLK, LANES), F32),
                        pltpu.VMEM((LANES, BLK), F32)],
        compiler_params=_params(("arbitrary", "arbitrary")),
    )(proj3, proj3, proj3, proj3, proj3, cwx, cwb, cbx, cbb, dtb, alog, dsk)


def _ssd_bwd(proj3, s_in, dy3, cwx, cwb, cbx, cbb, dtb, alog, dsk, D, tail):
    Bl, L, _ = proj3.shape
    CD = D + 512
    nc = L // BLK
    n_pair = D // PAIR
    n_heads = 2 * n_pair
    pairs_per_group = n_pair // SSD_GROUPS
    specs, cidx = _ssd_specs(Bl, L, D, True)
    specs = specs + [
        pl.BlockSpec((1, 1, n_pair, SSD_STATE, PAIR), lambda b, c: (b, cidx(c), 0, 0, 0)),
        pl.BlockSpec((1, BLK, D), lambda b, c: (b, cidx(c), 0)),
    ]

    def body(xr_ref, bcr_ref, xh_ref, bch_ref, dtr_ref, cwx_ref, cwb_ref, cbx_ref, cbb_ref, dtb_ref, alog_ref,
             dsk_ref, sin_ref, dy_ref, dxbc_ref, dcwx_ref, dcwb_ref, dcbx_ref, dcbb_ref, misc_ref,
             dst_s, extx_s, extb_s, acs_s, acsT_s, dux_s, dub_s, e2x_s, e2b_s, nxx_s, nxb_s):
        step = pl.program_id(1)
        first = step == nc - 1
        last = step == 0

        @pl.when(last)
        def _():
            dst_s[...] = jnp.zeros_like(dst_s)
            nxx_s[...] = jnp.zeros_like(nxx_s)
            nxb_s[...] = jnp.zeros_like(nxb_s)

        @pl.when(last & (pl.program_id(0) == 0))
        def _():
            for r in (dcwx_ref, dcwb_ref, dcbx_ref, dcbb_ref, misc_ref):
                r[...] = jnp.zeros_like(r)

        lane, lane1 = _iota((BLK, LANES), 1), _iota((1, LANES), 1)
        lo, lo1 = lane < HEAD_DIM, lane1 < HEAD_DIM
        tri = _iota((BLK, BLK), 1) <= _iota((BLK, BLK), 0)
        prex = _conv_pre(extx_s, xh_ref, xr_ref, cwx_ref, cbx_ref, first)
        sgx = _sigmoid(prex)
        ux = prex * sgx
        preb = _conv_pre(extb_s, bch_ref, bcr_ref, cwb_ref, cbb_ref, first)
        sgb = _sigmoid(preb)
        ub = preb * sgb
        dtv, a, acs = _ssd_common(dtr_ref, dtb_ref, alog_ref, acs_s, acsT_s)
        dacs = jnp.zeros((BLK, LANES), F32)
        dlast = jnp.zeros((1, LANES), F32)
        ddt = jnp.zeros((BLK, LANES), F32)
        dd = jnp.zeros((1, LANES), F32)
        for g in range(SSD_GROUPS):
            bg = ub[:, g * SSD_STATE:(g + 1) * SSD_STATE]
            cg = ub[:, (SSD_GROUPS + g) * SSD_STATE:(SSD_GROUPS + g + 1) * SSD_STATE]
            bb, cb_ = bg.astype(BF16), cg.astype(BF16)
            cbm = _dot_nt(cb_, bb)
            cbt = _dot_nt(bb, cb_)
            ctb = cg.T.astype(BF16)
            dbg = jnp.zeros((BLK, SSD_STATE), F32)
            dcg = jnp.zeros((BLK, SSD_STATE), F32)
            for pr in range(g * pairs_per_group, (g + 1) * pairs_per_group):
                t = _pair_terms(pr, acs, dtv, acs_s, lane, lo, lane1, lo1)
                h0, h1 = t["h"]
                xs_p = ux[:, pr * PAIR:(pr + 1) * PAIR]
                dy_p = dy_ref[0, :, pr * PAIR:(pr + 1) * PAIR]
                x_p = xs_p * t["dt_p"]
                ea_p = jnp.exp(t["acs_p"])
                dte_p = jnp.exp(t["last_p"] - t["acs_p"])
                cd_p = jnp.exp(t["last_p"])
                st = sin_ref[0, 0, pr]
                dst = dst_s[pr]
                stb, dstb = st.astype(BF16), dst.astype(BF16)
                s0, s1 = _half_sums(_colsum(dy_p * xs_p), lo1)
                dd = dd + jnp.where(lane1 == h0, s0, 0.0) + jnp.where(lane1 == h1, s1, 0.0)
                d_p = jnp.where(lo1, _lane_col(dsk_ref[...], lane1, h0), _lane_col(dsk_ref[...], lane1, h1))
                dxs_p = d_p * dy_p
                dp = dy_p * ea_p
                dpb = dp.astype(BF16)
                yo = dp * _dot(cb_, stb)
                dcg = dcg + _dot_nt(dpb, stb)
                dst_off = _dot(ctb, dpb)
                dac = [_rowsum(jnp.where(lo, yo, 0.0)), _rowsum(jnp.where(lo, 0.0, yo))]
                s0, s1 = _half_sums(_colsum(dst * st), lo1)
                dl = [s0 * jnp.exp(t["last"][0]), s1 * jnp.exp(t["last"][1])]
                dxd = _dot(bb, dstb)
                dx_p = dxd * dte_p
                tt = dxd * x_p
                dbg = dbg + _dot_nt((x_p * dte_p).astype(BF16), dstb)
                for k, ddte in enumerate((_rowsum(jnp.where(lo, tt, 0.0)), _rowsum(jnp.where(lo, 0.0, tt)))):
                    ek = ddte * jnp.exp(t["last"][k] - t["c"][k])
                    dl[k] = dl[k] + _colsum(ek)
                    dac[k] = dac[k] - ek
                x_pb = x_p.astype(BF16)
                for k in range(2):
                    row = acsT_s[t["h"][k]:t["h"][k] + 1, :]
                    lm, lmt = _decay_tiles(t["c"][k], row, tri, True)
                    dym = jnp.where(lo if k == 0 else ~lo, dy_p, 0.0).astype(BF16)
                    dm = _dot_nt(dym, x_pb)
                    dmt = _dot_nt(x_pb, dym)
                    mt = cbt * lmt
                    dx_p = dx_p + _dot(mt.astype(BF16), dym)
                    dac[k] = dac[k] + _rowsum(dm * (cbm * lm)) - _rowsum(dmt * mt)
                    dcg = dcg + _dot((dm * lm).astype(BF16), bb)
                    dbg = dbg + _dot((dmt * lmt).astype(BF16), cb_)
                dacs = dacs + jnp.where(lane == h0, dac[0], 0.0) + jnp.where(lane == h1, dac[1], 0.0)
                dlast = dlast + jnp.where(lane1 == h0, dl[0], 0.0) + jnp.where(lane1 == h1, dl[1], 0.0)
                dxs_p = dxs_p + dx_p * t["dt_p"]
                t3 = dx_p * xs_p
                ddt = ddt + jnp.where(lane == h0, _rowsum(jnp.where(lo, t3, 0.0)), 0.0) \
                    + jnp.where(lane == h1, _rowsum(jnp.where(lo, 0.0, t3)), 0.0)
                dux_s[:, pr * PAIR:(pr + 1) * PAIR] = dxs_p
                dst_s[pr] = dst * cd_p + dst_off
            dub_s[:, g * SSD_STATE:(g + 1) * SSD_STATE] = dbg
            dub_s[:, (SSD_GROUPS + g) * SSD_STATE:(SSD_GROUPS + g + 1) * SSD_STATE] = dcg
        dacs = dacs + jnp.where(_iota((BLK, LANES), 0) == BLK - 1, dlast, 0.0)
        utri = jnp.where(_iota((BLK, BLK), 1) >= _iota((BLK, BLK), 0), 1.0, 0.0).astype(BF16)
        dda = _dot_split(utri, dacs)
        ddt = ddt + dda * a
        ddtr = jnp.where(lane < n_heads, ddt * _sigmoid(dtr_ref[0] + dtb_ref[...]), 0.0)
        dxbc_ref[0, :, CD:CD + LANES] = ddtr.astype(BF16)
        dxbc_ref[0, :, CD + LANES:tail] = jnp.zeros((BLK, tail - CD - LANES), BF16)
        misc_ref[0:1, :] += _colsum(ddtr)
        misc_ref[1:2, :] += jnp.where(lane1 < n_heads, _colsum(dda * dtv) * a, 0.0)
        misc_ref[2:3, :] += dd
        for (du_s, pre, sg, ext_s, e2_s, nx_s, w_ref, dcw_ref, dcb_ref, c0, width) in (
                (dux_s, prex, sgx, extx_s, e2x_s, nxx_s, cwx_ref, dcwx_ref, dcbx_ref, 0, D),
                (dub_s, preb, sgb, extb_s, e2b_s, nxb_s, cwb_ref, dcwb_ref, dcbb_ref, D, 512)):
            dpre = du_s[...] * (sg * (1.0 + pre * (1.0 - sg)))
            dcb_ref[...] += _colsum(dpre)
            for i in range(CONV_K):
                dcw_ref[i:i + 1, :] += _colsum(dpre * ext_s[pl.ds(HALO - (CONV_K - 1 - i), BLK), :])
            e2_s[0:BLK, :] = dpre
            e2_s[BLK:BLK + HALO, :] = nx_s[...]
            dxr = jnp.zeros((BLK, width), F32)
            for i in range(CONV_K):
                dxr = dxr + e2_s[pl.ds(CONV_K - 1 - i, BLK), :] * w_ref[i:i + 1, :]
            dxbc_ref[0, :, c0:c0 + width] = dxr.astype(BF16)
            nx_s[...] = e2_s[0:HALO, :]

    full = lambda shape: pl.BlockSpec(shape, lambda b, c: (0,) * len(shape))
    return pl.pallas_call(
        body,
        name="ssd_bwd",
        grid=(Bl, nc),
        in_specs=specs,
        out_specs=[
            pl.BlockSpec((1, BLK, tail), lambda b, c: (b, cidx(c), 0)),
            full((CONV_K, D)), full((CONV_K, 512)), full((1, D)), full((1, 512)), full((8, LANES)),
        ],
        out_shape=[
            jax.ShapeDtypeStruct((Bl, L, tail), BF16),
            jax.ShapeDtypeStruct((CONV_K, D), F32), jax.ShapeDtypeStruct((CONV_K, 512), F32),
            jax.ShapeDtypeStruct((1, D), F32), jax.ShapeDtypeStruct((1, 512), F32),
            jax.ShapeDtypeStruct((8, LANES), F32),
        ],
        scratch_shapes=[
            pltpu.VMEM((n_pair, SSD_STATE, PAIR), F32),
            pltpu.VMEM((HALO + BLK, D), F32), pltpu.VMEM((HALO + BLK, 512), F32),
            pltpu.VMEM((BLK, LANES), F32), pltpu.VMEM((LANES, BLK), F32),
            pltpu.VMEM((BLK, D), F32), pltpu.VMEM((BLK, 512), F32),
            pltpu.VMEM((BLK + HALO, D), F32), pltpu.VMEM((BLK + HALO, 512), F32),
            pltpu.VMEM((HALO, D), F32), pltpu.VMEM((HALO, 512), F32),
        ],
        compiler_params=_params(("arbitrary", "arbitrary")),
    )(proj3, proj3, proj3, proj3, proj3, cwx, cwb, cbx, cbb, dtb, alog, dsk, s_in, dy3)


def _gate_out(x2, tgt2, o2, proj2, y2, sbw, ssw, w_out_bf, w_out_t):
    T, D = x2.shape
    tm = min(256, T)

    def body(x_ref, t_ref, o_ref, zs_ref, y_ref, zy_ref, sbw_ref, ssw_ref, wo_ref, wot_ref,
             dout_ref, doutb_ref, mixt_ref, do_ref, dy_ref, dz_ref, dnw_ref, loss_ref):
        @pl.when(pl.program_id(0) == 0)
        def _():
            dnw_ref[...] = jnp.zeros_like(dnw_ref)
            loss_ref[...] = jnp.zeros_like(loss_ref)

        def fwd(o, z, w):
            sg = _sigmoid(z)
            sl = z * sg
            g = o * sl
            r = lax.rsqrt(jnp.mean(g * g, axis=-1, keepdims=True) + EPS)
            n = g * r
            return sg, sl, r, n, n * w

        def bwd(dy, o, z, w, sg, sl, r, n):
            dn = dy * w
            dg = r * (dn - n * jnp.mean(dn * n, axis=-1, keepdims=True))
            return dg * sl, dg * o * (sg * (1.0 + z * (1.0 - sg))), _colsum(dy * n)

        o1, z1, w1 = o_ref[...], zs_ref[...], sbw_ref[...]
        o2_, z2, w2 = y_ref[...], zy_ref[...], ssw_ref[...]
        sg1, sl1, r1, n1, y1 = fwd(o1, z1, w1)
        sg2, sl2, r2, n2, y2_ = fwd(o2_, z2, w2)
        y1b, y2b = y1.astype(BF16), y2_.astype(BF16)
        mixt_ref[0:D, :] = y1.T.astype(BF16)
        mixt_ref[D:2 * D, :] = y2_.T.astype(BF16)
        out = x_ref[...] + (_dot(y1b, wo_ref[0:D, :]) + _dot(y2b, wo_ref[D:2 * D, :]))
        err = out - t_ref[...]
        loss_ref[...] += jnp.sum(err * err) * (0.5 / D)
        dout = err * (1.0 / D)
        dout_ref[...] = dout
        doutb = dout.astype(BF16)
        doutb_ref[...] = doutb
        do1, dz1, dw1 = bwd(_dot(doutb, wot_ref[:, 0:D]), o1, z1, w1, sg1, sl1, r1, n1)
        do2, dz2, dw2 = bwd(_dot(doutb, wot_ref[:, D:2 * D]), o2_, z2, w2, sg2, sl2, r2, n2)
        do_ref[...] = do1
        dy_ref[...] = do2
        dz_ref[:, 0:D] = dz1.astype(BF16)
        dz_ref[:, D:2 * D] = dz2.astype(BF16)
        dnw_ref[0:1, :] += dw1
        dnw_ref[1:2, :] += dw2

    row = lambda col: pl.BlockSpec((tm, D), lambda i: (i, col))
    full = lambda shape: pl.BlockSpec(shape, lambda i: (0,) * len(shape))
    wide = pl.BlockSpec((tm, 2 * D), lambda i: (i, 0))
    return pl.pallas_call(
        body,
        name="gate_out",
        grid=(T // tm,),
        in_specs=[row(0), row(0), row(0), row(3), row(0), row(4), full((1, D)), full((1, D)), full((2 * D, D)),
                  full((D, 2 * D))],
        out_specs=[row(0), row(0), pl.BlockSpec((2 * D, tm), lambda i: (0, i)), row(0), row(0), wide,
                   full((8, D)), full((8, LANES))],
        out_shape=[
            jax.ShapeDtypeStruct((T, D), F32), jax.ShapeDtypeStruct((T, D), BF16),
            jax.ShapeDtypeStruct((2 * D, T), BF16), jax.ShapeDtypeStruct((T, D), F32),
            jax.ShapeDtypeStruct((T, D), F32), jax.ShapeDtypeStruct((T, 2 * D), BF16),
            jax.ShapeDtypeStruct((8, D), F32), jax.ShapeDtypeStruct((8, LANES), F32),
        ],
        compiler_params=_params(("arbitrary",)),
    )(x2, tgt2, o2, proj2, y2, proj2, sbw, ssw, w_out_bf, w_out_t)


def _piece_blocks(pieces, D):
    counts = [p.shape[1] // D for p in pieces]
    return [sum(counts[:i]) for i in range(len(counts))], counts


def _dhn(pieces, w_pad_t, x2, dout, norm_w, h_in, h_out, slab_off, slab_w):
    T, D = x2.shape
    tm = min(1024, T)
    starts, counts = _piece_blocks(pieces, D)
    nk = sum(counts)
    ni = T // tm
    n_sem = 2 * (N_CHIPS - 1)
    assert nk * D == w_pad_t.shape[0]

    def body(*refs):
        p_refs = refs[:len(pieces)]
        (w_ref, x_hbm, dout_hbm, nw_ref, hin, hout, gx_ref, dnw_ref, rin, rout,
         acc_s, x_s, dout_s, send_sems, recv_sems, row_sems, own_sems) = refs[len(pieces):]
        i, k = pl.program_id(0), pl.program_id(1)

        def rows():
            r0 = pl.multiple_of(i * tm, tm)
            return [pltpu.make_async_copy(src.at[pl.ds(r0, tm)], dst, row_sems.at[n])
                    for n, (src, dst) in enumerate(((x_hbm, x_s), (dout_hbm, dout_s)))]

        @pl.when(k == 0)
        def _():
            for cp in rows():
                cp.start()

        def scatter():
            x, y, c, chips = _place()

            def slab(p):
                return hin.at[:, pl.ds(pl.multiple_of(p * slab_off, LANES), slab_w)]

            cps = []
            for j, (px, py) in enumerate(chips):
                for m, (src, dst) in enumerate(((slab(2 * px + py), rin.at[j]), (hout.at[2 * px + py], rout.at[j]))):
                    cps.append(pltpu.make_async_remote_copy(
                        src_ref=src, dst_ref=dst, send_sem=send_sems.at[2 * j + m], recv_sem=recv_sems.at[2 * j + m],
                        device_id=(px, py, c), device_id_type=MESH))
            me = 2 * x + y
            own = [pltpu.make_async_copy(slab(me), rin.at[N_CHIPS - 1], own_sems.at[0]),
                   pltpu.make_async_copy(hout.at[me], rout.at[N_CHIPS - 1], own_sems.at[1])]
            return cps + own

        @pl.when((i == 0) & (k == 0))
        def _():
            for cp in scatter():
                cp.start()

        @pl.when((i == ni - 1) & (k == nk - 1))
        def _():
            for cp in scatter():
                cp.wait()

        @pl.when((i == 0) & (k == 0))
        def _():
            dnw_ref[...] = jnp.zeros_like(dnw_ref)

        @pl.when(k == 0)
        def _():
            acc_s[...] = jnp.zeros_like(acc_s)

        for p_ref, s, n in zip(p_refs, starts, counts):
            @pl.when((k >= s) & (k < s + n))
            def _(p_ref=p_ref):
                acc_s[...] += _dot(p_ref[...], w_ref[...])

        @pl.when(k == nk - 1)
        def _():
            for cp in rows():
                cp.wait()
            xv = x_s[...]
            r = lax.rsqrt(jnp.mean(xv * xv, axis=-1, keepdims=True) + EPS)
            xh = xv * r
            dhn = acc_s[...]
            dxh = dhn * nw_ref[...]
            gx_ref[...] = dout_s[...] + r * (dxh - xh * jnp.mean(dxh * xh, axis=-1, keepdims=True))
            dnw_ref[0:1, :] += _colsum(dhn * xh)

    return pl.pallas_call(
        body,
        name="dhn",
        grid=(T // tm, nk),
        in_specs=[pl.BlockSpec((tm, D), lambda i, k, s=s, n=n: (i, jnp.clip(k - s, 0, n - 1)))
                  for s, n in zip(starts, counts)] + [
            pl.BlockSpec((D, D), lambda i, k: (k, 0)),
            ANY, ANY,
            pl.BlockSpec((1, D), lambda i, k: (0, 0)),
            ANY, ANY,
        ],
        out_specs=[pl.BlockSpec((tm, D), lambda i, k: (i, 0)), pl.BlockSpec((8, D), lambda i, k: (0, 0)), ANY, ANY],
        out_shape=[jax.ShapeDtypeStruct((T, D), F32), jax.ShapeDtypeStruct((8, D), F32),
                   jax.ShapeDtypeStruct((N_CHIPS, h_in.shape[0], slab_w), F32),
                   jax.ShapeDtypeStruct((N_CHIPS,) + h_out.shape[1:], F32)],
        scratch_shapes=[pltpu.VMEM((tm, D), F32)] * 3 + [pltpu.SemaphoreType.DMA((n_sem,)), pltpu.SemaphoreType.DMA((n_sem,)),
                                                      pltpu.SemaphoreType.DMA((2,)), pltpu.SemaphoreType.DMA((2,))],
        compiler_params=_params(("arbitrary", "arbitrary")),
    )(*pieces, w_pad_t, x2, dout, norm_w, h_in, h_out)


def _grad_w_in(hn_t, pieces):
    D, T = hn_t.shape
    tk = min(1024, T)
    starts, counts = _piece_blocks(pieces, D)

    def body(*refs):
        a_ref, p_refs, o_ref = refs[0], refs[1:-1], refs[-1]
        j = pl.program_id(0)

        @pl.when(pl.program_id(1) == 0)
        def _():
            o_ref[...] = jnp.zeros_like(o_ref)

        for p_ref, s, n in zip(p_refs, starts, counts):
            @pl.when((j >= s) & (j < s + n))
            def _(p_ref=p_ref):
                o_ref[...] += _dot(a_ref[...], p_ref[...])

    def piece_spec(s, n):
        return pl.BlockSpec((tk, D), lambda j, k: (jnp.where((j >= s) & (j < s + n), k, 0), jnp.clip(j - s, 0, n - 1)))

    return pl.pallas_call(
        body,
        name="grad_w_in",
        grid=(sum(counts), T // tk),
        in_specs=[pl.BlockSpec((D, tk), lambda j, k: (0, k))] + [piece_spec(s, n) for s, n in zip(starts, counts)],
        out_specs=pl.BlockSpec((D, D), lambda j, k: (0, j)),
        out_shape=jax.ShapeDtypeStruct((D, sum(counts) * D), F32),
        compiler_params=_params(("parallel", "arbitrary")),
    )(hn_t, *pieces)


def _matmul(a, b, name):
    M, K = a.shape
    N = b.shape[1]
    tm = min(1024, M)
    tn = 1024 if N % 1024 == 0 else (512 if N % 512 == 0 else N)
    tk = min(512, K)

    def body(a_ref, b_ref, o_ref):
        @pl.when(pl.program_id(2) == 0)
        def _():
            o_ref[...] = jnp.zeros_like(o_ref)

        o_ref[...] += _dot(a_ref[...], b_ref[...])

    return pl.pallas_call(
        body,
        name=name,
        grid=(M // tm, N // tn, K // tk),
        in_specs=[pl.BlockSpec((tm, tk), lambda i, j, k: (i, k)), pl.BlockSpec((tk, tn), lambda i, j, k: (k, j))],
        out_specs=pl.BlockSpec((tm, tn), lambda i, j, k: (i, j)),
        out_shape=jax.ShapeDtypeStruct((M, N), F32),
        compiler_params=_params(("parallel", "parallel", "arbitrary")),
    )(a, b)


def _adamw(w, g, m, v, name):
    R, C = w.shape
    tr = 256 if R % 256 == 0 else R
    c1 = 1.0 - ADAM_B1 ** ADAM_STEP
    c2 = 1.0 - ADAM_B2 ** ADAM_STEP

    def body(w_ref, g_ref, m_ref, v_ref, d_ref, nm_ref, nv_ref):
        gv = g_ref[...]
        m_new = ADAM_B1 * m_ref[...] + (1.0 - ADAM_B1) * gv
        v_new = ADAM_B2 * v_ref[...] + (1.0 - ADAM_B2) * (gv * gv)
        d_ref[...] = -ADAM_LR * ((m_new / c1) / (jnp.sqrt(v_new / c2) + ADAM_EPS) + ADAM_WD * w_ref[...])
        nm_ref[...] = m_new
        nv_ref[...] = v_new

    spec = pl.BlockSpec((tr, C), lambda i: (i, 0))
    return pl.pallas_call(
        body,
        name=name,
        grid=(R // tr,),
        in_specs=[spec] * 4,
        out_specs=[spec] * 3,
        out_shape=[jax.ShapeDtypeStruct((R, C), F32)] * 3,
        compiler_params=_params(("parallel",)),
    )(w, g, m, v)


def _add_core_rows(g, recv, core, name):
    h, width = recv.shape
    th = 128 if h % 128 == 0 else h

    def body(c_ref, g_ref, r_ref, o_ref):
        o_ref[...] = g_ref[...] + r_ref[...]

    return pl.pallas_call(
        body,
        name=name,
        grid_spec=pltpu.PrefetchScalarGridSpec(
            num_scalar_prefetch=1,
            grid=(h // th,),
            in_specs=[
                pl.BlockSpec((th, width), lambda i, c: (c[0] * (h // th) + i, 0)),
                pl.BlockSpec((th, width), lambda i, c: (i, 0)),
            ],
            out_specs=pl.BlockSpec((th, width), lambda i, c: (i, 0)),
        ),
        out_shape=jax.ShapeDtypeStruct((h, width), F32),
        compiler_params=_params(("parallel",)),
    )(core, g, recv)


def _add_core_blocks(g, recv, core, name):
    n, hb, C = recv.shape

    def body(c_ref, g_ref, r_ref, o_ref):
        o_ref[...] = g_ref[...] + r_ref[...]

    return pl.pallas_call(
        body,
        name=name,
        grid_spec=pltpu.PrefetchScalarGridSpec(
            num_scalar_prefetch=1,
            grid=(n,),
            in_specs=[
                pl.BlockSpec((hb, C), lambda p, c: (2 * p + c[0], 0)),
                pl.BlockSpec((None, hb, C), lambda p, c: (p, 0, 0)),
            ],
            out_specs=pl.BlockSpec((None, hb, C), lambda p, c: (p, 0, 0)),
        ),
        out_shape=jax.ShapeDtypeStruct((n, hb, C), F32),
        compiler_params=_params(("parallel",)),
    )(core, g, recv)


def _add_chips(recv, name):
    _, h, W = recv.shape
    th = 256 if h % 256 == 0 else h

    def body(r_ref, o_ref):
        o_ref[...] = ((r_ref[N_CHIPS - 1] + r_ref[0]) + r_ref[1]) + r_ref[2]

    return pl.pallas_call(
        body,
        name=name,
        grid=(h // th,),
        in_specs=[pl.BlockSpec((N_CHIPS, th, W), lambda i: (0, i, 0))],
        out_specs=pl.BlockSpec((th, W), lambda i: (i, 0)),
        out_shape=jax.ShapeDtypeStruct((h, W), F32),
        compiler_params=_params(("parallel",)),
    )(recv)


def _place():
    x, y, c = lax.axis_index("x"), lax.axis_index("y"), lax.axis_index("c")
    other_chips = [(1 - x, y), (x, 1 - y), (1 - x, 1 - y)]
    return x, y, c, other_chips


def _allgather_weights(w_in_bf, w_out_bf, conv_w, x2, norm_w):
    D, S = w_in_bf.shape
    R = w_out_bf.shape[0]
    T = x2.shape[0]
    tm = min(1024, T)
    ni = T // tm
    n_ici, n_fwd = 3 * (N_CHIPS - 1), 2 * (N_CHIPS - 1)

    def body(win, wout, cw, x_ref, nw_ref, gin, gout, gcw, hn_ref, hnt_ref, send_sems, recv_sems):
        step = pl.program_id(0)
        xv = x_ref[...]
        hn = xv * lax.rsqrt(jnp.mean(xv * xv, axis=-1, keepdims=True) + EPS) * nw_ref[...]
        hn_ref[...] = hn.astype(BF16)
        hnt_ref[...] = hn.T.astype(BF16)
        x, y, c, chips = _place()
        me = 2 * x + y
        sibling = (x, y, 1 - c)
        hin, hout = D // 2, R // 2

        def halves(chip_idx):
            return (gin.at[chip_idx, pl.ds(c * hin, hin)], gout.at[chip_idx, pl.ds(c * hout, hout)])

        def rcopy(k, src, dst, to):
            return pltpu.make_async_remote_copy(src_ref=src, dst_ref=dst, send_sem=send_sems.at[k],
                                                recv_sem=recv_sems.at[k], device_id=to, device_id_type=MESH)

        def sends():
            my_in, my_out = halves(me)
            src_in, src_out = win.at[pl.ds(c * hin, hin)], wout.at[pl.ds(c * hout, hout)]
            cps = []
            for j, chip in enumerate(chips):
                to = (*chip, c)
                cps += [rcopy(3 * j, src_in, my_in, to), rcopy(3 * j + 1, src_out, my_out, to),
                        rcopy(3 * j + 2, cw, gcw.at[me], to)]
            return cps

        @pl.when(step == 0)
        def _():
            for cp in sends():
                cp.start()

        @pl.when(step == ni - 1)
        def _():
            passed = []
            for j, (px, py) in enumerate(chips):
                their_in, their_out = halves(2 * px + py)
                rcopy(3 * j, their_in, their_in, sibling).wait_recv()
                rcopy(3 * j + 1, their_out, their_out, sibling).wait_recv()
                rcopy(3 * j + 2, cw, gcw.at[2 * px + py], sibling).wait_recv()
                fw = [rcopy(n_ici + 2 * j, their_in, their_in, sibling),
                      rcopy(n_ici + 2 * j + 1, their_out, their_out, sibling)]
                for cp in fw:
                    cp.start()
                passed += fw
            for j, (px, py) in enumerate(chips):
                oin = gin.at[2 * px + py, pl.ds((1 - c) * hin, hin)]
                oout = gout.at[2 * px + py, pl.ds((1 - c) * hout, hout)]
                rcopy(n_ici + 2 * j, oin, oin, sibling).wait_recv()
                rcopy(n_ici + 2 * j + 1, oout, oout, sibling).wait_recv()
            for cp in sends() + passed:
                cp.wait_send()

    return pl.pallas_call(
        body,
        name="allgather_weights",
        grid=(ni,),
        in_specs=[ANY, ANY, ANY, pl.BlockSpec((tm, D), lambda i: (i, 0)), pl.BlockSpec((1, D), lambda i: (0, 0))],
        out_specs=[ANY, ANY, ANY, pl.BlockSpec((tm, D), lambda i: (i, 0)), pl.BlockSpec((D, tm), lambda i: (0, i))],
        out_shape=[jax.ShapeDtypeStruct((N_CHIPS, D, S), BF16), jax.ShapeDtypeStruct((N_CHIPS, R, D), BF16),
                   jax.ShapeDtypeStruct((N_CHIPS,) + conv_w.shape, F32),
                   jax.ShapeDtypeStruct((T, D), BF16), jax.ShapeDtypeStruct((D, T), BF16)],
        scratch_shapes=[pltpu.SemaphoreType.DMA((n_ici + n_fwd,)), pltpu.SemaphoreType.DMA((n_ici + n_fwd,))],
        compiler_params=_params(("arbitrary",)),
    )(w_in_bf, w_out_bf, conv_w, x2, norm_w)


def _allreduce_small(packed):
    R = packed.shape[0]
    n_dev = 2 * N_CHIPS

    def body(p_ref, o_ref, buf, send_sems, recv_sems):
        x, y, c, _ = _place()
        me = 4 * x + 2 * y + c
        buf[me] = p_ref[...]
        copies = []
        for k in range(1, n_dev):
            px = 1 - x if k & 4 else x
            py = 1 - y if k & 2 else y
            pc = 1 - c if k & 1 else c
            copies.append((pltpu.make_async_remote_copy(
                src_ref=buf.at[me], dst_ref=buf.at[me], send_sem=send_sems.at[k - 1], recv_sem=recv_sems.at[k - 1],
                device_id=(px, py, pc), device_id_type=MESH), 4 * px + 2 * py + pc, (px, py, pc)))
        for cp, _, _ in copies:
            cp.start()
        for k, (_, peer, to) in enumerate(copies):
            pltpu.make_async_remote_copy(
                src_ref=buf.at[peer], dst_ref=buf.at[peer], send_sem=send_sems.at[k], recv_sem=recv_sems.at[k],
                device_id=to, device_id_type=MESH).wait_recv()
        for cp, _, _ in copies:
            cp.wait_send()
        acc = buf[0]
        for d in range(1, n_dev):
            acc = acc + buf[d]
        o_ref[...] = acc

    vm = pl.BlockSpec(memory_space=pltpu.VMEM)
    return pl.pallas_call(
        body,
        name="allreduce_small",
        in_specs=[vm],
        out_specs=vm,
        out_shape=jax.ShapeDtypeStruct((R, LANES), F32),
        scratch_shapes=[pltpu.VMEM((n_dev, R, LANES), F32), pltpu.SemaphoreType.DMA((n_dev - 1,)),
                        pltpu.SemaphoreType.DMA((n_dev - 1,))],
    )(packed)


def _swap_core_halves(g_in, g_out, width):
    h = g_in.shape[0] // 2
    hb = g_out.shape[0] // (2 * N_CHIPS)

    def body(gin, gout, rin, rout, send_sems, recv_sems):
        x, y, c, _ = _place()

        def rcopy(k, src, dst):
            return pltpu.make_async_remote_copy(src_ref=src, dst_ref=dst, send_sem=send_sems.at[k], recv_sem=recv_sems.at[k],
                                                device_id=(x, y, 1 - c), device_id_type=MESH)

        cps = [rcopy(0, gin.at[pl.ds((1 - c) * h, h), pl.ds(0, width)], rin)]
        cps += [rcopy(1 + p, gout.at[pl.ds((2 * p + 1 - c) * hb, hb)], rout.at[p]) for p in range(N_CHIPS)]
        for cp in cps:
            cp.start()
        for cp in cps:
            cp.wait()

    n = 1 + N_CHIPS
    return pl.pallas_call(
        body,
        name="reduce_core_swap",
        in_specs=[ANY, ANY],
        out_specs=[ANY, ANY],
        out_shape=[jax.ShapeDtypeStruct((h, width), F32), jax.ShapeDtypeStruct((N_CHIPS, hb, g_out.shape[1]), F32)],
        scratch_shapes=[pltpu.SemaphoreType.DMA((n,)), pltpu.SemaphoreType.DMA((n,))],
    )(g_in, g_out)


def _join_core_halves(g_in, g_out):
    def body(gin, gout, fin, fout, send_sems, recv_sems):
        x, y, c, _ = _place()
        cps = [pltpu.make_async_remote_copy(src_ref=s, dst_ref=d.at[c], send_sem=send_sems.at[k],
                                            recv_sem=recv_sems.at[k], device_id=(x, y, 1 - c), device_id_type=MESH)
               for k, (s, d) in enumerate(((gin, fin), (gout, fout)))]
        for cp in cps:
            cp.start()
        for k, (s, d) in enumerate(((gin, fin), (gout, fout))):
            pltpu.make_async_remote_copy(src_ref=s, dst_ref=d.at[1 - c], send_sem=send_sems.at[k],
                                         recv_sem=recv_sems.at[k], device_id=(x, y, 1 - c),
                                         device_id_type=MESH).wait_recv()
        for cp in cps:
            cp.wait_send()

    return pl.pallas_call(
        body,
        name="reduce_core_join",
        in_specs=[ANY, ANY],
        out_specs=[ANY, ANY],
        out_shape=[jax.ShapeDtypeStruct((2,) + g_in.shape, F32), jax.ShapeDtypeStruct((2,) + g_out.shape, F32)],
        scratch_shapes=[pltpu.SemaphoreType.DMA((2,)), pltpu.SemaphoreType.DMA((2,))],
    )(g_in, g_out)


def _pack(arrays):
    rows = []
    for a in arrays:
        flat = a.reshape(-1).astype(F32)
        n = -(-flat.shape[0] // LANES) * LANES
        rows.append(jnp.pad(flat, (0, n - flat.shape[0])).reshape(-1, LANES))
    out = jnp.concatenate(rows, axis=0)
    return jnp.pad(out, ((0, -out.shape[0] % 8), (0, 0)))


def _unpack(packed, shapes):
    out, r = [], 0
    for shp in shapes:
        n = math.prod(shp)
        nr = -(-n // LANES)
        out.append(packed[r:r + nr].reshape(-1)[:n].reshape(shp))
        r += nr
    return out


def _pad_lanes(a):
    return jnp.pad(a, ((0, 0), (0, LANES - a.shape[1])))


def kernel(x, norm_w, w_in, q_norm_w, k_norm_w, conv_w, conv_b, dt_bias, A_log, D_skip, sb_norm_w, ssd_norm_w, w_out, loss_target, m_norm_w, m_w_in, m_q_norm_w, m_k_norm_w, m_conv_w, m_conv_b, m_dt_bias, m_A_log, m_D_skip, m_sb_norm_w, m_ssd_norm_w, m_w_out, v_norm_w, v_w_in, v_q_norm_w, v_k_norm_w, v_conv_w, v_conv_b, v_dt_bias, v_A_log, v_D_skip, v_sb_norm_w, v_ssd_norm_w, v_w_out):
    Bl, L, D = x.shape
    T = Bl * L
    S = w_in.shape[2]
    R = w_out.shape[1]
    CW = conv_w.shape[2]
    n_in = N_CHIPS * S
    CD = D + 2 * SSD_GROUPS * SSD_STATE
    H = D // HEAD_DIM
    n_main = 6 * D + 512
    P = -(-(n_main + LANES) // 1024) * 1024
    assert n_in == n_main + H and CD == N_CHIPS * CW and 2 * D == N_CHIPS * R and CD == D + 512
    chip = (2 * lax.axis_index("x") + lax.axis_index("y")).astype(jnp.int32)
    core = lax.axis_index("c").astype(jnp.int32)

    w_in_bf, w_out_shard_bf = w_in[0].astype(BF16), w_out[0].astype(BF16)
    x2 = x.reshape(T, D)
    g_in, g_out, g_cw, hn, hn_t = _allgather_weights(w_in_bf, w_out_shard_bf, conv_w[0], x2, norm_w)
    g_in = lax.dynamic_update_slice(g_in, w_in_bf[None], (chip, 0, 0))
    g_out = lax.dynamic_update_slice(g_out, w_out_shard_bf[None], (chip, 0, 0))
    g_cw = lax.dynamic_update_slice(g_cw, conv_w, (chip, 0, 0))
    w_pad = jnp.pad(g_in.transpose(1, 0, 2).reshape(D, n_in), ((0, 0), (0, P - n_in)))
    w_out_bf = g_out.reshape(2 * D, D)
    conv_full = g_cw.transpose(1, 0, 2).reshape(CONV_K, CD)
    cwx, cwb = conv_full[:, :D], conv_full[:, D:]
    cbx, cbb = conv_b[:, :D], conv_b[:, D:]
    dtb, alog, dsk = _pad_lanes(dt_bias), _pad_lanes(A_log), _pad_lanes(D_skip)
    qw2, kw2 = jnp.tile(q_norm_w, (1, 2)), jnp.tile(k_norm_w, (1, 2))

    proj, w_pad_t = _inproj(hn, w_pad)
    proj3 = proj.reshape(Bl, L, P)
    o_sb = _attn_fwd(proj3, qw2, kw2, D)
    y_ssd, s_in = _ssd_fwd(proj3, cwx, cwb, cbx, cbb, dtb, alog, dsk, D)
    dout, dout_bf, mixed_t, do_sb, dy_ssd, dz_bf, dnw_out, loss_blk = _gate_out(
        x2, loss_target.reshape(T, D), o_sb.reshape(T, D), proj, y_ssd.reshape(T, D), sb_norm_w, ssd_norm_w, w_out_bf,
        w_out_bf.T)

    dq, dk, dv, dqkw = _attn_bwd(proj3, o_sb, do_sb.reshape(Bl, L, D), qw2, kw2, D)
    dtail, dcwx, dcwb, dcbx, dcbb, misc = _ssd_bwd(
        proj3, s_in, dy_ssd.reshape(Bl, L, D), cwx, cwb, cbx, cbb, dtb, alog, dsk, D, P - 5 * D)
    dproj = [dq.reshape(T, D), dk.reshape(T, D), dv.reshape(T, D), dz_bf, dtail.reshape(T, P - 5 * D)]
    gw_in = _grad_w_in(hn_t, dproj)
    gw_out = _matmul(mixed_t, dout_bf, "grad_w_out")

    slab_off = S // LANES * LANES
    slab_w = -(-(S + (N_CHIPS - 1) * (S - slab_off)) // LANES) * LANES
    width = (N_CHIPS - 1) * slab_off + slab_w
    assert n_in <= width <= P
    core1 = core.reshape(1)
    r_in, r_out = _swap_core_halves(gw_in, gw_out, width)
    h_in = _add_core_rows(gw_in, r_in, core1, "sum_cores_w_in")
    h_out = _add_core_blocks(gw_out, r_out, core1, "sum_cores_w_out")
    grad_x2, dnw_in, s_in_, s_out_ = _dhn(dproj, w_pad_t, x2, dout, norm_w, h_in, h_out, slab_off, slab_w)
    gh_in = _add_chips(s_in_, "sum_chips_w_in")
    gh_out = _add_chips(s_out_, "sum_chips_w_out")
    f_in, f_out = _join_core_halves(gh_in, gh_out)
    g_slab = lax.dynamic_update_slice(f_in, gh_in[None], (core, 0, 0)).reshape(D, slab_w)
    g_w_in = lax.dynamic_slice(g_slab, (0, chip * (S - slab_off)), (D, S))
    g_w_out = lax.dynamic_update_slice(f_out, gh_out[None], (core, 0, 0)).reshape(R, D)

    small_shapes = [(1, D), (1, D), (1, D), (1, CD), (1, HEAD_DIM), (1, HEAD_DIM), (1, H), (1, H), (1, H)]
    g_small_local = [dnw_in[0:1], dnw_out[0:1], dnw_out[1:2], jnp.concatenate([dcbx, dcbb], axis=1),
                     dqkw[0:1, :HEAD_DIM] + dqkw[0:1, HEAD_DIM:], dqkw[1:2, :HEAD_DIM] + dqkw[1:2, HEAD_DIM:],
                     misc[0:1, :H], misc[1:2, :H], misc[2:3, :H]]
    packed = _pack(g_small_local + [jnp.concatenate([dcwx, dcwb], axis=1), loss_blk[0:1, 0:1]])
    red = _allreduce_small(packed)
    g_small = _unpack(red, small_shapes + [(CONV_K, CD), (1, 1)])
    g_conv_w = lax.dynamic_slice_in_dim(g_small[9], chip * CW, CW, axis=1)
    loss = g_small[10][0, 0]

    d_in, nm_in, nv_in = _adamw(w_in[0], g_w_in, m_w_in[0], v_w_in[0], "adamw_w_in")
    d_out, nm_out, nv_out = _adamw(w_out[0], g_w_out, m_w_out[0], v_w_out[0], "adamw_w_out")
    d_cw, nm_cw, nv_cw = _adamw(conv_w[0], g_conv_w, m_conv_w[0], v_conv_w[0], "adamw_conv_w")
    small_w = [norm_w, sb_norm_w, ssd_norm_w, conv_b, q_norm_w, k_norm_w, dt_bias, A_log, D_skip]
    small_m = [m_norm_w, m_sb_norm_w, m_ssd_norm_w, m_conv_b, m_q_norm_w, m_k_norm_w, m_dt_bias, m_A_log, m_D_skip]
    small_v = [v_norm_w, v_sb_norm_w, v_ssd_norm_w, v_conv_b, v_q_norm_w, v_k_norm_w, v_dt_bias, v_A_log, v_D_skip]
    d_s, nm_s, nv_s = _adamw(_pack(small_w), _pack(g_small[:9]), _pack(small_m), _pack(small_v), "adamw_small")
    d_s, nm_s, nv_s = (_unpack(t, small_shapes) for t in (d_s, nm_s, nv_s))

    def ordered(s, w_in_, conv_w_, w_out_):
        return [s[0], w_in_[None], s[4], s[5], conv_w_[None], s[3], s[6], s[7], s[8], s[1], s[2], w_out_[None]]

    return (loss, grad_x2.reshape(Bl, L, D),
            *ordered(g_small[:9], g_w_in, g_conv_w, g_w_out),
            *ordered(d_s, d_in, d_cw, d_out),
            *ordered(nm_s, nm_in, nm_cw, nm_out),
            *ordered(nv_s, nv_in, nv_cw, nv_out))
```

```python
import functools
import math

import jax
import jax.numpy as jnp
from jax import lax
from jax.experimental import pallas as pl
from jax.experimental.pallas import tpu as pltpu

F32 = jnp.float32
BF16 = jnp.bfloat16
EPS = 1e-6
HEAD_DIM = 64
PAIR = 2 * HEAD_DIM
LANES = 128
SSD_STATE = 128
SSD_GROUPS = 2
BLK = 128
PREP_BLOCKS = 4
Q_TOGETHER_FWD = 2
Q_TOGETHER_BWD = 2
FIRST_LEFT = 2
UNDERFLOW = -105.0
CONV_K = 4
HALO = 8
N_CHIPS = 4
ADAM_LR, ADAM_B1, ADAM_B2, ADAM_EPS, ADAM_WD, ADAM_STEP = 0.001, 0.9, 0.999, 1e-08, 0.01, 10
VMEM_LIMIT_V7X = 56 * 1024 * 1024
MESH = pl.DeviceIdType.MESH
ANY = pl.BlockSpec(memory_space=pl.ANY)
NT = (((1,), (1,)), ((), ()))


def _params(sem=None):
    kw = dict(vmem_limit_bytes=VMEM_LIMIT_V7X)
    if sem is not None:
        kw["dimension_semantics"] = sem
    return pltpu.CompilerParams(**kw)


def _dot(a, b):
    return jnp.dot(a, b, preferred_element_type=F32)


def _dot_nt(a, b):
    return lax.dot_general(a, b, NT, preferred_element_type=F32)


def _dot_split(m, x):
    hi = x.astype(BF16)
    lo = (x - hi.astype(F32)).astype(BF16)
    return _dot(m, hi) + _dot(m, lo)


def _iota(shape, dim):
    return lax.broadcasted_iota(jnp.int32, shape, dim)


def _rowsum(x):
    return jnp.sum(x, axis=1, keepdims=True)


def _colsum(x):
    return jnp.sum(x, axis=0, keepdims=True)


def _sigmoid(x):
    return 0.5 * jnp.tanh(0.5 * x) + 0.5


def _softplus(x):
    return jnp.maximum(x, 0.0) + jnp.log(1.0 + jnp.exp(-jnp.abs(x)))


def _inproj(hn, w_pad, w_out_bf, conv_w):
    T, D = hn.shape
    P = w_pad.shape[1]
    tm = min(1024, T)
    tn = 1024 if P % 1024 == 0 else 512
    ni, nj = T // tm, P // tn
    n_sem = 2 * (N_CHIPS - 1)

    def body(hn_ref, w_ref, wout, cw, proj_ref, wt_ref, gout, gcw, send_sems, recv_sems):
        def gather():
            x, y, c, chips = _place()
            me = 2 * x + y
            return [pltpu.make_async_remote_copy(
                src_ref=src, dst_ref=dst.at[me], send_sem=send_sems.at[2 * j + m], recv_sem=recv_sems.at[2 * j + m],
                device_id=(px, py, c), device_id_type=MESH)
                for j, (px, py) in enumerate(chips) for m, (src, dst) in enumerate(((wout, gout), (cw, gcw)))]

        @pl.when((pl.program_id(0) == 0) & (pl.program_id(1) == 0))
        def _():
            for cp in gather():
                cp.start()

        @pl.when((pl.program_id(0) == ni - 1) & (pl.program_id(1) == nj - 1))
        def _():
            for cp in gather():
                cp.wait()

        @pl.when(pl.program_id(0) == 0)
        def _():
            wt_ref[...] = w_ref[...].astype(F32).T.astype(BF16)

        proj_ref[...] = _dot(hn_ref[...], w_ref[...])

    return pl.pallas_call(
        body,
        name="inproj",
        grid=(T // tm, P // tn),
        in_specs=[
            pl.BlockSpec((tm, D), lambda i, j: (i, 0)),
            pl.BlockSpec((D, tn), lambda i, j: (0, j)),
            ANY, ANY,
        ],
        out_specs=[
            pl.BlockSpec((tm, tn), lambda i, j: (i, j)),
            pl.BlockSpec((tn, D), lambda i, j: (jnp.where(i == 0, j, nj - 1), 0)),
            ANY, ANY,
        ],
        out_shape=[jax.ShapeDtypeStruct((T, P), F32), jax.ShapeDtypeStruct((P, D), BF16),
                   jax.ShapeDtypeStruct((N_CHIPS,) + w_out_bf.shape, BF16),
                   jax.ShapeDtypeStruct((N_CHIPS,) + conv_w.shape, F32)],
        scratch_shapes=[pltpu.SemaphoreType.DMA((n_sem,)), pltpu.SemaphoreType.DMA((n_sem,))],
        compiler_params=_params(("arbitrary", "arbitrary")),
    )(hn, w_pad, w_out_bf, conv_w)


def _pair_ones():
    ri = ((_iota((2 * PAIR, PAIR), 0) % PAIR) >= HEAD_DIM).astype(jnp.int32)
    ci = (_iota((2 * PAIR, PAIR), 1) >= HEAD_DIM).astype(jnp.int32)
    return jnp.where(ri == ci, 1.0, 0.0).astype(BF16)


def _pair_rms(v, ones2):
    return lax.rsqrt(_split_dots([v * v], ones2)[0] * (1.0 / HEAD_DIM) + EPS)


def _pair_mean(v, ones2):
    return _split_dots([v], ones2)[0] * (1.0 / HEAD_DIM)


def _suffix_ones():
    ri = _iota((2 * BLK, 2 * BLK), 0) % BLK
    ci = _iota((2 * BLK, 2 * BLK), 1)
    return jnp.where((ci >= BLK) | (ri > ci), 1.0, 0.0).astype(BF16)


def _split_dots(xs, m2):
    his = [x.astype(BF16) for x in xs]
    los = [(x - hi.astype(F32)).astype(BF16) for x, hi in zip(xs, his)]
    return [_dot(jnp.concatenate([hi, lo], axis=1), m2) for hi, lo in zip(his, los)]


def _sb_tiles(streams, km_s, uo):
    tiles = [(s, u, h) for s, st in enumerate(streams) for u in range(len(st["kbs"])) for h in range(2)]
    z2s = {(s, u): _dot_nt(st["q"], km_s[kb]) for s, st in enumerate(streams) for u, kb in enumerate(st["kbs"])}
    zs = [z2s[s, u][:, h * BLK:(h + 1) * BLK] for s, u, h in tiles]
    es = [jnp.exp(-jnp.abs(z)) for z in zs]
    las = [jnp.minimum(z, 0.0) - jnp.log(1.0 + e) for z, e in zip(zs, es)]
    lns = [a - z for a, z in zip(las, zs)]
    masks = [streams[s]["masks"][u] for s, u, h in tiles]
    lks = [lk if m is None else jnp.where(m, lk, 0.0) for m, lk in zip(masks, lns)]
    css = _split_dots(lks, uo)
    rests = [list(st["rest"]) for st in streams]
    ws = []
    for (s, u, h), m, a, cs in zip(tiles, masks, las, css):
        w = jnp.exp(a + rests[s][h] + cs[:, :BLK])
        ws.append(w if m is None else jnp.where(m, w, 0.0))
        rests[s][h] = rests[s][h] + cs[:, BLK:]
    return tiles, las, lns, ws, rests


def _stream(q_pair, qi, n_left, diag, zero):
    return dict(q=q_pair, kbs=[qi - u for u in range(n_left + 1)], masks=[diag] + [None] * n_left, rest=[zero, zero])


def _row0(block):
    return block * BLK if isinstance(block, int) else pl.multiple_of(block * BLK, BLK)


def _pair_of(vals, tiles, s, u):
    return [v for v, t in zip(vals, tiles) if t[0] == s and t[1] == u]


def _block_groups(nq, together):
    n_tog = math.gcd(together, nq)
    assert n_tog >= FIRST_LEFT
    return n_tog, list(range(n_tog)), nq // n_tog


def _attn_prep(src_ref, w_ref, dst_s, n_blocks, scale):
    per = math.gcd(PREP_BLOCKS, n_blocks)
    rows = per * BLK
    lo = _iota((rows, PAIR), 1) < HEAD_DIM
    ones2 = _pair_ones()

    def step(i, carry):
        r0 = pl.multiple_of(i * rows, rows)
        v = src_ref[0, pl.ds(r0, rows), :]
        if w_ref is not None:
            v = v * _pair_rms(v, ones2) * w_ref[...]
        if scale != 1.0:
            v = v * scale
        v0, v1 = jnp.where(lo, v, 0.0).astype(BF16), jnp.where(lo, 0.0, v).astype(BF16)
        for b in range(per):
            dst_s[i * per + b, 0:BLK, :] = v0[b * BLK:(b + 1) * BLK]
            dst_s[i * per + b, BLK:2 * BLK, :] = v1[b * BLK:(b + 1) * BLK]
        return carry

    lax.fori_loop(0, n_blocks // per, step, 0)


def _attn_fwd(proj3, qw2, kw2, D):
    Bl, L, _ = proj3.shape
    n_pair = D // PAIR
    nq = L // BLK
    scale = 1.0 / math.sqrt(HEAD_DIM)

    def body(q_ref, k_ref, v_ref, qw_ref, kw_ref, o_ref, qm_s, km_s, vm_s):
        uo = _suffix_ones()
        diag = _iota((BLK, BLK), 1) < _iota((BLK, BLK), 0)
        _attn_prep(q_ref, qw_ref, qm_s, nq, scale)
        _attn_prep(k_ref, kw_ref, km_s, nq, 1.0)
        _attn_prep(v_ref, None, vm_s, nq, 1.0)

        zero_c = jnp.zeros((BLK, BLK), F32)

        def q_of(qi):
            return qm_s[qi, 0:BLK, :] + qm_s[qi, BLK:2 * BLK, :]

        def values(streams, accs):
            tiles, _, _, ws, rests = _sb_tiles(streams, km_s, uo)
            wbs = [w.astype(BF16) for w in ws]
            accs = list(accs)
            for s, st in enumerate(streams):
                for u, kb in enumerate(st["kbs"]):
                    accs[s] = accs[s] + _dot(jnp.concatenate(_pair_of(wbs, tiles, s, u), axis=1), vm_s[kb])
            return accs, rests

        def group(qis, n_lefts):
            streams = [_stream(q_of(qi), qi, n, diag, zero_c) for qi, n in zip(qis, n_lefts)]
            accs, rests = values(streams, [jnp.zeros((BLK, PAIR), F32)] * len(qis))
            for qi, n, q, acc, rc in zip(qis, n_lefts, [st["q"] for st in streams], accs, rests):

                def sweep(state, n_blocks, q=q):
                    kb, rc0, rc1, acc1, _ = state
                    st = dict(q=q, kbs=[kb - u for u in range(n_blocks)], masks=[None] * n_blocks, rest=[rc0, rc1])
                    (acc1,), (r,) = values([st], [acc1])
                    return kb - n_blocks, r[0], r[1], acc1, jnp.maximum(jnp.max(r[0]), jnp.max(r[1]))

                state = (jnp.asarray(qi - n - 1, jnp.int32), rc[0], rc[1], acc, jnp.maximum(jnp.max(rc[0]), jnp.max(rc[1])))
                state = lax.while_loop(lambda t: (t[0] >= 1) & (t[4] >= UNDERFLOW), lambda t: sweep(t, 2), state)
                state = lax.while_loop(lambda t: (t[0] >= 0) & (t[4] >= UNDERFLOW), lambda t: sweep(t, 1), state)
                o_ref[0, pl.ds(_row0(qi), BLK), :] = state[3]

        n_tog, head, n_groups = _block_groups(nq, Q_TOGETHER_FWD)
        group(head, [min(qi, FIRST_LEFT) for qi in head])

        def groups(g, carry):
            group([g * n_tog + j for j in range(n_tog)], [FIRST_LEFT] * n_tog)
            return carry

        lax.fori_loop(1, n_groups, groups, 0)

    blk = lambda off: pl.BlockSpec((1, L, PAIR), lambda b, p: (b, 0, off + p))
    wspec = pl.BlockSpec((1, PAIR), lambda b, p: (0, 0))
    return pl.pallas_call(
        body,
        name="sb_attn_fwd",
        grid=(Bl, n_pair),
        in_specs=[blk(0), blk(n_pair), blk(2 * n_pair), wspec, wspec],
        out_specs=pl.BlockSpec((1, L, PAIR), lambda b, p: (b, 0, p)),
        out_shape=jax.ShapeDtypeStruct((Bl, L, D), F32),
        scratch_shapes=[pltpu.VMEM((nq, 2 * BLK, PAIR), BF16)] * 3,
        compiler_params=_params(("parallel", "parallel")),
    )(proj3, proj3, proj3, qw2, kw2)


def _attn_bwd(proj3, o3, do3, qw2, kw2, D):
    Bl, L, _ = proj3.shape
    n_pair = D // PAIR
    nq = L // BLK
    scale = 1.0 / math.sqrt(HEAD_DIM)

    def body(q_ref, k_ref, v_ref, o_ref, do_ref, qw_ref, kw_ref, dq_ref, dk_ref, dv_ref, dw_ref,
             qm_s, km_s, vm_s, dom_s, dq_s, dk_s, dv_s):
        uo = _suffix_ones()
        diag = _iota((BLK, BLK), 1) < _iota((BLK, BLK), 0)
        ones2 = _pair_ones()
        _attn_prep(q_ref, qw_ref, qm_s, nq, scale)
        _attn_prep(k_ref, kw_ref, km_s, nq, 1.0)
        _attn_prep(v_ref, None, vm_s, nq, 1.0)
        _attn_prep(do_ref, None, dom_s, nq, 1.0)

        @pl.when((pl.program_id(0) == 0) & (pl.program_id(1) == 0))
        def _():
            dw_ref[...] = jnp.zeros_like(dw_ref)

        def zero(i, carry):
            r0 = pl.multiple_of(i * BLK, BLK)
            dk_s[pl.ds(r0, BLK), :] = jnp.zeros((BLK, PAIR), F32)
            dv_s[pl.ds(r0, BLK), :] = jnp.zeros((BLK, PAIR), F32)
            return carry

        lax.fori_loop(0, nq, zero, 0)

        zero_c = jnp.zeros((BLK, BLK), F32)

        def tiles_bwd(streams, dqas):
            tiles, las, lns, ws, rests = _sb_tiles(streams, km_s, uo)
            dw2s = {(s, u): _dot_nt(st["do"], vm_s[kb]) for s, st in enumerate(streams) for u, kb in enumerate(st["kbs"])}
            dws = [dw2s[s, u][:, h * BLK:(h + 1) * BLK] for s, u, h in tiles]
            wfs = [w.astype(BF16).astype(F32) for w in ws]
            gs = [wf * dw for wf, dw in zip(wfs, dws)]
            gss = _split_dots(gs, uo)
            gcs = [list(st["g_rest"]) for st in streams]
            dzs = []
            for (s, u, h), a, ln, g, gsum in zip(tiles, las, lns, gs, gss):
                g_before = streams[s]["delta"][h] - (gcs[s][h] + gsum[:, :BLK] + g)
                gcs[s][h] = gcs[s][h] + gsum[:, BLK:]
                dz = g * jnp.exp(ln) - g_before * jnp.exp(a)
                m = streams[s]["masks"][u]
                dzs.append(dz if m is None else jnp.where(m, dz, 0.0))
            wts = [wf.T.astype(BF16) for wf in wfs]
            dzts = [dz.T.astype(BF16) for dz in dzs]
            dzbs = [dz.astype(BF16) for dz in dzs]
            dqas = list(dqas)
            for s, st in enumerate(streams):
                for u, kb in enumerate(st["kbs"]):
                    c0 = _row0(kb)
                    dv_s[pl.ds(c0, BLK), :] += _dot(jnp.concatenate(_pair_of(wts, tiles, s, u), axis=1), dom_s[st["qi"]])
                    dk_s[pl.ds(c0, BLK), :] += _dot(jnp.concatenate(_pair_of(dzts, tiles, s, u), axis=1), qm_s[st["qi"]])
                    dqas[s] = dqas[s] + _dot(jnp.concatenate(_pair_of(dzbs, tiles, s, u), axis=1), km_s[kb])
            return dqas, rests, gcs

        def group(qis, n_lefts):
            streams = []
            for qi, n in zip(qis, n_lefts):
                o_blk = o_ref[0, pl.ds(_row0(qi), BLK), :]
                doms = [dom_s[qi, 0:BLK, :], dom_s[qi, BLK:2 * BLK, :]]
                st = _stream(qm_s[qi, 0:BLK, :] + qm_s[qi, BLK:2 * BLK, :], qi, n, diag, zero_c)
                st.update(qi=qi, do=doms[0] + doms[1], delta=[_rowsum(d.astype(F32) * o_blk) for d in doms],
                          g_rest=[zero_c, zero_c])
                streams.append(st)
            dqas, rests, gcs = tiles_bwd(streams, [jnp.zeros((BLK, PAIR), F32)] * len(qis))
            for qi, n, st0, dqa, rc, gc in zip(qis, n_lefts, streams, dqas, rests, gcs):

                def sweep(state, n_blocks, st0=st0):
                    kb, rc0, rc1, gc0, gc1, dqa1, _ = state
                    st = dict(st0, kbs=[kb - u for u in range(n_blocks)], masks=[None] * n_blocks, rest=[rc0, rc1],
                              g_rest=[gc0, gc1])
                    (dqa1,), (r,), (g,) = tiles_bwd([st], [dqa1])
                    return kb - n_blocks, r[0], r[1], g[0], g[1], dqa1, jnp.maximum(jnp.max(r[0]), jnp.max(r[1]))

                state = (jnp.asarray(qi - n - 1, jnp.int32), rc[0], rc[1], gc[0], gc[1], dqa,
                         jnp.maximum(jnp.max(rc[0]), jnp.max(rc[1])))
                state = lax.while_loop(lambda t: (t[0] >= 1) & (t[6] >= UNDERFLOW), lambda t: sweep(t, 2), state)
                state = lax.while_loop(lambda t: (t[0] >= 0) & (t[6] >= UNDERFLOW), lambda t: sweep(t, 1), state)
                dq_s[pl.ds(_row0(qi), BLK), :] = state[5] * scale

        n_tog, head, n_groups = _block_groups(nq, Q_TOGETHER_BWD)
        group(head, [min(qi, FIRST_LEFT) for qi in head])

        def groups(g, carry):
            group([g * n_tog + j for j in range(n_tog)], [FIRST_LEFT] * n_tog)
            return carry

        lax.fori_loop(1, n_groups, groups, 0)

        per = math.gcd(PREP_BLOCKS, nq)
        rows = per * BLK

        def finish(i, carry):
            r0 = pl.multiple_of(i * rows, rows)
            dwq, dwk = carry
            out = []
            for src_ref, w_ref, d_s in ((q_ref, qw_ref, dq_s), (k_ref, kw_ref, dk_s)):
                v = src_ref[0, pl.ds(r0, rows), :]
                r = _pair_rms(v, ones2)
                vh = v * r
                dy = d_s[pl.ds(r0, rows), :]
                dvh = dy * w_ref[...]
                out.append((r * (dvh - vh * _pair_mean(dvh * vh, ones2)), _colsum(dy * vh)))
            dq_ref[0, pl.ds(r0, rows), :] = out[0][0].astype(BF16)
            dk_ref[0, pl.ds(r0, rows), :] = out[1][0].astype(BF16)
            dv_ref[0, pl.ds(r0, rows), :] = dv_s[pl.ds(r0, rows), :].astype(BF16)
            return dwq + out[0][1], dwk + out[1][1]

        zrow = jnp.zeros((1, PAIR), F32)
        dwq, dwk = lax.fori_loop(0, nq // per, finish, (zrow, zrow))
        dw_ref[0:1, :] += dwq
        dw_ref[1:2, :] += dwk

    blk = lambda off: pl.BlockSpec((1, L, PAIR), lambda b, p: (b, 0, off + p))
    wspec = pl.BlockSpec((1, PAIR), lambda b, p: (0, 0))
    oblk = pl.BlockSpec((1, L, PAIR), lambda b, p: (b, 0, p))
    return pl.pallas_call(
        body,
        name="sb_attn_bwd",
        grid=(Bl, n_pair),
        in_specs=[blk(0), blk(n_pair), blk(2 * n_pair), oblk, oblk, wspec, wspec],
        out_specs=[oblk, oblk, oblk, pl.BlockSpec((8, PAIR), lambda b, p: (0, 0))],
        out_shape=[jax.ShapeDtypeStruct((Bl, L, D), BF16)] * 3 + [jax.ShapeDtypeStruct((8, PAIR), F32)],
        scratch_shapes=[pltpu.VMEM((nq, 2 * BLK, PAIR), BF16)] * 4 + [pltpu.VMEM((L, PAIR), F32)] * 3,
        compiler_params=_params(("arbitrary", "arbitrary")),
    )(proj3, proj3, proj3, o3, do3, qw2, kw2)


def _conv_pre(ext_s, halo_ref, raw_ref, w_ref, b_ref, first):
    ext_s[0:HALO, :] = jnp.where(first, 0.0, halo_ref[0])
    ext_s[HALO:HALO + BLK, :] = raw_ref[0]
    pre = b_ref[...]
    for i in range(CONV_K):
        pre = pre + ext_s[pl.ds(HALO - (CONV_K - 1 - i), BLK), :] * w_ref[i:i + 1, :]
    return pre


def _lane_col(m, lane, h):
    return _rowsum(jnp.where(lane == h, m, 0.0))


def _half_sums(row, lo1):
    return _rowsum(jnp.where(lo1, row, 0.0)), _rowsum(jnp.where(lo1, 0.0, row))


def _ssd_specs(Bl, L, D, rev):
    nc = L // BLK
    rows_per = BLK // HALO
    cidx = (lambda c: nc - 1 - c) if rev else (lambda c: c)
    xoff = 5
    boff = (6 * D) // 512
    doff = (6 * D + 512) // LANES
    prev = lambda c: jnp.maximum(cidx(c) * rows_per - 1, 0)
    specs = [
        pl.BlockSpec((1, BLK, D), lambda b, c: (b, cidx(c), xoff)),
        pl.BlockSpec((1, BLK, 512), lambda b, c: (b, cidx(c), boff)),
        pl.BlockSpec((1, HALO, D), lambda b, c: (b, prev(c), xoff)),
        pl.BlockSpec((1, HALO, 512), lambda b, c: (b, prev(c), boff)),
        pl.BlockSpec((1, BLK, LANES), lambda b, c: (b, cidx(c), doff)),
    ]
    full = lambda shape: pl.BlockSpec(shape, lambda b, c: (0,) * len(shape))
    specs += [full((CONV_K, D)), full((CONV_K, 512)), full((1, D)), full((1, 512)),
              full((1, LANES)), full((1, LANES)), full((1, LANES))]
    return specs, cidx


def _ssd_common(dtr_ref, dtb_ref, alog_ref, acs_s, acsT_s):
    ltri = jnp.where(_iota((BLK, BLK), 1) <= _iota((BLK, BLK), 0), 1.0, 0.0).astype(BF16)
    dtv = _softplus(dtr_ref[0] + dtb_ref[...])
    a = -jnp.exp(alog_ref[...])
    acs = _dot_split(ltri, dtv * a)
    acs_s[...] = acs
    acsT_s[...] = acs.T
    return dtv, a, acs


def _pair_terms(pr, acs, dtv, acs_s, lane, lo, lane1, lo1):
    h0, h1 = 2 * pr, 2 * pr + 1
    c0, c1 = _lane_col(acs, lane, h0), _lane_col(acs, lane, h1)
    d0, d1 = _lane_col(dtv, lane, h0), _lane_col(dtv, lane, h1)
    lastv = acs_s[BLK - 1:BLK, :]
    l0, l1 = _lane_col(lastv, lane1, h0), _lane_col(lastv, lane1, h1)
    return dict(h=(h0, h1), c=(c0, c1), last=(l0, l1), acs_p=jnp.where(lo, c0, c1), dt_p=jnp.where(lo, d0, d1),
                last_p=jnp.where(lo1, l0, l1))


def _decay_tiles(cc, row, tri, want_t):
    lm = jnp.where(tri, jnp.exp(cc - row), 0.0)
    return lm, (lm.T if want_t else None)


def _ssd_fwd(proj3, cwx, cwb, cbx, cbb, dtb, alog, dsk, D):
    Bl, L, _ = proj3.shape
    nc = L // BLK
    n_pair = D // PAIR
    pairs_per_group = n_pair // SSD_GROUPS
    specs, _ = _ssd_specs(Bl, L, D, False)

    def body(xr_ref, bcr_ref, xh_ref, bch_ref, dtr_ref, cwx_ref, cwb_ref, cbx_ref, cbb_ref, dtb_ref, alog_ref,
             dsk_ref, y_ref, sin_ref, st_s, extx_s, extb_s, acs_s, acsT_s):
        first = pl.program_id(1) == 0

        @pl.when(first)
        def _():
            st_s[...] = jnp.zeros_like(st_s)

        lane, lane1 = _iota((BLK, LANES), 1), _iota((1, LANES), 1)
        lo, lo1 = lane < HEAD_DIM, lane1 < HEAD_DIM
        tri = _iota((BLK, BLK), 1) <= _iota((BLK, BLK), 0)
        pre = _conv_pre(extx_s, xh_ref, xr_ref, cwx_ref, cbx_ref, first)
        ux = pre * _sigmoid(pre)
        pre = _conv_pre(extb_s, bch_ref, bcr_ref, cwb_ref, cbb_ref, first)
        ub = pre * _sigmoid(pre)
        dtv, a, acs = _ssd_common(dtr_ref, dtb_ref, alog_ref, acs_s, acsT_s)
        for g in range(SSD_GROUPS):
            bg = ub[:, g * SSD_STATE:(g + 1) * SSD_STATE]
            cb_ = ub[:, (SSD_GROUPS + g) * SSD_STATE:(SSD_GROUPS + g + 1) * SSD_STATE].astype(BF16)
            cbm = _dot_nt(cb_, bg.astype(BF16))
            btb = bg.T.astype(BF16)
            for pr in range(g * pairs_per_group, (g + 1) * pairs_per_group):
                t = _pair_terms(pr, acs, dtv, acs_s, lane, lo, lane1, lo1)
                xs_p = ux[:, pr * PAIR:(pr + 1) * PAIR]
                x_p = xs_p * t["dt_p"]
                st = st_s[pr]
                sin_ref[0, 0, pr] = st
                y = _dot(cb_, st.astype(BF16)) * jnp.exp(t["acs_p"])
                for k in range(2):
                    row = acsT_s[t["h"][k]:t["h"][k] + 1, :]
                    lm, _ = _decay_tiles(t["c"][k], row, tri, False)
                    xm = jnp.where(lo if k == 0 else ~lo, x_p, 0.0).astype(BF16)
                    y = y + _dot((cbm * lm).astype(BF16), xm)
                d_p = jnp.where(lo1, _lane_col(dsk_ref[...], lane1, t["h"][0]), _lane_col(dsk_ref[...], lane1, t["h"][1]))
                y_ref[0, :, pr * PAIR:(pr + 1) * PAIR] = y + d_p * xs_p
                xd = (x_p * jnp.exp(t["last_p"] - t["acs_p"])).astype(BF16)
                st_s[pr] = st * jnp.exp(t["last_p"]) + _dot(btb, xd)

    return pl.pallas_call(
        body,
        name="ssd_fwd",
        grid=(Bl, nc),
        in_specs=specs,
        out_specs=[
            pl.BlockSpec((1, BLK, D), lambda b, c: (b, c, 0)),
            pl.BlockSpec((1, 1, n_pair, SSD_STATE, PAIR), lambda b, c: (b, c, 0, 0, 0)),
        ],
        out_shape=[jax.ShapeDtypeStruct((Bl, L, D), F32),
                   jax.ShapeDtypeStruct((Bl, nc, n_pair, SSD_STATE, PAIR), F32)],
        scratch_shapes=[pltpu.VMEM((n_pair, SSD_STATE, PAIR), F32), pltpu.VMEM((HALO + BLK, D), F32),
                        pltpu.VMEM((HALO + BLK, 512), F32), pltpu.VMEM((BLK, LANES), F32),
                        pltpu.VMEM((LANES, BLK), F32)],
        compiler_params=_params(("arbitrary", "arbitrary")),
    )(proj3, proj3, proj3, proj3, proj3, cwx, cwb, cbx, cbb, dtb, alog, dsk)


def _ssd_bwd(proj3, s_in, dy3, cwx, cwb, cbx, cbb, dtb, alog, dsk, D, tail):
    Bl, L, _ = proj3.shape
    CD = D + 512
    nc = L // BLK
    n_pair = D // PAIR
    n_heads = 2 * n_pair
    pairs_per_group = n_pair // SSD_GROUPS
    specs, cidx = _ssd_specs(Bl, L, D, True)
    specs = specs + [
        pl.BlockSpec((1, 1, n_pair, SSD_STATE, PAIR), lambda b, c: (b, cidx(c), 0, 0, 0)),
        pl.BlockSpec((1, BLK, D), lambda b, c: (b, cidx(c), 0)),
    ]

    def body(xr_ref, bcr_ref, xh_ref, bch_ref, dtr_ref, cwx_ref, cwb_ref, cbx_ref, cbb_ref, dtb_ref, alog_ref,
             dsk_ref, sin_ref, dy_ref, dxbc_ref, dcwx_ref, dcwb_ref, dcbx_ref, dcbb_ref, misc_ref,
             dst_s, extx_s, extb_s, acs_s, acsT_s, dux_s, dub_s, e2x_s, e2b_s, nxx_s, nxb_s):
        step = pl.program_id(1)
        first = step == nc - 1
        last = step == 0

        @pl.when(last)
        def _():
            dst_s[...] = jnp.zeros_like(dst_s)
            nxx_s[...] = jnp.zeros_like(nxx_s)
            nxb_s[...] = jnp.zeros_like(nxb_s)

        @pl.when(last & (pl.program_id(0) == 0))
        def _():
            for r in (dcwx_ref, dcwb_ref, dcbx_ref, dcbb_ref, misc_ref):
                r[...] = jnp.zeros_like(r)

        lane, lane1 = _iota((BLK, LANES), 1), _iota((1, LANES), 1)
        lo, lo1 = lane < HEAD_DIM, lane1 < HEAD_DIM
        tri = _iota((BLK, BLK), 1) <= _iota((BLK, BLK), 0)
        prex = _conv_pre(extx_s, xh_ref, xr_ref, cwx_ref, cbx_ref, first)
        sgx = _sigmoid(prex)
        ux = prex * sgx
        preb = _conv_pre(extb_s, bch_ref, bcr_ref, cwb_ref, cbb_ref, first)
        sgb = _sigmoid(preb)
        ub = preb * sgb
        dtv, a, acs = _ssd_common(dtr_ref, dtb_ref, alog_ref, acs_s, acsT_s)
        dacs = jnp.zeros((BLK, LANES), F32)
        dlast = jnp.zeros((1, LANES), F32)
        ddt = jnp.zeros((BLK, LANES), F32)
        dd = jnp.zeros((1, LANES), F32)
        for g in range(SSD_GROUPS):
            bg = ub[:, g * SSD_STATE:(g + 1) * SSD_STATE]
            cg = ub[:, (SSD_GROUPS + g) * SSD_STATE:(SSD_GROUPS + g + 1) * SSD_STATE]
            bb, cb_ = bg.astype(BF16), cg.astype(BF16)
            cbm = _dot_nt(cb_, bb)
            cbt = _dot_nt(bb, cb_)
            ctb = cg.T.astype(BF16)
            dbg = jnp.zeros((BLK, SSD_STATE), F32)
            dcg = jnp.zeros((BLK, SSD_STATE), F32)
            for pr in range(g * pairs_per_group, (g + 1) * pairs_per_group):
                t = _pair_terms(pr, acs, dtv, acs_s, lane, lo, lane1, lo1)
                h0, h1 = t["h"]
                xs_p = ux[:, pr * PAIR:(pr + 1) * PAIR]
                dy_p = dy_ref[0, :, pr * PAIR:(pr + 1) * PAIR]
                x_p = xs_p * t["dt_p"]
                ea_p = jnp.exp(t["acs_p"])
                dte_p = jnp.exp(t["last_p"] - t["acs_p"])
                cd_p = jnp.exp(t["last_p"])
                st = sin_ref[0, 0, pr]
                dst = dst_s[pr]
                stb, dstb = st.astype(BF16), dst.astype(BF16)
                s0, s1 = _half_sums(_colsum(dy_p * xs_p), lo1)
                dd = dd + jnp.where(lane1 == h0, s0, 0.0) + jnp.where(lane1 == h1, s1, 0.0)
                d_p = jnp.where(lo1, _lane_col(dsk_ref[...], lane1, h0), _lane_col(dsk_ref[...], lane1, h1))
                dxs_p = d_p * dy_p
                dp = dy_p * ea_p
                dpb = dp.astype(BF16)
                yo = dp * _dot(cb_, stb)
                dcg = dcg + _dot_nt(dpb, stb)
                dst_off = _dot(ctb, dpb)
                dac = [_rowsum(jnp.where(lo, yo, 0.0)), _rowsum(jnp.where(lo, 0.0, yo))]
                s0, s1 = _half_sums(_colsum(dst * st), lo1)
                dl = [s0 * jnp.exp(t["last"][0]), s1 * jnp.exp(t["last"][1])]
                dxd = _dot(bb, dstb)
                dx_p = dxd * dte_p
                tt = dxd * x_p
                dbg = dbg + _dot_nt((x_p * dte_p).astype(BF16), dstb)
                for k, ddte in enumerate((_rowsum(jnp.where(lo, tt, 0.0)), _rowsum(jnp.where(lo, 0.0, tt)))):
                    ek = ddte * jnp.exp(t["last"][k] - t["c"][k])
                    dl[k] = dl[k] + _colsum(ek)
                    dac[k] = dac[k] - ek
                x_pb = x_p.astype(BF16)
                for k in range(2):
                    row = acsT_s[t["h"][k]:t["h"][k] + 1, :]
                    lm, lmt = _decay_tiles(t["c"][k], row, tri, True)
                    dym = jnp.where(lo if k == 0 else ~lo, dy_p, 0.0).astype(BF16)
                    dm = _dot_nt(dym, x_pb)
                    dmt = _dot_nt(x_pb, dym)
                    mt = cbt * lmt
                    dx_p = dx_p + _dot(mt.astype(BF16), dym)
                    dac[k] = dac[k] + _rowsum(dm * (cbm * lm)) - _rowsum(dmt * mt)
                    dcg = dcg + _dot((dm * lm).astype(BF16), bb)
                    dbg = dbg + _dot((dmt * lmt).astype(BF16), cb_)
                dacs = dacs + jnp.where(lane == h0, dac[0], 0.0) + jnp.where(lane == h1, dac[1], 0.0)
                dlast = dlast + jnp.where(lane1 == h0, dl[0], 0.0) + jnp.where(lane1 == h1, dl[1], 0.0)
                dxs_p = dxs_p + dx_p * t["dt_p"]
                t3 = dx_p * xs_p
                ddt = ddt + jnp.where(lane == h0, _rowsum(jnp.where(lo, t3, 0.0)), 0.0) \
                    + jnp.where(lane == h1, _rowsum(jnp.where(lo, 0.0, t3)), 0.0)
                dux_s[:, pr * PAIR:(pr + 1) * PAIR] = dxs_p
                dst_s[pr] = dst * cd_p + dst_off
            dub_s[:, g * SSD_STATE:(g + 1) * SSD_STATE] = dbg
            dub_s[:, (SSD_GROUPS + g) * SSD_STATE:(SSD_GROUPS + g + 1) * SSD_STATE] = dcg
        dacs = dacs + jnp.where(_iota((BLK, LANES), 0) == BLK - 1, dlast, 0.0)
        utri = jnp.where(_iota((BLK, BLK), 1) >= _iota((BLK, BLK), 0), 1.0, 0.0).astype(BF16)
        dda = _dot_split(utri, dacs)
        ddt = ddt + dda * a
        ddtr = jnp.where(lane < n_heads, ddt * _sigmoid(dtr_ref[0] + dtb_ref[...]), 0.0)
        dxbc_ref[0, :, CD:CD + LANES] = ddtr.astype(BF16)
        dxbc_ref[0, :, CD + LANES:tail] = jnp.zeros((BLK, tail - CD - LANES), BF16)
        misc_ref[0:1, :] += _colsum(ddtr)
        misc_ref[1:2, :] += jnp.where(lane1 < n_heads, _colsum(dda * dtv) * a, 0.0)
        misc_ref[2:3, :] += dd
        for (du_s, pre, sg, ext_s, e2_s, nx_s, w_ref, dcw_ref, dcb_ref, c0, width) in (
                (dux_s, prex, sgx, extx_s, e2x_s, nxx_s, cwx_ref, dcwx_ref, dcbx_ref, 0, D),
                (dub_s, preb, sgb, extb_s, e2b_s, nxb_s, cwb_ref, dcwb_ref, dcbb_ref, D, 512)):
            dpre = du_s[...] * (sg * (1.0 + pre * (1.0 - sg)))
            dcb_ref[...] += _colsum(dpre)
            for i in range(CONV_K):
                dcw_ref[i:i + 1, :] += _colsum(dpre * ext_s[pl.ds(HALO - (CONV_K - 1 - i), BLK), :])
            e2_s[0:BLK, :] = dpre
            e2_s[BLK:BLK + HALO, :] = nx_s[...]
            dxr = jnp.zeros((BLK, width), F32)
            for i in range(CONV_K):
                dxr = dxr + e2_s[pl.ds(CONV_K - 1 - i, BLK), :] * w_ref[i:i + 1, :]
            dxbc_ref[0, :, c0:c0 + width] = dxr.astype(BF16)
            nx_s[...] = e2_s[0:HALO, :]

    full = lambda shape: pl.BlockSpec(shape, lambda b, c: (0,) * len(shape))
    return pl.pallas_call(
        body,
        name="ssd_bwd",
        grid=(Bl, nc),
        in_specs=specs,
        out_specs=[
            pl.BlockSpec((1, BLK, tail), lambda b, c: (b, cidx(c), 0)),
            full((CONV_K, D)), full((CONV_K, 512)), full((1, D)), full((1, 512)), full((8, LANES)),
        ],
        out_shape=[
            jax.ShapeDtypeStruct((Bl, L, tail), BF16),
            jax.ShapeDtypeStruct((CONV_K, D), F32), jax.ShapeDtypeStruct((CONV_K, 512), F32),
            jax.ShapeDtypeStruct((1, D), F32), jax.ShapeDtypeStruct((1, 512), F32),
            jax.ShapeDtypeStruct((8, LANES), F32),
        ],
        scratch_shapes=[
            pltpu.VMEM((n_pair, SSD_STATE, PAIR), F32),
            pltpu.VMEM((HALO + BLK, D), F32), pltpu.VMEM((HALO + BLK, 512), F32),
            pltpu.VMEM((BLK, LANES), F32), pltpu.VMEM((LANES, BLK), F32),
            pltpu.VMEM((BLK, D), F32), pltpu.VMEM((BLK, 512), F32),
            pltpu.VMEM((BLK + HALO, D), F32), pltpu.VMEM((BLK + HALO, 512), F32),
            pltpu.VMEM((HALO, D), F32), pltpu.VMEM((HALO, 512), F32),
        ],
        compiler_params=_params(("arbitrary", "arbitrary")),
    )(proj3, proj3, proj3, proj3, proj3, cwx, cwb, cbx, cbb, dtb, alog, dsk, s_in, dy3)


def _gate_out(x2, tgt2, o2, proj2, y2, sbw, ssw, w_out_bf, w_out_t):
    T, D = x2.shape
    tm = min(256, T)

    def body(x_ref, t_ref, o_ref, zs_ref, y_ref, zy_ref, sbw_ref, ssw_ref, wo_ref, wot_ref,
             dout_ref, doutb_ref, mixt_ref, do_ref, dy_ref, dz_ref, dnw_ref, loss_ref):
        @pl.when(pl.program_id(0) == 0)
        def _():
            dnw_ref[...] = jnp.zeros_like(dnw_ref)
            loss_ref[...] = jnp.zeros_like(loss_ref)

        def fwd(o, z, w):
            sg = _sigmoid(z)
            sl = z * sg
            g = o * sl
            r = lax.rsqrt(jnp.mean(g * g, axis=-1, keepdims=True) + EPS)
            n = g * r
            return sg, sl, r, n, n * w

        def bwd(dy, o, z, w, sg, sl, r, n):
            dn = dy * w
            dg = r * (dn - n * jnp.mean(dn * n, axis=-1, keepdims=True))
            return dg * sl, dg * o * (sg * (1.0 + z * (1.0 - sg))), _colsum(dy * n)

        o1, z1, w1 = o_ref[...], zs_ref[...], sbw_ref[...]
        o2_, z2, w2 = y_ref[...], zy_ref[...], ssw_ref[...]
        sg1, sl1, r1, n1, y1 = fwd(o1, z1, w1)
        sg2, sl2, r2, n2, y2_ = fwd(o2_, z2, w2)
        y1b, y2b = y1.astype(BF16), y2_.astype(BF16)
        mixt_ref[0:D, :] = y1.T.astype(BF16)
        mixt_ref[D:2 * D, :] = y2_.T.astype(BF16)
        out = x_ref[...] + (_dot(y1b, wo_ref[0:D, :]) + _dot(y2b, wo_ref[D:2 * D, :]))
        err = out - t_ref[...]
        loss_ref[...] += jnp.sum(err * err) * (0.5 / D)
        dout = err * (1.0 / D)
        dout_ref[...] = dout
        doutb = dout.astype(BF16)
        doutb_ref[...] = doutb
        do1, dz1, dw1 = bwd(_dot(doutb, wot_ref[:, 0:D]), o1, z1, w1, sg1, sl1, r1, n1)
        do2, dz2, dw2 = bwd(_dot(doutb, wot_ref[:, D:2 * D]), o2_, z2, w2, sg2, sl2, r2, n2)
        do_ref[...] = do1
        dy_ref[...] = do2
        dz_ref[:, 0:D] = dz1.astype(BF16)
        dz_ref[:, D:2 * D] = dz2.astype(BF16)
        dnw_ref[0:1, :] += dw1
        dnw_ref[1:2, :] += dw2

    row = lambda col: pl.BlockSpec((tm, D), lambda i: (i, col))
    full = lambda shape: pl.BlockSpec(shape, lambda i: (0,) * len(shape))
    wide = pl.BlockSpec((tm, 2 * D), lambda i: (i, 0))
    return pl.pallas_call(
        body,
        name="gate_out",
        grid=(T // tm,),
        in_specs=[row(0), row(0), row(0), row(3), row(0), row(4), full((1, D)), full((1, D)), full((2 * D, D)),
                  full((D, 2 * D))],
        out_specs=[row(0), row(0), pl.BlockSpec((2 * D, tm), lambda i: (0, i)), row(0), row(0), wide,
                   full((8, D)), full((8, LANES))],
        out_shape=[
            jax.ShapeDtypeStruct((T, D), F32), jax.ShapeDtypeStruct((T, D), BF16),
            jax.ShapeDtypeStruct((2 * D, T), BF16), jax.ShapeDtypeStruct((T, D), F32),
            jax.ShapeDtypeStruct((T, D), F32), jax.ShapeDtypeStruct((T, 2 * D), BF16),
            jax.ShapeDtypeStruct((8, D), F32), jax.ShapeDtypeStruct((8, LANES), F32),
        ],
        compiler_params=_params(("arbitrary",)),
    )(x2, tgt2, o2, proj2, y2, proj2, sbw, ssw, w_out_bf, w_out_t)


def _piece_blocks(pieces, D):
    counts = [p.shape[1] // D for p in pieces]
    return [sum(counts[:i]) for i in range(len(counts))], counts


def _dhn(pieces, w_pad_t, x2, dout, norm_w, h_in, h_out, slab_off, slab_w):
    T, D = x2.shape
    tm = min(1024, T)
    starts, counts = _piece_blocks(pieces, D)
    nk = sum(counts)
    ni = T // tm
    n_sem = 2 * (N_CHIPS - 1)
    assert nk * D == w_pad_t.shape[0]

    def body(*refs):
        p_refs = refs[:len(pieces)]
        (w_ref, x_hbm, dout_hbm, nw_ref, hin, hout, gx_ref, dnw_ref, rin, rout,
         acc_s, x_s, dout_s, send_sems, recv_sems, row_sems, own_sems) = refs[len(pieces):]
        i, k = pl.program_id(0), pl.program_id(1)

        def rows():
            r0 = pl.multiple_of(i * tm, tm)
            return [pltpu.make_async_copy(src.at[pl.ds(r0, tm)], dst, row_sems.at[n])
                    for n, (src, dst) in enumerate(((x_hbm, x_s), (dout_hbm, dout_s)))]

        @pl.when(k == 0)
        def _():
            for cp in rows():
                cp.start()

        def scatter():
            x, y, c, chips = _place()

            def slab(p):
                return hin.at[:, pl.ds(pl.multiple_of(p * slab_off, LANES), slab_w)]

            cps = []
            for j, (px, py) in enumerate(chips):
                for m, (src, dst) in enumerate(((slab(2 * px + py), rin.at[j]), (hout.at[2 * px + py], rout.at[j]))):
                    cps.append(pltpu.make_async_remote_copy(
                        src_ref=src, dst_ref=dst, send_sem=send_sems.at[2 * j + m], recv_sem=recv_sems.at[2 * j + m],
                        device_id=(px, py, c), device_id_type=MESH))
            me = 2 * x + y
            own = [pltpu.make_async_copy(slab(me), rin.at[N_CHIPS - 1], own_sems.at[0]),
                   pltpu.make_async_copy(hout.at[me], rout.at[N_CHIPS - 1], own_sems.at[1])]
            return cps + own

        @pl.when((i == 0) & (k == 0))
        def _():
            for cp in scatter():
                cp.start()

        @pl.when((i == ni - 1) & (k == nk - 1))
        def _():
            for cp in scatter():
                cp.wait()

        @pl.when((i == 0) & (k == 0))
        def _():
            dnw_ref[...] = jnp.zeros_like(dnw_ref)

        @pl.when(k == 0)
        def _():
            acc_s[...] = jnp.zeros_like(acc_s)

        for p_ref, s, n in zip(p_refs, starts, counts):
            @pl.when((k >= s) & (k < s + n))
            def _(p_ref=p_ref):
                acc_s[...] += _dot(p_ref[...], w_ref[...])

        @pl.when(k == nk - 1)
        def _():
            for cp in rows():
                cp.wait()
            xv = x_s[...]
            r = lax.rsqrt(jnp.mean(xv * xv, axis=-1, keepdims=True) + EPS)
            xh = xv * r
            dhn = acc_s[...]
            dxh = dhn * nw_ref[...]
            gx_ref[...] = dout_s[...] + r * (dxh - xh * jnp.mean(dxh * xh, axis=-1, keepdims=True))
            dnw_ref[0:1, :] += _colsum(dhn * xh)

    return pl.pallas_call(
        body,
        name="dhn",
        grid=(T // tm, nk),
        in_specs=[pl.BlockSpec((tm, D), lambda i, k, s=s, n=n: (i, jnp.clip(k - s, 0, n - 1)))
                  for s, n in zip(starts, counts)] + [
            pl.BlockSpec((D, D), lambda i, k: (k, 0)),
            ANY, ANY,
            pl.BlockSpec((1, D), lambda i, k: (0, 0)),
            ANY, ANY,
        ],
        out_specs=[pl.BlockSpec((tm, D), lambda i, k: (i, 0)), pl.BlockSpec((8, D), lambda i, k: (0, 0)), ANY, ANY],
        out_shape=[jax.ShapeDtypeStruct((T, D), F32), jax.ShapeDtypeStruct((8, D), F32),
                   jax.ShapeDtypeStruct((N_CHIPS, h_in.shape[0], slab_w), F32),
                   jax.ShapeDtypeStruct((N_CHIPS,) + h_out.shape[1:], F32)],
        scratch_shapes=[pltpu.VMEM((tm, D), F32)] * 3 + [pltpu.SemaphoreType.DMA((n_sem,)), pltpu.SemaphoreType.DMA((n_sem,)),
                                                      pltpu.SemaphoreType.DMA((2,)), pltpu.SemaphoreType.DMA((2,))],
        compiler_params=_params(("arbitrary", "arbitrary")),
    )(*pieces, w_pad_t, x2, dout, norm_w, h_in, h_out)


def _grad_w_in(hn_t, pieces):
    D, T = hn_t.shape
    tk = min(1024, T)
    starts, counts = _piece_blocks(pieces, D)

    def body(*refs):
        a_ref, p_refs, o_ref = refs[0], refs[1:-1], refs[-1]
        j = pl.program_id(0)

        @pl.when(pl.program_id(1) == 0)
        def _():
            o_ref[...] = jnp.zeros_like(o_ref)

        for p_ref, s, n in zip(p_refs, starts, counts):
            @pl.when((j >= s) & (j < s + n))
            def _(p_ref=p_ref):
                o_ref[...] += _dot(a_ref[...], p_ref[...])

    def piece_spec(s, n):
        return pl.BlockSpec((tk, D), lambda j, k: (jnp.where((j >= s) & (j < s + n), k, 0), jnp.clip(j - s, 0, n - 1)))

    return pl.pallas_call(
        body,
        name="grad_w_in",
        grid=(sum(counts), T // tk),
        in_specs=[pl.BlockSpec((D, tk), lambda j, k: (0, k))] + [piece_spec(s, n) for s, n in zip(starts, counts)],
        out_specs=pl.BlockSpec((D, D), lambda j, k: (0, j)),
        out_shape=jax.ShapeDtypeStruct((D, sum(counts) * D), F32),
        compiler_params=_params(("parallel", "arbitrary")),
    )(hn_t, *pieces)


def _grad_w_out(a, b, g_in, width):
    M, K = a.shape
    N = b.shape[1]
    tm = min(1024, M)
    tn = 1024 if N % 1024 == 0 else (512 if N % 512 == 0 else N)
    tk = min(512, K)
    grid = (M // tm, N // tn, K // tk)
    h = g_in.shape[0] // 2

    def body(a_ref, b_ref, gin, o_ref, rin, send_sem, recv_sem):
        ids = [pl.program_id(d) for d in range(3)]

        def swap():
            x, y, c, _ = _place()
            return pltpu.make_async_remote_copy(
                src_ref=gin.at[pl.ds((1 - c) * h, h), pl.ds(0, width)], dst_ref=rin, send_sem=send_sem, recv_sem=recv_sem,
                device_id=(x, y, 1 - c), device_id_type=MESH)

        @pl.when((ids[0] == 0) & (ids[1] == 0) & (ids[2] == 0))
        def _():
            swap().start()

        @pl.when(ids[2] == 0)
        def _():
            o_ref[...] = jnp.zeros_like(o_ref)

        o_ref[...] += _dot(a_ref[...], b_ref[...])

        @pl.when((ids[0] == grid[0] - 1) & (ids[1] == grid[1] - 1) & (ids[2] == grid[2] - 1))
        def _():
            swap().wait()

    return pl.pallas_call(
        body,
        name="grad_w_out",
        grid=grid,
        in_specs=[pl.BlockSpec((tm, tk), lambda i, j, k: (i, k)), pl.BlockSpec((tk, tn), lambda i, j, k: (k, j)), ANY],
        out_specs=[pl.BlockSpec((tm, tn), lambda i, j, k: (i, j)), ANY],
        out_shape=[jax.ShapeDtypeStruct((M, N), F32), jax.ShapeDtypeStruct((h, width), F32)],
        scratch_shapes=[pltpu.SemaphoreType.DMA, pltpu.SemaphoreType.DMA],
        compiler_params=_params(("arbitrary", "arbitrary", "arbitrary")),
    )(a, b, g_in)


def _adamw(w, g, m, v, name):
    R, C = w.shape
    tr = 256 if R % 256 == 0 else R
    c1 = 1.0 - ADAM_B1 ** ADAM_STEP
    c2 = 1.0 - ADAM_B2 ** ADAM_STEP

    def body(w_ref, g_ref, m_ref, v_ref, d_ref, nm_ref, nv_ref):
        gv = g_ref[...]
        m_new = ADAM_B1 * m_ref[...] + (1.0 - ADAM_B1) * gv
        v_new = ADAM_B2 * v_ref[...] + (1.0 - ADAM_B2) * (gv * gv)
        d_ref[...] = -ADAM_LR * ((m_new / c1) / (jnp.sqrt(v_new / c2) + ADAM_EPS) + ADAM_WD * w_ref[...])
        nm_ref[...] = m_new
        nv_ref[...] = v_new

    spec = pl.BlockSpec((tr, C), lambda i: (i, 0))
    return pl.pallas_call(
        body,
        name=name,
        grid=(R // tr,),
        in_specs=[spec] * 4,
        out_specs=[spec] * 3,
        out_shape=[jax.ShapeDtypeStruct((R, C), F32)] * 3,
        compiler_params=_params(("parallel",)),
    )(w, g, m, v)


def _add_core_rows(g, recv, core, name):
    h, width = recv.shape
    th = 128 if h % 128 == 0 else h

    def body(c_ref, g_ref, r_ref, o_ref):
        o_ref[...] = g_ref[...] + r_ref[...]

    return pl.pallas_call(
        body,
        name=name,
        grid_spec=pltpu.PrefetchScalarGridSpec(
            num_scalar_prefetch=1,
            grid=(h // th,),
            in_specs=[
                pl.BlockSpec((th, width), lambda i, c: (c[0] * (h // th) + i, 0)),
                pl.BlockSpec((th, width), lambda i, c: (i, 0)),
            ],
            out_specs=pl.BlockSpec((th, width), lambda i, c: (i, 0)),
        ),
        out_shape=jax.ShapeDtypeStruct((h, width), F32),
        compiler_params=_params(("parallel",)),
    )(core, g, recv)


def _add_core_blocks(g, recv, core, name):
    n, hb, C = recv.shape

    def body(c_ref, g_ref, r_ref, o_ref):
        o_ref[...] = g_ref[...] + r_ref[...]

    return pl.pallas_call(
        body,
        name=name,
        grid_spec=pltpu.PrefetchScalarGridSpec(
            num_scalar_prefetch=1,
            grid=(n,),
            in_specs=[
                pl.BlockSpec((hb, C), lambda p, c: (2 * p + c[0], 0)),
                pl.BlockSpec((None, hb, C), lambda p, c: (p, 0, 0)),
            ],
            out_specs=pl.BlockSpec((None, hb, C), lambda p, c: (p, 0, 0)),
        ),
        out_shape=jax.ShapeDtypeStruct((n, hb, C), F32),
        compiler_params=_params(("parallel",)),
    )(core, g, recv)


def _add_chips(recv, name):
    _, h, W = recv.shape
    th = 256 if h % 256 == 0 else h

    def body(r_ref, o_ref):
        o_ref[...] = ((r_ref[N_CHIPS - 1] + r_ref[0]) + r_ref[1]) + r_ref[2]

    return pl.pallas_call(
        body,
        name=name,
        grid=(h // th,),
        in_specs=[pl.BlockSpec((N_CHIPS, th, W), lambda i: (0, i, 0))],
        out_specs=pl.BlockSpec((th, W), lambda i: (i, 0)),
        out_shape=jax.ShapeDtypeStruct((h, W), F32),
        compiler_params=_params(("parallel",)),
    )(recv)


def _place():
    x, y, c = lax.axis_index("x"), lax.axis_index("y"), lax.axis_index("c")
    other_chips = [(1 - x, y), (x, 1 - y), (1 - x, 1 - y)]
    return x, y, c, other_chips


def _allgather_w_in(w_in_bf, x2, norm_w):
    D, S = w_in_bf.shape
    T = x2.shape[0]
    tm = min(1024, T)
    ni = T // tm
    n_ici = n_fwd = N_CHIPS - 1

    def body(win, x_ref, nw_ref, gin, hn_ref, hnt_ref, send_sems, recv_sems):
        step = pl.program_id(0)
        xv = x_ref[...]
        hn = xv * lax.rsqrt(jnp.mean(xv * xv, axis=-1, keepdims=True) + EPS) * nw_ref[...]
        hn_ref[...] = hn.astype(BF16)
        hnt_ref[...] = hn.T.astype(BF16)
        x, y, c, chips = _place()
        me = 2 * x + y
        sibling = (x, y, 1 - c)
        hin = D // 2

        def half(chip_idx, core):
            return gin.at[chip_idx, pl.ds(core * hin, hin)]

        def rcopy(k, src, dst, to):
            return pltpu.make_async_remote_copy(src_ref=src, dst_ref=dst, send_sem=send_sems.at[k],
                                                recv_sem=recv_sems.at[k], device_id=to, device_id_type=MESH)

        def sends():
            return [rcopy(j, win.at[pl.ds(c * hin, hin)], half(me, c), (*chip, c)) for j, chip in enumerate(chips)]

        @pl.when(step == 0)
        def _():
            for cp in sends():
                cp.start()

        @pl.when(step == ni - 1)
        def _():
            passed = []
            for j, (px, py) in enumerate(chips):
                theirs = half(2 * px + py, c)
                rcopy(j, theirs, theirs, sibling).wait_recv()
                passed.append(rcopy(n_ici + j, theirs, theirs, sibling))
                passed[-1].start()
            for j, (px, py) in enumerate(chips):
                other = half(2 * px + py, 1 - c)
                rcopy(n_ici + j, other, other, sibling).wait_recv()
            for cp in sends() + passed:
                cp.wait_send()

    return pl.pallas_call(
        body,
        name="allgather_w_in",
        grid=(ni,),
        in_specs=[ANY, pl.BlockSpec((tm, D), lambda i: (i, 0)), pl.BlockSpec((1, D), lambda i: (0, 0))],
        out_specs=[ANY, pl.BlockSpec((tm, D), lambda i: (i, 0)), pl.BlockSpec((D, tm), lambda i: (0, i))],
        out_shape=[jax.ShapeDtypeStruct((N_CHIPS, D, S), BF16),
                   jax.ShapeDtypeStruct((T, D), BF16), jax.ShapeDtypeStruct((D, T), BF16)],
        scratch_shapes=[pltpu.SemaphoreType.DMA((n_ici + n_fwd,)), pltpu.SemaphoreType.DMA((n_ici + n_fwd,))],
        compiler_params=_params(("arbitrary",)),
    )(w_in_bf, x2, norm_w)


def _allreduce_small(packed):
    R = packed.shape[0]
    n_dev = 2 * N_CHIPS

    def body(p_ref, o_ref, buf, send_sems, recv_sems):
        x, y, c, _ = _place()
        me = 4 * x + 2 * y + c
        buf[me] = p_ref[...]
        copies = []
        for k in range(1, n_dev):
            px = 1 - x if k & 4 else x
            py = 1 - y if k & 2 else y
            pc = 1 - c if k & 1 else c
            copies.append((pltpu.make_async_remote_copy(
                src_ref=buf.at[me], dst_ref=buf.at[me], send_sem=send_sems.at[k - 1], recv_sem=recv_sems.at[k - 1],
                device_id=(px, py, pc), device_id_type=MESH), 4 * px + 2 * py + pc, (px, py, pc)))
        for cp, _, _ in copies:
            cp.start()
        for k, (_, peer, to) in enumerate(copies):
            pltpu.make_async_remote_copy(
                src_ref=buf.at[peer], dst_ref=buf.at[peer], send_sem=send_sems.at[k], recv_sem=recv_sems.at[k],
                device_id=to, device_id_type=MESH).wait_recv()
        for cp, _, _ in copies:
            cp.wait_send()
        acc = buf[0]
        for d in range(1, n_dev):
            acc = acc + buf[d]
        o_ref[...] = acc

    vm = pl.BlockSpec(memory_space=pltpu.VMEM)
    return pl.pallas_call(
        body,
        name="allreduce_small",
        in_specs=[vm],
        out_specs=vm,
        out_shape=jax.ShapeDtypeStruct((R, LANES), F32),
        scratch_shapes=[pltpu.VMEM((n_dev, R, LANES), F32), pltpu.SemaphoreType.DMA((n_dev - 1,)),
                        pltpu.SemaphoreType.DMA((n_dev - 1,))],
    )(packed)


def _swap_core_halves(g_out):
    hb = g_out.shape[0] // (2 * N_CHIPS)

    def body(gout, rout, send_sems, recv_sems):
        x, y, c, _ = _place()
        cps = [pltpu.make_async_remote_copy(
            src_ref=gout.at[pl.ds((2 * p + 1 - c) * hb, hb)], dst_ref=rout.at[p], send_sem=send_sems.at[p],
            recv_sem=recv_sems.at[p], device_id=(x, y, 1 - c), device_id_type=MESH) for p in range(N_CHIPS)]
        for cp in cps:
            cp.start()
        for cp in cps:
            cp.wait()

    return pl.pallas_call(
        body,
        name="reduce_core_swap",
        in_specs=[ANY],
        out_specs=ANY,
        out_shape=jax.ShapeDtypeStruct((N_CHIPS, hb, g_out.shape[1]), F32),
        scratch_shapes=[pltpu.SemaphoreType.DMA((N_CHIPS,)), pltpu.SemaphoreType.DMA((N_CHIPS,))],
    )(g_out)


def _join_core_halves(g_in, g_out):
    def body(gin, gout, fin, fout, send_sems, recv_sems):
        x, y, c, _ = _place()
        cps = [pltpu.make_async_remote_copy(src_ref=s, dst_ref=d.at[c], send_sem=send_sems.at[k],
                                            recv_sem=recv_sems.at[k], device_id=(x, y, 1 - c), device_id_type=MESH)
               for k, (s, d) in enumerate(((gin, fin), (gout, fout)))]
        for cp in cps:
            cp.start()
        for k, (s, d) in enumerate(((gin, fin), (gout, fout))):
            pltpu.make_async_remote_copy(src_ref=s, dst_ref=d.at[1 - c], send_sem=send_sems.at[k],
                                         recv_sem=recv_sems.at[k], device_id=(x, y, 1 - c),
                                         device_id_type=MESH).wait_recv()
        for cp in cps:
            cp.wait_send()

    return pl.pallas_call(
        body,
        name="reduce_core_join",
        in_specs=[ANY, ANY],
        out_specs=[ANY, ANY],
        out_shape=[jax.ShapeDtypeStruct((2,) + g_in.shape, F32), jax.ShapeDtypeStruct((2,) + g_out.shape, F32)],
        scratch_shapes=[pltpu.SemaphoreType.DMA((2,)), pltpu.SemaphoreType.DMA((2,))],
    )(g_in, g_out)


def _pack(arrays):
    rows = []
    for a in arrays:
        flat = a.reshape(-1).astype(F32)
        n = -(-flat.shape[0] // LANES) * LANES
        rows.append(jnp.pad(flat, (0, n - flat.shape[0])).reshape(-1, LANES))
    out = jnp.concatenate(rows, axis=0)
    return jnp.pad(out, ((0, -out.shape[0] % 8), (0, 0)))


def _unpack(packed, shapes):
    out, r = [], 0
    for shp in shapes:
        n = math.prod(shp)
        nr = -(-n // LANES)
        out.append(packed[r:r + nr].reshape(-1)[:n].reshape(shp))
        r += nr
    return out


def _pad_lanes(a):
    return jnp.pad(a, ((0, 0), (0, LANES - a.shape[1])))


def kernel(x, norm_w, w_in, q_norm_w, k_norm_w, conv_w, conv_b, dt_bias, A_log, D_skip, sb_norm_w, ssd_norm_w, w_out, loss_target, m_norm_w, m_w_in, m_q_norm_w, m_k_norm_w, m_conv_w, m_conv_b, m_dt_bias, m_A_log, m_D_skip, m_sb_norm_w, m_ssd_norm_w, m_w_out, v_norm_w, v_w_in, v_q_norm_w, v_k_norm_w, v_conv_w, v_conv_b, v_dt_bias, v_A_log, v_D_skip, v_sb_norm_w, v_ssd_norm_w, v_w_out):
    Bl, L, D = x.shape
    T = Bl * L
    S = w_in.shape[2]
    R = w_out.shape[1]
    CW = conv_w.shape[2]
    n_in = N_CHIPS * S
    CD = D + 2 * SSD_GROUPS * SSD_STATE
    H = D // HEAD_DIM
    n_main = 6 * D + 512
    P = -(-(n_main + LANES) // 1024) * 1024
    assert n_in == n_main + H and CD == N_CHIPS * CW and 2 * D == N_CHIPS * R and CD == D + 512
    chip = (2 * lax.axis_index("x") + lax.axis_index("y")).astype(jnp.int32)
    core = lax.axis_index("c").astype(jnp.int32)

    w_in_bf, w_out_shard_bf = w_in[0].astype(BF16), w_out[0].astype(BF16)
    x2 = x.reshape(T, D)
    g_in, hn, hn_t = _allgather_w_in(w_in_bf, x2, norm_w)
    g_in = lax.dynamic_update_slice(g_in, w_in_bf[None], (chip, 0, 0))
    w_pad = jnp.pad(g_in.transpose(1, 0, 2).reshape(D, n_in), ((0, 0), (0, P - n_in)))
    proj, w_pad_t, g_out, g_cw = _inproj(hn, w_pad, w_out_shard_bf, conv_w[0])
    g_out = lax.dynamic_update_slice(g_out, w_out_shard_bf[None], (chip, 0, 0))
    g_cw = lax.dynamic_update_slice(g_cw, conv_w, (chip, 0, 0))
    w_out_bf = g_out.reshape(2 * D, D)
    conv_full = g_cw.transpose(1, 0, 2).reshape(CONV_K, CD)
    cwx, cwb = conv_full[:, :D], conv_full[:, D:]
    cbx, cbb = conv_b[:, :D], conv_b[:, D:]
    dtb, alog, dsk = _pad_lanes(dt_bias), _pad_lanes(A_log), _pad_lanes(D_skip)
    qw2, kw2 = jnp.tile(q_norm_w, (1, 2)), jnp.tile(k_norm_w, (1, 2))

    proj3 = proj.reshape(Bl, L, P)
    o_sb = _attn_fwd(proj3, qw2, kw2, D)
    y_ssd, s_in = _ssd_fwd(proj3, cwx, cwb, cbx, cbb, dtb, alog, dsk, D)
    dout, dout_bf, mixed_t, do_sb, dy_ssd, dz_bf, dnw_out, loss_blk = _gate_out(
        x2, loss_target.reshape(T, D), o_sb.reshape(T, D), proj, y_ssd.reshape(T, D), sb_norm_w, ssd_norm_w, w_out_bf,
        w_out_bf.T)

    dq, dk, dv, dqkw = _attn_bwd(proj3, o_sb, do_sb.reshape(Bl, L, D), qw2, kw2, D)
    dtail, dcwx, dcwb, dcbx, dcbb, misc = _ssd_bwd(
        proj3, s_in, dy_ssd.reshape(Bl, L, D), cwx, cwb, cbx, cbb, dtb, alog, dsk, D, P - 5 * D)
    dproj = [dq.reshape(T, D), dk.reshape(T, D), dv.reshape(T, D), dz_bf, dtail.reshape(T, P - 5 * D)]
    gw_in = _grad_w_in(hn_t, dproj)

    slab_off = S // LANES * LANES
    slab_w = -(-(S + (N_CHIPS - 1) * (S - slab_off)) // LANES) * LANES
    width = (N_CHIPS - 1) * slab_off + slab_w
    assert n_in <= width <= P
    core1 = core.reshape(1)
    gw_out, r_in = _grad_w_out(mixed_t, dout_bf, gw_in, width)
    r_out = _swap_core_halves(gw_out)
    h_in = _add_core_rows(gw_in, r_in, core1, "sum_cores_w_in")
    h_out = _add_core_blocks(gw_out, r_out, core1, "sum_cores_w_out")
    grad_x2, dnw_in, s_in_, s_out_ = _dhn(dproj, w_pad_t, x2, dout, norm_w, h_in, h_out, slab_off, slab_w)
    gh_in = _add_chips(s_in_, "sum_chips_w_in")
    gh_out = _add_chips(s_out_, "sum_chips_w_out")
    f_in, f_out = _join_core_halves(gh_in, gh_out)
    g_slab = lax.dynamic_update_slice(f_in, gh_in[None], (core, 0, 0)).reshape(D, slab_w)
    g_w_in = lax.dynamic_slice(g_slab, (0, chip * (S - slab_off)), (D, S))
    g_w_out = lax.dynamic_update_slice(f_out, gh_out[None], (core, 0, 0)).reshape(R, D)

    small_shapes = [(1, D), (1, D), (1, D), (1, CD), (1, HEAD_DIM), (1, HEAD_DIM), (1, H), (1, H), (1, H)]
    g_small_local = [dnw_in[0:1], dnw_out[0:1], dnw_out[1:2], jnp.concatenate([dcbx, dcbb], axis=1),
                     dqkw[0:1, :HEAD_DIM] + dqkw[0:1, HEAD_DIM:], dqkw[1:2, :HEAD_DIM] + dqkw[1:2, HEAD_DIM:],
                     misc[0:1, :H], misc[1:2, :H], misc[2:3, :H]]
    packed = _pack(g_small_local + [jnp.concatenate([dcwx, dcwb], axis=1), loss_blk[0:1, 0:1]])
    red = _allreduce_small(packed)
    g_small = _unpack(red, small_shapes + [(CONV_K, CD), (1, 1)])
    g_conv_w = lax.dynamic_slice_in_dim(g_small[9], chip * CW, CW, axis=1)
    loss = g_small[10][0, 0]

    d_in, nm_in, nv_in = _adamw(w_in[0], g_w_in, m_w_in[0], v_w_in[0], "adamw_w_in")
    d_out, nm_out, nv_out = _adamw(w_out[0], g_w_out, m_w_out[0], v_w_out[0], "adamw_w_out")
    d_cw, nm_cw, nv_cw = _adamw(conv_w[0], g_conv_w, m_conv_w[0], v_conv_w[0], "adamw_conv_w")
    small_w = [norm_w, sb_norm_w, ssd_norm_w, conv_b, q_norm_w, k_norm_w, dt_bias, A_log, D_skip]
    small_m = [m_norm_w, m_sb_norm_w, m_ssd_norm_w, m_conv_b, m_q_norm_w, m_k_norm_w, m_dt_bias, m_A_log, m_D_skip]
    small_v = [v_norm_w, v_sb_norm_w, v_ssd_norm_w, v_conv_b, v_q_norm_w, v_k_norm_w, v_dt_bias, v_A_log, v_D_skip]
    d_s, nm_s, nv_s = _adamw(_pack(small_w), _pack(g_small[:9]), _pack(small_m), _pack(small_v), "adamw_small")
    d_s, nm_s, nv_s = (_unpack(t, small_shapes) for t in (d_s, nm_s, nv_s))

    def ordered(s, w_in_, conv_w_, w_out_):
        return [s[0], w_in_[None], s[4], s[5], conv_w_[None], s[3], s[6], s[7], s[8], s[1], s[2], w_out_[None]]

    return (loss, grad_x2.reshape(Bl, L, D),
            *ordered(g_small[:9], g_w_in, g_conv_w, g_w_out),
            *ordered(d_s, d_in, d_cw, d_out),
            *ordered(nm_s, nm_in, nm_cw, nm_out),
            *ordered(nv_s, nv_in, nv_cw, nv_out))
```

```python
import functools
import math

import jax
import jax.numpy as jnp
from jax import lax
from jax.experimental import pallas as pl
from jax.experimental.pallas import tpu as pltpu

F32 = jnp.float32
BF16 = jnp.bfloat16
EPS = 1e-6
HEAD_DIM = 64
PAIR = 2 * HEAD_DIM
LANES = 128
SSD_STATE = 128
SSD_GROUPS = 2
BLK = 128
PREP_BLOCKS = 8
Q_TOGETHER_FWD = 2
Q_TOGETHER_BWD = 2
FIRST_LEFT = 2
UNDERFLOW = -105.0
CONV_K = 4
HALO = 8
N_CHIPS = 4
ADAM_LR, ADAM_B1, ADAM_B2, ADAM_EPS, ADAM_WD, ADAM_STEP = 0.001, 0.9, 0.999, 1e-08, 0.01, 10
VMEM_LIMIT_V7X = 56 * 1024 * 1024
MESH = pl.DeviceIdType.MESH
ANY = pl.BlockSpec(memory_space=pl.ANY)
NT = (((1,), (1,)), ((), ()))


def _params(sem=None):
    kw = dict(vmem_limit_bytes=VMEM_LIMIT_V7X)
    if sem is not None:
        kw["dimension_semantics"] = sem
    return pltpu.CompilerParams(**kw)


def _dot(a, b):
    return jnp.dot(a, b, preferred_element_type=F32)


def _dot_nt(a, b):
    return lax.dot_general(a, b, NT, preferred_element_type=F32)


def _dot_split(m, x):
    hi = x.astype(BF16)
    lo = (x - hi.astype(F32)).astype(BF16)
    return _dot(m, hi) + _dot(m, lo)


def _iota(shape, dim):
    return lax.broadcasted_iota(jnp.int32, shape, dim)


def _rowsum(x):
    return jnp.sum(x, axis=1, keepdims=True)


def _colsum(x):
    return jnp.sum(x, axis=0, keepdims=True)


def _sigmoid(x):
    return 0.5 * jnp.tanh(0.5 * x) + 0.5


def _softplus(x):
    return jnp.maximum(x, 0.0) + jnp.log(1.0 + jnp.exp(-jnp.abs(x)))


def _inproj(hn, w_pad, w_out_bf, conv_w):
    T, D = hn.shape
    P = w_pad.shape[1]
    tm = min(1024, T)
    tn = 1024 if P % 1024 == 0 else 512
    ni, nj = T // tm, P // tn
    n_sem = 2 * (N_CHIPS - 1)

    def body(hn_ref, w_ref, wout, cw, proj_ref, wt_ref, gout, gcw, send_sems, recv_sems):
        def gather():
            x, y, c, chips = _place()
            me = 2 * x + y
            return [pltpu.make_async_remote_copy(
                src_ref=src, dst_ref=dst.at[me], send_sem=send_sems.at[2 * j + m], recv_sem=recv_sems.at[2 * j + m],
                device_id=(px, py, c), device_id_type=MESH)
                for j, (px, py) in enumerate(chips) for m, (src, dst) in enumerate(((wout, gout), (cw, gcw)))]

        @pl.when((pl.program_id(0) == 0) & (pl.program_id(1) == 0))
        def _():
            for cp in gather():
                cp.start()

        @pl.when((pl.program_id(0) == ni - 1) & (pl.program_id(1) == nj - 1))
        def _():
            for cp in gather():
                cp.wait()

        @pl.when(pl.program_id(0) == 0)
        def _():
            wt_ref[...] = w_ref[...].astype(F32).T.astype(BF16)

        proj_ref[...] = _dot(hn_ref[...], w_ref[...])

    return pl.pallas_call(
        body,
        name="inproj",
        grid=(T // tm, P // tn),
        in_specs=[
            pl.BlockSpec((tm, D), lambda i, j: (i, 0)),
            pl.BlockSpec((D, tn), lambda i, j: (0, j)),
            ANY, ANY,
        ],
        out_specs=[
            pl.BlockSpec((tm, tn), lambda i, j: (i, j)),
            pl.BlockSpec((tn, D), lambda i, j: (jnp.where(i == 0, j, nj - 1), 0)),
            ANY, ANY,
        ],
        out_shape=[jax.ShapeDtypeStruct((T, P), F32), jax.ShapeDtypeStruct((P, D), BF16),
                   jax.ShapeDtypeStruct((N_CHIPS,) + w_out_bf.shape, BF16),
                   jax.ShapeDtypeStruct((N_CHIPS,) + conv_w.shape, F32)],
        scratch_shapes=[pltpu.SemaphoreType.DMA((n_sem,)), pltpu.SemaphoreType.DMA((n_sem,))],
        compiler_params=_params(("arbitrary", "arbitrary")),
    )(hn, w_pad, w_out_bf, conv_w)


def _pair_ones():
    ri = ((_iota((2 * PAIR, PAIR), 0) % PAIR) >= HEAD_DIM).astype(jnp.int32)
    ci = (_iota((2 * PAIR, PAIR), 1) >= HEAD_DIM).astype(jnp.int32)
    return jnp.where(ri == ci, 1.0, 0.0).astype(BF16)


def _pair_rms(v, ones2):
    return lax.rsqrt(_split_dots([v * v], ones2)[0] * (1.0 / HEAD_DIM) + EPS)


def _pair_mean(v, ones2):
    return _split_dots([v], ones2)[0] * (1.0 / HEAD_DIM)


def _suffix_ones():
    ri = _iota((2 * BLK, 2 * BLK), 0) % BLK
    ci = _iota((2 * BLK, 2 * BLK), 1)
    return jnp.where((ci >= BLK) | (ri > ci), 1.0, 0.0).astype(BF16)


def _split_dots(xs, m2):
    his = [x.astype(BF16) for x in xs]
    los = [(x - hi.astype(F32)).astype(BF16) for x, hi in zip(xs, his)]
    return [_dot(jnp.concatenate([hi, lo], axis=1), m2) for hi, lo in zip(his, los)]


def _sb_tiles(streams, km_s, uo):
    tiles = [(s, u, h) for s, st in enumerate(streams) for u in range(len(st["kbs"])) for h in range(2)]
    z2s = {(s, u): _dot_nt(st["q"], km_s[kb]) for s, st in enumerate(streams) for u, kb in enumerate(st["kbs"])}
    zs = [z2s[s, u][:, h * BLK:(h + 1) * BLK] for s, u, h in tiles]
    es = [jnp.exp(-jnp.abs(z)) for z in zs]
    las = [jnp.minimum(z, 0.0) - jnp.log(1.0 + e) for z, e in zip(zs, es)]
    lns = [a - z for a, z in zip(las, zs)]
    masks = [streams[s]["masks"][u] for s, u, h in tiles]
    lks = [lk if m is None else jnp.where(m, lk, 0.0) for m, lk in zip(masks, lns)]
    css = _split_dots(lks, uo)
    rests = [list(st["rest"]) for st in streams]
    ws = []
    for (s, u, h), m, a, cs in zip(tiles, masks, las, css):
        w = jnp.exp(a + rests[s][h] + cs[:, :BLK])
        ws.append(w if m is None else jnp.where(m, w, 0.0))
        rests[s][h] = rests[s][h] + cs[:, BLK:]
    return tiles, las, lns, ws, rests


def _stream(q_pair, qi, n_left, diag, zero):
    return dict(q=q_pair, kbs=[qi - u for u in range(n_left + 1)], masks=[diag] + [None] * n_left, rest=[zero, zero])


def _row0(block):
    return block * BLK if isinstance(block, int) else pl.multiple_of(block * BLK, BLK)


def _pair_of(vals, tiles, s, u):
    return [v for v, t in zip(vals, tiles) if t[0] == s and t[1] == u]


def _block_groups(nq, together):
    n_tog = math.gcd(together, nq)
    assert n_tog >= FIRST_LEFT
    return n_tog, list(range(n_tog)), nq // n_tog


def _attn_prep(src_ref, w_ref, dst_s, n_blocks, scale):
    per = math.gcd(PREP_BLOCKS, n_blocks)
    rows = per * BLK
    lo = _iota((rows, PAIR), 1) < HEAD_DIM
    ones2 = _pair_ones()

    def step(i, carry):
        r0 = pl.multiple_of(i * rows, rows)
        v = src_ref[0, pl.ds(r0, rows), :]
        if w_ref is not None:
            v = v * _pair_rms(v, ones2) * w_ref[...]
        if scale != 1.0:
            v = v * scale
        v0, v1 = jnp.where(lo, v, 0.0).astype(BF16), jnp.where(lo, 0.0, v).astype(BF16)
        for b in range(per):
            dst_s[i * per + b, 0:BLK, :] = v0[b * BLK:(b + 1) * BLK]
            dst_s[i * per + b, BLK:2 * BLK, :] = v1[b * BLK:(b + 1) * BLK]
        return carry

    lax.fori_loop(0, n_blocks // per, step, 0)


def _attn_fwd(proj3, qw2, kw2, D):
    Bl, L, _ = proj3.shape
    n_pair = D // PAIR
    nq = L // BLK
    scale = 1.0 / math.sqrt(HEAD_DIM)

    def body(q_ref, k_ref, v_ref, qw_ref, kw_ref, o_ref, qm_s, km_s, vm_s):
        uo = _suffix_ones()
        diag = _iota((BLK, BLK), 1) < _iota((BLK, BLK), 0)
        _attn_prep(q_ref, qw_ref, qm_s, nq, scale)
        _attn_prep(k_ref, kw_ref, km_s, nq, 1.0)
        _attn_prep(v_ref, None, vm_s, nq, 1.0)

        zero_c = jnp.zeros((BLK, BLK), F32)

        def q_of(qi):
            return qm_s[qi, 0:BLK, :] + qm_s[qi, BLK:2 * BLK, :]

        def values(streams, accs):
            tiles, _, _, ws, rests = _sb_tiles(streams, km_s, uo)
            wbs = [w.astype(BF16) for w in ws]
            accs = list(accs)
            for s, st in enumerate(streams):
                for u, kb in enumerate(st["kbs"]):
                    accs[s] = accs[s] + _dot(jnp.concatenate(_pair_of(wbs, tiles, s, u), axis=1), vm_s[kb])
            return accs, rests

        def group(qis, n_lefts):
            streams = [_stream(q_of(qi), qi, n, diag, zero_c) for qi, n in zip(qis, n_lefts)]
            accs, rests = values(streams, [jnp.zeros((BLK, PAIR), F32)] * len(qis))
            for qi, n, q, acc, rc in zip(qis, n_lefts, [st["q"] for st in streams], accs, rests):

                def sweep(state, n_blocks, q=q):
                    kb, rc0, rc1, acc1, _ = state
                    st = dict(q=q, kbs=[kb - u for u in range(n_blocks)], masks=[None] * n_blocks, rest=[rc0, rc1])
                    (acc1,), (r,) = values([st], [acc1])
                    return kb - n_blocks, r[0], r[1], acc1, jnp.maximum(jnp.max(r[0]), jnp.max(r[1]))

                state = (jnp.asarray(qi - n - 1, jnp.int32), rc[0], rc[1], acc, jnp.maximum(jnp.max(rc[0]), jnp.max(rc[1])))
                state = lax.while_loop(lambda t: (t[0] >= 1) & (t[4] >= UNDERFLOW), lambda t: sweep(t, 2), state)
                state = lax.while_loop(lambda t: (t[0] >= 0) & (t[4] >= UNDERFLOW), lambda t: sweep(t, 1), state)
                o_ref[0, pl.ds(_row0(qi), BLK), :] = state[3]

        n_tog, head, n_groups = _block_groups(nq, Q_TOGETHER_FWD)
        group(head, [min(qi, FIRST_LEFT) for qi in head])

        def groups(g, carry):
            group([g * n_tog + j for j in range(n_tog)], [FIRST_LEFT] * n_tog)
            return carry

        lax.fori_loop(1, n_groups, groups, 0)

    blk = lambda off: pl.BlockSpec((1, L, PAIR), lambda b, p: (b, 0, off + p))
    wspec = pl.BlockSpec((1, PAIR), lambda b, p: (0, 0))
    return pl.pallas_call(
        body,
        name="sb_attn_fwd",
        grid=(Bl, n_pair),
        in_specs=[blk(0), blk(n_pair), blk(2 * n_pair), wspec, wspec],
        out_specs=pl.BlockSpec((1, L, PAIR), lambda b, p: (b, 0, p)),
        out_shape=jax.ShapeDtypeStruct((Bl, L, D), F32),
        scratch_shapes=[pltpu.VMEM((nq, 2 * BLK, PAIR), BF16)] * 3,
        compiler_params=_params(("parallel", "parallel")),
    )(proj3, proj3, proj3, qw2, kw2)


def _attn_bwd(proj3, o3, do3, qw2, kw2, D):
    Bl, L, _ = proj3.shape
    n_pair = D // PAIR
    nq = L // BLK
    scale = 1.0 / math.sqrt(HEAD_DIM)

    def body(q_ref, k_ref, v_ref, o_ref, do_ref, qw_ref, kw_ref, dq_ref, dk_ref, dv_ref, dw_ref,
             qm_s, km_s, vm_s, dom_s, dq_s, dk_s, dv_s):
        uo = _suffix_ones()
        diag = _iota((BLK, BLK), 1) < _iota((BLK, BLK), 0)
        ones2 = _pair_ones()
        _attn_prep(q_ref, qw_ref, qm_s, nq, scale)
        _attn_prep(k_ref, kw_ref, km_s, nq, 1.0)
        _attn_prep(v_ref, None, vm_s, nq, 1.0)
        _attn_prep(do_ref, None, dom_s, nq, 1.0)

        @pl.when((pl.program_id(0) == 0) & (pl.program_id(1) == 0))
        def _():
            dw_ref[...] = jnp.zeros_like(dw_ref)

        def zero(i, carry):
            r0 = pl.multiple_of(i * BLK, BLK)
            dk_s[pl.ds(r0, BLK), :] = jnp.zeros((BLK, PAIR), F32)
            dv_s[pl.ds(r0, BLK), :] = jnp.zeros((BLK, PAIR), F32)
            return carry

        lax.fori_loop(0, nq, zero, 0)

        zero_c = jnp.zeros((BLK, BLK), F32)

        def tiles_bwd(streams, dqas):
            tiles, las, lns, ws, rests = _sb_tiles(streams, km_s, uo)
            dw2s = {(s, u): _dot_nt(st["do"], vm_s[kb]) for s, st in enumerate(streams) for u, kb in enumerate(st["kbs"])}
            dws = [dw2s[s, u][:, h * BLK:(h + 1) * BLK] for s, u, h in tiles]
            wfs = [w.astype(BF16).astype(F32) for w in ws]
            gs = [wf * dw for wf, dw in zip(wfs, dws)]
            gss = _split_dots(gs, uo)
            gcs = [list(st["g_rest"]) for st in streams]
            dzs = []
            for (s, u, h), a, ln, g, gsum in zip(tiles, las, lns, gs, gss):
                g_before = streams[s]["delta"][h] - (gcs[s][h] + gsum[:, :BLK] + g)
                gcs[s][h] = gcs[s][h] + gsum[:, BLK:]
                dz = g * jnp.exp(ln) - g_before * jnp.exp(a)
                m = streams[s]["masks"][u]
                dzs.append(dz if m is None else jnp.where(m, dz, 0.0))
            wts = [wf.T.astype(BF16) for wf in wfs]
            dzts = [dz.T.astype(BF16) for dz in dzs]
            dzbs = [dz.astype(BF16) for dz in dzs]
            dqas = list(dqas)
            for s, st in enumerate(streams):
                for u, kb in enumerate(st["kbs"]):
                    c0 = _row0(kb)
                    dv_s[pl.ds(c0, BLK), :] += _dot(jnp.concatenate(_pair_of(wts, tiles, s, u), axis=1), dom_s[st["qi"]])
                    dk_s[pl.ds(c0, BLK), :] += _dot(jnp.concatenate(_pair_of(dzts, tiles, s, u), axis=1), qm_s[st["qi"]])
                    dqas[s] = dqas[s] + _dot(jnp.concatenate(_pair_of(dzbs, tiles, s, u), axis=1), km_s[kb])
            return dqas, rests, gcs

        def group(qis, n_lefts):
            streams = []
            for qi, n in zip(qis, n_lefts):
                o_blk = o_ref[0, pl.ds(_row0(qi), BLK), :]
                doms = [dom_s[qi, 0:BLK, :], dom_s[qi, BLK:2 * BLK, :]]
                st = _stream(qm_s[qi, 0:BLK, :] + qm_s[qi, BLK:2 * BLK, :], qi, n, diag, zero_c)
                st.update(qi=qi, do=doms[0] + doms[1], delta=[_rowsum(d.astype(F32) * o_blk) for d in doms],
                          g_rest=[zero_c, zero_c])
                streams.append(st)
            dqas, rests, gcs = tiles_bwd(streams, [jnp.zeros((BLK, PAIR), F32)] * len(qis))
            for qi, n, st0, dqa, rc, gc in zip(qis, n_lefts, streams, dqas, rests, gcs):

                def sweep(state, n_blocks, st0=st0):
                    kb, rc0, rc1, gc0, gc1, dqa1, _ = state
                    st = dict(st0, kbs=[kb - u for u in range(n_blocks)], masks=[None] * n_blocks, rest=[rc0, rc1],
                              g_rest=[gc0, gc1])
                    (dqa1,), (r,), (g,) = tiles_bwd([st], [dqa1])
                    return kb - n_blocks, r[0], r[1], g[0], g[1], dqa1, jnp.maximum(jnp.max(r[0]), jnp.max(r[1]))

                state = (jnp.asarray(qi - n - 1, jnp.int32), rc[0], rc[1], gc[0], gc[1], dqa,
                         jnp.maximum(jnp.max(rc[0]), jnp.max(rc[1])))
                state = lax.while_loop(lambda t: (t[0] >= 1) & (t[6] >= UNDERFLOW), lambda t: sweep(t, 2), state)
                state = lax.while_loop(lambda t: (t[0] >= 0) & (t[6] >= UNDERFLOW), lambda t: sweep(t, 1), state)
                dq_s[pl.ds(_row0(qi), BLK), :] = state[5] * scale

        n_tog, head, n_groups = _block_groups(nq, Q_TOGETHER_BWD)
        group(head, [min(qi, FIRST_LEFT) for qi in head])

        def groups(g, carry):
            group([g * n_tog + j for j in range(n_tog)], [FIRST_LEFT] * n_tog)
            return carry

        lax.fori_loop(1, n_groups, groups, 0)

        per = math.gcd(PREP_BLOCKS, nq)
        rows = per * BLK

        def finish(i, carry):
            r0 = pl.multiple_of(i * rows, rows)
            dwq, dwk = carry
            out = []
            for src_ref, w_ref, d_s in ((q_ref, qw_ref, dq_s), (k_ref, kw_ref, dk_s)):
                v = src_ref[0, pl.ds(r0, rows), :]
                r = _pair_rms(v, ones2)
                vh = v * r
                dy = d_s[pl.ds(r0, rows), :]
                dvh = dy * w_ref[...]
                out.append((r * (dvh - vh * _pair_mean(dvh * vh, ones2)), _colsum(dy * vh)))
            dq_ref[0, pl.ds(r0, rows), :] = out[0][0].astype(BF16)
            dk_ref[0, pl.ds(r0, rows), :] = out[1][0].astype(BF16)
            dv_ref[0, pl.ds(r0, rows), :] = dv_s[pl.ds(r0, rows), :].astype(BF16)
            return dwq + out[0][1], dwk + out[1][1]

        zrow = jnp.zeros((1, PAIR), F32)
        dwq, dwk = lax.fori_loop(0, nq // per, finish, (zrow, zrow))
        dw_ref[0:1, :] += dwq
        dw_ref[1:2, :] += dwk

    blk = lambda off: pl.BlockSpec((1, L, PAIR), lambda b, p: (b, 0, off + p))
    wspec = pl.BlockSpec((1, PAIR), lambda b, p: (0, 0))
    oblk = pl.BlockSpec((1, L, PAIR), lambda b, p: (b, 0, p))
    return pl.pallas_call(
        body,
        name="sb_attn_bwd",
        grid=(Bl, n_pair),
        in_specs=[blk(0), blk(n_pair), blk(2 * n_pair), oblk, oblk, wspec, wspec],
        out_specs=[oblk, oblk, oblk, pl.BlockSpec((8, PAIR), lambda b, p: (0, 0))],
        out_shape=[jax.ShapeDtypeStruct((Bl, L, D), BF16)] * 3 + [jax.ShapeDtypeStruct((8, PAIR), F32)],
        scratch_shapes=[pltpu.VMEM((nq, 2 * BLK, PAIR), BF16)] * 4 + [pltpu.VMEM((L, PAIR), F32)] * 3,
        compiler_params=_params(("arbitrary", "arbitrary")),
    )(proj3, proj3, proj3, o3, do3, qw2, kw2)


def _conv_pre(ext_s, halo_ref, raw_ref, w_ref, b_ref, first):
    ext_s[0:HALO, :] = jnp.where(first, 0.0, halo_ref[0])
    ext_s[HALO:HALO + BLK, :] = raw_ref[0]
    pre = b_ref[...]
    for i in range(CONV_K):
        pre = pre + ext_s[pl.ds(HALO - (CONV_K - 1 - i), BLK), :] * w_ref[i:i + 1, :]
    return pre


def _lane_col(m, lane, h):
    return _rowsum(jnp.where(lane == h, m, 0.0))


def _half_sums(row, lo1):
    return _rowsum(jnp.where(lo1, row, 0.0)), _rowsum(jnp.where(lo1, 0.0, row))


def _ssd_specs(Bl, L, D, rev):
    nc = L // BLK
    rows_per = BLK // HALO
    cidx = (lambda c: nc - 1 - c) if rev else (lambda c: c)
    xoff = 5
    boff = (6 * D) // 512
    doff = (6 * D + 512) // LANES
    prev = lambda c: jnp.maximum(cidx(c) * rows_per - 1, 0)
    specs = [
        pl.BlockSpec((1, BLK, D), lambda b, c: (b, cidx(c), xoff)),
        pl.BlockSpec((1, BLK, 512), lambda b, c: (b, cidx(c), boff)),
        pl.BlockSpec((1, HALO, D), lambda b, c: (b, prev(c), xoff)),
        pl.BlockSpec((1, HALO, 512), lambda b, c: (b, prev(c), boff)),
        pl.BlockSpec((1, BLK, LANES), lambda b, c: (b, cidx(c), doff)),
    ]
    full = lambda shape: pl.BlockSpec(shape, lambda b, c: (0,) * len(shape))
    specs += [full((CONV_K, D)), full((CONV_K, 512)), full((1, D)), full((1, 512)),
              full((1, LANES)), full((1, LANES)), full((1, LANES))]
    return specs, cidx


def _ssd_common(dtr_ref, dtb_ref, alog_ref, acs_s, acsT_s):
    ltri = jnp.where(_iota((BLK, BLK), 1) <= _iota((BLK, BLK), 0), 1.0, 0.0).astype(BF16)
    dtv = _softplus(dtr_ref[0] + dtb_ref[...])
    a = -jnp.exp(alog_ref[...])
    acs = _dot_split(ltri, dtv * a)
    acs_s[...] = acs
    acsT_s[...] = acs.T
    return dtv, a, acs


def _pair_terms(pr, acs, dtv, acs_s, lane, lo, lane1, lo1):
    h0, h1 = 2 * pr, 2 * pr + 1
    c0, c1 = _lane_col(acs, lane, h0), _lane_col(acs, lane, h1)
    d0, d1 = _lane_col(dtv, lane, h0), _lane_col(dtv, lane, h1)
    lastv = acs_s[BLK - 1:BLK, :]
    l0, l1 = _lane_col(lastv, lane1, h0), _lane_col(lastv, lane1, h1)
    return dict(h=(h0, h1), c=(c0, c1), last=(l0, l1), acs_p=jnp.where(lo, c0, c1), dt_p=jnp.where(lo, d0, d1),
                last_p=jnp.where(lo1, l0, l1))


def _decay_tiles(cc, row, tri, want_t):
    lm = jnp.where(tri, jnp.exp(cc - row), 0.0)
    return lm, (lm.T if want_t else None)


def _ssd_fwd(proj3, cwx, cwb, cbx, cbb, dtb, alog, dsk, D):
    Bl, L, _ = proj3.shape
    nc = L // BLK
    n_pair = D // PAIR
    pairs_per_group = n_pair // SSD_GROUPS
    specs, _ = _ssd_specs(Bl, L, D, False)

    def body(xr_ref, bcr_ref, xh_ref, bch_ref, dtr_ref, cwx_ref, cwb_ref, cbx_ref, cbb_ref, dtb_ref, alog_ref,
             dsk_ref, y_ref, sin_ref, st_s, extx_s, extb_s, acs_s, acsT_s):
        first = pl.program_id(1) == 0

        @pl.when(first)
        def _():
            st_s[...] = jnp.zeros_like(st_s)

        lane, lane1 = _iota((BLK, LANES), 1), _iota((1, LANES), 1)
        lo, lo1 = lane < HEAD_DIM, lane1 < HEAD_DIM
        tri = _iota((BLK, BLK), 1) <= _iota((BLK, BLK), 0)
        pre = _conv_pre(extx_s, xh_ref, xr_ref, cwx_ref, cbx_ref, first)
        ux = pre * _sigmoid(pre)
        pre = _conv_pre(extb_s, bch_ref, bcr_ref, cwb_ref, cbb_ref, first)
        ub = pre * _sigmoid(pre)
        dtv, a, acs = _ssd_common(dtr_ref, dtb_ref, alog_ref, acs_s, acsT_s)
        for g in range(SSD_GROUPS):
            bg = ub[:, g * SSD_STATE:(g + 1) * SSD_STATE]
            cb_ = ub[:, (SSD_GROUPS + g) * SSD_STATE:(SSD_GROUPS + g + 1) * SSD_STATE].astype(BF16)
            cbm = _dot_nt(cb_, bg.astype(BF16))
            btb = bg.T.astype(BF16)
            for pr in range(g * pairs_per_group, (g + 1) * pairs_per_group):
                t = _pair_terms(pr, acs, dtv, acs_s, lane, lo, lane1, lo1)
                xs_p = ux[:, pr * PAIR:(pr + 1) * PAIR]
                x_p = xs_p * t["dt_p"]
                st = st_s[pr]
                sin_ref[0, 0, pr] = st
                y = _dot(cb_, st.astype(BF16)) * jnp.exp(t["acs_p"])
                for k in range(2):
                    row = acsT_s[t["h"][k]:t["h"][k] + 1, :]
                    lm, _ = _decay_tiles(t["c"][k], row, tri, False)
                    xm = jnp.where(lo if k == 0 else ~lo, x_p, 0.0).astype(BF16)
                    y = y + _dot((cbm * lm).astype(BF16), xm)
                d_p = jnp.where(lo1, _lane_col(dsk_ref[...], lane1, t["h"][0]), _lane_col(dsk_ref[...], lane1, t["h"][1]))
                y_ref[0, :, pr * PAIR:(pr + 1) * PAIR] = y + d_p * xs_p
                xd = (x_p * jnp.exp(t["last_p"] - t["acs_p"])).astype(BF16)
                st_s[pr] = st * jnp.exp(t["last_p"]) + _dot(btb, xd)

    return pl.pallas_call(
        body,
        name="ssd_fwd",
        grid=(Bl, nc),
        in_specs=specs,
        out_specs=[
            pl.BlockSpec((1, BLK, D), lambda b, c: (b, c, 0)),
            pl.BlockSpec((1, 1, n_pair, SSD_STATE, PAIR), lambda b, c: (b, c, 0, 0, 0)),
        ],
        out_shape=[jax.ShapeDtypeStruct((Bl, L, D), F32),
                   jax.ShapeDtypeStruct((Bl, nc, n_pair, SSD_STATE, PAIR), F32)],
        scratch_shapes=[pltpu.VMEM((n_pair, SSD_STATE, PAIR), F32), pltpu.VMEM((HALO + BLK, D), F32),
                        pltpu.VMEM((HALO + BLK, 512), F32), pltpu.VMEM((BLK, LANES), F32),
                        pltpu.VMEM((LANES, BLK), F32)],
        compiler_params=_params(("arbitrary", "arbitrary")),
    )(proj3, proj3, proj3, proj3, proj3, cwx, cwb, cbx, cbb, dtb, alog, dsk)


def _ssd_bwd(proj3, s_in, dy3, cwx, cwb, cbx, cbb, dtb, alog, dsk, D, tail):
    Bl, L, _ = proj3.shape
    CD = D + 512
    nc = L // BLK
    n_pair = D // PAIR
    n_heads = 2 * n_pair
    pairs_per_group = n_pair // SSD_GROUPS
    specs, cidx = _ssd_specs(Bl, L, D, True)
    specs = specs + [
        pl.BlockSpec((1, 1, n_pair, SSD_STATE, PAIR), lambda b, c: (b, cidx(c), 0, 0, 0)),
        pl.BlockSpec((1, BLK, D), lambda b, c: (b, cidx(c), 0)),
    ]

    def body(xr_ref, bcr_ref, xh_ref, bch_ref, dtr_ref, cwx_ref, cwb_ref, cbx_ref, cbb_ref, dtb_ref, alog_ref,
             dsk_ref, sin_ref, dy_ref, dxbc_ref, dcwx_ref, dcwb_ref, dcbx_ref, dcbb_ref, misc_ref,
             dst_s, extx_s, extb_s, acs_s, acsT_s, dux_s, dub_s, e2x_s, e2b_s, nxx_s, nxb_s):
        step = pl.program_id(1)
        first = step == nc - 1
        last = step == 0

        @pl.when(last)
        def _():
            dst_s[...] = jnp.zeros_like(dst_s)
            nxx_s[...] = jnp.zeros_like(nxx_s)
            nxb_s[...] = jnp.zeros_like(nxb_s)

        @pl.when(last & (pl.program_id(0) == 0))
        def _():
            for r in (dcwx_ref, dcwb_ref, dcbx_ref, dcbb_ref, misc_ref):
                r[...] = jnp.zeros_like(r)

        lane, lane1 = _iota((BLK, LANES), 1), _iota((1, LANES), 1)
        lo, lo1 = lane < HEAD_DIM, lane1 < HEAD_DIM
        tri = _iota((BLK, BLK), 1) <= _iota((BLK, BLK), 0)
        prex = _conv_pre(extx_s, xh_ref, xr_ref, cwx_ref, cbx_ref, first)
        sgx = _sigmoid(prex)
        ux = prex * sgx
        preb = _conv_pre(extb_s, bch_ref, bcr_ref, cwb_ref, cbb_ref, first)
        sgb = _sigmoid(preb)
        ub = preb * sgb
        dtv, a, acs = _ssd_common(dtr_ref, dtb_ref, alog_ref, acs_s, acsT_s)
        dacs = jnp.zeros((BLK, LANES), F32)
        dlast = jnp.zeros((1, LANES), F32)
        ddt = jnp.zeros((BLK, LANES), F32)
        dd = jnp.zeros((1, LANES), F32)
        for g in range(SSD_GROUPS):
            bg = ub[:, g * SSD_STATE:(g + 1) * SSD_STATE]
            cg = ub[:, (SSD_GROUPS + g) * SSD_STATE:(SSD_GROUPS + g + 1) * SSD_STATE]
            bb, cb_ = bg.astype(BF16), cg.astype(BF16)
            cbm = _dot_nt(cb_, bb)
            cbt = _dot_nt(bb, cb_)
            ctb = cg.T.astype(BF16)
            dbg = jnp.zeros((BLK, SSD_STATE), F32)
            dcg = jnp.zeros((BLK, SSD_STATE), F32)
            for pr in range(g * pairs_per_group, (g + 1) * pairs_per_group):
                t = _pair_terms(pr, acs, dtv, acs_s, lane, lo, lane1, lo1)
                h0, h1 = t["h"]
                xs_p = ux[:, pr * PAIR:(pr + 1) * PAIR]
                dy_p = dy_ref[0, :, pr * PAIR:(pr + 1) * PAIR]
                x_p = xs_p * t["dt_p"]
                ea_p = jnp.exp(t["acs_p"])
                dte_p = jnp.exp(t["last_p"] - t["acs_p"])
                cd_p = jnp.exp(t["last_p"])
                st = sin_ref[0, 0, pr]
                dst = dst_s[pr]
                stb, dstb = st.astype(BF16), dst.astype(BF16)
                s0, s1 = _half_sums(_colsum(dy_p * xs_p), lo1)
                dd = dd + jnp.where(lane1 == h0, s0, 0.0) + jnp.where(lane1 == h1, s1, 0.0)
                d_p = jnp.where(lo1, _lane_col(dsk_ref[...], lane1, h0), _lane_col(dsk_ref[...], lane1, h1))
                dxs_p = d_p * dy_p
                dp = dy_p * ea_p
                dpb = dp.astype(BF16)
                yo = dp * _dot(cb_, stb)
                dcg = dcg + _dot_nt(dpb, stb)
                dst_off = _dot(ctb, dpb)
                dac = [_rowsum(jnp.where(lo, yo, 0.0)), _rowsum(jnp.where(lo, 0.0, yo))]
                s0, s1 = _half_sums(_colsum(dst * st), lo1)
                dl = [s0 * jnp.exp(t["last"][0]), s1 * jnp.exp(t["last"][1])]
                dxd = _dot(bb, dstb)
                dx_p = dxd * dte_p
                tt = dxd * x_p
                dbg = dbg + _dot_nt((x_p * dte_p).astype(BF16), dstb)
                for k, ddte in enumerate((_rowsum(jnp.where(lo, tt, 0.0)), _rowsum(jnp.where(lo, 0.0, tt)))):
                    ek = ddte * jnp.exp(t["last"][k] - t["c"][k])
                    dl[k] = dl[k] + _colsum(ek)
                    dac[k] = dac[k] - ek
                x_pb = x_p.astype(BF16)
                for k in range(2):
                    row = acsT_s[t["h"][k]:t["h"][k] + 1, :]
                    lm, lmt = _decay_tiles(t["c"][k], row, tri, True)
                    dym = jnp.where(lo if k == 0 else ~lo, dy_p, 0.0).astype(BF16)
                    dm = _dot_nt(dym, x_pb)
                    dmt = _dot_nt(x_pb, dym)
                    mt = cbt * lmt
                    dx_p = dx_p + _dot(mt.astype(BF16), dym)
                    dac[k] = dac[k] + _rowsum(dm * (cbm * lm)) - _rowsum(dmt * mt)
                    dcg = dcg + _dot((dm * lm).astype(BF16), bb)
                    dbg = dbg + _dot((dmt * lmt).astype(BF16), cb_)
                dacs = dacs + jnp.where(lane == h0, dac[0], 0.0) + jnp.where(lane == h1, dac[1], 0.0)
                dlast = dlast + jnp.where(lane1 == h0, dl[0], 0.0) + jnp.where(lane1 == h1, dl[1], 0.0)
                dxs_p = dxs_p + dx_p * t["dt_p"]
                t3 = dx_p * xs_p
                ddt = ddt + jnp.where(lane == h0, _rowsum(jnp.where(lo, t3, 0.0)), 0.0) \
                    + jnp.where(lane == h1, _rowsum(jnp.where(lo, 0.0, t3)), 0.0)
                dux_s[:, pr * PAIR:(pr + 1) * PAIR] = dxs_p
                dst_s[pr] = dst * cd_p + dst_off
            dub_s[:, g * SSD_STATE:(g + 1) * SSD_STATE] = dbg
            dub_s[:, (SSD_GROUPS + g) * SSD_STATE:(SSD_GROUPS + g + 1) * SSD_STATE] = dcg
        dacs = dacs + jnp.where(_iota((BLK, LANES), 0) == BLK - 1, dlast, 0.0)
        utri = jnp.where(_iota((BLK, BLK), 1) >= _iota((BLK, BLK), 0), 1.0, 0.0).astype(BF16)
        dda = _dot_split(utri, dacs)
        ddt = ddt + dda * a
        ddtr = jnp.where(lane < n_heads, ddt * _sigmoid(dtr_ref[0] + dtb_ref[...]), 0.0)
        dxbc_ref[0, :, CD:CD + LANES] = ddtr.astype(BF16)
        dxbc_ref[0, :, CD + LANES:tail] = jnp.zeros((BLK, tail - CD - LANES), BF16)
        misc_ref[0:1, :] += _colsum(ddtr)
        misc_ref[1:2, :] += jnp.where(lane1 < n_heads, _colsum(dda * dtv) * a, 0.0)
        misc_ref[2:3, :] += dd
        for (du_s, pre, sg, ext_s, e2_s, nx_s, w_ref, dcw_ref, dcb_ref, c0, width) in (
                (dux_s, prex, sgx, extx_s, e2x_s, nxx_s, cwx_ref, dcwx_ref, dcbx_ref, 0, D),
                (dub_s, preb, sgb, extb_s, e2b_s, nxb_s, cwb_ref, dcwb_ref, dcbb_ref, D, 512)):
            dpre = du_s[...] * (sg * (1.0 + pre * (1.0 - sg)))
            dcb_ref[...] += _colsum(dpre)
            for i in range(CONV_K):
                dcw_ref[i:i + 1, :] += _colsum(dpre * ext_s[pl.ds(HALO - (CONV_K - 1 - i), BLK), :])
            e2_s[0:BLK, :] = dpre
            e2_s[BLK:BLK + HALO, :] = nx_s[...]
            dxr = jnp.zeros((BLK, width), F32)
            for i in range(CONV_K):
                dxr = dxr + e2_s[pl.ds(CONV_K - 1 - i, BLK), :] * w_ref[i:i + 1, :]
            dxbc_ref[0, :, c0:c0 + width] = dxr.astype(BF16)
            nx_s[...] = e2_s[0:HALO, :]

    full = lambda shape: pl.BlockSpec(shape, lambda b, c: (0,) * len(shape))
    return pl.pallas_call(
        body,
        name="ssd_bwd",
        grid=(Bl, nc),
        in_specs=specs,
        out_specs=[
            pl.BlockSpec((1, BLK, tail), lambda b, c: (b, cidx(c), 0)),
            full((CONV_K, D)), full((CONV_K, 512)), full((1, D)), full((1, 512)), full((8, LANES)),
        ],
        out_shape=[
            jax.ShapeDtypeStruct((Bl, L, tail), BF16),
            jax.ShapeDtypeStruct((CONV_K, D), F32), jax.ShapeDtypeStruct((CONV_K, 512), F32),
            jax.ShapeDtypeStruct((1, D), F32), jax.ShapeDtypeStruct((1, 512), F32),
            jax.ShapeDtypeStruct((8, LANES), F32),
        ],
        scratch_shapes=[
            pltpu.VMEM((n_pair, SSD_STATE, PAIR), F32),
            pltpu.VMEM((HALO + BLK, D), F32), pltpu.VMEM((HALO + BLK, 512), F32),
            pltpu.VMEM((BLK, LANES), F32), pltpu.VMEM((LANES, BLK), F32),
            pltpu.VMEM((BLK, D), F32), pltpu.VMEM((BLK, 512), F32),
            pltpu.VMEM((BLK + HALO, D), F32), pltpu.VMEM((BLK + HALO, 512), F32),
            pltpu.VMEM((HALO, D), F32), pltpu.VMEM((HALO, 512), F32),
        ],
        compiler_params=_params(("arbitrary", "arbitrary")),
    )(proj3, proj3, proj3, proj3, proj3, cwx, cwb, cbx, cbb, dtb, alog, dsk, s_in, dy3)


def _gate_out(x2, tgt2, o2, proj2, y2, sbw, ssw, w_out_bf, w_out_t):
    T, D = x2.shape
    tm = min(256, T)

    def body(x_ref, t_ref, o_ref, zs_ref, y_ref, zy_ref, sbw_ref, ssw_ref, wo_ref, wot_ref,
             dout_ref, doutb_ref, mixt_ref, do_ref, dy_ref, dz_ref, dnw_ref, loss_ref):
        @pl.when(pl.program_id(0) == 0)
        def _():
            dnw_ref[...] = jnp.zeros_like(dnw_ref)
            loss_ref[...] = jnp.zeros_like(loss_ref)

        def fwd(o, z, w):
            sg = _sigmoid(z)
            sl = z * sg
            g = o * sl
            r = lax.rsqrt(jnp.mean(g * g, axis=-1, keepdims=True) + EPS)
            n = g * r
            return sg, sl, r, n, n * w

        def bwd(dy, o, z, w, sg, sl, r, n):
            dn = dy * w
            dg = r * (dn - n * jnp.mean(dn * n, axis=-1, keepdims=True))
            return dg * sl, dg * o * (sg * (1.0 + z * (1.0 - sg))), _colsum(dy * n)

        o1, z1, w1 = o_ref[...], zs_ref[...], sbw_ref[...]
        o2_, z2, w2 = y_ref[...], zy_ref[...], ssw_ref[...]
        sg1, sl1, r1, n1, y1 = fwd(o1, z1, w1)
        sg2, sl2, r2, n2, y2_ = fwd(o2_, z2, w2)
        y1b, y2b = y1.astype(BF16), y2_.astype(BF16)
        mixt_ref[0:D, :] = y1.T.astype(BF16)
        mixt_ref[D:2 * D, :] = y2_.T.astype(BF16)
        out = x_ref[...] + (_dot(y1b, wo_ref[0:D, :]) + _dot(y2b, wo_ref[D:2 * D, :]))
        err = out - t_ref[...]
        loss_ref[...] += jnp.sum(err * err) * (0.5 / D)
        dout = err * (1.0 / D)
        dout_ref[...] = dout
        doutb = dout.astype(BF16)
        doutb_ref[...] = doutb
        do1, dz1, dw1 = bwd(_dot(doutb, wot_ref[:, 0:D]), o1, z1, w1, sg1, sl1, r1, n1)
        do2, dz2, dw2 = bwd(_dot(doutb, wot_ref[:, D:2 * D]), o2_, z2, w2, sg2, sl2, r2, n2)
        do_ref[...] = do1
        dy_ref[...] = do2
        dz_ref[:, 0:D] = dz1.astype(BF16)
        dz_ref[:, D:2 * D] = dz2.astype(BF16)
        dnw_ref[0:1, :] += dw1
        dnw_ref[1:2, :] += dw2

    row = lambda col: pl.BlockSpec((tm, D), lambda i: (i, col))
    full = lambda shape: pl.BlockSpec(shape, lambda i: (0,) * len(shape))
    wide = pl.BlockSpec((tm, 2 * D), lambda i: (i, 0))
    return pl.pallas_call(
        body,
        name="gate_out",
        grid=(T // tm,),
        in_specs=[row(0), row(0), row(0), row(3), row(0), row(4), full((1, D)), full((1, D)), full((2 * D, D)),
                  full((D, 2 * D))],
        out_specs=[row(0), row(0), pl.BlockSpec((2 * D, tm), lambda i: (0, i)), row(0), row(0), wide,
                   full((8, D)), full((8, LANES))],
        out_shape=[
            jax.ShapeDtypeStruct((T, D), F32), jax.ShapeDtypeStruct((T, D), BF16),
            jax.ShapeDtypeStruct((2 * D, T), BF16), jax.ShapeDtypeStruct((T, D), F32),
            jax.ShapeDtypeStruct((T, D), F32), jax.ShapeDtypeStruct((T, 2 * D), BF16),
            jax.ShapeDtypeStruct((8, D), F32), jax.ShapeDtypeStruct((8, LANES), F32),
        ],
        compiler_params=_params(("arbitrary",)),
    )(x2, tgt2, o2, proj2, y2, proj2, sbw, ssw, w_out_bf, w_out_t)


def _piece_blocks(pieces, D):
    counts = [p.shape[1] // D for p in pieces]
    return [sum(counts[:i]) for i in range(len(counts))], counts


def _dhn(pieces, w_pad_t, x2, dout, norm_w, h_in, h_out, slab_off, slab_w):
    T, D = x2.shape
    tm = min(1024, T)
    starts, counts = _piece_blocks(pieces, D)
    nk = sum(counts)
    ni = T // tm
    n_sem = 2 * (N_CHIPS - 1)
    assert nk * D == w_pad_t.shape[0]

    def body(*refs):
        p_refs = refs[:len(pieces)]
        (w_ref, x_hbm, dout_hbm, nw_ref, hin, hout, gx_ref, dnw_ref, rin, rout,
         acc_s, x_s, dout_s, send_sems, recv_sems, row_sems, own_sems) = refs[len(pieces):]
        i, k = pl.program_id(0), pl.program_id(1)

        def rows():
            r0 = pl.multiple_of(i * tm, tm)
            return [pltpu.make_async_copy(src.at[pl.ds(r0, tm)], dst, row_sems.at[n])
                    for n, (src, dst) in enumerate(((x_hbm, x_s), (dout_hbm, dout_s)))]

        @pl.when(k == 0)
        def _():
            for cp in rows():
                cp.start()

        def scatter():
            x, y, c, chips = _place()

            def slab(p):
                return hin.at[:, pl.ds(pl.multiple_of(p * slab_off, LANES), slab_w)]

            cps = []
            for j, (px, py) in enumerate(chips):
                for m, (src, dst) in enumerate(((slab(2 * px + py), rin.at[j]), (hout.at[2 * px + py], rout.at[j]))):
                    cps.append(pltpu.make_async_remote_copy(
                        src_ref=src, dst_ref=dst, send_sem=send_sems.at[2 * j + m], recv_sem=recv_sems.at[2 * j + m],
                        device_id=(px, py, c), device_id_type=MESH))
            me = 2 * x + y
            own = [pltpu.make_async_copy(slab(me), rin.at[N_CHIPS - 1], own_sems.at[0]),
                   pltpu.make_async_copy(hout.at[me], rout.at[N_CHIPS - 1], own_sems.at[1])]
            return cps + own

        @pl.when((i == 0) & (k == 0))
        def _():
            for cp in scatter():
                cp.start()

        @pl.when((i == ni - 1) & (k == nk - 1))
        def _():
            for cp in scatter():
                cp.wait()

        @pl.when((i == 0) & (k == 0))
        def _():
            dnw_ref[...] = jnp.zeros_like(dnw_ref)

        @pl.when(k == 0)
        def _():
            acc_s[...] = jnp.zeros_like(acc_s)

        for p_ref, s, n in zip(p_refs, starts, counts):
            @pl.when((k >= s) & (k < s + n))
            def _(p_ref=p_ref):
                acc_s[...] += _dot(p_ref[...], w_ref[...])

        @pl.when(k == nk - 1)
        def _():
            for cp in rows():
                cp.wait()
            xv = x_s[...]
            r = lax.rsqrt(jnp.mean(xv * xv, axis=-1, keepdims=True) + EPS)
            xh = xv * r
            dhn = acc_s[...]
            dxh = dhn * nw_ref[...]
            gx_ref[...] = dout_s[...] + r * (dxh - xh * jnp.mean(dxh * xh, axis=-1, keepdims=True))
            dnw_ref[0:1, :] += _colsum(dhn * xh)

    return pl.pallas_call(
        body,
        name="dhn",
        grid=(T // tm, nk),
        in_specs=[pl.BlockSpec((tm, D), lambda i, k, s=s, n=n: (i, jnp.clip(k - s, 0, n - 1)))
                  for s, n in zip(starts, counts)] + [
            pl.BlockSpec((D, D), lambda i, k: (k, 0)),
            ANY, ANY,
            pl.BlockSpec((1, D), lambda i, k: (0, 0)),
            ANY, ANY,
        ],
        out_specs=[pl.BlockSpec((tm, D), lambda i, k: (i, 0)), pl.BlockSpec((8, D), lambda i, k: (0, 0)), ANY, ANY],
        out_shape=[jax.ShapeDtypeStruct((T, D), F32), jax.ShapeDtypeStruct((8, D), F32),
                   jax.ShapeDtypeStruct((N_CHIPS, h_in.shape[0], slab_w), F32),
                   jax.ShapeDtypeStruct((N_CHIPS,) + h_out.shape[1:], F32)],
        scratch_shapes=[pltpu.VMEM((tm, D), F32)] * 3 + [pltpu.SemaphoreType.DMA((n_sem,)), pltpu.SemaphoreType.DMA((n_sem,)),
                                                      pltpu.SemaphoreType.DMA((2,)), pltpu.SemaphoreType.DMA((2,))],
        compiler_params=_params(("arbitrary", "arbitrary")),
    )(*pieces, w_pad_t, x2, dout, norm_w, h_in, h_out)


def _grad_w_in(hn_t, pieces):
    D, T = hn_t.shape
    tk = min(1024, T)
    starts, counts = _piece_blocks(pieces, D)

    def body(*refs):
        a_ref, p_refs, o_ref = refs[0], refs[1:-1], refs[-1]
        j = pl.program_id(0)

        @pl.when(pl.program_id(1) == 0)
        def _():
            o_ref[...] = jnp.zeros_like(o_ref)

        for p_ref, s, n in zip(p_refs, starts, counts):
            @pl.when((j >= s) & (j < s + n))
            def _(p_ref=p_ref):
                o_ref[...] += _dot(a_ref[...], p_ref[...])

    def piece_spec(s, n):
        return pl.BlockSpec((tk, D), lambda j, k: (jnp.where((j >= s) & (j < s + n), k, 0), jnp.clip(j - s, 0, n - 1)))

    return pl.pallas_call(
        body,
        name="grad_w_in",
        grid=(sum(counts), T // tk),
        in_specs=[pl.BlockSpec((D, tk), lambda j, k: (0, k))] + [piece_spec(s, n) for s, n in zip(starts, counts)],
        out_specs=pl.BlockSpec((D, D), lambda j, k: (0, j)),
        out_shape=jax.ShapeDtypeStruct((D, sum(counts) * D), F32),
        compiler_params=_params(("parallel", "arbitrary")),
    )(hn_t, *pieces)


def _grad_w_out(a, b, g_in, width):
    M, K = a.shape
    N = b.shape[1]
    tm = min(1024, M)
    tn = 1024 if N % 1024 == 0 else (512 if N % 512 == 0 else N)
    tk = min(512, K)
    grid = (M // tm, N // tn, K // tk)
    h = g_in.shape[0] // 2

    def body(a_ref, b_ref, gin, o_ref, rin, send_sem, recv_sem):
        ids = [pl.program_id(d) for d in range(3)]

        def swap():
            x, y, c, _ = _place()
            return pltpu.make_async_remote_copy(
                src_ref=gin.at[pl.ds((1 - c) * h, h), pl.ds(0, width)], dst_ref=rin, send_sem=send_sem, recv_sem=recv_sem,
                device_id=(x, y, 1 - c), device_id_type=MESH)

        @pl.when((ids[0] == 0) & (ids[1] == 0) & (ids[2] == 0))
        def _():
            swap().start()

        @pl.when(ids[2] == 0)
        def _():
            o_ref[...] = jnp.zeros_like(o_ref)

        o_ref[...] += _dot(a_ref[...], b_ref[...])

        @pl.when((ids[0] == grid[0] - 1) & (ids[1] == grid[1] - 1) & (ids[2] == grid[2] - 1))
        def _():
            swap().wait()

    return pl.pallas_call(
        body,
        name="grad_w_out",
        grid=grid,
        in_specs=[pl.BlockSpec((tm, tk), lambda i, j, k: (i, k)), pl.BlockSpec((tk, tn), lambda i, j, k: (k, j)), ANY],
        out_specs=[pl.BlockSpec((tm, tn), lambda i, j, k: (i, j)), ANY],
        out_shape=[jax.ShapeDtypeStruct((M, N), F32), jax.ShapeDtypeStruct((h, width), F32)],
        scratch_shapes=[pltpu.SemaphoreType.DMA, pltpu.SemaphoreType.DMA],
        compiler_params=_params(("arbitrary", "arbitrary", "arbitrary")),
    )(a, b, g_in)


def _adamw(w, g, m, v, name):
    R, C = w.shape
    tr = 256 if R % 256 == 0 else R
    c1 = 1.0 - ADAM_B1 ** ADAM_STEP
    c2 = 1.0 - ADAM_B2 ** ADAM_STEP

    def body(w_ref, g_ref, m_ref, v_ref, d_ref, nm_ref, nv_ref):
        gv = g_ref[...]
        m_new = ADAM_B1 * m_ref[...] + (1.0 - ADAM_B1) * gv
        v_new = ADAM_B2 * v_ref[...] + (1.0 - ADAM_B2) * (gv * gv)
        d_ref[...] = -ADAM_LR * ((m_new / c1) / (jnp.sqrt(v_new / c2) + ADAM_EPS) + ADAM_WD * w_ref[...])
        nm_ref[...] = m_new
        nv_ref[...] = v_new

    spec = pl.BlockSpec((tr, C), lambda i: (i, 0))
    return pl.pallas_call(
        body,
        name=name,
        grid=(R // tr,),
        in_specs=[spec] * 4,
        out_specs=[spec] * 3,
        out_shape=[jax.ShapeDtypeStruct((R, C), F32)] * 3,
        compiler_params=_params(("parallel",)),
    )(w, g, m, v)


def _add_core_rows(g, recv, core, name):
    h, width = recv.shape
    th = 128 if h % 128 == 0 else h

    def body(c_ref, g_ref, r_ref, o_ref):
        o_ref[...] = g_ref[...] + r_ref[...]

    return pl.pallas_call(
        body,
        name=name,
        grid_spec=pltpu.PrefetchScalarGridSpec(
            num_scalar_prefetch=1,
            grid=(h // th,),
            in_specs=[
                pl.BlockSpec((th, width), lambda i, c: (c[0] * (h // th) + i, 0)),
                pl.BlockSpec((th, width), lambda i, c: (i, 0)),
            ],
            out_specs=pl.BlockSpec((th, width), lambda i, c: (i, 0)),
        ),
        out_shape=jax.ShapeDtypeStruct((h, width), F32),
        compiler_params=_params(("parallel",)),
    )(core, g, recv)


def _add_core_blocks(g, recv, core, name):
    n, hb, C = recv.shape

    def body(c_ref, g_ref, r_ref, o_ref):
        o_ref[...] = g_ref[...] + r_ref[...]

    return pl.pallas_call(
        body,
        name=name,
        grid_spec=pltpu.PrefetchScalarGridSpec(
            num_scalar_prefetch=1,
            grid=(n,),
            in_specs=[
                pl.BlockSpec((hb, C), lambda p, c: (2 * p + c[0], 0)),
                pl.BlockSpec((None, hb, C), lambda p, c: (p, 0, 0)),
            ],
            out_specs=pl.BlockSpec((None, hb, C), lambda p, c: (p, 0, 0)),
        ),
        out_shape=jax.ShapeDtypeStruct((n, hb, C), F32),
        compiler_params=_params(("parallel",)),
    )(core, g, recv)


def _add_chips(recv, name):
    _, h, W = recv.shape
    th = 256 if h % 256 == 0 else h

    def body(r_ref, o_ref):
        o_ref[...] = ((r_ref[N_CHIPS - 1] + r_ref[0]) + r_ref[1]) + r_ref[2]

    return pl.pallas_call(
        body,
        name=name,
        grid=(h // th,),
        in_specs=[pl.BlockSpec((N_CHIPS, th, W), lambda i: (0, i, 0))],
        out_specs=pl.BlockSpec((th, W), lambda i: (i, 0)),
        out_shape=jax.ShapeDtypeStruct((h, W), F32),
        compiler_params=_params(("parallel",)),
    )(recv)


def _place():
    x, y, c = lax.axis_index("x"), lax.axis_index("y"), lax.axis_index("c")
    other_chips = [(1 - x, y), (x, 1 - y), (1 - x, 1 - y)]
    return x, y, c, other_chips


def _allgather_w_in(w_in_bf, x2, norm_w):
    D, S = w_in_bf.shape
    T = x2.shape[0]
    tm = min(1024, T)
    ni = T // tm
    n_ici = n_fwd = N_CHIPS - 1

    def body(win, x_ref, nw_ref, gin, hn_ref, hnt_ref, send_sems, recv_sems):
        step = pl.program_id(0)
        xv = x_ref[...]
        hn = xv * lax.rsqrt(jnp.mean(xv * xv, axis=-1, keepdims=True) + EPS) * nw_ref[...]
        hn_ref[...] = hn.astype(BF16)
        hnt_ref[...] = hn.T.astype(BF16)
        x, y, c, chips = _place()
        me = 2 * x + y
        sibling = (x, y, 1 - c)
        hin = D // 2

        def half(chip_idx, core):
            return gin.at[chip_idx, pl.ds(core * hin, hin)]

        def rcopy(k, src, dst, to):
            return pltpu.make_async_remote_copy(src_ref=src, dst_ref=dst, send_sem=send_sems.at[k],
                                                recv_sem=recv_sems.at[k], device_id=to, device_id_type=MESH)

        def sends():
            return [rcopy(j, win.at[pl.ds(c * hin, hin)], half(me, c), (*chip, c)) for j, chip in enumerate(chips)]

        @pl.when(step == 0)
        def _():
            for cp in sends():
                cp.start()

        @pl.when(step == ni - 1)
        def _():
            passed = []
            for j, (px, py) in enumerate(chips):
                theirs = half(2 * px + py, c)
                rcopy(j, theirs, theirs, sibling).wait_recv()
                passed.append(rcopy(n_ici + j, theirs, theirs, sibling))
                passed[-1].start()
            for j, (px, py) in enumerate(chips):
                other = half(2 * px + py, 1 - c)
                rcopy(n_ici + j, other, other, sibling).wait_recv()
            for cp in sends() + passed:
                cp.wait_send()

    return pl.pallas_call(
        body,
        name="allgather_w_in",
        grid=(ni,),
        in_specs=[ANY, pl.BlockSpec((tm, D), lambda i: (i, 0)), pl.BlockSpec((1, D), lambda i: (0, 0))],
        out_specs=[ANY, pl.BlockSpec((tm, D), lambda i: (i, 0)), pl.BlockSpec((D, tm), lambda i: (0, i))],
        out_shape=[jax.ShapeDtypeStruct((N_CHIPS, D, S), BF16),
                   jax.ShapeDtypeStruct((T, D), BF16), jax.ShapeDtypeStruct((D, T), BF16)],
        scratch_shapes=[pltpu.SemaphoreType.DMA((n_ici + n_fwd,)), pltpu.SemaphoreType.DMA((n_ici + n_fwd,))],
        compiler_params=_params(("arbitrary",)),
    )(w_in_bf, x2, norm_w)


def _allreduce_small(packed):
    R = packed.shape[0]
    n_dev = 2 * N_CHIPS

    def body(p_ref, o_ref, buf, send_sems, recv_sems):
        x, y, c, _ = _place()
        me = 4 * x + 2 * y + c
        buf[me] = p_ref[...]
        copies = []
        for k in range(1, n_dev):
            px = 1 - x if k & 4 else x
            py = 1 - y if k & 2 else y
            pc = 1 - c if k & 1 else c
            copies.append((pltpu.make_async_remote_copy(
                src_ref=buf.at[me], dst_ref=buf.at[me], send_sem=send_sems.at[k - 1], recv_sem=recv_sems.at[k - 1],
                device_id=(px, py, pc), device_id_type=MESH), 4 * px + 2 * py + pc, (px, py, pc)))
        for cp, _, _ in copies:
            cp.start()
        for k, (_, peer, to) in enumerate(copies):
            pltpu.make_async_remote_copy(
                src_ref=buf.at[peer], dst_ref=buf.at[peer], send_sem=send_sems.at[k], recv_sem=recv_sems.at[k],
                device_id=to, device_id_type=MESH).wait_recv()
        for cp, _, _ in copies:
            cp.wait_send()
        acc = buf[0]
        for d in range(1, n_dev):
            acc = acc + buf[d]
        o_ref[...] = acc

    vm = pl.BlockSpec(memory_space=pltpu.VMEM)
    return pl.pallas_call(
        body,
        name="allreduce_small",
        in_specs=[vm],
        out_specs=vm,
        out_shape=jax.ShapeDtypeStruct((R, LANES), F32),
        scratch_shapes=[pltpu.VMEM((n_dev, R, LANES), F32), pltpu.SemaphoreType.DMA((n_dev - 1,)),
                        pltpu.SemaphoreType.DMA((n_dev - 1,))],
    )(packed)


def _swap_core_halves(g_out):
    hb = g_out.shape[0] // (2 * N_CHIPS)

    def body(gout, rout, send_sems, recv_sems):
        x, y, c, _ = _place()
        cps = [pltpu.make_async_remote_copy(
            src_ref=gout.at[pl.ds((2 * p + 1 - c) * hb, hb)], dst_ref=rout.at[p], send_sem=send_sems.at[p],
            recv_sem=recv_sems.at[p], device_id=(x, y, 1 - c), device_id_type=MESH) for p in range(N_CHIPS)]
        for cp in cps:
            cp.start()
        for cp in cps:
            cp.wait()

    return pl.pallas_call(
        body,
        name="reduce_core_swap",
        in_specs=[ANY],
        out_specs=ANY,
        out_shape=jax.ShapeDtypeStruct((N_CHIPS, hb, g_out.shape[1]), F32),
        scratch_shapes=[pltpu.SemaphoreType.DMA((N_CHIPS,)), pltpu.SemaphoreType.DMA((N_CHIPS,))],
    )(g_out)


def _join_core_halves(g_in, g_out):
    def body(gin, gout, fin, fout, send_sems, recv_sems):
        x, y, c, _ = _place()
        cps = [pltpu.make_async_remote_copy(src_ref=s, dst_ref=d.at[c], send_sem=send_sems.at[k],
                                            recv_sem=recv_sems.at[k], device_id=(x, y, 1 - c), device_id_type=MESH)
               for k, (s, d) in enumerate(((gin, fin), (gout, fout)))]
        for cp in cps:
            cp.start()
        for k, (s, d) in enumerate(((gin, fin), (gout, fout))):
            pltpu.make_async_remote_copy(src_ref=s, dst_ref=d.at[1 - c], send_sem=send_sems.at[k],
                                         recv_sem=recv_sems.at[k], device_id=(x, y, 1 - c),
                                         device_id_type=MESH).wait_recv()
        for cp in cps:
            cp.wait_send()

    return pl.pallas_call(
        body,
        name="reduce_core_join",
        in_specs=[ANY, ANY],
        out_specs=[ANY, ANY],
        out_shape=[jax.ShapeDtypeStruct((2,) + g_in.shape, F32), jax.ShapeDtypeStruct((2,) + g_out.shape, F32)],
        scratch_shapes=[pltpu.SemaphoreType.DMA((2,)), pltpu.SemaphoreType.DMA((2,))],
    )(g_in, g_out)


def _pack(arrays):
    rows = []
    for a in arrays:
        flat = a.reshape(-1).astype(F32)
        n = -(-flat.shape[0] // LANES) * LANES
        rows.append(jnp.pad(flat, (0, n - flat.shape[0])).reshape(-1, LANES))
    out = jnp.concatenate(rows, axis=0)
    return jnp.pad(out, ((0, -out.shape[0] % 8), (0, 0)))


def _unpack(packed, shapes):
    out, r = [], 0
    for shp in shapes:
        n = math.prod(shp)
        nr = -(-n // LANES)
        out.append(packed[r:r + nr].reshape(-1)[:n].reshape(shp))
        r += nr
    return out


def _pad_lanes(a):
    return jnp.pad(a, ((0, 0), (0, LANES - a.shape[1])))


def kernel(x, norm_w, w_in, q_norm_w, k_norm_w, conv_w, conv_b, dt_bias, A_log, D_skip, sb_norm_w, ssd_norm_w, w_out, loss_target, m_norm_w, m_w_in, m_q_norm_w, m_k_norm_w, m_conv_w, m_conv_b, m_dt_bias, m_A_log, m_D_skip, m_sb_norm_w, m_ssd_norm_w, m_w_out, v_norm_w, v_w_in, v_q_norm_w, v_k_norm_w, v_conv_w, v_conv_b, v_dt_bias, v_A_log, v_D_skip, v_sb_norm_w, v_ssd_norm_w, v_w_out):
    Bl, L, D = x.shape
    T = Bl * L
    S = w_in.shape[2]
    R = w_out.shape[1]
    CW = conv_w.shape[2]
    n_in = N_CHIPS * S
    CD = D + 2 * SSD_GROUPS * SSD_STATE
    H = D // HEAD_DIM
    n_main = 6 * D + 512
    P = -(-(n_main + LANES) // 1024) * 1024
    assert n_in == n_main + H and CD == N_CHIPS * CW and 2 * D == N_CHIPS * R and CD == D + 512
    chip = (2 * lax.axis_index("x") + lax.axis_index("y")).astype(jnp.int32)
    core = lax.axis_index("c").astype(jnp.int32)

    w_in_bf, w_out_shard_bf = w_in[0].astype(BF16), w_out[0].astype(BF16)
    x2 = x.reshape(T, D)
    g_in, hn, hn_t = _allgather_w_in(w_in_bf, x2, norm_w)
    g_in = lax.dynamic_update_slice(g_in, w_in_bf[None], (chip, 0, 0))
    w_pad = jnp.pad(g_in.transpose(1, 0, 2).reshape(D, n_in), ((0, 0), (0, P - n_in)))
    proj, w_pad_t, g_out, g_cw = _inproj(hn, w_pad, w_out_shard_bf, conv_w[0])
    g_out = lax.dynamic_update_slice(g_out, w_out_shard_bf[None], (chip, 0, 0))
    g_cw = lax.dynamic_update_slice(g_cw, conv_w, (chip, 0, 0))
    w_out_bf = g_out.reshape(2 * D, D)
    conv_full = g_cw.transpose(1, 0, 2).reshape(CONV_K, CD)
    cwx, cwb = conv_full[:, :D], conv_full[:, D:]
    cbx, cbb = conv_b[:, :D], conv_b[:, D:]
    dtb, alog, dsk = _pad_lanes(dt_bias), _pad_lanes(A_log), _pad_lanes(D_skip)
    qw2, kw2 = jnp.tile(q_norm_w, (1, 2)), jnp.tile(k_norm_w, (1, 2))

    proj3 = proj.reshape(Bl, L, P)
    o_sb = _attn_fwd(proj3, qw2, kw2, D)
    y_ssd, s_in = _ssd_fwd(proj3, cwx, cwb, cbx, cbb, dtb, alog, dsk, D)
    dout, dout_bf, mixed_t, do_sb, dy_ssd, dz_bf, dnw_out, loss_blk = _gate_out(
        x2, loss_target.reshape(T, D), o_sb.reshape(T, D), proj, y_ssd.reshape(T, D), sb_norm_w, ssd_norm_w, w_out_bf,
        w_out_bf.T)

    dq, dk, dv, dqkw = _attn_bwd(proj3, o_sb, do_sb.reshape(Bl, L, D), qw2, kw2, D)
    dtail, dcwx, dcwb, dcbx, dcbb, misc = _ssd_bwd(
        proj3, s_in, dy_ssd.reshape(Bl, L, D), cwx, cwb, cbx, cbb, dtb, alog, dsk, D, P - 5 * D)
    dproj = [dq.reshape(T, D), dk.reshape(T, D), dv.reshape(T, D), dz_bf, dtail.reshape(T, P - 5 * D)]
    gw_in = _grad_w_in(hn_t, dproj)

    slab_off = S // LANES * LANES
    slab_w = -(-(S + (N_CHIPS - 1) * (S - slab_off)) // LANES) * LANES
    width = (N_CHIPS - 1) * slab_off + slab_w
    assert n_in <= width <= P
    core1 = core.reshape(1)
    gw_out, r_in = _grad_w_out(mixed_t, dout_bf, gw_in, width)
    r_out = _swap_core_halves(gw_out)
    h_in = _add_core_rows(gw_in, r_in, core1, "sum_cores_w_in")
    h_out = _add_core_blocks(gw_out, r_out, core1, "sum_cores_w_out")
    grad_x2, dnw_in, s_in_, s_out_ = _dhn(dproj, w_pad_t, x2, dout, norm_w, h_in, h_out, slab_off, slab_w)
    gh_in = _add_chips(s_in_, "sum_chips_w_in")
    gh_out = _add_chips(s_out_, "sum_chips_w_out")
    f_in, f_out = _join_core_halves(gh_in, gh_out)
    g_slab = lax.dynamic_update_slice(f_in, gh_in[None], (core, 0, 0)).reshape(D, slab_w)
    g_w_in = lax.dynamic_slice(g_slab, (0, chip * (S - slab_off)), (D, S))
    g_w_out = lax.dynamic_update_slice(f_out, gh_out[None], (core, 0, 0)).reshape(R, D)

    small_shapes = [(1, D), (1, D), (1, D), (1, CD), (1, HEAD_DIM), (1, HEAD_DIM), (1, H), (1, H), (1, H)]
    g_small_local = [dnw_in[0:1], dnw_out[0:1], dnw_out[1:2], jnp.concatenate([dcbx, dcbb], axis=1),
                     dqkw[0:1, :HEAD_DIM] + dqkw[0:1, HEAD_DIM:], dqkw[1:2, :HEAD_DIM] + dqkw[1:2, HEAD_DIM:],
                     misc[0:1, :H], misc[1:2, :H], misc[2:3, :H]]
    packed = _pack(g_small_local + [jnp.concatenate([dcwx, dcwb], axis=1), loss_blk[0:1, 0:1]])
    red = _allreduce_small(packed)
    g_small = _unpack(red, small_shapes + [(CONV_K, CD), (1, 1)])
    g_conv_w = lax.dynamic_slice_in_dim(g_small[9], chip * CW, CW, axis=1)
    loss = g_small[10][0, 0]

    d_in, nm_in, nv_in = _adamw(w_in[0], g_w_in, m_w_in[0], v_w_in[0], "adamw_w_in")
    d_out, nm_out, nv_out = _adamw(w_out[0], g_w_out, m_w_out[0], v_w_out[0], "adamw_w_out")
    d_cw, nm_cw, nv_cw = _adamw(conv_w[0], g_conv_w, m_conv_w[0], v_conv_w[0], "adamw_conv_w")
    small_w = [norm_w, sb_norm_w, ssd_norm_w, conv_b, q_norm_w, k_norm_w, dt_bias, A_log, D_skip]
    small_m = [m_norm_w, m_sb_norm_w, m_ssd_norm_w, m_conv_b, m_q_norm_w, m_k_norm_w, m_dt_bias, m_A_log, m_D_skip]
    small_v = [v_norm_w, v_sb_norm_w, v_ssd_norm_w, v_conv_b, v_q_norm_w, v_k_norm_w, v_dt_bias, v_A_log, v_D_skip]
    d_s, nm_s, nv_s = _adamw(_pack(small_w), _pack(g_small[:9]), _pack(small_m), _pack(small_v), "adamw_small")
    d_s, nm_s, nv_s = (_unpack(t, small_shapes) for t in (d_s, nm_s, nv_s))

    def ordered(s, w_in_, conv_w_, w_out_):
        return [s[0], w_in_[None], s[4], s[5], conv_w_[None], s[3], s[6], s[7], s[8], s[1], s[2], w_out_[None]]

    return (loss, grad_x2.reshape(Bl, L, D),
            *ordered(g_small[:9], g_w_in, g_conv_w, g_w_out),
            *ordered(d_s, d_in, d_cw, d_out),
            *ordered(nm_s, nm_in, nm_cw, nm_out),
            *ordered(nv_s, nv_in, nv_cw, nv_out))
```

```python
import functools
import math

import jax
import jax.numpy as jnp
from jax import lax
from jax.experimental import pallas as pl
from jax.experimental.pallas import tpu as pltpu

F32 = jnp.float32
BF16 = jnp.bfloat16
EPS = 1e-6
HEAD_DIM = 64
PAIR = 2 * HEAD_DIM
LANES = 128
SSD_STATE = 128
SSD_GROUPS = 2
BLK = 128
PREP_BLOCKS = 16
Q_TOGETHER_FWD = 2
Q_TOGETHER_BWD = 2
FIRST_LEFT = 2
UNDERFLOW = -105.0
CONV_K = 4
HALO = 8
N_CHIPS = 4
ADAM_LR, ADAM_B1, ADAM_B2, ADAM_EPS, ADAM_WD, ADAM_STEP = 0.001, 0.9, 0.999, 1e-08, 0.01, 10
VMEM_LIMIT_V7X = 56 * 1024 * 1024
MESH = pl.DeviceIdType.MESH
ANY = pl.BlockSpec(memory_space=pl.ANY)
NT = (((1,), (1,)), ((), ()))


def _params(sem=None):
    kw = dict(vmem_limit_bytes=VMEM_LIMIT_V7X)
    if sem is not None:
        kw["dimension_semantics"] = sem
    return pltpu.CompilerParams(**kw)


def _dot(a, b):
    return jnp.dot(a, b, preferred_element_type=F32)


def _dot_nt(a, b):
    return lax.dot_general(a, b, NT, preferred_element_type=F32)


def _dot_split(m, x):
    hi = x.astype(BF16)
    lo = (x - hi.astype(F32)).astype(BF16)
    return _dot(m, hi) + _dot(m, lo)


def _iota(shape, dim):
    return lax.broadcasted_iota(jnp.int32, shape, dim)


def _rowsum(x):
    return jnp.sum(x, axis=1, keepdims=True)


def _colsum(x):
    return jnp.sum(x, axis=0, keepdims=True)


def _sigmoid(x):
    return 0.5 * jnp.tanh(0.5 * x) + 0.5


def _softplus(x):
    return jnp.maximum(x, 0.0) + jnp.log(1.0 + jnp.exp(-jnp.abs(x)))


def _inproj(hn, w_pad, w_out_bf, conv_w):
    T, D = hn.shape
    P = w_pad.shape[1]
    tm = min(1024, T)
    tn = 1024 if P % 1024 == 0 else 512
    ni, nj = T // tm, P // tn
    n_sem = 2 * (N_CHIPS - 1)

    def body(hn_ref, w_ref, wout, cw, proj_ref, wt_ref, gout, gcw, send_sems, recv_sems):
        def gather():
            x, y, c, chips = _place()
            me = 2 * x + y
            return [pltpu.make_async_remote_copy(
                src_ref=src, dst_ref=dst.at[me], send_sem=send_sems.at[2 * j + m], recv_sem=recv_sems.at[2 * j + m],
                device_id=(px, py, c), device_id_type=MESH)
                for j, (px, py) in enumerate(chips) for m, (src, dst) in enumerate(((wout, gout), (cw, gcw)))]

        @pl.when((pl.program_id(0) == 0) & (pl.program_id(1) == 0))
        def _():
            for cp in gather():
                cp.start()

        @pl.when((pl.program_id(0) == ni - 1) & (pl.program_id(1) == nj - 1))
        def _():
            for cp in gather():
                cp.wait()

        @pl.when(pl.program_id(0) == 0)
        def _():
            wt_ref[...] = w_ref[...].astype(F32).T.astype(BF16)

        proj_ref[...] = _dot(hn_ref[...], w_ref[...])

    return pl.pallas_call(
        body,
        name="inproj",
        grid=(T // tm, P // tn),
        in_specs=[
            pl.BlockSpec((tm, D), lambda i, j: (i, 0)),
            pl.BlockSpec((D, tn), lambda i, j: (0, j)),
            ANY, ANY,
        ],
        out_specs=[
            pl.BlockSpec((tm, tn), lambda i, j: (i, j)),
            pl.BlockSpec((tn, D), lambda i, j: (jnp.where(i == 0, j, nj - 1), 0)),
            ANY, ANY,
        ],
        out_shape=[jax.ShapeDtypeStruct((T, P), F32), jax.ShapeDtypeStruct((P, D), BF16),
                   jax.ShapeDtypeStruct((N_CHIPS,) + w_out_bf.shape, BF16),
                   jax.ShapeDtypeStruct((N_CHIPS,) + conv_w.shape, F32)],
        scratch_shapes=[pltpu.SemaphoreType.DMA((n_sem,)), pltpu.SemaphoreType.DMA((n_sem,))],
        compiler_params=_params(("arbitrary", "arbitrary")),
    )(hn, w_pad, w_out_bf, conv_w)


def _pair_ones():
    ri = ((_iota((2 * PAIR, PAIR), 0) % PAIR) >= HEAD_DIM).astype(jnp.int32)
    ci = (_iota((2 * PAIR, PAIR), 1) >= HEAD_DIM).astype(jnp.int32)
    return jnp.where(ri == ci, 1.0, 0.0).astype(BF16)


def _pair_rms(v, ones2):
    return lax.rsqrt(_split_dots([v * v], ones2)[0] * (1.0 / HEAD_DIM) + EPS)


def _pair_mean(v, ones2):
    return _split_dots([v], ones2)[0] * (1.0 / HEAD_DIM)


def _suffix_ones():
    ri = _iota((2 * BLK, 2 * BLK), 0) % BLK
    ci = _iota((2 * BLK, 2 * BLK), 1)
    return jnp.where((ci >= BLK) | (ri > ci), 1.0, 0.0).astype(BF16)


def _split_dots(xs, m2):
    his = [x.astype(BF16) for x in xs]
    los = [(x - hi.astype(F32)).astype(BF16) for x, hi in zip(xs, his)]
    return [_dot(jnp.concatenate([hi, lo], axis=1), m2) for hi, lo in zip(his, los)]


def _sb_tiles(streams, km_s, uo):
    tiles = [(s, u, h) for s, st in enumerate(streams) for u in range(len(st["kbs"])) for h in range(2)]
    z2s = {(s, u): _dot_nt(st["q"], km_s[kb]) for s, st in enumerate(streams) for u, kb in enumerate(st["kbs"])}
    zs = [z2s[s, u][:, h * BLK:(h + 1) * BLK] for s, u, h in tiles]
    es = [jnp.exp(-jnp.abs(z)) for z in zs]
    las = [jnp.minimum(z, 0.0) - jnp.log(1.0 + e) for z, e in zip(zs, es)]
    lns = [a - z for a, z in zip(las, zs)]
    masks = [streams[s]["masks"][u] for s, u, h in tiles]
    lks = [lk if m is None else jnp.where(m, lk, 0.0) for m, lk in zip(masks, lns)]
    css = _split_dots(lks, uo)
    rests = [list(st["rest"]) for st in streams]
    ws = []
    for (s, u, h), m, a, cs in zip(tiles, masks, las, css):
        w = jnp.exp(a + rests[s][h] + cs[:, :BLK])
        ws.append(w if m is None else jnp.where(m, w, 0.0))
        rests[s][h] = rests[s][h] + cs[:, BLK:]
    return tiles, las, lns, ws, rests


def _stream(q_pair, qi, n_left, diag, zero):
    return dict(q=q_pair, kbs=[qi - u for u in range(n_left + 1)], masks=[diag] + [None] * n_left, rest=[zero, zero])


def _row0(block):
    return block * BLK if isinstance(block, int) else pl.multiple_of(block * BLK, BLK)


def _pair_of(vals, tiles, s, u):
    return [v for v, t in zip(vals, tiles) if t[0] == s and t[1] == u]


def _block_groups(nq, together):
    n_tog = math.gcd(together, nq)
    assert n_tog >= FIRST_LEFT
    return n_tog, list(range(n_tog)), nq // n_tog


def _attn_prep(src_ref, w_ref, dst_s, n_blocks, scale):
    per = math.gcd(PREP_BLOCKS, n_blocks)
    rows = per * BLK
    lo = _iota((rows, PAIR), 1) < HEAD_DIM
    ones2 = _pair_ones()

    def step(i, carry):
        r0 = pl.multiple_of(i * rows, rows)
        v = src_ref[0, pl.ds(r0, rows), :]
        if w_ref is not None:
            v = v * _pair_rms(v, ones2) * w_ref[...]
        if scale != 1.0:
            v = v * scale
        v0, v1 = jnp.where(lo, v, 0.0).astype(BF16), jnp.where(lo, 0.0, v).astype(BF16)
        for b in range(per):
            dst_s[i * per + b, 0:BLK, :] = v0[b * BLK:(b + 1) * BLK]
            dst_s[i * per + b, BLK:2 * BLK, :] = v1[b * BLK:(b + 1) * BLK]
        return carry

    lax.fori_loop(0, n_blocks // per, step, 0)


def _attn_fwd(proj3, qw2, kw2, D):
    Bl, L, _ = proj3.shape
    n_pair = D // PAIR
    nq = L // BLK
    scale = 1.0 / math.sqrt(HEAD_DIM)

    def body(q_ref, k_ref, v_ref, qw_ref, kw_ref, o_ref, qm_s, km_s, vm_s):
        uo = _suffix_ones()
        diag = _iota((BLK, BLK), 1) < _iota((BLK, BLK), 0)
        _attn_prep(q_ref, qw_ref, qm_s, nq, scale)
        _attn_prep(k_ref, kw_ref, km_s, nq, 1.0)
        _attn_prep(v_ref, None, vm_s, nq, 1.0)

        zero_c = jnp.zeros((BLK, BLK), F32)

        def q_of(qi):
            return qm_s[qi, 0:BLK, :] + qm_s[qi, BLK:2 * BLK, :]

        def values(streams, accs):
            tiles, _, _, ws, rests = _sb_tiles(streams, km_s, uo)
            wbs = [w.astype(BF16) for w in ws]
            accs = list(accs)
            for s, st in enumerate(streams):
                for u, kb in enumerate(st["kbs"]):
                    accs[s] = accs[s] + _dot(jnp.concatenate(_pair_of(wbs, tiles, s, u), axis=1), vm_s[kb])
            return accs, rests

        def group(qis, n_lefts):
            streams = [_stream(q_of(qi), qi, n, diag, zero_c) for qi, n in zip(qis, n_lefts)]
            accs, rests = values(streams, [jnp.zeros((BLK, PAIR), F32)] * len(qis))
            for qi, n, q, acc, rc in zip(qis, n_lefts, [st["q"] for st in streams], accs, rests):

                def sweep(state, n_blocks, q=q):
                    kb, rc0, rc1, acc1, _ = state
                    st = dict(q=q, kbs=[kb - u for u in range(n_blocks)], masks=[None] * n_blocks, rest=[rc0, rc1])
                    (acc1,), (r,) = values([st], [acc1])
                    return kb - n_blocks, r[0], r[1], acc1, jnp.maximum(jnp.max(r[0]), jnp.max(r[1]))

                state = (jnp.asarray(qi - n - 1, jnp.int32), rc[0], rc[1], acc, jnp.maximum(jnp.max(rc[0]), jnp.max(rc[1])))
                state = lax.while_loop(lambda t: (t[0] >= 1) & (t[4] >= UNDERFLOW), lambda t: sweep(t, 2), state)
                state = lax.while_loop(lambda t: (t[0] >= 0) & (t[4] >= UNDERFLOW), lambda t: sweep(t, 1), state)
                o_ref[0, pl.ds(_row0(qi), BLK), :] = state[3]

        n_tog, head, n_groups = _block_groups(nq, Q_TOGETHER_FWD)
        group(head, [min(qi, FIRST_LEFT) for qi in head])

        def groups(g, carry):
            group([g * n_tog + j for j in range(n_tog)], [FIRST_LEFT] * n_tog)
            return carry

        lax.fori_loop(1, n_groups, groups, 0)

    blk = lambda off: pl.BlockSpec((1, L, PAIR), lambda b, p: (b, 0, off + p))
    wspec = pl.BlockSpec((1, PAIR), lambda b, p: (0, 0))
    return pl.pallas_call(
        body,
        name="sb_attn_fwd",
        grid=(Bl, n_pair),
        in_specs=[blk(0), blk(n_pair), blk(2 * n_pair), wspec, wspec],
        out_specs=pl.BlockSpec((1, L, PAIR), lambda b, p: (b, 0, p)),
        out_shape=jax.ShapeDtypeStruct((Bl, L, D), F32),
        scratch_shapes=[pltpu.VMEM((nq, 2 * BLK, PAIR), BF16)] * 3,
        compiler_params=_params(("parallel", "parallel")),
    )(proj3, proj3, proj3, qw2, kw2)


def _attn_bwd(proj3, o3, do3, qw2, kw2, D):
    Bl, L, _ = proj3.shape
    n_pair = D // PAIR
    nq = L // BLK
    scale = 1.0 / math.sqrt(HEAD_DIM)

    def body(q_ref, k_ref, v_ref, o_ref, do_ref, qw_ref, kw_ref, dq_ref, dk_ref, dv_ref, dw_ref,
             qm_s, km_s, vm_s, dom_s, dq_s, dk_s, dv_s):
        uo = _suffix_ones()
        diag = _iota((BLK, BLK), 1) < _iota((BLK, BLK), 0)
        ones2 = _pair_ones()
        _attn_prep(q_ref, qw_ref, qm_s, nq, scale)
        _attn_prep(k_ref, kw_ref, km_s, nq, 1.0)
        _attn_prep(v_ref, None, vm_s, nq, 1.0)
        _attn_prep(do_ref, None, dom_s, nq, 1.0)

        @pl.when((pl.program_id(0) == 0) & (pl.program_id(1) == 0))
        def _():
            dw_ref[...] = jnp.zeros_like(dw_ref)

        def zero(i, carry):
            r0 = pl.multiple_of(i * BLK, BLK)
            dk_s[pl.ds(r0, BLK), :] = jnp.zeros((BLK, PAIR), F32)
            dv_s[pl.ds(r0, BLK), :] = jnp.zeros((BLK, PAIR), F32)
            return carry

        lax.fori_loop(0, nq, zero, 0)

        zero_c = jnp.zeros((BLK, BLK), F32)

        def tiles_bwd(streams, dqas):
            tiles, las, lns, ws, rests = _sb_tiles(streams, km_s, uo)
            dw2s = {(s, u): _dot_nt(st["do"], vm_s[kb]) for s, st in enumerate(streams) for u, kb in enumerate(st["kbs"])}
            dws = [dw2s[s, u][:, h * BLK:(h + 1) * BLK] for s, u, h in tiles]
            wfs = [w.astype(BF16).astype(F32) for w in ws]
            gs = [wf * dw for wf, dw in zip(wfs, dws)]
            gss = _split_dots(gs, uo)
            gcs = [list(st["g_rest"]) for st in streams]
            dzs = []
            for (s, u, h), a, ln, g, gsum in zip(tiles, las, lns, gs, gss):
                g_before = streams[s]["delta"][h] - (gcs[s][h] + gsum[:, :BLK] + g)
                gcs[s][h] = gcs[s][h] + gsum[:, BLK:]
                dz = g * jnp.exp(ln) - g_before * jnp.exp(a)
                m = streams[s]["masks"][u]
                dzs.append(dz if m is None else jnp.where(m, dz, 0.0))
            wts = [wf.T.astype(BF16) for wf in wfs]
            dzts = [dz.T.astype(BF16) for dz in dzs]
            dzbs = [dz.astype(BF16) for dz in dzs]
            dqas = list(dqas)
            for s, st in enumerate(streams):
                for u, kb in enumerate(st["kbs"]):
                    c0 = _row0(kb)
                    dv_s[pl.ds(c0, BLK), :] += _dot(jnp.concatenate(_pair_of(wts, tiles, s, u), axis=1), dom_s[st["qi"]])
                    dk_s[pl.ds(c0, BLK), :] += _dot(jnp.concatenate(_pair_of(dzts, tiles, s, u), axis=1), qm_s[st["qi"]])
                    dqas[s] = dqas[s] + _dot(jnp.concatenate(_pair_of(dzbs, tiles, s, u), axis=1), km_s[kb])
            return dqas, rests, gcs

        def group(qis, n_lefts):
            streams = []
            for qi, n in zip(qis, n_lefts):
                o_blk = o_ref[0, pl.ds(_row0(qi), BLK), :]
                doms = [dom_s[qi, 0:BLK, :], dom_s[qi, BLK:2 * BLK, :]]
                st = _stream(qm_s[qi, 0:BLK, :] + qm_s[qi, BLK:2 * BLK, :], qi, n, diag, zero_c)
                st.update(qi=qi, do=doms[0] + doms[1], delta=[_rowsum(d.astype(F32) * o_blk) for d in doms],
                          g_rest=[zero_c, zero_c])
                streams.append(st)
            dqas, rests, gcs = tiles_bwd(streams, [jnp.zeros((BLK, PAIR), F32)] * len(qis))
            for qi, n, st0, dqa, rc, gc in zip(qis, n_lefts, streams, dqas, rests, gcs):

                def sweep(state, n_blocks, st0=st0):
                    kb, rc0, rc1, gc0, gc1, dqa1, _ = state
                    st = dict(st0, kbs=[kb - u for u in range(n_blocks)], masks=[None] * n_blocks, rest=[rc0, rc1],
                              g_rest=[gc0, gc1])
                    (dqa1,), (r,), (g,) = tiles_bwd([st], [dqa1])
                    return kb - n_blocks, r[0], r[1], g[0], g[1], dqa1, jnp.maximum(jnp.max(r[0]), jnp.max(r[1]))

                state = (jnp.asarray(qi - n - 1, jnp.int32), rc[0], rc[1], gc[0], gc[1], dqa,
                         jnp.maximum(jnp.max(rc[0]), jnp.max(rc[1])))
                state = lax.while_loop(lambda t: (t[0] >= 1) & (t[6] >= UNDERFLOW), lambda t: sweep(t, 2), state)
                state = lax.while_loop(lambda t: (t[0] >= 0) & (t[6] >= UNDERFLOW), lambda t: sweep(t, 1), state)
                dq_s[pl.ds(_row0(qi), BLK), :] = state[5] * scale

        n_tog, head, n_groups = _block_groups(nq, Q_TOGETHER_BWD)
        group(head, [min(qi, FIRST_LEFT) for qi in head])

        def groups(g, carry):
            group([g * n_tog + j for j in range(n_tog)], [FIRST_LEFT] * n_tog)
            return carry

        lax.fori_loop(1, n_groups, groups, 0)

        per = math.gcd(PREP_BLOCKS, nq)
        rows = per * BLK

        def finish(i, carry):
            r0 = pl.multiple_of(i * rows, rows)
            dwq, dwk = carry
            out = []
            for src_ref, w_ref, d_s in ((q_ref, qw_ref, dq_s), (k_ref, kw_ref, dk_s)):
                v = src_ref[0, pl.ds(r0, rows), :]
                r = _pair_rms(v, ones2)
                vh = v * r
                dy = d_s[pl.ds(r0, rows), :]
                dvh = dy * w_ref[...]
                out.append((r * (dvh - vh * _pair_mean(dvh * vh, ones2)), _colsum(dy * vh)))
            dq_ref[0, pl.ds(r0, rows), :] = out[0][0].astype(BF16)
            dk_ref[0, pl.ds(r0, rows), :] = out[1][0].astype(BF16)
            dv_ref[0, pl.ds(r0, rows), :] = dv_s[pl.ds(r0, rows), :].astype(BF16)
            return dwq + out[0][1], dwk + out[1][1]

        zrow = jnp.zeros((1, PAIR), F32)
        dwq, dwk = lax.fori_loop(0, nq // per, finish, (zrow, zrow))
        dw_ref[0:1, :] += dwq
        dw_ref[1:2, :] += dwk

    blk = lambda off: pl.BlockSpec((1, L, PAIR), lambda b, p: (b, 0, off + p))
    wspec = pl.BlockSpec((1, PAIR), lambda b, p: (0, 0))
    oblk = pl.BlockSpec((1, L, PAIR), lambda b, p: (b, 0, p))
    return pl.pallas_call(
        body,
        name="sb_attn_bwd",
        grid=(Bl, n_pair),
        in_specs=[blk(0), blk(n_pair), blk(2 * n_pair), oblk, oblk, wspec, wspec],
        out_specs=[oblk, oblk, oblk, pl.BlockSpec((8, PAIR), lambda b, p: (0, 0))],
        out_shape=[jax.ShapeDtypeStruct((Bl, L, D), BF16)] * 3 + [jax.ShapeDtypeStruct((8, PAIR), F32)],
        scratch_shapes=[pltpu.VMEM((nq, 2 * BLK, PAIR), BF16)] * 4 + [pltpu.VMEM((L, PAIR), F32)] * 3,
        compiler_params=_params(("arbitrary", "arbitrary")),
    )(proj3, proj3, proj3, o3, do3, qw2, kw2)


def _conv_pre(ext_s, halo_ref, raw_ref, w_ref, b_ref, first):
    ext_s[0:HALO, :] = jnp.where(first, 0.0, halo_ref[0])
    ext_s[HALO:HALO + BLK, :] = raw_ref[0]
    pre = b_ref[...]
    for i in range(CONV_K):
        pre = pre + ext_s[pl.ds(HALO - (CONV_K - 1 - i), BLK), :] * w_ref[i:i + 1, :]
    return pre


def _lane_col(m, lane, h):
    return _rowsum(jnp.where(lane == h, m, 0.0))


def _half_sums(row, lo1):
    return _rowsum(jnp.where(lo1, row, 0.0)), _rowsum(jnp.where(lo1, 0.0, row))


def _ssd_specs(Bl, L, D, rev):
    nc = L // BLK
    rows_per = BLK // HALO
    cidx = (lambda c: nc - 1 - c) if rev else (lambda c: c)
    xoff = 5
    boff = (6 * D) // 512
    doff = (6 * D + 512) // LANES
    prev = lambda c: jnp.maximum(cidx(c) * rows_per - 1, 0)
    specs = [
        pl.BlockSpec((1, BLK, D), lambda b, c: (b, cidx(c), xoff)),
        pl.BlockSpec((1, BLK, 512), lambda b, c: (b, cidx(c), boff)),
        pl.BlockSpec((1, HALO, D), lambda b, c: (b, prev(c), xoff)),
        pl.BlockSpec((1, HALO, 512), lambda b, c: (b, prev(c), boff)),
        pl.BlockSpec((1, BLK, LANES), lambda b, c: (b, cidx(c), doff)),
    ]
    full = lambda shape: pl.BlockSpec(shape, lambda b, c: (0,) * len(shape))
    specs += [full((CONV_K, D)), full((CONV_K, 512)), full((1, D)), full((1, 512)),
              full((1, LANES)), full((1, LANES)), full((1, LANES))]
    return specs, cidx


def _ssd_common(dtr_ref, dtb_ref, alog_ref, acs_s, acsT_s):
    ltri = jnp.where(_iota((BLK, BLK), 1) <= _iota((BLK, BLK), 0), 1.0, 0.0).astype(BF16)
    dtv = _softplus(dtr_ref[0] + dtb_ref[...])
    a = -jnp.exp(alog_ref[...])
    acs = _dot_split(ltri, dtv * a)
    acs_s[...] = acs
    acsT_s[...] = acs.T
    return dtv, a, acs


def _pair_terms(pr, acs, dtv, acs_s, lane, lo, lane1, lo1):
    h0, h1 = 2 * pr, 2 * pr + 1
    c0, c1 = _lane_col(acs, lane, h0), _lane_col(acs, lane, h1)
    d0, d1 = _lane_col(dtv, lane, h0), _lane_col(dtv, lane, h1)
    lastv = acs_s[BLK - 1:BLK, :]
    l0, l1 = _lane_col(lastv, lane1, h0), _lane_col(lastv, lane1, h1)
    return dict(h=(h0, h1), c=(c0, c1), last=(l0, l1), acs_p=jnp.where(lo, c0, c1), dt_p=jnp.where(lo, d0, d1),
                last_p=jnp.where(lo1, l0, l1))


def _decay_tiles(cc, row, tri, want_t):
    lm = jnp.where(tri, jnp.exp(cc - row), 0.0)
    return lm, (lm.T if want_t else None)


def _ssd_fwd(proj3, cwx, cwb, cbx, cbb, dtb, alog, dsk, D):
    Bl, L, _ = proj3.shape
    nc = L // BLK
    n_pair = D // PAIR
    pairs_per_group = n_pair // SSD_GROUPS
    specs, _ = _ssd_specs(Bl, L, D, False)

    def body(xr_ref, bcr_ref, xh_ref, bch_ref, dtr_ref, cwx_ref, cwb_ref, cbx_ref, cbb_ref, dtb_ref, alog_ref,
             dsk_ref, y_ref, sin_ref, st_s, extx_s, extb_s, acs_s, acsT_s):
        first = pl.program_id(1) == 0

        @pl.when(first)
        def _():
            st_s[...] = jnp.zeros_like(st_s)

        lane, lane1 = _iota((BLK, LANES), 1), _iota((1, LANES), 1)
        lo, lo1 = lane < HEAD_DIM, lane1 < HEAD_DIM
        tri = _iota((BLK, BLK), 1) <= _iota((BLK, BLK), 0)
        pre = _conv_pre(extx_s, xh_ref, xr_ref, cwx_ref, cbx_ref, first)
        ux = pre * _sigmoid(pre)
        pre = _conv_pre(extb_s, bch_ref, bcr_ref, cwb_ref, cbb_ref, first)
        ub = pre * _sigmoid(pre)
        dtv, a, acs = _ssd_common(dtr_ref, dtb_ref, alog_ref, acs_s, acsT_s)
        for g in range(SSD_GROUPS):
            bg = ub[:, g * SSD_STATE:(g + 1) * SSD_STATE]
            cb_ = ub[:, (SSD_GROUPS + g) * SSD_STATE:(SSD_GROUPS + g + 1) * SSD_STATE].astype(BF16)
            cbm = _dot_nt(cb_, bg.astype(BF16))
            btb = bg.T.astype(BF16)
            for pr in range(g * pairs_per_group, (g + 1) * pairs_per_group):
                t = _pair_terms(pr, acs, dtv, acs_s, lane, lo, lane1, lo1)
                xs_p = ux[:, pr * PAIR:(pr + 1) * PAIR]
                x_p = xs_p * t["dt_p"]
                st = st_s[pr]
                sin_ref[0, 0, pr] = st
                y = _dot(cb_, st.astype(BF16)) * jnp.exp(t["acs_p"])
                for k in range(2):
                    row = acsT_s[t["h"][k]:t["h"][k] + 1, :]
                    lm, _ = _decay_tiles(t["c"][k], row, tri, False)
                    xm = jnp.where(lo if k == 0 else ~lo, x_p, 0.0).astype(BF16)
                    y = y + _dot((cbm * lm).astype(BF16), xm)
                d_p = jnp.where(lo1, _lane_col(dsk_ref[...], lane1, t["h"][0]), _lane_col(dsk_ref[...], lane1, t["h"][1]))
                y_ref[0, :, pr * PAIR:(pr + 1) * PAIR] = y + d_p * xs_p
                xd = (x_p * jnp.exp(t["last_p"] - t["acs_p"])).astype(BF16)
                st_s[pr] = st * jnp.exp(t["last_p"]) + _dot(btb, xd)

    return pl.pallas_call(
        body,
        name="ssd_fwd",
        grid=(Bl, nc),
        in_specs=specs,
        out_specs=[
            pl.BlockSpec((1, BLK, D), lambda b, c: (b, c, 0)),
            pl.BlockSpec((1, 1, n_pair, SSD_STATE, PAIR), lambda b, c: (b, c, 0, 0, 0)),
        ],
        out_shape=[jax.ShapeDtypeStruct((Bl, L, D), F32),
                   jax.ShapeDtypeStruct((Bl, nc, n_pair, SSD_STATE, PAIR), F32)],
        scratch_shapes=[pltpu.VMEM((n_pair, SSD_STATE, PAIR), F32), pltpu.VMEM((HALO + BLK, D), F32),
                        pltpu.VMEM((HALO + BLK, 512), F32), pltpu.VMEM((BLK, LANES), F32),
                        pltpu.VMEM((LANES, BLK), F32)],
        compiler_params=_params(("arbitrary", "arbitrary")),
    )(proj3, proj3, proj3, proj3, proj3, cwx, cwb, cbx, cbb, dtb, alog, dsk)


def _ssd_bwd(proj3, s_in, dy3, cwx, cwb, cbx, cbb, dtb, alog, dsk, D, tail):
    Bl, L, _ = proj3.shape
    CD = D + 512
    nc = L // BLK
    n_pair = D // PAIR
    n_heads = 2 * n_pair
    pairs_per_group = n_pair // SSD_GROUPS
    specs, cidx = _ssd_specs(Bl, L, D, True)
    specs = specs + [
        pl.BlockSpec((1, 1, n_pair, SSD_STATE, PAIR), lambda b, c: (b, cidx(c), 0, 0, 0)),
        pl.BlockSpec((1, BLK, D), lambda b, c: (b, cidx(c), 0)),
    ]

    def body(xr_ref, bcr_ref, xh_ref, bch_ref, dtr_ref, cwx_ref, cwb_ref, cbx_ref, cbb_ref, dtb_ref, alog_ref,
             dsk_ref, sin_ref, dy_ref, dxbc_ref, dcwx_ref, dcwb_ref, dcbx_ref, dcbb_ref, misc_ref,
             dst_s, extx_s, extb_s, acs_s, acsT_s, dux_s, dub_s, e2x_s, e2b_s, nxx_s, nxb_s):
        step = pl.program_id(1)
        first = step == nc - 1
        last = step == 0

        @pl.when(last)
        def _():
            dst_s[...] = jnp.zeros_like(dst_s)
            nxx_s[...] = jnp.zeros_like(nxx_s)
            nxb_s[...] = jnp.zeros_like(nxb_s)

        @pl.when(last & (pl.program_id(0) == 0))
        def _():
            for r in (dcwx_ref, dcwb_ref, dcbx_ref, dcbb_ref, misc_ref):
                r[...] = jnp.zeros_like(r)

        lane, lane1 = _iota((BLK, LANES), 1), _iota((1, LANES), 1)
        lo, lo1 = lane < HEAD_DIM, lane1 < HEAD_DIM
        tri = _iota((BLK, BLK), 1) <= _iota((BLK, BLK), 0)
        prex = _conv_pre(extx_s, xh_ref, xr_ref, cwx_ref, cbx_ref, first)
        sgx = _sigmoid(prex)
        ux = prex * sgx
        preb = _conv_pre(extb_s, bch_ref, bcr_ref, cwb_ref, cbb_ref, first)
        sgb = _sigmoid(preb)
        ub = preb * sgb
        dtv, a, acs = _ssd_common(dtr_ref, dtb_ref, alog_ref, acs_s, acsT_s)
        dacs = jnp.zeros((BLK, LANES), F32)
        dlast = jnp.zeros((1, LANES), F32)
        ddt = jnp.zeros((BLK, LANES), F32)
        dd = jnp.zeros((1, LANES), F32)
        for g in range(SSD_GROUPS):
            bg = ub[:, g * SSD_STATE:(g + 1) * SSD_STATE]
            cg = ub[:, (SSD_GROUPS + g) * SSD_STATE:(SSD_GROUPS + g + 1) * SSD_STATE]
            bb, cb_ = bg.astype(BF16), cg.astype(BF16)
            cbm = _dot_nt(cb_, bb)
            cbt = _dot_nt(bb, cb_)
            ctb = cg.T.astype(BF16)
            dbg = jnp.zeros((BLK, SSD_STATE), F32)
            dcg = jnp.zeros((BLK, SSD_STATE), F32)
            for pr in range(g * pairs_per_group, (g + 1) * pairs_per_group):
                t = _pair_terms(pr, acs, dtv, acs_s, lane, lo, lane1, lo1)
                h0, h1 = t["h"]
                xs_p = ux[:, pr * PAIR:(pr + 1) * PAIR]
                dy_p = dy_ref[0, :, pr * PAIR:(pr + 1) * PAIR]
                x_p = xs_p * t["dt_p"]
                ea_p = jnp.exp(t["acs_p"])
                dte_p = jnp.exp(t["last_p"] - t["acs_p"])
                cd_p = jnp.exp(t["last_p"])
                st = sin_ref[0, 0, pr]
                dst = dst_s[pr]
                stb, dstb = st.astype(BF16), dst.astype(BF16)
                s0, s1 = _half_sums(_colsum(dy_p * xs_p), lo1)
                dd = dd + jnp.where(lane1 == h0, s0, 0.0) + jnp.where(lane1 == h1, s1, 0.0)
                d_p = jnp.where(lo1, _lane_col(dsk_ref[...], lane1, h0), _lane_col(dsk_ref[...], lane1, h1))
                dxs_p = d_p * dy_p
                dp = dy_p * ea_p
                dpb = dp.astype(BF16)
                yo = dp * _dot(cb_, stb)
                dcg = dcg + _dot_nt(dpb, stb)
                dst_off = _dot(ctb, dpb)
                dac = [_rowsum(jnp.where(lo, yo, 0.0)), _rowsum(jnp.where(lo, 0.0, yo))]
                s0, s1 = _half_sums(_colsum(dst * st), lo1)
                dl = [s0 * jnp.exp(t["last"][0]), s1 * jnp.exp(t["last"][1])]
                dxd = _dot(bb, dstb)
                dx_p = dxd * dte_p
                tt = dxd * x_p
                dbg = dbg + _dot_nt((x_p * dte_p).astype(BF16), dstb)
                for k, ddte in enumerate((_rowsum(jnp.where(lo, tt, 0.0)), _rowsum(jnp.where(lo, 0.0, tt)))):
                    ek = ddte * jnp.exp(t["last"][k] - t["c"][k])
                    dl[k] = dl[k] + _colsum(ek)
                    dac[k] = dac[k] - ek
                x_pb = x_p.astype(BF16)
                for k in range(2):
                    row = acsT_s[t["h"][k]:t["h"][k] + 1, :]
                    lm, lmt = _decay_tiles(t["c"][k], row, tri, True)
                    dym = jnp.where(lo if k == 0 else ~lo, dy_p, 0.0).astype(BF16)
                    dm = _dot_nt(dym, x_pb)
                    dmt = _dot_nt(x_pb, dym)
                    mt = cbt * lmt
                    dx_p = dx_p + _dot(mt.astype(BF16), dym)
                    dac[k] = dac[k] + _rowsum(dm * (cbm * lm)) - _rowsum(dmt * mt)
                    dcg = dcg + _dot((dm * lm).astype(BF16), bb)
                    dbg = dbg + _dot((dmt * lmt).astype(BF16), cb_)
                dacs = dacs + jnp.where(lane == h0, dac[0], 0.0) + jnp.where(lane == h1, dac[1], 0.0)
                dlast = dlast + jnp.where(lane1 == h0, dl[0], 0.0) + jnp.where(lane1 == h1, dl[1], 0.0)
                dxs_p = dxs_p + dx_p * t["dt_p"]
                t3 = dx_p * xs_p
                ddt = ddt + jnp.where(lane == h0, _rowsum(jnp.where(lo, t3, 0.0)), 0.0) \
                    + jnp.where(lane == h1, _rowsum(jnp.where(lo, 0.0, t3)), 0.0)
                dux_s[:, pr * PAIR:(pr + 1) * PAIR] = dxs_p
                dst_s[pr] = dst * cd_p + dst_off
            dub_s[:, g * SSD_STATE:(g + 1) * SSD_STATE] = dbg
            dub_s[:, (SSD_GROUPS + g) * SSD_STATE:(SSD_GROUPS + g + 1) * SSD_STATE] = dcg
        dacs = dacs + jnp.where(_iota((BLK, LANES), 0) == BLK - 1, dlast, 0.0)
        utri = jnp.where(_iota((BLK, BLK), 1) >= _iota((BLK, BLK), 0), 1.0, 0.0).astype(BF16)
        dda = _dot_split(utri, dacs)
        ddt = ddt + dda * a
        ddtr = jnp.where(lane < n_heads, ddt * _sigmoid(dtr_ref[0] + dtb_ref[...]), 0.0)
        dxbc_ref[0, :, CD:CD + LANES] = ddtr.astype(BF16)
        dxbc_ref[0, :, CD + LANES:tail] = jnp.zeros((BLK, tail - CD - LANES), BF16)
        misc_ref[0:1, :] += _colsum(ddtr)
        misc_ref[1:2, :] += jnp.where(lane1 < n_heads, _colsum(dda * dtv) * a, 0.0)
        misc_ref[2:3, :] += dd
        for (du_s, pre, sg, ext_s, e2_s, nx_s, w_ref, dcw_ref, dcb_ref, c0, width) in (
                (dux_s, prex, sgx, extx_s, e2x_s, nxx_s, cwx_ref, dcwx_ref, dcbx_ref, 0, D),
                (dub_s, preb, sgb, extb_s, e2b_s, nxb_s, cwb_ref, dcwb_ref, dcbb_ref, D, 512)):
            dpre = du_s[...] * (sg * (1.0 + pre * (1.0 - sg)))
            dcb_ref[...] += _colsum(dpre)
            for i in range(CONV_K):
                dcw_ref[i:i + 1, :] += _colsum(dpre * ext_s[pl.ds(HALO - (CONV_K - 1 - i), BLK), :])
            e2_s[0:BLK, :] = dpre
            e2_s[BLK:BLK + HALO, :] = nx_s[...]
            dxr = jnp.zeros((BLK, width), F32)
            for i in range(CONV_K):
                dxr = dxr + e2_s[pl.ds(CONV_K - 1 - i, BLK), :] * w_ref[i:i + 1, :]
            dxbc_ref[0, :, c0:c0 + width] = dxr.astype(BF16)
            nx_s[...] = e2_s[0:HALO, :]

    full = lambda shape: pl.BlockSpec(shape, lambda b, c: (0,) * len(shape))
    return pl.pallas_call(
        body,
        name="ssd_bwd",
        grid=(Bl, nc),
        in_specs=specs,
        out_specs=[
            pl.BlockSpec((1, BLK, tail), lambda b, c: (b, cidx(c), 0)),
            full((CONV_K, D)), full((CONV_K, 512)), full((1, D)), full((1, 512)), full((8, LANES)),
        ],
        out_shape=[
            jax.ShapeDtypeStruct((Bl, L, tail), BF16),
            jax.ShapeDtypeStruct((CONV_K, D), F32), jax.ShapeDtypeStruct((CONV_K, 512), F32),
            jax.ShapeDtypeStruct((1, D), F32), jax.ShapeDtypeStruct((1, 512), F32),
            jax.ShapeDtypeStruct((8, LANES), F32),
        ],
        scratch_shapes=[
            pltpu.VMEM((n_pair, SSD_STATE, PAIR), F32),
            pltpu.VMEM((HALO + BLK, D), F32), pltpu.VMEM((HALO + BLK, 512), F32),
            pltpu.VMEM((BLK, LANES), F32), pltpu.VMEM((LANES, BLK), F32),
            pltpu.VMEM((BLK, D), F32), pltpu.VMEM((BLK, 512), F32),
            pltpu.VMEM((BLK + HALO, D), F32), pltpu.VMEM((BLK + HALO, 512), F32),
            pltpu.VMEM((HALO, D), F32), pltpu.VMEM((HALO, 512), F32),
        ],
        compiler_params=_params(("arbitrary", "arbitrary")),
    )(proj3, proj3, proj3, proj3, proj3, cwx, cwb, cbx, cbb, dtb, alog, dsk, s_in, dy3)


def _gate_out(x2, tgt2, o2, proj2, y2, sbw, ssw, w_out_bf, w_out_t):
    T, D = x2.shape
    tm = min(256, T)

    def body(x_ref, t_ref, o_ref, zs_ref, y_ref, zy_ref, sbw_ref, ssw_ref, wo_ref, wot_ref,
             dout_ref, doutb_ref, mixt_ref, do_ref, dy_ref, dz_ref, dnw_ref, loss_ref):
        @pl.when(pl.program_id(0) == 0)
        def _():
            dnw_ref[...] = jnp.zeros_like(dnw_ref)
            loss_ref[...] = jnp.zeros_like(loss_ref)

        def fwd(o, z, w):
            sg = _sigmoid(z)
            sl = z * sg
            g = o * sl
            r = lax.rsqrt(jnp.mean(g * g, axis=-1, keepdims=True) + EPS)
            n = g * r
            return sg, sl, r, n, n * w

        def bwd(dy, o, z, w, sg, sl, r, n):
            dn = dy * w
            dg = r * (dn - n * jnp.mean(dn * n, axis=-1, keepdims=True))
            return dg * sl, dg * o * (sg * (1.0 + z * (1.0 - sg))), _colsum(dy * n)

        o1, z1, w1 = o_ref[...], zs_ref[...], sbw_ref[...]
        o2_, z2, w2 = y_ref[...], zy_ref[...], ssw_ref[...]
        sg1, sl1, r1, n1, y1 = fwd(o1, z1, w1)
        sg2, sl2, r2, n2, y2_ = fwd(o2_, z2, w2)
        y1b, y2b = y1.astype(BF16), y2_.astype(BF16)
        mixt_ref[0:D, :] = y1.T.astype(BF16)
        mixt_ref[D:2 * D, :] = y2_.T.astype(BF16)
        out = x_ref[...] + (_dot(y1b, wo_ref[0:D, :]) + _dot(y2b, wo_ref[D:2 * D, :]))
        err = out - t_ref[...]
        loss_ref[...] += jnp.sum(err * err) * (0.5 / D)
        dout = err * (1.0 / D)
        dout_ref[...] = dout
        doutb = dout.astype(BF16)
        doutb_ref[...] = doutb
        do1, dz1, dw1 = bwd(_dot(doutb, wot_ref[:, 0:D]), o1, z1, w1, sg1, sl1, r1, n1)
        do2, dz2, dw2 = bwd(_dot(doutb, wot_ref[:, D:2 * D]), o2_, z2, w2, sg2, sl2, r2, n2)
        do_ref[...] = do1
        dy_ref[...] = do2
        dz_ref[:, 0:D] = dz1.astype(BF16)
        dz_ref[:, D:2 * D] = dz2.astype(BF16)
        dnw_ref[0:1, :] += dw1
        dnw_ref[1:2, :] += dw2

    row = lambda col: pl.BlockSpec((tm, D), lambda i: (i, col))
    full = lambda shape: pl.BlockSpec(shape, lambda i: (0,) * len(shape))
    wide = pl.BlockSpec((tm, 2 * D), lambda i: (i, 0))
    return pl.pallas_call(
        body,
        name="gate_out",
        grid=(T // tm,),
        in_specs=[row(0), row(0), row(0), row(3), row(0), row(4), full((1, D)), full((1, D)), full((2 * D, D)),
                  full((D, 2 * D))],
        out_specs=[row(0), row(0), pl.BlockSpec((2 * D, tm), lambda i: (0, i)), row(0), row(0), wide,
                   full((8, D)), full((8, LANES))],
        out_shape=[
            jax.ShapeDtypeStruct((T, D), F32), jax.ShapeDtypeStruct((T, D), BF16),
            jax.ShapeDtypeStruct((2 * D, T), BF16), jax.ShapeDtypeStruct((T, D), F32),
            jax.ShapeDtypeStruct((T, D), F32), jax.ShapeDtypeStruct((T, 2 * D), BF16),
            jax.ShapeDtypeStruct((8, D), F32), jax.ShapeDtypeStruct((8, LANES), F32),
        ],
        compiler_params=_params(("arbitrary",)),
    )(x2, tgt2, o2, proj2, y2, proj2, sbw, ssw, w_out_bf, w_out_t)


def _piece_blocks(pieces, D):
    counts = [p.shape[1] // D for p in pieces]
    return [sum(counts[:i]) for i in range(len(counts))], counts


def _dhn(pieces, w_pad_t, x2, dout, norm_w, h_in, h_out, slab_off, slab_w):
    T, D = x2.shape
    tm = min(1024, T)
    starts, counts = _piece_blocks(pieces, D)
    nk = sum(counts)
    ni = T // tm
    n_sem = 2 * (N_CHIPS - 1)
    assert nk * D == w_pad_t.shape[0]

    def body(*refs):
        p_refs = refs[:len(pieces)]
        (w_ref, x_hbm, dout_hbm, nw_ref, hin, hout, gx_ref, dnw_ref, rin, rout,
         acc_s, x_s, dout_s, send_sems, recv_sems, row_sems, own_sems) = refs[len(pieces):]
        i, k = pl.program_id(0), pl.program_id(1)

        def rows():
            r0 = pl.multiple_of(i * tm, tm)
            return [pltpu.make_async_copy(src.at[pl.ds(r0, tm)], dst, row_sems.at[n])
                    for n, (src, dst) in enumerate(((x_hbm, x_s), (dout_hbm, dout_s)))]

        @pl.when(k == 0)
        def _():
            for cp in rows():
                cp.start()

        def scatter():
            x, y, c, chips = _place()

            def slab(p):
                return hin.at[:, pl.ds(pl.multiple_of(p * slab_off, LANES), slab_w)]

            cps = []
            for j, (px, py) in enumerate(chips):
                for m, (src, dst) in enumerate(((slab(2 * px + py), rin.at[j]), (hout.at[2 * px + py], rout.at[j]))):
                    cps.append(pltpu.make_async_remote_copy(
                        src_ref=src, dst_ref=dst, send_sem=send_sems.at[2 * j + m], recv_sem=recv_sems.at[2 * j + m],
                        device_id=(px, py, c), device_id_type=MESH))
            me = 2 * x + y
            own = [pltpu.make_async_copy(slab(me), rin.at[N_CHIPS - 1], own_sems.at[0]),
                   pltpu.make_async_copy(hout.at[me], rout.at[N_CHIPS - 1], own_sems.at[1])]
            return cps + own

        @pl.when((i == 0) & (k == 0))
        def _():
            for cp in scatter():
                cp.start()

        @pl.when((i == ni - 1) & (k == nk - 1))
        def _():
            for cp in scatter():
                cp.wait()

        @pl.when((i == 0) & (k == 0))
        def _():
            dnw_ref[...] = jnp.zeros_like(dnw_ref)

        @pl.when(k == 0)
        def _():
            acc_s[...] = jnp.zeros_like(acc_s)

        for p_ref, s, n in zip(p_refs, starts, counts):
            @pl.when((k >= s) & (k < s + n))
            def _(p_ref=p_ref):
                acc_s[...] += _dot(p_ref[...], w_ref[...])

        @pl.when(k == nk - 1)
        def _():
            for cp in rows():
                cp.wait()
            xv = x_s[...]
            r = lax.rsqrt(jnp.mean(xv * xv, axis=-1, keepdims=True) + EPS)
            xh = xv * r
            dhn = acc_s[...]
            dxh = dhn * nw_ref[...]
            gx_ref[...] = dout_s[...] + r * (dxh - xh * jnp.mean(dxh * xh, axis=-1, keepdims=True))
            dnw_ref[0:1, :] += _colsum(dhn * xh)

    return pl.pallas_call(
        body,
        name="dhn",
        grid=(T // tm, nk),
        in_specs=[pl.BlockSpec((tm, D), lambda i, k, s=s, n=n: (i, jnp.clip(k - s, 0, n - 1)))
                  for s, n in zip(starts, counts)] + [
            pl.BlockSpec((D, D), lambda i, k: (k, 0)),
            ANY, ANY,
            pl.BlockSpec((1, D), lambda i, k: (0, 0)),
            ANY, ANY,
        ],
        out_specs=[pl.BlockSpec((tm, D), lambda i, k: (i, 0)), pl.BlockSpec((8, D), lambda i, k: (0, 0)), ANY, ANY],
        out_shape=[jax.ShapeDtypeStruct((T, D), F32), jax.ShapeDtypeStruct((8, D), F32),
                   jax.ShapeDtypeStruct((N_CHIPS, h_in.shape[0], slab_w), F32),
                   jax.ShapeDtypeStruct((N_CHIPS,) + h_out.shape[1:], F32)],
        scratch_shapes=[pltpu.VMEM((tm, D), F32)] * 3 + [pltpu.SemaphoreType.DMA((n_sem,)), pltpu.SemaphoreType.DMA((n_sem,)),
                                                      pltpu.SemaphoreType.DMA((2,)), pltpu.SemaphoreType.DMA((2,))],
        compiler_params=_params(("arbitrary", "arbitrary")),
    )(*pieces, w_pad_t, x2, dout, norm_w, h_in, h_out)


def _grad_w_in(hn_t, pieces):
    D, T = hn_t.shape
    tk = min(1024, T)
    starts, counts = _piece_blocks(pieces, D)

    def body(*refs):
        a_ref, p_refs, o_ref = refs[0], refs[1:-1], refs[-1]
        j = pl.program_id(0)

        @pl.when(pl.program_id(1) == 0)
        def _():
            o_ref[...] = jnp.zeros_like(o_ref)

        for p_ref, s, n in zip(p_refs, starts, counts):
            @pl.when((j >= s) & (j < s + n))
            def _(p_ref=p_ref):
                o_ref[...] += _dot(a_ref[...], p_ref[...])

    def piece_spec(s, n):
        return pl.BlockSpec((tk, D), lambda j, k: (jnp.where((j >= s) & (j < s + n), k, 0), jnp.clip(j - s, 0, n - 1)))

    return pl.pallas_call(
        body,
        name="grad_w_in",
        grid=(sum(counts), T // tk),
        in_specs=[pl.BlockSpec((D, tk), lambda j, k: (0, k))] + [piece_spec(s, n) for s, n in zip(starts, counts)],
        out_specs=pl.BlockSpec((D, D), lambda j, k: (0, j)),
        out_shape=jax.ShapeDtypeStruct((D, sum(counts) * D), F32),
        compiler_params=_params(("parallel", "arbitrary")),
    )(hn_t, *pieces)


def _grad_w_out(a, b, g_in, width):
    M, K = a.shape
    N = b.shape[1]
    tm = min(1024, M)
    tn = 1024 if N % 1024 == 0 else (512 if N % 512 == 0 else N)
    tk = min(512, K)
    grid = (M // tm, N // tn, K // tk)
    h = g_in.shape[0] // 2

    def body(a_ref, b_ref, gin, o_ref, rin, send_sem, recv_sem):
        ids = [pl.program_id(d) for d in range(3)]

        def swap():
            x, y, c, _ = _place()
            return pltpu.make_async_remote_copy(
                src_ref=gin.at[pl.ds((1 - c) * h, h), pl.ds(0, width)], dst_ref=rin, send_sem=send_sem, recv_sem=recv_sem,
                device_id=(x, y, 1 - c), device_id_type=MESH)

        @pl.when((ids[0] == 0) & (ids[1] == 0) & (ids[2] == 0))
        def _():
            swap().start()

        @pl.when(ids[2] == 0)
        def _():
            o_ref[...] = jnp.zeros_like(o_ref)

        o_ref[...] += _dot(a_ref[...], b_ref[...])

        @pl.when((ids[0] == grid[0] - 1) & (ids[1] == grid[1] - 1) & (ids[2] == grid[2] - 1))
        def _():
            swap().wait()

    return pl.pallas_call(
        body,
        name="grad_w_out",
        grid=grid,
        in_specs=[pl.BlockSpec((tm, tk), lambda i, j, k: (i, k)), pl.BlockSpec((tk, tn), lambda i, j, k: (k, j)), ANY],
        out_specs=[pl.BlockSpec((tm, tn), lambda i, j, k: (i, j)), ANY],
        out_shape=[jax.ShapeDtypeStruct((M, N), F32), jax.ShapeDtypeStruct((h, width), F32)],
        scratch_shapes=[pltpu.SemaphoreType.DMA, pltpu.SemaphoreType.DMA],
        compiler_params=_params(("arbitrary", "arbitrary", "arbitrary")),
    )(a, b, g_in)


def _adamw(w, g, m, v, name):
    R, C = w.shape
    tr = 256 if R % 256 == 0 else R
    c1 = 1.0 - ADAM_B1 ** ADAM_STEP
    c2 = 1.0 - ADAM_B2 ** ADAM_STEP

    def body(w_ref, g_ref, m_ref, v_ref, d_ref, nm_ref, nv_ref):
        gv = g_ref[...]
        m_new = ADAM_B1 * m_ref[...] + (1.0 - ADAM_B1) * gv
        v_new = ADAM_B2 * v_ref[...] + (1.0 - ADAM_B2) * (gv * gv)
        d_ref[...] = -ADAM_LR * ((m_new / c1) / (jnp.sqrt(v_new / c2) + ADAM_EPS) + ADAM_WD * w_ref[...])
        nm_ref[...] = m_new
        nv_ref[...] = v_new

    spec = pl.BlockSpec((tr, C), lambda i: (i, 0))
    return pl.pallas_call(
        body,
        name=name,
        grid=(R // tr,),
        in_specs=[spec] * 4,
        out_specs=[spec] * 3,
        out_shape=[jax.ShapeDtypeStruct((R, C), F32)] * 3,
        compiler_params=_params(("parallel",)),
    )(w, g, m, v)


def _add_core_rows(g, recv, core, name):
    h, width = recv.shape
    th = 128 if h % 128 == 0 else h

    def body(c_ref, g_ref, r_ref, o_ref):
        o_ref[...] = g_ref[...] + r_ref[...]

    return pl.pallas_call(
        body,
        name=name,
        grid_spec=pltpu.PrefetchScalarGridSpec(
            num_scalar_prefetch=1,
            grid=(h // th,),
            in_specs=[
                pl.BlockSpec((th, width), lambda i, c: (c[0] * (h // th) + i, 0)),
                pl.BlockSpec((th, width), lambda i, c: (i, 0)),
            ],
            out_specs=pl.BlockSpec((th, width), lambda i, c: (i, 0)),
        ),
        out_shape=jax.ShapeDtypeStruct((h, width), F32),
        compiler_params=_params(("parallel",)),
    )(core, g, recv)


def _add_core_blocks(g, recv, core, name):
    n, hb, C = recv.shape

    def body(c_ref, g_ref, r_ref, o_ref):
        o_ref[...] = g_ref[...] + r_ref[...]

    return pl.pallas_call(
        body,
        name=name,
        grid_spec=pltpu.PrefetchScalarGridSpec(
            num_scalar_prefetch=1,
            grid=(n,),
            in_specs=[
                pl.BlockSpec((hb, C), lambda p, c: (2 * p + c[0], 0)),
                pl.BlockSpec((None, hb, C), lambda p, c: (p, 0, 0)),
            ],
            out_specs=pl.BlockSpec((None, hb, C), lambda p, c: (p, 0, 0)),
        ),
        out_shape=jax.ShapeDtypeStruct((n, hb, C), F32),
        compiler_params=_params(("parallel",)),
    )(core, g, recv)


def _add_chips(recv, name):
    _, h, W = recv.shape
    th = 256 if h % 256 == 0 else h

    def body(r_ref, o_ref):
        o_ref[...] = ((r_ref[N_CHIPS - 1] + r_ref[0]) + r_ref[1]) + r_ref[2]

    return pl.pallas_call(
        body,
        name=name,
        grid=(h // th,),
        in_specs=[pl.BlockSpec((N_CHIPS, th, W), lambda i: (0, i, 0))],
        out_specs=pl.BlockSpec((th, W), lambda i: (i, 0)),
        out_shape=jax.ShapeDtypeStruct((h, W), F32),
        compiler_params=_params(("parallel",)),
    )(recv)


def _place():
    x, y, c = lax.axis_index("x"), lax.axis_index("y"), lax.axis_index("c")
    other_chips = [(1 - x, y), (x, 1 - y), (1 - x, 1 - y)]
    return x, y, c, other_chips


def _allgather_w_in(w_in_bf, x2, norm_w):
    D, S = w_in_bf.shape
    T = x2.shape[0]
    tm = min(1024, T)
    ni = T // tm
    n_ici = n_fwd = N_CHIPS - 1

    def body(win, x_ref, nw_ref, gin, hn_ref, hnt_ref, send_sems, recv_sems):
        step = pl.program_id(0)
        xv = x_ref[...]
        hn = xv * lax.rsqrt(jnp.mean(xv * xv, axis=-1, keepdims=True) + EPS) * nw_ref[...]
        hn_ref[...] = hn.astype(BF16)
        hnt_ref[...] = hn.T.astype(BF16)
        x, y, c, chips = _place()
        me = 2 * x + y
        sibling = (x, y, 1 - c)
        hin = D // 2

        def half(chip_idx, core):
            return gin.at[chip_idx, pl.ds(core * hin, hin)]

        def rcopy(k, src, dst, to):
            return pltpu.make_async_remote_copy(src_ref=src, dst_ref=dst, send_sem=send_sems.at[k],
                                                recv_sem=recv_sems.at[k], device_id=to, device_id_type=MESH)

        def sends():
            return [rcopy(j, win.at[pl.ds(c * hin, hin)], half(me, c), (*chip, c)) for j, chip in enumerate(chips)]

        @pl.when(step == 0)
        def _():
            for cp in sends():
                cp.start()

        @pl.when(step == ni - 1)
        def _():
            passed = []
            for j, (px, py) in enumerate(chips):
                theirs = half(2 * px + py, c)
                rcopy(j, theirs, theirs, sibling).wait_recv()
                passed.append(rcopy(n_ici + j, theirs, theirs, sibling))
                passed[-1].start()
            for j, (px, py) in enumerate(chips):
                other = half(2 * px + py, 1 - c)
                rcopy(n_ici + j, other, other, sibling).wait_recv()
            for cp in sends() + passed:
                cp.wait_send()

    return pl.pallas_call(
        body,
        name="allgather_w_in",
        grid=(ni,),
        in_specs=[ANY, pl.BlockSpec((tm, D), lambda i: (i, 0)), pl.BlockSpec((1, D), lambda i: (0, 0))],
        out_specs=[ANY, pl.BlockSpec((tm, D), lambda i: (i, 0)), pl.BlockSpec((D, tm), lambda i: (0, i))],
        out_shape=[jax.ShapeDtypeStruct((N_CHIPS, D, S), BF16),
                   jax.ShapeDtypeStruct((T, D), BF16), jax.ShapeDtypeStruct((D, T), BF16)],
        scratch_shapes=[pltpu.SemaphoreType.DMA((n_ici + n_fwd,)), pltpu.SemaphoreType.DMA((n_ici + n_fwd,))],
        compiler_params=_params(("arbitrary",)),
    )(w_in_bf, x2, norm_w)


def _allreduce_small(packed):
    R = packed.shape[0]
    n_dev = 2 * N_CHIPS

    def body(p_ref, o_ref, buf, send_sems, recv_sems):
        x, y, c, _ = _place()
        me = 4 * x + 2 * y + c
        buf[me] = p_ref[...]
        copies = []
        for k in range(1, n_dev):
            px = 1 - x if k & 4 else x
            py = 1 - y if k & 2 else y
            pc = 1 - c if k & 1 else c
            copies.append((pltpu.make_async_remote_copy(
                src_ref=buf.at[me], dst_ref=buf.at[me], send_sem=send_sems.at[k - 1], recv_sem=recv_sems.at[k - 1],
                device_id=(px, py, pc), device_id_type=MESH), 4 * px + 2 * py + pc, (px, py, pc)))
        for cp, _, _ in copies:
            cp.start()
        for k, (_, peer, to) in enumerate(copies):
            pltpu.make_async_remote_copy(
                src_ref=buf.at[peer], dst_ref=buf.at[peer], send_sem=send_sems.at[k], recv_sem=recv_sems.at[k],
                device_id=to, device_id_type=MESH).wait_recv()
        for cp, _, _ in copies:
            cp.wait_send()
        acc = buf[0]
        for d in range(1, n_dev):
            acc = acc + buf[d]
        o_ref[...] = acc

    vm = pl.BlockSpec(memory_space=pltpu.VMEM)
    return pl.pallas_call(
        body,
        name="allreduce_small",
        in_specs=[vm],
        out_specs=vm,
        out_shape=jax.ShapeDtypeStruct((R, LANES), F32),
        scratch_shapes=[pltpu.VMEM((n_dev, R, LANES), F32), pltpu.SemaphoreType.DMA((n_dev - 1,)),
                        pltpu.SemaphoreType.DMA((n_dev - 1,))],
    )(packed)


def _swap_core_halves(g_out):
    hb = g_out.shape[0] // (2 * N_CHIPS)

    def body(gout, rout, send_sems, recv_sems):
        x, y, c, _ = _place()
        cps = [pltpu.make_async_remote_copy(
            src_ref=gout.at[pl.ds((2 * p + 1 - c) * hb, hb)], dst_ref=rout.at[p], send_sem=send_sems.at[p],
            recv_sem=recv_sems.at[p], device_id=(x, y, 1 - c), device_id_type=MESH) for p in range(N_CHIPS)]
        for cp in cps:
            cp.start()
        for cp in cps:
            cp.wait()

    return pl.pallas_call(
        body,
        name="reduce_core_swap",
        in_specs=[ANY],
        out_specs=ANY,
        out_shape=jax.ShapeDtypeStruct((N_CHIPS, hb, g_out.shape[1]), F32),
        scratch_shapes=[pltpu.SemaphoreType.DMA((N_CHIPS,)), pltpu.SemaphoreType.DMA((N_CHIPS,))],
    )(g_out)


def _join_core_halves(g_in, g_out):
    def body(gin, gout, fin, fout, send_sems, recv_sems):
        x, y, c, _ = _place()
        cps = [pltpu.make_async_remote_copy(src_ref=s, dst_ref=d.at[c], send_sem=send_sems.at[k],
                                            recv_sem=recv_sems.at[k], device_id=(x, y, 1 - c), device_id_type=MESH)
               for k, (s, d) in enumerate(((gin, fin), (gout, fout)))]
        for cp in cps:
            cp.start()
        for k, (s, d) in enumerate(((gin, fin), (gout, fout))):
            pltpu.make_async_remote_copy(src_ref=s, dst_ref=d.at[1 - c], send_sem=send_sems.at[k],
                                         recv_sem=recv_sems.at[k], device_id=(x, y, 1 - c),
                                         device_id_type=MESH).wait_recv()
        for cp in cps:
            cp.wait_send()

    return pl.pallas_call(
        body,
        name="reduce_core_join",
        in_specs=[ANY, ANY],
        out_specs=[ANY, ANY],
        out_shape=[jax.ShapeDtypeStruct((2,) + g_in.shape, F32), jax.ShapeDtypeStruct((2,) + g_out.shape, F32)],
        scratch_shapes=[pltpu.SemaphoreType.DMA((2,)), pltpu.SemaphoreType.DMA((2,))],
    )(g_in, g_out)


def _pack(arrays):
    rows = []
    for a in arrays:
        flat = a.reshape(-1).astype(F32)
        n = -(-flat.shape[0] // LANES) * LANES
        rows.append(jnp.pad(flat, (0, n - flat.shape[0])).reshape(-1, LANES))
    out = jnp.concatenate(rows, axis=0)
    return jnp.pad(out, ((0, -out.shape[0] % 8), (0, 0)))


def _unpack(packed, shapes):
    out, r = [], 0
    for shp in shapes:
        n = math.prod(shp)
        nr = -(-n // LANES)
        out.append(packed[r:r + nr].reshape(-1)[:n].reshape(shp))
        r += nr
    return out


def _pad_lanes(a):
    return jnp.pad(a, ((0, 0), (0, LANES - a.shape[1])))


def kernel(x, norm_w, w_in, q_norm_w, k_norm_w, conv_w, conv_b, dt_bias, A_log, D_skip, sb_norm_w, ssd_norm_w, w_out, loss_target, m_norm_w, m_w_in, m_q_norm_w, m_k_norm_w, m_conv_w, m_conv_b, m_dt_bias, m_A_log, m_D_skip, m_sb_norm_w, m_ssd_norm_w, m_w_out, v_norm_w, v_w_in, v_q_norm_w, v_k_norm_w, v_conv_w, v_conv_b, v_dt_bias, v_A_log, v_D_skip, v_sb_norm_w, v_ssd_norm_w, v_w_out):
    Bl, L, D = x.shape
    T = Bl * L
    S = w_in.shape[2]
    R = w_out.shape[1]
    CW = conv_w.shape[2]
    n_in = N_CHIPS * S
    CD = D + 2 * SSD_GROUPS * SSD_STATE
    H = D // HEAD_DIM
    n_main = 6 * D + 512
    P = -(-(n_main + LANES) // 1024) * 1024
    assert n_in == n_main + H and CD == N_CHIPS * CW and 2 * D == N_CHIPS * R and CD == D + 512
    chip = (2 * lax.axis_index("x") + lax.axis_index("y")).astype(jnp.int32)
    core = lax.axis_index("c").astype(jnp.int32)

    w_in_bf, w_out_shard_bf = w_in[0].astype(BF16), w_out[0].astype(BF16)
    x2 = x.reshape(T, D)
    g_in, hn, hn_t = _allgather_w_in(w_in_bf, x2, norm_w)
    g_in = lax.dynamic_update_slice(g_in, w_in_bf[None], (chip, 0, 0))
    w_pad = jnp.concatenate([g_in[p] for p in range(N_CHIPS)] + [jnp.zeros((D, P - n_in), BF16)], axis=1)
    proj, w_pad_t, g_out, g_cw = _inproj(hn, w_pad, w_out_shard_bf, conv_w[0])
    g_out = lax.dynamic_update_slice(g_out, w_out_shard_bf[None], (chip, 0, 0))
    g_cw = lax.dynamic_update_slice(g_cw, conv_w, (chip, 0, 0))
    w_out_bf = g_out.reshape(2 * D, D)
    conv_full = g_cw.transpose(1, 0, 2).reshape(CONV_K, CD)
    cwx, cwb = conv_full[:, :D], conv_full[:, D:]
    cbx, cbb = conv_b[:, :D], conv_b[:, D:]
    dtb, alog, dsk = _pad_lanes(dt_bias), _pad_lanes(A_log), _pad_lanes(D_skip)
    qw2, kw2 = jnp.tile(q_norm_w, (1, 2)), jnp.tile(k_norm_w, (1, 2))

    proj3 = proj.reshape(Bl, L, P)
    o_sb = _attn_fwd(proj3, qw2, kw2, D)
    y_ssd, s_in = _ssd_fwd(proj3, cwx, cwb, cbx, cbb, dtb, alog, dsk, D)
    dout, dout_bf, mixed_t, do_sb, dy_ssd, dz_bf, dnw_out, loss_blk = _gate_out(
        x2, loss_target.reshape(T, D), o_sb.reshape(T, D), proj, y_ssd.reshape(T, D), sb_norm_w, ssd_norm_w, w_out_bf,
        w_out_bf.T)

    dq, dk, dv, dqkw = _attn_bwd(proj3, o_sb, do_sb.reshape(Bl, L, D), qw2, kw2, D)
    dtail, dcwx, dcwb, dcbx, dcbb, misc = _ssd_bwd(
        proj3, s_in, dy_ssd.reshape(Bl, L, D), cwx, cwb, cbx, cbb, dtb, alog, dsk, D, P - 5 * D)
    dproj = [dq.reshape(T, D), dk.reshape(T, D), dv.reshape(T, D), dz_bf, dtail.reshape(T, P - 5 * D)]
    gw_in = _grad_w_in(hn_t, dproj)

    slab_off = S // LANES * LANES
    slab_w = -(-(S + (N_CHIPS - 1) * (S - slab_off)) // LANES) * LANES
    width = (N_CHIPS - 1) * slab_off + slab_w
    assert n_in <= width <= P
    core1 = core.reshape(1)
    gw_out, r_in = _grad_w_out(mixed_t, dout_bf, gw_in, width)
    r_out = _swap_core_halves(gw_out)
    h_in = _add_core_rows(gw_in, r_in, core1, "sum_cores_w_in")
    h_out = _add_core_blocks(gw_out, r_out, core1, "sum_cores_w_out")
    grad_x2, dnw_in, s_in_, s_out_ = _dhn(dproj, w_pad_t, x2, dout, norm_w, h_in, h_out, slab_off, slab_w)
    gh_in = _add_chips(s_in_, "sum_chips_w_in")
    gh_out = _add_chips(s_out_, "sum_chips_w_out")
    f_in, f_out = _join_core_halves(gh_in, gh_out)
    g_slab = lax.dynamic_update_slice(f_in, gh_in[None], (core, 0, 0)).reshape(D, slab_w)
    g_w_in = lax.dynamic_slice(g_slab, (0, chip * (S - slab_off)), (D, S))
    g_w_out = lax.dynamic_update_slice(f_out, gh_out[None], (core, 0, 0)).reshape(R, D)

    small_shapes = [(1, D), (1, D), (1, D), (1, CD), (1, HEAD_DIM), (1, HEAD_DIM), (1, H), (1, H), (1, H)]
    g_small_local = [dnw_in[0:1], dnw_out[0:1], dnw_out[1:2], jnp.concatenate([dcbx, dcbb], axis=1),
                     dqkw[0:1, :HEAD_DIM] + dqkw[0:1, HEAD_DIM:], dqkw[1:2, :HEAD_DIM] + dqkw[1:2, HEAD_DIM:],
                     misc[0:1, :H], misc[1:2, :H], misc[2:3, :H]]
    packed = _pack(g_small_local + [jnp.concatenate([dcwx, dcwb], axis=1), loss_blk[0:1, 0:1]])
    red = _allreduce_small(packed)
    g_small = _unpack(red, small_shapes + [(CONV_K, CD), (1, 1)])
    g_conv_w = lax.dynamic_slice_in_dim(g_small[9], chip * CW, CW, axis=1)
    loss = g_small[10][0, 0]

    d_in, nm_in, nv_in = _adamw(w_in[0], g_w_in, m_w_in[0], v_w_in[0], "adamw_w_in")
    d_out, nm_out, nv_out = _adamw(w_out[0], g_w_out, m_w_out[0], v_w_out[0], "adamw_w_out")
    d_cw, nm_cw, nv_cw = _adamw(conv_w[0], g_conv_w, m_conv_w[0], v_conv_w[0], "adamw_conv_w")
    small_w = [norm_w, sb_norm_w, ssd_norm_w, conv_b, q_norm_w, k_norm_w, dt_bias, A_log, D_skip]
    small_m = [m_norm_w, m_sb_norm_w, m_ssd_norm_w, m_conv_b, m_q_norm_w, m_k_norm_w, m_dt_bias, m_A_log, m_D_skip]
    small_v = [v_norm_w, v_sb_norm_w, v_ssd_norm_w, v_conv_b, v_q_norm_w, v_k_norm_w, v_dt_bias, v_A_log, v_D_skip]
    d_s, nm_s, nv_s = _adamw(_pack(small_w), _pack(g_small[:9]), _pack(small_m), _pack(small_v), "adamw_small")
    d_s, nm_s, nv_s = (_unpack(t, small_shapes) for t in (d_s, nm_s, nv_s))

    def ordered(s, w_in_, conv_w_, w_out_):
        return [s[0], w_in_[None], s[4], s[5], conv_w_[None], s[3], s[6], s[7], s[8], s[1], s[2], w_out_[None]]

    return (loss, grad_x2.reshape(Bl, L, D),
            *ordered(g_small[:9], g_w_in, g_conv_w, g_w_out),
            *ordered(d_s, d_in, d_cw, d_out),
            *ordered(nm_s, nm_in, nm_cw, nm_out),
            *ordered(nv_s, nv_in, nv_cw, nv_out))
```

```python
import functools
import math

import jax
import jax.numpy as jnp
from jax import lax
from jax.experimental import pallas as pl
from jax.experimental.pallas import tpu as pltpu

F32 = jnp.float32
BF16 = jnp.bfloat16
EPS = 1e-6
HEAD_DIM = 64
PAIR = 2 * HEAD_DIM
LANES = 128
SSD_STATE = 128
SSD_GROUPS = 2
BLK = 128
PREP_BLOCKS = 16
Q_TOGETHER_FWD = 2
Q_TOGETHER_BWD = 2
FIRST_LEFT = 2
UNDERFLOW = -105.0
CONV_K = 4
HALO = 8
N_CHIPS = 4
ADAM_LR, ADAM_B1, ADAM_B2, ADAM_EPS, ADAM_WD, ADAM_STEP = 0.001, 0.9, 0.999, 1e-08, 0.01, 10
VMEM_LIMIT_V7X = 56 * 1024 * 1024
MESH = pl.DeviceIdType.MESH
ANY = pl.BlockSpec(memory_space=pl.ANY)
NT = (((1,), (1,)), ((), ()))


def _params(sem=None):
    kw = dict(vmem_limit_bytes=VMEM_LIMIT_V7X)
    if sem is not None:
        kw["dimension_semantics"] = sem
    return pltpu.CompilerParams(**kw)


def _dot(a, b):
    return jnp.dot(a, b, preferred_element_type=F32)


def _dot_nt(a, b):
    return lax.dot_general(a, b, NT, preferred_element_type=F32)


def _dot_split(m, x):
    hi = x.astype(BF16)
    lo = (x - hi.astype(F32)).astype(BF16)
    return _dot(m, hi) + _dot(m, lo)


def _iota(shape, dim):
    return lax.broadcasted_iota(jnp.int32, shape, dim)


def _rowsum(x):
    return jnp.sum(x, axis=1, keepdims=True)


def _colsum(x):
    return jnp.sum(x, axis=0, keepdims=True)


def _sigmoid(x):
    return 0.5 * jnp.tanh(0.5 * x) + 0.5


def _softplus(x):
    return jnp.maximum(x, 0.0) + jnp.log(1.0 + jnp.exp(-jnp.abs(x)))


def _inproj(hn, w_pad, w_out_bf, conv_w):
    T, D = hn.shape
    P = w_pad.shape[1]
    tm = min(1024, T)
    tn = 1024 if P % 1024 == 0 else 512
    ni, nj = T // tm, P // tn
    n_sem = 2 * (N_CHIPS - 1)

    def body(hn_ref, w_ref, wout, cw, proj_ref, wt_ref, gout, gcw, send_sems, recv_sems):
        def gather():
            x, y, c, chips = _place()
            me = 2 * x + y
            return [pltpu.make_async_remote_copy(
                src_ref=src, dst_ref=dst.at[me], send_sem=send_sems.at[2 * j + m], recv_sem=recv_sems.at[2 * j + m],
                device_id=(px, py, c), device_id_type=MESH)
                for j, (px, py) in enumerate(chips) for m, (src, dst) in enumerate(((wout, gout), (cw, gcw)))]

        @pl.when((pl.program_id(0) == 0) & (pl.program_id(1) == 0))
        def _():
            for cp in gather():
                cp.start()

        @pl.when((pl.program_id(0) == ni - 1) & (pl.program_id(1) == nj - 1))
        def _():
            for cp in gather():
                cp.wait()

        @pl.when(pl.program_id(0) == 0)
        def _():
            wt_ref[...] = w_ref[...].astype(F32).T.astype(BF16)

        proj_ref[...] = _dot(hn_ref[...], w_ref[...])

    return pl.pallas_call(
        body,
        name="inproj",
        grid=(T // tm, P // tn),
        in_specs=[
            pl.BlockSpec((tm, D), lambda i, j: (i, 0)),
            pl.BlockSpec((D, tn), lambda i, j: (0, j)),
            ANY, ANY,
        ],
        out_specs=[
            pl.BlockSpec((tm, tn), lambda i, j: (i, j)),
            pl.BlockSpec((tn, D), lambda i, j: (jnp.where(i == 0, j, nj - 1), 0)),
            ANY, ANY,
        ],
        out_shape=[jax.ShapeDtypeStruct((T, P), F32), jax.ShapeDtypeStruct((P, D), BF16),
                   jax.ShapeDtypeStruct((N_CHIPS,) + w_out_bf.shape, BF16),
                   jax.ShapeDtypeStruct((N_CHIPS,) + conv_w.shape, F32)],
        scratch_shapes=[pltpu.SemaphoreType.DMA((n_sem,)), pltpu.SemaphoreType.DMA((n_sem,))],
        compiler_params=_params(("arbitrary", "arbitrary")),
    )(hn, w_pad, w_out_bf, conv_w)


def _pair_ones():
    ri = ((_iota((2 * PAIR, PAIR), 0) % PAIR) >= HEAD_DIM).astype(jnp.int32)
    ci = (_iota((2 * PAIR, PAIR), 1) >= HEAD_DIM).astype(jnp.int32)
    return jnp.where(ri == ci, 1.0, 0.0).astype(BF16)


def _pair_rms(v, ones2):
    return lax.rsqrt(_split_dots([v * v], ones2)[0] * (1.0 / HEAD_DIM) + EPS)


def _pair_mean(v, ones2):
    return _split_dots([v], ones2)[0] * (1.0 / HEAD_DIM)


def _suffix_ones():
    ri = _iota((2 * BLK, 2 * BLK), 0) % BLK
    ci = _iota((2 * BLK, 2 * BLK), 1)
    return jnp.where((ci >= BLK) | (ri > ci), 1.0, 0.0).astype(BF16)


def _split_dots(xs, m2):
    his = [x.astype(BF16) for x in xs]
    los = [(x - hi.astype(F32)).astype(BF16) for x, hi in zip(xs, his)]
    return [_dot(jnp.concatenate([hi, lo], axis=1), m2) for hi, lo in zip(his, los)]


def _sb_tiles(streams, km_s, uo):
    tiles = [(s, u, h) for s, st in enumerate(streams) for u in range(len(st["kbs"])) for h in range(2)]
    z2s = {(s, u): _dot_nt(st["q"], km_s[kb]) for s, st in enumerate(streams) for u, kb in enumerate(st["kbs"])}
    zs = [z2s[s, u][:, h * BLK:(h + 1) * BLK] for s, u, h in tiles]
    es = [jnp.exp(-jnp.abs(z)) for z in zs]
    las = [jnp.minimum(z, 0.0) - jnp.log(1.0 + e) for z, e in zip(zs, es)]
    lns = [a - z for a, z in zip(las, zs)]
    masks = [streams[s]["masks"][u] for s, u, h in tiles]
    lks = [lk if m is None else jnp.where(m, lk, 0.0) for m, lk in zip(masks, lns)]
    css = _split_dots(lks, uo)
    rests = [list(st["rest"]) for st in streams]
    ws = []
    for (s, u, h), m, a, cs in zip(tiles, masks, las, css):
        w = jnp.exp(a + rests[s][h] + cs[:, :BLK])
        ws.append(w if m is None else jnp.where(m, w, 0.0))
        rests[s][h] = rests[s][h] + cs[:, BLK:]
    return tiles, las, lns, ws, rests


def _stream(q_pair, qi, n_left, diag, zero):
    return dict(q=q_pair, kbs=[qi - u for u in range(n_left + 1)], masks=[diag] + [None] * n_left, rest=[zero, zero])


def _row0(block):
    return block * BLK if isinstance(block, int) else pl.multiple_of(block * BLK, BLK)


def _pair_of(vals, tiles, s, u):
    return [v for v, t in zip(vals, tiles) if t[0] == s and t[1] == u]


def _block_groups(nq, together):
    n_tog = math.gcd(together, nq)
    assert n_tog >= FIRST_LEFT
    return n_tog, list(range(n_tog)), nq // n_tog


def _attn_prep(src_ref, w_ref, dst_s, n_blocks, scale):
    per = math.gcd(PREP_BLOCKS, n_blocks)
    rows = per * BLK
    lo = _iota((rows, PAIR), 1) < HEAD_DIM
    ones2 = _pair_ones()

    def step(i, carry):
        r0 = pl.multiple_of(i * rows, rows)
        v = src_ref[0, pl.ds(r0, rows), :]
        if w_ref is not None:
            v = v * _pair_rms(v, ones2) * w_ref[...]
        if scale != 1.0:
            v = v * scale
        v0, v1 = jnp.where(lo, v, 0.0).astype(BF16), jnp.where(lo, 0.0, v).astype(BF16)
        for b in range(per):
            dst_s[i * per + b, 0:BLK, :] = v0[b * BLK:(b + 1) * BLK]
            dst_s[i * per + b, BLK:2 * BLK, :] = v1[b * BLK:(b + 1) * BLK]
        return carry

    lax.fori_loop(0, n_blocks // per, step, 0)


def _attn_fwd(proj3, qw2, kw2, D):
    Bl, L, _ = proj3.shape
    n_pair = D // PAIR
    nq = L // BLK
    scale = 1.0 / math.sqrt(HEAD_DIM)

    def body(q_ref, k_ref, v_ref, qw_ref, kw_ref, o_ref, qm_s, km_s, vm_s):
        uo = _suffix_ones()
        diag = _iota((BLK, BLK), 1) < _iota((BLK, BLK), 0)
        _attn_prep(q_ref, qw_ref, qm_s, nq, scale)
        _attn_prep(k_ref, kw_ref, km_s, nq, 1.0)
        _attn_prep(v_ref, None, vm_s, nq, 1.0)

        zero_c = jnp.zeros((BLK, BLK), F32)

        def q_of(qi):
            return qm_s[qi, 0:BLK, :] + qm_s[qi, BLK:2 * BLK, :]

        def values(streams, accs):
            tiles, _, _, ws, rests = _sb_tiles(streams, km_s, uo)
            wbs = [w.astype(BF16) for w in ws]
            accs = list(accs)
            for s, st in enumerate(streams):
                for u, kb in enumerate(st["kbs"]):
                    accs[s] = accs[s] + _dot(jnp.concatenate(_pair_of(wbs, tiles, s, u), axis=1), vm_s[kb])
            return accs, rests

        def group(qis, n_lefts):
            streams = [_stream(q_of(qi), qi, n, diag, zero_c) for qi, n in zip(qis, n_lefts)]
            accs, rests = values(streams, [jnp.zeros((BLK, PAIR), F32)] * len(qis))
            for qi, n, q, acc, rc in zip(qis, n_lefts, [st["q"] for st in streams], accs, rests):

                def sweep(state, n_blocks, q=q):
                    kb, rc0, rc1, acc1, _ = state
                    st = dict(q=q, kbs=[kb - u for u in range(n_blocks)], masks=[None] * n_blocks, rest=[rc0, rc1])
                    (acc1,), (r,) = values([st], [acc1])
                    return kb - n_blocks, r[0], r[1], acc1, jnp.maximum(jnp.max(r[0]), jnp.max(r[1]))

                state = (jnp.asarray(qi - n - 1, jnp.int32), rc[0], rc[1], acc, jnp.maximum(jnp.max(rc[0]), jnp.max(rc[1])))
                state = lax.while_loop(lambda t: (t[0] >= 1) & (t[4] >= UNDERFLOW), lambda t: sweep(t, 2), state)
                state = lax.while_loop(lambda t: (t[0] >= 0) & (t[4] >= UNDERFLOW), lambda t: sweep(t, 1), state)
                o_ref[0, pl.ds(_row0(qi), BLK), :] = state[3]

        n_tog, head, n_groups = _block_groups(nq, Q_TOGETHER_FWD)
        group(head, [min(qi, FIRST_LEFT) for qi in head])

        def groups(g, carry):
            group([g * n_tog + j for j in range(n_tog)], [FIRST_LEFT] * n_tog)
            return carry

        lax.fori_loop(1, n_groups, groups, 0)

    blk = lambda off: pl.BlockSpec((1, L, PAIR), lambda b, p: (b, 0, off + p))
    wspec = pl.BlockSpec((1, PAIR), lambda b, p: (0, 0))
    return pl.pallas_call(
        body,
        name="sb_attn_fwd",
        grid=(Bl, n_pair),
        in_specs=[blk(0), blk(n_pair), blk(2 * n_pair), wspec, wspec],
        out_specs=pl.BlockSpec((1, L, PAIR), lambda b, p: (b, 0, p)),
        out_shape=jax.ShapeDtypeStruct((Bl, L, D), F32),
        scratch_shapes=[pltpu.VMEM((nq, 2 * BLK, PAIR), BF16)] * 3,
        compiler_params=_params(("parallel", "parallel")),
    )(proj3, proj3, proj3, qw2, kw2)


def _attn_bwd(proj3, o3, do3, qw2, kw2, D):
    Bl, L, _ = proj3.shape
    n_pair = D // PAIR
    nq = L // BLK
    scale = 1.0 / math.sqrt(HEAD_DIM)

    def body(q_ref, k_ref, v_ref, o_ref, do_ref, qw_ref, kw_ref, dq_ref, dk_ref, dv_ref, dw_ref,
             qm_s, km_s, vm_s, dom_s, dq_s, dk_s, dv_s):
        uo = _suffix_ones()
        diag = _iota((BLK, BLK), 1) < _iota((BLK, BLK), 0)
        ones2 = _pair_ones()
        _attn_prep(q_ref, qw_ref, qm_s, nq, scale)
        _attn_prep(k_ref, kw_ref, km_s, nq, 1.0)
        _attn_prep(v_ref, None, vm_s, nq, 1.0)
        _attn_prep(do_ref, None, dom_s, nq, 1.0)

        @pl.when((pl.program_id(0) == 0) & (pl.program_id(1) == 0))
        def _():
            dw_ref[...] = jnp.zeros_like(dw_ref)

        def zero(i, carry):
            r0 = pl.multiple_of(i * BLK, BLK)
            dk_s[pl.ds(r0, BLK), :] = jnp.zeros((BLK, PAIR), F32)
            dv_s[pl.ds(r0, BLK), :] = jnp.zeros((BLK, PAIR), F32)
            return carry

        lax.fori_loop(0, nq, zero, 0)

        zero_c = jnp.zeros((BLK, BLK), F32)

        def tiles_bwd(streams, dqas):
            tiles, las, lns, ws, rests = _sb_tiles(streams, km_s, uo)
            dw2s = {(s, u): _dot_nt(st["do"], vm_s[kb]) for s, st in enumerate(streams) for u, kb in enumerate(st["kbs"])}
            dws = [dw2s[s, u][:, h * BLK:(h + 1) * BLK] for s, u, h in tiles]
            wfs = [w.astype(BF16).astype(F32) for w in ws]
            gs = [wf * dw for wf, dw in zip(wfs, dws)]
            gss = _split_dots(gs, uo)
            gcs = [list(st["g_rest"]) for st in streams]
            dzs = []
            for (s, u, h), a, ln, g, gsum in zip(tiles, las, lns, gs, gss):
                g_before = streams[s]["delta"][h] - (gcs[s][h] + gsum[:, :BLK] + g)
                gcs[s][h] = gcs[s][h] + gsum[:, BLK:]
                dz = g * jnp.exp(ln) - g_before * jnp.exp(a)
                m = streams[s]["masks"][u]
                dzs.append(dz if m is None else jnp.where(m, dz, 0.0))
            wts = [wf.T.astype(BF16) for wf in wfs]
            dzts = [dz.T.astype(BF16) for dz in dzs]
            dzbs = [dz.astype(BF16) for dz in dzs]
            dqas = list(dqas)
            for s, st in enumerate(streams):
                for u, kb in enumerate(st["kbs"]):
                    c0 = _row0(kb)
                    dv_s[pl.ds(c0, BLK), :] += _dot(jnp.concatenate(_pair_of(wts, tiles, s, u), axis=1), dom_s[st["qi"]])
                    dk_s[pl.ds(c0, BLK), :] += _dot(jnp.concatenate(_pair_of(dzts, tiles, s, u), axis=1), qm_s[st["qi"]])
                    dqas[s] = dqas[s] + _dot(jnp.concatenate(_pair_of(dzbs, tiles, s, u), axis=1), km_s[kb])
            return dqas, rests, gcs

        def group(qis, n_lefts):
            streams = []
            for qi, n in zip(qis, n_lefts):
                o_blk = o_ref[0, pl.ds(_row0(qi), BLK), :]
                doms = [dom_s[qi, 0:BLK, :], dom_s[qi, BLK:2 * BLK, :]]
                st = _stream(qm_s[qi, 0:BLK, :] + qm_s[qi, BLK:2 * BLK, :], qi, n, diag, zero_c)
                st.update(qi=qi, do=doms[0] + doms[1], delta=[_rowsum(d.astype(F32) * o_blk) for d in doms],
                          g_rest=[zero_c, zero_c])
                streams.append(st)
            dqas, rests, gcs = tiles_bwd(streams, [jnp.zeros((BLK, PAIR), F32)] * len(qis))
            for qi, n, st0, dqa, rc, gc in zip(qis, n_lefts, streams, dqas, rests, gcs):

                def sweep(state, n_blocks, st0=st0):
                    kb, rc0, rc1, gc0, gc1, dqa1, _ = state
                    st = dict(st0, kbs=[kb - u for u in range(n_blocks)], masks=[None] * n_blocks, rest=[rc0, rc1],
                              g_rest=[gc0, gc1])
                    (dqa1,), (r,), (g,) = tiles_bwd([st], [dqa1])
                    return kb - n_blocks, r[0], r[1], g[0], g[1], dqa1, jnp.maximum(jnp.max(r[0]), jnp.max(r[1]))

                state = (jnp.asarray(qi - n - 1, jnp.int32), rc[0], rc[1], gc[0], gc[1], dqa,
                         jnp.maximum(jnp.max(rc[0]), jnp.max(rc[1])))
                state = lax.while_loop(lambda t: (t[0] >= 1) & (t[6] >= UNDERFLOW), lambda t: sweep(t, 2), state)
                state = lax.while_loop(lambda t: (t[0] >= 0) & (t[6] >= UNDERFLOW), lambda t: sweep(t, 1), state)
                dq_s[pl.ds(_row0(qi), BLK), :] = state[5] * scale

        n_tog, head, n_groups = _block_groups(nq, Q_TOGETHER_BWD)
        group(head, [min(qi, FIRST_LEFT) for qi in head])

        def groups(g, carry):
            group([g * n_tog + j for j in range(n_tog)], [FIRST_LEFT] * n_tog)
            return carry

        lax.fori_loop(1, n_groups, groups, 0)

        per = math.gcd(PREP_BLOCKS, nq)
        rows = per * BLK

        def finish(i, carry):
            r0 = pl.multiple_of(i * rows, rows)
            dwq, dwk = carry
            out = []
            for src_ref, w_ref, d_s in ((q_ref, qw_ref, dq_s), (k_ref, kw_ref, dk_s)):
                v = src_ref[0, pl.ds(r0, rows), :]
                r = _pair_rms(v, ones2)
                vh = v * r
                dy = d_s[pl.ds(r0, rows), :]
                dvh = dy * w_ref[...]
                out.append((r * (dvh - vh * _pair_mean(dvh * vh, ones2)), _colsum(dy * vh)))
            dq_ref[0, pl.ds(r0, rows), :] = out[0][0].astype(BF16)
            dk_ref[0, pl.ds(r0, rows), :] = out[1][0].astype(BF16)
            dv_ref[0, pl.ds(r0, rows), :] = dv_s[pl.ds(r0, rows), :].astype(BF16)
            return dwq + out[0][1], dwk + out[1][1]

        zrow = jnp.zeros((1, PAIR), F32)
        dwq, dwk = lax.fori_loop(0, nq // per, finish, (zrow, zrow))
        dw_ref[0:1, :] += dwq
        dw_ref[1:2, :] += dwk

    blk = lambda off: pl.BlockSpec((1, L, PAIR), lambda b, p: (b, 0, off + p))
    wspec = pl.BlockSpec((1, PAIR), lambda b, p: (0, 0))
    oblk = pl.BlockSpec((1, L, PAIR), lambda b, p: (b, 0, p))
    return pl.pallas_call(
        body,
        name="sb_attn_bwd",
        grid=(Bl, n_pair),
        in_specs=[blk(0), blk(n_pair), blk(2 * n_pair), oblk, oblk, wspec, wspec],
        out_specs=[oblk, oblk, oblk, pl.BlockSpec((8, PAIR), lambda b, p: (0, 0))],
        out_shape=[jax.ShapeDtypeStruct((Bl, L, D), BF16)] * 3 + [jax.ShapeDtypeStruct((8, PAIR), F32)],
        scratch_shapes=[pltpu.VMEM((nq, 2 * BLK, PAIR), BF16)] * 4 + [pltpu.VMEM((L, PAIR), F32)] * 3,
        compiler_params=_params(("arbitrary", "arbitrary")),
    )(proj3, proj3, proj3, o3, do3, qw2, kw2)


def _conv_pre(ext_s, halo_ref, raw_ref, w_ref, b_ref, first):
    ext_s[0:HALO, :] = jnp.where(first, 0.0, halo_ref[0])
    ext_s[HALO:HALO + BLK, :] = raw_ref[0]
    pre = b_ref[...]
    for i in range(CONV_K):
        pre = pre + ext_s[pl.ds(HALO - (CONV_K - 1 - i), BLK), :] * w_ref[i:i + 1, :]
    return pre


def _lane_col(m, lane, h):
    return _rowsum(jnp.where(lane == h, m, 0.0))


def _half_sums(row, lo1):
    return _rowsum(jnp.where(lo1, row, 0.0)), _rowsum(jnp.where(lo1, 0.0, row))


def _ssd_specs(Bl, L, D, rev):
    nc = L // BLK
    rows_per = BLK // HALO
    cidx = (lambda c: nc - 1 - c) if rev else (lambda c: c)
    xoff = 5
    boff = (6 * D) // 512
    doff = (6 * D + 512) // LANES
    prev = lambda c: jnp.maximum(cidx(c) * rows_per - 1, 0)
    specs = [
        pl.BlockSpec((1, BLK, D), lambda b, c: (b, cidx(c), xoff)),
        pl.BlockSpec((1, BLK, 512), lambda b, c: (b, cidx(c), boff)),
        pl.BlockSpec((1, HALO, D), lambda b, c: (b, prev(c), xoff)),
        pl.BlockSpec((1, HALO, 512), lambda b, c: (b, prev(c), boff)),
        pl.BlockSpec((1, BLK, LANES), lambda b, c: (b, cidx(c), doff)),
    ]
    full = lambda shape: pl.BlockSpec(shape, lambda b, c: (0,) * len(shape))
    specs += [full((CONV_K, D)), full((CONV_K, 512)), full((1, D)), full((1, 512)),
              full((1, LANES)), full((1, LANES)), full((1, LANES))]
    return specs, cidx


def _ssd_common(dtr_ref, dtb_ref, alog_ref, acs_s, acsT_s):
    ltri = jnp.where(_iota((BLK, BLK), 1) <= _iota((BLK, BLK), 0), 1.0, 0.0).astype(BF16)
    dtv = _softplus(dtr_ref[0] + dtb_ref[...])
    a = -jnp.exp(alog_ref[...])
    acs = _dot_split(ltri, dtv * a)
    acs_s[...] = acs
    acsT_s[...] = acs.T
    return dtv, a, acs


def _pair_terms(pr, acs, dtv, acs_s, lane, lo, lane1, lo1):
    h0, h1 = 2 * pr, 2 * pr + 1
    c0, c1 = _lane_col(acs, lane, h0), _lane_col(acs, lane, h1)
    d0, d1 = _lane_col(dtv, lane, h0), _lane_col(dtv, lane, h1)
    lastv = acs_s[BLK - 1:BLK, :]
    l0, l1 = _lane_col(lastv, lane1, h0), _lane_col(lastv, lane1, h1)
    return dict(h=(h0, h1), c=(c0, c1), last=(l0, l1), acs_p=jnp.where(lo, c0, c1), dt_p=jnp.where(lo, d0, d1),
                last_p=jnp.where(lo1, l0, l1))


def _decay_tiles(cc, row, tri, want_t):
    lm = jnp.where(tri, jnp.exp(cc - row), 0.0)
    return lm, (lm.T if want_t else None)


def _ssd_fwd(proj3, cwx, cwb, cbx, cbb, dtb, alog, dsk, D):
    Bl, L, _ = proj3.shape
    nc = L // BLK
    n_pair = D // PAIR
    pairs_per_group = n_pair // SSD_GROUPS
    specs, _ = _ssd_specs(Bl, L, D, False)

    def body(xr_ref, bcr_ref, xh_ref, bch_ref, dtr_ref, cwx_ref, cwb_ref, cbx_ref, cbb_ref, dtb_ref, alog_ref,
             dsk_ref, y_ref, sin_ref, st_s, extx_s, extb_s, acs_s, acsT_s):
        first = pl.program_id(1) == 0

        @pl.when(first)
        def _():
            st_s[...] = jnp.zeros_like(st_s)

        lane, lane1 = _iota((BLK, LANES), 1), _iota((1, LANES), 1)
        lo, lo1 = lane < HEAD_DIM, lane1 < HEAD_DIM
        tri = _iota((BLK, BLK), 1) <= _iota((BLK, BLK), 0)
        pre = _conv_pre(extx_s, xh_ref, xr_ref, cwx_ref, cbx_ref, first)
        ux = pre * _sigmoid(pre)
        pre = _conv_pre(extb_s, bch_ref, bcr_ref, cwb_ref, cbb_ref, first)
        ub = pre * _sigmoid(pre)
        dtv, a, acs = _ssd_common(dtr_ref, dtb_ref, alog_ref, acs_s, acsT_s)
        for g in range(SSD_GROUPS):
            bg = ub[:, g * SSD_STATE:(g + 1) * SSD_STATE]
            cb_ = ub[:, (SSD_GROUPS + g) * SSD_STATE:(SSD_GROUPS + g + 1) * SSD_STATE].astype(BF16)
            cbm = _dot_nt(cb_, bg.astype(BF16))
            btb = bg.T.astype(BF16)
            for pr in range(g * pairs_per_group, (g + 1) * pairs_per_group):
                t = _pair_terms(pr, acs, dtv, acs_s, lane, lo, lane1, lo1)
                xs_p = ux[:, pr * PAIR:(pr + 1) * PAIR]
                x_p = xs_p * t["dt_p"]
                st = st_s[pr]
                sin_ref[0, 0, pr] = st
                y = _dot(cb_, st.astype(BF16)) * jnp.exp(t["acs_p"])
                for k in range(2):
                    row = acsT_s[t["h"][k]:t["h"][k] + 1, :]
                    lm, _ = _decay_tiles(t["c"][k], row, tri, False)
                    xm = jnp.where(lo if k == 0 else ~lo, x_p, 0.0).astype(BF16)
                    y = y + _dot((cbm * lm).astype(BF16), xm)
                d_p = jnp.where(lo1, _lane_col(dsk_ref[...], lane1, t["h"][0]), _lane_col(dsk_ref[...], lane1, t["h"][1]))
                y_ref[0, :, pr * PAIR:(pr + 1) * PAIR] = y + d_p * xs_p
                xd = (x_p * jnp.exp(t["last_p"] - t["acs_p"])).astype(BF16)
                st_s[pr] = st * jnp.exp(t["last_p"]) + _dot(btb, xd)

    return pl.pallas_call(
        body,
        name="ssd_fwd",
        grid=(Bl, nc),
        in_specs=specs,
        out_specs=[
            pl.BlockSpec((1, BLK, D), lambda b, c: (b, c, 0)),
            pl.BlockSpec((1, 1, n_pair, SSD_STATE, PAIR), lambda b, c: (b, c, 0, 0, 0)),
        ],
        out_shape=[jax.ShapeDtypeStruct((Bl, L, D), F32),
                   jax.ShapeDtypeStruct((Bl, nc, n_pair, SSD_STATE, PAIR), F32)],
        scratch_shapes=[pltpu.VMEM((n_pair, SSD_STATE, PAIR), F32), pltpu.VMEM((HALO + BLK, D), F32),
                        pltpu.VMEM((HALO + BLK, 512), F32), pltpu.VMEM((BLK, LANES), F32),
                        pltpu.VMEM((LANES, BLK), F32)],
        compiler_params=_params(("arbitrary", "arbitrary")),
    )(proj3, proj3, proj3, proj3, proj3, cwx, cwb, cbx, cbb, dtb, alog, dsk)


def _ssd_bwd(proj3, s_in, dy3, cwx, cwb, cbx, cbb, dtb, alog, dsk, D, tail):
    Bl, L, _ = proj3.shape
    CD = D + 512
    nc = L // BLK
    n_pair = D // PAIR
    n_heads = 2 * n_pair
    pairs_per_group = n_pair // SSD_GROUPS
    specs, cidx = _ssd_specs(Bl, L, D, True)
    specs = specs + [
        pl.BlockSpec((1, 1, n_pair, SSD_STATE, PAIR), lambda b, c: (b, cidx(c), 0, 0, 0)),
        pl.BlockSpec((1, BLK, D), lambda b, c: (b, cidx(c), 0)),
    ]

    def body(xr_ref, bcr_ref, xh_ref, bch_ref, dtr_ref, cwx_ref, cwb_ref, cbx_ref, cbb_ref, dtb_ref, alog_ref,
             dsk_ref, sin_ref, dy_ref, dxbc_ref, dcwx_ref, dcwb_ref, dcbx_ref, dcbb_ref, misc_ref,
             dst_s, extx_s, extb_s, acs_s, acsT_s, dux_s, dub_s, e2x_s, e2b_s, nxx_s, nxb_s):
        step = pl.program_id(1)
        first = step == nc - 1
        last = step == 0

        @pl.when(last)
        def _():
            dst_s[...] = jnp.zeros_like(dst_s)
            nxx_s[...] = jnp.zeros_like(nxx_s)
            nxb_s[...] = jnp.zeros_like(nxb_s)

        @pl.when(last & (pl.program_id(0) == 0))
        def _():
            for r in (dcwx_ref, dcwb_ref, dcbx_ref, dcbb_ref, misc_ref):
                r[...] = jnp.zeros_like(r)

        lane, lane1 = _iota((BLK, LANES), 1), _iota((1, LANES), 1)
        lo, lo1 = lane < HEAD_DIM, lane1 < HEAD_DIM
        tri = _iota((BLK, BLK), 1) <= _iota((BLK, BLK), 0)
        prex = _conv_pre(extx_s, xh_ref, xr_ref, cwx_ref, cbx_ref, first)
        sgx = _sigmoid(prex)
        ux = prex * sgx
        preb = _conv_pre(extb_s, bch_ref, bcr_ref, cwb_ref, cbb_ref, first)
        sgb = _sigmoid(preb)
        ub = preb * sgb
        dtv, a, acs = _ssd_common(dtr_ref, dtb_ref, alog_ref, acs_s, acsT_s)
        dacs = jnp.zeros((BLK, LANES), F32)
        dlast = jnp.zeros((1, LANES), F32)
        ddt = jnp.zeros((BLK, LANES), F32)
        dd = jnp.zeros((1, LANES), F32)
        for g in range(SSD_GROUPS):
            bg = ub[:, g * SSD_STATE:(g + 1) * SSD_STATE]
            cg = ub[:, (SSD_GROUPS + g) * SSD_STATE:(SSD_GROUPS + g + 1) * SSD_STATE]
            bb, cb_ = bg.astype(BF16), cg.astype(BF16)
            cbm = _dot_nt(cb_, bb)
            cbt = _dot_nt(bb, cb_)
            ctb = cg.T.astype(BF16)
            dbg = jnp.zeros((BLK, SSD_STATE), F32)
            dcg = jnp.zeros((BLK, SSD_STATE), F32)
            for pr in range(g * pairs_per_group, (g + 1) * pairs_per_group):
                t = _pair_terms(pr, acs, dtv, acs_s, lane, lo, lane1, lo1)
                h0, h1 = t["h"]
                xs_p = ux[:, pr * PAIR:(pr + 1) * PAIR]
                dy_p = dy_ref[0, :, pr * PAIR:(pr + 1) * PAIR]
                x_p = xs_p * t["dt_p"]
                ea_p = jnp.exp(t["acs_p"])
                dte_p = jnp.exp(t["last_p"] - t["acs_p"])
                cd_p = jnp.exp(t["last_p"])
                st = sin_ref[0, 0, pr]
                dst = dst_s[pr]
                stb, dstb = st.astype(BF16), dst.astype(BF16)
                s0, s1 = _half_sums(_colsum(dy_p * xs_p), lo1)
                dd = dd + jnp.where(lane1 == h0, s0, 0.0) + jnp.where(lane1 == h1, s1, 0.0)
                d_p = jnp.where(lo1, _lane_col(dsk_ref[...], lane1, h0), _lane_col(dsk_ref[...], lane1, h1))
                dxs_p = d_p * dy_p
                dp = dy_p * ea_p
                dpb = dp.astype(BF16)
                yo = dp * _dot(cb_, stb)
                dcg = dcg + _dot_nt(dpb, stb)
                dst_off = _dot(ctb, dpb)
                dac = [_rowsum(jnp.where(lo, yo, 0.0)), _rowsum(jnp.where(lo, 0.0, yo))]
                s0, s1 = _half_sums(_colsum(dst * st), lo1)
                dl = [s0 * jnp.exp(t["last"][0]), s1 * jnp.exp(t["last"][1])]
                dxd = _dot(bb, dstb)
                dx_p = dxd * dte_p
                tt = dxd * x_p
                dbg = dbg + _dot_nt((x_p * dte_p).astype(BF16), dstb)
                for k, ddte in enumerate((_rowsum(jnp.where(lo, tt, 0.0)), _rowsum(jnp.where(lo, 0.0, tt)))):
                    ek = ddte * jnp.exp(t["last"][k] - t["c"][k])
                    dl[k] = dl[k] + _colsum(ek)
                    dac[k] = dac[k] - ek
                x_pb = x_p.astype(BF16)
                for k in range(2):
                    row = acsT_s[t["h"][k]:t["h"][k] + 1, :]
                    lm, lmt = _decay_tiles(t["c"][k], row, tri, True)
                    dym = jnp.where(lo if k == 0 else ~lo, dy_p, 0.0).astype(BF16)
                    dm = _dot_nt(dym, x_pb)
                    dmt = _dot_nt(x_pb, dym)
                    mt = cbt * lmt
                    dx_p = dx_p + _dot(mt.astype(BF16), dym)
                    dac[k] = dac[k] + _rowsum(dm * (cbm * lm)) - _rowsum(dmt * mt)
                    dcg = dcg + _dot((dm * lm).astype(BF16), bb)
                    dbg = dbg + _dot((dmt * lmt).astype(BF16), cb_)
                dacs = dacs + jnp.where(lane == h0, dac[0], 0.0) + jnp.where(lane == h1, dac[1], 0.0)
                dlast = dlast + jnp.where(lane1 == h0, dl[0], 0.0) + jnp.where(lane1 == h1, dl[1], 0.0)
                dxs_p = dxs_p + dx_p * t["dt_p"]
                t3 = dx_p * xs_p
                ddt = ddt + jnp.where(lane == h0, _rowsum(jnp.where(lo, t3, 0.0)), 0.0) \
                    + jnp.where(lane == h1, _rowsum(jnp.where(lo, 0.0, t3)), 0.0)
                dux_s[:, pr * PAIR:(pr + 1) * PAIR] = dxs_p
                dst_s[pr] = dst * cd_p + dst_off
            dub_s[:, g * SSD_STATE:(g + 1) * SSD_STATE] = dbg
            dub_s[:, (SSD_GROUPS + g) * SSD_STATE:(SSD_GROUPS + g + 1) * SSD_STATE] = dcg
        dacs = dacs + jnp.where(_iota((BLK, LANES), 0) == BLK - 1, dlast, 0.0)
        utri = jnp.where(_iota((BLK, BLK), 1) >= _iota((BLK, BLK), 0), 1.0, 0.0).astype(BF16)
        dda = _dot_split(utri, dacs)
        ddt = ddt + dda * a
        ddtr = jnp.where(lane < n_heads, ddt * _sigmoid(dtr_ref[0] + dtb_ref[...]), 0.0)
        dxbc_ref[0, :, CD:CD + LANES] = ddtr.astype(BF16)
        dxbc_ref[0, :, CD + LANES:tail] = jnp.zeros((BLK, tail - CD - LANES), BF16)
        misc_ref[0:1, :] += _colsum(ddtr)
        misc_ref[1:2, :] += jnp.where(lane1 < n_heads, _colsum(dda * dtv) * a, 0.0)
        misc_ref[2:3, :] += dd
        for (du_s, pre, sg, ext_s, e2_s, nx_s, w_ref, dcw_ref, dcb_ref, c0, width) in (
                (dux_s, prex, sgx, extx_s, e2x_s, nxx_s, cwx_ref, dcwx_ref, dcbx_ref, 0, D),
                (dub_s, preb, sgb, extb_s, e2b_s, nxb_s, cwb_ref, dcwb_ref, dcbb_ref, D, 512)):
            dpre = du_s[...] * (sg * (1.0 + pre * (1.0 - sg)))
            dcb_ref[...] += _colsum(dpre)
            for i in range(CONV_K):
                dcw_ref[i:i + 1, :] += _colsum(dpre * ext_s[pl.ds(HALO - (CONV_K - 1 - i), BLK), :])
            e2_s[0:BLK, :] = dpre
            e2_s[BLK:BLK + HALO, :] = nx_s[...]
            dxr = jnp.zeros((BLK, width), F32)
            for i in range(CONV_K):
                dxr = dxr + e2_s[pl.ds(CONV_K - 1 - i, BLK), :] * w_ref[i:i + 1, :]
            dxbc_ref[0, :, c0:c0 + width] = dxr.astype(BF16)
            nx_s[...] = e2_s[0:HALO, :]

    full = lambda shape: pl.BlockSpec(shape, lambda b, c: (0,) * len(shape))
    return pl.pallas_call(
        body,
        name="ssd_bwd",
        grid=(Bl, nc),
        in_specs=specs,
        out_specs=[
            pl.BlockSpec((1, BLK, tail), lambda b, c: (b, cidx(c), 0)),
            full((CONV_K, D)), full((CONV_K, 512)), full((1, D)), full((1, 512)), full((8, LANES)),
        ],
        out_shape=[
            jax.ShapeDtypeStruct((Bl, L, tail), BF16),
            jax.ShapeDtypeStruct((CONV_K, D), F32), jax.ShapeDtypeStruct((CONV_K, 512), F32),
            jax.ShapeDtypeStruct((1, D), F32), jax.ShapeDtypeStruct((1, 512), F32),
            jax.ShapeDtypeStruct((8, LANES), F32),
        ],
        scratch_shapes=[
            pltpu.VMEM((n_pair, SSD_STATE, PAIR), F32),
            pltpu.VMEM((HALO + BLK, D), F32), pltpu.VMEM((HALO + BLK, 512), F32),
            pltpu.VMEM((BLK, LANES), F32), pltpu.VMEM((LANES, BLK), F32),
            pltpu.VMEM((BLK, D), F32), pltpu.VMEM((BLK, 512), F32),
            pltpu.VMEM((BLK + HALO, D), F32), pltpu.VMEM((BLK + HALO, 512), F32),
            pltpu.VMEM((HALO, D), F32), pltpu.VMEM((HALO, 512), F32),
        ],
        compiler_params=_params(("arbitrary", "arbitrary")),
    )(proj3, proj3, proj3, proj3, proj3, cwx, cwb, cbx, cbb, dtb, alog, dsk, s_in, dy3)


def _gate_out(x2, tgt2, o2, proj2, y2, sbw, ssw, w_out_bf, w_out_t):
    T, D = x2.shape
    tm = min(256, T)

    def body(x_ref, t_ref, o_ref, zs_ref, y_ref, zy_ref, sbw_ref, ssw_ref, wo_ref, wot_ref,
             dout_ref, doutb_ref, mixt_ref, do_ref, dy_ref, dz_ref, dnw_ref, loss_ref):
        @pl.when(pl.program_id(0) == 0)
        def _():
            dnw_ref[...] = jnp.zeros_like(dnw_ref)
            loss_ref[...] = jnp.zeros_like(loss_ref)

        def fwd(o, z, w):
            sg = _sigmoid(z)
            sl = z * sg
            g = o * sl
            r = lax.rsqrt(jnp.mean(g * g, axis=-1, keepdims=True) + EPS)
            n = g * r
            return sg, sl, r, n, n * w

        def bwd(dy, o, z, w, sg, sl, r, n):
            dn = dy * w
            dg = r * (dn - n * jnp.mean(dn * n, axis=-1, keepdims=True))
            return dg * sl, dg * o * (sg * (1.0 + z * (1.0 - sg))), _colsum(dy * n)

        o1, z1, w1 = o_ref[...], zs_ref[...], sbw_ref[...]
        o2_, z2, w2 = y_ref[...], zy_ref[...], ssw_ref[...]
        sg1, sl1, r1, n1, y1 = fwd(o1, z1, w1)
        sg2, sl2, r2, n2, y2_ = fwd(o2_, z2, w2)
        y1b, y2b = y1.astype(BF16), y2_.astype(BF16)
        mixt_ref[0:D, :] = y1.T.astype(BF16)
        mixt_ref[D:2 * D, :] = y2_.T.astype(BF16)
        out = x_ref[...] + (_dot(y1b, wo_ref[0:D, :]) + _dot(y2b, wo_ref[D:2 * D, :]))
        err = out - t_ref[...]
        loss_ref[...] += jnp.sum(err * err) * (0.5 / D)
        dout = err * (1.0 / D)
        dout_ref[...] = dout
        doutb = dout.astype(BF16)
        doutb_ref[...] = doutb
        do1, dz1, dw1 = bwd(_dot(doutb, wot_ref[:, 0:D]), o1, z1, w1, sg1, sl1, r1, n1)
        do2, dz2, dw2 = bwd(_dot(doutb, wot_ref[:, D:2 * D]), o2_, z2, w2, sg2, sl2, r2, n2)
        do_ref[...] = do1
        dy_ref[...] = do2
        dz_ref[:, 0:D] = dz1.astype(BF16)
        dz_ref[:, D:2 * D] = dz2.astype(BF16)
        dnw_ref[0:1, :] += dw1
        dnw_ref[1:2, :] += dw2

    row = lambda col: pl.BlockSpec((tm, D), lambda i: (i, col))
    full = lambda shape: pl.BlockSpec(shape, lambda i: (0,) * len(shape))
    wide = pl.BlockSpec((tm, 2 * D), lambda i: (i, 0))
    return pl.pallas_call(
        body,
        name="gate_out",
        grid=(T // tm,),
        in_specs=[row(0), row(0), row(0), row(3), row(0), row(4), full((1, D)), full((1, D)), full((2 * D, D)),
                  full((D, 2 * D))],
        out_specs=[row(0), row(0), pl.BlockSpec((2 * D, tm), lambda i: (0, i)), row(0), row(0), wide,
                   full((8, D)), full((8, LANES))],
        out_shape=[
            jax.ShapeDtypeStruct((T, D), F32), jax.ShapeDtypeStruct((T, D), BF16),
            jax.ShapeDtypeStruct((2 * D, T), BF16), jax.ShapeDtypeStruct((T, D), F32),
            jax.ShapeDtypeStruct((T, D), F32), jax.ShapeDtypeStruct((T, 2 * D), BF16),
            jax.ShapeDtypeStruct((8, D), F32), jax.ShapeDtypeStruct((8, LANES), F32),
        ],
        compiler_params=_params(("arbitrary",)),
    )(x2, tgt2, o2, proj2, y2, proj2, sbw, ssw, w_out_bf, w_out_t)


def _piece_blocks(pieces, D):
    counts = [p.shape[1] // D for p in pieces]
    return [sum(counts[:i]) for i in range(len(counts))], counts


def _dhn(pieces, w_pad_t, x2, dout, norm_w, h_in, h_out, hb_in, hb_out, slab_off, slab_w):
    T, D = x2.shape
    tm = min(1024, T)
    starts, counts = _piece_blocks(pieces, D)
    nk = sum(counts)
    ni = T // tm
    n_sem = 2 * (N_CHIPS - 1)
    assert nk * D == w_pad_t.shape[0]

    def body(*refs):
        p_refs = refs[:len(pieces)]
        (w_ref, x_hbm, dout_hbm, nw_ref, hin, hout, hbin, hbout, gx_ref, dnw_ref, rin, rout, oin, oout,
         acc_s, x_s, dout_s, send_sems, recv_sems, row_sems, own_sems) = refs[len(pieces):]
        i, k = pl.program_id(0), pl.program_id(1)

        def rows():
            r0 = pl.multiple_of(i * tm, tm)
            return [pltpu.make_async_copy(src.at[pl.ds(r0, tm)], dst, row_sems.at[n])
                    for n, (src, dst) in enumerate(((x_hbm, x_s), (dout_hbm, dout_s)))]

        @pl.when(k == 0)
        def _():
            for cp in rows():
                cp.start()

        def scatter():
            x, y, c, chips = _place()

            def slab(ref, p):
                return ref.at[:, pl.ds(pl.multiple_of(p * slab_off, LANES), slab_w)]

            cps = []
            for j, (px, py) in enumerate(chips):
                p = 2 * px + py
                for m, (src, dst) in enumerate(((slab(hbin, p), rin.at[j]), (hbout.at[p], rout.at[j]))):
                    cps.append(pltpu.make_async_remote_copy(
                        src_ref=src, dst_ref=dst, send_sem=send_sems.at[2 * j + m], recv_sem=recv_sems.at[2 * j + m],
                        device_id=(px, py, c), device_id_type=MESH))
            me = 2 * x + y
            own = [pltpu.make_async_copy(slab(hin, me), oin, own_sems.at[0]),
                   pltpu.make_async_copy(hout.at[me], oout, own_sems.at[1])]
            return cps + own

        @pl.when((i == 0) & (k == 0))
        def _():
            for cp in scatter():
                cp.start()

        @pl.when((i == ni - 1) & (k == nk - 1))
        def _():
            for cp in scatter():
                cp.wait()

        @pl.when((i == 0) & (k == 0))
        def _():
            dnw_ref[...] = jnp.zeros_like(dnw_ref)

        @pl.when(k == 0)
        def _():
            acc_s[...] = jnp.zeros_like(acc_s)

        for p_ref, s, n in zip(p_refs, starts, counts):
            @pl.when((k >= s) & (k < s + n))
            def _(p_ref=p_ref):
                acc_s[...] += _dot(p_ref[...], w_ref[...])

        @pl.when(k == nk - 1)
        def _():
            for cp in rows():
                cp.wait()
            xv = x_s[...]
            r = lax.rsqrt(jnp.mean(xv * xv, axis=-1, keepdims=True) + EPS)
            xh = xv * r
            dhn = acc_s[...]
            dxh = dhn * nw_ref[...]
            gx_ref[...] = dout_s[...] + r * (dxh - xh * jnp.mean(dxh * xh, axis=-1, keepdims=True))
            dnw_ref[0:1, :] += _colsum(dhn * xh)

    return pl.pallas_call(
        body,
        name="dhn",
        grid=(T // tm, nk),
        in_specs=[pl.BlockSpec((tm, D), lambda i, k, s=s, n=n: (i, jnp.clip(k - s, 0, n - 1)))
                  for s, n in zip(starts, counts)] + [
            pl.BlockSpec((D, D), lambda i, k: (k, 0)),
            ANY, ANY,
            pl.BlockSpec((1, D), lambda i, k: (0, 0)),
            ANY, ANY, ANY, ANY,
        ],
        out_specs=[pl.BlockSpec((tm, D), lambda i, k: (i, 0)), pl.BlockSpec((8, D), lambda i, k: (0, 0)),
                   ANY, ANY, ANY, ANY],
        out_shape=[jax.ShapeDtypeStruct((T, D), F32), jax.ShapeDtypeStruct((8, D), F32),
                   jax.ShapeDtypeStruct((N_CHIPS - 1, h_in.shape[0], slab_w), BF16),
                   jax.ShapeDtypeStruct((N_CHIPS - 1,) + h_out.shape[1:], BF16),
                   jax.ShapeDtypeStruct((h_in.shape[0], slab_w), F32),
                   jax.ShapeDtypeStruct(h_out.shape[1:], F32)],
        scratch_shapes=[pltpu.VMEM((tm, D), F32)] * 3 + [pltpu.SemaphoreType.DMA((n_sem,)), pltpu.SemaphoreType.DMA((n_sem,)),
                                                      pltpu.SemaphoreType.DMA((2,)), pltpu.SemaphoreType.DMA((2,))],
        compiler_params=_params(("arbitrary", "arbitrary")),
    )(*pieces, w_pad_t, x2, dout, norm_w, h_in, h_out, hb_in, hb_out)


def _grad_w_in(hn_t, pieces):
    D, T = hn_t.shape
    tk = min(1024, T)
    starts, counts = _piece_blocks(pieces, D)

    def body(*refs):
        a_ref, p_refs, o_ref = refs[0], refs[1:-1], refs[-1]
        j = pl.program_id(0)

        @pl.when(pl.program_id(1) == 0)
        def _():
            o_ref[...] = jnp.zeros_like(o_ref)

        for p_ref, s, n in zip(p_refs, starts, counts):
            @pl.when((j >= s) & (j < s + n))
            def _(p_ref=p_ref):
                o_ref[...] += _dot(a_ref[...], p_ref[...])

    def piece_spec(s, n):
        return pl.BlockSpec((tk, D), lambda j, k: (jnp.where((j >= s) & (j < s + n), k, 0), jnp.clip(j - s, 0, n - 1)))

    return pl.pallas_call(
        body,
        name="grad_w_in",
        grid=(sum(counts), T // tk),
        in_specs=[pl.BlockSpec((D, tk), lambda j, k: (0, k))] + [piece_spec(s, n) for s, n in zip(starts, counts)],
        out_specs=pl.BlockSpec((D, D), lambda j, k: (0, j)),
        out_shape=jax.ShapeDtypeStruct((D, sum(counts) * D), F32),
        compiler_params=_params(("parallel", "arbitrary")),
    )(hn_t, *pieces)


def _grad_w_out(a, b, g_in, width):
    M, K = a.shape
    N = b.shape[1]
    tm = min(1024, M)
    tn = 1024 if N % 1024 == 0 else (512 if N % 512 == 0 else N)
    tk = min(512, K)
    grid = (M // tm, N // tn, K // tk)
    h = g_in.shape[0] // 2

    def body(a_ref, b_ref, gin, o_ref, rin, send_sem, recv_sem):
        ids = [pl.program_id(d) for d in range(3)]

        def swap():
            x, y, c, _ = _place()
            return pltpu.make_async_remote_copy(
                src_ref=gin.at[pl.ds((1 - c) * h, h), pl.ds(0, width)], dst_ref=rin, send_sem=send_sem, recv_sem=recv_sem,
                device_id=(x, y, 1 - c), device_id_type=MESH)

        @pl.when((ids[0] == 0) & (ids[1] == 0) & (ids[2] == 0))
        def _():
            swap().start()

        @pl.when(ids[2] == 0)
        def _():
            o_ref[...] = jnp.zeros_like(o_ref)

        o_ref[...] += _dot(a_ref[...], b_ref[...])

        @pl.when((ids[0] == grid[0] - 1) & (ids[1] == grid[1] - 1) & (ids[2] == grid[2] - 1))
        def _():
            swap().wait()

    return pl.pallas_call(
        body,
        name="grad_w_out",
        grid=grid,
        in_specs=[pl.BlockSpec((tm, tk), lambda i, j, k: (i, k)), pl.BlockSpec((tk, tn), lambda i, j, k: (k, j)), ANY],
        out_specs=[pl.BlockSpec((tm, tn), lambda i, j, k: (i, j)), ANY],
        out_shape=[jax.ShapeDtypeStruct((M, N), F32), jax.ShapeDtypeStruct((h, width), F32)],
        scratch_shapes=[pltpu.SemaphoreType.DMA, pltpu.SemaphoreType.DMA],
        compiler_params=_params(("arbitrary", "arbitrary", "arbitrary")),
    )(a, b, g_in)


def _adamw(w, g, m, v, name):
    R, C = w.shape
    tr = 256 if R % 256 == 0 else R
    c1 = 1.0 - ADAM_B1 ** ADAM_STEP
    c2 = 1.0 - ADAM_B2 ** ADAM_STEP

    def body(w_ref, g_ref, m_ref, v_ref, d_ref, nm_ref, nv_ref):
        gv = g_ref[...]
        m_new = ADAM_B1 * m_ref[...] + (1.0 - ADAM_B1) * gv
        v_new = ADAM_B2 * v_ref[...] + (1.0 - ADAM_B2) * (gv * gv)
        d_ref[...] = -ADAM_LR * ((m_new / c1) / (jnp.sqrt(v_new / c2) + ADAM_EPS) + ADAM_WD * w_ref[...])
        nm_ref[...] = m_new
        nv_ref[...] = v_new

    spec = pl.BlockSpec((tr, C), lambda i: (i, 0))
    return pl.pallas_call(
        body,
        name=name,
        grid=(R // tr,),
        in_specs=[spec] * 4,
        out_specs=[spec] * 3,
        out_shape=[jax.ShapeDtypeStruct((R, C), F32)] * 3,
        compiler_params=_params(("parallel",)),
    )(w, g, m, v)


def _add_core_rows(g, recv, core, name):
    h, width = recv.shape
    th = 128 if h % 128 == 0 else h

    def body(c_ref, g_ref, r_ref, o_ref, ob_ref):
        o_ref[...] = g_ref[...] + r_ref[...]
        ob_ref[...] = o_ref[...].astype(BF16)

    return pl.pallas_call(
        body,
        name=name,
        grid_spec=pltpu.PrefetchScalarGridSpec(
            num_scalar_prefetch=1,
            grid=(h // th,),
            in_specs=[
                pl.BlockSpec((th, width), lambda i, c: (c[0] * (h // th) + i, 0)),
                pl.BlockSpec((th, width), lambda i, c: (i, 0)),
            ],
            out_specs=[pl.BlockSpec((th, width), lambda i, c: (i, 0))] * 2,
        ),
        out_shape=[jax.ShapeDtypeStruct((h, width), F32), jax.ShapeDtypeStruct((h, width), BF16)],
        compiler_params=_params(("parallel",)),
    )(core, g, recv)


def _add_core_blocks(g, recv, core, name):
    n, hb, C = recv.shape

    def body(c_ref, g_ref, r_ref, o_ref, ob_ref):
        o_ref[...] = g_ref[...] + r_ref[...]
        ob_ref[...] = o_ref[...].astype(BF16)

    return pl.pallas_call(
        body,
        name=name,
        grid_spec=pltpu.PrefetchScalarGridSpec(
            num_scalar_prefetch=1,
            grid=(n,),
            in_specs=[
                pl.BlockSpec((hb, C), lambda p, c: (2 * p + c[0], 0)),
                pl.BlockSpec((None, hb, C), lambda p, c: (p, 0, 0)),
            ],
            out_specs=[pl.BlockSpec((None, hb, C), lambda p, c: (p, 0, 0))] * 2,
        ),
        out_shape=[jax.ShapeDtypeStruct((n, hb, C), F32), jax.ShapeDtypeStruct((n, hb, C), BF16)],
        compiler_params=_params(("parallel",)),
    )(core, g, recv)


def _add_chips(own, recv, name):
    h, W = own.shape
    th = 256 if h % 256 == 0 else h

    def body(a_ref, r_ref, o_ref):
        o_ref[...] = ((a_ref[...] + r_ref[0].astype(F32)) + r_ref[1].astype(F32)) + r_ref[2].astype(F32)

    return pl.pallas_call(
        body,
        name=name,
        grid=(h // th,),
        in_specs=[pl.BlockSpec((th, W), lambda i: (i, 0)), pl.BlockSpec((N_CHIPS - 1, th, W), lambda i: (0, i, 0))],
        out_specs=pl.BlockSpec((th, W), lambda i: (i, 0)),
        out_shape=jax.ShapeDtypeStruct((h, W), F32),
        compiler_params=_params(("parallel",)),
    )(own, recv)


def _place():
    x, y, c = lax.axis_index("x"), lax.axis_index("y"), lax.axis_index("c")
    other_chips = [(1 - x, y), (x, 1 - y), (1 - x, 1 - y)]
    return x, y, c, other_chips


def _allgather_w_in(w_in_bf, x2, norm_w):
    D, S = w_in_bf.shape
    T = x2.shape[0]
    tm = min(1024, T)
    ni = T // tm
    n_ici = n_fwd = N_CHIPS - 1

    def body(win, x_ref, nw_ref, gin, hn_ref, hnt_ref, send_sems, recv_sems):
        step = pl.program_id(0)
        xv = x_ref[...]
        hn = xv * lax.rsqrt(jnp.mean(xv * xv, axis=-1, keepdims=True) + EPS) * nw_ref[...]
        hn_ref[...] = hn.astype(BF16)
        hnt_ref[...] = hn.T.astype(BF16)
        x, y, c, chips = _place()
        me = 2 * x + y
        sibling = (x, y, 1 - c)
        hin = D // 2

        def half(chip_idx, core):
            return gin.at[chip_idx, pl.ds(core * hin, hin)]

        def rcopy(k, src, dst, to):
            return pltpu.make_async_remote_copy(src_ref=src, dst_ref=dst, send_sem=send_sems.at[k],
                                                recv_sem=recv_sems.at[k], device_id=to, device_id_type=MESH)

        def sends():
            return [rcopy(j, win.at[pl.ds(c * hin, hin)], half(me, c), (*chip, c)) for j, chip in enumerate(chips)]

        @pl.when(step == 0)
        def _():
            for cp in sends():
                cp.start()

        @pl.when(step == ni - 1)
        def _():
            passed = []
            for j, (px, py) in enumerate(chips):
                theirs = half(2 * px + py, c)
                rcopy(j, theirs, theirs, sibling).wait_recv()
                passed.append(rcopy(n_ici + j, theirs, theirs, sibling))
                passed[-1].start()
            for j, (px, py) in enumerate(chips):
                other = half(2 * px + py, 1 - c)
                rcopy(n_ici + j, other, other, sibling).wait_recv()
            for cp in sends() + passed:
                cp.wait_send()

    return pl.pallas_call(
        body,
        name="allgather_w_in",
        grid=(ni,),
        in_specs=[ANY, pl.BlockSpec((tm, D), lambda i: (i, 0)), pl.BlockSpec((1, D), lambda i: (0, 0))],
        out_specs=[ANY, pl.BlockSpec((tm, D), lambda i: (i, 0)), pl.BlockSpec((D, tm), lambda i: (0, i))],
        out_shape=[jax.ShapeDtypeStruct((N_CHIPS, D, S), BF16),
                   jax.ShapeDtypeStruct((T, D), BF16), jax.ShapeDtypeStruct((D, T), BF16)],
        scratch_shapes=[pltpu.SemaphoreType.DMA((n_ici + n_fwd,)), pltpu.SemaphoreType.DMA((n_ici + n_fwd,))],
        compiler_params=_params(("arbitrary",)),
    )(w_in_bf, x2, norm_w)


def _allreduce_small(packed):
    R = packed.shape[0]
    n_dev = 2 * N_CHIPS

    def body(p_ref, o_ref, buf, send_sems, recv_sems):
        x, y, c, _ = _place()
        me = 4 * x + 2 * y + c
        buf[me] = p_ref[...]
        copies = []
        for k in range(1, n_dev):
            px = 1 - x if k & 4 else x
            py = 1 - y if k & 2 else y
            pc = 1 - c if k & 1 else c
            copies.append((pltpu.make_async_remote_copy(
                src_ref=buf.at[me], dst_ref=buf.at[me], send_sem=send_sems.at[k - 1], recv_sem=recv_sems.at[k - 1],
                device_id=(px, py, pc), device_id_type=MESH), 4 * px + 2 * py + pc, (px, py, pc)))
        for cp, _, _ in copies:
            cp.start()
        for k, (_, peer, to) in enumerate(copies):
            pltpu.make_async_remote_copy(
                src_ref=buf.at[peer], dst_ref=buf.at[peer], send_sem=send_sems.at[k], recv_sem=recv_sems.at[k],
                device_id=to, device_id_type=MESH).wait_recv()
        for cp, _, _ in copies:
            cp.wait_send()
        acc = buf[0]
        for d in range(1, n_dev):
            acc = acc + buf[d]
        o_ref[...] = acc

    vm = pl.BlockSpec(memory_space=pltpu.VMEM)
    return pl.pallas_call(
        body,
        name="allreduce_small",
        in_specs=[vm],
        out_specs=vm,
        out_shape=jax.ShapeDtypeStruct((R, LANES), F32),
        scratch_shapes=[pltpu.VMEM((n_dev, R, LANES), F32), pltpu.SemaphoreType.DMA((n_dev - 1,)),
                        pltpu.SemaphoreType.DMA((n_dev - 1,))],
    )(packed)


def _swap_core_halves(g_out):
    hb = g_out.shape[0] // (2 * N_CHIPS)

    def body(gout, rout, send_sems, recv_sems):
        x, y, c, _ = _place()
        cps = [pltpu.make_async_remote_copy(
            src_ref=gout.at[pl.ds((2 * p + 1 - c) * hb, hb)], dst_ref=rout.at[p], send_sem=send_sems.at[p],
            recv_sem=recv_sems.at[p], device_id=(x, y, 1 - c), device_id_type=MESH) for p in range(N_CHIPS)]
        for cp in cps:
            cp.start()
        for cp in cps:
            cp.wait()

    return pl.pallas_call(
        body,
        name="reduce_core_swap",
        in_specs=[ANY],
        out_specs=ANY,
        out_shape=jax.ShapeDtypeStruct((N_CHIPS, hb, g_out.shape[1]), F32),
        scratch_shapes=[pltpu.SemaphoreType.DMA((N_CHIPS,)), pltpu.SemaphoreType.DMA((N_CHIPS,))],
    )(g_out)


def _join_core_halves(g_in, g_out):
    def body(gin, gout, fin, fout, send_sems, recv_sems):
        x, y, c, _ = _place()
        cps = [pltpu.make_async_remote_copy(src_ref=s, dst_ref=d.at[c], send_sem=send_sems.at[k],
                                            recv_sem=recv_sems.at[k], device_id=(x, y, 1 - c), device_id_type=MESH)
               for k, (s, d) in enumerate(((gin, fin), (gout, fout)))]
        for cp in cps:
            cp.start()
        for k, (s, d) in enumerate(((gin, fin), (gout, fout))):
            pltpu.make_async_remote_copy(src_ref=s, dst_ref=d.at[1 - c], send_sem=send_sems.at[k],
                                         recv_sem=recv_sems.at[k], device_id=(x, y, 1 - c),
                                         device_id_type=MESH).wait_recv()
        for cp in cps:
            cp.wait_send()

    return pl.pallas_call(
        body,
        name="reduce_core_join",
        in_specs=[ANY, ANY],
        out_specs=[ANY, ANY],
        out_shape=[jax.ShapeDtypeStruct((2,) + g_in.shape, F32), jax.ShapeDtypeStruct((2,) + g_out.shape, F32)],
        scratch_shapes=[pltpu.SemaphoreType.DMA((2,)), pltpu.SemaphoreType.DMA((2,))],
    )(g_in, g_out)


def _pack(arrays):
    rows = []
    for a in arrays:
        flat = a.reshape(-1).astype(F32)
        n = -(-flat.shape[0] // LANES) * LANES
        rows.append(jnp.pad(flat, (0, n - flat.shape[0])).reshape(-1, LANES))
    out = jnp.concatenate(rows, axis=0)
    return jnp.pad(out, ((0, -out.shape[0] % 8), (0, 0)))


def _unpack(packed, shapes):
    out, r = [], 0
    for shp in shapes:
        n = math.prod(shp)
        nr = -(-n // LANES)
        out.append(packed[r:r + nr].reshape(-1)[:n].reshape(shp))
        r += nr
    return out


def _pad_lanes(a):
    return jnp.pad(a, ((0, 0), (0, LANES - a.shape[1])))


def kernel(x, norm_w, w_in, q_norm_w, k_norm_w, conv_w, conv_b, dt_bias, A_log, D_skip, sb_norm_w, ssd_norm_w, w_out, loss_target, m_norm_w, m_w_in, m_q_norm_w, m_k_norm_w, m_conv_w, m_conv_b, m_dt_bias, m_A_log, m_D_skip, m_sb_norm_w, m_ssd_norm_w, m_w_out, v_norm_w, v_w_in, v_q_norm_w, v_k_norm_w, v_conv_w, v_conv_b, v_dt_bias, v_A_log, v_D_skip, v_sb_norm_w, v_ssd_norm_w, v_w_out):
    Bl, L, D = x.shape
    T = Bl * L
    S = w_in.shape[2]
    R = w_out.shape[1]
    CW = conv_w.shape[2]
    n_in = N_CHIPS * S
    CD = D + 2 * SSD_GROUPS * SSD_STATE
    H = D // HEAD_DIM
    n_main = 6 * D + 512
    P = -(-(n_main + LANES) // 1024) * 1024
    assert n_in == n_main + H and CD == N_CHIPS * CW and 2 * D == N_CHIPS * R and CD == D + 512
    chip = (2 * lax.axis_index("x") + lax.axis_index("y")).astype(jnp.int32)
    core = lax.axis_index("c").astype(jnp.int32)

    w_in_bf, w_out_shard_bf = w_in[0].astype(BF16), w_out[0].astype(BF16)
    x2 = x.reshape(T, D)
    g_in, hn, hn_t = _allgather_w_in(w_in_bf, x2, norm_w)
    g_in = lax.dynamic_update_slice(g_in, w_in_bf[None], (chip, 0, 0))
    w_pad = jnp.concatenate([g_in[p] for p in range(N_CHIPS)] + [jnp.zeros((D, P - n_in), BF16)], axis=1)
    proj, w_pad_t, g_out, g_cw = _inproj(hn, w_pad, w_out_shard_bf, conv_w[0])
    g_out = lax.dynamic_update_slice(g_out, w_out_shard_bf[None], (chip, 0, 0))
    g_cw = lax.dynamic_update_slice(g_cw, conv_w, (chip, 0, 0))
    w_out_bf = g_out.reshape(2 * D, D)
    conv_full = g_cw.transpose(1, 0, 2).reshape(CONV_K, CD)
    cwx, cwb = conv_full[:, :D], conv_full[:, D:]
    cbx, cbb = conv_b[:, :D], conv_b[:, D:]
    dtb, alog, dsk = _pad_lanes(dt_bias), _pad_lanes(A_log), _pad_lanes(D_skip)
    qw2, kw2 = jnp.tile(q_norm_w, (1, 2)), jnp.tile(k_norm_w, (1, 2))

    proj3 = proj.reshape(Bl, L, P)
    o_sb = _attn_fwd(proj3, qw2, kw2, D)
    y_ssd, s_in = _ssd_fwd(proj3, cwx, cwb, cbx, cbb, dtb, alog, dsk, D)
    dout, dout_bf, mixed_t, do_sb, dy_ssd, dz_bf, dnw_out, loss_blk = _gate_out(
        x2, loss_target.reshape(T, D), o_sb.reshape(T, D), proj, y_ssd.reshape(T, D), sb_norm_w, ssd_norm_w, w_out_bf,
        w_out_bf.T)

    dq, dk, dv, dqkw = _attn_bwd(proj3, o_sb, do_sb.reshape(Bl, L, D), qw2, kw2, D)
    dtail, dcwx, dcwb, dcbx, dcbb, misc = _ssd_bwd(
        proj3, s_in, dy_ssd.reshape(Bl, L, D), cwx, cwb, cbx, cbb, dtb, alog, dsk, D, P - 5 * D)
    dproj = [dq.reshape(T, D), dk.reshape(T, D), dv.reshape(T, D), dz_bf, dtail.reshape(T, P - 5 * D)]
    gw_in = _grad_w_in(hn_t, dproj)

    slab_off = S // LANES * LANES
    slab_w = -(-(S + (N_CHIPS - 1) * (S - slab_off)) // LANES) * LANES
    width = (N_CHIPS - 1) * slab_off + slab_w
    assert n_in <= width <= P
    core1 = core.reshape(1)
    gw_out, r_in = _grad_w_out(mixed_t, dout_bf, gw_in, width)
    r_out = _swap_core_halves(gw_out)
    h_in, hb_in = _add_core_rows(gw_in, r_in, core1, "sum_cores_w_in")
    h_out, hb_out = _add_core_blocks(gw_out, r_out, core1, "sum_cores_w_out")
    grad_x2, dnw_in, s_in_, s_out_, o_in_, o_out_ = _dhn(dproj, w_pad_t, x2, dout, norm_w, h_in, h_out, hb_in, hb_out,
                                                         slab_off, slab_w)
    gh_in = _add_chips(o_in_, s_in_, "sum_chips_w_in")
    gh_out = _add_chips(o_out_, s_out_, "sum_chips_w_out")
    f_in, f_out = _join_core_halves(gh_in, gh_out)
    g_slab = lax.dynamic_update_slice(f_in, gh_in[None], (core, 0, 0)).reshape(D, slab_w)
    g_w_in = lax.dynamic_slice(g_slab, (0, chip * (S - slab_off)), (D, S))
    g_w_out = lax.dynamic_update_slice(f_out, gh_out[None], (core, 0, 0)).reshape(R, D)

    small_shapes = [(1, D), (1, D), (1, D), (1, CD), (1, HEAD_DIM), (1, HEAD_DIM), (1, H), (1, H), (1, H)]
    g_small_local = [dnw_in[0:1], dnw_out[0:1], dnw_out[1:2], jnp.concatenate([dcbx, dcbb], axis=1),
                     dqkw[0:1, :HEAD_DIM] + dqkw[0:1, HEAD_DIM:], dqkw[1:2, :HEAD_DIM] + dqkw[1:2, HEAD_DIM:],
                     misc[0:1, :H], misc[1:2, :H], misc[2:3, :H]]
    packed = _pack(g_small_local + [jnp.concatenate([dcwx, dcwb], axis=1), loss_blk[0:1, 0:1]])
    red = _allreduce_small(packed)
    g_small = _unpack(red, small_shapes + [(CONV_K, CD), (1, 1)])
    g_conv_w = lax.dynamic_slice_in_dim(g_small[9], chip * CW, CW, axis=1)
    loss = g_small[10][0, 0]

    d_in, nm_in, nv_in = _adamw(w_in[0], g_w_in, m_w_in[0], v_w_in[0], "adamw_w_in")
    d_out, nm_out, nv_out = _adamw(w_out[0], g_w_out, m_w_out[0], v_w_out[0], "adamw_w_out")
    d_cw, nm_cw, nv_cw = _adamw(conv_w[0], g_conv_w, m_conv_w[0], v_conv_w[0], "adamw_conv_w")
    small_w = [norm_w, sb_norm_w, ssd_norm_w, conv_b, q_norm_w, k_norm_w, dt_bias, A_log, D_skip]
    small_m = [m_norm_w, m_sb_norm_w, m_ssd_norm_w, m_conv_b, m_q_norm_w, m_k_norm_w, m_dt_bias, m_A_log, m_D_skip]
    small_v = [v_norm_w, v_sb_norm_w, v_ssd_norm_w, v_conv_b, v_q_norm_w, v_k_norm_w, v_dt_bias, v_A_log, v_D_skip]
    d_s, nm_s, nv_s = _adamw(_pack(small_w), _pack(g_small[:9]), _pack(small_m), _pack(small_v), "adamw_small")
    d_s, nm_s, nv_s = (_unpack(t, small_shapes) for t in (d_s, nm_s, nv_s))

    def ordered(s, w_in_, conv_w_, w_out_):
        return [s[0], w_in_[None], s[4], s[5], conv_w_[None], s[3], s[6], s[7], s[8], s[1], s[2], w_out_[None]]

    return (loss, grad_x2.reshape(Bl, L, D),
            *ordered(g_small[:9], g_w_in, g_conv_w, g_w_out),
            *ordered(d_s, d_in, d_cw, d_out),
            *ordered(nm_s, nm_in, nm_cw, nm_out),
            *ordered(nv_s, nv_in, nv_cw, nv_out))
```

```python
import functools
import math

import jax
import jax.numpy as jnp
from jax import lax
from jax.experimental import pallas as pl
from jax.experimental.pallas import tpu as pltpu

F32 = jnp.float32
BF16 = jnp.bfloat16
EPS = 1e-6
HEAD_DIM = 64
PAIR = 2 * HEAD_DIM
LANES = 128
SSD_STATE = 128
SSD_GROUPS = 2
BLK = 128
PREP_BLOCKS = 16
Q_TOGETHER_FWD = 2
Q_TOGETHER_BWD = 2
FIRST_LEFT = 2
UNDERFLOW = -105.0
CONV_K = 4
HALO = 8
N_CHIPS = 4
ADAM_LR, ADAM_B1, ADAM_B2, ADAM_EPS, ADAM_WD, ADAM_STEP = 0.001, 0.9, 0.999, 1e-08, 0.01, 10
VMEM_LIMIT_V7X = 56 * 1024 * 1024
MESH = pl.DeviceIdType.MESH
ANY = pl.BlockSpec(memory_space=pl.ANY)
NT = (((1,), (1,)), ((), ()))


def _params(sem=None):
    kw = dict(vmem_limit_bytes=VMEM_LIMIT_V7X)
    if sem is not None:
        kw["dimension_semantics"] = sem
    return pltpu.CompilerParams(**kw)


def _dot(a, b):
    return jnp.dot(a, b, preferred_element_type=F32)


def _dot_nt(a, b):
    return lax.dot_general(a, b, NT, preferred_element_type=F32)


def _dot_split(m, x):
    hi = x.astype(BF16)
    lo = (x - hi.astype(F32)).astype(BF16)
    return _dot(m, hi) + _dot(m, lo)


def _iota(shape, dim):
    return lax.broadcasted_iota(jnp.int32, shape, dim)


def _rowsum(x):
    return jnp.sum(x, axis=1, keepdims=True)


def _colsum(x):
    return jnp.sum(x, axis=0, keepdims=True)


def _sigmoid(x):
    return 0.5 * jnp.tanh(0.5 * x) + 0.5


def _softplus(x):
    return jnp.maximum(x, 0.0) + jnp.log(1.0 + jnp.exp(-jnp.abs(x)))


def _inproj(hn, w_pad, w_out_bf, conv_w):
    T, D = hn.shape
    P = w_pad.shape[1]
    tm = min(1024, T)
    tn = 1024 if P % 1024 == 0 else 512
    ni, nj = T // tm, P // tn
    n_sem = 2 * (N_CHIPS - 1)

    def body(hn_ref, w_ref, wout, cw, proj_ref, wt_ref, gout, gcw, send_sems, recv_sems):
        def gather():
            x, y, c, chips = _place()
            me = 2 * x + y
            return [pltpu.make_async_remote_copy(
                src_ref=src, dst_ref=dst.at[me], send_sem=send_sems.at[2 * j + m], recv_sem=recv_sems.at[2 * j + m],
                device_id=(px, py, c), device_id_type=MESH)
                for j, (px, py) in enumerate(chips) for m, (src, dst) in enumerate(((wout, gout), (cw, gcw)))]

        @pl.when((pl.program_id(0) == 0) & (pl.program_id(1) == 0))
        def _():
            for cp in gather():
                cp.start()

        @pl.when((pl.program_id(0) == ni - 1) & (pl.program_id(1) == nj - 1))
        def _():
            for cp in gather():
                cp.wait()

        @pl.when(pl.program_id(0) == 0)
        def _():
            wt_ref[...] = w_ref[...].astype(F32).T.astype(BF16)

        proj_ref[...] = _dot(hn_ref[...], w_ref[...])

    return pl.pallas_call(
        body,
        name="inproj",
        grid=(T // tm, P // tn),
        in_specs=[
            pl.BlockSpec((tm, D), lambda i, j: (i, 0)),
            pl.BlockSpec((D, tn), lambda i, j: (0, j)),
            ANY, ANY,
        ],
        out_specs=[
            pl.BlockSpec((tm, tn), lambda i, j: (i, j)),
            pl.BlockSpec((tn, D), lambda i, j: (jnp.where(i == 0, j, nj - 1), 0)),
            ANY, ANY,
        ],
        out_shape=[jax.ShapeDtypeStruct((T, P), F32), jax.ShapeDtypeStruct((P, D), BF16),
                   jax.ShapeDtypeStruct((N_CHIPS,) + w_out_bf.shape, BF16),
                   jax.ShapeDtypeStruct((N_CHIPS,) + conv_w.shape, F32)],
        scratch_shapes=[pltpu.SemaphoreType.DMA((n_sem,)), pltpu.SemaphoreType.DMA((n_sem,))],
        compiler_params=_params(("arbitrary", "arbitrary")),
    )(hn, w_pad, w_out_bf, conv_w)


def _pair_ones():
    ri = ((_iota((2 * PAIR, PAIR), 0) % PAIR) >= HEAD_DIM).astype(jnp.int32)
    ci = (_iota((2 * PAIR, PAIR), 1) >= HEAD_DIM).astype(jnp.int32)
    return jnp.where(ri == ci, 1.0, 0.0).astype(BF16)


def _pair_rms(v, ones2):
    return lax.rsqrt(_split_dots([v * v], ones2)[0] * (1.0 / HEAD_DIM) + EPS)


def _pair_mean(v, ones2):
    return _split_dots([v], ones2)[0] * (1.0 / HEAD_DIM)


def _suffix_ones():
    ri = _iota((2 * BLK, 2 * BLK), 0) % BLK
    ci = _iota((2 * BLK, 2 * BLK), 1)
    return jnp.where((ci >= BLK) | (ri > ci), 1.0, 0.0).astype(BF16)


def _split_dots(xs, m2):
    his = [x.astype(BF16) for x in xs]
    los = [(x - hi.astype(F32)).astype(BF16) for x, hi in zip(xs, his)]
    return [_dot(jnp.concatenate([hi, lo], axis=1), m2) for hi, lo in zip(his, los)]


def _sb_tiles(streams, km_s, uo):
    tiles = [(s, u, h) for s, st in enumerate(streams) for u in range(len(st["kbs"])) for h in range(2)]
    z2s = {(s, u): _dot_nt(st["q"], km_s[kb]) for s, st in enumerate(streams) for u, kb in enumerate(st["kbs"])}
    zs = [z2s[s, u][:, h * BLK:(h + 1) * BLK] for s, u, h in tiles]
    es = [jnp.exp(-jnp.abs(z)) for z in zs]
    las = [jnp.minimum(z, 0.0) - jnp.log(1.0 + e) for z, e in zip(zs, es)]
    lns = [a - z for a, z in zip(las, zs)]
    masks = [streams[s]["masks"][u] for s, u, h in tiles]
    lks = [lk if m is None else jnp.where(m, lk, 0.0) for m, lk in zip(masks, lns)]
    css = _split_dots(lks, uo)
    rests = [list(st["rest"]) for st in streams]
    ws = []
    for (s, u, h), m, a, cs in zip(tiles, masks, las, css):
        w = jnp.exp(a + rests[s][h] + cs[:, :BLK])
        ws.append(w if m is None else jnp.where(m, w, 0.0))
        rests[s][h] = rests[s][h] + cs[:, BLK:]
    return tiles, las, lns, ws, rests


def _stream(q_pair, qi, n_left, diag, zero):
    return dict(q=q_pair, kbs=[qi - u for u in range(n_left + 1)], masks=[diag] + [None] * n_left, rest=[zero, zero])


def _row0(block):
    return block * BLK if isinstance(block, int) else pl.multiple_of(block * BLK, BLK)


def _pair_of(vals, tiles, s, u):
    return [v for v, t in zip(vals, tiles) if t[0] == s and t[1] == u]


def _block_groups(nq, together):
    n_tog = math.gcd(together, nq)
    assert n_tog >= FIRST_LEFT
    return n_tog, list(range(n_tog)), nq // n_tog


def _attn_prep(src_ref, w_ref, dst_s, n_blocks, scale):
    per = math.gcd(PREP_BLOCKS, n_blocks)
    rows = per * BLK
    lo = _iota((rows, PAIR), 1) < HEAD_DIM
    ones2 = _pair_ones()

    def step(i, carry):
        r0 = pl.multiple_of(i * rows, rows)
        v = src_ref[0, pl.ds(r0, rows), :]
        if w_ref is not None:
            v = v * _pair_rms(v, ones2) * w_ref[...]
        if scale != 1.0:
            v = v * scale
        v0, v1 = jnp.where(lo, v, 0.0).astype(BF16), jnp.where(lo, 0.0, v).astype(BF16)
        for b in range(per):
            dst_s[i * per + b, 0:BLK, :] = v0[b * BLK:(b + 1) * BLK]
            dst_s[i * per + b, BLK:2 * BLK, :] = v1[b * BLK:(b + 1) * BLK]
        return carry

    lax.fori_loop(0, n_blocks // per, step, 0)


def _attn_fwd(proj3, qw2, kw2, D):
    Bl, L, _ = proj3.shape
    n_pair = D // PAIR
    nq = L // BLK
    scale = 1.0 / math.sqrt(HEAD_DIM)

    def body(q_ref, k_ref, v_ref, qw_ref, kw_ref, o_ref, qm_s, km_s, vm_s):
        uo = _suffix_ones()
        diag = _iota((BLK, BLK), 1) < _iota((BLK, BLK), 0)
        _attn_prep(q_ref, qw_ref, qm_s, nq, scale)
        _attn_prep(k_ref, kw_ref, km_s, nq, 1.0)
        _attn_prep(v_ref, None, vm_s, nq, 1.0)

        zero_c = jnp.zeros((BLK, BLK), F32)

        def q_of(qi):
            return qm_s[qi, 0:BLK, :] + qm_s[qi, BLK:2 * BLK, :]

        def values(streams, accs):
            tiles, _, _, ws, rests = _sb_tiles(streams, km_s, uo)
            wbs = [w.astype(BF16) for w in ws]
            accs = list(accs)
            for s, st in enumerate(streams):
                for u, kb in enumerate(st["kbs"]):
                    accs[s] = accs[s] + _dot(jnp.concatenate(_pair_of(wbs, tiles, s, u), axis=1), vm_s[kb])
            return accs, rests

        def group(qis, n_lefts):
            streams = [_stream(q_of(qi), qi, n, diag, zero_c) for qi, n in zip(qis, n_lefts)]
            accs, rests = values(streams, [jnp.zeros((BLK, PAIR), F32)] * len(qis))
            for qi, n, q, acc, rc in zip(qis, n_lefts, [st["q"] for st in streams], accs, rests):

                def sweep(state, n_blocks, q=q):
                    kb, rc0, rc1, acc1, _ = state
                    st = dict(q=q, kbs=[kb - u for u in range(n_blocks)], masks=[None] * n_blocks, rest=[rc0, rc1])
                    (acc1,), (r,) = values([st], [acc1])
                    return kb - n_blocks, r[0], r[1], acc1, jnp.maximum(jnp.max(r[0]), jnp.max(r[1]))

                state = (jnp.asarray(qi - n - 1, jnp.int32), rc[0], rc[1], acc, jnp.maximum(jnp.max(rc[0]), jnp.max(rc[1])))
                state = lax.while_loop(lambda t: (t[0] >= 1) & (t[4] >= UNDERFLOW), lambda t: sweep(t, 2), state)
                state = lax.while_loop(lambda t: (t[0] >= 0) & (t[4] >= UNDERFLOW), lambda t: sweep(t, 1), state)
                o_ref[0, pl.ds(_row0(qi), BLK), :] = state[3]

        n_tog, head, n_groups = _block_groups(nq, Q_TOGETHER_FWD)
        group(head, [min(qi, FIRST_LEFT) for qi in head])

        def groups(g, carry):
            group([g * n_tog + j for j in range(n_tog)], [FIRST_LEFT] * n_tog)
            return carry

        lax.fori_loop(1, n_groups, groups, 0)

    blk = lambda off: pl.BlockSpec((1, L, PAIR), lambda b, p: (b, 0, off + p))
    wspec = pl.BlockSpec((1, PAIR), lambda b, p: (0, 0))
    return pl.pallas_call(
        body,
        name="sb_attn_fwd",
        grid=(Bl, n_pair),
        in_specs=[blk(0), blk(n_pair), blk(2 * n_pair), wspec, wspec],
        out_specs=pl.BlockSpec((1, L, PAIR), lambda b, p: (b, 0, p)),
        out_shape=jax.ShapeDtypeStruct((Bl, L, D), F32),
        scratch_shapes=[pltpu.VMEM((nq, 2 * BLK, PAIR), BF16)] * 3,
        compiler_params=_params(("parallel", "parallel")),
    )(proj3, proj3, proj3, qw2, kw2)


def _attn_bwd(proj3, o3, do3, qw2, kw2, D):
    Bl, L, _ = proj3.shape
    n_pair = D // PAIR
    nq = L // BLK
    scale = 1.0 / math.sqrt(HEAD_DIM)

    def body(q_ref, k_ref, v_ref, o_ref, do_ref, qw_ref, kw_ref, dq_ref, dk_ref, dv_ref, dw_ref,
             qm_s, km_s, vm_s, dom_s, dq_s, dk_s, dv_s):
        uo = _suffix_ones()
        diag = _iota((BLK, BLK), 1) < _iota((BLK, BLK), 0)
        ones2 = _pair_ones()
        _attn_prep(q_ref, qw_ref, qm_s, nq, scale)
        _attn_prep(k_ref, kw_ref, km_s, nq, 1.0)
        _attn_prep(v_ref, None, vm_s, nq, 1.0)
        _attn_prep(do_ref, None, dom_s, nq, 1.0)

        @pl.when((pl.program_id(0) == 0) & (pl.program_id(1) == 0))
        def _():
            dw_ref[...] = jnp.zeros_like(dw_ref)

        def zero(i, carry):
            r0 = pl.multiple_of(i * BLK, BLK)
            dk_s[pl.ds(r0, BLK), :] = jnp.zeros((BLK, PAIR), F32)
            dv_s[pl.ds(r0, BLK), :] = jnp.zeros((BLK, PAIR), F32)
            return carry

        lax.fori_loop(0, nq, zero, 0)

        zero_c = jnp.zeros((BLK, BLK), F32)

        def tiles_bwd(streams, dqas):
            tiles, las, lns, ws, rests = _sb_tiles(streams, km_s, uo)
            dw2s = {(s, u): _dot_nt(st["do"], vm_s[kb]) for s, st in enumerate(streams) for u, kb in enumerate(st["kbs"])}
            dws = [dw2s[s, u][:, h * BLK:(h + 1) * BLK] for s, u, h in tiles]
            wfs = [w.astype(BF16).astype(F32) for w in ws]
            gs = [wf * dw for wf, dw in zip(wfs, dws)]
            gss = _split_dots(gs, uo)
            gcs = [list(st["g_rest"]) for st in streams]
            dzs = []
            for (s, u, h), a, ln, g, gsum in zip(tiles, las, lns, gs, gss):
                g_before = streams[s]["delta"][h] - (gcs[s][h] + gsum[:, :BLK] + g)
                gcs[s][h] = gcs[s][h] + gsum[:, BLK:]
                dz = g * jnp.exp(ln) - g_before * jnp.exp(a)
                m = streams[s]["masks"][u]
                dzs.append(dz if m is None else jnp.where(m, dz, 0.0))
            wts = [wf.T.astype(BF16) for wf in wfs]
            dzts = [dz.T.astype(BF16) for dz in dzs]
            dzbs = [dz.astype(BF16) for dz in dzs]
            dqas = list(dqas)
            for s, st in enumerate(streams):
                for u, kb in enumerate(st["kbs"]):
                    c0 = _row0(kb)
                    dv_s[pl.ds(c0, BLK), :] += _dot(jnp.concatenate(_pair_of(wts, tiles, s, u), axis=1), dom_s[st["qi"]])
                    dk_s[pl.ds(c0, BLK), :] += _dot(jnp.concatenate(_pair_of(dzts, tiles, s, u), axis=1), qm_s[st["qi"]])
                    dqas[s] = dqas[s] + _dot(jnp.concatenate(_pair_of(dzbs, tiles, s, u), axis=1), km_s[kb])
            return dqas, rests, gcs

        def group(qis, n_lefts):
            streams = []
            for qi, n in zip(qis, n_lefts):
                o_blk = o_ref[0, pl.ds(_row0(qi), BLK), :]
                doms = [dom_s[qi, 0:BLK, :], dom_s[qi, BLK:2 * BLK, :]]
                st = _stream(qm_s[qi, 0:BLK, :] + qm_s[qi, BLK:2 * BLK, :], qi, n, diag, zero_c)
                st.update(qi=qi, do=doms[0] + doms[1], delta=[_rowsum(d.astype(F32) * o_blk) for d in doms],
                          g_rest=[zero_c, zero_c])
                streams.append(st)
            dqas, rests, gcs = tiles_bwd(streams, [jnp.zeros((BLK, PAIR), F32)] * len(qis))
            for qi, n, st0, dqa, rc, gc in zip(qis, n_lefts, streams, dqas, rests, gcs):

                def sweep(state, n_blocks, st0=st0):
                    kb, rc0, rc1, gc0, gc1, dqa1, _ = state
                    st = dict(st0, kbs=[kb - u for u in range(n_blocks)], masks=[None] * n_blocks, rest=[rc0, rc1],
                              g_rest=[gc0, gc1])
                    (dqa1,), (r,), (g,) = tiles_bwd([st], [dqa1])
                    return kb - n_blocks, r[0], r[1], g[0], g[1], dqa1, jnp.maximum(jnp.max(r[0]), jnp.max(r[1]))

                state = (jnp.asarray(qi - n - 1, jnp.int32), rc[0], rc[1], gc[0], gc[1], dqa,
                         jnp.maximum(jnp.max(rc[0]), jnp.max(rc[1])))
                state = lax.while_loop(lambda t: (t[0] >= 1) & (t[6] >= UNDERFLOW), lambda t: sweep(t, 2), state)
                state = lax.while_loop(lambda t: (t[0] >= 0) & (t[6] >= UNDERFLOW), lambda t: sweep(t, 1), state)
                dq_s[pl.ds(_row0(qi), BLK), :] = state[5] * scale

        n_tog, head, n_groups = _block_groups(nq, Q_TOGETHER_BWD)
        group(head, [min(qi, FIRST_LEFT) for qi in head])

        def groups(g, carry):
            group([g * n_tog + j for j in range(n_tog)], [FIRST_LEFT] * n_tog)
            return carry

        lax.fori_loop(1, n_groups, groups, 0)

        per = math.gcd(PREP_BLOCKS, nq)
        rows = per * BLK

        def finish(i, carry):
            r0 = pl.multiple_of(i * rows, rows)
            dwq, dwk = carry
            out = []
            for src_ref, w_ref, d_s in ((q_ref, qw_ref, dq_s), (k_ref, kw_ref, dk_s)):
                v = src_ref[0, pl.ds(r0, rows), :]
                r = _pair_rms(v, ones2)
                vh = v * r
                dy = d_s[pl.ds(r0, rows), :]
                dvh = dy * w_ref[...]
                out.append((r * (dvh - vh * _pair_mean(dvh * vh, ones2)), _colsum(dy * vh)))
            dq_ref[0, pl.ds(r0, rows), :] = out[0][0].astype(BF16)
            dk_ref[0, pl.ds(r0, rows), :] = out[1][0].astype(BF16)
            dv_ref[0, pl.ds(r0, rows), :] = dv_s[pl.ds(r0, rows), :].astype(BF16)
            return dwq + out[0][1], dwk + out[1][1]

        zrow = jnp.zeros((1, PAIR), F32)
        dwq, dwk = lax.fori_loop(0, nq // per, finish, (zrow, zrow))
        dw_ref[0:1, :] += dwq
        dw_ref[1:2, :] += dwk

    blk = lambda off: pl.BlockSpec((1, L, PAIR), lambda b, p: (b, 0, off + p))
    wspec = pl.BlockSpec((1, PAIR), lambda b, p: (0, 0))
    oblk = pl.BlockSpec((1, L, PAIR), lambda b, p: (b, 0, p))
    return pl.pallas_call(
        body,
        name="sb_attn_bwd",
        grid=(Bl, n_pair),
        in_specs=[blk(0), blk(n_pair), blk(2 * n_pair), oblk, oblk, wspec, wspec],
        out_specs=[oblk, oblk, oblk, pl.BlockSpec((8, PAIR), lambda b, p: (0, 0))],
        out_shape=[jax.ShapeDtypeStruct((Bl, L, D), BF16)] * 3 + [jax.ShapeDtypeStruct((8, PAIR), F32)],
        scratch_shapes=[pltpu.VMEM((nq, 2 * BLK, PAIR), BF16)] * 4 + [pltpu.VMEM((L, PAIR), F32)] * 3,
        compiler_params=_params(("arbitrary", "arbitrary")),
    )(proj3, proj3, proj3, o3, do3, qw2, kw2)


def _conv_pre(ext_s, halo_ref, raw_ref, w_ref, b_ref, first):
    ext_s[0:HALO, :] = jnp.where(first, 0.0, halo_ref[0])
    ext_s[HALO:HALO + BLK, :] = raw_ref[0]
    pre = b_ref[...]
    for i in range(CONV_K):
        pre = pre + ext_s[pl.ds(HALO - (CONV_K - 1 - i), BLK), :] * w_ref[i:i + 1, :]
    return pre


def _lane_col(m, lane, h):
    return _rowsum(jnp.where(lane == h, m, 0.0))


def _half_sums(row, lo1):
    return _rowsum(jnp.where(lo1, row, 0.0)), _rowsum(jnp.where(lo1, 0.0, row))


def _ssd_specs(Bl, L, D, rev):
    nc = L // BLK
    rows_per = BLK // HALO
    cidx = (lambda c: nc - 1 - c) if rev else (lambda c: c)
    xoff = 5
    boff = (6 * D) // 512
    doff = (6 * D + 512) // LANES
    prev = lambda c: jnp.maximum(cidx(c) * rows_per - 1, 0)
    specs = [
        pl.BlockSpec((1, BLK, D), lambda b, c: (b, cidx(c), xoff)),
        pl.BlockSpec((1, BLK, 512), lambda b, c: (b, cidx(c), boff)),
        pl.BlockSpec((1, HALO, D), lambda b, c: (b, prev(c), xoff)),
        pl.BlockSpec((1, HALO, 512), lambda b, c: (b, prev(c), boff)),
        pl.BlockSpec((1, BLK, LANES), lambda b, c: (b, cidx(c), doff)),
    ]
    full = lambda shape: pl.BlockSpec(shape, lambda b, c: (0,) * len(shape))
    specs += [full((CONV_K, D)), full((CONV_K, 512)), full((1, D)), full((1, 512)),
              full((1, LANES)), full((1, LANES)), full((1, LANES))]
    return specs, cidx


def _ssd_common(dtr_ref, dtb_ref, alog_ref, acs_s, acsT_s):
    ltri = jnp.where(_iota((BLK, BLK), 1) <= _iota((BLK, BLK), 0), 1.0, 0.0).astype(BF16)
    dtv = _softplus(dtr_ref[0] + dtb_ref[...])
    a = -jnp.exp(alog_ref[...])
    acs = _dot_split(ltri, dtv * a)
    acs_s[...] = acs
    acsT_s[...] = acs.T
    return dtv, a, acs


def _pair_terms(pr, acs, dtv, acs_s, lane, lo, lane1, lo1):
    h0, h1 = 2 * pr, 2 * pr + 1
    c0, c1 = _lane_col(acs, lane, h0), _lane_col(acs, lane, h1)
    d0, d1 = _lane_col(dtv, lane, h0), _lane_col(dtv, lane, h1)
    lastv = acs_s[BLK - 1:BLK, :]
    l0, l1 = _lane_col(lastv, lane1, h0), _lane_col(lastv, lane1, h1)
    return dict(h=(h0, h1), c=(c0, c1), last=(l0, l1), acs_p=jnp.where(lo, c0, c1), dt_p=jnp.where(lo, d0, d1),
                last_p=jnp.where(lo1, l0, l1))


def _decay_tiles(cc, row, tri, want_t):
    lm = jnp.where(tri, jnp.exp(cc - row), 0.0)
    return lm, (lm.T if want_t else None)


def _ssd_fwd(proj3, cwx, cwb, cbx, cbb, dtb, alog, dsk, D):
    Bl, L, _ = proj3.shape
    nc = L // BLK
    n_pair = D // PAIR
    pairs_per_group = n_pair // SSD_GROUPS
    specs, _ = _ssd_specs(Bl, L, D, False)

    def body(xr_ref, bcr_ref, xh_ref, bch_ref, dtr_ref, cwx_ref, cwb_ref, cbx_ref, cbb_ref, dtb_ref, alog_ref,
             dsk_ref, y_ref, sin_ref, st_s, extx_s, extb_s, acs_s, acsT_s):
        first = pl.program_id(1) == 0

        @pl.when(first)
        def _():
            st_s[...] = jnp.zeros_like(st_s)

        lane, lane1 = _iota((BLK, LANES), 1), _iota((1, LANES), 1)
        lo, lo1 = lane < HEAD_DIM, lane1 < HEAD_DIM
        tri = _iota((BLK, BLK), 1) <= _iota((BLK, BLK), 0)
        pre = _conv_pre(extx_s, xh_ref, xr_ref, cwx_ref, cbx_ref, first)
        ux = pre * _sigmoid(pre)
        pre = _conv_pre(extb_s, bch_ref, bcr_ref, cwb_ref, cbb_ref, first)
        ub = pre * _sigmoid(pre)
        dtv, a, acs = _ssd_common(dtr_ref, dtb_ref, alog_ref, acs_s, acsT_s)
        for g in range(SSD_GROUPS):
            bg = ub[:, g * SSD_STATE:(g + 1) * SSD_STATE]
            cb_ = ub[:, (SSD_GROUPS + g) * SSD_STATE:(SSD_GROUPS + g + 1) * SSD_STATE].astype(BF16)
            cbm = _dot_nt(cb_, bg.astype(BF16))
            btb = bg.T.astype(BF16)
            for pr in range(g * pairs_per_group, (g + 1) * pairs_per_group):
                t = _pair_terms(pr, acs, dtv, acs_s, lane, lo, lane1, lo1)
                xs_p = ux[:, pr * PAIR:(pr + 1) * PAIR]
                x_p = xs_p * t["dt_p"]
                st = st_s[pr]
                sin_ref[0, 0, pr] = st
                y = _dot(cb_, st.astype(BF16)) * jnp.exp(t["acs_p"])
                for k in range(2):
                    row = acsT_s[t["h"][k]:t["h"][k] + 1, :]
                    lm, _ = _decay_tiles(t["c"][k], row, tri, False)
                    xm = jnp.where(lo if k == 0 else ~lo, x_p, 0.0).astype(BF16)
                    y = y + _dot((cbm * lm).astype(BF16), xm)
                d_p = jnp.where(lo1, _lane_col(dsk_ref[...], lane1, t["h"][0]), _lane_col(dsk_ref[...], lane1, t["h"][1]))
                y_ref[0, :, pr * PAIR:(pr + 1) * PAIR] = y + d_p * xs_p
                xd = (x_p * jnp.exp(t["last_p"] - t["acs_p"])).astype(BF16)
                st_s[pr] = st * jnp.exp(t["last_p"]) + _dot(btb, xd)

    return pl.pallas_call(
        body,
        name="ssd_fwd",
        grid=(Bl, nc),
        in_specs=specs,
        out_specs=[
            pl.BlockSpec((1, BLK, D), lambda b, c: (b, c, 0)),
            pl.BlockSpec((1, 1, n_pair, SSD_STATE, PAIR), lambda b, c: (b, c, 0, 0, 0)),
        ],
        out_shape=[jax.ShapeDtypeStruct((Bl, L, D), F32),
                   jax.ShapeDtypeStruct((Bl, nc, n_pair, SSD_STATE, PAIR), F32)],
        scratch_shapes=[pltpu.VMEM((n_pair, SSD_STATE, PAIR), F32), pltpu.VMEM((HALO + BLK, D), F32),
                        pltpu.VMEM((HALO + BLK, 512), F32), pltpu.VMEM((BLK, LANES), F32),
                        pltpu.VMEM((LANES, BLK), F32)],
        compiler_params=_params(("arbitrary", "arbitrary")),
    )(proj3, proj3, proj3, proj3, proj3, cwx, cwb, cbx, cbb, dtb, alog, dsk)


def _ssd_bwd(proj3, s_in, dy3, cwx, cwb, cbx, cbb, dtb, alog, dsk, D, tail):
    Bl, L, _ = proj3.shape
    CD = D + 512
    nc = L // BLK
    n_pair = D // PAIR
    n_heads = 2 * n_pair
    pairs_per_group = n_pair // SSD_GROUPS
    specs, cidx = _ssd_specs(Bl, L, D, True)
    specs = specs + [
        pl.BlockSpec((1, 1, n_pair, SSD_STATE, PAIR), lambda b, c: (b, cidx(c), 0, 0, 0)),
        pl.BlockSpec((1, BLK, D), lambda b, c: (b, cidx(c), 0)),
    ]

    def body(xr_ref, bcr_ref, xh_ref, bch_ref, dtr_ref, cwx_ref, cwb_ref, cbx_ref, cbb_ref, dtb_ref, alog_ref,
             dsk_ref, sin_ref, dy_ref, dxbc_ref, dcwx_ref, dcwb_ref, dcbx_ref, dcbb_ref, misc_ref,
             dst_s, extx_s, extb_s, acs_s, acsT_s, dux_s, dub_s, e2x_s, e2b_s, nxx_s, nxb_s):
        step = pl.program_id(1)
        first = step == nc - 1
        last = step == 0

        @pl.when(last)
        def _():
            dst_s[...] = jnp.zeros_like(dst_s)
            nxx_s[...] = jnp.zeros_like(nxx_s)
            nxb_s[...] = jnp.zeros_like(nxb_s)

        @pl.when(last & (pl.program_id(0) == 0))
        def _():
            for r in (dcwx_ref, dcwb_ref, dcbx_ref, dcbb_ref, misc_ref):
                r[...] = jnp.zeros_like(r)

        lane, lane1 = _iota((BLK, LANES), 1), _iota((1, LANES), 1)
        lo, lo1 = lane < HEAD_DIM, lane1 < HEAD_DIM
        tri = _iota((BLK, BLK), 1) <= _iota((BLK, BLK), 0)
        prex = _conv_pre(extx_s, xh_ref, xr_ref, cwx_ref, cbx_ref, first)
        sgx = _sigmoid(prex)
        ux = prex * sgx
        preb = _conv_pre(extb_s, bch_ref, bcr_ref, cwb_ref, cbb_ref, first)
        sgb = _sigmoid(preb)
        ub = preb * sgb
        dtv, a, acs = _ssd_common(dtr_ref, dtb_ref, alog_ref, acs_s, acsT_s)
        dacs = jnp.zeros((BLK, LANES), F32)
        dlast = jnp.zeros((1, LANES), F32)
        ddt = jnp.zeros((BLK, LANES), F32)
        dd = jnp.zeros((1, LANES), F32)
        for g in range(SSD_GROUPS):
            bg = ub[:, g * SSD_STATE:(g + 1) * SSD_STATE]
            cg = ub[:, (SSD_GROUPS + g) * SSD_STATE:(SSD_GROUPS + g + 1) * SSD_STATE]
            bb, cb_ = bg.astype(BF16), cg.astype(BF16)
            cbm = _dot_nt(cb_, bb)
            cbt = _dot_nt(bb, cb_)
            ctb = cg.T.astype(BF16)
            dbg = jnp.zeros((BLK, SSD_STATE), F32)
            dcg = jnp.zeros((BLK, SSD_STATE), F32)
            for pr in range(g * pairs_per_group, (g + 1) * pairs_per_group):
                t = _pair_terms(pr, acs, dtv, acs_s, lane, lo, lane1, lo1)
                h0, h1 = t["h"]
                xs_p = ux[:, pr * PAIR:(pr + 1) * PAIR]
                dy_p = dy_ref[0, :, pr * PAIR:(pr + 1) * PAIR]
                x_p = xs_p * t["dt_p"]
                ea_p = jnp.exp(t["acs_p"])
                dte_p = jnp.exp(t["last_p"] - t["acs_p"])
                cd_p = jnp.exp(t["last_p"])
                st = sin_ref[0, 0, pr]
                dst = dst_s[pr]
                stb, dstb = st.astype(BF16), dst.astype(BF16)
                s0, s1 = _half_sums(_colsum(dy_p * xs_p), lo1)
                dd = dd + jnp.where(lane1 == h0, s0, 0.0) + jnp.where(lane1 == h1, s1, 0.0)
                d_p = jnp.where(lo1, _lane_col(dsk_ref[...], lane1, h0), _lane_col(dsk_ref[...], lane1, h1))
                dxs_p = d_p * dy_p
                dp = dy_p * ea_p
                dpb = dp.astype(BF16)
                yo = dp * _dot(cb_, stb)
                dcg = dcg + _dot_nt(dpb, stb)
                dst_off = _dot(ctb, dpb)
                dac = [_rowsum(jnp.where(lo, yo, 0.0)), _rowsum(jnp.where(lo, 0.0, yo))]
                s0, s1 = _half_sums(_colsum(dst * st), lo1)
                dl = [s0 * jnp.exp(t["last"][0]), s1 * jnp.exp(t["last"][1])]
                dxd = _dot(bb, dstb)
                dx_p = dxd * dte_p
                tt = dxd * x_p
                dbg = dbg + _dot_nt((x_p * dte_p).astype(BF16), dstb)
                for k, ddte in enumerate((_rowsum(jnp.where(lo, tt, 0.0)), _rowsum(jnp.where(lo, 0.0, tt)))):
                    ek = ddte * jnp.exp(t["last"][k] - t["c"][k])
                    dl[k] = dl[k] + _colsum(ek)
                    dac[k] = dac[k] - ek
                x_pb = x_p.astype(BF16)
                for k in range(2):
                    row = acsT_s[t["h"][k]:t["h"][k] + 1, :]
                    lm, lmt = _decay_tiles(t["c"][k], row, tri, True)
                    dym = jnp.where(lo if k == 0 else ~lo, dy_p, 0.0).astype(BF16)
                    dm = _dot_nt(dym, x_pb)
                    dmt = _dot_nt(x_pb, dym)
                    mt = cbt * lmt
                    dx_p = dx_p + _dot(mt.astype(BF16), dym)
                    dac[k] = dac[k] + _rowsum(dm * (cbm * lm)) - _rowsum(dmt * mt)
                    dcg = dcg + _dot((dm * lm).astype(BF16), bb)
                    dbg = dbg + _dot((dmt * lmt).astype(BF16), cb_)
                dacs = dacs + jnp.where(lane == h0, dac[0], 0.0) + jnp.where(lane == h1, dac[1], 0.0)
                dlast = dlast + jnp.where(lane1 == h0, dl[0], 0.0) + jnp.where(lane1 == h1, dl[1], 0.0)
                dxs_p = dxs_p + dx_p * t["dt_p"]
                t3 = dx_p * xs_p
                ddt = ddt + jnp.where(lane == h0, _rowsum(jnp.where(lo, t3, 0.0)), 0.0) \
                    + jnp.where(lane == h1, _rowsum(jnp.where(lo, 0.0, t3)), 0.0)
                dux_s[:, pr * PAIR:(pr + 1) * PAIR] = dxs_p
                dst_s[pr] = dst * cd_p + dst_off
            dub_s[:, g * SSD_STATE:(g + 1) * SSD_STATE] = dbg
            dub_s[:, (SSD_GROUPS + g) * SSD_STATE:(SSD_GROUPS + g + 1) * SSD_STATE] = dcg
        dacs = dacs + jnp.where(_iota((BLK, LANES), 0) == BLK - 1, dlast, 0.0)
        utri = jnp.where(_iota((BLK, BLK), 1) >= _iota((BLK, BLK), 0), 1.0, 0.0).astype(BF16)
        dda = _dot_split(utri, dacs)
        ddt = ddt + dda * a
        ddtr = jnp.where(lane < n_heads, ddt * _sigmoid(dtr_ref[0] + dtb_ref[...]), 0.0)
        dxbc_ref[0, :, CD:CD + LANES] = ddtr.astype(BF16)
        dxbc_ref[0, :, CD + LANES:tail] = jnp.zeros((BLK, tail - CD - LANES), BF16)
        misc_ref[0:1, :] += _colsum(ddtr)
        misc_ref[1:2, :] += jnp.where(lane1 < n_heads, _colsum(dda * dtv) * a, 0.0)
        misc_ref[2:3, :] += dd
        for (du_s, pre, sg, ext_s, e2_s, nx_s, w_ref, dcw_ref, dcb_ref, c0, width) in (
                (dux_s, prex, sgx, extx_s, e2x_s, nxx_s, cwx_ref, dcwx_ref, dcbx_ref, 0, D),
                (dub_s, preb, sgb, extb_s, e2b_s, nxb_s, cwb_ref, dcwb_ref, dcbb_ref, D, 512)):
            dpre = du_s[...] * (sg * (1.0 + pre * (1.0 - sg)))
            dcb_ref[...] += _colsum(dpre)
            for i in range(CONV_K):
                dcw_ref[i:i + 1, :] += _colsum(dpre * ext_s[pl.ds(HALO - (CONV_K - 1 - i), BLK), :])
            e2_s[0:BLK, :] = dpre
            e2_s[BLK:BLK + HALO, :] = nx_s[...]
            dxr = jnp.zeros((BLK, width), F32)
            for i in range(CONV_K):
                dxr = dxr + e2_s[pl.ds(CONV_K - 1 - i, BLK), :] * w_ref[i:i + 1, :]
            dxbc_ref[0, :, c0:c0 + width] = dxr.astype(BF16)
            nx_s[...] = e2_s[0:HALO, :]

    full = lambda shape: pl.BlockSpec(shape, lambda b, c: (0,) * len(shape))
    return pl.pallas_call(
        body,
        name="ssd_bwd",
        grid=(Bl, nc),
        in_specs=specs,
        out_specs=[
            pl.BlockSpec((1, BLK, tail), lambda b, c: (b, cidx(c), 0)),
            full((CONV_K, D)), full((CONV_K, 512)), full((1, D)), full((1, 512)), full((8, LANES)),
        ],
        out_shape=[
            jax.ShapeDtypeStruct((Bl, L, tail), BF16),
            jax.ShapeDtypeStruct((CONV_K, D), F32), jax.ShapeDtypeStruct((CONV_K, 512), F32),
            jax.ShapeDtypeStruct((1, D), F32), jax.ShapeDtypeStruct((1, 512), F32),
            jax.ShapeDtypeStruct((8, LANES), F32),
        ],
        scratch_shapes=[
            pltpu.VMEM((n_pair, SSD_STATE, PAIR), F32),
            pltpu.VMEM((HALO + BLK, D), F32), pltpu.VMEM((HALO + BLK, 512), F32),
            pltpu.VMEM((BLK, LANES), F32), pltpu.VMEM((LANES, BLK), F32),
            pltpu.VMEM((BLK, D), F32), pltpu.VMEM((BLK, 512), F32),
            pltpu.VMEM((BLK + HALO, D), F32), pltpu.VMEM((BLK + HALO, 512), F32),
            pltpu.VMEM((HALO, D), F32), pltpu.VMEM((HALO, 512), F32),
        ],
        compiler_params=_params(("arbitrary", "arbitrary")),
    )(proj3, proj3, proj3, proj3, proj3, cwx, cwb, cbx, cbb, dtb, alog, dsk, s_in, dy3)


def _gate_out(x2, tgt2, o2, proj2, y2, sbw, ssw, w_out_bf, w_out_t):
    T, D = x2.shape
    tm = min(256, T)

    def body(x_ref, t_ref, o_ref, zs_ref, y_ref, zy_ref, sbw_ref, ssw_ref, wo_ref, wot_ref,
             dout_ref, doutb_ref, mixt_ref, do_ref, dy_ref, dz_ref, dnw_ref, loss_ref):
        @pl.when(pl.program_id(0) == 0)
        def _():
            dnw_ref[...] = jnp.zeros_like(dnw_ref)
            loss_ref[...] = jnp.zeros_like(loss_ref)

        def fwd(o, z, w):
            sg = _sigmoid(z)
            sl = z * sg
            g = o * sl
            r = lax.rsqrt(jnp.mean(g * g, axis=-1, keepdims=True) + EPS)
            n = g * r
            return sg, sl, r, n, n * w

        def bwd(dy, o, z, w, sg, sl, r, n):
            dn = dy * w
            dg = r * (dn - n * jnp.mean(dn * n, axis=-1, keepdims=True))
            return dg * sl, dg * o * (sg * (1.0 + z * (1.0 - sg))), _colsum(dy * n)

        o1, z1, w1 = o_ref[...], zs_ref[...], sbw_ref[...]
        o2_, z2, w2 = y_ref[...], zy_ref[...], ssw_ref[...]
        sg1, sl1, r1, n1, y1 = fwd(o1, z1, w1)
        sg2, sl2, r2, n2, y2_ = fwd(o2_, z2, w2)
        y1b, y2b = y1.astype(BF16), y2_.astype(BF16)
        mixt_ref[0:D, :] = y1.T.astype(BF16)
        mixt_ref[D:2 * D, :] = y2_.T.astype(BF16)
        out = x_ref[...] + (_dot(y1b, wo_ref[0:D, :]) + _dot(y2b, wo_ref[D:2 * D, :]))
        err = out - t_ref[...]
        loss_ref[...] += jnp.sum(err * err) * (0.5 / D)
        dout = err * (1.0 / D)
        dout_ref[...] = dout
        doutb = dout.astype(BF16)
        doutb_ref[...] = doutb
        do1, dz1, dw1 = bwd(_dot(doutb, wot_ref[:, 0:D]), o1, z1, w1, sg1, sl1, r1, n1)
        do2, dz2, dw2 = bwd(_dot(doutb, wot_ref[:, D:2 * D]), o2_, z2, w2, sg2, sl2, r2, n2)
        do_ref[...] = do1
        dy_ref[...] = do2
        dz_ref[:, 0:D] = dz1.astype(BF16)
        dz_ref[:, D:2 * D] = dz2.astype(BF16)
        dnw_ref[0:1, :] += dw1
        dnw_ref[1:2, :] += dw2

    row = lambda col: pl.BlockSpec((tm, D), lambda i: (i, col))
    full = lambda shape: pl.BlockSpec(shape, lambda i: (0,) * len(shape))
    wide = pl.BlockSpec((tm, 2 * D), lambda i: (i, 0))
    return pl.pallas_call(
        body,
        name="gate_out",
        grid=(T // tm,),
        in_specs=[row(0), row(0), row(0), row(3), row(0), row(4), full((1, D)), full((1, D)), full((2 * D, D)),
                  full((D, 2 * D))],
        out_specs=[row(0), row(0), pl.BlockSpec((2 * D, tm), lambda i: (0, i)), row(0), row(0), wide,
                   full((8, D)), full((8, LANES))],
        out_shape=[
            jax.ShapeDtypeStruct((T, D), F32), jax.ShapeDtypeStruct((T, D), BF16),
            jax.ShapeDtypeStruct((2 * D, T), BF16), jax.ShapeDtypeStruct((T, D), F32),
            jax.ShapeDtypeStruct((T, D), F32), jax.ShapeDtypeStruct((T, 2 * D), BF16),
            jax.ShapeDtypeStruct((8, D), F32), jax.ShapeDtypeStruct((8, LANES), F32),
        ],
        compiler_params=_params(("arbitrary",)),
    )(x2, tgt2, o2, proj2, y2, proj2, sbw, ssw, w_out_bf, w_out_t)


def _piece_blocks(pieces, D):
    counts = [p.shape[1] // D for p in pieces]
    return [sum(counts[:i]) for i in range(len(counts))], counts


def _dhn(pieces, w_pad_t, x2, dout, norm_w, h_in, h_out, hb_in, hb_out, slab_off, slab_w):
    T, D = x2.shape
    tm = min(1024, T)
    starts, counts = _piece_blocks(pieces, D)
    units = [p for p, n in enumerate(counts) for _ in range(n)]
    per = 2 if all(units[2 * k] != units[2 * k + 1] for k in range(len(units) // 2)) else 1
    nk = -(-len(units) // per)
    ni = T // tm
    n_sem = 2 * (N_CHIPS - 1)
    assert len(units) * D == w_pad_t.shape[0]

    def body(*refs):
        p_refs = refs[:len(pieces)]
        (wa_ref, wb_ref, x_hbm, dout_hbm, nw_ref, hin, hout, hbin, hbout, gx_ref, dnw_ref, rin, rout, oin, oout,
         acc_s, x_s, dout_s, send_sems, recv_sems, row_sems, own_sems) = refs[len(pieces):]
        i, k = pl.program_id(0), pl.program_id(1)

        def rows():
            r0 = pl.multiple_of(i * tm, tm)
            return [pltpu.make_async_copy(src.at[pl.ds(r0, tm)], dst, row_sems.at[n])
                    for n, (src, dst) in enumerate(((x_hbm, x_s), (dout_hbm, dout_s)))]

        @pl.when(k == 0)
        def _():
            for cp in rows():
                cp.start()

        def scatter():
            x, y, c, chips = _place()

            def slab(ref, p):
                return ref.at[:, pl.ds(pl.multiple_of(p * slab_off, LANES), slab_w)]

            cps = []
            for j, (px, py) in enumerate(chips):
                p = 2 * px + py
                for m, (src, dst) in enumerate(((slab(hbin, p), rin.at[j]), (hbout.at[p], rout.at[j]))):
                    cps.append(pltpu.make_async_remote_copy(
                        src_ref=src, dst_ref=dst, send_sem=send_sems.at[2 * j + m], recv_sem=recv_sems.at[2 * j + m],
                        device_id=(px, py, c), device_id_type=MESH))
            me = 2 * x + y
            own = [pltpu.make_async_copy(slab(hin, me), oin, own_sems.at[0]),
                   pltpu.make_async_copy(hout.at[me], oout, own_sems.at[1])]
            return cps + own

        @pl.when((i == 0) & (k == 0))
        def _():
            for cp in scatter():
                cp.start()

        @pl.when((i == ni - 1) & (k == nk - 1))
        def _():
            for cp in scatter():
                cp.wait()

        @pl.when((i == 0) & (k == 0))
        def _():
            dnw_ref[...] = jnp.zeros_like(dnw_ref)

        for step in range(nk):
            @pl.when(k == step)
            def _(step=step):
                part = sum(_dot(p_refs[units[u]][...], w[...])
                           for u, w in list(zip(range(per * step, per * step + per), (wa_ref, wb_ref))) if u < len(units))
                acc_s[...] = part if step == 0 else acc_s[...] + part

        @pl.when(k == nk - 1)
        def _():
            for cp in rows():
                cp.wait()
            xv = x_s[...]
            r = lax.rsqrt(jnp.mean(xv * xv, axis=-1, keepdims=True) + EPS)
            xh = xv * r
            dhn = acc_s[...]
            dxh = dhn * nw_ref[...]
            gx_ref[...] = dout_s[...] + r * (dxh - xh * jnp.mean(dxh * xh, axis=-1, keepdims=True))
            dnw_ref[0:1, :] += _colsum(dhn * xh)

    return pl.pallas_call(
        body,
        name="dhn",
        grid=(T // tm, nk),
        in_specs=[pl.BlockSpec((tm, D), lambda i, k, s=s, n=n: (i, jnp.clip(per * k - s + (per * k < s), 0, n - 1)))
                  for s, n in zip(starts, counts)] + [
            pl.BlockSpec((D, D), lambda i, k: (jnp.minimum(per * k, len(units) - 1), 0)),
            pl.BlockSpec((D, D), lambda i, k: (jnp.minimum(per * k + per - 1, len(units) - 1), 0)),
            ANY, ANY,
            pl.BlockSpec((1, D), lambda i, k: (0, 0)),
            ANY, ANY, ANY, ANY,
        ],
        out_specs=[pl.BlockSpec((tm, D), lambda i, k: (i, 0)), pl.BlockSpec((8, D), lambda i, k: (0, 0)),
                   ANY, ANY, ANY, ANY],
        out_shape=[jax.ShapeDtypeStruct((T, D), F32), jax.ShapeDtypeStruct((8, D), F32),
                   jax.ShapeDtypeStruct((N_CHIPS - 1, h_in.shape[0], slab_w), BF16),
                   jax.ShapeDtypeStruct((N_CHIPS - 1,) + h_out.shape[1:], BF16),
                   jax.ShapeDtypeStruct((h_in.shape[0], slab_w), F32),
                   jax.ShapeDtypeStruct(h_out.shape[1:], F32)],
        scratch_shapes=[pltpu.VMEM((tm, D), F32)] * 3 + [pltpu.SemaphoreType.DMA((n_sem,)), pltpu.SemaphoreType.DMA((n_sem,)),
                                                      pltpu.SemaphoreType.DMA((2,)), pltpu.SemaphoreType.DMA((2,))],
        compiler_params=_params(("arbitrary", "arbitrary")),
    )(*pieces, w_pad_t, w_pad_t, x2, dout, norm_w, h_in, h_out, hb_in, hb_out)


def _grad_w_in(hn_t, pieces):
    D, T = hn_t.shape
    tk = min(1024, T)
    starts, counts = _piece_blocks(pieces, D)

    def body(*refs):
        a_ref, p_refs, o_ref = refs[0], refs[1:-1], refs[-1]
        j = pl.program_id(0)

        @pl.when(pl.program_id(1) == 0)
        def _():
            o_ref[...] = jnp.zeros_like(o_ref)

        for p_ref, s, n in zip(p_refs, starts, counts):
            @pl.when((j >= s) & (j < s + n))
            def _(p_ref=p_ref):
                o_ref[...] += _dot(a_ref[...], p_ref[...])

    def piece_spec(s, n):
        return pl.BlockSpec((tk, D), lambda j, k: (jnp.where((j >= s) & (j < s + n), k, 0), jnp.clip(j - s, 0, n - 1)))

    return pl.pallas_call(
        body,
        name="grad_w_in",
        grid=(sum(counts), T // tk),
        in_specs=[pl.BlockSpec((D, tk), lambda j, k: (0, k))] + [piece_spec(s, n) for s, n in zip(starts, counts)],
        out_specs=pl.BlockSpec((D, D), lambda j, k: (0, j)),
        out_shape=jax.ShapeDtypeStruct((D, sum(counts) * D), F32),
        compiler_params=_params(("parallel", "arbitrary")),
    )(hn_t, *pieces)


def _grad_w_out(a, b, g_in, width):
    M, K = a.shape
    N = b.shape[1]
    tm = min(1024, M)
    tn = 1024 if N % 1024 == 0 else (512 if N % 512 == 0 else N)
    tk = min(512, K)
    grid = (M // tm, N // tn, K // tk)
    h = g_in.shape[0] // 2

    def body(a_ref, b_ref, gin, o_ref, rin, send_sem, recv_sem):
        ids = [pl.program_id(d) for d in range(3)]

        def swap():
            x, y, c, _ = _place()
            return pltpu.make_async_remote_copy(
                src_ref=gin.at[pl.ds((1 - c) * h, h), pl.ds(0, width)], dst_ref=rin, send_sem=send_sem, recv_sem=recv_sem,
                device_id=(x, y, 1 - c), device_id_type=MESH)

        @pl.when((ids[0] == 0) & (ids[1] == 0) & (ids[2] == 0))
        def _():
            swap().start()

        @pl.when(ids[2] == 0)
        def _():
            o_ref[...] = jnp.zeros_like(o_ref)

        o_ref[...] += _dot(a_ref[...], b_ref[...])

        @pl.when((ids[0] == grid[0] - 1) & (ids[1] == grid[1] - 1) & (ids[2] == grid[2] - 1))
        def _():
            swap().wait()

    return pl.pallas_call(
        body,
        name="grad_w_out",
        grid=grid,
        in_specs=[pl.BlockSpec((tm, tk), lambda i, j, k: (i, k)), pl.BlockSpec((tk, tn), lambda i, j, k: (k, j)), ANY],
        out_specs=[pl.BlockSpec((tm, tn), lambda i, j, k: (i, j)), ANY],
        out_shape=[jax.ShapeDtypeStruct((M, N), F32), jax.ShapeDtypeStruct((h, width), F32)],
        scratch_shapes=[pltpu.SemaphoreType.DMA, pltpu.SemaphoreType.DMA],
        compiler_params=_params(("arbitrary", "arbitrary", "arbitrary")),
    )(a, b, g_in)


def _adamw(w, g, m, v, name):
    R, C = w.shape
    tr = 256 if R % 256 == 0 else R
    c1 = 1.0 - ADAM_B1 ** ADAM_STEP
    c2 = 1.0 - ADAM_B2 ** ADAM_STEP

    def body(w_ref, g_ref, m_ref, v_ref, d_ref, nm_ref, nv_ref):
        gv = g_ref[...]
        m_new = ADAM_B1 * m_ref[...] + (1.0 - ADAM_B1) * gv
        v_new = ADAM_B2 * v_ref[...] + (1.0 - ADAM_B2) * (gv * gv)
        d_ref[...] = -ADAM_LR * ((m_new / c1) / (jnp.sqrt(v_new / c2) + ADAM_EPS) + ADAM_WD * w_ref[...])
        nm_ref[...] = m_new
        nv_ref[...] = v_new

    spec = pl.BlockSpec((tr, C), lambda i: (i, 0))
    return pl.pallas_call(
        body,
        name=name,
        grid=(R // tr,),
        in_specs=[spec] * 4,
        out_specs=[spec] * 3,
        out_shape=[jax.ShapeDtypeStruct((R, C), F32)] * 3,
        compiler_params=_params(("parallel",)),
    )(w, g, m, v)


def _add_core_rows(g, recv, core, name):
    h, width = recv.shape
    th = 128 if h % 128 == 0 else h

    def body(c_ref, g_ref, r_ref, o_ref, ob_ref):
        o_ref[...] = g_ref[...] + r_ref[...]
        ob_ref[...] = o_ref[...].astype(BF16)

    return pl.pallas_call(
        body,
        name=name,
        grid_spec=pltpu.PrefetchScalarGridSpec(
            num_scalar_prefetch=1,
            grid=(h // th,),
            in_specs=[
                pl.BlockSpec((th, width), lambda i, c: (c[0] * (h // th) + i, 0)),
                pl.BlockSpec((th, width), lambda i, c: (i, 0)),
            ],
            out_specs=[pl.BlockSpec((th, width), lambda i, c: (i, 0))] * 2,
        ),
        out_shape=[jax.ShapeDtypeStruct((h, width), F32), jax.ShapeDtypeStruct((h, width), BF16)],
        compiler_params=_params(("parallel",)),
    )(core, g, recv)


def _add_core_blocks(g, recv, core, name):
    n, hb, C = recv.shape

    def body(c_ref, g_ref, r_ref, o_ref, ob_ref):
        o_ref[...] = g_ref[...] + r_ref[...]
        ob_ref[...] = o_ref[...].astype(BF16)

    return pl.pallas_call(
        body,
        name=name,
        grid_spec=pltpu.PrefetchScalarGridSpec(
            num_scalar_prefetch=1,
            grid=(n,),
            in_specs=[
                pl.BlockSpec((hb, C), lambda p, c: (2 * p + c[0], 0)),
                pl.BlockSpec((None, hb, C), lambda p, c: (p, 0, 0)),
            ],
            out_specs=[pl.BlockSpec((None, hb, C), lambda p, c: (p, 0, 0))] * 2,
        ),
        out_shape=[jax.ShapeDtypeStruct((n, hb, C), F32), jax.ShapeDtypeStruct((n, hb, C), BF16)],
        compiler_params=_params(("parallel",)),
    )(core, g, recv)


def _add_chips(own, recv, name):
    h, W = own.shape
    th = 256 if h % 256 == 0 else h

    def body(a_ref, r_ref, o_ref):
        o_ref[...] = ((a_ref[...] + r_ref[0].astype(F32)) + r_ref[1].astype(F32)) + r_ref[2].astype(F32)

    return pl.pallas_call(
        body,
        name=name,
        grid=(h // th,),
        in_specs=[pl.BlockSpec((th, W), lambda i: (i, 0)), pl.BlockSpec((N_CHIPS - 1, th, W), lambda i: (0, i, 0))],
        out_specs=pl.BlockSpec((th, W), lambda i: (i, 0)),
        out_shape=jax.ShapeDtypeStruct((h, W), F32),
        compiler_params=_params(("parallel",)),
    )(own, recv)


def _place():
    x, y, c = lax.axis_index("x"), lax.axis_index("y"), lax.axis_index("c")
    other_chips = [(1 - x, y), (x, 1 - y), (1 - x, 1 - y)]
    return x, y, c, other_chips


def _allgather_w_in(w_in_bf, x2, norm_w):
    D, S = w_in_bf.shape
    T = x2.shape[0]
    tm = min(1024, T)
    ni = T // tm
    n_ici = n_fwd = N_CHIPS - 1

    def body(win, x_ref, nw_ref, gin, hn_ref, hnt_ref, send_sems, recv_sems):
        step = pl.program_id(0)
        xv = x_ref[...]
        hn = xv * lax.rsqrt(jnp.mean(xv * xv, axis=-1, keepdims=True) + EPS) * nw_ref[...]
        hn_ref[...] = hn.astype(BF16)
        hnt_ref[...] = hn.T.astype(BF16)
        x, y, c, chips = _place()
        me = 2 * x + y
        sibling = (x, y, 1 - c)
        hin = D // 2

        def half(chip_idx, core):
            return gin.at[chip_idx, pl.ds(core * hin, hin)]

        def rcopy(k, src, dst, to):
            return pltpu.make_async_remote_copy(src_ref=src, dst_ref=dst, send_sem=send_sems.at[k],
                                                recv_sem=recv_sems.at[k], device_id=to, device_id_type=MESH)

        def sends():
            return [rcopy(j, win.at[pl.ds(c * hin, hin)], half(me, c), (*chip, c)) for j, chip in enumerate(chips)]

        @pl.when(step == 0)
        def _():
            for cp in sends():
                cp.start()

        @pl.when(step == ni - 1)
        def _():
            passed = []
            for j, (px, py) in enumerate(chips):
                theirs = half(2 * px + py, c)
                rcopy(j, theirs, theirs, sibling).wait_recv()
                passed.append(rcopy(n_ici + j, theirs, theirs, sibling))
                passed[-1].start()
            for j, (px, py) in enumerate(chips):
                other = half(2 * px + py, 1 - c)
                rcopy(n_ici + j, other, other, sibling).wait_recv()
            for cp in sends() + passed:
                cp.wait_send()

    return pl.pallas_call(
        body,
        name="allgather_w_in",
        grid=(ni,),
        in_specs=[ANY, pl.BlockSpec((tm, D), lambda i: (i, 0)), pl.BlockSpec((1, D), lambda i: (0, 0))],
        out_specs=[ANY, pl.BlockSpec((tm, D), lambda i: (i, 0)), pl.BlockSpec((D, tm), lambda i: (0, i))],
        out_shape=[jax.ShapeDtypeStruct((N_CHIPS, D, S), BF16),
                   jax.ShapeDtypeStruct((T, D), BF16), jax.ShapeDtypeStruct((D, T), BF16)],
        scratch_shapes=[pltpu.SemaphoreType.DMA((n_ici + n_fwd,)), pltpu.SemaphoreType.DMA((n_ici + n_fwd,))],
        compiler_params=_params(("arbitrary",)),
    )(w_in_bf, x2, norm_w)


def _allreduce_small(packed):
    R = packed.shape[0]
    n_dev = 2 * N_CHIPS

    def body(p_ref, o_ref, buf, send_sems, recv_sems):
        x, y, c, _ = _place()
        me = 4 * x + 2 * y + c
        buf[me] = p_ref[...]
        copies = []
        for k in range(1, n_dev):
            px = 1 - x if k & 4 else x
            py = 1 - y if k & 2 else y
            pc = 1 - c if k & 1 else c
            copies.append((pltpu.make_async_remote_copy(
                src_ref=buf.at[me], dst_ref=buf.at[me], send_sem=send_sems.at[k - 1], recv_sem=recv_sems.at[k - 1],
                device_id=(px, py, pc), device_id_type=MESH), 4 * px + 2 * py + pc, (px, py, pc)))
        for cp, _, _ in copies:
            cp.start()
        for k, (_, peer, to) in enumerate(copies):
            pltpu.make_async_remote_copy(
                src_ref=buf.at[peer], dst_ref=buf.at[peer], send_sem=send_sems.at[k], recv_sem=recv_sems.at[k],
                device_id=to, device_id_type=MESH).wait_recv()
        for cp, _, _ in copies:
            cp.wait_send()
        acc = buf[0]
        for d in range(1, n_dev):
            acc = acc + buf[d]
        o_ref[...] = acc

    vm = pl.BlockSpec(memory_space=pltpu.VMEM)
    return pl.pallas_call(
        body,
        name="allreduce_small",
        in_specs=[vm],
        out_specs=vm,
        out_shape=jax.ShapeDtypeStruct((R, LANES), F32),
        scratch_shapes=[pltpu.VMEM((n_dev, R, LANES), F32), pltpu.SemaphoreType.DMA((n_dev - 1,)),
                        pltpu.SemaphoreType.DMA((n_dev - 1,))],
    )(packed)


def _swap_core_halves(g_out):
    hb = g_out.shape[0] // (2 * N_CHIPS)

    def body(gout, rout, send_sems, recv_sems):
        x, y, c, _ = _place()
        cps = [pltpu.make_async_remote_copy(
            src_ref=gout.at[pl.ds((2 * p + 1 - c) * hb, hb)], dst_ref=rout.at[p], send_sem=send_sems.at[p],
            recv_sem=recv_sems.at[p], device_id=(x, y, 1 - c), device_id_type=MESH) for p in range(N_CHIPS)]
        for cp in cps:
            cp.start()
        for cp in cps:
            cp.wait()

    return pl.pallas_call(
        body,
        name="reduce_core_swap",
        in_specs=[ANY],
        out_specs=ANY,
        out_shape=jax.ShapeDtypeStruct((N_CHIPS, hb, g_out.shape[1]), F32),
        scratch_shapes=[pltpu.SemaphoreType.DMA((N_CHIPS,)), pltpu.SemaphoreType.DMA((N_CHIPS,))],
    )(g_out)


def _join_core_halves(g_in, g_out):
    def body(gin, gout, fin, fout, send_sems, recv_sems):
        x, y, c, _ = _place()
        cps = [pltpu.make_async_remote_copy(src_ref=s, dst_ref=d.at[c], send_sem=send_sems.at[k],
                                            recv_sem=recv_sems.at[k], device_id=(x, y, 1 - c), device_id_type=MESH)
               for k, (s, d) in enumerate(((gin, fin), (gout, fout)))]
        for cp in cps:
            cp.start()
        for k, (s, d) in enumerate(((gin, fin), (gout, fout))):
            pltpu.make_async_remote_copy(src_ref=s, dst_ref=d.at[1 - c], send_sem=send_sems.at[k],
                                         recv_sem=recv_sems.at[k], device_id=(x, y, 1 - c),
                                         device_id_type=MESH).wait_recv()
        for cp in cps:
            cp.wait_send()

    return pl.pallas_call(
        body,
        name="reduce_core_join",
        in_specs=[ANY, ANY],
        out_specs=[ANY, ANY],
        out_shape=[jax.ShapeDtypeStruct((2,) + g_in.shape, F32), jax.ShapeDtypeStruct((2,) + g_out.shape, F32)],
        scratch_shapes=[pltpu.SemaphoreType.DMA((2,)), pltpu.SemaphoreType.DMA((2,))],
    )(g_in, g_out)


def _pack(arrays):
    rows = []
    for a in arrays:
        flat = a.reshape(-1).astype(F32)
        n = -(-flat.shape[0] // LANES) * LANES
        rows.append(jnp.pad(flat, (0, n - flat.shape[0])).reshape(-1, LANES))
    out = jnp.concatenate(rows, axis=0)
    return jnp.pad(out, ((0, -out.shape[0] % 8), (0, 0)))


def _unpack(packed, shapes):
    out, r = [], 0
    for shp in shapes:
        n = math.prod(shp)
        nr = -(-n // LANES)
        out.append(packed[r:r + nr].reshape(-1)[:n].reshape(shp))
        r += nr
    return out


def _pad_lanes(a):
    return jnp.pad(a, ((0, 0), (0, LANES - a.shape[1])))


def kernel(x, norm_w, w_in, q_norm_w, k_norm_w, conv_w, conv_b, dt_bias, A_log, D_skip, sb_norm_w, ssd_norm_w, w_out, loss_target, m_norm_w, m_w_in, m_q_norm_w, m_k_norm_w, m_conv_w, m_conv_b, m_dt_bias, m_A_log, m_D_skip, m_sb_norm_w, m_ssd_norm_w, m_w_out, v_norm_w, v_w_in, v_q_norm_w, v_k_norm_w, v_conv_w, v_conv_b, v_dt_bias, v_A_log, v_D_skip, v_sb_norm_w, v_ssd_norm_w, v_w_out):
    Bl, L, D = x.shape
    T = Bl * L
    S = w_in.shape[2]
    R = w_out.shape[1]
    CW = conv_w.shape[2]
    n_in = N_CHIPS * S
    CD = D + 2 * SSD_GROUPS * SSD_STATE
    H = D // HEAD_DIM
    n_main = 6 * D + 512
    P = -(-(n_main + LANES) // 1024) * 1024
    assert n_in == n_main + H and CD == N_CHIPS * CW and 2 * D == N_CHIPS * R and CD == D + 512
    chip = (2 * lax.axis_index("x") + lax.axis_index("y")).astype(jnp.int32)
    core = lax.axis_index("c").astype(jnp.int32)

    w_in_bf, w_out_shard_bf = w_in[0].astype(BF16), w_out[0].astype(BF16)
    x2 = x.reshape(T, D)
    g_in, hn, hn_t = _allgather_w_in(w_in_bf, x2, norm_w)
    g_in = lax.dynamic_update_slice(g_in, w_in_bf[None], (chip, 0, 0))
    w_pad = jnp.concatenate([g_in[p] for p in range(N_CHIPS)] + [jnp.zeros((D, P - n_in), BF16)], axis=1)
    proj, w_pad_t, g_out, g_cw = _inproj(hn, w_pad, w_out_shard_bf, conv_w[0])
    g_out = lax.dynamic_update_slice(g_out, w_out_shard_bf[None], (chip, 0, 0))
    g_cw = lax.dynamic_update_slice(g_cw, conv_w, (chip, 0, 0))
    w_out_bf = g_out.reshape(2 * D, D)
    conv_full = g_cw.transpose(1, 0, 2).reshape(CONV_K, CD)
    cwx, cwb = conv_full[:, :D], conv_full[:, D:]
    cbx, cbb = conv_b[:, :D], conv_b[:, D:]
    dtb, alog, dsk = _pad_lanes(dt_bias), _pad_lanes(A_log), _pad_lanes(D_skip)
    qw2, kw2 = jnp.tile(q_norm_w, (1, 2)), jnp.tile(k_norm_w, (1, 2))

    proj3 = proj.reshape(Bl, L, P)
    o_sb = _attn_fwd(proj3, qw2, kw2, D)
    y_ssd, s_in = _ssd_fwd(proj3, cwx, cwb, cbx, cbb, dtb, alog, dsk, D)
    dout, dout_bf, mixed_t, do_sb, dy_ssd, dz_bf, dnw_out, loss_blk = _gate_out(
        x2, loss_target.reshape(T, D), o_sb.reshape(T, D), proj, y_ssd.reshape(T, D), sb_norm_w, ssd_norm_w, w_out_bf,
        w_out_bf.T)

    dq, dk, dv, dqkw = _attn_bwd(proj3, o_sb, do_sb.reshape(Bl, L, D), qw2, kw2, D)
    dtail, dcwx, dcwb, dcbx, dcbb, misc = _ssd_bwd(
        proj3, s_in, dy_ssd.reshape(Bl, L, D), cwx, cwb, cbx, cbb, dtb, alog, dsk, D, P - 5 * D)
    dproj = [dq.reshape(T, D), dk.reshape(T, D), dv.reshape(T, D), dz_bf, dtail.reshape(T, P - 5 * D)]
    gw_in = _grad_w_in(hn_t, dproj)

    slab_off = S // LANES * LANES
    slab_w = -(-(S + (N_CHIPS - 1) * (S - slab_off)) // LANES) * LANES
    width = (N_CHIPS - 1) * slab_off + slab_w
    assert n_in <= width <= P
    core1 = core.reshape(1)
    gw_out, r_in = _grad_w_out(mixed_t, dout_bf, gw_in, width)
    r_out = _swap_core_halves(gw_out)
    h_in, hb_in = _add_core_rows(gw_in, r_in, core1, "sum_cores_w_in")
    h_out, hb_out = _add_core_blocks(gw_out, r_out, core1, "sum_cores_w_out")
    grad_x2, dnw_in, s_in_, s_out_, o_in_, o_out_ = _dhn(dproj, w_pad_t, x2, dout, norm_w, h_in, h_out, hb_in, hb_out,
                                                         slab_off, slab_w)
    gh_in = _add_chips(o_in_, s_in_, "sum_chips_w_in")
    gh_out = _add_chips(o_out_, s_out_, "sum_chips_w_out")
    f_in, f_out = _join_core_halves(gh_in, gh_out)
    g_slab = lax.dynamic_update_slice(f_in, gh_in[None], (core, 0, 0)).reshape(D, slab_w)
    g_w_in = lax.dynamic_slice(g_slab, (0, chip * (S - slab_off)), (D, S))
    g_w_out = lax.dynamic_update_slice(f_out, gh_out[None], (core, 0, 0)).reshape(R, D)

    small_shapes = [(1, D), (1, D), (1, D), (1, CD), (1, HEAD_DIM), (1, HEAD_DIM), (1, H), (1, H), (1, H)]
    g_small_local = [dnw_in[0:1], dnw_out[0:1], dnw_out[1:2], jnp.concatenate([dcbx, dcbb], axis=1),
                     dqkw[0:1, :HEAD_DIM] + dqkw[0:1, HEAD_DIM:], dqkw[1:2, :HEAD_DIM] + dqkw[1:2, HEAD_DIM:],
                     misc[0:1, :H], misc[1:2, :H], misc[2:3, :H]]
    packed = _pack(g_small_local + [jnp.concatenate([dcwx, dcwb], axis=1), loss_blk[0:1, 0:1]])
    red = _allreduce_small(packed)
    g_small = _unpack(red, small_shapes + [(CONV_K, CD), (1, 1)])
    g_conv_w = lax.dynamic_slice_in_dim(g_small[9], chip * CW, CW, axis=1)
    loss = g_small[10][0, 0]

    d_in, nm_in, nv_in = _adamw(w_in[0], g_w_in, m_w_in[0], v_w_in[0], "adamw_w_in")
    d_out, nm_out, nv_out = _adamw(w_out[0], g_w_out, m_w_out[0], v_w_out[0], "adamw_w_out")
    d_cw, nm_cw, nv_cw = _adamw(conv_w[0], g_conv_w, m_conv_w[0], v_conv_w[0], "adamw_conv_w")
    small_w = [norm_w, sb_norm_w, ssd_norm_w, conv_b, q_norm_w, k_norm_w, dt_bias, A_log, D_skip]
    small_m = [m_norm_w, m_sb_norm_w, m_ssd_norm_w, m_conv_b, m_q_norm_w, m_k_norm_w, m_dt_bias, m_A_log, m_D_skip]
    small_v = [v_norm_w, v_sb_norm_w, v_ssd_norm_w, v_conv_b, v_q_norm_w, v_k_norm_w, v_dt_bias, v_A_log, v_D_skip]
    d_s, nm_s, nv_s = _adamw(_pack(small_w), _pack(g_small[:9]), _pack(small_m), _pack(small_v), "adamw_small")
    d_s, nm_s, nv_s = (_unpack(t, small_shapes) for t in (d_s, nm_s, nv_s))

    def ordered(s, w_in_, conv_w_, w_out_):
        return [s[0], w_in_[None], s[4], s[5], conv_w_[None], s[3], s[6], s[7], s[8], s[1], s[2], w_out_[None]]

    return (loss, grad_x2.reshape(Bl, L, D),
            *ordered(g_small[:9], g_w_in, g_conv_w, g_w_out),
            *ordered(d_s, d_in, d_cw, d_out),
            *ordered(nm_s, nm_in, nm_cw, nm_out),
            *ordered(nv_s, nv_in, nv_cw, nv_out))
```

```python
import functools
import math

import jax
import jax.numpy as jnp
from jax import lax
from jax.experimental import pallas as pl
from jax.experimental.pallas import tpu as pltpu

F32 = jnp.float32
BF16 = jnp.bfloat16
EPS = 1e-6
HEAD_DIM = 64
PAIR = 2 * HEAD_DIM
LANES = 128
SSD_STATE = 128
SSD_GROUPS = 2
BLK = 128
PREP_BLOCKS = 16
Q_TOGETHER_FWD = 2
Q_TOGETHER_BWD = 2
FIRST_LEFT = 2
UNDERFLOW = -105.0
CONV_K = 4
HALO = 8
N_CHIPS = 4
ADAM_LR, ADAM_B1, ADAM_B2, ADAM_EPS, ADAM_WD, ADAM_STEP = 0.001, 0.9, 0.999, 1e-08, 0.01, 10
VMEM_LIMIT_V7X = 56 * 1024 * 1024
MESH = pl.DeviceIdType.MESH
ANY = pl.BlockSpec(memory_space=pl.ANY)
NT = (((1,), (1,)), ((), ()))


def _params(sem=None):
    kw = dict(vmem_limit_bytes=VMEM_LIMIT_V7X)
    if sem is not None:
        kw["dimension_semantics"] = sem
    return pltpu.CompilerParams(**kw)


def _dot(a, b):
    return jnp.dot(a, b, preferred_element_type=F32)


def _dot_nt(a, b):
    return lax.dot_general(a, b, NT, preferred_element_type=F32)


def _dot_split(m, x):
    hi = x.astype(BF16)
    lo = (x - hi.astype(F32)).astype(BF16)
    return _dot(m, hi) + _dot(m, lo)


def _iota(shape, dim):
    return lax.broadcasted_iota(jnp.int32, shape, dim)


def _rowsum(x):
    return jnp.sum(x, axis=1, keepdims=True)


def _colsum(x):
    return jnp.sum(x, axis=0, keepdims=True)


def _sigmoid(x):
    return 0.5 * jnp.tanh(0.5 * x) + 0.5


def _softplus(x):
    return jnp.maximum(x, 0.0) + jnp.log(1.0 + jnp.exp(-jnp.abs(x)))


def _inproj(hn, w_pad_t, w_out_bf, conv_w):
    T, D = hn.shape
    P = w_pad_t.shape[0]
    tm = min(1024, T)
    tn = 1024 if P % 1024 == 0 else 512
    ni, nj = T // tm, P // tn
    n_sem = 2 * (N_CHIPS - 1)

    def body(hn_ref, w_ref, wout, cw, proj_ref, gout, gcw, send_sems, recv_sems):
        def gather():
            x, y, c, chips = _place()
            me = 2 * x + y
            return [pltpu.make_async_remote_copy(
                src_ref=src, dst_ref=dst.at[me], send_sem=send_sems.at[2 * j + m], recv_sem=recv_sems.at[2 * j + m],
                device_id=(px, py, c), device_id_type=MESH)
                for j, (px, py) in enumerate(chips) for m, (src, dst) in enumerate(((wout, gout), (cw, gcw)))]

        @pl.when((pl.program_id(0) == 0) & (pl.program_id(1) == 0))
        def _():
            for cp in gather():
                cp.start()

        @pl.when((pl.program_id(0) == ni - 1) & (pl.program_id(1) == nj - 1))
        def _():
            for cp in gather():
                cp.wait()

        proj_ref[...] = _dot_nt(hn_ref[...], w_ref[...])

    return pl.pallas_call(
        body,
        name="inproj",
        grid=(T // tm, P // tn),
        in_specs=[
            pl.BlockSpec((tm, D), lambda i, j: (i, 0)),
            pl.BlockSpec((tn, D), lambda i, j: (j, 0)),
            ANY, ANY,
        ],
        out_specs=[
            pl.BlockSpec((tm, tn), lambda i, j: (i, j)),
            ANY, ANY,
        ],
        out_shape=[jax.ShapeDtypeStruct((T, P), F32),
                   jax.ShapeDtypeStruct((N_CHIPS,) + w_out_bf.shape, BF16),
                   jax.ShapeDtypeStruct((N_CHIPS,) + conv_w.shape, F32)],
        scratch_shapes=[pltpu.SemaphoreType.DMA((n_sem,)), pltpu.SemaphoreType.DMA((n_sem,))],
        compiler_params=_params(("arbitrary", "arbitrary")),
    )(hn, w_pad_t, w_out_bf, conv_w)


def _pair_ones():
    ri = ((_iota((2 * PAIR, PAIR), 0) % PAIR) >= HEAD_DIM).astype(jnp.int32)
    ci = (_iota((2 * PAIR, PAIR), 1) >= HEAD_DIM).astype(jnp.int32)
    return jnp.where(ri == ci, 1.0, 0.0).astype(BF16)


def _pair_rms(v, ones2):
    return lax.rsqrt(_split_dots([v * v], ones2)[0] * (1.0 / HEAD_DIM) + EPS)


def _pair_mean(v, ones2):
    return _split_dots([v], ones2)[0] * (1.0 / HEAD_DIM)


def _suffix_ones():
    ri = _iota((2 * BLK, 2 * BLK), 0) % BLK
    ci = _iota((2 * BLK, 2 * BLK), 1)
    return jnp.where((ci >= BLK) | (ri > ci), 1.0, 0.0).astype(BF16)


def _split_dots(xs, m2):
    his = [x.astype(BF16) for x in xs]
    los = [(x - hi.astype(F32)).astype(BF16) for x, hi in zip(xs, his)]
    return [_dot(jnp.concatenate([hi, lo], axis=1), m2) for hi, lo in zip(his, los)]


def _sb_tiles(streams, km_s, uo):
    tiles = [(s, u, h) for s, st in enumerate(streams) for u in range(len(st["kbs"])) for h in range(2)]
    z2s = {(s, u): _dot_nt(st["q"], km_s[kb]) for s, st in enumerate(streams) for u, kb in enumerate(st["kbs"])}
    zs = [z2s[s, u][:, h * BLK:(h + 1) * BLK] for s, u, h in tiles]
    es = [jnp.exp(-jnp.abs(z)) for z in zs]
    las = [jnp.minimum(z, 0.0) - jnp.log(1.0 + e) for z, e in zip(zs, es)]
    lns = [a - z for a, z in zip(las, zs)]
    masks = [streams[s]["masks"][u] for s, u, h in tiles]
    lks = [lk if m is None else jnp.where(m, lk, 0.0) for m, lk in zip(masks, lns)]
    css = _split_dots(lks, uo)
    rests = [list(st["rest"]) for st in streams]
    ws = []
    for (s, u, h), m, a, cs in zip(tiles, masks, las, css):
        w = jnp.exp(a + rests[s][h] + cs[:, :BLK])
        ws.append(w if m is None else jnp.where(m, w, 0.0))
        rests[s][h] = rests[s][h] + cs[:, BLK:]
    return tiles, las, lns, ws, rests


def _stream(q_pair, qi, n_left, diag, zero):
    return dict(q=q_pair, kbs=[qi - u for u in range(n_left + 1)], masks=[diag] + [None] * n_left, rest=[zero, zero])


def _row0(block):
    return block * BLK if isinstance(block, int) else pl.multiple_of(block * BLK, BLK)


def _pair_of(vals, tiles, s, u):
    return [v for v, t in zip(vals, tiles) if t[0] == s and t[1] == u]


def _block_groups(nq, together):
    n_tog = math.gcd(together, nq)
    assert n_tog >= FIRST_LEFT
    return n_tog, list(range(n_tog)), nq // n_tog


def _attn_prep(src_ref, w_ref, dst_s, n_blocks, scale):
    per = math.gcd(PREP_BLOCKS, n_blocks)
    rows = per * BLK
    lo = _iota((rows, PAIR), 1) < HEAD_DIM
    ones2 = _pair_ones()

    def step(i, carry):
        r0 = pl.multiple_of(i * rows, rows)
        v = src_ref[0, pl.ds(r0, rows), :]
        if w_ref is not None:
            v = v * _pair_rms(v, ones2) * w_ref[...]
        if scale != 1.0:
            v = v * scale
        v0, v1 = jnp.where(lo, v, 0.0).astype(BF16), jnp.where(lo, 0.0, v).astype(BF16)
        for b in range(per):
            dst_s[i * per + b, 0:BLK, :] = v0[b * BLK:(b + 1) * BLK]
            dst_s[i * per + b, BLK:2 * BLK, :] = v1[b * BLK:(b + 1) * BLK]
        return carry

    lax.fori_loop(0, n_blocks // per, step, 0)


def _attn_fwd(proj3, qw2, kw2, D):
    Bl, L, _ = proj3.shape
    n_pair = D // PAIR
    nq = L // BLK
    scale = 1.0 / math.sqrt(HEAD_DIM)

    def body(q_ref, k_ref, v_ref, qw_ref, kw_ref, o_ref, qm_s, km_s, vm_s):
        uo = _suffix_ones()
        diag = _iota((BLK, BLK), 1) < _iota((BLK, BLK), 0)
        _attn_prep(q_ref, qw_ref, qm_s, nq, scale)
        _attn_prep(k_ref, kw_ref, km_s, nq, 1.0)
        _attn_prep(v_ref, None, vm_s, nq, 1.0)

        zero_c = jnp.zeros((BLK, BLK), F32)

        def q_of(qi):
            return qm_s[qi, 0:BLK, :] + qm_s[qi, BLK:2 * BLK, :]

        def values(streams, accs):
            tiles, _, _, ws, rests = _sb_tiles(streams, km_s, uo)
            wbs = [w.astype(BF16) for w in ws]
            accs = list(accs)
            for s, st in enumerate(streams):
                for u, kb in enumerate(st["kbs"]):
                    accs[s] = accs[s] + _dot(jnp.concatenate(_pair_of(wbs, tiles, s, u), axis=1), vm_s[kb])
            return accs, rests

        def group(qis, n_lefts):
            streams = [_stream(q_of(qi), qi, n, diag, zero_c) for qi, n in zip(qis, n_lefts)]
            accs, rests = values(streams, [jnp.zeros((BLK, PAIR), F32)] * len(qis))
            for qi, n, q, acc, rc in zip(qis, n_lefts, [st["q"] for st in streams], accs, rests):

                def sweep(state, n_blocks, q=q):
                    kb, rc0, rc1, acc1, _ = state
                    st = dict(q=q, kbs=[kb - u for u in range(n_blocks)], masks=[None] * n_blocks, rest=[rc0, rc1])
                    (acc1,), (r,) = values([st], [acc1])
                    return kb - n_blocks, r[0], r[1], acc1, jnp.maximum(jnp.max(r[0]), jnp.max(r[1]))

                state = (jnp.asarray(qi - n - 1, jnp.int32), rc[0], rc[1], acc, jnp.maximum(jnp.max(rc[0]), jnp.max(rc[1])))
                state = lax.while_loop(lambda t: (t[0] >= 1) & (t[4] >= UNDERFLOW), lambda t: sweep(t, 2), state)
                state = lax.while_loop(lambda t: (t[0] >= 0) & (t[4] >= UNDERFLOW), lambda t: sweep(t, 1), state)
                o_ref[0, pl.ds(_row0(qi), BLK), :] = state[3]

        n_tog, head, n_groups = _block_groups(nq, Q_TOGETHER_FWD)
        group(head, [min(qi, FIRST_LEFT) for qi in head])

        def groups(g, carry):
            group([g * n_tog + j for j in range(n_tog)], [FIRST_LEFT] * n_tog)
            return carry

        lax.fori_loop(1, n_groups, groups, 0)

    blk = lambda off: pl.BlockSpec((1, L, PAIR), lambda b, p: (b, 0, off + p))
    wspec = pl.BlockSpec((1, PAIR), lambda b, p: (0, 0))
    return pl.pallas_call(
        body,
        name="sb_attn_fwd",
        grid=(Bl, n_pair),
        in_specs=[blk(0), blk(n_pair), blk(2 * n_pair), wspec, wspec],
        out_specs=pl.BlockSpec((1, L, PAIR), lambda b, p: (b, 0, p)),
        out_shape=jax.ShapeDtypeStruct((Bl, L, D), F32),
        scratch_shapes=[pltpu.VMEM((nq, 2 * BLK, PAIR), BF16)] * 3,
        compiler_params=_params(("parallel", "parallel")),
    )(proj3, proj3, proj3, qw2, kw2)


def _attn_bwd(proj3, o3, do3, qw2, kw2, D):
    Bl, L, _ = proj3.shape
    n_pair = D // PAIR
    nq = L // BLK
    scale = 1.0 / math.sqrt(HEAD_DIM)

    def body(q_ref, k_ref, v_ref, o_ref, do_ref, qw_ref, kw_ref, dq_ref, dk_ref, dv_ref, dw_ref,
             qm_s, km_s, vm_s, dom_s, dq_s, dk_s, dv_s):
        uo = _suffix_ones()
        diag = _iota((BLK, BLK), 1) < _iota((BLK, BLK), 0)
        ones2 = _pair_ones()
        _attn_prep(q_ref, qw_ref, qm_s, nq, scale)
        _attn_prep(k_ref, kw_ref, km_s, nq, 1.0)
        _attn_prep(v_ref, None, vm_s, nq, 1.0)
        _attn_prep(do_ref, None, dom_s, nq, 1.0)

        @pl.when((pl.program_id(0) == 0) & (pl.program_id(1) == 0))
        def _():
            dw_ref[...] = jnp.zeros_like(dw_ref)

        def zero(i, carry):
            r0 = pl.multiple_of(i * BLK, BLK)
            dk_s[pl.ds(r0, BLK), :] = jnp.zeros((BLK, PAIR), F32)
            dv_s[pl.ds(r0, BLK), :] = jnp.zeros((BLK, PAIR), F32)
            return carry

        lax.fori_loop(0, nq, zero, 0)

        zero_c = jnp.zeros((BLK, BLK), F32)

        def tiles_bwd(streams, dqas):
            tiles, las, lns, ws, rests = _sb_tiles(streams, km_s, uo)
            dw2s = {(s, u): _dot_nt(st["do"], vm_s[kb]) for s, st in enumerate(streams) for u, kb in enumerate(st["kbs"])}
            dws = [dw2s[s, u][:, h * BLK:(h + 1) * BLK] for s, u, h in tiles]
            wfs = [w.astype(BF16).astype(F32) for w in ws]
            gs = [wf * dw for wf, dw in zip(wfs, dws)]
            gss = _split_dots(gs, uo)
            gcs = [list(st["g_rest"]) for st in streams]
            dzs = []
            for (s, u, h), a, ln, g, gsum in zip(tiles, las, lns, gs, gss):
                g_before = streams[s]["delta"][h] - (gcs[s][h] + gsum[:, :BLK] + g)
                gcs[s][h] = gcs[s][h] + gsum[:, BLK:]
                dz = g * jnp.exp(ln) - g_before * jnp.exp(a)
                m = streams[s]["masks"][u]
                dzs.append(dz if m is None else jnp.where(m, dz, 0.0))
            wts = [wf.T.astype(BF16) for wf in wfs]
            dzts = [dz.T.astype(BF16) for dz in dzs]
            dzbs = [dz.astype(BF16) for dz in dzs]
            dqas = list(dqas)
            for s, st in enumerate(streams):
                for u, kb in enumerate(st["kbs"]):
                    c0 = _row0(kb)
                    dv_s[pl.ds(c0, BLK), :] += _dot(jnp.concatenate(_pair_of(wts, tiles, s, u), axis=1), dom_s[st["qi"]])
                    dk_s[pl.ds(c0, BLK), :] += _dot(jnp.concatenate(_pair_of(dzts, tiles, s, u), axis=1), qm_s[st["qi"]])
                    dqas[s] = dqas[s] + _dot(jnp.concatenate(_pair_of(dzbs, tiles, s, u), axis=1), km_s[kb])
            return dqas, rests, gcs

        def group(qis, n_lefts):
            streams = []
            for qi, n in zip(qis, n_lefts):
                o_blk = o_ref[0, pl.ds(_row0(qi), BLK), :]
                doms = [dom_s[qi, 0:BLK, :], dom_s[qi, BLK:2 * BLK, :]]
                st = _stream(qm_s[qi, 0:BLK, :] + qm_s[qi, BLK:2 * BLK, :], qi, n, diag, zero_c)
                st.update(qi=qi, do=doms[0] + doms[1], delta=[_rowsum(d.astype(F32) * o_blk) for d in doms],
                          g_rest=[zero_c, zero_c])
                streams.append(st)
            dqas, rests, gcs = tiles_bwd(streams, [jnp.zeros((BLK, PAIR), F32)] * len(qis))
            for qi, n, st0, dqa, rc, gc in zip(qis, n_lefts, streams, dqas, rests, gcs):

                def sweep(state, n_blocks, st0=st0):
                    kb, rc0, rc1, gc0, gc1, dqa1, _ = state
                    st = dict(st0, kbs=[kb - u for u in range(n_blocks)], masks=[None] * n_blocks, rest=[rc0, rc1],
                              g_rest=[gc0, gc1])
                    (dqa1,), (r,), (g,) = tiles_bwd([st], [dqa1])
                    return kb - n_blocks, r[0], r[1], g[0], g[1], dqa1, jnp.maximum(jnp.max(r[0]), jnp.max(r[1]))

                state = (jnp.asarray(qi - n - 1, jnp.int32), rc[0], rc[1], gc[0], gc[1], dqa,
                         jnp.maximum(jnp.max(rc[0]), jnp.max(rc[1])))
                state = lax.while_loop(lambda t: (t[0] >= 1) & (t[6] >= UNDERFLOW), lambda t: sweep(t, 2), state)
                state = lax.while_loop(lambda t: (t[0] >= 0) & (t[6] >= UNDERFLOW), lambda t: sweep(t, 1), state)
                dq_s[pl.ds(_row0(qi), BLK), :] = state[5] * scale

        n_tog, head, n_groups = _block_groups(nq, Q_TOGETHER_BWD)
        group(head, [min(qi, FIRST_LEFT) for qi in head])

        def groups(g, carry):
            group([g * n_tog + j for j in range(n_tog)], [FIRST_LEFT] * n_tog)
            return carry

        lax.fori_loop(1, n_groups, groups, 0)

        per = math.gcd(PREP_BLOCKS, nq)
        rows = per * BLK

        def finish(i, carry):
            r0 = pl.multiple_of(i * rows, rows)
            dwq, dwk = carry
            out = []
            for src_ref, w_ref, d_s in ((q_ref, qw_ref, dq_s), (k_ref, kw_ref, dk_s)):
                v = src_ref[0, pl.ds(r0, rows), :]
                r = _pair_rms(v, ones2)
                vh = v * r
                dy = d_s[pl.ds(r0, rows), :]
                dvh = dy * w_ref[...]
                out.append((r * (dvh - vh * _pair_mean(dvh * vh, ones2)), _colsum(dy * vh)))
            dq_ref[0, pl.ds(r0, rows), :] = out[0][0].astype(BF16)
            dk_ref[0, pl.ds(r0, rows), :] = out[1][0].astype(BF16)
            dv_ref[0, pl.ds(r0, rows), :] = dv_s[pl.ds(r0, rows), :].astype(BF16)
            return dwq + out[0][1], dwk + out[1][1]

        zrow = jnp.zeros((1, PAIR), F32)
        dwq, dwk = lax.fori_loop(0, nq // per, finish, (zrow, zrow))
        dw_ref[0:1, :] += dwq
        dw_ref[1:2, :] += dwk

    blk = lambda off: pl.BlockSpec((1, L, PAIR), lambda b, p: (b, 0, off + p))
    wspec = pl.BlockSpec((1, PAIR), lambda b, p: (0, 0))
    oblk = pl.BlockSpec((1, L, PAIR), lambda b, p: (b, 0, p))
    return pl.pallas_call(
        body,
        name="sb_attn_bwd",
        grid=(Bl, n_pair),
        in_specs=[blk(0), blk(n_pair), blk(2 * n_pair), oblk, oblk, wspec, wspec],
        out_specs=[oblk, oblk, oblk, pl.BlockSpec((8, PAIR), lambda b, p: (0, 0))],
        out_shape=[jax.ShapeDtypeStruct((Bl, L, D), BF16)] * 3 + [jax.ShapeDtypeStruct((8, PAIR), F32)],
        scratch_shapes=[pltpu.VMEM((nq, 2 * BLK, PAIR), BF16)] * 4 + [pltpu.VMEM((L, PAIR), F32)] * 3,
        compiler_params=_params(("arbitrary", "arbitrary")),
    )(proj3, proj3, proj3, o3, do3, qw2, kw2)


def _conv_pre(ext_s, halo_ref, raw_ref, w_ref, b_ref, first):
    ext_s[0:HALO, :] = jnp.where(first, 0.0, halo_ref[0])
    ext_s[HALO:HALO + BLK, :] = raw_ref[0]
    pre = b_ref[...]
    for i in range(CONV_K):
        pre = pre + ext_s[pl.ds(HALO - (CONV_K - 1 - i), BLK), :] * w_ref[i:i + 1, :]
    return pre


def _lane_col(m, lane, h):
    return _rowsum(jnp.where(lane == h, m, 0.0))


def _half_sums(row, lo1):
    return _rowsum(jnp.where(lo1, row, 0.0)), _rowsum(jnp.where(lo1, 0.0, row))


def _ssd_specs(Bl, L, D, rev):
    nc = L // BLK
    rows_per = BLK // HALO
    cidx = (lambda c: nc - 1 - c) if rev else (lambda c: c)
    xoff = 5
    boff = (6 * D) // 512
    doff = (6 * D + 512) // LANES
    prev = lambda c: jnp.maximum(cidx(c) * rows_per - 1, 0)
    specs = [
        pl.BlockSpec((1, BLK, D), lambda b, c: (b, cidx(c), xoff)),
        pl.BlockSpec((1, BLK, 512), lambda b, c: (b, cidx(c), boff)),
        pl.BlockSpec((1, HALO, D), lambda b, c: (b, prev(c), xoff)),
        pl.BlockSpec((1, HALO, 512), lambda b, c: (b, prev(c), boff)),
        pl.BlockSpec((1, BLK, LANES), lambda b, c: (b, cidx(c), doff)),
    ]
    full = lambda shape: pl.BlockSpec(shape, lambda b, c: (0,) * len(shape))
    specs += [full((CONV_K, D)), full((CONV_K, 512)), full((1, D)), full((1, 512)),
              full((1, LANES)), full((1, LANES)), full((1, LANES))]
    return specs, cidx


def _ssd_common(dtr_ref, dtb_ref, alog_ref, acs_s, acsT_s):
    ltri = jnp.where(_iota((BLK, BLK), 1) <= _iota((BLK, BLK), 0), 1.0, 0.0).astype(BF16)
    dtv = _softplus(dtr_ref[0] + dtb_ref[...])
    a = -jnp.exp(alog_ref[...])
    acs = _dot_split(ltri, dtv * a)
    acs_s[...] = acs
    acsT_s[...] = acs.T
    return dtv, a, acs


def _pair_terms(pr, acs, dtv, acs_s, lane, lo, lane1, lo1):
    h0, h1 = 2 * pr, 2 * pr + 1
    c0, c1 = _lane_col(acs, lane, h0), _lane_col(acs, lane, h1)
    d0, d1 = _lane_col(dtv, lane, h0), _lane_col(dtv, lane, h1)
    lastv = acs_s[BLK - 1:BLK, :]
    l0, l1 = _lane_col(lastv, lane1, h0), _lane_col(lastv, lane1, h1)
    return dict(h=(h0, h1), c=(c0, c1), last=(l0, l1), acs_p=jnp.where(lo, c0, c1), dt_p=jnp.where(lo, d0, d1),
                last_p=jnp.where(lo1, l0, l1))


def _decay_tiles(cc, row, tri, want_t):
    lm = jnp.where(tri, jnp.exp(cc - row), 0.0)
    return lm, (lm.T if want_t else None)


def _ssd_fwd(proj3, cwx, cwb, cbx, cbb, dtb, alog, dsk, D):
    Bl, L, _ = proj3.shape
    nc = L // BLK
    n_pair = D // PAIR
    pairs_per_group = n_pair // SSD_GROUPS
    specs, _ = _ssd_specs(Bl, L, D, False)

    def body(xr_ref, bcr_ref, xh_ref, bch_ref, dtr_ref, cwx_ref, cwb_ref, cbx_ref, cbb_ref, dtb_ref, alog_ref,
             dsk_ref, y_ref, sin_ref, st_s, extx_s, extb_s, acs_s, acsT_s):
        first = pl.program_id(1) == 0

        @pl.when(first)
        def _():
            st_s[...] = jnp.zeros_like(st_s)

        lane, lane1 = _iota((BLK, LANES), 1), _iota((1, LANES), 1)
        lo, lo1 = lane < HEAD_DIM, lane1 < HEAD_DIM
        tri = _iota((BLK, BLK), 1) <= _iota((BLK, BLK), 0)
        pre = _conv_pre(extx_s, xh_ref, xr_ref, cwx_ref, cbx_ref, first)
        ux = pre * _sigmoid(pre)
        pre = _conv_pre(extb_s, bch_ref, bcr_ref, cwb_ref, cbb_ref, first)
        ub = pre * _sigmoid(pre)
        dtv, a, acs = _ssd_common(dtr_ref, dtb_ref, alog_ref, acs_s, acsT_s)
        for g in range(SSD_GROUPS):
            bg = ub[:, g * SSD_STATE:(g + 1) * SSD_STATE]
            cb_ = ub[:, (SSD_GROUPS + g) * SSD_STATE:(SSD_GROUPS + g + 1) * SSD_STATE].astype(BF16)
            cbm = _dot_nt(cb_, bg.astype(BF16))
            btb = bg.T.astype(BF16)
            for pr in range(g * pairs_per_group, (g + 1) * pairs_per_group):
                t = _pair_terms(pr, acs, dtv, acs_s, lane, lo, lane1, lo1)
                xs_p = ux[:, pr * PAIR:(pr + 1) * PAIR]
                x_p = xs_p * t["dt_p"]
                st = st_s[pr]
                sin_ref[0, 0, pr] = st
                y = _dot(cb_, st.astype(BF16)) * jnp.exp(t["acs_p"])
                for k in range(2):
                    row = acsT_s[t["h"][k]:t["h"][k] + 1, :]
                    lm, _ = _decay_tiles(t["c"][k], row, tri, False)
                    xm = jnp.where(lo if k == 0 else ~lo, x_p, 0.0).astype(BF16)
                    y = y + _dot((cbm * lm).astype(BF16), xm)
                d_p = jnp.where(lo1, _lane_col(dsk_ref[...], lane1, t["h"][0]), _lane_col(dsk_ref[...], lane1, t["h"][1]))
                y_ref[0, :, pr * PAIR:(pr + 1) * PAIR] = y + d_p * xs_p
                xd = (x_p * jnp.exp(t["last_p"] - t["acs_p"])).astype(BF16)
                st_s[pr] = st * jnp.exp(t["last_p"]) + _dot(btb, xd)

    return pl.pallas_call(
        body,
        name="ssd_fwd",
        grid=(Bl, nc),
        in_specs=specs,
        out_specs=[
            pl.BlockSpec((1, BLK, D), lambda b, c: (b, c, 0)),
            pl.BlockSpec((1, 1, n_pair, SSD_STATE, PAIR), lambda b, c: (b, c, 0, 0, 0)),
        ],
        out_shape=[jax.ShapeDtypeStruct((Bl, L, D), F32),
                   jax.ShapeDtypeStruct((Bl, nc, n_pair, SSD_STATE, PAIR), F32)],
        scratch_shapes=[pltpu.VMEM((n_pair, SSD_STATE, PAIR), F32), pltpu.VMEM((HALO + BLK, D), F32),
                        pltpu.VMEM((HALO + BLK, 512), F32), pltpu.VMEM((BLK, LANES), F32),
                        pltpu.VMEM((LANES, BLK), F32)],
        compiler_params=_params(("arbitrary", "arbitrary")),
    )(proj3, proj3, proj3, proj3, proj3, cwx, cwb, cbx, cbb, dtb, alog, dsk)


def _ssd_bwd(proj3, s_in, dy3, cwx, cwb, cbx, cbb, dtb, alog, dsk, D, tail):
    Bl, L, _ = proj3.shape
    CD = D + 512
    nc = L // BLK
    n_pair = D // PAIR
    n_heads = 2 * n_pair
    pairs_per_group = n_pair // SSD_GROUPS
    specs, cidx = _ssd_specs(Bl, L, D, True)
    specs = specs + [
        pl.BlockSpec((1, 1, n_pair, SSD_STATE, PAIR), lambda b, c: (b, cidx(c), 0, 0, 0)),
        pl.BlockSpec((1, BLK, D), lambda b, c: (b, cidx(c), 0)),
    ]

    def body(xr_ref, bcr_ref, xh_ref, bch_ref, dtr_ref, cwx_ref, cwb_ref, cbx_ref, cbb_ref, dtb_ref, alog_ref,
             dsk_ref, sin_ref, dy_ref, dxbc_ref, dcwx_ref, dcwb_ref, dcbx_ref, dcbb_ref, misc_ref,
             dst_s, extx_s, extb_s, acs_s, acsT_s, dux_s, dub_s, e2x_s, e2b_s, nxx_s, nxb_s):
        step = pl.program_id(1)
        first = step == nc - 1
        last = step == 0

        @pl.when(last)
        def _():
            dst_s[...] = jnp.zeros_like(dst_s)
            nxx_s[...] = jnp.zeros_like(nxx_s)
            nxb_s[...] = jnp.zeros_like(nxb_s)

        @pl.when(last & (pl.program_id(0) == 0))
        def _():
            for r in (dcwx_ref, dcwb_ref, dcbx_ref, dcbb_ref, misc_ref):
                r[...] = jnp.zeros_like(r)

        lane, lane1 = _iota((BLK, LANES), 1), _iota((1, LANES), 1)
        lo, lo1 = lane < HEAD_DIM, lane1 < HEAD_DIM
        tri = _iota((BLK, BLK), 1) <= _iota((BLK, BLK), 0)
        prex = _conv_pre(extx_s, xh_ref, xr_ref, cwx_ref, cbx_ref, first)
        sgx = _sigmoid(prex)
        ux = prex * sgx
        preb = _conv_pre(extb_s, bch_ref, bcr_ref, cwb_ref, cbb_ref, first)
        sgb = _sigmoid(preb)
        ub = preb * sgb
        dtv, a, acs = _ssd_common(dtr_ref, dtb_ref, alog_ref, acs_s, acsT_s)
        dacs = jnp.zeros((BLK, LANES), F32)
        dlast = jnp.zeros((1, LANES), F32)
        ddt = jnp.zeros((BLK, LANES), F32)
        dd = jnp.zeros((1, LANES), F32)
        for g in range(SSD_GROUPS):
            bg = ub[:, g * SSD_STATE:(g + 1) * SSD_STATE]
            cg = ub[:, (SSD_GROUPS + g) * SSD_STATE:(SSD_GROUPS + g + 1) * SSD_STATE]
            bb, cb_ = bg.astype(BF16), cg.astype(BF16)
            cbm = _dot_nt(cb_, bb)
            cbt = _dot_nt(bb, cb_)
            ctb = cg.T.astype(BF16)
            dbg = jnp.zeros((BLK, SSD_STATE), F32)
            dcg = jnp.zeros((BLK, SSD_STATE), F32)
            for pr in range(g * pairs_per_group, (g + 1) * pairs_per_group):
                t = _pair_terms(pr, acs, dtv, acs_s, lane, lo, lane1, lo1)
                h0, h1 = t["h"]
                xs_p = ux[:, pr * PAIR:(pr + 1) * PAIR]
                dy_p = dy_ref[0, :, pr * PAIR:(pr + 1) * PAIR]
                x_p = xs_p * t["dt_p"]
                ea_p = jnp.exp(t["acs_p"])
                dte_p = jnp.exp(t["last_p"] - t["acs_p"])
                cd_p = jnp.exp(t["last_p"])
                st = sin_ref[0, 0, pr]
                dst = dst_s[pr]
                stb, dstb = st.astype(BF16), dst.astype(BF16)
                s0, s1 = _half_sums(_colsum(dy_p * xs_p), lo1)
                dd = dd + jnp.where(lane1 == h0, s0, 0.0) + jnp.where(lane1 == h1, s1, 0.0)
                d_p = jnp.where(lo1, _lane_col(dsk_ref[...], lane1, h0), _lane_col(dsk_ref[...], lane1, h1))
                dxs_p = d_p * dy_p
                dp = dy_p * ea_p
                dpb = dp.astype(BF16)
                yo = dp * _dot(cb_, stb)
                dcg = dcg + _dot_nt(dpb, stb)
                dst_off = _dot(ctb, dpb)
                dac = [_rowsum(jnp.where(lo, yo, 0.0)), _rowsum(jnp.where(lo, 0.0, yo))]
                s0, s1 = _half_sums(_colsum(dst * st), lo1)
                dl = [s0 * jnp.exp(t["last"][0]), s1 * jnp.exp(t["last"][1])]
                dxd = _dot(bb, dstb)
                dx_p = dxd * dte_p
                tt = dxd * x_p
                dbg = dbg + _dot_nt((x_p * dte_p).astype(BF16), dstb)
                for k, ddte in enumerate((_rowsum(jnp.where(lo, tt, 0.0)), _rowsum(jnp.where(lo, 0.0, tt)))):
                    ek = ddte * jnp.exp(t["last"][k] - t["c"][k])
                    dl[k] = dl[k] + _colsum(ek)
                    dac[k] = dac[k] - ek
                x_pb = x_p.astype(BF16)
                for k in range(2):
                    row = acsT_s[t["h"][k]:t["h"][k] + 1, :]
                    lm, lmt = _decay_tiles(t["c"][k], row, tri, True)
                    dym = jnp.where(lo if k == 0 else ~lo, dy_p, 0.0).astype(BF16)
                    dm = _dot_nt(dym, x_pb)
                    dmt = _dot_nt(x_pb, dym)
                    mt = cbt * lmt
                    dx_p = dx_p + _dot(mt.astype(BF16), dym)
                    dac[k] = dac[k] + _rowsum(dm * (cbm * lm)) - _rowsum(dmt * mt)
                    dcg = dcg + _dot((dm * lm).astype(BF16), bb)
                    dbg = dbg + _dot((dmt * lmt).astype(BF16), cb_)
                dacs = dacs + jnp.where(lane == h0, dac[0], 0.0) + jnp.where(lane == h1, dac[1], 0.0)
                dlast = dlast + jnp.where(lane1 == h0, dl[0], 0.0) + jnp.where(lane1 == h1, dl[1], 0.0)
                dxs_p = dxs_p + dx_p * t["dt_p"]
                t3 = dx_p * xs_p
                ddt = ddt + jnp.where(lane == h0, _rowsum(jnp.where(lo, t3, 0.0)), 0.0) \
                    + jnp.where(lane == h1, _rowsum(jnp.where(lo, 0.0, t3)), 0.0)
                dux_s[:, pr * PAIR:(pr + 1) * PAIR] = dxs_p
                dst_s[pr] = dst * cd_p + dst_off
            dub_s[:, g * SSD_STATE:(g + 1) * SSD_STATE] = dbg
            dub_s[:, (SSD_GROUPS + g) * SSD_STATE:(SSD_GROUPS + g + 1) * SSD_STATE] = dcg
        dacs = dacs + jnp.where(_iota((BLK, LANES), 0) == BLK - 1, dlast, 0.0)
        utri = jnp.where(_iota((BLK, BLK), 1) >= _iota((BLK, BLK), 0), 1.0, 0.0).astype(BF16)
        dda = _dot_split(utri, dacs)
        ddt = ddt + dda * a
        ddtr = jnp.where(lane < n_heads, ddt * _sigmoid(dtr_ref[0] + dtb_ref[...]), 0.0)
        dxbc_ref[0, :, CD:CD + LANES] = ddtr.astype(BF16)
        dxbc_ref[0, :, CD + LANES:tail] = jnp.zeros((BLK, tail - CD - LANES), BF16)
        misc_ref[0:1, :] += _colsum(ddtr)
        misc_ref[1:2, :] += jnp.where(lane1 < n_heads, _colsum(dda * dtv) * a, 0.0)
        misc_ref[2:3, :] += dd
        for (du_s, pre, sg, ext_s, e2_s, nx_s, w_ref, dcw_ref, dcb_ref, c0, width) in (
                (dux_s, prex, sgx, extx_s, e2x_s, nxx_s, cwx_ref, dcwx_ref, dcbx_ref, 0, D),
                (dub_s, preb, sgb, extb_s, e2b_s, nxb_s, cwb_ref, dcwb_ref, dcbb_ref, D, 512)):
            dpre = du_s[...] * (sg * (1.0 + pre * (1.0 - sg)))
            dcb_ref[...] += _colsum(dpre)
            for i in range(CONV_K):
                dcw_ref[i:i + 1, :] += _colsum(dpre * ext_s[pl.ds(HALO - (CONV_K - 1 - i), BLK), :])
            e2_s[0:BLK, :] = dpre
            e2_s[BLK:BLK + HALO, :] = nx_s[...]
            dxr = jnp.zeros((BLK, width), F32)
            for i in range(CONV_K):
                dxr = dxr + e2_s[pl.ds(CONV_K - 1 - i, BLK), :] * w_ref[i:i + 1, :]
            dxbc_ref[0, :, c0:c0 + width] = dxr.astype(BF16)
            nx_s[...] = e2_s[0:HALO, :]

    full = lambda shape: pl.BlockSpec(shape, lambda b, c: (0,) * len(shape))
    return pl.pallas_call(
        body,
        name="ssd_bwd",
        grid=(Bl, nc),
        in_specs=specs,
        out_specs=[
            pl.BlockSpec((1, BLK, tail), lambda b, c: (b, cidx(c), 0)),
            full((CONV_K, D)), full((CONV_K, 512)), full((1, D)), full((1, 512)), full((8, LANES)),
        ],
        out_shape=[
            jax.ShapeDtypeStruct((Bl, L, tail), BF16),
            jax.ShapeDtypeStruct((CONV_K, D), F32), jax.ShapeDtypeStruct((CONV_K, 512), F32),
            jax.ShapeDtypeStruct((1, D), F32), jax.ShapeDtypeStruct((1, 512), F32),
            jax.ShapeDtypeStruct((8, LANES), F32),
        ],
        scratch_shapes=[
            pltpu.VMEM((n_pair, SSD_STATE, PAIR), F32),
            pltpu.VMEM((HALO + BLK, D), F32), pltpu.VMEM((HALO + BLK, 512), F32),
            pltpu.VMEM((BLK, LANES), F32), pltpu.VMEM((LANES, BLK), F32),
            pltpu.VMEM((BLK, D), F32), pltpu.VMEM((BLK, 512), F32),
            pltpu.VMEM((BLK + HALO, D), F32), pltpu.VMEM((BLK + HALO, 512), F32),
            pltpu.VMEM((HALO, D), F32), pltpu.VMEM((HALO, 512), F32),
        ],
        compiler_params=_params(("arbitrary", "arbitrary")),
    )(proj3, proj3, proj3, proj3, proj3, cwx, cwb, cbx, cbb, dtb, alog, dsk, s_in, dy3)


def _gate_out(x2, tgt2, o2, proj2, y2, sbw, ssw, w_out_bf, w_out_t):
    T, D = x2.shape
    tm = min(256, T)

    def body(x_ref, t_ref, o_ref, zs_ref, y_ref, zy_ref, sbw_ref, ssw_ref, wo_ref, wot_ref,
             dout_ref, doutb_ref, mixt_ref, do_ref, dy_ref, dz_ref, dnw_ref, loss_ref):
        @pl.when(pl.program_id(0) == 0)
        def _():
            dnw_ref[...] = jnp.zeros_like(dnw_ref)
            loss_ref[...] = jnp.zeros_like(loss_ref)

        def fwd(o, z, w):
            sg = _sigmoid(z)
            sl = z * sg
            g = o * sl
            r = lax.rsqrt(jnp.mean(g * g, axis=-1, keepdims=True) + EPS)
            n = g * r
            return sg, sl, r, n, n * w

        def bwd(dy, o, z, w, sg, sl, r, n):
            dn = dy * w
            dg = r * (dn - n * jnp.mean(dn * n, axis=-1, keepdims=True))
            return dg * sl, dg * o * (sg * (1.0 + z * (1.0 - sg))), _colsum(dy * n)

        o1, z1, w1 = o_ref[...], zs_ref[...], sbw_ref[...]
        o2_, z2, w2 = y_ref[...], zy_ref[...], ssw_ref[...]
        sg1, sl1, r1, n1, y1 = fwd(o1, z1, w1)
        sg2, sl2, r2, n2, y2_ = fwd(o2_, z2, w2)
        y1b, y2b = y1.astype(BF16), y2_.astype(BF16)
        mixt_ref[0:D, :] = y1.T.astype(BF16)
        mixt_ref[D:2 * D, :] = y2_.T.astype(BF16)
        out = x_ref[...] + (_dot(y1b, wo_ref[0:D, :]) + _dot(y2b, wo_ref[D:2 * D, :]))
        err = out - t_ref[...]
        loss_ref[...] += jnp.sum(err * err) * (0.5 / D)
        dout = err * (1.0 / D)
        dout_ref[...] = dout
        doutb = dout.astype(BF16)
        doutb_ref[...] = doutb
        do1, dz1, dw1 = bwd(_dot(doutb, wot_ref[:, 0:D]), o1, z1, w1, sg1, sl1, r1, n1)
        do2, dz2, dw2 = bwd(_dot(doutb, wot_ref[:, D:2 * D]), o2_, z2, w2, sg2, sl2, r2, n2)
        do_ref[...] = do1
        dy_ref[...] = do2
        dz_ref[:, 0:D] = dz1.astype(BF16)
        dz_ref[:, D:2 * D] = dz2.astype(BF16)
        dnw_ref[0:1, :] += dw1
        dnw_ref[1:2, :] += dw2

    row = lambda col: pl.BlockSpec((tm, D), lambda i: (i, col))
    full = lambda shape: pl.BlockSpec(shape, lambda i: (0,) * len(shape))
    wide = pl.BlockSpec((tm, 2 * D), lambda i: (i, 0))
    return pl.pallas_call(
        body,
        name="gate_out",
        grid=(T // tm,),
        in_specs=[row(0), row(0), row(0), row(3), row(0), row(4), full((1, D)), full((1, D)), full((2 * D, D)),
                  full((D, 2 * D))],
        out_specs=[row(0), row(0), pl.BlockSpec((2 * D, tm), lambda i: (0, i)), row(0), row(0), wide,
                   full((8, D)), full((8, LANES))],
        out_shape=[
            jax.ShapeDtypeStruct((T, D), F32), jax.ShapeDtypeStruct((T, D), BF16),
            jax.ShapeDtypeStruct((2 * D, T), BF16), jax.ShapeDtypeStruct((T, D), F32),
            jax.ShapeDtypeStruct((T, D), F32), jax.ShapeDtypeStruct((T, 2 * D), BF16),
            jax.ShapeDtypeStruct((8, D), F32), jax.ShapeDtypeStruct((8, LANES), F32),
        ],
        compiler_params=_params(("arbitrary",)),
    )(x2, tgt2, o2, proj2, y2, proj2, sbw, ssw, w_out_bf, w_out_t)


def _piece_blocks(pieces, D):
    counts = [p.shape[1] // D for p in pieces]
    return [sum(counts[:i]) for i in range(len(counts))], counts


def _dhn(pieces, w_pad_t, x2, dout, norm_w, h_in, h_out, hb_in, hb_out, slab_off, slab_w):
    T, D = x2.shape
    tm = min(1024, T)
    starts, counts = _piece_blocks(pieces, D)
    units = [p for p, n in enumerate(counts) for _ in range(n)]
    per = 2 if all(units[2 * k] != units[2 * k + 1] for k in range(len(units) // 2)) else 1
    nk = -(-len(units) // per)
    ni = T // tm
    n_sem = 2 * (N_CHIPS - 1)
    assert len(units) * D == w_pad_t.shape[0]

    def body(*refs):
        p_refs = refs[:len(pieces)]
        (wa_ref, wb_ref, x_hbm, dout_hbm, nw_ref, hin, hout, hbin, hbout, gx_ref, dnw_ref, rin, rout, oin, oout,
         acc_s, x_s, dout_s, send_sems, recv_sems, row_sems, own_sems) = refs[len(pieces):]
        i, k = pl.program_id(0), pl.program_id(1)

        def rows():
            r0 = pl.multiple_of(i * tm, tm)
            return [pltpu.make_async_copy(src.at[pl.ds(r0, tm)], dst, row_sems.at[n])
                    for n, (src, dst) in enumerate(((x_hbm, x_s), (dout_hbm, dout_s)))]

        @pl.when(k == 0)
        def _():
            for cp in rows():
                cp.start()

        def scatter():
            x, y, c, chips = _place()

            def slab(ref, p):
                return ref.at[:, pl.ds(pl.multiple_of(p * slab_off, LANES), slab_w)]

            cps = []
            for j, (px, py) in enumerate(chips):
                p = 2 * px + py
                for m, (src, dst) in enumerate(((slab(hbin, p), rin.at[j]), (hbout.at[p], rout.at[j]))):
                    cps.append(pltpu.make_async_remote_copy(
                        src_ref=src, dst_ref=dst, send_sem=send_sems.at[2 * j + m], recv_sem=recv_sems.at[2 * j + m],
                        device_id=(px, py, c), device_id_type=MESH))
            me = 2 * x + y
            own = [pltpu.make_async_copy(slab(hin, me), oin, own_sems.at[0]),
                   pltpu.make_async_copy(hout.at[me], oout, own_sems.at[1])]
            return cps + own

        @pl.when((i == 0) & (k == 0))
        def _():
            for cp in scatter():
                cp.start()

        @pl.when((i == ni - 1) & (k == nk - 1))
        def _():
            for cp in scatter():
                cp.wait()

        @pl.when((i == 0) & (k == 0))
        def _():
            dnw_ref[...] = jnp.zeros_like(dnw_ref)

        for step in range(nk):
            @pl.when(k == step)
            def _(step=step):
                part = sum(_dot(p_refs[units[u]][...], w[...])
                           for u, w in list(zip(range(per * step, per * step + per), (wa_ref, wb_ref))) if u < len(units))
                acc_s[...] = part if step == 0 else acc_s[...] + part

        @pl.when(k == nk - 1)
        def _():
            for cp in rows():
                cp.wait()
            xv = x_s[...]
            r = lax.rsqrt(jnp.mean(xv * xv, axis=-1, keepdims=True) + EPS)
            xh = xv * r
            dhn = acc_s[...]
            dxh = dhn * nw_ref[...]
            gx_ref[...] = dout_s[...] + r * (dxh - xh * jnp.mean(dxh * xh, axis=-1, keepdims=True))
            dnw_ref[0:1, :] += _colsum(dhn * xh)

    return pl.pallas_call(
        body,
        name="dhn",
        grid=(T // tm, nk),
        in_specs=[pl.BlockSpec((tm, D), lambda i, k, s=s, n=n: (i, jnp.clip(per * k - s + (per * k < s), 0, n - 1)))
                  for s, n in zip(starts, counts)] + [
            pl.BlockSpec((D, D), lambda i, k: (jnp.minimum(per * k, len(units) - 1), 0)),
            pl.BlockSpec((D, D), lambda i, k: (jnp.minimum(per * k + per - 1, len(units) - 1), 0)),
            ANY, ANY,
            pl.BlockSpec((1, D), lambda i, k: (0, 0)),
            ANY, ANY, ANY, ANY,
        ],
        out_specs=[pl.BlockSpec((tm, D), lambda i, k: (i, 0)), pl.BlockSpec((8, D), lambda i, k: (0, 0)),
                   ANY, ANY, ANY, ANY],
        out_shape=[jax.ShapeDtypeStruct((T, D), F32), jax.ShapeDtypeStruct((8, D), F32),
                   jax.ShapeDtypeStruct((N_CHIPS - 1, h_in.shape[0], slab_w), BF16),
                   jax.ShapeDtypeStruct((N_CHIPS - 1,) + h_out.shape[1:], BF16),
                   jax.ShapeDtypeStruct((h_in.shape[0], slab_w), F32),
                   jax.ShapeDtypeStruct(h_out.shape[1:], F32)],
        scratch_shapes=[pltpu.VMEM((tm, D), F32)] * 3 + [pltpu.SemaphoreType.DMA((n_sem,)), pltpu.SemaphoreType.DMA((n_sem,)),
                                                      pltpu.SemaphoreType.DMA((2,)), pltpu.SemaphoreType.DMA((2,))],
        compiler_params=_params(("arbitrary", "arbitrary")),
    )(*pieces, w_pad_t, w_pad_t, x2, dout, norm_w, h_in, h_out, hb_in, hb_out)


def _grad_w_in(hn_t, pieces):
    D, T = hn_t.shape
    tk = min(1024, T)
    starts, counts = _piece_blocks(pieces, D)

    def body(*refs):
        a_ref, p_refs, o_ref = refs[0], refs[1:-1], refs[-1]
        j = pl.program_id(0)

        @pl.when(pl.program_id(1) == 0)
        def _():
            o_ref[...] = jnp.zeros_like(o_ref)

        for p_ref, s, n in zip(p_refs, starts, counts):
            @pl.when((j >= s) & (j < s + n))
            def _(p_ref=p_ref):
                o_ref[...] += _dot(a_ref[...], p_ref[...])

    def piece_spec(s, n):
        return pl.BlockSpec((tk, D), lambda j, k: (jnp.where((j >= s) & (j < s + n), k, 0), jnp.clip(j - s, 0, n - 1)))

    return pl.pallas_call(
        body,
        name="grad_w_in",
        grid=(sum(counts), T // tk),
        in_specs=[pl.BlockSpec((D, tk), lambda j, k: (0, k))] + [piece_spec(s, n) for s, n in zip(starts, counts)],
        out_specs=pl.BlockSpec((D, D), lambda j, k: (0, j)),
        out_shape=jax.ShapeDtypeStruct((D, sum(counts) * D), F32),
        compiler_params=_params(("parallel", "arbitrary")),
    )(hn_t, *pieces)


def _grad_w_out(a, b, g_in, width):
    M, K = a.shape
    N = b.shape[1]
    tm = min(1024, M)
    tn = 1024 if N % 1024 == 0 else (512 if N % 512 == 0 else N)
    tk = min(512, K)
    grid = (M // tm, N // tn, K // tk)
    h = g_in.shape[0] // 2

    def body(a_ref, b_ref, gin, o_ref, rin, send_sem, recv_sem):
        ids = [pl.program_id(d) for d in range(3)]

        def swap():
            x, y, c, _ = _place()
            return pltpu.make_async_remote_copy(
                src_ref=gin.at[pl.ds((1 - c) * h, h), pl.ds(0, width)], dst_ref=rin, send_sem=send_sem, recv_sem=recv_sem,
                device_id=(x, y, 1 - c), device_id_type=MESH)

        @pl.when((ids[0] == 0) & (ids[1] == 0) & (ids[2] == 0))
        def _():
            swap().start()

        @pl.when(ids[2] == 0)
        def _():
            o_ref[...] = jnp.zeros_like(o_ref)

        o_ref[...] += _dot(a_ref[...], b_ref[...])

        @pl.when((ids[0] == grid[0] - 1) & (ids[1] == grid[1] - 1) & (ids[2] == grid[2] - 1))
        def _():
            swap().wait()

    return pl.pallas_call(
        body,
        name="grad_w_out",
        grid=grid,
        in_specs=[pl.BlockSpec((tm, tk), lambda i, j, k: (i, k)), pl.BlockSpec((tk, tn), lambda i, j, k: (k, j)), ANY],
        out_specs=[pl.BlockSpec((tm, tn), lambda i, j, k: (i, j)), ANY],
        out_shape=[jax.ShapeDtypeStruct((M, N), F32), jax.ShapeDtypeStruct((h, width), F32)],
        scratch_shapes=[pltpu.SemaphoreType.DMA, pltpu.SemaphoreType.DMA],
        compiler_params=_params(("arbitrary", "arbitrary", "arbitrary")),
    )(a, b, g_in)


def _adamw(w, g, m, v, name):
    R, C = w.shape
    tr = 256 if R % 256 == 0 else R
    tc = LANES if (tr == R and R > 256 and C % LANES == 0) else C
    c1 = 1.0 - ADAM_B1 ** ADAM_STEP
    c2 = 1.0 - ADAM_B2 ** ADAM_STEP

    def body(w_ref, g_ref, m_ref, v_ref, d_ref, nm_ref, nv_ref):
        gv = g_ref[...]
        m_new = ADAM_B1 * m_ref[...] + (1.0 - ADAM_B1) * gv
        v_new = ADAM_B2 * v_ref[...] + (1.0 - ADAM_B2) * (gv * gv)
        d_ref[...] = -ADAM_LR * ((m_new / c1) / (jnp.sqrt(v_new / c2) + ADAM_EPS) + ADAM_WD * w_ref[...])
        nm_ref[...] = m_new
        nv_ref[...] = v_new

    spec = pl.BlockSpec((tr, tc), lambda i, j: (i, j))
    return pl.pallas_call(
        body,
        name=name,
        grid=(R // tr, C // tc),
        in_specs=[spec] * 4,
        out_specs=[spec] * 3,
        out_shape=[jax.ShapeDtypeStruct((R, C), F32)] * 3,
        compiler_params=_params(("parallel", "parallel")),
    )(w, g, m, v)


def _add_core_rows(g, recv, core, name):
    h, width = recv.shape
    th = 128 if h % 128 == 0 else h

    def body(c_ref, g_ref, r_ref, o_ref, ob_ref):
        o_ref[...] = g_ref[...] + r_ref[...]
        ob_ref[...] = o_ref[...].astype(BF16)

    return pl.pallas_call(
        body,
        name=name,
        grid_spec=pltpu.PrefetchScalarGridSpec(
            num_scalar_prefetch=1,
            grid=(h // th,),
            in_specs=[
                pl.BlockSpec((th, width), lambda i, c: (c[0] * (h // th) + i, 0)),
                pl.BlockSpec((th, width), lambda i, c: (i, 0)),
            ],
            out_specs=[pl.BlockSpec((th, width), lambda i, c: (i, 0))] * 2,
        ),
        out_shape=[jax.ShapeDtypeStruct((h, width), F32), jax.ShapeDtypeStruct((h, width), BF16)],
        compiler_params=_params(("parallel",)),
    )(core, g, recv)


def _add_core_blocks(g, recv, core, name):
    n, hb, C = recv.shape

    def body(c_ref, g_ref, r_ref, o_ref, ob_ref):
        o_ref[...] = g_ref[...] + r_ref[...]
        ob_ref[...] = o_ref[...].astype(BF16)

    return pl.pallas_call(
        body,
        name=name,
        grid_spec=pltpu.PrefetchScalarGridSpec(
            num_scalar_prefetch=1,
            grid=(n,),
            in_specs=[
                pl.BlockSpec((hb, C), lambda p, c: (2 * p + c[0], 0)),
                pl.BlockSpec((None, hb, C), lambda p, c: (p, 0, 0)),
            ],
            out_specs=[pl.BlockSpec((None, hb, C), lambda p, c: (p, 0, 0))] * 2,
        ),
        out_shape=[jax.ShapeDtypeStruct((n, hb, C), F32), jax.ShapeDtypeStruct((n, hb, C), BF16)],
        compiler_params=_params(("parallel",)),
    )(core, g, recv)


def _add_chips(own, recv, name):
    h, W = own.shape
    th = 256 if h % 256 == 0 else h

    def body(a_ref, r_ref, o_ref):
        o_ref[...] = ((a_ref[...] + r_ref[0].astype(F32)) + r_ref[1].astype(F32)) + r_ref[2].astype(F32)

    return pl.pallas_call(
        body,
        name=name,
        grid=(h // th,),
        in_specs=[pl.BlockSpec((th, W), lambda i: (i, 0)), pl.BlockSpec((N_CHIPS - 1, th, W), lambda i: (0, i, 0))],
        out_specs=pl.BlockSpec((th, W), lambda i: (i, 0)),
        out_shape=jax.ShapeDtypeStruct((h, W), F32),
        compiler_params=_params(("parallel",)),
    )(own, recv)


def _place():
    x, y, c = lax.axis_index("x"), lax.axis_index("y"), lax.axis_index("c")
    other_chips = [(1 - x, y), (x, 1 - y), (1 - x, 1 - y)]
    return x, y, c, other_chips


def _allgather_w_in(w_in_bf, x2, norm_w):
    S, D = w_in_bf.shape
    T = x2.shape[0]
    tm = min(1024, T)
    ni = T // tm
    n_ici = n_fwd = N_CHIPS - 1

    def body(win, x_ref, nw_ref, gin, hn_ref, hnt_ref, send_sems, recv_sems):
        step = pl.program_id(0)
        xv = x_ref[...]
        hn = xv * lax.rsqrt(jnp.mean(xv * xv, axis=-1, keepdims=True) + EPS) * nw_ref[...]
        hn_ref[...] = hn.astype(BF16)
        hnt_ref[...] = hn.T.astype(BF16)
        x, y, c, chips = _place()
        me = 2 * x + y
        sibling = (x, y, 1 - c)
        hin = D // 2

        def half(chip_idx, core):
            return gin.at[chip_idx, :, pl.ds(core * hin, hin)]

        def rcopy(k, src, dst, to):
            return pltpu.make_async_remote_copy(src_ref=src, dst_ref=dst, send_sem=send_sems.at[k],
                                                recv_sem=recv_sems.at[k], device_id=to, device_id_type=MESH)

        def sends():
            return [rcopy(j, win.at[:, pl.ds(c * hin, hin)], half(me, c), (*chip, c)) for j, chip in enumerate(chips)]

        @pl.when(step == 0)
        def _():
            for cp in sends():
                cp.start()

        @pl.when(step == ni - 1)
        def _():
            passed = []
            for j, (px, py) in enumerate(chips):
                theirs = half(2 * px + py, c)
                rcopy(j, theirs, theirs, sibling).wait_recv()
                passed.append(rcopy(n_ici + j, theirs, theirs, sibling))
                passed[-1].start()
            for j, (px, py) in enumerate(chips):
                other = half(2 * px + py, 1 - c)
                rcopy(n_ici + j, other, other, sibling).wait_recv()
            for cp in sends() + passed:
                cp.wait_send()

    return pl.pallas_call(
        body,
        name="allgather_w_in",
        grid=(ni,),
        in_specs=[ANY, pl.BlockSpec((tm, D), lambda i: (i, 0)), pl.BlockSpec((1, D), lambda i: (0, 0))],
        out_specs=[ANY, pl.BlockSpec((tm, D), lambda i: (i, 0)), pl.BlockSpec((D, tm), lambda i: (0, i))],
        out_shape=[jax.ShapeDtypeStruct((N_CHIPS, S, D), BF16),
                   jax.ShapeDtypeStruct((T, D), BF16), jax.ShapeDtypeStruct((D, T), BF16)],
        scratch_shapes=[pltpu.SemaphoreType.DMA((n_ici + n_fwd,)), pltpu.SemaphoreType.DMA((n_ici + n_fwd,))],
        compiler_params=_params(("arbitrary",)),
    )(w_in_bf, x2, norm_w)


def _allreduce_small(packed):
    R = packed.shape[0]
    n_dev = 2 * N_CHIPS

    def body(p_ref, o_ref, buf, send_sems, recv_sems):
        x, y, c, _ = _place()
        me = 4 * x + 2 * y + c
        buf[me] = p_ref[...]
        copies = []
        for k in range(1, n_dev):
            px = 1 - x if k & 4 else x
            py = 1 - y if k & 2 else y
            pc = 1 - c if k & 1 else c
            copies.append((pltpu.make_async_remote_copy(
                src_ref=buf.at[me], dst_ref=buf.at[me], send_sem=send_sems.at[k - 1], recv_sem=recv_sems.at[k - 1],
                device_id=(px, py, pc), device_id_type=MESH), 4 * px + 2 * py + pc, (px, py, pc)))
        for cp, _, _ in copies:
            cp.start()
        for k, (_, peer, to) in enumerate(copies):
            pltpu.make_async_remote_copy(
                src_ref=buf.at[peer], dst_ref=buf.at[peer], send_sem=send_sems.at[k], recv_sem=recv_sems.at[k],
                device_id=to, device_id_type=MESH).wait_recv()
        for cp, _, _ in copies:
            cp.wait_send()
        acc = buf[0]
        for d in range(1, n_dev):
            acc = acc + buf[d]
        o_ref[...] = acc

    vm = pl.BlockSpec(memory_space=pltpu.VMEM)
    return pl.pallas_call(
        body,
        name="allreduce_small",
        in_specs=[vm],
        out_specs=vm,
        out_shape=jax.ShapeDtypeStruct((R, LANES), F32),
        scratch_shapes=[pltpu.VMEM((n_dev, R, LANES), F32), pltpu.SemaphoreType.DMA((n_dev - 1,)),
                        pltpu.SemaphoreType.DMA((n_dev - 1,))],
    )(packed)


def _swap_core_halves(g_out):
    hb = g_out.shape[0] // (2 * N_CHIPS)

    def body(gout, rout, send_sems, recv_sems):
        x, y, c, _ = _place()
        cps = [pltpu.make_async_remote_copy(
            src_ref=gout.at[pl.ds((2 * p + 1 - c) * hb, hb)], dst_ref=rout.at[p], send_sem=send_sems.at[p],
            recv_sem=recv_sems.at[p], device_id=(x, y, 1 - c), device_id_type=MESH) for p in range(N_CHIPS)]
        for cp in cps:
            cp.start()
        for cp in cps:
            cp.wait()

    return pl.pallas_call(
        body,
        name="reduce_core_swap",
        in_specs=[ANY],
        out_specs=ANY,
        out_shape=jax.ShapeDtypeStruct((N_CHIPS, hb, g_out.shape[1]), F32),
        scratch_shapes=[pltpu.SemaphoreType.DMA((N_CHIPS,)), pltpu.SemaphoreType.DMA((N_CHIPS,))],
    )(g_out)


def _join_core_halves(g_in, g_out):
    def body(gin, gout, fin, fout, send_sems, recv_sems):
        x, y, c, _ = _place()
        cps = [pltpu.make_async_remote_copy(src_ref=s, dst_ref=d.at[c], send_sem=send_sems.at[k],
                                            recv_sem=recv_sems.at[k], device_id=(x, y, 1 - c), device_id_type=MESH)
               for k, (s, d) in enumerate(((gin, fin), (gout, fout)))]
        for cp in cps:
            cp.start()
        for k, (s, d) in enumerate(((gin, fin), (gout, fout))):
            pltpu.make_async_remote_copy(src_ref=s, dst_ref=d.at[1 - c], send_sem=send_sems.at[k],
                                         recv_sem=recv_sems.at[k], device_id=(x, y, 1 - c),
                                         device_id_type=MESH).wait_recv()
        for cp in cps:
            cp.wait_send()

    return pl.pallas_call(
        body,
        name="reduce_core_join",
        in_specs=[ANY, ANY],
        out_specs=[ANY, ANY],
        out_shape=[jax.ShapeDtypeStruct((2,) + g_in.shape, F32), jax.ShapeDtypeStruct((2,) + g_out.shape, F32)],
        scratch_shapes=[pltpu.SemaphoreType.DMA((2,)), pltpu.SemaphoreType.DMA((2,))],
    )(g_in, g_out)


def _pack(arrays):
    rows = []
    for a in arrays:
        flat = a.reshape(-1).astype(F32)
        n = -(-flat.shape[0] // LANES) * LANES
        rows.append(jnp.pad(flat, (0, n - flat.shape[0])).reshape(-1, LANES))
    out = jnp.concatenate(rows, axis=0)
    return jnp.pad(out, ((0, -out.shape[0] % 8), (0, 0)))


def _unpack(packed, shapes):
    out, r = [], 0
    for shp in shapes:
        n = math.prod(shp)
        nr = -(-n // LANES)
        out.append(packed[r:r + nr].reshape(-1)[:n].reshape(shp))
        r += nr
    return out


def _pad_lanes(a):
    return jnp.pad(a, ((0, 0), (0, LANES - a.shape[1])))


def kernel(x, norm_w, w_in, q_norm_w, k_norm_w, conv_w, conv_b, dt_bias, A_log, D_skip, sb_norm_w, ssd_norm_w, w_out, loss_target, m_norm_w, m_w_in, m_q_norm_w, m_k_norm_w, m_conv_w, m_conv_b, m_dt_bias, m_A_log, m_D_skip, m_sb_norm_w, m_ssd_norm_w, m_w_out, v_norm_w, v_w_in, v_q_norm_w, v_k_norm_w, v_conv_w, v_conv_b, v_dt_bias, v_A_log, v_D_skip, v_sb_norm_w, v_ssd_norm_w, v_w_out):
    Bl, L, D = x.shape
    T = Bl * L
    S = w_in.shape[2]
    R = w_out.shape[1]
    CW = conv_w.shape[2]
    n_in = N_CHIPS * S
    CD = D + 2 * SSD_GROUPS * SSD_STATE
    H = D // HEAD_DIM
    n_main = 6 * D + 512
    P = -(-(n_main + LANES) // 1024) * 1024
    assert n_in == n_main + H and CD == N_CHIPS * CW and 2 * D == N_CHIPS * R and CD == D + 512
    chip = (2 * lax.axis_index("x") + lax.axis_index("y")).astype(jnp.int32)
    core = lax.axis_index("c").astype(jnp.int32)

    w_in_t, m_in_t, v_in_t = w_in[0].T, m_w_in[0].T, v_w_in[0].T
    w_in_bf, w_out_shard_bf = w_in_t.astype(BF16), w_out[0].astype(BF16)
    x2 = x.reshape(T, D)
    g_in, hn, hn_t = _allgather_w_in(w_in_bf, x2, norm_w)
    g_in = lax.dynamic_update_slice(g_in, w_in_bf[None], (chip, 0, 0))
    w_pad_t = jnp.concatenate([g_in[p] for p in range(N_CHIPS)] + [jnp.zeros((P - n_in, D), BF16)], axis=0)
    proj, g_out, g_cw = _inproj(hn, w_pad_t, w_out_shard_bf, conv_w[0])
    g_out = lax.dynamic_update_slice(g_out, w_out_shard_bf[None], (chip, 0, 0))
    g_cw = lax.dynamic_update_slice(g_cw, conv_w, (chip, 0, 0))
    w_out_bf = g_out.reshape(2 * D, D)
    conv_full = g_cw.transpose(1, 0, 2).reshape(CONV_K, CD)
    cwx, cwb = conv_full[:, :D], conv_full[:, D:]
    cbx, cbb = conv_b[:, :D], conv_b[:, D:]
    dtb, alog, dsk = _pad_lanes(dt_bias), _pad_lanes(A_log), _pad_lanes(D_skip)
    qw2, kw2 = jnp.tile(q_norm_w, (1, 2)), jnp.tile(k_norm_w, (1, 2))

    proj3 = proj.reshape(Bl, L, P)
    o_sb = _attn_fwd(proj3, qw2, kw2, D)
    y_ssd, s_in = _ssd_fwd(proj3, cwx, cwb, cbx, cbb, dtb, alog, dsk, D)
    dout, dout_bf, mixed_t, do_sb, dy_ssd, dz_bf, dnw_out, loss_blk = _gate_out(
        x2, loss_target.reshape(T, D), o_sb.reshape(T, D), proj, y_ssd.reshape(T, D), sb_norm_w, ssd_norm_w, w_out_bf,
        w_out_bf.T)

    dq, dk, dv, dqkw = _attn_bwd(proj3, o_sb, do_sb.reshape(Bl, L, D), qw2, kw2, D)
    dtail, dcwx, dcwb, dcbx, dcbb, misc = _ssd_bwd(
        proj3, s_in, dy_ssd.reshape(Bl, L, D), cwx, cwb, cbx, cbb, dtb, alog, dsk, D, P - 5 * D)
    dproj = [dq.reshape(T, D), dk.reshape(T, D), dv.reshape(T, D), dz_bf, dtail.reshape(T, P - 5 * D)]
    gw_in = _grad_w_in(hn_t, dproj)

    slab_off = S // LANES * LANES
    slab_w = -(-(S + (N_CHIPS - 1) * (S - slab_off)) // LANES) * LANES
    width = (N_CHIPS - 1) * slab_off + slab_w
    assert n_in <= width <= P
    core1 = core.reshape(1)
    gw_out, r_in = _grad_w_out(mixed_t, dout_bf, gw_in, width)
    r_out = _swap_core_halves(gw_out)
    h_in, hb_in = _add_core_rows(gw_in, r_in, core1, "sum_cores_w_in")
    h_out, hb_out = _add_core_blocks(gw_out, r_out, core1, "sum_cores_w_out")
    grad_x2, dnw_in, s_in_, s_out_, o_in_, o_out_ = _dhn(dproj, w_pad_t, x2, dout, norm_w, h_in, h_out, hb_in, hb_out,
                                                         slab_off, slab_w)
    gh_in = _add_chips(o_in_, s_in_, "sum_chips_w_in")
    gh_out = _add_chips(o_out_, s_out_, "sum_chips_w_out")
    f_in, f_out = _join_core_halves(gh_in, gh_out)
    g_slab = lax.dynamic_update_slice(f_in, gh_in[None], (core, 0, 0)).reshape(D, slab_w)
    g_w_in = lax.dynamic_slice(g_slab, (0, chip * (S - slab_off)), (D, S))
    g_w_out = lax.dynamic_update_slice(f_out, gh_out[None], (core, 0, 0)).reshape(R, D)

    small_shapes = [(1, D), (1, D), (1, D), (1, CD), (1, HEAD_DIM), (1, HEAD_DIM), (1, H), (1, H), (1, H)]
    g_small_local = [dnw_in[0:1], dnw_out[0:1], dnw_out[1:2], jnp.concatenate([dcbx, dcbb], axis=1),
                     dqkw[0:1, :HEAD_DIM] + dqkw[0:1, HEAD_DIM:], dqkw[1:2, :HEAD_DIM] + dqkw[1:2, HEAD_DIM:],
                     misc[0:1, :H], misc[1:2, :H], misc[2:3, :H]]
    packed = _pack(g_small_local + [jnp.concatenate([dcwx, dcwb], axis=1), loss_blk[0:1, 0:1]])
    red = _allreduce_small(packed)
    g_small = _unpack(red, small_shapes + [(CONV_K, CD), (1, 1)])
    g_conv_w = lax.dynamic_slice_in_dim(g_small[9], chip * CW, CW, axis=1)
    loss = g_small[10][0, 0]

    d_in, nm_in, nv_in = (t.T for t in _adamw(w_in_t, g_w_in.T, m_in_t, v_in_t, "adamw_w_in"))
    d_out, nm_out, nv_out = _adamw(w_out[0], g_w_out, m_w_out[0], v_w_out[0], "adamw_w_out")
    d_cw, nm_cw, nv_cw = _adamw(conv_w[0], g_conv_w, m_conv_w[0], v_conv_w[0], "adamw_conv_w")
    small_w = [norm_w, sb_norm_w, ssd_norm_w, conv_b, q_norm_w, k_norm_w, dt_bias, A_log, D_skip]
    small_m = [m_norm_w, m_sb_norm_w, m_ssd_norm_w, m_conv_b, m_q_norm_w, m_k_norm_w, m_dt_bias, m_A_log, m_D_skip]
    small_v = [v_norm_w, v_sb_norm_w, v_ssd_norm_w, v_conv_b, v_q_norm_w, v_k_norm_w, v_dt_bias, v_A_log, v_D_skip]
    d_s, nm_s, nv_s = _adamw(_pack(small_w), _pack(g_small[:9]), _pack(small_m), _pack(small_v), "adamw_small")
    d_s, nm_s, nv_s = (_unpack(t, small_shapes) for t in (d_s, nm_s, nv_s))

    def ordered(s, w_in_, conv_w_, w_out_):
        return [s[0], w_in_[None], s[4], s[5], conv_w_[None], s[3], s[6], s[7], s[8], s[1], s[2], w_out_[None]]

    return (loss, grad_x2.reshape(Bl, L, D),
            *ordered(g_small[:9], g_w_in, g_conv_w, g_w_out),
            *ordered(d_s, d_in, d_cw, d_out),
            *ordered(nm_s, nm_in, nm_cw, nm_out),
            *ordered(nv_s, nv_in, nv_cw, nv_out))
```

```python
import functools
import math

import jax
import jax.numpy as jnp
from jax import lax
from jax.experimental import pallas as pl
from jax.experimental.pallas import tpu as pltpu

F32 = jnp.float32
BF16 = jnp.bfloat16
EPS = 1e-6
HEAD_DIM = 64
PAIR = 2 * HEAD_DIM
LANES = 128
SSD_STATE = 128
SSD_GROUPS = 2
BLK = 128
PREP_BLOCKS = 16
Q_TOGETHER_FWD = 2
Q_TOGETHER_BWD = 2
FIRST_LEFT = 2
UNDERFLOW = -105.0
CONV_K = 4
HALO = 8
N_CHIPS = 4
ADAM_LR, ADAM_B1, ADAM_B2, ADAM_EPS, ADAM_WD, ADAM_STEP = 0.001, 0.9, 0.999, 1e-08, 0.01, 10
VMEM_LIMIT_V7X = 56 * 1024 * 1024
MESH = pl.DeviceIdType.MESH
ANY = pl.BlockSpec(memory_space=pl.ANY)
NT = (((1,), (1,)), ((), ()))


def _params(sem=None):
    kw = dict(vmem_limit_bytes=VMEM_LIMIT_V7X)
    if sem is not None:
        kw["dimension_semantics"] = sem
    return pltpu.CompilerParams(**kw)


def _dot(a, b):
    return jnp.dot(a, b, preferred_element_type=F32)


def _dot_nt(a, b):
    return lax.dot_general(a, b, NT, preferred_element_type=F32)


def _dot_split(m, x):
    hi = x.astype(BF16)
    lo = (x - hi.astype(F32)).astype(BF16)
    return _dot(m, hi) + _dot(m, lo)


def _iota(shape, dim):
    return lax.broadcasted_iota(jnp.int32, shape, dim)


def _rowsum(x):
    return jnp.sum(x, axis=1, keepdims=True)


def _colsum(x):
    return jnp.sum(x, axis=0, keepdims=True)


def _sigmoid(x):
    return 0.5 * jnp.tanh(0.5 * x) + 0.5


def _softplus(x):
    return jnp.maximum(x, 0.0) + jnp.log(1.0 + jnp.exp(-jnp.abs(x)))


def _stack_shards(g_in, P):
    n, S, D = g_in.shape

    def body(g_ref, o_ref):
        for p in range(n):
            o_ref[p * S:(p + 1) * S, :] = g_ref[p]
        o_ref[n * S:P, :] = jnp.zeros((P - n * S, LANES), BF16)

    return pl.pallas_call(
        body,
        name="stack_w_in",
        grid=(D // LANES,),
        in_specs=[pl.BlockSpec((n, S, LANES), lambda j: (0, 0, j))],
        out_specs=pl.BlockSpec((P, LANES), lambda j: (0, j)),
        out_shape=jax.ShapeDtypeStruct((P, D), BF16),
        compiler_params=_params(("parallel",)),
    )(g_in)


def _inproj(hn, w_pad_t, w_out_bf, conv_w):
    T, D = hn.shape
    P = w_pad_t.shape[0]
    tm = min(1024, T)
    tn = 1024 if P % 1024 == 0 else 512
    ni, nj = T // tm, P // tn
    n_sem = 2 * (N_CHIPS - 1)

    def body(hn_ref, w_ref, wout, cw, proj_ref, gout, gcw, send_sems, recv_sems):
        def gather():
            x, y, c, chips = _place()
            me = 2 * x + y
            return [pltpu.make_async_remote_copy(
                src_ref=src, dst_ref=dst.at[me], send_sem=send_sems.at[2 * j + m], recv_sem=recv_sems.at[2 * j + m],
                device_id=(px, py, c), device_id_type=MESH)
                for j, (px, py) in enumerate(chips) for m, (src, dst) in enumerate(((wout, gout), (cw, gcw)))]

        @pl.when((pl.program_id(0) == 0) & (pl.program_id(1) == 0))
        def _():
            for cp in gather():
                cp.start()

        @pl.when((pl.program_id(0) == ni - 1) & (pl.program_id(1) == nj - 1))
        def _():
            for cp in gather():
                cp.wait()

        proj_ref[...] = _dot_nt(hn_ref[...], w_ref[...])

    return pl.pallas_call(
        body,
        name="inproj",
        grid=(T // tm, P // tn),
        in_specs=[
            pl.BlockSpec((tm, D), lambda i, j: (i, 0)),
            pl.BlockSpec((tn, D), lambda i, j: (j, 0)),
            ANY, ANY,
        ],
        out_specs=[
            pl.BlockSpec((tm, tn), lambda i, j: (i, j)),
            ANY, ANY,
        ],
        out_shape=[jax.ShapeDtypeStruct((T, P), F32),
                   jax.ShapeDtypeStruct((N_CHIPS,) + w_out_bf.shape, BF16),
                   jax.ShapeDtypeStruct((N_CHIPS,) + conv_w.shape, F32)],
        scratch_shapes=[pltpu.SemaphoreType.DMA((n_sem,)), pltpu.SemaphoreType.DMA((n_sem,))],
        compiler_params=_params(("arbitrary", "arbitrary")),
    )(hn, w_pad_t, w_out_bf, conv_w)


def _pair_ones():
    ri = ((_iota((2 * PAIR, PAIR), 0) % PAIR) >= HEAD_DIM).astype(jnp.int32)
    ci = (_iota((2 * PAIR, PAIR), 1) >= HEAD_DIM).astype(jnp.int32)
    return jnp.where(ri == ci, 1.0, 0.0).astype(BF16)


def _pair_rms(v, ones2):
    return lax.rsqrt(_split_dots([v * v], ones2)[0] * (1.0 / HEAD_DIM) + EPS)


def _pair_mean(v, ones2):
    return _split_dots([v], ones2)[0] * (1.0 / HEAD_DIM)


def _suffix_ones():
    ri = _iota((2 * BLK, 2 * BLK), 0) % BLK
    ci = _iota((2 * BLK, 2 * BLK), 1)
    return jnp.where((ci >= BLK) | (ri > ci), 1.0, 0.0).astype(BF16)


def _split_dots(xs, m2):
    his = [x.astype(BF16) for x in xs]
    los = [(x - hi.astype(F32)).astype(BF16) for x, hi in zip(xs, his)]
    return [_dot(jnp.concatenate([hi, lo], axis=1), m2) for hi, lo in zip(his, los)]


def _sb_tiles(streams, km_s, uo):
    tiles = [(s, u, h) for s, st in enumerate(streams) for u in range(len(st["kbs"])) for h in range(2)]
    z2s = {(s, u): _dot_nt(st["q"], km_s[kb]) for s, st in enumerate(streams) for u, kb in enumerate(st["kbs"])}
    zs = [z2s[s, u][:, h * BLK:(h + 1) * BLK] for s, u, h in tiles]
    es = [jnp.exp(-jnp.abs(z)) for z in zs]
    las = [jnp.minimum(z, 0.0) - jnp.log(1.0 + e) for z, e in zip(zs, es)]
    lns = [a - z for a, z in zip(las, zs)]
    masks = [streams[s]["masks"][u] for s, u, h in tiles]
    lks = [lk if m is None else jnp.where(m, lk, 0.0) for m, lk in zip(masks, lns)]
    css = _split_dots(lks, uo)
    rests = [list(st["rest"]) for st in streams]
    ws = []
    for (s, u, h), m, a, cs in zip(tiles, masks, las, css):
        w = jnp.exp(a + rests[s][h] + cs[:, :BLK])
        ws.append(w if m is None else jnp.where(m, w, 0.0))
        rests[s][h] = rests[s][h] + cs[:, BLK:]
    return tiles, las, lns, ws, rests


def _stream(q_pair, qi, n_left, diag, zero):
    return dict(q=q_pair, kbs=[qi - u for u in range(n_left + 1)], masks=[diag] + [None] * n_left, rest=[zero, zero])


def _row0(block):
    return block * BLK if isinstance(block, int) else pl.multiple_of(block * BLK, BLK)


def _pair_of(vals, tiles, s, u):
    return [v for v, t in zip(vals, tiles) if t[0] == s and t[1] == u]


def _block_groups(nq, together):
    n_tog = math.gcd(together, nq)
    assert n_tog >= FIRST_LEFT
    return n_tog, list(range(n_tog)), nq // n_tog


def _attn_prep(src_ref, w_ref, dst_s, n_blocks, scale):
    per = math.gcd(PREP_BLOCKS, n_blocks)
    rows = per * BLK
    lo = _iota((rows, PAIR), 1) < HEAD_DIM
    ones2 = _pair_ones()

    def step(i, carry):
        r0 = pl.multiple_of(i * rows, rows)
        v = src_ref[0, pl.ds(r0, rows), :]
        if w_ref is not None:
            v = v * _pair_rms(v, ones2) * w_ref[...]
        if scale != 1.0:
            v = v * scale
        v0, v1 = jnp.where(lo, v, 0.0).astype(BF16), jnp.where(lo, 0.0, v).astype(BF16)
        for b in range(per):
            dst_s[i * per + b, 0:BLK, :] = v0[b * BLK:(b + 1) * BLK]
            dst_s[i * per + b, BLK:2 * BLK, :] = v1[b * BLK:(b + 1) * BLK]
        return carry

    lax.fori_loop(0, n_blocks // per, step, 0)


def _attn_fwd(proj3, qw2, kw2, D):
    Bl, L, _ = proj3.shape
    n_pair = D // PAIR
    nq = L // BLK
    scale = 1.0 / math.sqrt(HEAD_DIM)

    def body(q_ref, k_ref, v_ref, qw_ref, kw_ref, o_ref, qm_s, km_s, vm_s):
        uo = _suffix_ones()
        diag = _iota((BLK, BLK), 1) < _iota((BLK, BLK), 0)
        _attn_prep(q_ref, qw_ref, qm_s, nq, scale)
        _attn_prep(k_ref, kw_ref, km_s, nq, 1.0)
        _attn_prep(v_ref, None, vm_s, nq, 1.0)

        zero_c = jnp.zeros((BLK, BLK), F32)

        def q_of(qi):
            return qm_s[qi, 0:BLK, :] + qm_s[qi, BLK:2 * BLK, :]

        def values(streams, accs):
            tiles, _, _, ws, rests = _sb_tiles(streams, km_s, uo)
            wbs = [w.astype(BF16) for w in ws]
            accs = list(accs)
            for s, st in enumerate(streams):
                for u, kb in enumerate(st["kbs"]):
                    accs[s] = accs[s] + _dot(jnp.concatenate(_pair_of(wbs, tiles, s, u), axis=1), vm_s[kb])
            return accs, rests

        def group(qis, n_lefts):
            streams = [_stream(q_of(qi), qi, n, diag, zero_c) for qi, n in zip(qis, n_lefts)]
            accs, rests = values(streams, [jnp.zeros((BLK, PAIR), F32)] * len(qis))
            for qi, n, q, acc, rc in zip(qis, n_lefts, [st["q"] for st in streams], accs, rests):

                def sweep(state, n_blocks, q=q):
                    kb, rc0, rc1, acc1, _ = state
                    st = dict(q=q, kbs=[kb - u for u in range(n_blocks)], masks=[None] * n_blocks, rest=[rc0, rc1])
                    (acc1,), (r,) = values([st], [acc1])
                    return kb - n_blocks, r[0], r[1], acc1, jnp.maximum(jnp.max(r[0]), jnp.max(r[1]))

                state = (jnp.asarray(qi - n - 1, jnp.int32), rc[0], rc[1], acc, jnp.maximum(jnp.max(rc[0]), jnp.max(rc[1])))
                state = lax.while_loop(lambda t: (t[0] >= 1) & (t[4] >= UNDERFLOW), lambda t: sweep(t, 2), state)
                state = lax.while_loop(lambda t: (t[0] >= 0) & (t[4] >= UNDERFLOW), lambda t: sweep(t, 1), state)
                o_ref[0, pl.ds(_row0(qi), BLK), :] = state[3]

        n_tog, head, n_groups = _block_groups(nq, Q_TOGETHER_FWD)
        group(head, [min(qi, FIRST_LEFT) for qi in head])

        def groups(g, carry):
            group([g * n_tog + j for j in range(n_tog)], [FIRST_LEFT] * n_tog)
            return carry

        lax.fori_loop(1, n_groups, groups, 0)

    blk = lambda off: pl.BlockSpec((1, L, PAIR), lambda b, p: (b, 0, off + p))
    wspec = pl.BlockSpec((1, PAIR), lambda b, p: (0, 0))
    return pl.pallas_call(
        body,
        name="sb_attn_fwd",
        grid=(Bl, n_pair),
        in_specs=[blk(0), blk(n_pair), blk(2 * n_pair), wspec, wspec],
        out_specs=pl.BlockSpec((1, L, PAIR), lambda b, p: (b, 0, p)),
        out_shape=jax.ShapeDtypeStruct((Bl, L, D), F32),
        scratch_shapes=[pltpu.VMEM((nq, 2 * BLK, PAIR), BF16)] * 3,
        compiler_params=_params(("parallel", "parallel")),
    )(proj3, proj3, proj3, qw2, kw2)


def _attn_bwd(proj3, o3, do3, qw2, kw2, D):
    Bl, L, _ = proj3.shape
    n_pair = D // PAIR
    nq = L // BLK
    scale = 1.0 / math.sqrt(HEAD_DIM)

    def body(q_ref, k_ref, v_ref, o_ref, do_ref, qw_ref, kw_ref, dq_ref, dk_ref, dv_ref, dw_ref,
             qm_s, km_s, vm_s, dom_s, dq_s, dk_s, dv_s):
        uo = _suffix_ones()
        diag = _iota((BLK, BLK), 1) < _iota((BLK, BLK), 0)
        ones2 = _pair_ones()
        _attn_prep(q_ref, qw_ref, qm_s, nq, scale)
        _attn_prep(k_ref, kw_ref, km_s, nq, 1.0)
        _attn_prep(v_ref, None, vm_s, nq, 1.0)
        _attn_prep(do_ref, None, dom_s, nq, 1.0)

        @pl.when((pl.program_id(0) == 0) & (pl.program_id(1) == 0))
        def _():
            dw_ref[...] = jnp.zeros_like(dw_ref)

        def zero(i, carry):
            r0 = pl.multiple_of(i * BLK, BLK)
            dk_s[pl.ds(r0, BLK), :] = jnp.zeros((BLK, PAIR), F32)
            dv_s[pl.ds(r0, BLK), :] = jnp.zeros((BLK, PAIR), F32)
            return carry

        lax.fori_loop(0, nq, zero, 0)

        zero_c = jnp.zeros((BLK, BLK), F32)

        def tiles_bwd(streams, dqas):
            tiles, las, lns, ws, rests = _sb_tiles(streams, km_s, uo)
            dw2s = {(s, u): _dot_nt(st["do"], vm_s[kb]) for s, st in enumerate(streams) for u, kb in enumerate(st["kbs"])}
            dws = [dw2s[s, u][:, h * BLK:(h + 1) * BLK] for s, u, h in tiles]
            wfs = [w.astype(BF16).astype(F32) for w in ws]
            gs = [wf * dw for wf, dw in zip(wfs, dws)]
            gss = _split_dots(gs, uo)
            gcs = [list(st["g_rest"]) for st in streams]
            dzs = []
            for (s, u, h), a, ln, g, gsum in zip(tiles, las, lns, gs, gss):
                g_before = streams[s]["delta"][h] - (gcs[s][h] + gsum[:, :BLK] + g)
                gcs[s][h] = gcs[s][h] + gsum[:, BLK:]
                dz = g * jnp.exp(ln) - g_before * jnp.exp(a)
                m = streams[s]["masks"][u]
                dzs.append(dz if m is None else jnp.where(m, dz, 0.0))
            wts = [wf.T.astype(BF16) for wf in wfs]
            dzts = [dz.T.astype(BF16) for dz in dzs]
            dzbs = [dz.astype(BF16) for dz in dzs]
            dqas = list(dqas)
            for s, st in enumerate(streams):
                for u, kb in enumerate(st["kbs"]):
                    c0 = _row0(kb)
                    dv_s[pl.ds(c0, BLK), :] += _dot(jnp.concatenate(_pair_of(wts, tiles, s, u), axis=1), dom_s[st["qi"]])
                    dk_s[pl.ds(c0, BLK), :] += _dot(jnp.concatenate(_pair_of(dzts, tiles, s, u), axis=1), qm_s[st["qi"]])
                    dqas[s] = dqas[s] + _dot(jnp.concatenate(_pair_of(dzbs, tiles, s, u), axis=1), km_s[kb])
            return dqas, rests, gcs

        def group(qis, n_lefts):
            streams = []
            for qi, n in zip(qis, n_lefts):
                o_blk = o_ref[0, pl.ds(_row0(qi), BLK), :]
                doms = [dom_s[qi, 0:BLK, :], dom_s[qi, BLK:2 * BLK, :]]
                st = _stream(qm_s[qi, 0:BLK, :] + qm_s[qi, BLK:2 * BLK, :], qi, n, diag, zero_c)
                st.update(qi=qi, do=doms[0] + doms[1], delta=[_rowsum(d.astype(F32) * o_blk) for d in doms],
                          g_rest=[zero_c, zero_c])
                streams.append(st)
            dqas, rests, gcs = tiles_bwd(streams, [jnp.zeros((BLK, PAIR), F32)] * len(qis))
            for qi, n, st0, dqa, rc, gc in zip(qis, n_lefts, streams, dqas, rests, gcs):

                def sweep(state, n_blocks, st0=st0):
                    kb, rc0, rc1, gc0, gc1, dqa1, _ = state
                    st = dict(st0, kbs=[kb - u for u in range(n_blocks)], masks=[None] * n_blocks, rest=[rc0, rc1],
                              g_rest=[gc0, gc1])
                    (dqa1,), (r,), (g,) = tiles_bwd([st], [dqa1])
                    return kb - n_blocks, r[0], r[1], g[0], g[1], dqa1, jnp.maximum(jnp.max(r[0]), jnp.max(r[1]))

                state = (jnp.asarray(qi - n - 1, jnp.int32), rc[0], rc[1], gc[0], gc[1], dqa,
                         jnp.maximum(jnp.max(rc[0]), jnp.max(rc[1])))
                state = lax.while_loop(lambda t: (t[0] >= 1) & (t[6] >= UNDERFLOW), lambda t: sweep(t, 2), state)
                state = lax.while_loop(lambda t: (t[0] >= 0) & (t[6] >= UNDERFLOW), lambda t: sweep(t, 1), state)
                dq_s[pl.ds(_row0(qi), BLK), :] = state[5] * scale

        n_tog, head, n_groups = _block_groups(nq, Q_TOGETHER_BWD)
        group(head, [min(qi, FIRST_LEFT) for qi in head])

        def groups(g, carry):
            group([g * n_tog + j for j in range(n_tog)], [FIRST_LEFT] * n_tog)
            return carry

        lax.fori_loop(1, n_groups, groups, 0)

        per = math.gcd(PREP_BLOCKS, nq)
        rows = per * BLK

        def finish(i, carry):
            r0 = pl.multiple_of(i * rows, rows)
            dwq, dwk = carry
            out = []
            for src_ref, w_ref, d_s in ((q_ref, qw_ref, dq_s), (k_ref, kw_ref, dk_s)):
                v = src_ref[0, pl.ds(r0, rows), :]
                r = _pair_rms(v, ones2)
                vh = v * r
                dy = d_s[pl.ds(r0, rows), :]
                dvh = dy * w_ref[...]
                out.append((r * (dvh - vh * _pair_mean(dvh * vh, ones2)), _colsum(dy * vh)))
            dq_ref[0, pl.ds(r0, rows), :] = out[0][0].astype(BF16)
            dk_ref[0, pl.ds(r0, rows), :] = out[1][0].astype(BF16)
            dv_ref[0, pl.ds(r0, rows), :] = dv_s[pl.ds(r0, rows), :].astype(BF16)
            return dwq + out[0][1], dwk + out[1][1]

        zrow = jnp.zeros((1, PAIR), F32)
        dwq, dwk = lax.fori_loop(0, nq // per, finish, (zrow, zrow))
        dw_ref[0:1, :] += dwq
        dw_ref[1:2, :] += dwk

    blk = lambda off: pl.BlockSpec((1, L, PAIR), lambda b, p: (b, 0, off + p))
    wspec = pl.BlockSpec((1, PAIR), lambda b, p: (0, 0))
    oblk = pl.BlockSpec((1, L, PAIR), lambda b, p: (b, 0, p))
    return pl.pallas_call(
        body,
        name="sb_attn_bwd",
        grid=(Bl, n_pair),
        in_specs=[blk(0), blk(n_pair), blk(2 * n_pair), oblk, oblk, wspec, wspec],
        out_specs=[oblk, oblk, oblk, pl.BlockSpec((8, PAIR), lambda b, p: (0, 0))],
        out_shape=[jax.ShapeDtypeStruct((Bl, L, D), BF16)] * 3 + [jax.ShapeDtypeStruct((8, PAIR), F32)],
        scratch_shapes=[pltpu.VMEM((nq, 2 * BLK, PAIR), BF16)] * 4 + [pltpu.VMEM((L, PAIR), F32)] * 3,
        compiler_params=_params(("arbitrary", "arbitrary")),
    )(proj3, proj3, proj3, o3, do3, qw2, kw2)


def _conv_pre(ext_s, halo_ref, raw_ref, w_ref, b_ref, first):
    ext_s[0:HALO, :] = jnp.where(first, 0.0, halo_ref[0])
    ext_s[HALO:HALO + BLK, :] = raw_ref[0]
    pre = b_ref[...]
    for i in range(CONV_K):
        pre = pre + ext_s[pl.ds(HALO - (CONV_K - 1 - i), BLK), :] * w_ref[i:i + 1, :]
    return pre


def _lane_col(m, lane, h):
    return _rowsum(jnp.where(lane == h, m, 0.0))


def _half_sums(row, lo1):
    return _rowsum(jnp.where(lo1, row, 0.0)), _rowsum(jnp.where(lo1, 0.0, row))


def _ssd_specs(Bl, L, D, rev):
    nc = L // BLK
    rows_per = BLK // HALO
    cidx = (lambda c: nc - 1 - c) if rev else (lambda c: c)
    xoff = 5
    boff = (6 * D) // 512
    doff = (6 * D + 512) // LANES
    prev = lambda c: jnp.maximum(cidx(c) * rows_per - 1, 0)
    specs = [
        pl.BlockSpec((1, BLK, D), lambda b, c: (b, cidx(c), xoff)),
        pl.BlockSpec((1, BLK, 512), lambda b, c: (b, cidx(c), boff)),
        pl.BlockSpec((1, HALO, D), lambda b, c: (b, prev(c), xoff)),
        pl.BlockSpec((1, HALO, 512), lambda b, c: (b, prev(c), boff)),
        pl.BlockSpec((1, BLK, LANES), lambda b, c: (b, cidx(c), doff)),
    ]
    full = lambda shape: pl.BlockSpec(shape, lambda b, c: (0,) * len(shape))
    specs += [full((CONV_K, D)), full((CONV_K, 512)), full((1, D)), full((1, 512)),
              full((1, LANES)), full((1, LANES)), full((1, LANES))]
    return specs, cidx


def _ssd_common(dtr_ref, dtb_ref, alog_ref, acs_s, acsT_s):
    ltri = jnp.where(_iota((BLK, BLK), 1) <= _iota((BLK, BLK), 0), 1.0, 0.0).astype(BF16)
    dtv = _softplus(dtr_ref[0] + dtb_ref[...])
    a = -jnp.exp(alog_ref[...])
    acs = _dot_split(ltri, dtv * a)
    acs_s[...] = acs
    acsT_s[...] = acs.T
    return dtv, a, acs


def _pair_terms(pr, acs, dtv, acs_s, lane, lo, lane1, lo1):
    h0, h1 = 2 * pr, 2 * pr + 1
    c0, c1 = _lane_col(acs, lane, h0), _lane_col(acs, lane, h1)
    d0, d1 = _lane_col(dtv, lane, h0), _lane_col(dtv, lane, h1)
    lastv = acs_s[BLK - 1:BLK, :]
    l0, l1 = _lane_col(lastv, lane1, h0), _lane_col(lastv, lane1, h1)
    return dict(h=(h0, h1), c=(c0, c1), last=(l0, l1), acs_p=jnp.where(lo, c0, c1), dt_p=jnp.where(lo, d0, d1),
                last_p=jnp.where(lo1, l0, l1))


def _decay_tiles(cc, row, tri, want_t):
    lm = jnp.where(tri, jnp.exp(cc - row), 0.0)
    return lm, (lm.T if want_t else None)


def _ssd_fwd(proj3, cwx, cwb, cbx, cbb, dtb, alog, dsk, D):
    Bl, L, _ = proj3.shape
    nc = L // BLK
    n_pair = D // PAIR
    pairs_per_group = n_pair // SSD_GROUPS
    specs, _ = _ssd_specs(Bl, L, D, False)

    def body(xr_ref, bcr_ref, xh_ref, bch_ref, dtr_ref, cwx_ref, cwb_ref, cbx_ref, cbb_ref, dtb_ref, alog_ref,
             dsk_ref, y_ref, sin_ref, st_s, extx_s, extb_s, acs_s, acsT_s):
        first = pl.program_id(1) == 0

        @pl.when(first)
        def _():
            st_s[...] = jnp.zeros_like(st_s)

        lane, lane1 = _iota((BLK, LANES), 1), _iota((1, LANES), 1)
        lo, lo1 = lane < HEAD_DIM, lane1 < HEAD_DIM
        tri = _iota((BLK, BLK), 1) <= _iota((BLK, BLK), 0)
        pre = _conv_pre(extx_s, xh_ref, xr_ref, cwx_ref, cbx_ref, first)
        ux = pre * _sigmoid(pre)
        pre = _conv_pre(extb_s, bch_ref, bcr_ref, cwb_ref, cbb_ref, first)
        ub = pre * _sigmoid(pre)
        dtv, a, acs = _ssd_common(dtr_ref, dtb_ref, alog_ref, acs_s, acsT_s)
        for g in range(SSD_GROUPS):
            bg = ub[:, g * SSD_STATE:(g + 1) * SSD_STATE]
            cb_ = ub[:, (SSD_GROUPS + g) * SSD_STATE:(SSD_GROUPS + g + 1) * SSD_STATE].astype(BF16)
            cbm = _dot_nt(cb_, bg.astype(BF16))
            btb = bg.T.astype(BF16)
            for pr in range(g * pairs_per_group, (g + 1) * pairs_per_group):
                t = _pair_terms(pr, acs, dtv, acs_s, lane, lo, lane1, lo1)
                xs_p = ux[:, pr * PAIR:(pr + 1) * PAIR]
                x_p = xs_p * t["dt_p"]
                st = st_s[pr]
                sin_ref[0, 0, pr] = st
                y = _dot(cb_, st.astype(BF16)) * jnp.exp(t["acs_p"])
                for k in range(2):
                    row = acsT_s[t["h"][k]:t["h"][k] + 1, :]
                    lm, _ = _decay_tiles(t["c"][k], row, tri, False)
                    xm = jnp.where(lo if k == 0 else ~lo, x_p, 0.0).astype(BF16)
                    y = y + _dot((cbm * lm).astype(BF16), xm)
                d_p = jnp.where(lo1, _lane_col(dsk_ref[...], lane1, t["h"][0]), _lane_col(dsk_ref[...], lane1, t["h"][1]))
                y_ref[0, :, pr * PAIR:(pr + 1) * PAIR] = y + d_p * xs_p
                xd = (x_p * jnp.exp(t["last_p"] - t["acs_p"])).astype(BF16)
                st_s[pr] = st * jnp.exp(t["last_p"]) + _dot(btb, xd)

    return pl.pallas_call(
        body,
        name="ssd_fwd",
        grid=(Bl, nc),
        in_specs=specs,
        out_specs=[
            pl.BlockSpec((1, BLK, D), lambda b, c: (b, c, 0)),
            pl.BlockSpec((1, 1, n_pair, SSD_STATE, PAIR), lambda b, c: (b, c, 0, 0, 0)),
        ],
        out_shape=[jax.ShapeDtypeStruct((Bl, L, D), F32),
                   jax.ShapeDtypeStruct((Bl, nc, n_pair, SSD_STATE, PAIR), F32)],
        scratch_shapes=[pltpu.VMEM((n_pair, SSD_STATE, PAIR), F32), pltpu.VMEM((HALO + BLK, D), F32),
                        pltpu.VMEM((HALO + BLK, 512), F32), pltpu.VMEM((BLK, LANES), F32),
                        pltpu.VMEM((LANES, BLK), F32)],
        compiler_params=_params(("arbitrary", "arbitrary")),
    )(proj3, proj3, proj3, proj3, proj3, cwx, cwb, cbx, cbb, dtb, alog, dsk)


def _ssd_bwd(proj3, s_in, dy3, cwx, cwb, cbx, cbb, dtb, alog, dsk, D, tail):
    Bl, L, _ = proj3.shape
    CD = D + 512
    nc = L // BLK
    n_pair = D // PAIR
    n_heads = 2 * n_pair
    pairs_per_group = n_pair // SSD_GROUPS
    specs, cidx = _ssd_specs(Bl, L, D, True)
    specs = specs + [
        pl.BlockSpec((1, 1, n_pair, SSD_STATE, PAIR), lambda b, c: (b, cidx(c), 0, 0, 0)),
        pl.BlockSpec((1, BLK, D), lambda b, c: (b, cidx(c), 0)),
    ]

    def body(xr_ref, bcr_ref, xh_ref, bch_ref, dtr_ref, cwx_ref, cwb_ref, cbx_ref, cbb_ref, dtb_ref, alog_ref,
             dsk_ref, sin_ref, dy_ref, dxbc_ref, dcwx_ref, dcwb_ref, dcbx_ref, dcbb_ref, misc_ref,
             dst_s, extx_s, extb_s, acs_s, acsT_s, dux_s, dub_s, e2x_s, e2b_s, nxx_s, nxb_s):
        step = pl.program_id(1)
        first = step == nc - 1
        last = step == 0

        @pl.when(last)
        def _():
            dst_s[...] = jnp.zeros_like(dst_s)
            nxx_s[...] = jnp.zeros_like(nxx_s)
            nxb_s[...] = jnp.zeros_like(nxb_s)

        @pl.when(last & (pl.program_id(0) == 0))
        def _():
            for r in (dcwx_ref, dcwb_ref, dcbx_ref, dcbb_ref, misc_ref):
                r[...] = jnp.zeros_like(r)

        lane, lane1 = _iota((BLK, LANES), 1), _iota((1, LANES), 1)
        lo, lo1 = lane < HEAD_DIM, lane1 < HEAD_DIM
        tri = _iota((BLK, BLK), 1) <= _iota((BLK, BLK), 0)
        prex = _conv_pre(extx_s, xh_ref, xr_ref, cwx_ref, cbx_ref, first)
        sgx = _sigmoid(prex)
        ux = prex * sgx
        preb = _conv_pre(extb_s, bch_ref, bcr_ref, cwb_ref, cbb_ref, first)
        sgb = _sigmoid(preb)
        ub = preb * sgb
        dtv, a, acs = _ssd_common(dtr_ref, dtb_ref, alog_ref, acs_s, acsT_s)
        dacs = jnp.zeros((BLK, LANES), F32)
        dlast = jnp.zeros((1, LANES), F32)
        ddt = jnp.zeros((BLK, LANES), F32)
        dd = jnp.zeros((1, LANES), F32)
        for g in range(SSD_GROUPS):
            bg = ub[:, g * SSD_STATE:(g + 1) * SSD_STATE]
            cg = ub[:, (SSD_GROUPS + g) * SSD_STATE:(SSD_GROUPS + g + 1) * SSD_STATE]
            bb, cb_ = bg.astype(BF16), cg.astype(BF16)
            cbm = _dot_nt(cb_, bb)
            cbt = _dot_nt(bb, cb_)
            ctb = cg.T.astype(BF16)
            dbg = jnp.zeros((BLK, SSD_STATE), F32)
            dcg = jnp.zeros((BLK, SSD_STATE), F32)
            for pr in range(g * pairs_per_group, (g + 1) * pairs_per_group):
                t = _pair_terms(pr, acs, dtv, acs_s, lane, lo, lane1, lo1)
                h0, h1 = t["h"]
                xs_p = ux[:, pr * PAIR:(pr + 1) * PAIR]
                dy_p = dy_ref[0, :, pr * PAIR:(pr + 1) * PAIR]
                x_p = xs_p * t["dt_p"]
                ea_p = jnp.exp(t["acs_p"])
                dte_p = jnp.exp(t["last_p"] - t["acs_p"])
                cd_p = jnp.exp(t["last_p"])
                st = sin_ref[0, 0, pr]
                dst = dst_s[pr]
                stb, dstb = st.astype(BF16), dst.astype(BF16)
                s0, s1 = _half_sums(_colsum(dy_p * xs_p), lo1)
                dd = dd + jnp.where(lane1 == h0, s0, 0.0) + jnp.where(lane1 == h1, s1, 0.0)
                d_p = jnp.where(lo1, _lane_col(dsk_ref[...], lane1, h0), _lane_col(dsk_ref[...], lane1, h1))
                dxs_p = d_p * dy_p
                dp = dy_p * ea_p
                dpb = dp.astype(BF16)
                yo = dp * _dot(cb_, stb)
                dcg = dcg + _dot_nt(dpb, stb)
                dst_off = _dot(ctb, dpb)
                dac = [_rowsum(jnp.where(lo, yo, 0.0)), _rowsum(jnp.where(lo, 0.0, yo))]
                s0, s1 = _half_sums(_colsum(dst * st), lo1)
                dl = [s0 * jnp.exp(t["last"][0]), s1 * jnp.exp(t["last"][1])]
                dxd = _dot(bb, dstb)
                dx_p = dxd * dte_p
                tt = dxd * x_p
                dbg = dbg + _dot_nt((x_p * dte_p).astype(BF16), dstb)
                for k, ddte in enumerate((_rowsum(jnp.where(lo, tt, 0.0)), _rowsum(jnp.where(lo, 0.0, tt)))):
                    ek = ddte * jnp.exp(t["last"][k] - t["c"][k])
                    dl[k] = dl[k] + _colsum(ek)
                    dac[k] = dac[k] - ek
                x_pb = x_p.astype(BF16)
                for k in range(2):
                    row = acsT_s[t["h"][k]:t["h"][k] + 1, :]
                    lm, lmt = _decay_tiles(t["c"][k], row, tri, True)
                    dym = jnp.where(lo if k == 0 else ~lo, dy_p, 0.0).astype(BF16)
                    dm = _dot_nt(dym, x_pb)
                    dmt = _dot_nt(x_pb, dym)
                    mt = cbt * lmt
                    dx_p = dx_p + _dot(mt.astype(BF16), dym)
                    dac[k] = dac[k] + _rowsum(dm * (cbm * lm)) - _rowsum(dmt * mt)
                    dcg = dcg + _dot((dm * lm).astype(BF16), bb)
                    dbg = dbg + _dot((dmt * lmt).astype(BF16), cb_)
                dacs = dacs + jnp.where(lane == h0, dac[0], 0.0) + jnp.where(lane == h1, dac[1], 0.0)
                dlast = dlast + jnp.where(lane1 == h0, dl[0], 0.0) + jnp.where(lane1 == h1, dl[1], 0.0)
                dxs_p = dxs_p + dx_p * t["dt_p"]
                t3 = dx_p * xs_p
                ddt = ddt + jnp.where(lane == h0, _rowsum(jnp.where(lo, t3, 0.0)), 0.0) \
                    + jnp.where(lane == h1, _rowsum(jnp.where(lo, 0.0, t3)), 0.0)
                dux_s[:, pr * PAIR:(pr + 1) * PAIR] = dxs_p
                dst_s[pr] = dst * cd_p + dst_off
            dub_s[:, g * SSD_STATE:(g + 1) * SSD_STATE] = dbg
            dub_s[:, (SSD_GROUPS + g) * SSD_STATE:(SSD_GROUPS + g + 1) * SSD_STATE] = dcg
        dacs = dacs + jnp.where(_iota((BLK, LANES), 0) == BLK - 1, dlast, 0.0)
        utri = jnp.where(_iota((BLK, BLK), 1) >= _iota((BLK, BLK), 0), 1.0, 0.0).astype(BF16)
        dda = _dot_split(utri, dacs)
        ddt = ddt + dda * a
        ddtr = jnp.where(lane < n_heads, ddt * _sigmoid(dtr_ref[0] + dtb_ref[...]), 0.0)
        dxbc_ref[0, :, CD:CD + LANES] = ddtr.astype(BF16)
        dxbc_ref[0, :, CD + LANES:tail] = jnp.zeros((BLK, tail - CD - LANES), BF16)
        misc_ref[0:1, :] += _colsum(ddtr)
        misc_ref[1:2, :] += jnp.where(lane1 < n_heads, _colsum(dda * dtv) * a, 0.0)
        misc_ref[2:3, :] += dd
        for (du_s, pre, sg, ext_s, e2_s, nx_s, w_ref, dcw_ref, dcb_ref, c0, width) in (
                (dux_s, prex, sgx, extx_s, e2x_s, nxx_s, cwx_ref, dcwx_ref, dcbx_ref, 0, D),
                (dub_s, preb, sgb, extb_s, e2b_s, nxb_s, cwb_ref, dcwb_ref, dcbb_ref, D, 512)):
            dpre = du_s[...] * (sg * (1.0 + pre * (1.0 - sg)))
            dcb_ref[...] += _colsum(dpre)
            for i in range(CONV_K):
                dcw_ref[i:i + 1, :] += _colsum(dpre * ext_s[pl.ds(HALO - (CONV_K - 1 - i), BLK), :])
            e2_s[0:BLK, :] = dpre
            e2_s[BLK:BLK + HALO, :] = nx_s[...]
            dxr = jnp.zeros((BLK, width), F32)
            for i in range(CONV_K):
                dxr = dxr + e2_s[pl.ds(CONV_K - 1 - i, BLK), :] * w_ref[i:i + 1, :]
            dxbc_ref[0, :, c0:c0 + width] = dxr.astype(BF16)
            nx_s[...] = e2_s[0:HALO, :]

    full = lambda shape: pl.BlockSpec(shape, lambda b, c: (0,) * len(shape))
    return pl.pallas_call(
        body,
        name="ssd_bwd",
        grid=(Bl, nc),
        in_specs=specs,
        out_specs=[
            pl.BlockSpec((1, BLK, tail), lambda b, c: (b, cidx(c), 0)),
            full((CONV_K, D)), full((CONV_K, 512)), full((1, D)), full((1, 512)), full((8, LANES)),
        ],
        out_shape=[
            jax.ShapeDtypeStruct((Bl, L, tail), BF16),
            jax.ShapeDtypeStruct((CONV_K, D), F32), jax.ShapeDtypeStruct((CONV_K, 512), F32),
            jax.ShapeDtypeStruct((1, D), F32), jax.ShapeDtypeStruct((1, 512), F32),
            jax.ShapeDtypeStruct((8, LANES), F32),
        ],
        scratch_shapes=[
            pltpu.VMEM((n_pair, SSD_STATE, PAIR), F32),
            pltpu.VMEM((HALO + BLK, D), F32), pltpu.VMEM((HALO + BLK, 512), F32),
            pltpu.VMEM((BLK, LANES), F32), pltpu.VMEM((LANES, BLK), F32),
            pltpu.VMEM((BLK, D), F32), pltpu.VMEM((BLK, 512), F32),
            pltpu.VMEM((BLK + HALO, D), F32), pltpu.VMEM((BLK + HALO, 512), F32),
            pltpu.VMEM((HALO, D), F32), pltpu.VMEM((HALO, 512), F32),
        ],
        compiler_params=_params(("arbitrary", "arbitrary")),
    )(proj3, proj3, proj3, proj3, proj3, cwx, cwb, cbx, cbb, dtb, alog, dsk, s_in, dy3)


def _gate_out(x2, tgt2, o2, proj2, y2, sbw, ssw, w_out_bf, w_out_t):
    T, D = x2.shape
    tm = min(256, T)

    def body(x_ref, t_ref, o_ref, zs_ref, y_ref, zy_ref, sbw_ref, ssw_ref, wo_ref, wot_ref,
             dout_ref, doutb_ref, mixt_ref, do_ref, dy_ref, dz_ref, dnw_ref, loss_ref):
        @pl.when(pl.program_id(0) == 0)
        def _():
            dnw_ref[...] = jnp.zeros_like(dnw_ref)
            loss_ref[...] = jnp.zeros_like(loss_ref)

        def fwd(o, z, w):
            sg = _sigmoid(z)
            sl = z * sg
            g = o * sl
            r = lax.rsqrt(jnp.mean(g * g, axis=-1, keepdims=True) + EPS)
            n = g * r
            return sg, sl, r, n, n * w

        def bwd(dy, o, z, w, sg, sl, r, n):
            dn = dy * w
            dg = r * (dn - n * jnp.mean(dn * n, axis=-1, keepdims=True))
            return dg * sl, dg * o * (sg * (1.0 + z * (1.0 - sg))), _colsum(dy * n)

        o1, z1, w1 = o_ref[...], zs_ref[...], sbw_ref[...]
        o2_, z2, w2 = y_ref[...], zy_ref[...], ssw_ref[...]
        sg1, sl1, r1, n1, y1 = fwd(o1, z1, w1)
        sg2, sl2, r2, n2, y2_ = fwd(o2_, z2, w2)
        y1b, y2b = y1.astype(BF16), y2_.astype(BF16)
        mixt_ref[0:D, :] = y1.T.astype(BF16)
        mixt_ref[D:2 * D, :] = y2_.T.astype(BF16)
        out = x_ref[...] + (_dot(y1b, wo_ref[0:D, :]) + _dot(y2b, wo_ref[D:2 * D, :]))
        err = out - t_ref[...]
        loss_ref[...] += jnp.sum(err * err) * (0.5 / D)
        dout = err * (1.0 / D)
        dout_ref[...] = dout
        doutb = dout.astype(BF16)
        doutb_ref[...] = doutb
        do1, dz1, dw1 = bwd(_dot(doutb, wot_ref[:, 0:D]), o1, z1, w1, sg1, sl1, r1, n1)
        do2, dz2, dw2 = bwd(_dot(doutb, wot_ref[:, D:2 * D]), o2_, z2, w2, sg2, sl2, r2, n2)
        do_ref[...] = do1
        dy_ref[...] = do2
        dz_ref[:, 0:D] = dz1.astype(BF16)
        dz_ref[:, D:2 * D] = dz2.astype(BF16)
        dnw_ref[0:1, :] += dw1
        dnw_ref[1:2, :] += dw2

    row = lambda col: pl.BlockSpec((tm, D), lambda i: (i, col))
    full = lambda shape: pl.BlockSpec(shape, lambda i: (0,) * len(shape))
    wide = pl.BlockSpec((tm, 2 * D), lambda i: (i, 0))
    return pl.pallas_call(
        body,
        name="gate_out",
        grid=(T // tm,),
        in_specs=[row(0), row(0), row(0), row(3), row(0), row(4), full((1, D)), full((1, D)), full((2 * D, D)),
                  full((D, 2 * D))],
        out_specs=[row(0), row(0), pl.BlockSpec((2 * D, tm), lambda i: (0, i)), row(0), row(0), wide,
                   full((8, D)), full((8, LANES))],
        out_shape=[
            jax.ShapeDtypeStruct((T, D), F32), jax.ShapeDtypeStruct((T, D), BF16),
            jax.ShapeDtypeStruct((2 * D, T), BF16), jax.ShapeDtypeStruct((T, D), F32),
            jax.ShapeDtypeStruct((T, D), F32), jax.ShapeDtypeStruct((T, 2 * D), BF16),
            jax.ShapeDtypeStruct((8, D), F32), jax.ShapeDtypeStruct((8, LANES), F32),
        ],
        compiler_params=_params(("arbitrary",)),
    )(x2, tgt2, o2, proj2, y2, proj2, sbw, ssw, w_out_bf, w_out_t)


def _piece_blocks(pieces, D):
    counts = [p.shape[1] // D for p in pieces]
    return [sum(counts[:i]) for i in range(len(counts))], counts


def _dhn(pieces, w_pad_t, x2, dout, norm_w, h_in, h_out, hb_in, hb_out, slab_off, slab_w):
    T, D = x2.shape
    tm = min(1024, T)
    starts, counts = _piece_blocks(pieces, D)
    units = [p for p, n in enumerate(counts) for _ in range(n)]
    per = 2 if all(units[2 * k] != units[2 * k + 1] for k in range(len(units) // 2)) else 1
    nk = -(-len(units) // per)
    ni = T // tm
    n_sem = 2 * (N_CHIPS - 1)
    assert len(units) * D == w_pad_t.shape[0]

    def body(*refs):
        p_refs = refs[:len(pieces)]
        (wa_ref, wb_ref, x_hbm, dout_hbm, nw_ref, hin, hout, hbin, hbout, gx_ref, dnw_ref, rin, rout, oin, oout,
         acc_s, x_s, dout_s, send_sems, recv_sems, row_sems, own_sems) = refs[len(pieces):]
        i, k = pl.program_id(0), pl.program_id(1)

        def rows():
            r0 = pl.multiple_of(i * tm, tm)
            return [pltpu.make_async_copy(src.at[pl.ds(r0, tm)], dst, row_sems.at[n])
                    for n, (src, dst) in enumerate(((x_hbm, x_s), (dout_hbm, dout_s)))]

        @pl.when(k == 0)
        def _():
            for cp in rows():
                cp.start()

        def scatter():
            x, y, c, chips = _place()

            def slab(ref, p):
                return ref.at[:, pl.ds(pl.multiple_of(p * slab_off, LANES), slab_w)]

            cps = []
            for j, (px, py) in enumerate(chips):
                p = 2 * px + py
                for m, (src, dst) in enumerate(((slab(hbin, p), rin.at[j]), (hbout.at[p], rout.at[j]))):
                    cps.append(pltpu.make_async_remote_copy(
                        src_ref=src, dst_ref=dst, send_sem=send_sems.at[2 * j + m], recv_sem=recv_sems.at[2 * j + m],
                        device_id=(px, py, c), device_id_type=MESH))
            me = 2 * x + y
            own = [pltpu.make_async_copy(slab(hin, me), oin, own_sems.at[0]),
                   pltpu.make_async_copy(hout.at[me], oout, own_sems.at[1])]
            return cps + own

        @pl.when((i == 0) & (k == 0))
        def _():
            for cp in scatter():
                cp.start()

        @pl.when((i == ni - 1) & (k == nk - 1))
        def _():
            for cp in scatter():
                cp.wait()

        @pl.when((i == 0) & (k == 0))
        def _():
            dnw_ref[...] = jnp.zeros_like(dnw_ref)

        for step in range(nk):
            @pl.when(k == step)
            def _(step=step):
                part = sum(_dot(p_refs[units[u]][...], w[...])
                           for u, w in list(zip(range(per * step, per * step + per), (wa_ref, wb_ref))) if u < len(units))
                acc_s[...] = part if step == 0 else acc_s[...] + part

        @pl.when(k == nk - 1)
        def _():
            for cp in rows():
                cp.wait()
            xv = x_s[...]
            r = lax.rsqrt(jnp.mean(xv * xv, axis=-1, keepdims=True) + EPS)
            xh = xv * r
            dhn = acc_s[...]
            dxh = dhn * nw_ref[...]
            gx_ref[...] = dout_s[...] + r * (dxh - xh * jnp.mean(dxh * xh, axis=-1, keepdims=True))
            dnw_ref[0:1, :] += _colsum(dhn * xh)

    return pl.pallas_call(
        body,
        name="dhn",
        grid=(T // tm, nk),
        in_specs=[pl.BlockSpec((tm, D), lambda i, k, s=s, n=n: (i, jnp.clip(per * k - s + (per * k < s), 0, n - 1)))
                  for s, n in zip(starts, counts)] + [
            pl.BlockSpec((D, D), lambda i, k: (jnp.minimum(per * k, len(units) - 1), 0)),
            pl.BlockSpec((D, D), lambda i, k: (jnp.minimum(per * k + per - 1, len(units) - 1), 0)),
            ANY, ANY,
            pl.BlockSpec((1, D), lambda i, k: (0, 0)),
            ANY, ANY, ANY, ANY,
        ],
        out_specs=[pl.BlockSpec((tm, D), lambda i, k: (i, 0)), pl.BlockSpec((8, D), lambda i, k: (0, 0)),
                   ANY, ANY, ANY, ANY],
        out_shape=[jax.ShapeDtypeStruct((T, D), F32), jax.ShapeDtypeStruct((8, D), F32),
                   jax.ShapeDtypeStruct((N_CHIPS - 1, h_in.shape[0], slab_w), BF16),
                   jax.ShapeDtypeStruct((N_CHIPS - 1,) + h_out.shape[1:], BF16),
                   jax.ShapeDtypeStruct((h_in.shape[0], slab_w), F32),
                   jax.ShapeDtypeStruct(h_out.shape[1:], F32)],
        scratch_shapes=[pltpu.VMEM((tm, D), F32)] * 3 + [pltpu.SemaphoreType.DMA((n_sem,)), pltpu.SemaphoreType.DMA((n_sem,)),
                                                      pltpu.SemaphoreType.DMA((2,)), pltpu.SemaphoreType.DMA((2,))],
        compiler_params=_params(("arbitrary", "arbitrary")),
    )(*pieces, w_pad_t, w_pad_t, x2, dout, norm_w, h_in, h_out, hb_in, hb_out)


def _grad_w_in(hn_t, pieces):
    D, T = hn_t.shape
    tk = min(1024, T)
    starts, counts = _piece_blocks(pieces, D)

    def body(*refs):
        a_ref, p_refs, o_ref = refs[0], refs[1:-1], refs[-1]
        j = pl.program_id(0)

        @pl.when(pl.program_id(1) == 0)
        def _():
            o_ref[...] = jnp.zeros_like(o_ref)

        for p_ref, s, n in zip(p_refs, starts, counts):
            @pl.when((j >= s) & (j < s + n))
            def _(p_ref=p_ref):
                o_ref[...] += _dot(a_ref[...], p_ref[...])

    def piece_spec(s, n):
        return pl.BlockSpec((tk, D), lambda j, k: (jnp.where((j >= s) & (j < s + n), k, 0), jnp.clip(j - s, 0, n - 1)))

    return pl.pallas_call(
        body,
        name="grad_w_in",
        grid=(sum(counts), T // tk),
        in_specs=[pl.BlockSpec((D, tk), lambda j, k: (0, k))] + [piece_spec(s, n) for s, n in zip(starts, counts)],
        out_specs=pl.BlockSpec((D, D), lambda j, k: (0, j)),
        out_shape=jax.ShapeDtypeStruct((D, sum(counts) * D), F32),
        compiler_params=_params(("parallel", "arbitrary")),
    )(hn_t, *pieces)


def _grad_w_out(a, b, g_in, width):
    M, K = a.shape
    N = b.shape[1]
    tm = min(1024, M)
    tn = 1024 if N % 1024 == 0 else (512 if N % 512 == 0 else N)
    tk = min(512, K)
    grid = (M // tm, N // tn, K // tk)
    h = g_in.shape[0] // 2

    def body(a_ref, b_ref, gin, o_ref, rin, send_sem, recv_sem):
        ids = [pl.program_id(d) for d in range(3)]

        def swap():
            x, y, c, _ = _place()
            return pltpu.make_async_remote_copy(
                src_ref=gin.at[pl.ds((1 - c) * h, h), pl.ds(0, width)], dst_ref=rin, send_sem=send_sem, recv_sem=recv_sem,
                device_id=(x, y, 1 - c), device_id_type=MESH)

        @pl.when((ids[0] == 0) & (ids[1] == 0) & (ids[2] == 0))
        def _():
            swap().start()

        @pl.when(ids[2] == 0)
        def _():
            o_ref[...] = jnp.zeros_like(o_ref)

        o_ref[...] += _dot(a_ref[...], b_ref[...])

        @pl.when((ids[0] == grid[0] - 1) & (ids[1] == grid[1] - 1) & (ids[2] == grid[2] - 1))
        def _():
            swap().wait()

    return pl.pallas_call(
        body,
        name="grad_w_out",
        grid=grid,
        in_specs=[pl.BlockSpec((tm, tk), lambda i, j, k: (i, k)), pl.BlockSpec((tk, tn), lambda i, j, k: (k, j)), ANY],
        out_specs=[pl.BlockSpec((tm, tn), lambda i, j, k: (i, j)), ANY],
        out_shape=[jax.ShapeDtypeStruct((M, N), F32), jax.ShapeDtypeStruct((h, width), F32)],
        scratch_shapes=[pltpu.SemaphoreType.DMA, pltpu.SemaphoreType.DMA],
        compiler_params=_params(("arbitrary", "arbitrary", "arbitrary")),
    )(a, b, g_in)


def _adamw(w, g, m, v, name):
    R, C = w.shape
    tr = 256 if R % 256 == 0 else R
    tc = LANES if (tr == R and R > 256 and C % LANES == 0) else C
    c1 = 1.0 - ADAM_B1 ** ADAM_STEP
    c2 = 1.0 - ADAM_B2 ** ADAM_STEP

    def body(w_ref, g_ref, m_ref, v_ref, d_ref, nm_ref, nv_ref):
        gv = g_ref[...]
        m_new = ADAM_B1 * m_ref[...] + (1.0 - ADAM_B1) * gv
        v_new = ADAM_B2 * v_ref[...] + (1.0 - ADAM_B2) * (gv * gv)
        d_ref[...] = -ADAM_LR * ((m_new / c1) / (jnp.sqrt(v_new / c2) + ADAM_EPS) + ADAM_WD * w_ref[...])
        nm_ref[...] = m_new
        nv_ref[...] = v_new

    spec = pl.BlockSpec((tr, tc), lambda i, j: (i, j))
    return pl.pallas_call(
        body,
        name=name,
        grid=(R // tr, C // tc),
        in_specs=[spec] * 4,
        out_specs=[spec] * 3,
        out_shape=[jax.ShapeDtypeStruct((R, C), F32)] * 3,
        compiler_params=_params(("parallel", "parallel")),
    )(w, g, m, v)


def _add_core_rows(g, recv, core, name):
    h, width = recv.shape
    th = 128 if h % 128 == 0 else h

    def body(c_ref, g_ref, r_ref, o_ref, ob_ref):
        o_ref[...] = g_ref[...] + r_ref[...]
        ob_ref[...] = o_ref[...].astype(BF16)

    return pl.pallas_call(
        body,
        name=name,
        grid_spec=pltpu.PrefetchScalarGridSpec(
            num_scalar_prefetch=1,
            grid=(h // th,),
            in_specs=[
                pl.BlockSpec((th, width), lambda i, c: (c[0] * (h // th) + i, 0)),
                pl.BlockSpec((th, width), lambda i, c: (i, 0)),
            ],
            out_specs=[pl.BlockSpec((th, width), lambda i, c: (i, 0))] * 2,
        ),
        out_shape=[jax.ShapeDtypeStruct((h, width), F32), jax.ShapeDtypeStruct((h, width), BF16)],
        compiler_params=_params(("parallel",)),
    )(core, g, recv)


def _add_core_blocks(g, recv, core, name):
    n, hb, C = recv.shape

    def body(c_ref, g_ref, r_ref, o_ref, ob_ref):
        o_ref[...] = g_ref[...] + r_ref[...]
        ob_ref[...] = o_ref[...].astype(BF16)

    return pl.pallas_call(
        body,
        name=name,
        grid_spec=pltpu.PrefetchScalarGridSpec(
            num_scalar_prefetch=1,
            grid=(n,),
            in_specs=[
                pl.BlockSpec((hb, C), lambda p, c: (2 * p + c[0], 0)),
                pl.BlockSpec((None, hb, C), lambda p, c: (p, 0, 0)),
            ],
            out_specs=[pl.BlockSpec((None, hb, C), lambda p, c: (p, 0, 0))] * 2,
        ),
        out_shape=[jax.ShapeDtypeStruct((n, hb, C), F32), jax.ShapeDtypeStruct((n, hb, C), BF16)],
        compiler_params=_params(("parallel",)),
    )(core, g, recv)


def _add_chips(own, recv, name):
    h, W = own.shape
    th = 256 if h % 256 == 0 else h

    def body(a_ref, r_ref, o_ref):
        o_ref[...] = ((a_ref[...] + r_ref[0].astype(F32)) + r_ref[1].astype(F32)) + r_ref[2].astype(F32)

    return pl.pallas_call(
        body,
        name=name,
        grid=(h // th,),
        in_specs=[pl.BlockSpec((th, W), lambda i: (i, 0)), pl.BlockSpec((N_CHIPS - 1, th, W), lambda i: (0, i, 0))],
        out_specs=pl.BlockSpec((th, W), lambda i: (i, 0)),
        out_shape=jax.ShapeDtypeStruct((h, W), F32),
        compiler_params=_params(("parallel",)),
    )(own, recv)


def _place():
    x, y, c = lax.axis_index("x"), lax.axis_index("y"), lax.axis_index("c")
    other_chips = [(1 - x, y), (x, 1 - y), (1 - x, 1 - y)]
    return x, y, c, other_chips


def _allgather_w_in(w_in_bf, x2, norm_w):
    S, D = w_in_bf.shape
    T = x2.shape[0]
    tm = min(1024, T)
    ni = T // tm
    n_ici = n_fwd = N_CHIPS - 1

    def body(win, x_ref, nw_ref, gin, hn_ref, hnt_ref, send_sems, recv_sems):
        step = pl.program_id(0)
        xv = x_ref[...]
        hn = xv * lax.rsqrt(jnp.mean(xv * xv, axis=-1, keepdims=True) + EPS) * nw_ref[...]
        hn_ref[...] = hn.astype(BF16)
        hnt_ref[...] = hn.T.astype(BF16)
        x, y, c, chips = _place()
        me = 2 * x + y
        sibling = (x, y, 1 - c)
        hin = D // 2

        def half(chip_idx, core):
            return gin.at[chip_idx, :, pl.ds(core * hin, hin)]

        def rcopy(k, src, dst, to):
            return pltpu.make_async_remote_copy(src_ref=src, dst_ref=dst, send_sem=send_sems.at[k],
                                                recv_sem=recv_sems.at[k], device_id=to, device_id_type=MESH)

        def sends():
            return [rcopy(j, win.at[:, pl.ds(c * hin, hin)], half(me, c), (*chip, c)) for j, chip in enumerate(chips)]

        @pl.when(step == 0)
        def _():
            for cp in sends():
                cp.start()

        @pl.when(step == ni - 1)
        def _():
            passed = []
            for j, (px, py) in enumerate(chips):
                theirs = half(2 * px + py, c)
                rcopy(j, theirs, theirs, sibling).wait_recv()
                passed.append(rcopy(n_ici + j, theirs, theirs, sibling))
                passed[-1].start()
            for j, (px, py) in enumerate(chips):
                other = half(2 * px + py, 1 - c)
                rcopy(n_ici + j, other, other, sibling).wait_recv()
            for cp in sends() + passed:
                cp.wait_send()

    return pl.pallas_call(
        body,
        name="allgather_w_in",
        grid=(ni,),
        in_specs=[ANY, pl.BlockSpec((tm, D), lambda i: (i, 0)), pl.BlockSpec((1, D), lambda i: (0, 0))],
        out_specs=[ANY, pl.BlockSpec((tm, D), lambda i: (i, 0)), pl.BlockSpec((D, tm), lambda i: (0, i))],
        out_shape=[jax.ShapeDtypeStruct((N_CHIPS, S, D), BF16),
                   jax.ShapeDtypeStruct((T, D), BF16), jax.ShapeDtypeStruct((D, T), BF16)],
        scratch_shapes=[pltpu.SemaphoreType.DMA((n_ici + n_fwd,)), pltpu.SemaphoreType.DMA((n_ici + n_fwd,))],
        compiler_params=_params(("arbitrary",)),
    )(w_in_bf, x2, norm_w)


def _allreduce_small(packed):
    R = packed.shape[0]
    n_dev = 2 * N_CHIPS

    def body(p_ref, o_ref, buf, send_sems, recv_sems):
        x, y, c, _ = _place()
        me = 4 * x + 2 * y + c
        buf[me] = p_ref[...]
        copies = []
        for k in range(1, n_dev):
            px = 1 - x if k & 4 else x
            py = 1 - y if k & 2 else y
            pc = 1 - c if k & 1 else c
            copies.append((pltpu.make_async_remote_copy(
                src_ref=buf.at[me], dst_ref=buf.at[me], send_sem=send_sems.at[k - 1], recv_sem=recv_sems.at[k - 1],
                device_id=(px, py, pc), device_id_type=MESH), 4 * px + 2 * py + pc, (px, py, pc)))
        for cp, _, _ in copies:
            cp.start()
        for k, (_, peer, to) in enumerate(copies):
            pltpu.make_async_remote_copy(
                src_ref=buf.at[peer], dst_ref=buf.at[peer], send_sem=send_sems.at[k], recv_sem=recv_sems.at[k],
                device_id=to, device_id_type=MESH).wait_recv()
        for cp, _, _ in copies:
            cp.wait_send()
        acc = buf[0]
        for d in range(1, n_dev):
            acc = acc + buf[d]
        o_ref[...] = acc

    vm = pl.BlockSpec(memory_space=pltpu.VMEM)
    return pl.pallas_call(
        body,
        name="allreduce_small",
        in_specs=[vm],
        out_specs=vm,
        out_shape=jax.ShapeDtypeStruct((R, LANES), F32),
        scratch_shapes=[pltpu.VMEM((n_dev, R, LANES), F32), pltpu.SemaphoreType.DMA((n_dev - 1,)),
                        pltpu.SemaphoreType.DMA((n_dev - 1,))],
    )(packed)


def _swap_core_halves(g_out):
    hb = g_out.shape[0] // (2 * N_CHIPS)

    def body(gout, rout, send_sems, recv_sems):
        x, y, c, _ = _place()
        cps = [pltpu.make_async_remote_copy(
            src_ref=gout.at[pl.ds((2 * p + 1 - c) * hb, hb)], dst_ref=rout.at[p], send_sem=send_sems.at[p],
            recv_sem=recv_sems.at[p], device_id=(x, y, 1 - c), device_id_type=MESH) for p in range(N_CHIPS)]
        for cp in cps:
            cp.start()
        for cp in cps:
            cp.wait()

    return pl.pallas_call(
        body,
        name="reduce_core_swap",
        in_specs=[ANY],
        out_specs=ANY,
        out_shape=jax.ShapeDtypeStruct((N_CHIPS, hb, g_out.shape[1]), F32),
        scratch_shapes=[pltpu.SemaphoreType.DMA((N_CHIPS,)), pltpu.SemaphoreType.DMA((N_CHIPS,))],
    )(g_out)


def _join_core_halves(g_in, g_out):
    def body(gin, gout, fin, fout, send_sems, recv_sems):
        x, y, c, _ = _place()
        cps = [pltpu.make_async_remote_copy(src_ref=s, dst_ref=d.at[c], send_sem=send_sems.at[k],
                                            recv_sem=recv_sems.at[k], device_id=(x, y, 1 - c), device_id_type=MESH)
               for k, (s, d) in enumerate(((gin, fin), (gout, fout)))]
        for cp in cps:
            cp.start()
        for k, (s, d) in enumerate(((gin, fin), (gout, fout))):
            pltpu.make_async_remote_copy(src_ref=s, dst_ref=d.at[1 - c], send_sem=send_sems.at[k],
                                         recv_sem=recv_sems.at[k], device_id=(x, y, 1 - c),
                                         device_id_type=MESH).wait_recv()
        for cp in cps:
            cp.wait_send()

    return pl.pallas_call(
        body,
        name="reduce_core_join",
        in_specs=[ANY, ANY],
        out_specs=[ANY, ANY],
        out_shape=[jax.ShapeDtypeStruct((2,) + g_in.shape, F32), jax.ShapeDtypeStruct((2,) + g_out.shape, F32)],
        scratch_shapes=[pltpu.SemaphoreType.DMA((2,)), pltpu.SemaphoreType.DMA((2,))],
    )(g_in, g_out)


def _pack(arrays):
    rows = []
    for a in arrays:
        flat = a.reshape(-1).astype(F32)
        n = -(-flat.shape[0] // LANES) * LANES
        rows.append(jnp.pad(flat, (0, n - flat.shape[0])).reshape(-1, LANES))
    out = jnp.concatenate(rows, axis=0)
    return jnp.pad(out, ((0, -out.shape[0] % 8), (0, 0)))


def _unpack(packed, shapes):
    out, r = [], 0
    for shp in shapes:
        n = math.prod(shp)
        nr = -(-n // LANES)
        out.append(packed[r:r + nr].reshape(-1)[:n].reshape(shp))
        r += nr
    return out


def _pad_lanes(a):
    return jnp.pad(a, ((0, 0), (0, LANES - a.shape[1])))


def kernel(x, norm_w, w_in, q_norm_w, k_norm_w, conv_w, conv_b, dt_bias, A_log, D_skip, sb_norm_w, ssd_norm_w, w_out, loss_target, m_norm_w, m_w_in, m_q_norm_w, m_k_norm_w, m_conv_w, m_conv_b, m_dt_bias, m_A_log, m_D_skip, m_sb_norm_w, m_ssd_norm_w, m_w_out, v_norm_w, v_w_in, v_q_norm_w, v_k_norm_w, v_conv_w, v_conv_b, v_dt_bias, v_A_log, v_D_skip, v_sb_norm_w, v_ssd_norm_w, v_w_out):
    Bl, L, D = x.shape
    T = Bl * L
    S = w_in.shape[2]
    R = w_out.shape[1]
    CW = conv_w.shape[2]
    n_in = N_CHIPS * S
    CD = D + 2 * SSD_GROUPS * SSD_STATE
    H = D // HEAD_DIM
    n_main = 6 * D + 512
    P = -(-(n_main + LANES) // 1024) * 1024
    assert n_in == n_main + H and CD == N_CHIPS * CW and 2 * D == N_CHIPS * R and CD == D + 512
    chip = (2 * lax.axis_index("x") + lax.axis_index("y")).astype(jnp.int32)
    core = lax.axis_index("c").astype(jnp.int32)

    w_in_t, m_in_t, v_in_t = w_in[0].T, m_w_in[0].T, v_w_in[0].T
    w_in_bf, w_out_shard_bf = w_in_t.astype(BF16), w_out[0].astype(BF16)
    x2 = x.reshape(T, D)
    g_in, hn, hn_t = _allgather_w_in(w_in_bf, x2, norm_w)
    g_in = lax.dynamic_update_slice(g_in, w_in_bf[None], (chip, 0, 0))
    w_pad_t = _stack_shards(g_in, P)
    proj, g_out, g_cw = _inproj(hn, w_pad_t, w_out_shard_bf, conv_w[0])
    g_out = lax.dynamic_update_slice(g_out, w_out_shard_bf[None], (chip, 0, 0))
    g_cw = lax.dynamic_update_slice(g_cw, conv_w, (chip, 0, 0))
    w_out_bf = g_out.reshape(2 * D, D)
    conv_full = g_cw.transpose(1, 0, 2).reshape(CONV_K, CD)
    cwx, cwb = conv_full[:, :D], conv_full[:, D:]
    cbx, cbb = conv_b[:, :D], conv_b[:, D:]
    dtb, alog, dsk = _pad_lanes(dt_bias), _pad_lanes(A_log), _pad_lanes(D_skip)
    qw2, kw2 = jnp.tile(q_norm_w, (1, 2)), jnp.tile(k_norm_w, (1, 2))

    proj3 = proj.reshape(Bl, L, P)
    o_sb = _attn_fwd(proj3, qw2, kw2, D)
    y_ssd, s_in = _ssd_fwd(proj3, cwx, cwb, cbx, cbb, dtb, alog, dsk, D)
    dout, dout_bf, mixed_t, do_sb, dy_ssd, dz_bf, dnw_out, loss_blk = _gate_out(
        x2, loss_target.reshape(T, D), o_sb.reshape(T, D), proj, y_ssd.reshape(T, D), sb_norm_w, ssd_norm_w, w_out_bf,
        w_out_bf.T)

    dq, dk, dv, dqkw = _attn_bwd(proj3, o_sb, do_sb.reshape(Bl, L, D), qw2, kw2, D)
    dtail, dcwx, dcwb, dcbx, dcbb, misc = _ssd_bwd(
        proj3, s_in, dy_ssd.reshape(Bl, L, D), cwx, cwb, cbx, cbb, dtb, alog, dsk, D, P - 5 * D)
    dproj = [dq.reshape(T, D), dk.reshape(T, D), dv.reshape(T, D), dz_bf, dtail.reshape(T, P - 5 * D)]
    gw_in = _grad_w_in(hn_t, dproj)

    slab_off = S // LANES * LANES
    slab_w = -(-(S + (N_CHIPS - 1) * (S - slab_off)) // LANES) * LANES
    width = (N_CHIPS - 1) * slab_off + slab_w
    assert n_in <= width <= P
    core1 = core.reshape(1)
    gw_out, r_in = _grad_w_out(mixed_t, dout_bf, gw_in, width)
    r_out = _swap_core_halves(gw_out)
    h_in, hb_in = _add_core_rows(gw_in, r_in, core1, "sum_cores_w_in")
    h_out, hb_out = _add_core_blocks(gw_out, r_out, core1, "sum_cores_w_out")
    grad_x2, dnw_in, s_in_, s_out_, o_in_, o_out_ = _dhn(dproj, w_pad_t, x2, dout, norm_w, h_in, h_out, hb_in, hb_out,
                                                         slab_off, slab_w)
    gh_in = _add_chips(o_in_, s_in_, "sum_chips_w_in")
    gh_out = _add_chips(o_out_, s_out_, "sum_chips_w_out")
    f_in, f_out = _join_core_halves(gh_in, gh_out)
    g_slab = lax.dynamic_update_slice(f_in, gh_in[None], (core, 0, 0)).reshape(D, slab_w)
    g_w_in = lax.dynamic_slice(g_slab, (0, chip * (S - slab_off)), (D, S))
    g_w_out = lax.dynamic_update_slice(f_out, gh_out[None], (core, 0, 0)).reshape(R, D)

    small_shapes = [(1, D), (1, D), (1, D), (1, CD), (1, HEAD_DIM), (1, HEAD_DIM), (1, H), (1, H), (1, H)]
    g_small_local = [dnw_in[0:1], dnw_out[0:1], dnw_out[1:2], jnp.concatenate([dcbx, dcbb], axis=1),
                     dqkw[0:1, :HEAD_DIM] + dqkw[0:1, HEAD_DIM:], dqkw[1:2, :HEAD_DIM] + dqkw[1:2, HEAD_DIM:],
                     misc[0:1, :H], misc[1:2, :H], misc[2:3, :H]]
    packed = _pack(g_small_local + [jnp.concatenate([dcwx, dcwb], axis=1), loss_blk[0:1, 0:1]])
    red = _allreduce_small(packed)
    g_small = _unpack(red, small_shapes + [(CONV_K, CD), (1, 1)])
    g_conv_w = lax.dynamic_slice_in_dim(g_small[9], chip * CW, CW, axis=1)
    loss = g_small[10][0, 0]

    d_in, nm_in, nv_in = (t.T for t in _adamw(w_in_t, g_w_in.T, m_in_t, v_in_t, "adamw_w_in"))
    d_out, nm_out, nv_out = _adamw(w_out[0], g_w_out, m_w_out[0], v_w_out[0], "adamw_w_out")
    d_cw, nm_cw, nv_cw = _adamw(conv_w[0], g_conv_w, m_conv_w[0], v_conv_w[0], "adamw_conv_w")
    small_w = [norm_w, sb_norm_w, ssd_norm_w, conv_b, q_norm_w, k_norm_w, dt_bias, A_log, D_skip]
    small_m = [m_norm_w, m_sb_norm_w, m_ssd_norm_w, m_conv_b, m_q_norm_w, m_k_norm_w, m_dt_bias, m_A_log, m_D_skip]
    small_v = [v_norm_w, v_sb_norm_w, v_ssd_norm_w, v_conv_b, v_q_norm_w, v_k_norm_w, v_dt_bias, v_A_log, v_D_skip]
    d_s, nm_s, nv_s = _adamw(_pack(small_w), _pack(g_small[:9]), _pack(small_m), _pack(small_v), "adamw_small")
    d_s, nm_s, nv_s = (_unpack(t, small_shapes) for t in (d_s, nm_s, nv_s))

    def ordered(s, w_in_, conv_w_, w_out_):
        return [s[0], w_in_[None], s[4], s[5], conv_w_[None], s[3], s[6], s[7], s[8], s[1], s[2], w_out_[None]]

    return (loss, grad_x2.reshape(Bl, L, D),
            *ordered(g_small[:9], g_w_in, g_conv_w, g_w_out),
            *ordered(d_s, d_in, d_cw, d_out),
            *ordered(nm_s, nm_in, nm_cw, nm_out),
            *ordered(nv_s, nv_in, nv_cw, nv_out))
```

```python
import functools
import math

import jax
import jax.numpy as jnp
from jax import lax
from jax.experimental import pallas as pl
from jax.experimental.pallas import tpu as pltpu

F32 = jnp.float32
BF16 = jnp.bfloat16
EPS = 1e-6
HEAD_DIM = 64
PAIR = 2 * HEAD_DIM
LANES = 128
SSD_STATE = 128
SSD_GROUPS = 2
BLK = 128
PREP_BLOCKS = 16
Q_TOGETHER_FWD = 2
Q_TOGETHER_BWD = 2
FIRST_LEFT = 2
UNDERFLOW = -105.0
CONV_K = 4
HALO = 8
N_CHIPS = 4
ADAM_LR, ADAM_B1, ADAM_B2, ADAM_EPS, ADAM_WD, ADAM_STEP = 0.001, 0.9, 0.999, 1e-08, 0.01, 10
VMEM_LIMIT_V7X = 56 * 1024 * 1024
MESH = pl.DeviceIdType.MESH
ANY = pl.BlockSpec(memory_space=pl.ANY)
NT = (((1,), (1,)), ((), ()))


def _params(sem=None):
    kw = dict(vmem_limit_bytes=VMEM_LIMIT_V7X)
    if sem is not None:
        kw["dimension_semantics"] = sem
    return pltpu.CompilerParams(**kw)


def _dot(a, b):
    return jnp.dot(a, b, preferred_element_type=F32)


def _dot_nt(a, b):
    return lax.dot_general(a, b, NT, preferred_element_type=F32)


def _dot_split(m, x):
    hi = x.astype(BF16)
    lo = (x - hi.astype(F32)).astype(BF16)
    return _dot(m, hi) + _dot(m, lo)


def _iota(shape, dim):
    return lax.broadcasted_iota(jnp.int32, shape, dim)


def _rowsum(x):
    return jnp.sum(x, axis=1, keepdims=True)


def _colsum(x):
    return jnp.sum(x, axis=0, keepdims=True)


def _sigmoid(x):
    return 0.5 * jnp.tanh(0.5 * x) + 0.5


def _softplus(x):
    return jnp.maximum(x, 0.0) + jnp.log(1.0 + jnp.exp(-jnp.abs(x)))


def _stack_shards(g_in, P):
    n, S, D = g_in.shape

    def body(g_ref, o_ref):
        for p in range(n):
            o_ref[p * S:(p + 1) * S, :] = g_ref[p]
        o_ref[n * S:P, :] = jnp.zeros((P - n * S, LANES), BF16)

    return pl.pallas_call(
        body,
        name="stack_w_in",
        grid=(D // LANES,),
        in_specs=[pl.BlockSpec((n, S, LANES), lambda j: (0, 0, j))],
        out_specs=pl.BlockSpec((P, LANES), lambda j: (0, j)),
        out_shape=jax.ShapeDtypeStruct((P, D), BF16),
        compiler_params=_params(("parallel",)),
    )(g_in)


def _inproj(hn, w_pad_t, w_out_bf, conv_w):
    T, D = hn.shape
    P = w_pad_t.shape[0]
    tm = min(1024, T)
    tn = 1024 if P % 1024 == 0 else 512
    ni, nj = T // tm, P // tn
    n_sem = 2 * (N_CHIPS - 1)

    def body(hn_ref, w_ref, wout, cw, proj_ref, gout, gcw, send_sems, recv_sems):
        def gather():
            x, y, c, chips = _place()
            me = 2 * x + y
            return [pltpu.make_async_remote_copy(
                src_ref=src, dst_ref=dst.at[me], send_sem=send_sems.at[2 * j + m], recv_sem=recv_sems.at[2 * j + m],
                device_id=(px, py, c), device_id_type=MESH)
                for j, (px, py) in enumerate(chips) for m, (src, dst) in enumerate(((wout, gout), (cw, gcw)))]

        @pl.when((pl.program_id(0) == 0) & (pl.program_id(1) == 0))
        def _():
            for cp in gather():
                cp.start()

        @pl.when((pl.program_id(0) == ni - 1) & (pl.program_id(1) == nj - 1))
        def _():
            for cp in gather():
                cp.wait()

        proj_ref[...] = _dot_nt(hn_ref[...], w_ref[...])

    return pl.pallas_call(
        body,
        name="inproj",
        grid=(T // tm, P // tn),
        in_specs=[
            pl.BlockSpec((tm, D), lambda i, j: (i, 0)),
            pl.BlockSpec((tn, D), lambda i, j: (j, 0)),
            ANY, ANY,
        ],
        out_specs=[
            pl.BlockSpec((tm, tn), lambda i, j: (i, j)),
            ANY, ANY,
        ],
        out_shape=[jax.ShapeDtypeStruct((T, P), F32),
                   jax.ShapeDtypeStruct((N_CHIPS,) + w_out_bf.shape, BF16),
                   jax.ShapeDtypeStruct((N_CHIPS,) + conv_w.shape, F32)],
        scratch_shapes=[pltpu.SemaphoreType.DMA((n_sem,)), pltpu.SemaphoreType.DMA((n_sem,))],
        compiler_params=_params(("arbitrary", "arbitrary")),
    )(hn, w_pad_t, w_out_bf, conv_w)


def _pair_ones():
    ri = ((_iota((2 * PAIR, PAIR), 0) % PAIR) >= HEAD_DIM).astype(jnp.int32)
    ci = (_iota((2 * PAIR, PAIR), 1) >= HEAD_DIM).astype(jnp.int32)
    return jnp.where(ri == ci, 1.0, 0.0).astype(BF16)


def _pair_rms(v, ones2):
    return lax.rsqrt(_split_dots([v * v], ones2)[0] * (1.0 / HEAD_DIM) + EPS)


def _pair_mean(v, ones2):
    return _split_dots([v], ones2)[0] * (1.0 / HEAD_DIM)


def _suffix_ones():
    ri = _iota((2 * BLK, 2 * BLK), 0) % BLK
    ci = _iota((2 * BLK, 2 * BLK), 1)
    return jnp.where((ci >= BLK) | (ri > ci), 1.0, 0.0).astype(BF16)


def _split_dots(xs, m2):
    his = [x.astype(BF16) for x in xs]
    los = [(x - hi.astype(F32)).astype(BF16) for x, hi in zip(xs, his)]
    return [_dot(jnp.concatenate([hi, lo], axis=1), m2) for hi, lo in zip(his, los)]


def _sb_tiles(streams, km_s, uo):
    tiles = [(s, u, h) for s, st in enumerate(streams) for u in range(len(st["kbs"])) for h in range(2)]
    z2s = {(s, u): _dot_nt(st["q"], km_s[kb]) for s, st in enumerate(streams) for u, kb in enumerate(st["kbs"])}
    zs = [z2s[s, u][:, h * BLK:(h + 1) * BLK] for s, u, h in tiles]
    es = [jnp.exp(-jnp.abs(z)) for z in zs]
    las = [jnp.minimum(z, 0.0) - jnp.log(1.0 + e) for z, e in zip(zs, es)]
    lns = [a - z for a, z in zip(las, zs)]
    masks = [streams[s]["masks"][u] for s, u, h in tiles]
    lks = [lk if m is None else jnp.where(m, lk, 0.0) for m, lk in zip(masks, lns)]
    css = _split_dots(lks, uo)
    rests = [list(st["rest"]) for st in streams]
    ws = []
    for (s, u, h), m, a, cs in zip(tiles, masks, las, css):
        w = jnp.exp(a + rests[s][h] + cs[:, :BLK])
        ws.append(w if m is None else jnp.where(m, w, 0.0))
        rests[s][h] = rests[s][h] + cs[:, BLK:]
    return tiles, las, lns, ws, rests


def _stream(q_pair, qi, n_left, diag, zero):
    return dict(q=q_pair, kbs=[qi - u for u in range(n_left + 1)], masks=[diag] + [None] * n_left, rest=[zero, zero])


def _row0(block):
    return block * BLK if isinstance(block, int) else pl.multiple_of(block * BLK, BLK)


def _pair_of(vals, tiles, s, u):
    return [v for v, t in zip(vals, tiles) if t[0] == s and t[1] == u]


def _block_groups(nq, together):
    n_tog = math.gcd(together, nq)
    assert n_tog >= FIRST_LEFT
    return n_tog, list(range(n_tog)), nq // n_tog


def _attn_prep(src_ref, w_ref, dst_s, n_blocks, scale):
    per = math.gcd(PREP_BLOCKS, n_blocks)
    rows = per * BLK
    lo = _iota((rows, PAIR), 1) < HEAD_DIM
    ones2 = _pair_ones()

    def step(i, carry):
        r0 = pl.multiple_of(i * rows, rows)
        v = src_ref[0, pl.ds(r0, rows), :]
        if w_ref is not None:
            v = v * _pair_rms(v, ones2) * w_ref[...]
        if scale != 1.0:
            v = v * scale
        v0, v1 = jnp.where(lo, v, 0.0).astype(BF16), jnp.where(lo, 0.0, v).astype(BF16)
        for b in range(per):
            dst_s[i * per + b, 0:BLK, :] = v0[b * BLK:(b + 1) * BLK]
            dst_s[i * per + b, BLK:2 * BLK, :] = v1[b * BLK:(b + 1) * BLK]
        return carry

    lax.fori_loop(0, n_blocks // per, step, 0)


def _attn_fwd(proj3, qw2, kw2, D):
    Bl, L, _ = proj3.shape
    n_pair = D // PAIR
    nq = L // BLK
    scale = 1.0 / math.sqrt(HEAD_DIM)

    def body(q_ref, k_ref, v_ref, qw_ref, kw_ref, o_ref, qm_s, km_s, vm_s):
        uo = _suffix_ones()
        diag = _iota((BLK, BLK), 1) < _iota((BLK, BLK), 0)
        _attn_prep(q_ref, qw_ref, qm_s, nq, scale)
        _attn_prep(k_ref, kw_ref, km_s, nq, 1.0)
        _attn_prep(v_ref, None, vm_s, nq, 1.0)

        zero_c = jnp.zeros((BLK, BLK), F32)

        def q_of(qi):
            return qm_s[qi, 0:BLK, :] + qm_s[qi, BLK:2 * BLK, :]

        def values(streams, accs):
            tiles, _, _, ws, rests = _sb_tiles(streams, km_s, uo)
            wbs = [w.astype(BF16) for w in ws]
            accs = list(accs)
            for s, st in enumerate(streams):
                for u, kb in enumerate(st["kbs"]):
                    accs[s] = accs[s] + _dot(jnp.concatenate(_pair_of(wbs, tiles, s, u), axis=1), vm_s[kb])
            return accs, rests

        def group(qis, n_lefts):
            streams = [_stream(q_of(qi), qi, n, diag, zero_c) for qi, n in zip(qis, n_lefts)]
            accs, rests = values(streams, [jnp.zeros((BLK, PAIR), F32)] * len(qis))
            for qi, n, q, acc, rc in zip(qis, n_lefts, [st["q"] for st in streams], accs, rests):

                def sweep(state, n_blocks, q=q):
                    kb, rc0, rc1, acc1, _ = state
                    st = dict(q=q, kbs=[kb - u for u in range(n_blocks)], masks=[None] * n_blocks, rest=[rc0, rc1])
                    (acc1,), (r,) = values([st], [acc1])
                    return kb - n_blocks, r[0], r[1], acc1, jnp.maximum(jnp.max(r[0]), jnp.max(r[1]))

                state = (jnp.asarray(qi - n - 1, jnp.int32), rc[0], rc[1], acc, jnp.maximum(jnp.max(rc[0]), jnp.max(rc[1])))
                state = lax.while_loop(lambda t: (t[0] >= 1) & (t[4] >= UNDERFLOW), lambda t: sweep(t, 2), state)
                state = lax.while_loop(lambda t: (t[0] >= 0) & (t[4] >= UNDERFLOW), lambda t: sweep(t, 1), state)
                o_ref[0, pl.ds(_row0(qi), BLK), :] = state[3]

        n_tog, head, n_groups = _block_groups(nq, Q_TOGETHER_FWD)
        group(head, [min(qi, FIRST_LEFT) for qi in head])

        def groups(g, carry):
            group([g * n_tog + j for j in range(n_tog)], [FIRST_LEFT] * n_tog)
            return carry

        lax.fori_loop(1, n_groups, groups, 0)

    blk = lambda off: pl.BlockSpec((1, L, PAIR), lambda b, p: (b, 0, off + p))
    wspec = pl.BlockSpec((1, PAIR), lambda b, p: (0, 0))
    return pl.pallas_call(
        body,
        name="sb_attn_fwd",
        grid=(Bl, n_pair),
        in_specs=[blk(0), blk(n_pair), blk(2 * n_pair), wspec, wspec],
        out_specs=pl.BlockSpec((1, L, PAIR), lambda b, p: (b, 0, p)),
        out_shape=jax.ShapeDtypeStruct((Bl, L, D), F32),
        scratch_shapes=[pltpu.VMEM((nq, 2 * BLK, PAIR), BF16)] * 3,
        compiler_params=_params(("parallel", "parallel")),
    )(proj3, proj3, proj3, qw2, kw2)


def _attn_bwd(proj3, o3, do3, qw2, kw2, D):
    Bl, L, _ = proj3.shape
    n_pair = D // PAIR
    nq = L // BLK
    scale = 1.0 / math.sqrt(HEAD_DIM)

    def body(q_ref, k_ref, v_ref, o_ref, do_ref, qw_ref, kw_ref, dq_ref, dk_ref, dv_ref, dw_ref,
             qm_s, km_s, vm_s, dom_s, dq_s, dk_s, dv_s):
        uo = _suffix_ones()
        diag = _iota((BLK, BLK), 1) < _iota((BLK, BLK), 0)
        ones2 = _pair_ones()
        _attn_prep(q_ref, qw_ref, qm_s, nq, scale)
        _attn_prep(k_ref, kw_ref, km_s, nq, 1.0)
        _attn_prep(v_ref, None, vm_s, nq, 1.0)
        _attn_prep(do_ref, None, dom_s, nq, 1.0)

        @pl.when((pl.program_id(0) == 0) & (pl.program_id(1) == 0))
        def _():
            dw_ref[...] = jnp.zeros_like(dw_ref)

        def zero(i, carry):
            r0 = pl.multiple_of(i * BLK, BLK)
            dk_s[pl.ds(r0, BLK), :] = jnp.zeros((BLK, PAIR), F32)
            dv_s[pl.ds(r0, BLK), :] = jnp.zeros((BLK, PAIR), F32)
            return carry

        lax.fori_loop(0, nq, zero, 0)

        zero_c = jnp.zeros((BLK, BLK), F32)

        def tiles_bwd(streams, dqas):
            tiles, las, lns, ws, rests = _sb_tiles(streams, km_s, uo)
            dw2s = {(s, u): _dot_nt(st["do"], vm_s[kb]) for s, st in enumerate(streams) for u, kb in enumerate(st["kbs"])}
            dws = [dw2s[s, u][:, h * BLK:(h + 1) * BLK] for s, u, h in tiles]
            wfs = [w.astype(BF16).astype(F32) for w in ws]
            gs = [wf * dw for wf, dw in zip(wfs, dws)]
            gss = _split_dots(gs, uo)
            gcs = [list(st["g_rest"]) for st in streams]
            dzs = []
            for (s, u, h), a, ln, g, gsum in zip(tiles, las, lns, gs, gss):
                g_before = streams[s]["delta"][h] - (gcs[s][h] + gsum[:, :BLK] + g)
                gcs[s][h] = gcs[s][h] + gsum[:, BLK:]
                dz = g * jnp.exp(ln) - g_before * jnp.exp(a)
                m = streams[s]["masks"][u]
                dzs.append(dz if m is None else jnp.where(m, dz, 0.0))
            wts = [wf.T.astype(BF16) for wf in wfs]
            dzts = [dz.T.astype(BF16) for dz in dzs]
            dzbs = [dz.astype(BF16) for dz in dzs]
            dqas = list(dqas)
            for s, st in enumerate(streams):
                for u, kb in enumerate(st["kbs"]):
                    c0 = _row0(kb)
                    dv_s[pl.ds(c0, BLK), :] += _dot(jnp.concatenate(_pair_of(wts, tiles, s, u), axis=1), dom_s[st["qi"]])
                    dk_s[pl.ds(c0, BLK), :] += _dot(jnp.concatenate(_pair_of(dzts, tiles, s, u), axis=1), qm_s[st["qi"]])
                    dqas[s] = dqas[s] + _dot(jnp.concatenate(_pair_of(dzbs, tiles, s, u), axis=1), km_s[kb])
            return dqas, rests, gcs

        def group(qis, n_lefts):
            streams = []
            for qi, n in zip(qis, n_lefts):
                o_blk = o_ref[0, pl.ds(_row0(qi), BLK), :]
                doms = [dom_s[qi, 0:BLK, :], dom_s[qi, BLK:2 * BLK, :]]
                st = _stream(qm_s[qi, 0:BLK, :] + qm_s[qi, BLK:2 * BLK, :], qi, n, diag, zero_c)
                st.update(qi=qi, do=doms[0] + doms[1], delta=[_rowsum(d.astype(F32) * o_blk) for d in doms],
                          g_rest=[zero_c, zero_c])
                streams.append(st)
            dqas, rests, gcs = tiles_bwd(streams, [jnp.zeros((BLK, PAIR), F32)] * len(qis))
            for qi, n, st0, dqa, rc, gc in zip(qis, n_lefts, streams, dqas, rests, gcs):

                def sweep(state, n_blocks, st0=st0):
                    kb, rc0, rc1, gc0, gc1, dqa1, _ = state
                    st = dict(st0, kbs=[kb - u for u in range(n_blocks)], masks=[None] * n_blocks, rest=[rc0, rc1],
                              g_rest=[gc0, gc1])
                    (dqa1,), (r,), (g,) = tiles_bwd([st], [dqa1])
                    return kb - n_blocks, r[0], r[1], g[0], g[1], dqa1, jnp.maximum(jnp.max(r[0]), jnp.max(r[1]))

                state = (jnp.asarray(qi - n - 1, jnp.int32), rc[0], rc[1], gc[0], gc[1], dqa,
                         jnp.maximum(jnp.max(rc[0]), jnp.max(rc[1])))
                state = lax.while_loop(lambda t: (t[0] >= 1) & (t[6] >= UNDERFLOW), lambda t: sweep(t, 2), state)
                state = lax.while_loop(lambda t: (t[0] >= 0) & (t[6] >= UNDERFLOW), lambda t: sweep(t, 1), state)
                dq_s[pl.ds(_row0(qi), BLK), :] = state[5] * scale

        n_tog, head, n_groups = _block_groups(nq, Q_TOGETHER_BWD)
        group(head, [min(qi, FIRST_LEFT) for qi in head])

        def groups(g, carry):
            group([g * n_tog + j for j in range(n_tog)], [FIRST_LEFT] * n_tog)
            return carry

        lax.fori_loop(1, n_groups, groups, 0)

        per = math.gcd(PREP_BLOCKS, nq)
        rows = per * BLK

        def finish(i, carry):
            r0 = pl.multiple_of(i * rows, rows)
            dwq, dwk = carry
            out = []
            for src_ref, w_ref, d_s in ((q_ref, qw_ref, dq_s), (k_ref, kw_ref, dk_s)):
                v = src_ref[0, pl.ds(r0, rows), :]
                r = _pair_rms(v, ones2)
                vh = v * r
                dy = d_s[pl.ds(r0, rows), :]
                dvh = dy * w_ref[...]
                out.append((r * (dvh - vh * _pair_mean(dvh * vh, ones2)), _colsum(dy * vh)))
            dq_ref[0, pl.ds(r0, rows), :] = out[0][0].astype(BF16)
            dk_ref[0, pl.ds(r0, rows), :] = out[1][0].astype(BF16)
            dv_ref[0, pl.ds(r0, rows), :] = dv_s[pl.ds(r0, rows), :].astype(BF16)
            return dwq + out[0][1], dwk + out[1][1]

        zrow = jnp.zeros((1, PAIR), F32)
        dwq, dwk = lax.fori_loop(0, nq // per, finish, (zrow, zrow))
        dw_ref[0:1, :] += dwq
        dw_ref[1:2, :] += dwk

    blk = lambda off: pl.BlockSpec((1, L, PAIR), lambda b, p: (b, 0, off + p))
    wspec = pl.BlockSpec((1, PAIR), lambda b, p: (0, 0))
    oblk = pl.BlockSpec((1, L, PAIR), lambda b, p: (b, 0, p))
    return pl.pallas_call(
        body,
        name="sb_attn_bwd",
        grid=(Bl, n_pair),
        in_specs=[blk(0), blk(n_pair), blk(2 * n_pair), oblk, oblk, wspec, wspec],
        out_specs=[oblk, oblk, oblk, pl.BlockSpec((8, PAIR), lambda b, p: (0, 0))],
        out_shape=[jax.ShapeDtypeStruct((Bl, L, D), BF16)] * 3 + [jax.ShapeDtypeStruct((8, PAIR), F32)],
        scratch_shapes=[pltpu.VMEM((nq, 2 * BLK, PAIR), BF16)] * 4 + [pltpu.VMEM((L, PAIR), F32)] * 3,
        compiler_params=_params(("arbitrary", "arbitrary")),
    )(proj3, proj3, proj3, o3, do3, qw2, kw2)


def _conv_pre(ext_s, halo_ref, raw_ref, w_ref, b_ref, first):
    ext_s[0:HALO, :] = jnp.where(first, 0.0, halo_ref[0])
    ext_s[HALO:HALO + BLK, :] = raw_ref[0]
    pre = b_ref[...]
    for i in range(CONV_K):
        pre = pre + ext_s[pl.ds(HALO - (CONV_K - 1 - i), BLK), :] * w_ref[i:i + 1, :]
    return pre


def _lane_col(m, lane, h):
    return _rowsum(jnp.where(lane == h, m, 0.0))


def _half_sums(row, lo1):
    return _rowsum(jnp.where(lo1, row, 0.0)), _rowsum(jnp.where(lo1, 0.0, row))


def _ssd_specs(Bl, L, D, rev):
    nc = L // BLK
    rows_per = BLK // HALO
    cidx = (lambda c: nc - 1 - c) if rev else (lambda c: c)
    xoff = 5
    boff = (6 * D) // 512
    doff = (6 * D + 512) // LANES
    prev = lambda c: jnp.maximum(cidx(c) * rows_per - 1, 0)
    specs = [
        pl.BlockSpec((1, BLK, D), lambda b, c: (b, cidx(c), xoff)),
        pl.BlockSpec((1, BLK, 512), lambda b, c: (b, cidx(c), boff)),
        pl.BlockSpec((1, HALO, D), lambda b, c: (b, prev(c), xoff)),
        pl.BlockSpec((1, HALO, 512), lambda b, c: (b, prev(c), boff)),
        pl.BlockSpec((1, BLK, LANES), lambda b, c: (b, cidx(c), doff)),
    ]
    full = lambda shape: pl.BlockSpec(shape, lambda b, c: (0,) * len(shape))
    specs += [full((CONV_K, D)), full((CONV_K, 512)), full((1, D)), full((1, 512)),
              full((1, LANES)), full((1, LANES)), full((1, LANES))]
    return specs, cidx


def _ssd_common(dtr_ref, dtb_ref, alog_ref, acs_s, acsT_s):
    ltri = jnp.where(_iota((BLK, BLK), 1) <= _iota((BLK, BLK), 0), 1.0, 0.0).astype(BF16)
    dtv = _softplus(dtr_ref[0] + dtb_ref[...])
    a = -jnp.exp(alog_ref[...])
    acs = _dot_split(ltri, dtv * a)
    acs_s[...] = acs
    acsT_s[...] = acs.T
    return dtv, a, acs


def _pair_terms(pr, acs, dtv, acs_s, lane, lo, lane1, lo1):
    h0, h1 = 2 * pr, 2 * pr + 1
    c0, c1 = _lane_col(acs, lane, h0), _lane_col(acs, lane, h1)
    d0, d1 = _lane_col(dtv, lane, h0), _lane_col(dtv, lane, h1)
    lastv = acs_s[BLK - 1:BLK, :]
    l0, l1 = _lane_col(lastv, lane1, h0), _lane_col(lastv, lane1, h1)
    return dict(h=(h0, h1), c=(c0, c1), last=(l0, l1), acs_p=jnp.where(lo, c0, c1), dt_p=jnp.where(lo, d0, d1),
                last_p=jnp.where(lo1, l0, l1))


def _decay_tiles(cc, row, tri, want_t):
    lm = jnp.where(tri, jnp.exp(cc - row), 0.0)
    return lm, (lm.T if want_t else None)


def _ssd_fwd(proj3, cwx, cwb, cbx, cbb, dtb, alog, dsk, D):
    Bl, L, _ = proj3.shape
    nc = L // BLK
    n_pair = D // PAIR
    pairs_per_group = n_pair // SSD_GROUPS
    specs, _ = _ssd_specs(Bl, L, D, False)

    def body(xr_ref, bcr_ref, xh_ref, bch_ref, dtr_ref, cwx_ref, cwb_ref, cbx_ref, cbb_ref, dtb_ref, alog_ref,
             dsk_ref, y_ref, sin_ref, st_s, extx_s, extb_s, acs_s, acsT_s):
        first = pl.program_id(1) == 0

        @pl.when(first)
        def _():
            st_s[...] = jnp.zeros_like(st_s)

        lane, lane1 = _iota((BLK, LANES), 1), _iota((1, LANES), 1)
        lo, lo1 = lane < HEAD_DIM, lane1 < HEAD_DIM
        tri = _iota((BLK, BLK), 1) <= _iota((BLK, BLK), 0)
        pre = _conv_pre(extx_s, xh_ref, xr_ref, cwx_ref, cbx_ref, first)
        ux = pre * _sigmoid(pre)
        pre = _conv_pre(extb_s, bch_ref, bcr_ref, cwb_ref, cbb_ref, first)
        ub = pre * _sigmoid(pre)
        dtv, a, acs = _ssd_common(dtr_ref, dtb_ref, alog_ref, acs_s, acsT_s)
        for g in range(SSD_GROUPS):
            bg = ub[:, g * SSD_STATE:(g + 1) * SSD_STATE]
            cb_ = ub[:, (SSD_GROUPS + g) * SSD_STATE:(SSD_GROUPS + g + 1) * SSD_STATE].astype(BF16)
            cbm = _dot_nt(cb_, bg.astype(BF16))
            btb = bg.T.astype(BF16)
            for pr in range(g * pairs_per_group, (g + 1) * pairs_per_group):
                t = _pair_terms(pr, acs, dtv, acs_s, lane, lo, lane1, lo1)
                xs_p = ux[:, pr * PAIR:(pr + 1) * PAIR]
                x_p = xs_p * t["dt_p"]
                st = st_s[pr]
                sin_ref[0, 0, pr] = st
                y = _dot(cb_, st.astype(BF16)) * jnp.exp(t["acs_p"])
                for k in range(2):
                    row = acsT_s[t["h"][k]:t["h"][k] + 1, :]
                    lm, _ = _decay_tiles(t["c"][k], row, tri, False)
                    xm = jnp.where(lo if k == 0 else ~lo, x_p, 0.0).astype(BF16)
                    y = y + _dot((cbm * lm).astype(BF16), xm)
                d_p = jnp.where(lo1, _lane_col(dsk_ref[...], lane1, t["h"][0]), _lane_col(dsk_ref[...], lane1, t["h"][1]))
                y_ref[0, :, pr * PAIR:(pr + 1) * PAIR] = y + d_p * xs_p
                xd = (x_p * jnp.exp(t["last_p"] - t["acs_p"])).astype(BF16)
                st_s[pr] = st * jnp.exp(t["last_p"]) + _dot(btb, xd)

    return pl.pallas_call(
        body,
        name="ssd_fwd",
        grid=(Bl, nc),
        in_specs=specs,
        out_specs=[
            pl.BlockSpec((1, BLK, D), lambda b, c: (b, c, 0)),
            pl.BlockSpec((1, 1, n_pair, SSD_STATE, PAIR), lambda b, c: (b, c, 0, 0, 0)),
        ],
        out_shape=[jax.ShapeDtypeStruct((Bl, L, D), F32),
                   jax.ShapeDtypeStruct((Bl, nc, n_pair, SSD_STATE, PAIR), F32)],
        scratch_shapes=[pltpu.VMEM((n_pair, SSD_STATE, PAIR), F32), pltpu.VMEM((HALO + BLK, D), F32),
                        pltpu.VMEM((HALO + BLK, 512), F32), pltpu.VMEM((BLK, LANES), F32),
                        pltpu.VMEM((LANES, BLK), F32)],
        compiler_params=_params(("arbitrary", "arbitrary")),
    )(proj3, proj3, proj3, proj3, proj3, cwx, cwb, cbx, cbb, dtb, alog, dsk)


def _ssd_bwd(proj3, s_in, dy3, cwx, cwb, cbx, cbb, dtb, alog, dsk, D, tail):
    Bl, L, _ = proj3.shape
    CD = D + 512
    nc = L // BLK
    n_pair = D // PAIR
    n_heads = 2 * n_pair
    pairs_per_group = n_pair // SSD_GROUPS
    specs, cidx = _ssd_specs(Bl, L, D, True)
    specs = specs + [
        pl.BlockSpec((1, 1, n_pair, SSD_STATE, PAIR), lambda b, c: (b, cidx(c), 0, 0, 0)),
        pl.BlockSpec((1, BLK, D), lambda b, c: (b, cidx(c), 0)),
    ]

    def body(xr_ref, bcr_ref, xh_ref, bch_ref, dtr_ref, cwx_ref, cwb_ref, cbx_ref, cbb_ref, dtb_ref, alog_ref,
             dsk_ref, sin_ref, dy_ref, dxbc_ref, dcwx_ref, dcwb_ref, dcbx_ref, dcbb_ref, misc_ref,
             dst_s, extx_s, extb_s, acs_s, acsT_s, dux_s, dub_s, e2x_s, e2b_s, nxx_s, nxb_s):
        step = pl.program_id(1)
        first = step == nc - 1
        last = step == 0

        @pl.when(last)
        def _():
            dst_s[...] = jnp.zeros_like(dst_s)
            nxx_s[...] = jnp.zeros_like(nxx_s)
            nxb_s[...] = jnp.zeros_like(nxb_s)

        @pl.when(last & (pl.program_id(0) == 0))
        def _():
            for r in (dcwx_ref, dcwb_ref, dcbx_ref, dcbb_ref, misc_ref):
                r[...] = jnp.zeros_like(r)

        lane, lane1 = _iota((BLK, LANES), 1), _iota((1, LANES), 1)
        lo, lo1 = lane < HEAD_DIM, lane1 < HEAD_DIM
        tri = _iota((BLK, BLK), 1) <= _iota((BLK, BLK), 0)
        prex = _conv_pre(extx_s, xh_ref, xr_ref, cwx_ref, cbx_ref, first)
        sgx = _sigmoid(prex)
        ux = prex * sgx
        preb = _conv_pre(extb_s, bch_ref, bcr_ref, cwb_ref, cbb_ref, first)
        sgb = _sigmoid(preb)
        ub = preb * sgb
        dtv, a, acs = _ssd_common(dtr_ref, dtb_ref, alog_ref, acs_s, acsT_s)
        dacs = jnp.zeros((BLK, LANES), F32)
        dlast = jnp.zeros((1, LANES), F32)
        ddt = jnp.zeros((BLK, LANES), F32)
        dd = jnp.zeros((1, LANES), F32)
        for g in range(SSD_GROUPS):
            bg = ub[:, g * SSD_STATE:(g + 1) * SSD_STATE]
            cg = ub[:, (SSD_GROUPS + g) * SSD_STATE:(SSD_GROUPS + g + 1) * SSD_STATE]
            bb, cb_ = bg.astype(BF16), cg.astype(BF16)
            cbm = _dot_nt(cb_, bb)
            cbt = _dot_nt(bb, cb_)
            ctb = cg.T.astype(BF16)
            dbg = jnp.zeros((BLK, SSD_STATE), F32)
            dcg = jnp.zeros((BLK, SSD_STATE), F32)
            for pr in range(g * pairs_per_group, (g + 1) * pairs_per_group):
                t = _pair_terms(pr, acs, dtv, acs_s, lane, lo, lane1, lo1)
                h0, h1 = t["h"]
                xs_p = ux[:, pr * PAIR:(pr + 1) * PAIR]
                dy_p = dy_ref[0, :, pr * PAIR:(pr + 1) * PAIR]
                x_p = xs_p * t["dt_p"]
                ea_p = jnp.exp(t["acs_p"])
                dte_p = jnp.exp(t["last_p"] - t["acs_p"])
                cd_p = jnp.exp(t["last_p"])
                st = sin_ref[0, 0, pr]
                dst = dst_s[pr]
                stb, dstb = st.astype(BF16), dst.astype(BF16)
                s0, s1 = _half_sums(_colsum(dy_p * xs_p), lo1)
                dd = dd + jnp.where(lane1 == h0, s0, 0.0) + jnp.where(lane1 == h1, s1, 0.0)
                d_p = jnp.where(lo1, _lane_col(dsk_ref[...], lane1, h0), _lane_col(dsk_ref[...], lane1, h1))
                dxs_p = d_p * dy_p
                dp = dy_p * ea_p
                dpb = dp.astype(BF16)
                yo = dp * _dot(cb_, stb)
                dcg = dcg + _dot_nt(dpb, stb)
                dst_off = _dot(ctb, dpb)
                dac = [_rowsum(jnp.where(lo, yo, 0.0)), _rowsum(jnp.where(lo, 0.0, yo))]
                s0, s1 = _half_sums(_colsum(dst * st), lo1)
                dl = [s0 * jnp.exp(t["last"][0]), s1 * jnp.exp(t["last"][1])]
                dxd = _dot(bb, dstb)
                dx_p = dxd * dte_p
                tt = dxd * x_p
                dbg = dbg + _dot_nt((x_p * dte_p).astype(BF16), dstb)
                for k, ddte in enumerate((_rowsum(jnp.where(lo, tt, 0.0)), _rowsum(jnp.where(lo, 0.0, tt)))):
                    ek = ddte * jnp.exp(t["last"][k] - t["c"][k])
                    dl[k] = dl[k] + _colsum(ek)
                    dac[k] = dac[k] - ek
                x_pb = x_p.astype(BF16)
                for k in range(2):
                    row = acsT_s[t["h"][k]:t["h"][k] + 1, :]
                    lm, lmt = _decay_tiles(t["c"][k], row, tri, True)
                    dym = jnp.where(lo if k == 0 else ~lo, dy_p, 0.0).astype(BF16)
                    dm = _dot_nt(dym, x_pb)
                    dmt = _dot_nt(x_pb, dym)
                    mt = cbt * lmt
                    dx_p = dx_p + _dot(mt.astype(BF16), dym)
                    dac[k] = dac[k] + _rowsum(dm * (cbm * lm)) - _rowsum(dmt * mt)
                    dcg = dcg + _dot((dm * lm).astype(BF16), bb)
                    dbg = dbg + _dot((dmt * lmt).astype(BF16), cb_)
                dacs = dacs + jnp.where(lane == h0, dac[0], 0.0) + jnp.where(lane == h1, dac[1], 0.0)
                dlast = dlast + jnp.where(lane1 == h0, dl[0], 0.0) + jnp.where(lane1 == h1, dl[1], 0.0)
                dxs_p = dxs_p + dx_p * t["dt_p"]
                t3 = dx_p * xs_p
                ddt = ddt + jnp.where(lane == h0, _rowsum(jnp.where(lo, t3, 0.0)), 0.0) \
                    + jnp.where(lane == h1, _rowsum(jnp.where(lo, 0.0, t3)), 0.0)
                dux_s[:, pr * PAIR:(pr + 1) * PAIR] = dxs_p
                dst_s[pr] = dst * cd_p + dst_off
            dub_s[:, g * SSD_STATE:(g + 1) * SSD_STATE] = dbg
            dub_s[:, (SSD_GROUPS + g) * SSD_STATE:(SSD_GROUPS + g + 1) * SSD_STATE] = dcg
        dacs = dacs + jnp.where(_iota((BLK, LANES), 0) == BLK - 1, dlast, 0.0)
        utri = jnp.where(_iota((BLK, BLK), 1) >= _iota((BLK, BLK), 0), 1.0, 0.0).astype(BF16)
        dda = _dot_split(utri, dacs)
        ddt = ddt + dda * a
        ddtr = jnp.where(lane < n_heads, ddt * _sigmoid(dtr_ref[0] + dtb_ref[...]), 0.0)
        dxbc_ref[0, :, CD:CD + LANES] = ddtr.astype(BF16)
        dxbc_ref[0, :, CD + LANES:tail] = jnp.zeros((BLK, tail - CD - LANES), BF16)
        misc_ref[0:1, :] += _colsum(ddtr)
        misc_ref[1:2, :] += jnp.where(lane1 < n_heads, _colsum(dda * dtv) * a, 0.0)
        misc_ref[2:3, :] += dd
        for (du_s, pre, sg, ext_s, e2_s, nx_s, w_ref, dcw_ref, dcb_ref, c0, width) in (
                (dux_s, prex, sgx, extx_s, e2x_s, nxx_s, cwx_ref, dcwx_ref, dcbx_ref, 0, D),
                (dub_s, preb, sgb, extb_s, e2b_s, nxb_s, cwb_ref, dcwb_ref, dcbb_ref, D, 512)):
            dpre = du_s[...] * (sg * (1.0 + pre * (1.0 - sg)))
            dcb_ref[...] += _colsum(dpre)
            for i in range(CONV_K):
                dcw_ref[i:i + 1, :] += _colsum(dpre * ext_s[pl.ds(HALO - (CONV_K - 1 - i), BLK), :])
            e2_s[0:BLK, :] = dpre
            e2_s[BLK:BLK + HALO, :] = nx_s[...]
            dxr = jnp.zeros((BLK, width), F32)
            for i in range(CONV_K):
                dxr = dxr + e2_s[pl.ds(CONV_K - 1 - i, BLK), :] * w_ref[i:i + 1, :]
            dxbc_ref[0, :, c0:c0 + width] = dxr.astype(BF16)
            nx_s[...] = e2_s[0:HALO, :]

    full = lambda shape: pl.BlockSpec(shape, lambda b, c: (0,) * len(shape))
    return pl.pallas_call(
        body,
        name="ssd_bwd",
        grid=(Bl, nc),
        in_specs=specs,
        out_specs=[
            pl.BlockSpec((1, BLK, tail), lambda b, c: (b, cidx(c), 0)),
            full((CONV_K, D)), full((CONV_K, 512)), full((1, D)), full((1, 512)), full((8, LANES)),
        ],
        out_shape=[
            jax.ShapeDtypeStruct((Bl, L, tail), BF16),
            jax.ShapeDtypeStruct((CONV_K, D), F32), jax.ShapeDtypeStruct((CONV_K, 512), F32),
            jax.ShapeDtypeStruct((1, D), F32), jax.ShapeDtypeStruct((1, 512), F32),
            jax.ShapeDtypeStruct((8, LANES), F32),
        ],
        scratch_shapes=[
            pltpu.VMEM((n_pair, SSD_STATE, PAIR), F32),
            pltpu.VMEM((HALO + BLK, D), F32), pltpu.VMEM((HALO + BLK, 512), F32),
            pltpu.VMEM((BLK, LANES), F32), pltpu.VMEM((LANES, BLK), F32),
            pltpu.VMEM((BLK, D), F32), pltpu.VMEM((BLK, 512), F32),
            pltpu.VMEM((BLK + HALO, D), F32), pltpu.VMEM((BLK + HALO, 512), F32),
            pltpu.VMEM((HALO, D), F32), pltpu.VMEM((HALO, 512), F32),
        ],
        compiler_params=_params(("arbitrary", "arbitrary")),
    )(proj3, proj3, proj3, proj3, proj3, cwx, cwb, cbx, cbb, dtb, alog, dsk, s_in, dy3)


def _gate_out(x2, tgt2, o2, proj2, y2, sbw, ssw, w_out_bf, w_out_t):
    T, D = x2.shape
    tm = min(256, T)

    def body(x_ref, t_ref, o_ref, zs_ref, y_ref, zy_ref, sbw_ref, ssw_ref, wo_ref, wot_ref,
             dout_ref, doutb_ref, mixt_ref, do_ref, dy_ref, dz_ref, dnw_ref, loss_ref):
        @pl.when(pl.program_id(0) == 0)
        def _():
            dnw_ref[...] = jnp.zeros_like(dnw_ref)
            loss_ref[...] = jnp.zeros_like(loss_ref)

        def fwd(o, z, w):
            sg = _sigmoid(z)
            sl = z * sg
            g = o * sl
            r = lax.rsqrt(jnp.mean(g * g, axis=-1, keepdims=True) + EPS)
            n = g * r
            return sg, sl, r, n, n * w

        def bwd(dy, o, z, w, sg, sl, r, n):
            dn = dy * w
            dg = r * (dn - n * jnp.mean(dn * n, axis=-1, keepdims=True))
            return dg * sl, dg * o * (sg * (1.0 + z * (1.0 - sg))), _colsum(dy * n)

        o1, z1, w1 = o_ref[...], zs_ref[...], sbw_ref[...]
        o2_, z2, w2 = y_ref[...], zy_ref[...], ssw_ref[...]
        sg1, sl1, r1, n1, y1 = fwd(o1, z1, w1)
        sg2, sl2, r2, n2, y2_ = fwd(o2_, z2, w2)
        y1b, y2b = y1.astype(BF16), y2_.astype(BF16)
        mixt_ref[0:D, :] = y1.T.astype(BF16)
        mixt_ref[D:2 * D, :] = y2_.T.astype(BF16)
        out = x_ref[...] + (_dot(y1b, wo_ref[0:D, :]) + _dot(y2b, wo_ref[D:2 * D, :]))
        err = out - t_ref[...]
        loss_ref[...] += jnp.sum(err * err) * (0.5 / D)
        dout = err * (1.0 / D)
        dout_ref[...] = dout
        doutb = dout.astype(BF16)
        doutb_ref[...] = doutb
        do1, dz1, dw1 = bwd(_dot(doutb, wot_ref[:, 0:D]), o1, z1, w1, sg1, sl1, r1, n1)
        do2, dz2, dw2 = bwd(_dot(doutb, wot_ref[:, D:2 * D]), o2_, z2, w2, sg2, sl2, r2, n2)
        do_ref[...] = do1
        dy_ref[...] = do2
        dz_ref[:, 0:D] = dz1.astype(BF16)
        dz_ref[:, D:2 * D] = dz2.astype(BF16)
        dnw_ref[0:1, :] += dw1
        dnw_ref[1:2, :] += dw2

    row = lambda col: pl.BlockSpec((tm, D), lambda i: (i, col))
    full = lambda shape: pl.BlockSpec(shape, lambda i: (0,) * len(shape))
    wide = pl.BlockSpec((tm, 2 * D), lambda i: (i, 0))
    return pl.pallas_call(
        body,
        name="gate_out",
        grid=(T // tm,),
        in_specs=[row(0), row(0), row(0), row(3), row(0), row(4), full((1, D)), full((1, D)), full((2 * D, D)),
                  full((D, 2 * D))],
        out_specs=[row(0), row(0), pl.BlockSpec((2 * D, tm), lambda i: (0, i)), row(0), row(0), wide,
                   full((8, D)), full((8, LANES))],
        out_shape=[
            jax.ShapeDtypeStruct((T, D), F32), jax.ShapeDtypeStruct((T, D), BF16),
            jax.ShapeDtypeStruct((2 * D, T), BF16), jax.ShapeDtypeStruct((T, D), F32),
            jax.ShapeDtypeStruct((T, D), F32), jax.ShapeDtypeStruct((T, 2 * D), BF16),
            jax.ShapeDtypeStruct((8, D), F32), jax.ShapeDtypeStruct((8, LANES), F32),
        ],
        compiler_params=_params(("arbitrary",)),
    )(x2, tgt2, o2, proj2, y2, proj2, sbw, ssw, w_out_bf, w_out_t)


def _piece_blocks(pieces, D):
    counts = [p.shape[1] // D for p in pieces]
    return [sum(counts[:i]) for i in range(len(counts))], counts


def _dhn(pieces, w_pad_t, x2, dout, norm_w, h_in, h_out, hb_in, hb_out, slab_off, slab_w):
    T, D = x2.shape
    tm = min(1024, T)
    starts, counts = _piece_blocks(pieces, D)
    units = [p for p, n in enumerate(counts) for _ in range(n)]
    per = 2 if all(units[2 * k] != units[2 * k + 1] for k in range(len(units) // 2)) else 1
    nk = -(-len(units) // per)
    ni = T // tm
    n_sem = 2 * (N_CHIPS - 1)
    assert len(units) * D == w_pad_t.shape[0]

    def body(*refs):
        p_refs = refs[:len(pieces)]
        (wa_ref, wb_ref, x_hbm, dout_hbm, nw_ref, hin, hout, hbin, hbout, gx_ref, dnw_ref, rin, rout, oin, oout,
         acc_s, x_s, dout_s, send_sems, recv_sems, row_sems, own_sems) = refs[len(pieces):]
        i, k = pl.program_id(0), pl.program_id(1)

        def rows():
            r0 = pl.multiple_of(i * tm, tm)
            return [pltpu.make_async_copy(src.at[pl.ds(r0, tm)], dst, row_sems.at[n])
                    for n, (src, dst) in enumerate(((x_hbm, x_s), (dout_hbm, dout_s)))]

        @pl.when(k == 0)
        def _():
            for cp in rows():
                cp.start()

        def scatter():
            x, y, c, chips = _place()

            def slab(ref, p):
                return ref.at[:, pl.ds(pl.multiple_of(p * slab_off, LANES), slab_w)]

            cps = []
            for j, (px, py) in enumerate(chips):
                p = 2 * px + py
                for m, (src, dst) in enumerate(((slab(hbin, p), rin.at[j]), (hbout.at[p], rout.at[j]))):
                    cps.append(pltpu.make_async_remote_copy(
                        src_ref=src, dst_ref=dst, send_sem=send_sems.at[2 * j + m], recv_sem=recv_sems.at[2 * j + m],
                        device_id=(px, py, c), device_id_type=MESH))
            me = 2 * x + y
            own = [pltpu.make_async_copy(slab(hin, me), oin, own_sems.at[0]),
                   pltpu.make_async_copy(hout.at[me], oout, own_sems.at[1])]
            return cps + own

        @pl.when((i == 0) & (k == 0))
        def _():
            for cp in scatter():
                cp.start()

        @pl.when((i == ni - 1) & (k == nk - 1))
        def _():
            for cp in scatter():
                cp.wait()

        @pl.when((i == 0) & (k == 0))
        def _():
            dnw_ref[...] = jnp.zeros_like(dnw_ref)

        for step in range(nk):
            @pl.when(k == step)
            def _(step=step):
                part = sum(_dot(p_refs[units[u]][...], w[...])
                           for u, w in list(zip(range(per * step, per * step + per), (wa_ref, wb_ref))) if u < len(units))
                acc_s[...] = part if step == 0 else acc_s[...] + part

        @pl.when(k == nk - 1)
        def _():
            for cp in rows():
                cp.wait()
            xv = x_s[...]
            r = lax.rsqrt(jnp.mean(xv * xv, axis=-1, keepdims=True) + EPS)
            xh = xv * r
            dhn = acc_s[...]
            dxh = dhn * nw_ref[...]
            gx_ref[...] = dout_s[...] + r * (dxh - xh * jnp.mean(dxh * xh, axis=-1, keepdims=True))
            dnw_ref[0:1, :] += _colsum(dhn * xh)

    return pl.pallas_call(
        body,
        name="dhn",
        grid=(T // tm, nk),
        in_specs=[pl.BlockSpec((tm, D), lambda i, k, s=s, n=n: (i, jnp.clip(per * k - s + (per * k < s), 0, n - 1)))
                  for s, n in zip(starts, counts)] + [
            pl.BlockSpec((D, D), lambda i, k: (jnp.minimum(per * k, len(units) - 1), 0)),
            pl.BlockSpec((D, D), lambda i, k: (jnp.minimum(per * k + per - 1, len(units) - 1), 0)),
            ANY, ANY,
            pl.BlockSpec((1, D), lambda i, k: (0, 0)),
            ANY, ANY, ANY, ANY,
        ],
        out_specs=[pl.BlockSpec((tm, D), lambda i, k: (i, 0)), pl.BlockSpec((8, D), lambda i, k: (0, 0)),
                   ANY, ANY, ANY, ANY],
        out_shape=[jax.ShapeDtypeStruct((T, D), F32), jax.ShapeDtypeStruct((8, D), F32),
                   jax.ShapeDtypeStruct((N_CHIPS - 1, h_in.shape[0], slab_w), BF16),
                   jax.ShapeDtypeStruct((N_CHIPS - 1,) + h_out.shape[1:], BF16),
                   jax.ShapeDtypeStruct((h_in.shape[0], slab_w), F32),
                   jax.ShapeDtypeStruct(h_out.shape[1:], F32)],
        scratch_shapes=[pltpu.VMEM((tm, D), F32)] * 3 + [pltpu.SemaphoreType.DMA((n_sem,)), pltpu.SemaphoreType.DMA((n_sem,)),
                                                      pltpu.SemaphoreType.DMA((2,)), pltpu.SemaphoreType.DMA((2,))],
        compiler_params=_params(("arbitrary", "arbitrary")),
    )(*pieces, w_pad_t, w_pad_t, x2, dout, norm_w, h_in, h_out, hb_in, hb_out)


def _grad_w_in(hn_t, pieces):
    D, T = hn_t.shape
    tk = min(1024, T)
    starts, counts = _piece_blocks(pieces, D)

    def body(*refs):
        a_ref, p_refs, o_ref = refs[0], refs[1:-1], refs[-1]
        j = pl.program_id(0)

        @pl.when(pl.program_id(1) == 0)
        def _():
            o_ref[...] = jnp.zeros_like(o_ref)

        for p_ref, s, n in zip(p_refs, starts, counts):
            @pl.when((j >= s) & (j < s + n))
            def _(p_ref=p_ref):
                o_ref[...] += _dot(a_ref[...], p_ref[...])

    def piece_spec(s, n):
        return pl.BlockSpec((tk, D), lambda j, k: (jnp.where((j >= s) & (j < s + n), k, 0), jnp.clip(j - s, 0, n - 1)))

    return pl.pallas_call(
        body,
        name="grad_w_in",
        grid=(sum(counts), T // tk),
        in_specs=[pl.BlockSpec((D, tk), lambda j, k: (0, k))] + [piece_spec(s, n) for s, n in zip(starts, counts)],
        out_specs=pl.BlockSpec((D, D), lambda j, k: (0, j)),
        out_shape=jax.ShapeDtypeStruct((D, sum(counts) * D), F32),
        compiler_params=_params(("parallel", "arbitrary")),
    )(hn_t, *pieces)


def _grad_w_out(a, b, g_in, width):
    M, K = a.shape
    N = b.shape[1]
    tm = min(1024, M)
    tn = 1024 if N % 1024 == 0 else (512 if N % 512 == 0 else N)
    tk = min(512, K)
    grid = (M // tm, N // tn, K // tk)
    h = g_in.shape[0] // 2

    def body(a_ref, b_ref, gin, o_ref, rin, send_sem, recv_sem):
        ids = [pl.program_id(d) for d in range(3)]

        def swap():
            x, y, c, _ = _place()
            return pltpu.make_async_remote_copy(
                src_ref=gin.at[pl.ds((1 - c) * h, h), pl.ds(0, width)], dst_ref=rin, send_sem=send_sem, recv_sem=recv_sem,
                device_id=(x, y, 1 - c), device_id_type=MESH)

        @pl.when((ids[0] == 0) & (ids[1] == 0) & (ids[2] == 0))
        def _():
            swap().start()

        @pl.when(ids[2] == 0)
        def _():
            o_ref[...] = jnp.zeros_like(o_ref)

        o_ref[...] += _dot(a_ref[...], b_ref[...])

        @pl.when((ids[0] == grid[0] - 1) & (ids[1] == grid[1] - 1) & (ids[2] == grid[2] - 1))
        def _():
            swap().wait()

    return pl.pallas_call(
        body,
        name="grad_w_out",
        grid=grid,
        in_specs=[pl.BlockSpec((tm, tk), lambda i, j, k: (i, k)), pl.BlockSpec((tk, tn), lambda i, j, k: (k, j)), ANY],
        out_specs=[pl.BlockSpec((tm, tn), lambda i, j, k: (i, j)), ANY],
        out_shape=[jax.ShapeDtypeStruct((M, N), F32), jax.ShapeDtypeStruct((h, width), F32)],
        scratch_shapes=[pltpu.SemaphoreType.DMA, pltpu.SemaphoreType.DMA],
        compiler_params=_params(("arbitrary", "arbitrary", "arbitrary")),
    )(a, b, g_in)


def _adamw(w, g, m, v, name):
    R, C = w.shape
    tr = 256 if R % 256 == 0 else R
    tc = LANES if (tr == R and R > 256 and C % LANES == 0) else C
    c1 = 1.0 - ADAM_B1 ** ADAM_STEP
    c2 = 1.0 - ADAM_B2 ** ADAM_STEP

    def body(w_ref, g_ref, m_ref, v_ref, d_ref, nm_ref, nv_ref):
        gv = g_ref[...]
        m_new = ADAM_B1 * m_ref[...] + (1.0 - ADAM_B1) * gv
        v_new = ADAM_B2 * v_ref[...] + (1.0 - ADAM_B2) * (gv * gv)
        d_ref[...] = -ADAM_LR * ((m_new / c1) / (jnp.sqrt(v_new / c2) + ADAM_EPS) + ADAM_WD * w_ref[...])
        nm_ref[...] = m_new
        nv_ref[...] = v_new

    spec = pl.BlockSpec((tr, tc), lambda i, j: (i, j))
    return pl.pallas_call(
        body,
        name=name,
        grid=(R // tr, C // tc),
        in_specs=[spec] * 4,
        out_specs=[spec] * 3,
        out_shape=[jax.ShapeDtypeStruct((R, C), F32)] * 3,
        compiler_params=_params(("parallel", "parallel")),
    )(w, g, m, v)


def _add_core_rows(g, recv, core, name):
    h, width = recv.shape
    th = 128 if h % 128 == 0 else h

    def body(c_ref, g_ref, r_ref, o_ref, ob_ref):
        o_ref[...] = g_ref[...] + r_ref[...]
        ob_ref[...] = o_ref[...].astype(BF16)

    return pl.pallas_call(
        body,
        name=name,
        grid_spec=pltpu.PrefetchScalarGridSpec(
            num_scalar_prefetch=1,
            grid=(h // th,),
            in_specs=[
                pl.BlockSpec((th, width), lambda i, c: (c[0] * (h // th) + i, 0)),
                pl.BlockSpec((th, width), lambda i, c: (i, 0)),
            ],
            out_specs=[pl.BlockSpec((th, width), lambda i, c: (i, 0))] * 2,
        ),
        out_shape=[jax.ShapeDtypeStruct((h, width), F32), jax.ShapeDtypeStruct((h, width), BF16)],
        compiler_params=_params(("parallel",)),
    )(core, g, recv)


def _add_core_blocks(g, recv, core, name):
    n, hb, C = recv.shape

    def body(c_ref, g_ref, r_ref, o_ref, ob_ref):
        o_ref[...] = g_ref[...] + r_ref[...]
        ob_ref[...] = o_ref[...].astype(BF16)

    return pl.pallas_call(
        body,
        name=name,
        grid_spec=pltpu.PrefetchScalarGridSpec(
            num_scalar_prefetch=1,
            grid=(n,),
            in_specs=[
                pl.BlockSpec((hb, C), lambda p, c: (2 * p + c[0], 0)),
                pl.BlockSpec((None, hb, C), lambda p, c: (p, 0, 0)),
            ],
            out_specs=[pl.BlockSpec((None, hb, C), lambda p, c: (p, 0, 0))] * 2,
        ),
        out_shape=[jax.ShapeDtypeStruct((n, hb, C), F32), jax.ShapeDtypeStruct((n, hb, C), BF16)],
        compiler_params=_params(("parallel",)),
    )(core, g, recv)


def _add_chips(own, recv, name):
    h, W = own.shape
    th = 256 if h % 256 == 0 else h

    def body(a_ref, r_ref, o_ref):
        o_ref[...] = ((a_ref[...] + r_ref[0].astype(F32)) + r_ref[1].astype(F32)) + r_ref[2].astype(F32)

    return pl.pallas_call(
        body,
        name=name,
        grid=(h // th,),
        in_specs=[pl.BlockSpec((th, W), lambda i: (i, 0)), pl.BlockSpec((N_CHIPS - 1, th, W), lambda i: (0, i, 0))],
        out_specs=pl.BlockSpec((th, W), lambda i: (i, 0)),
        out_shape=jax.ShapeDtypeStruct((h, W), F32),
        compiler_params=_params(("parallel",)),
    )(own, recv)


def _place():
    x, y, c = lax.axis_index("x"), lax.axis_index("y"), lax.axis_index("c")
    other_chips = [(1 - x, y), (x, 1 - y), (1 - x, 1 - y)]
    return x, y, c, other_chips


def _allgather_w_in(w_in_bf, x2, norm_w):
    S, D = w_in_bf.shape
    T = x2.shape[0]
    tm = min(1024, T)
    ni = T // tm
    n_ici = n_fwd = N_CHIPS - 1

    def body(win, x_ref, nw_ref, gin, hn_ref, hnt_ref, send_sems, recv_sems):
        step = pl.program_id(0)
        xv = x_ref[...]
        hn = xv * lax.rsqrt(jnp.mean(xv * xv, axis=-1, keepdims=True) + EPS) * nw_ref[...]
        hn_ref[...] = hn.astype(BF16)
        hnt_ref[...] = hn.T.astype(BF16)
        x, y, c, chips = _place()
        me = 2 * x + y
        sibling = (x, y, 1 - c)
        hin = D // 2

        def half(chip_idx, core):
            return gin.at[chip_idx, :, pl.ds(core * hin, hin)]

        def rcopy(k, src, dst, to):
            return pltpu.make_async_remote_copy(src_ref=src, dst_ref=dst, send_sem=send_sems.at[k],
                                                recv_sem=recv_sems.at[k], device_id=to, device_id_type=MESH)

        def sends():
            return [rcopy(j, win.at[:, pl.ds(c * hin, hin)], half(me, c), (*chip, c)) for j, chip in enumerate(chips)]

        @pl.when(step == 0)
        def _():
            for cp in sends():
                cp.start()

        @pl.when(step == ni - 1)
        def _():
            passed = []
            for j, (px, py) in enumerate(chips):
                theirs = half(2 * px + py, c)
                rcopy(j, theirs, theirs, sibling).wait_recv()
                passed.append(rcopy(n_ici + j, theirs, theirs, sibling))
                passed[-1].start()
            for j, (px, py) in enumerate(chips):
                other = half(2 * px + py, 1 - c)
                rcopy(n_ici + j, other, other, sibling).wait_recv()
            for cp in sends() + passed:
                cp.wait_send()

    return pl.pallas_call(
        body,
        name="allgather_w_in",
        grid=(ni,),
        in_specs=[ANY, pl.BlockSpec((tm, D), lambda i: (i, 0)), pl.BlockSpec((1, D), lambda i: (0, 0))],
        out_specs=[ANY, pl.BlockSpec((tm, D), lambda i: (i, 0)), pl.BlockSpec((D, tm), lambda i: (0, i))],
        out_shape=[jax.ShapeDtypeStruct((N_CHIPS, S, D), BF16),
                   jax.ShapeDtypeStruct((T, D), BF16), jax.ShapeDtypeStruct((D, T), BF16)],
        scratch_shapes=[pltpu.SemaphoreType.DMA((n_ici + n_fwd,)), pltpu.SemaphoreType.DMA((n_ici + n_fwd,))],
        compiler_params=_params(("arbitrary",)),
    )(w_in_bf, x2, norm_w)


def _allreduce_small(packed):
    R = packed.shape[0]
    n_dev = 2 * N_CHIPS

    def body(p_ref, o_ref, buf, send_sems, recv_sems):
        x, y, c, _ = _place()
        me = 4 * x + 2 * y + c
        buf[me] = p_ref[...]
        copies = []
        for k in range(1, n_dev):
            px = 1 - x if k & 4 else x
            py = 1 - y if k & 2 else y
            pc = 1 - c if k & 1 else c
            copies.append((pltpu.make_async_remote_copy(
                src_ref=buf.at[me], dst_ref=buf.at[me], send_sem=send_sems.at[k - 1], recv_sem=recv_sems.at[k - 1],
                device_id=(px, py, pc), device_id_type=MESH), 4 * px + 2 * py + pc, (px, py, pc)))
        for cp, _, _ in copies:
            cp.start()
        for k, (_, peer, to) in enumerate(copies):
            pltpu.make_async_remote_copy(
                src_ref=buf.at[peer], dst_ref=buf.at[peer], send_sem=send_sems.at[k], recv_sem=recv_sems.at[k],
                device_id=to, device_id_type=MESH).wait_recv()
        for cp, _, _ in copies:
            cp.wait_send()
        acc = buf[0]
        for d in range(1, n_dev):
            acc = acc + buf[d]
        o_ref[...] = acc

    vm = pl.BlockSpec(memory_space=pltpu.VMEM)
    return pl.pallas_call(
        body,
        name="allreduce_small",
        in_specs=[vm],
        out_specs=vm,
        out_shape=jax.ShapeDtypeStruct((R, LANES), F32),
        scratch_shapes=[pltpu.VMEM((n_dev, R, LANES), F32), pltpu.SemaphoreType.DMA((n_dev - 1,)),
                        pltpu.SemaphoreType.DMA((n_dev - 1,))],
    )(packed)


def _swap_core_halves(g_out):
    hb = g_out.shape[0] // (2 * N_CHIPS)

    def body(gout, rout, send_sems, recv_sems):
        x, y, c, _ = _place()
        cps = [pltpu.make_async_remote_copy(
            src_ref=gout.at[pl.ds((2 * p + 1 - c) * hb, hb)], dst_ref=rout.at[p], send_sem=send_sems.at[p],
            recv_sem=recv_sems.at[p], device_id=(x, y, 1 - c), device_id_type=MESH) for p in range(N_CHIPS)]
        for cp in cps:
            cp.start()
        for cp in cps:
            cp.wait()

    return pl.pallas_call(
        body,
        name="reduce_core_swap",
        in_specs=[ANY],
        out_specs=ANY,
        out_shape=jax.ShapeDtypeStruct((N_CHIPS, hb, g_out.shape[1]), F32),
        scratch_shapes=[pltpu.SemaphoreType.DMA((N_CHIPS,)), pltpu.SemaphoreType.DMA((N_CHIPS,))],
    )(g_out)


def _join_core_halves(g_in, g_out):
    def body(gin, gout, fin, fout, send_sems, recv_sems):
        x, y, c, _ = _place()
        cps = [pltpu.make_async_remote_copy(src_ref=s, dst_ref=d.at[c], send_sem=send_sems.at[k],
                                            recv_sem=recv_sems.at[k], device_id=(x, y, 1 - c), device_id_type=MESH)
               for k, (s, d) in enumerate(((gin, fin), (gout, fout)))]
        for cp in cps:
            cp.start()
        for k, (s, d) in enumerate(((gin, fin), (gout, fout))):
            pltpu.make_async_remote_copy(src_ref=s, dst_ref=d.at[1 - c], send_sem=send_sems.at[k],
                                         recv_sem=recv_sems.at[k], device_id=(x, y, 1 - c),
                                         device_id_type=MESH).wait_recv()
        for cp in cps:
            cp.wait_send()

    return pl.pallas_call(
        body,
        name="reduce_core_join",
        in_specs=[ANY, ANY],
        out_specs=[ANY, ANY],
        out_shape=[jax.ShapeDtypeStruct((2,) + g_in.shape, F32), jax.ShapeDtypeStruct((2,) + g_out.shape, F32)],
        scratch_shapes=[pltpu.SemaphoreType.DMA((2,)), pltpu.SemaphoreType.DMA((2,))],
    )(g_in, g_out)


def _pack(arrays):
    rows = []
    for a in arrays:
        flat = a.reshape(-1).astype(F32)
        n = -(-flat.shape[0] // LANES) * LANES
        rows.append(jnp.pad(flat, (0, n - flat.shape[0])).reshape(-1, LANES))
    out = jnp.concatenate(rows, axis=0)
    return jnp.pad(out, ((0, -out.shape[0] % 8), (0, 0)))


def _unpack(packed, shapes):
    out, r = [], 0
    for shp in shapes:
        n = math.prod(shp)
        nr = -(-n // LANES)
        out.append(packed[r:r + nr].reshape(-1)[:n].reshape(shp))
        r += nr
    return out


def _pad_lanes(a):
    return jnp.pad(a, ((0, 0), (0, LANES - a.shape[1])))


def kernel(x, norm_w, w_in, q_norm_w, k_norm_w, conv_w, conv_b, dt_bias, A_log, D_skip, sb_norm_w, ssd_norm_w, w_out, loss_target, m_norm_w, m_w_in, m_q_norm_w, m_k_norm_w, m_conv_w, m_conv_b, m_dt_bias, m_A_log, m_D_skip, m_sb_norm_w, m_ssd_norm_w, m_w_out, v_norm_w, v_w_in, v_q_norm_w, v_k_norm_w, v_conv_w, v_conv_b, v_dt_bias, v_A_log, v_D_skip, v_sb_norm_w, v_ssd_norm_w, v_w_out):
    Bl, L, D = x.shape
    T = Bl * L
    S = w_in.shape[2]
    R = w_out.shape[1]
    CW = conv_w.shape[2]
    n_in = N_CHIPS * S
    CD = D + 2 * SSD_GROUPS * SSD_STATE
    H = D // HEAD_DIM
    n_main = 6 * D + 512
    P = -(-(n_main + LANES) // 1024) * 1024
    assert n_in == n_main + H and CD == N_CHIPS * CW and 2 * D == N_CHIPS * R and CD == D + 512
    chip = (2 * lax.axis_index("x") + lax.axis_index("y")).astype(jnp.int32)
    core = lax.axis_index("c").astype(jnp.int32)

    w_in_t, m_in_t, v_in_t = w_in[0].T, m_w_in[0].T, v_w_in[0].T
    w_in_bf, w_out_shard_bf = w_in_t.astype(BF16), w_out[0].astype(BF16)
    x2 = x.reshape(T, D)
    g_in, hn, hn_t = _allgather_w_in(w_in_bf, x2, norm_w)
    g_in = lax.dynamic_update_slice(g_in, w_in_bf[None], (chip, 0, 0))
    w_pad_t = _stack_shards(g_in, P)
    proj, g_out, g_cw = _inproj(hn, w_pad_t, w_out_shard_bf, conv_w[0])
    g_out = lax.dynamic_update_slice(g_out, w_out_shard_bf[None], (chip, 0, 0))
    g_cw = lax.dynamic_update_slice(g_cw, conv_w, (chip, 0, 0))
    w_out_bf = g_out.reshape(2 * D, D)
    conv_full = g_cw.transpose(1, 0, 2).reshape(CONV_K, CD)
    cwx, cwb = conv_full[:, :D], conv_full[:, D:]
    cbx, cbb = conv_b[:, :D], conv_b[:, D:]
    dtb, alog, dsk = _pad_lanes(dt_bias), _pad_lanes(A_log), _pad_lanes(D_skip)
    qw2, kw2 = jnp.tile(q_norm_w, (1, 2)), jnp.tile(k_norm_w, (1, 2))

    proj3 = proj.reshape(Bl, L, P)
    o_sb = _attn_fwd(proj3, qw2, kw2, D)
    y_ssd, s_in = _ssd_fwd(proj3, cwx, cwb, cbx, cbb, dtb, alog, dsk, D)
    dout, dout_bf, mixed_t, do_sb, dy_ssd, dz_bf, dnw_out, loss_blk = _gate_out(
        x2, loss_target.reshape(T, D), o_sb.reshape(T, D), proj, y_ssd.reshape(T, D), sb_norm_w, ssd_norm_w, w_out_bf,
        w_out_bf.T)

    dq, dk, dv, dqkw = _attn_bwd(proj3, o_sb, do_sb.reshape(Bl, L, D), qw2, kw2, D)
    dtail, dcwx, dcwb, dcbx, dcbb, misc = _ssd_bwd(
        proj3, s_in, dy_ssd.reshape(Bl, L, D), cwx, cwb, cbx, cbb, dtb, alog, dsk, D, P - 5 * D)
    dproj = [dq.reshape(T, D), dk.reshape(T, D), dv.reshape(T, D), dz_bf, dtail.reshape(T, P - 5 * D)]
    gw_in = _grad_w_in(hn_t, dproj)

    slab_off = S // LANES * LANES
    slab_w = -(-(S + (N_CHIPS - 1) * (S - slab_off)) // LANES) * LANES
    width = (N_CHIPS - 1) * slab_off + slab_w
    assert n_in <= width <= P
    core1 = core.reshape(1)
    gw_out, r_in = _grad_w_out(mixed_t, dout_bf, gw_in, width)
    r_out = _swap_core_halves(gw_out)
    h_in, hb_in = _add_core_rows(gw_in, r_in, core1, "sum_cores_w_in")
    h_out, hb_out = _add_core_blocks(gw_out, r_out, core1, "sum_cores_w_out")
    grad_x2, dnw_in, s_in_, s_out_, o_in_, o_out_ = _dhn(dproj, w_pad_t, x2, dout, norm_w, h_in, h_out, hb_in, hb_out,
                                                         slab_off, slab_w)
    gh_in = _add_chips(o_in_, s_in_, "sum_chips_w_in")
    gh_out = _add_chips(o_out_, s_out_, "sum_chips_w_out")
    f_in, f_out = _join_core_halves(gh_in, gh_out)
    g_slab = lax.dynamic_update_slice(f_in, gh_in[None], (core, 0, 0)).reshape(D, slab_w)
    g_w_in_t = lax.dynamic_slice(g_slab.T, (chip * (S - slab_off), 0), (S, D))
    g_w_in = g_w_in_t.T
    g_w_out = lax.dynamic_update_slice(f_out, gh_out[None], (core, 0, 0)).reshape(R, D)

    small_shapes = [(1, D), (1, D), (1, D), (1, CD), (1, HEAD_DIM), (1, HEAD_DIM), (1, H), (1, H), (1, H)]
    g_small_local = [dnw_in[0:1], dnw_out[0:1], dnw_out[1:2], jnp.concatenate([dcbx, dcbb], axis=1),
                     dqkw[0:1, :HEAD_DIM] + dqkw[0:1, HEAD_DIM:], dqkw[1:2, :HEAD_DIM] + dqkw[1:2, HEAD_DIM:],
                     misc[0:1, :H], misc[1:2, :H], misc[2:3, :H]]
    packed = _pack(g_small_local + [jnp.concatenate([dcwx, dcwb], axis=1), loss_blk[0:1, 0:1]])
    red = _allreduce_small(packed)
    g_small = _unpack(red, small_shapes + [(CONV_K, CD), (1, 1)])
    g_conv_w = lax.dynamic_slice_in_dim(g_small[9], chip * CW, CW, axis=1)
    loss = g_small[10][0, 0]

    d_in, nm_in, nv_in = (t.T for t in _adamw(w_in_t, g_w_in_t, m_in_t, v_in_t, "adamw_w_in"))
    d_out, nm_out, nv_out = _adamw(w_out[0], g_w_out, m_w_out[0], v_w_out[0], "adamw_w_out")
    d_cw, nm_cw, nv_cw = _adamw(conv_w[0], g_conv_w, m_conv_w[0], v_conv_w[0], "adamw_conv_w")
    small_w = [norm_w, sb_norm_w, ssd_norm_w, conv_b, q_norm_w, k_norm_w, dt_bias, A_log, D_skip]
    small_m = [m_norm_w, m_sb_norm_w, m_ssd_norm_w, m_conv_b, m_q_norm_w, m_k_norm_w, m_dt_bias, m_A_log, m_D_skip]
    small_v = [v_norm_w, v_sb_norm_w, v_ssd_norm_w, v_conv_b, v_q_norm_w, v_k_norm_w, v_dt_bias, v_A_log, v_D_skip]
    d_s, nm_s, nv_s = _adamw(_pack(small_w), _pack(g_small[:9]), _pack(small_m), _pack(small_v), "adamw_small")
    d_s, nm_s, nv_s = (_unpack(t, small_shapes) for t in (d_s, nm_s, nv_s))

    def ordered(s, w_in_, conv_w_, w_out_):
        return [s[0], w_in_[None], s[4], s[5], conv_w_[None], s[3], s[6], s[7], s[8], s[1], s[2], w_out_[None]]

    return (loss, grad_x2.reshape(Bl, L, D),
            *ordered(g_small[:9], g_w_in, g_conv_w, g_w_out),
            *ordered(d_s, d_in, d_cw, d_out),
            *ordered(nm_s, nm_in, nm_cw, nm_out),
            *ordered(nv_s, nv_in, nv_cw, nv_out))
```

```python
import functools
import math

import jax
import jax.numpy as jnp
from jax import lax
from jax.experimental import pallas as pl
from jax.experimental.pallas import tpu as pltpu

F32 = jnp.float32
BF16 = jnp.bfloat16
EPS = 1e-6
HEAD_DIM = 64
PAIR = 2 * HEAD_DIM
LANES = 128
SSD_STATE = 128
SSD_GROUPS = 2
BLK = 128
PREP_BLOCKS = 16
Q_TOGETHER_FWD = 2
Q_TOGETHER_BWD = 2
FIRST_LEFT = 2
UNDERFLOW = -105.0
CONV_K = 4
HALO = 8
N_CHIPS = 4
ADAM_LR, ADAM_B1, ADAM_B2, ADAM_EPS, ADAM_WD, ADAM_STEP = 0.001, 0.9, 0.999, 1e-08, 0.01, 10
VMEM_LIMIT_V7X = 56 * 1024 * 1024
MESH = pl.DeviceIdType.MESH
ANY = pl.BlockSpec(memory_space=pl.ANY)
NT = (((1,), (1,)), ((), ()))


def _params(sem=None):
    kw = dict(vmem_limit_bytes=VMEM_LIMIT_V7X)
    if sem is not None:
        kw["dimension_semantics"] = sem
    return pltpu.CompilerParams(**kw)


def _dot(a, b):
    return jnp.dot(a, b, preferred_element_type=F32)


def _dot_nt(a, b):
    return lax.dot_general(a, b, NT, preferred_element_type=F32)


def _dot_split(m, x):
    hi = x.astype(BF16)
    lo = (x - hi.astype(F32)).astype(BF16)
    return _dot(m, hi) + _dot(m, lo)


def _iota(shape, dim):
    return lax.broadcasted_iota(jnp.int32, shape, dim)


def _rowsum(x):
    return jnp.sum(x, axis=1, keepdims=True)


def _colsum(x):
    return jnp.sum(x, axis=0, keepdims=True)


def _sigmoid(x):
    return 0.5 * jnp.tanh(0.5 * x) + 0.5


def _softplus(x):
    return jnp.maximum(x, 0.0) + jnp.log(1.0 + jnp.exp(-jnp.abs(x)))


def _stack_shards(g_in, P):
    n, S, D = g_in.shape

    def body(g_ref, o_ref):
        for p in range(n):
            o_ref[p * S:(p + 1) * S, :] = g_ref[p]
        o_ref[n * S:P, :] = jnp.zeros((P - n * S, LANES), BF16)

    return pl.pallas_call(
        body,
        name="stack_w_in",
        grid=(D // LANES,),
        in_specs=[pl.BlockSpec((n, S, LANES), lambda j: (0, 0, j))],
        out_specs=pl.BlockSpec((P, LANES), lambda j: (0, j)),
        out_shape=jax.ShapeDtypeStruct((P, D), BF16),
        compiler_params=_params(("parallel",)),
    )(g_in)


def _inproj(hn, w_pad_t, w_out_bf, conv_w):
    T, D = hn.shape
    P = w_pad_t.shape[0]
    tm = min(1024, T)
    tn = 1024 if P % 1024 == 0 else 512
    ni, nj = T // tm, P // tn
    n_sem = 2 * (N_CHIPS - 1)

    def body(hn_ref, w_ref, wout, cw, proj_ref, gout, gcw, send_sems, recv_sems):
        def gather():
            x, y, c, chips = _place()
            me = 2 * x + y
            return [pltpu.make_async_remote_copy(
                src_ref=src, dst_ref=dst.at[me], send_sem=send_sems.at[2 * j + m], recv_sem=recv_sems.at[2 * j + m],
                device_id=(px, py, c), device_id_type=MESH)
                for j, (px, py) in enumerate(chips) for m, (src, dst) in enumerate(((wout, gout), (cw, gcw)))]

        @pl.when((pl.program_id(0) == 0) & (pl.program_id(1) == 0))
        def _():
            for cp in gather():
                cp.start()

        @pl.when((pl.program_id(0) == ni - 1) & (pl.program_id(1) == nj - 1))
        def _():
            for cp in gather():
                cp.wait()

        proj_ref[...] = _dot_nt(hn_ref[...], w_ref[...])

    return pl.pallas_call(
        body,
        name="inproj",
        grid=(T // tm, P // tn),
        in_specs=[
            pl.BlockSpec((tm, D), lambda i, j: (i, 0)),
            pl.BlockSpec((tn, D), lambda i, j: (j, 0)),
            ANY, ANY,
        ],
        out_specs=[
            pl.BlockSpec((tm, tn), lambda i, j: (i, j)),
            ANY, ANY,
        ],
        out_shape=[jax.ShapeDtypeStruct((T, P), F32),
                   jax.ShapeDtypeStruct((N_CHIPS,) + w_out_bf.shape, BF16),
                   jax.ShapeDtypeStruct((N_CHIPS,) + conv_w.shape, F32)],
        scratch_shapes=[pltpu.SemaphoreType.DMA((n_sem,)), pltpu.SemaphoreType.DMA((n_sem,))],
        compiler_params=_params(("arbitrary", "arbitrary")),
    )(hn, w_pad_t, w_out_bf, conv_w)


def _pair_ones():
    ri = ((_iota((2 * PAIR, PAIR), 0) % PAIR) >= HEAD_DIM).astype(jnp.int32)
    ci = (_iota((2 * PAIR, PAIR), 1) >= HEAD_DIM).astype(jnp.int32)
    return jnp.where(ri == ci, 1.0, 0.0).astype(BF16)


def _pair_rms(v, ones2):
    return lax.rsqrt(_split_dots([v * v], ones2)[0] * (1.0 / HEAD_DIM) + EPS)


def _pair_mean(v, ones2):
    return _split_dots([v], ones2)[0] * (1.0 / HEAD_DIM)


def _suffix_ones():
    ri = _iota((2 * BLK, 2 * BLK), 0) % BLK
    ci = _iota((2 * BLK, 2 * BLK), 1)
    return jnp.where((ci >= BLK) | (ri > ci), 1.0, 0.0).astype(BF16)


def _split_dots(xs, m2):
    his = [x.astype(BF16) for x in xs]
    los = [(x - hi.astype(F32)).astype(BF16) for x, hi in zip(xs, his)]
    return [_dot(jnp.concatenate([hi, lo], axis=1), m2) for hi, lo in zip(his, los)]


def _sb_tiles(streams, km_s, uo):
    tiles = [(s, u, h) for s, st in enumerate(streams) for u in range(len(st["kbs"])) for h in range(2)]
    z2s = {(s, u): _dot_nt(st["q"], km_s[kb]) for s, st in enumerate(streams) for u, kb in enumerate(st["kbs"])}
    zs = [z2s[s, u][:, h * BLK:(h + 1) * BLK] for s, u, h in tiles]
    es = [jnp.exp(-jnp.abs(z)) for z in zs]
    las = [jnp.minimum(z, 0.0) - jnp.log(1.0 + e) for z, e in zip(zs, es)]
    lns = [a - z for a, z in zip(las, zs)]
    masks = [streams[s]["masks"][u] for s, u, h in tiles]
    lks = [lk if m is None else jnp.where(m, lk, 0.0) for m, lk in zip(masks, lns)]
    css = _split_dots(lks, uo)
    rests = [list(st["rest"]) for st in streams]
    ws = []
    for (s, u, h), m, a, cs in zip(tiles, masks, las, css):
        w = jnp.exp(a + rests[s][h] + cs[:, :BLK])
        ws.append(w if m is None else jnp.where(m, w, 0.0))
        rests[s][h] = rests[s][h] + cs[:, BLK:]
    return tiles, las, lns, ws, rests


def _stream(q_pair, qi, n_left, diag, zero):
    return dict(q=q_pair, kbs=[qi - u for u in range(n_left + 1)], masks=[diag] + [None] * n_left, rest=[zero, zero])


def _row0(block):
    return block * BLK if isinstance(block, int) else pl.multiple_of(block * BLK, BLK)


def _pair_of(vals, tiles, s, u):
    return [v for v, t in zip(vals, tiles) if t[0] == s and t[1] == u]


def _block_groups(nq, together):
    n_tog = math.gcd(together, nq)
    assert n_tog >= FIRST_LEFT
    return n_tog, list(range(n_tog)), nq // n_tog


def _attn_prep(src_ref, w_ref, dst_s, n_blocks, scale):
    per = math.gcd(PREP_BLOCKS, n_blocks)
    rows = per * BLK
    lo = _iota((rows, PAIR), 1) < HEAD_DIM
    ones2 = _pair_ones()

    def step(i, carry):
        r0 = pl.multiple_of(i * rows, rows)
        v = src_ref[0, pl.ds(r0, rows), :]
        if w_ref is not None:
            v = v * _pair_rms(v, ones2) * w_ref[...]
        if scale != 1.0:
            v = v * scale
        v0, v1 = jnp.where(lo, v, 0.0).astype(BF16), jnp.where(lo, 0.0, v).astype(BF16)
        for b in range(per):
            dst_s[i * per + b, 0:BLK, :] = v0[b * BLK:(b + 1) * BLK]
            dst_s[i * per + b, BLK:2 * BLK, :] = v1[b * BLK:(b + 1) * BLK]
        return carry

    lax.fori_loop(0, n_blocks // per, step, 0)


def _attn_fwd(proj3, qw2, kw2, D):
    Bl, L, _ = proj3.shape
    n_pair = D // PAIR
    nq = L // BLK
    scale = 1.0 / math.sqrt(HEAD_DIM)

    def body(q_ref, k_ref, v_ref, qw_ref, kw_ref, o_ref, qm_s, km_s, vm_s):
        uo = _suffix_ones()
        diag = _iota((BLK, BLK), 1) < _iota((BLK, BLK), 0)
        _attn_prep(q_ref, qw_ref, qm_s, nq, scale)
        _attn_prep(k_ref, kw_ref, km_s, nq, 1.0)
        _attn_prep(v_ref, None, vm_s, nq, 1.0)

        zero_c = jnp.zeros((BLK, BLK), F32)

        def q_of(qi):
            return qm_s[qi, 0:BLK, :] + qm_s[qi, BLK:2 * BLK, :]

        def values(streams, accs):
            tiles, _, _, ws, rests = _sb_tiles(streams, km_s, uo)
            wbs = [w.astype(BF16) for w in ws]
            accs = list(accs)
            for s, st in enumerate(streams):
                for u, kb in enumerate(st["kbs"]):
                    accs[s] = accs[s] + _dot(jnp.concatenate(_pair_of(wbs, tiles, s, u), axis=1), vm_s[kb])
            return accs, rests

        def group(qis, n_lefts):
            streams = [_stream(q_of(qi), qi, n, diag, zero_c) for qi, n in zip(qis, n_lefts)]
            accs, rests = values(streams, [jnp.zeros((BLK, PAIR), F32)] * len(qis))
            for qi, n, q, acc, rc in zip(qis, n_lefts, [st["q"] for st in streams], accs, rests):

                def sweep(state, n_blocks, q=q):
                    kb, rc0, rc1, acc1, _ = state
                    st = dict(q=q, kbs=[kb - u for u in range(n_blocks)], masks=[None] * n_blocks, rest=[rc0, rc1])
                    (acc1,), (r,) = values([st], [acc1])
                    return kb - n_blocks, r[0], r[1], acc1, jnp.maximum(jnp.max(r[0]), jnp.max(r[1]))

                state = (jnp.asarray(qi - n - 1, jnp.int32), rc[0], rc[1], acc, jnp.maximum(jnp.max(rc[0]), jnp.max(rc[1])))
                state = lax.while_loop(lambda t: (t[0] >= 1) & (t[4] >= UNDERFLOW), lambda t: sweep(t, 2), state)
                state = lax.while_loop(lambda t: (t[0] >= 0) & (t[4] >= UNDERFLOW), lambda t: sweep(t, 1), state)
                o_ref[0, pl.ds(_row0(qi), BLK), :] = state[3]

        n_tog, head, n_groups = _block_groups(nq, Q_TOGETHER_FWD)
        group(head, [min(qi, FIRST_LEFT) for qi in head])

        def groups(g, carry):
            group([g * n_tog + j for j in range(n_tog)], [FIRST_LEFT] * n_tog)
            return carry

        lax.fori_loop(1, n_groups, groups, 0)

    blk = lambda off: pl.BlockSpec((1, L, PAIR), lambda b, p: (b, 0, off + p))
    wspec = pl.BlockSpec((1, PAIR), lambda b, p: (0, 0))
    return pl.pallas_call(
        body,
        name="sb_attn_fwd",
        grid=(Bl, n_pair),
        in_specs=[blk(0), blk(n_pair), blk(2 * n_pair), wspec, wspec],
        out_specs=pl.BlockSpec((1, L, PAIR), lambda b, p: (b, 0, p)),
        out_shape=jax.ShapeDtypeStruct((Bl, L, D), F32),
        scratch_shapes=[pltpu.VMEM((nq, 2 * BLK, PAIR), BF16)] * 3,
        compiler_params=_params(("parallel", "parallel")),
    )(proj3, proj3, proj3, qw2, kw2)


def _attn_bwd(proj3, o3, do3, qw2, kw2, D):
    Bl, L, _ = proj3.shape
    n_pair = D // PAIR
    nq = L // BLK
    scale = 1.0 / math.sqrt(HEAD_DIM)

    def body(q_ref, k_ref, v_ref, o_ref, do_ref, qw_ref, kw_ref, dq_ref, dk_ref, dv_ref, dw_ref,
             qm_s, km_s, vm_s, dom_s, dq_s, dk_s, dv_s):
        uo = _suffix_ones()
        diag = _iota((BLK, BLK), 1) < _iota((BLK, BLK), 0)
        ones2 = _pair_ones()
        _attn_prep(q_ref, qw_ref, qm_s, nq, scale)
        _attn_prep(k_ref, kw_ref, km_s, nq, 1.0)
        _attn_prep(v_ref, None, vm_s, nq, 1.0)
        _attn_prep(do_ref, None, dom_s, nq, 1.0)

        @pl.when((pl.program_id(0) == 0) & (pl.program_id(1) == 0))
        def _():
            dw_ref[...] = jnp.zeros_like(dw_ref)

        def zero(i, carry):
            r0 = pl.multiple_of(i * BLK, BLK)
            dk_s[pl.ds(r0, BLK), :] = jnp.zeros((BLK, PAIR), F32)
            dv_s[pl.ds(r0, BLK), :] = jnp.zeros((BLK, PAIR), F32)
            return carry

        lax.fori_loop(0, nq, zero, 0)

        zero_c = jnp.zeros((BLK, BLK), F32)

        def tiles_bwd(streams, dqas):
            tiles, las, lns, ws, rests = _sb_tiles(streams, km_s, uo)
            dw2s = {(s, u): _dot_nt(st["do"], vm_s[kb]) for s, st in enumerate(streams) for u, kb in enumerate(st["kbs"])}
            dws = [dw2s[s, u][:, h * BLK:(h + 1) * BLK] for s, u, h in tiles]
            wfs = [w.astype(BF16).astype(F32) for w in ws]
            gs = [wf * dw for wf, dw in zip(wfs, dws)]
            gss = _split_dots(gs, uo)
            gcs = [list(st["g_rest"]) for st in streams]
            dzs = []
            for (s, u, h), a, ln, g, gsum in zip(tiles, las, lns, gs, gss):
                g_before = streams[s]["delta"][h] - (gcs[s][h] + gsum[:, :BLK] + g)
                gcs[s][h] = gcs[s][h] + gsum[:, BLK:]
                dz = g * jnp.exp(ln) - g_before * jnp.exp(a)
                m = streams[s]["masks"][u]
                dzs.append(dz if m is None else jnp.where(m, dz, 0.0))
            wts = [wf.T.astype(BF16) for wf in wfs]
            dzts = [dz.T.astype(BF16) for dz in dzs]
            dzbs = [dz.astype(BF16) for dz in dzs]
            dqas = list(dqas)
            for s, st in enumerate(streams):
                for u, kb in enumerate(st["kbs"]):
                    c0 = _row0(kb)
                    dv_s[pl.ds(c0, BLK), :] += _dot(jnp.concatenate(_pair_of(wts, tiles, s, u), axis=1), dom_s[st["qi"]])
                    dk_s[pl.ds(c0, BLK), :] += _dot(jnp.concatenate(_pair_of(dzts, tiles, s, u), axis=1), qm_s[st["qi"]])
                    dqas[s] = dqas[s] + _dot(jnp.concatenate(_pair_of(dzbs, tiles, s, u), axis=1), km_s[kb])
            return dqas, rests, gcs

        def group(qis, n_lefts):
            streams = []
            for qi, n in zip(qis, n_lefts):
                o_blk = o_ref[0, pl.ds(_row0(qi), BLK), :]
                doms = [dom_s[qi, 0:BLK, :], dom_s[qi, BLK:2 * BLK, :]]
                st = _stream(qm_s[qi, 0:BLK, :] + qm_s[qi, BLK:2 * BLK, :], qi, n, diag, zero_c)
                st.update(qi=qi, do=doms[0] + doms[1], delta=[_rowsum(d.astype(F32) * o_blk) for d in doms],
                          g_rest=[zero_c, zero_c])
                streams.append(st)
            dqas, rests, gcs = tiles_bwd(streams, [jnp.zeros((BLK, PAIR), F32)] * len(qis))
            for qi, n, st0, dqa, rc, gc in zip(qis, n_lefts, streams, dqas, rests, gcs):

                def sweep(state, n_blocks, st0=st0):
                    kb, rc0, rc1, gc0, gc1, dqa1, _ = state
                    st = dict(st0, kbs=[kb - u for u in range(n_blocks)], masks=[None] * n_blocks, rest=[rc0, rc1],
                              g_rest=[gc0, gc1])
                    (dqa1,), (r,), (g,) = tiles_bwd([st], [dqa1])
                    return kb - n_blocks, r[0], r[1], g[0], g[1], dqa1, jnp.maximum(jnp.max(r[0]), jnp.max(r[1]))

                state = (jnp.asarray(qi - n - 1, jnp.int32), rc[0], rc[1], gc[0], gc[1], dqa,
                         jnp.maximum(jnp.max(rc[0]), jnp.max(rc[1])))
                state = lax.while_loop(lambda t: (t[0] >= 1) & (t[6] >= UNDERFLOW), lambda t: sweep(t, 2), state)
                state = lax.while_loop(lambda t: (t[0] >= 0) & (t[6] >= UNDERFLOW), lambda t: sweep(t, 1), state)
                dq_s[pl.ds(_row0(qi), BLK), :] = state[5] * scale

        n_tog, head, n_groups = _block_groups(nq, Q_TOGETHER_BWD)
        group(head, [min(qi, FIRST_LEFT) for qi in head])

        def groups(g, carry):
            group([g * n_tog + j for j in range(n_tog)], [FIRST_LEFT] * n_tog)
            return carry

        lax.fori_loop(1, n_groups, groups, 0)

        per = math.gcd(PREP_BLOCKS, nq)
        rows = per * BLK

        def finish(i, carry):
            r0 = pl.multiple_of(i * rows, rows)
            dwq, dwk = carry
            out = []
            for src_ref, w_ref, d_s in ((q_ref, qw_ref, dq_s), (k_ref, kw_ref, dk_s)):
                v = src_ref[0, pl.ds(r0, rows), :]
                r = _pair_rms(v, ones2)
                vh = v * r
                dy = d_s[pl.ds(r0, rows), :]
                dvh = dy * w_ref[...]
                out.append((r * (dvh - vh * _pair_mean(dvh * vh, ones2)), _colsum(dy * vh)))
            dq_ref[0, pl.ds(r0, rows), :] = out[0][0].astype(BF16)
            dk_ref[0, pl.ds(r0, rows), :] = out[1][0].astype(BF16)
            dv_ref[0, pl.ds(r0, rows), :] = dv_s[pl.ds(r0, rows), :].astype(BF16)
            return dwq + out[0][1], dwk + out[1][1]

        zrow = jnp.zeros((1, PAIR), F32)
        dwq, dwk = lax.fori_loop(0, nq // per, finish, (zrow, zrow))
        dw_ref[0:1, :] += dwq
        dw_ref[1:2, :] += dwk

    blk = lambda off: pl.BlockSpec((1, L, PAIR), lambda b, p: (b, 0, off + p))
    wspec = pl.BlockSpec((1, PAIR), lambda b, p: (0, 0))
    oblk = pl.BlockSpec((1, L, PAIR), lambda b, p: (b, 0, p))
    return pl.pallas_call(
        body,
        name="sb_attn_bwd",
        grid=(Bl, n_pair),
        in_specs=[blk(0), blk(n_pair), blk(2 * n_pair), oblk, oblk, wspec, wspec],
        out_specs=[oblk, oblk, oblk, pl.BlockSpec((8, PAIR), lambda b, p: (0, 0))],
        out_shape=[jax.ShapeDtypeStruct((Bl, L, D), BF16)] * 3 + [jax.ShapeDtypeStruct((8, PAIR), F32)],
        scratch_shapes=[pltpu.VMEM((nq, 2 * BLK, PAIR), BF16)] * 4 + [pltpu.VMEM((L, PAIR), F32)] * 3,
        compiler_params=_params(("arbitrary", "arbitrary")),
    )(proj3, proj3, proj3, o3, do3, qw2, kw2)


def _conv_pre(ext_s, halo_ref, raw_ref, w_ref, b_ref, first):
    ext_s[0:HALO, :] = jnp.where(first, 0.0, halo_ref[0])
    ext_s[HALO:HALO + BLK, :] = raw_ref[0]
    pre = b_ref[...]
    for i in range(CONV_K):
        pre = pre + ext_s[pl.ds(HALO - (CONV_K - 1 - i), BLK), :] * w_ref[i:i + 1, :]
    return pre


def _lane_col(m, lane, h):
    return _rowsum(jnp.where(lane == h, m, 0.0))


def _half_sums(row, lo1):
    return _rowsum(jnp.where(lo1, row, 0.0)), _rowsum(jnp.where(lo1, 0.0, row))


def _ssd_specs(Bl, L, D, rev):
    nc = L // BLK
    rows_per = BLK // HALO
    cidx = (lambda c: nc - 1 - c) if rev else (lambda c: c)
    xoff = 5
    boff = (6 * D) // 512
    doff = (6 * D + 512) // LANES
    prev = lambda c: jnp.maximum(cidx(c) * rows_per - 1, 0)
    specs = [
        pl.BlockSpec((1, BLK, D), lambda b, c: (b, cidx(c), xoff)),
        pl.BlockSpec((1, BLK, 512), lambda b, c: (b, cidx(c), boff)),
        pl.BlockSpec((1, HALO, D), lambda b, c: (b, prev(c), xoff)),
        pl.BlockSpec((1, HALO, 512), lambda b, c: (b, prev(c), boff)),
        pl.BlockSpec((1, BLK, LANES), lambda b, c: (b, cidx(c), doff)),
    ]
    full = lambda shape: pl.BlockSpec(shape, lambda b, c: (0,) * len(shape))
    specs += [full((CONV_K, D)), full((CONV_K, 512)), full((1, D)), full((1, 512)),
              full((1, LANES)), full((1, LANES)), full((1, LANES))]
    return specs, cidx


def _ssd_common(dtr_ref, dtb_ref, alog_ref, acs_s, acsT_s):
    ltri = jnp.where(_iota((BLK, BLK), 1) <= _iota((BLK, BLK), 0), 1.0, 0.0).astype(BF16)
    dtv = _softplus(dtr_ref[0] + dtb_ref[...])
    a = -jnp.exp(alog_ref[...])
    acs = _dot_split(ltri, dtv * a)
    acs_s[...] = acs
    acsT_s[...] = acs.T
    return dtv, a, acs


def _pair_terms(pr, acs, dtv, acs_s, lane, lo, lane1, lo1):
    h0, h1 = 2 * pr, 2 * pr + 1
    c0, c1 = _lane_col(acs, lane, h0), _lane_col(acs, lane, h1)
    d0, d1 = _lane_col(dtv, lane, h0), _lane_col(dtv, lane, h1)
    lastv = acs_s[BLK - 1:BLK, :]
    l0, l1 = _lane_col(lastv, lane1, h0), _lane_col(lastv, lane1, h1)
    return dict(h=(h0, h1), c=(c0, c1), last=(l0, l1), acs_p=jnp.where(lo, c0, c1), dt_p=jnp.where(lo, d0, d1),
                last_p=jnp.where(lo1, l0, l1))


def _decay_tiles(cc, row, tri, want_t):
    lm = jnp.where(tri, jnp.exp(cc - row), 0.0)
    return lm, (lm.T if want_t else None)


def _ssd_fwd(proj3, cwx, cwb, cbx, cbb, dtb, alog, dsk, D):
    Bl, L, _ = proj3.shape
    nc = L // BLK
    n_pair = D // PAIR
    pairs_per_group = n_pair // SSD_GROUPS
    specs, _ = _ssd_specs(Bl, L, D, False)

    def body(xr_ref, bcr_ref, xh_ref, bch_ref, dtr_ref, cwx_ref, cwb_ref, cbx_ref, cbb_ref, dtb_ref, alog_ref,
             dsk_ref, y_ref, sin_ref, st_s, extx_s, extb_s, acs_s, acsT_s):
        first = pl.program_id(1) == 0

        @pl.when(first)
        def _():
            st_s[...] = jnp.zeros_like(st_s)

        lane, lane1 = _iota((BLK, LANES), 1), _iota((1, LANES), 1)
        lo, lo1 = lane < HEAD_DIM, lane1 < HEAD_DIM
        tri = _iota((BLK, BLK), 1) <= _iota((BLK, BLK), 0)
        pre = _conv_pre(extx_s, xh_ref, xr_ref, cwx_ref, cbx_ref, first)
        ux = pre * _sigmoid(pre)
        pre = _conv_pre(extb_s, bch_ref, bcr_ref, cwb_ref, cbb_ref, first)
        ub = pre * _sigmoid(pre)
        dtv, a, acs = _ssd_common(dtr_ref, dtb_ref, alog_ref, acs_s, acsT_s)
        for g in range(SSD_GROUPS):
            bg = ub[:, g * SSD_STATE:(g + 1) * SSD_STATE]
            cb_ = ub[:, (SSD_GROUPS + g) * SSD_STATE:(SSD_GROUPS + g + 1) * SSD_STATE].astype(BF16)
            cbm = _dot_nt(cb_, bg.astype(BF16))
            btb = bg.T.astype(BF16)
            for pr in range(g * pairs_per_group, (g + 1) * pairs_per_group):
                t = _pair_terms(pr, acs, dtv, acs_s, lane, lo, lane1, lo1)
                xs_p = ux[:, pr * PAIR:(pr + 1) * PAIR]
                x_p = xs_p * t["dt_p"]
                st = st_s[pr]
                sin_ref[0, 0, pr] = st
                y = _dot(cb_, st.astype(BF16)) * jnp.exp(t["acs_p"])
                for k in range(2):
                    row = acsT_s[t["h"][k]:t["h"][k] + 1, :]
                    lm, _ = _decay_tiles(t["c"][k], row, tri, False)
                    xm = jnp.where(lo if k == 0 else ~lo, x_p, 0.0).astype(BF16)
                    y = y + _dot((cbm * lm).astype(BF16), xm)
                d_p = jnp.where(lo1, _lane_col(dsk_ref[...], lane1, t["h"][0]), _lane_col(dsk_ref[...], lane1, t["h"][1]))
                y_ref[0, :, pr * PAIR:(pr + 1) * PAIR] = y + d_p * xs_p
                xd = (x_p * jnp.exp(t["last_p"] - t["acs_p"])).astype(BF16)
                st_s[pr] = st * jnp.exp(t["last_p"]) + _dot(btb, xd)

    return pl.pallas_call(
        body,
        name="ssd_fwd",
        grid=(Bl, nc),
        in_specs=specs,
        out_specs=[
            pl.BlockSpec((1, BLK, D), lambda b, c: (b, c, 0)),
            pl.BlockSpec((1, 1, n_pair, SSD_STATE, PAIR), lambda b, c: (b, c, 0, 0, 0)),
        ],
        out_shape=[jax.ShapeDtypeStruct((Bl, L, D), F32),
                   jax.ShapeDtypeStruct((Bl, nc, n_pair, SSD_STATE, PAIR), F32)],
        scratch_shapes=[pltpu.VMEM((n_pair, SSD_STATE, PAIR), F32), pltpu.VMEM((HALO + BLK, D), F32),
                        pltpu.VMEM((HALO + BLK, 512), F32), pltpu.VMEM((BLK, LANES), F32),
                        pltpu.VMEM((LANES, BLK), F32)],
        compiler_params=_params(("arbitrary", "arbitrary")),
    )(proj3, proj3, proj3, proj3, proj3, cwx, cwb, cbx, cbb, dtb, alog, dsk)


def _ssd_bwd(proj3, s_in, dy3, cwx, cwb, cbx, cbb, dtb, alog, dsk, D, tail):
    Bl, L, _ = proj3.shape
    CD = D + 512
    nc = L // BLK
    n_pair = D // PAIR
    n_heads = 2 * n_pair
    pairs_per_group = n_pair // SSD_GROUPS
    specs, cidx = _ssd_specs(Bl, L, D, True)
    specs = specs + [
        pl.BlockSpec((1, 1, n_pair, SSD_STATE, PAIR), lambda b, c: (b, cidx(c), 0, 0, 0)),
        pl.BlockSpec((1, BLK, D), lambda b, c: (b, cidx(c), 0)),
    ]

    def body(xr_ref, bcr_ref, xh_ref, bch_ref, dtr_ref, cwx_ref, cwb_ref, cbx_ref, cbb_ref, dtb_ref, alog_ref,
             dsk_ref, sin_ref, dy_ref, dxbc_ref, dcwx_ref, dcwb_ref, dcbx_ref, dcbb_ref, misc_ref,
             dst_s, extx_s, extb_s, acs_s, acsT_s, dux_s, dub_s, e2x_s, e2b_s, nxx_s, nxb_s,
             prex_s, sgx_s, ux_s, preb_s, sgb_s, ub_s):
        step = pl.program_id(1)
        first = step == nc - 1
        last = step == 0

        @pl.when(last)
        def _():
            dst_s[...] = jnp.zeros_like(dst_s)
            nxx_s[...] = jnp.zeros_like(nxx_s)
            nxb_s[...] = jnp.zeros_like(nxb_s)

        @pl.when(last & (pl.program_id(0) == 0))
        def _():
            for r in (dcwx_ref, dcwb_ref, dcbx_ref, dcbb_ref, misc_ref):
                r[...] = jnp.zeros_like(r)

        lane, lane1 = _iota((BLK, LANES), 1), _iota((1, LANES), 1)
        lo, lo1 = lane < HEAD_DIM, lane1 < HEAD_DIM
        tri = _iota((BLK, BLK), 1) <= _iota((BLK, BLK), 0)
        for halo_ref, raw_ref, w_ref, b_ref, ext_s, pre_s, sg_s, u_s in (
                (xh_ref, xr_ref, cwx_ref, cbx_ref, extx_s, prex_s, sgx_s, ux_s),
                (bch_ref, bcr_ref, cwb_ref, cbb_ref, extb_s, preb_s, sgb_s, ub_s)):
            pre = _conv_pre(ext_s, halo_ref, raw_ref, w_ref, b_ref, first)
            sg = _sigmoid(pre)
            pre_s[...] = pre
            sg_s[...] = sg
            u_s[...] = pre * sg
        dtv, a, acs = _ssd_common(dtr_ref, dtb_ref, alog_ref, acs_s, acsT_s)
        dacs = jnp.zeros((BLK, LANES), F32)
        dlast = jnp.zeros((1, LANES), F32)
        ddt = jnp.zeros((BLK, LANES), F32)
        dd = jnp.zeros((1, LANES), F32)
        for g in range(SSD_GROUPS):
            bg = ub_s[:, g * SSD_STATE:(g + 1) * SSD_STATE]
            cg = ub_s[:, (SSD_GROUPS + g) * SSD_STATE:(SSD_GROUPS + g + 1) * SSD_STATE]
            bb, cb_ = bg.astype(BF16), cg.astype(BF16)
            cbm = _dot_nt(cb_, bb)
            cbt = _dot_nt(bb, cb_)
            ctb = cg.T.astype(BF16)
            dbg = jnp.zeros((BLK, SSD_STATE), F32)
            dcg = jnp.zeros((BLK, SSD_STATE), F32)
            for pr in range(g * pairs_per_group, (g + 1) * pairs_per_group):
                t = _pair_terms(pr, acs, dtv, acs_s, lane, lo, lane1, lo1)
                h0, h1 = t["h"]
                xs_p = ux_s[:, pr * PAIR:(pr + 1) * PAIR]
                dy_p = dy_ref[0, :, pr * PAIR:(pr + 1) * PAIR]
                x_p = xs_p * t["dt_p"]
                ea_p = jnp.exp(t["acs_p"])
                dte_p = jnp.exp(t["last_p"] - t["acs_p"])
                cd_p = jnp.exp(t["last_p"])
                st = sin_ref[0, 0, pr]
                dst = dst_s[pr]
                stb, dstb = st.astype(BF16), dst.astype(BF16)
                s0, s1 = _half_sums(_colsum(dy_p * xs_p), lo1)
                dd = dd + jnp.where(lane1 == h0, s0, 0.0) + jnp.where(lane1 == h1, s1, 0.0)
                d_p = jnp.where(lo1, _lane_col(dsk_ref[...], lane1, h0), _lane_col(dsk_ref[...], lane1, h1))
                dxs_p = d_p * dy_p
                dp = dy_p * ea_p
                dpb = dp.astype(BF16)
                yo = dp * _dot(cb_, stb)
                dcg = dcg + _dot_nt(dpb, stb)
                dst_off = _dot(ctb, dpb)
                dac = [_rowsum(jnp.where(lo, yo, 0.0)), _rowsum(jnp.where(lo, 0.0, yo))]
                s0, s1 = _half_sums(_colsum(dst * st), lo1)
                dl = [s0 * jnp.exp(t["last"][0]), s1 * jnp.exp(t["last"][1])]
                dxd = _dot(bb, dstb)
                dx_p = dxd * dte_p
                tt = dxd * x_p
                dbg = dbg + _dot_nt((x_p * dte_p).astype(BF16), dstb)
                for k, ddte in enumerate((_rowsum(jnp.where(lo, tt, 0.0)), _rowsum(jnp.where(lo, 0.0, tt)))):
                    ek = ddte * jnp.exp(t["last"][k] - t["c"][k])
                    dl[k] = dl[k] + _colsum(ek)
                    dac[k] = dac[k] - ek
                x_pb = x_p.astype(BF16)
                for k in range(2):
                    row = acsT_s[t["h"][k]:t["h"][k] + 1, :]
                    lm, lmt = _decay_tiles(t["c"][k], row, tri, True)
                    dym = jnp.where(lo if k == 0 else ~lo, dy_p, 0.0).astype(BF16)
                    dm = _dot_nt(dym, x_pb)
                    dmt = _dot_nt(x_pb, dym)
                    mt = cbt * lmt
                    dx_p = dx_p + _dot(mt.astype(BF16), dym)
                    dac[k] = dac[k] + _rowsum(dm * (cbm * lm)) - _rowsum(dmt * mt)
                    dcg = dcg + _dot((dm * lm).astype(BF16), bb)
                    dbg = dbg + _dot((dmt * lmt).astype(BF16), cb_)
                dacs = dacs + jnp.where(lane == h0, dac[0], 0.0) + jnp.where(lane == h1, dac[1], 0.0)
                dlast = dlast + jnp.where(lane1 == h0, dl[0], 0.0) + jnp.where(lane1 == h1, dl[1], 0.0)
                dxs_p = dxs_p + dx_p * t["dt_p"]
                t3 = dx_p * xs_p
                ddt = ddt + jnp.where(lane == h0, _rowsum(jnp.where(lo, t3, 0.0)), 0.0) \
                    + jnp.where(lane == h1, _rowsum(jnp.where(lo, 0.0, t3)), 0.0)
                dux_s[:, pr * PAIR:(pr + 1) * PAIR] = dxs_p
                dst_s[pr] = dst * cd_p + dst_off
            dub_s[:, g * SSD_STATE:(g + 1) * SSD_STATE] = dbg
            dub_s[:, (SSD_GROUPS + g) * SSD_STATE:(SSD_GROUPS + g + 1) * SSD_STATE] = dcg
        dacs = dacs + jnp.where(_iota((BLK, LANES), 0) == BLK - 1, dlast, 0.0)
        utri = jnp.where(_iota((BLK, BLK), 1) >= _iota((BLK, BLK), 0), 1.0, 0.0).astype(BF16)
        dda = _dot_split(utri, dacs)
        ddt = ddt + dda * a
        ddtr = jnp.where(lane < n_heads, ddt * _sigmoid(dtr_ref[0] + dtb_ref[...]), 0.0)
        dxbc_ref[0, :, CD:CD + LANES] = ddtr.astype(BF16)
        dxbc_ref[0, :, CD + LANES:tail] = jnp.zeros((BLK, tail - CD - LANES), BF16)
        misc_ref[0:1, :] += _colsum(ddtr)
        misc_ref[1:2, :] += jnp.where(lane1 < n_heads, _colsum(dda * dtv) * a, 0.0)
        misc_ref[2:3, :] += dd
        for (du_s, pre_s, sg_s, ext_s, e2_s, nx_s, w_ref, dcw_ref, dcb_ref, c0, width) in (
                (dux_s, prex_s, sgx_s, extx_s, e2x_s, nxx_s, cwx_ref, dcwx_ref, dcbx_ref, 0, D),
                (dub_s, preb_s, sgb_s, extb_s, e2b_s, nxb_s, cwb_ref, dcwb_ref, dcbb_ref, D, 512)):
            pre, sg = pre_s[...], sg_s[...]
            dpre = du_s[...] * (sg * (1.0 + pre * (1.0 - sg)))
            dcb_ref[...] += _colsum(dpre)
            for i in range(CONV_K):
                dcw_ref[i:i + 1, :] += _colsum(dpre * ext_s[pl.ds(HALO - (CONV_K - 1 - i), BLK), :])
            e2_s[0:BLK, :] = dpre
            e2_s[BLK:BLK + HALO, :] = nx_s[...]
            dxr = jnp.zeros((BLK, width), F32)
            for i in range(CONV_K):
                dxr = dxr + e2_s[pl.ds(CONV_K - 1 - i, BLK), :] * w_ref[i:i + 1, :]
            dxbc_ref[0, :, c0:c0 + width] = dxr.astype(BF16)
            nx_s[...] = e2_s[0:HALO, :]

    full = lambda shape: pl.BlockSpec(shape, lambda b, c: (0,) * len(shape))
    return pl.pallas_call(
        body,
        name="ssd_bwd",
        grid=(Bl, nc),
        in_specs=specs,
        out_specs=[
            pl.BlockSpec((1, BLK, tail), lambda b, c: (b, cidx(c), 0)),
            full((CONV_K, D)), full((CONV_K, 512)), full((1, D)), full((1, 512)), full((8, LANES)),
        ],
        out_shape=[
            jax.ShapeDtypeStruct((Bl, L, tail), BF16),
            jax.ShapeDtypeStruct((CONV_K, D), F32), jax.ShapeDtypeStruct((CONV_K, 512), F32),
            jax.ShapeDtypeStruct((1, D), F32), jax.ShapeDtypeStruct((1, 512), F32),
            jax.ShapeDtypeStruct((8, LANES), F32),
        ],
        scratch_shapes=[
            pltpu.VMEM((n_pair, SSD_STATE, PAIR), F32),
            pltpu.VMEM((HALO + BLK, D), F32), pltpu.VMEM((HALO + BLK, 512), F32),
            pltpu.VMEM((BLK, LANES), F32), pltpu.VMEM((LANES, BLK), F32),
            pltpu.VMEM((BLK, D), F32), pltpu.VMEM((BLK, 512), F32),
            pltpu.VMEM((BLK + HALO, D), F32), pltpu.VMEM((BLK + HALO, 512), F32),
            pltpu.VMEM((HALO, D), F32), pltpu.VMEM((HALO, 512), F32),
        ] + [pltpu.VMEM((BLK, D), F32)] * 3 + [pltpu.VMEM((BLK, 512), F32)] * 3,
        compiler_params=_params(("arbitrary", "arbitrary")),
    )(proj3, proj3, proj3, proj3, proj3, cwx, cwb, cbx, cbb, dtb, alog, dsk, s_in, dy3)


def _gate_out(x2, tgt2, o2, proj2, y2, sbw, ssw, w_out_bf, w_out_t):
    T, D = x2.shape
    tm = min(256, T)

    def body(x_ref, t_ref, o_ref, zs_ref, y_ref, zy_ref, sbw_ref, ssw_ref, wo_ref, wot_ref,
             dout_ref, doutb_ref, mixt_ref, do_ref, dy_ref, dz_ref, dnw_ref, loss_ref):
        @pl.when(pl.program_id(0) == 0)
        def _():
            dnw_ref[...] = jnp.zeros_like(dnw_ref)
            loss_ref[...] = jnp.zeros_like(loss_ref)

        def fwd(o, z, w):
            sg = _sigmoid(z)
            sl = z * sg
            g = o * sl
            r = lax.rsqrt(jnp.mean(g * g, axis=-1, keepdims=True) + EPS)
            n = g * r
            return sg, sl, r, n, n * w

        def bwd(dy, o, z, w, sg, sl, r, n):
            dn = dy * w
            dg = r * (dn - n * jnp.mean(dn * n, axis=-1, keepdims=True))
            return dg * sl, dg * o * (sg * (1.0 + z * (1.0 - sg))), _colsum(dy * n)

        o1, z1, w1 = o_ref[...], zs_ref[...], sbw_ref[...]
        o2_, z2, w2 = y_ref[...], zy_ref[...], ssw_ref[...]
        sg1, sl1, r1, n1, y1 = fwd(o1, z1, w1)
        sg2, sl2, r2, n2, y2_ = fwd(o2_, z2, w2)
        y1b, y2b = y1.astype(BF16), y2_.astype(BF16)
        mixt_ref[0:D, :] = y1.T.astype(BF16)
        mixt_ref[D:2 * D, :] = y2_.T.astype(BF16)
        out = x_ref[...] + (_dot(y1b, wo_ref[0:D, :]) + _dot(y2b, wo_ref[D:2 * D, :]))
        err = out - t_ref[...]
        loss_ref[...] += jnp.sum(err * err) * (0.5 / D)
        dout = err * (1.0 / D)
        dout_ref[...] = dout
        doutb = dout.astype(BF16)
        doutb_ref[...] = doutb
        do1, dz1, dw1 = bwd(_dot(doutb, wot_ref[:, 0:D]), o1, z1, w1, sg1, sl1, r1, n1)
        do2, dz2, dw2 = bwd(_dot(doutb, wot_ref[:, D:2 * D]), o2_, z2, w2, sg2, sl2, r2, n2)
        do_ref[...] = do1
        dy_ref[...] = do2
        dz_ref[:, 0:D] = dz1.astype(BF16)
        dz_ref[:, D:2 * D] = dz2.astype(BF16)
        dnw_ref[0:1, :] += dw1
        dnw_ref[1:2, :] += dw2

    row = lambda col: pl.BlockSpec((tm, D), lambda i: (i, col))
    full = lambda shape: pl.BlockSpec(shape, lambda i: (0,) * len(shape))
    wide = pl.BlockSpec((tm, 2 * D), lambda i: (i, 0))
    return pl.pallas_call(
        body,
        name="gate_out",
        grid=(T // tm,),
        in_specs=[row(0), row(0), row(0), row(3), row(0), row(4), full((1, D)), full((1, D)), full((2 * D, D)),
                  full((D, 2 * D))],
        out_specs=[row(0), row(0), pl.BlockSpec((2 * D, tm), lambda i: (0, i)), row(0), row(0), wide,
                   full((8, D)), full((8, LANES))],
        out_shape=[
            jax.ShapeDtypeStruct((T, D), F32), jax.ShapeDtypeStruct((T, D), BF16),
            jax.ShapeDtypeStruct((2 * D, T), BF16), jax.ShapeDtypeStruct((T, D), F32),
            jax.ShapeDtypeStruct((T, D), F32), jax.ShapeDtypeStruct((T, 2 * D), BF16),
            jax.ShapeDtypeStruct((8, D), F32), jax.ShapeDtypeStruct((8, LANES), F32),
        ],
        compiler_params=_params(("arbitrary",)),
    )(x2, tgt2, o2, proj2, y2, proj2, sbw, ssw, w_out_bf, w_out_t)


def _piece_blocks(pieces, D):
    counts = [p.shape[1] // D for p in pieces]
    return [sum(counts[:i]) for i in range(len(counts))], counts


def _dhn(pieces, w_pad_t, x2, dout, norm_w, h_in, h_out, hb_in, hb_out, slab_off, slab_w):
    T, D = x2.shape
    tm = min(1024, T)
    starts, counts = _piece_blocks(pieces, D)
    units = [p for p, n in enumerate(counts) for _ in range(n)]
    per = 2 if all(units[2 * k] != units[2 * k + 1] for k in range(len(units) // 2)) else 1
    nk = -(-len(units) // per)
    ni = T // tm
    n_sem = 2 * (N_CHIPS - 1)
    assert len(units) * D == w_pad_t.shape[0]

    def body(*refs):
        p_refs = refs[:len(pieces)]
        (wa_ref, wb_ref, x_hbm, dout_hbm, nw_ref, hin, hout, hbin, hbout, gx_ref, dnw_ref, rin, rout, oin, oout,
         acc_s, x_s, dout_s, send_sems, recv_sems, row_sems, own_sems) = refs[len(pieces):]
        i, k = pl.program_id(0), pl.program_id(1)

        def rows():
            r0 = pl.multiple_of(i * tm, tm)
            return [pltpu.make_async_copy(src.at[pl.ds(r0, tm)], dst, row_sems.at[n])
                    for n, (src, dst) in enumerate(((x_hbm, x_s), (dout_hbm, dout_s)))]

        @pl.when(k == 0)
        def _():
            for cp in rows():
                cp.start()

        def scatter():
            x, y, c, chips = _place()

            def slab(ref, p):
                return ref.at[:, pl.ds(pl.multiple_of(p * slab_off, LANES), slab_w)]

            cps = []
            for j, (px, py) in enumerate(chips):
                p = 2 * px + py
                for m, (src, dst) in enumerate(((slab(hbin, p), rin.at[j]), (hbout.at[p], rout.at[j]))):
                    cps.append(pltpu.make_async_remote_copy(
                        src_ref=src, dst_ref=dst, send_sem=send_sems.at[2 * j + m], recv_sem=recv_sems.at[2 * j + m],
                        device_id=(px, py, c), device_id_type=MESH))
            me = 2 * x + y
            own = [pltpu.make_async_copy(slab(hin, me), oin, own_sems.at[0]),
                   pltpu.make_async_copy(hout.at[me], oout, own_sems.at[1])]
            return cps + own

        @pl.when((i == 0) & (k == 0))
        def _():
            for cp in scatter():
                cp.start()

        @pl.when((i == ni - 1) & (k == nk - 1))
        def _():
            for cp in scatter():
                cp.wait()

        @pl.when((i == 0) & (k == 0))
        def _():
            dnw_ref[...] = jnp.zeros_like(dnw_ref)

        for step in range(nk):
            @pl.when(k == step)
            def _(step=step):
                part = sum(_dot(p_refs[units[u]][...], w[...])
                           for u, w in list(zip(range(per * step, per * step + per), (wa_ref, wb_ref))) if u < len(units))
                acc_s[...] = part if step == 0 else acc_s[...] + part

        @pl.when(k == nk - 1)
        def _():
            for cp in rows():
                cp.wait()
            xv = x_s[...]
            r = lax.rsqrt(jnp.mean(xv * xv, axis=-1, keepdims=True) + EPS)
            xh = xv * r
            dhn = acc_s[...]
            dxh = dhn * nw_ref[...]
            gx_ref[...] = dout_s[...] + r * (dxh - xh * jnp.mean(dxh * xh, axis=-1, keepdims=True))
            dnw_ref[0:1, :] += _colsum(dhn * xh)

    return pl.pallas_call(
        body,
        name="dhn",
        grid=(T // tm, nk),
        in_specs=[pl.BlockSpec((tm, D), lambda i, k, s=s, n=n: (i, jnp.clip(per * k - s + (per * k < s), 0, n - 1)))
                  for s, n in zip(starts, counts)] + [
            pl.BlockSpec((D, D), lambda i, k: (jnp.minimum(per * k, len(units) - 1), 0)),
            pl.BlockSpec((D, D), lambda i, k: (jnp.minimum(per * k + per - 1, len(units) - 1), 0)),
            ANY, ANY,
            pl.BlockSpec((1, D), lambda i, k: (0, 0)),
            ANY, ANY, ANY, ANY,
        ],
        out_specs=[pl.BlockSpec((tm, D), lambda i, k: (i, 0)), pl.BlockSpec((8, D), lambda i, k: (0, 0)),
                   ANY, ANY, ANY, ANY],
        out_shape=[jax.ShapeDtypeStruct((T, D), F32), jax.ShapeDtypeStruct((8, D), F32),
                   jax.ShapeDtypeStruct((N_CHIPS - 1, h_in.shape[0], slab_w), BF16),
                   jax.ShapeDtypeStruct((N_CHIPS - 1,) + h_out.shape[1:], BF16),
                   jax.ShapeDtypeStruct((h_in.shape[0], slab_w), F32),
                   jax.ShapeDtypeStruct(h_out.shape[1:], F32)],
        scratch_shapes=[pltpu.VMEM((tm, D), F32)] * 3 + [pltpu.SemaphoreType.DMA((n_sem,)), pltpu.SemaphoreType.DMA((n_sem,)),
                                                      pltpu.SemaphoreType.DMA((2,)), pltpu.SemaphoreType.DMA((2,))],
        compiler_params=_params(("arbitrary", "arbitrary")),
    )(*pieces, w_pad_t, w_pad_t, x2, dout, norm_w, h_in, h_out, hb_in, hb_out)


def _grad_w_in(hn_t, pieces):
    D, T = hn_t.shape
    tk = min(1024, T)
    starts, counts = _piece_blocks(pieces, D)

    def body(*refs):
        a_ref, p_refs, o_ref = refs[0], refs[1:-1], refs[-1]
        j = pl.program_id(0)

        @pl.when(pl.program_id(1) == 0)
        def _():
            o_ref[...] = jnp.zeros_like(o_ref)

        for p_ref, s, n in zip(p_refs, starts, counts):
            @pl.when((j >= s) & (j < s + n))
            def _(p_ref=p_ref):
                o_ref[...] += _dot(a_ref[...], p_ref[...])

    def piece_spec(s, n):
        return pl.BlockSpec((tk, D), lambda j, k: (jnp.where((j >= s) & (j < s + n), k, 0), jnp.clip(j - s, 0, n - 1)))

    return pl.pallas_call(
        body,
        name="grad_w_in",
        grid=(sum(counts), T // tk),
        in_specs=[pl.BlockSpec((D, tk), lambda j, k: (0, k))] + [piece_spec(s, n) for s, n in zip(starts, counts)],
        out_specs=pl.BlockSpec((D, D), lambda j, k: (0, j)),
        out_shape=jax.ShapeDtypeStruct((D, sum(counts) * D), F32),
        compiler_params=_params(("parallel", "arbitrary")),
    )(hn_t, *pieces)


def _grad_w_out(a, b, g_in, width):
    M, K = a.shape
    N = b.shape[1]
    tm = min(1024, M)
    tn = 1024 if N % 1024 == 0 else (512 if N % 512 == 0 else N)
    tk = min(512, K)
    grid = (M // tm, N // tn, K // tk)
    h = g_in.shape[0] // 2

    def body(a_ref, b_ref, gin, o_ref, rin, send_sem, recv_sem):
        ids = [pl.program_id(d) for d in range(3)]

        def swap():
            x, y, c, _ = _place()
            return pltpu.make_async_remote_copy(
                src_ref=gin.at[pl.ds((1 - c) * h, h), pl.ds(0, width)], dst_ref=rin, send_sem=send_sem, recv_sem=recv_sem,
                device_id=(x, y, 1 - c), device_id_type=MESH)

        @pl.when((ids[0] == 0) & (ids[1] == 0) & (ids[2] == 0))
        def _():
            swap().start()

        @pl.when(ids[2] == 0)
        def _():
            o_ref[...] = jnp.zeros_like(o_ref)

        o_ref[...] += _dot(a_ref[...], b_ref[...])

        @pl.when((ids[0] == grid[0] - 1) & (ids[1] == grid[1] - 1) & (ids[2] == grid[2] - 1))
        def _():
            swap().wait()

    return pl.pallas_call(
        body,
        name="grad_w_out",
        grid=grid,
        in_specs=[pl.BlockSpec((tm, tk), lambda i, j, k: (i, k)), pl.BlockSpec((tk, tn), lambda i, j, k: (k, j)), ANY],
        out_specs=[pl.BlockSpec((tm, tn), lambda i, j, k: (i, j)), ANY],
        out_shape=[jax.ShapeDtypeStruct((M, N), F32), jax.ShapeDtypeStruct((h, width), F32)],
        scratch_shapes=[pltpu.SemaphoreType.DMA, pltpu.SemaphoreType.DMA],
        compiler_params=_params(("arbitrary", "arbitrary", "arbitrary")),
    )(a, b, g_in)


def _adamw(w, g, m, v, name):
    R, C = w.shape
    tr = 256 if R % 256 == 0 else R
    tc = LANES if (tr == R and R > 256 and C % LANES == 0) else C
    c1 = 1.0 - ADAM_B1 ** ADAM_STEP
    c2 = 1.0 - ADAM_B2 ** ADAM_STEP

    def body(w_ref, g_ref, m_ref, v_ref, d_ref, nm_ref, nv_ref):
        gv = g_ref[...]
        m_new = ADAM_B1 * m_ref[...] + (1.0 - ADAM_B1) * gv
        v_new = ADAM_B2 * v_ref[...] + (1.0 - ADAM_B2) * (gv * gv)
        d_ref[...] = -ADAM_LR * ((m_new / c1) / (jnp.sqrt(v_new / c2) + ADAM_EPS) + ADAM_WD * w_ref[...])
        nm_ref[...] = m_new
        nv_ref[...] = v_new

    spec = pl.BlockSpec((tr, tc), lambda i, j: (i, j))
    return pl.pallas_call(
        body,
        name=name,
        grid=(R // tr, C // tc),
        in_specs=[spec] * 4,
        out_specs=[spec] * 3,
        out_shape=[jax.ShapeDtypeStruct((R, C), F32)] * 3,
        compiler_params=_params(("parallel", "parallel")),
    )(w, g, m, v)


def _add_core_rows(g, recv, core, name):
    h, width = recv.shape
    th = 128 if h % 128 == 0 else h

    def body(c_ref, g_ref, r_ref, o_ref, ob_ref):
        o_ref[...] = g_ref[...] + r_ref[...]
        ob_ref[...] = o_ref[...].astype(BF16)

    return pl.pallas_call(
        body,
        name=name,
        grid_spec=pltpu.PrefetchScalarGridSpec(
            num_scalar_prefetch=1,
            grid=(h // th,),
            in_specs=[
                pl.BlockSpec((th, width), lambda i, c: (c[0] * (h // th) + i, 0)),
                pl.BlockSpec((th, width), lambda i, c: (i, 0)),
            ],
            out_specs=[pl.BlockSpec((th, width), lambda i, c: (i, 0))] * 2,
        ),
        out_shape=[jax.ShapeDtypeStruct((h, width), F32), jax.ShapeDtypeStruct((h, width), BF16)],
        compiler_params=_params(("parallel",)),
    )(core, g, recv)


def _add_core_blocks(g, recv, core, name):
    n, hb, C = recv.shape

    def body(c_ref, g_ref, r_ref, o_ref, ob_ref):
        o_ref[...] = g_ref[...] + r_ref[...]
        ob_ref[...] = o_ref[...].astype(BF16)

    return pl.pallas_call(
        body,
        name=name,
        grid_spec=pltpu.PrefetchScalarGridSpec(
            num_scalar_prefetch=1,
            grid=(n,),
            in_specs=[
                pl.BlockSpec((hb, C), lambda p, c: (2 * p + c[0], 0)),
                pl.BlockSpec((None, hb, C), lambda p, c: (p, 0, 0)),
            ],
            out_specs=[pl.BlockSpec((None, hb, C), lambda p, c: (p, 0, 0))] * 2,
        ),
        out_shape=[jax.ShapeDtypeStruct((n, hb, C), F32), jax.ShapeDtypeStruct((n, hb, C), BF16)],
        compiler_params=_params(("parallel",)),
    )(core, g, recv)


def _add_chips(own, recv, name):
    h, W = own.shape
    th = 256 if h % 256 == 0 else h

    def body(a_ref, r_ref, o_ref):
        o_ref[...] = ((a_ref[...] + r_ref[0].astype(F32)) + r_ref[1].astype(F32)) + r_ref[2].astype(F32)

    return pl.pallas_call(
        body,
        name=name,
        grid=(h // th,),
        in_specs=[pl.BlockSpec((th, W), lambda i: (i, 0)), pl.BlockSpec((N_CHIPS - 1, th, W), lambda i: (0, i, 0))],
        out_specs=pl.BlockSpec((th, W), lambda i: (i, 0)),
        out_shape=jax.ShapeDtypeStruct((h, W), F32),
        compiler_params=_params(("parallel",)),
    )(own, recv)


def _place():
    x, y, c = lax.axis_index("x"), lax.axis_index("y"), lax.axis_index("c")
    other_chips = [(1 - x, y), (x, 1 - y), (1 - x, 1 - y)]
    return x, y, c, other_chips


def _allgather_w_in(w_in_bf, x2, norm_w):
    S, D = w_in_bf.shape
    T = x2.shape[0]
    tm = min(1024, T)
    ni = T // tm
    n_ici = n_fwd = N_CHIPS - 1

    def body(win, x_ref, nw_ref, gin, hn_ref, hnt_ref, send_sems, recv_sems):
        step = pl.program_id(0)
        xv = x_ref[...]
        hn = xv * lax.rsqrt(jnp.mean(xv * xv, axis=-1, keepdims=True) + EPS) * nw_ref[...]
        hn_ref[...] = hn.astype(BF16)
        hnt_ref[...] = hn.T.astype(BF16)
        x, y, c, chips = _place()
        me = 2 * x + y
        sibling = (x, y, 1 - c)
        hin = D // 2

        def half(chip_idx, core):
            return gin.at[chip_idx, :, pl.ds(core * hin, hin)]

        def rcopy(k, src, dst, to):
            return pltpu.make_async_remote_copy(src_ref=src, dst_ref=dst, send_sem=send_sems.at[k],
                                                recv_sem=recv_sems.at[k], device_id=to, device_id_type=MESH)

        def sends():
            return [rcopy(j, win.at[:, pl.ds(c * hin, hin)], half(me, c), (*chip, c)) for j, chip in enumerate(chips)]

        @pl.when(step == 0)
        def _():
            for cp in sends():
                cp.start()

        @pl.when(step == ni - 1)
        def _():
            passed = []
            for j, (px, py) in enumerate(chips):
                theirs = half(2 * px + py, c)
                rcopy(j, theirs, theirs, sibling).wait_recv()
                passed.append(rcopy(n_ici + j, theirs, theirs, sibling))
                passed[-1].start()
            for j, (px, py) in enumerate(chips):
                other = half(2 * px + py, 1 - c)
                rcopy(n_ici + j, other, other, sibling).wait_recv()
            for cp in sends() + passed:
                cp.wait_send()

    return pl.pallas_call(
        body,
        name="allgather_w_in",
        grid=(ni,),
        in_specs=[ANY, pl.BlockSpec((tm, D), lambda i: (i, 0)), pl.BlockSpec((1, D), lambda i: (0, 0))],
        out_specs=[ANY, pl.BlockSpec((tm, D), lambda i: (i, 0)), pl.BlockSpec((D, tm), lambda i: (0, i))],
        out_shape=[jax.ShapeDtypeStruct((N_CHIPS, S, D), BF16),
                   jax.ShapeDtypeStruct((T, D), BF16), jax.ShapeDtypeStruct((D, T), BF16)],
        scratch_shapes=[pltpu.SemaphoreType.DMA((n_ici + n_fwd,)), pltpu.SemaphoreType.DMA((n_ici + n_fwd,))],
        compiler_params=_params(("arbitrary",)),
    )(w_in_bf, x2, norm_w)


def _allreduce_small(packed):
    R = packed.shape[0]
    n_dev = 2 * N_CHIPS

    def body(p_ref, o_ref, buf, send_sems, recv_sems):
        x, y, c, _ = _place()
        me = 4 * x + 2 * y + c
        buf[me] = p_ref[...]
        copies = []
        for k in range(1, n_dev):
            px = 1 - x if k & 4 else x
            py = 1 - y if k & 2 else y
            pc = 1 - c if k & 1 else c
            copies.append((pltpu.make_async_remote_copy(
                src_ref=buf.at[me], dst_ref=buf.at[me], send_sem=send_sems.at[k - 1], recv_sem=recv_sems.at[k - 1],
                device_id=(px, py, pc), device_id_type=MESH), 4 * px + 2 * py + pc, (px, py, pc)))
        for cp, _, _ in copies:
            cp.start()
        for k, (_, peer, to) in enumerate(copies):
            pltpu.make_async_remote_copy(
                src_ref=buf.at[peer], dst_ref=buf.at[peer], send_sem=send_sems.at[k], recv_sem=recv_sems.at[k],
                device_id=to, device_id_type=MESH).wait_recv()
        for cp, _, _ in copies:
            cp.wait_send()
        acc = buf[0]
        for d in range(1, n_dev):
            acc = acc + buf[d]
        o_ref[...] = acc

    vm = pl.BlockSpec(memory_space=pltpu.VMEM)
    return pl.pallas_call(
        body,
        name="allreduce_small",
        in_specs=[vm],
        out_specs=vm,
        out_shape=jax.ShapeDtypeStruct((R, LANES), F32),
        scratch_shapes=[pltpu.VMEM((n_dev, R, LANES), F32), pltpu.SemaphoreType.DMA((n_dev - 1,)),
                        pltpu.SemaphoreType.DMA((n_dev - 1,))],
    )(packed)


def _swap_core_halves(g_out):
    hb = g_out.shape[0] // (2 * N_CHIPS)

    def body(gout, rout, send_sems, recv_sems):
        x, y, c, _ = _place()
        cps = [pltpu.make_async_remote_copy(
            src_ref=gout.at[pl.ds((2 * p + 1 - c) * hb, hb)], dst_ref=rout.at[p], send_sem=send_sems.at[p],
            recv_sem=recv_sems.at[p], device_id=(x, y, 1 - c), device_id_type=MESH) for p in range(N_CHIPS)]
        for cp in cps:
            cp.start()
        for cp in cps:
            cp.wait()

    return pl.pallas_call(
        body,
        name="reduce_core_swap",
        in_specs=[ANY],
        out_specs=ANY,
        out_shape=jax.ShapeDtypeStruct((N_CHIPS, hb, g_out.shape[1]), F32),
        scratch_shapes=[pltpu.SemaphoreType.DMA((N_CHIPS,)), pltpu.SemaphoreType.DMA((N_CHIPS,))],
    )(g_out)


def _join_core_halves(g_in, g_out):
    def body(gin, gout, fin, fout, send_sems, recv_sems):
        x, y, c, _ = _place()
        cps = [pltpu.make_async_remote_copy(src_ref=s, dst_ref=d.at[c], send_sem=send_sems.at[k],
                                            recv_sem=recv_sems.at[k], device_id=(x, y, 1 - c), device_id_type=MESH)
               for k, (s, d) in enumerate(((gin, fin), (gout, fout)))]
        for cp in cps:
            cp.start()
        for k, (s, d) in enumerate(((gin, fin), (gout, fout))):
            pltpu.make_async_remote_copy(src_ref=s, dst_ref=d.at[1 - c], send_sem=send_sems.at[k],
                                         recv_sem=recv_sems.at[k], device_id=(x, y, 1 - c),
                                         device_id_type=MESH).wait_recv()
        for cp in cps:
            cp.wait_send()

    return pl.pallas_call(
        body,
        name="reduce_core_join",
        in_specs=[ANY, ANY],
        out_specs=[ANY, ANY],
        out_shape=[jax.ShapeDtypeStruct((2,) + g_in.shape, F32), jax.ShapeDtypeStruct((2,) + g_out.shape, F32)],
        scratch_shapes=[pltpu.SemaphoreType.DMA((2,)), pltpu.SemaphoreType.DMA((2,))],
    )(g_in, g_out)


def _pack(arrays):
    rows = []
    for a in arrays:
        flat = a.reshape(-1).astype(F32)
        n = -(-flat.shape[0] // LANES) * LANES
        rows.append(jnp.pad(flat, (0, n - flat.shape[0])).reshape(-1, LANES))
    out = jnp.concatenate(rows, axis=0)
    return jnp.pad(out, ((0, -out.shape[0] % 8), (0, 0)))


def _unpack(packed, shapes):
    out, r = [], 0
    for shp in shapes:
        n = math.prod(shp)
        nr = -(-n // LANES)
        out.append(packed[r:r + nr].reshape(-1)[:n].reshape(shp))
        r += nr
    return out


def _pad_lanes(a):
    return jnp.pad(a, ((0, 0), (0, LANES - a.shape[1])))


def kernel(x, norm_w, w_in, q_norm_w, k_norm_w, conv_w, conv_b, dt_bias, A_log, D_skip, sb_norm_w, ssd_norm_w, w_out, loss_target, m_norm_w, m_w_in, m_q_norm_w, m_k_norm_w, m_conv_w, m_conv_b, m_dt_bias, m_A_log, m_D_skip, m_sb_norm_w, m_ssd_norm_w, m_w_out, v_norm_w, v_w_in, v_q_norm_w, v_k_norm_w, v_conv_w, v_conv_b, v_dt_bias, v_A_log, v_D_skip, v_sb_norm_w, v_ssd_norm_w, v_w_out):
    Bl, L, D = x.shape
    T = Bl * L
    S = w_in.shape[2]
    R = w_out.shape[1]
    CW = conv_w.shape[2]
    n_in = N_CHIPS * S
    CD = D + 2 * SSD_GROUPS * SSD_STATE
    H = D // HEAD_DIM
    n_main = 6 * D + 512
    P = -(-(n_main + LANES) // 1024) * 1024
    assert n_in == n_main + H and CD == N_CHIPS * CW and 2 * D == N_CHIPS * R and CD == D + 512
    chip = (2 * lax.axis_index("x") + lax.axis_index("y")).astype(jnp.int32)
    core = lax.axis_index("c").astype(jnp.int32)

    w_in_t, m_in_t, v_in_t = w_in[0].T, m_w_in[0].T, v_w_in[0].T
    w_in_bf, w_out_shard_bf = w_in_t.astype(BF16), w_out[0].astype(BF16)
    x2 = x.reshape(T, D)
    g_in, hn, hn_t = _allgather_w_in(w_in_bf, x2, norm_w)
    g_in = lax.dynamic_update_slice(g_in, w_in_bf[None], (chip, 0, 0))
    w_pad_t = _stack_shards(g_in, P)
    proj, g_out, g_cw = _inproj(hn, w_pad_t, w_out_shard_bf, conv_w[0])
    g_out = lax.dynamic_update_slice(g_out, w_out_shard_bf[None], (chip, 0, 0))
    g_cw = lax.dynamic_update_slice(g_cw, conv_w, (chip, 0, 0))
    w_out_bf = g_out.reshape(2 * D, D)
    conv_full = g_cw.transpose(1, 0, 2).reshape(CONV_K, CD)
    cwx, cwb = conv_full[:, :D], conv_full[:, D:]
    cbx, cbb = conv_b[:, :D], conv_b[:, D:]
    dtb, alog, dsk = _pad_lanes(dt_bias), _pad_lanes(A_log), _pad_lanes(D_skip)
    qw2, kw2 = jnp.tile(q_norm_w, (1, 2)), jnp.tile(k_norm_w, (1, 2))

    proj3 = proj.reshape(Bl, L, P)
    o_sb = _attn_fwd(proj3, qw2, kw2, D)
    y_ssd, s_in = _ssd_fwd(proj3, cwx, cwb, cbx, cbb, dtb, alog, dsk, D)
    dout, dout_bf, mixed_t, do_sb, dy_ssd, dz_bf, dnw_out, loss_blk = _gate_out(
        x2, loss_target.reshape(T, D), o_sb.reshape(T, D), proj, y_ssd.reshape(T, D), sb_norm_w, ssd_norm_w, w_out_bf,
        w_out_bf.T)

    dq, dk, dv, dqkw = _attn_bwd(proj3, o_sb, do_sb.reshape(Bl, L, D), qw2, kw2, D)
    dtail, dcwx, dcwb, dcbx, dcbb, misc = _ssd_bwd(
        proj3, s_in, dy_ssd.reshape(Bl, L, D), cwx, cwb, cbx, cbb, dtb, alog, dsk, D, P - 5 * D)
    dproj = [dq.reshape(T, D), dk.reshape(T, D), dv.reshape(T, D), dz_bf, dtail.reshape(T, P - 5 * D)]
    gw_in = _grad_w_in(hn_t, dproj)

    slab_off = S // LANES * LANES
    slab_w = -(-(S + (N_CHIPS - 1) * (S - slab_off)) // LANES) * LANES
    width = (N_CHIPS - 1) * slab_off + slab_w
    assert n_in <= width <= P
    core1 = core.reshape(1)
    gw_out, r_in = _grad_w_out(mixed_t, dout_bf, gw_in, width)
    r_out = _swap_core_halves(gw_out)
    h_in, hb_in = _add_core_rows(gw_in, r_in, core1, "sum_cores_w_in")
    h_out, hb_out = _add_core_blocks(gw_out, r_out, core1, "sum_cores_w_out")
    grad_x2, dnw_in, s_in_, s_out_, o_in_, o_out_ = _dhn(dproj, w_pad_t, x2, dout, norm_w, h_in, h_out, hb_in, hb_out,
                                                         slab_off, slab_w)
    gh_in = _add_chips(o_in_, s_in_, "sum_chips_w_in")
    gh_out = _add_chips(o_out_, s_out_, "sum_chips_w_out")
    f_in, f_out = _join_core_halves(gh_in, gh_out)
    g_slab = lax.dynamic_update_slice(f_in, gh_in[None], (core, 0, 0)).reshape(D, slab_w)
    g_w_in = lax.dynamic_slice(g_slab, (0, chip * (S - slab_off)), (D, S))
    g_w_out = lax.dynamic_update_slice(f_out, gh_out[None], (core, 0, 0)).reshape(R, D)

    small_shapes = [(1, D), (1, D), (1, D), (1, CD), (1, HEAD_DIM), (1, HEAD_DIM), (1, H), (1, H), (1, H)]
    g_small_local = [dnw_in[0:1], dnw_out[0:1], dnw_out[1:2], jnp.concatenate([dcbx, dcbb], axis=1),
                     dqkw[0:1, :HEAD_DIM] + dqkw[0:1, HEAD_DIM:], dqkw[1:2, :HEAD_DIM] + dqkw[1:2, HEAD_DIM:],
                     misc[0:1, :H], misc[1:2, :H], misc[2:3, :H]]
    packed = _pack(g_small_local + [jnp.concatenate([dcwx, dcwb], axis=1), loss_blk[0:1, 0:1]])
    red = _allreduce_small(packed)
    g_small = _unpack(red, small_shapes + [(CONV_K, CD), (1, 1)])
    g_conv_w = lax.dynamic_slice_in_dim(g_small[9], chip * CW, CW, axis=1)
    loss = g_small[10][0, 0]

    d_in, nm_in, nv_in = (t.T for t in _adamw(w_in_t, g_w_in.T, m_in_t, v_in_t, "adamw_w_in"))
    d_out, nm_out, nv_out = _adamw(w_out[0], g_w_out, m_w_out[0], v_w_out[0], "adamw_w_out")
    d_cw, nm_cw, nv_cw = _adamw(conv_w[0], g_conv_w, m_conv_w[0], v_conv_w[0], "adamw_conv_w")
    small_w = [norm_w, sb_norm_w, ssd_norm_w, conv_b, q_norm_w, k_norm_w, dt_bias, A_log, D_skip]
    small_m = [m_norm_w, m_sb_norm_w, m_ssd_norm_w, m_conv_b, m_q_norm_w, m_k_norm_w, m_dt_bias, m_A_log, m_D_skip]
    small_v = [v_norm_w, v_sb_norm_w, v_ssd_norm_w, v_conv_b, v_q_norm_w, v_k_norm_w, v_dt_bias, v_A_log, v_D_skip]
    d_s, nm_s, nv_s = _adamw(_pack(small_w), _pack(g_small[:9]), _pack(small_m), _pack(small_v), "adamw_small")
    d_s, nm_s, nv_s = (_unpack(t, small_shapes) for t in (d_s, nm_s, nv_s))

    def ordered(s, w_in_, conv_w_, w_out_):
        return [s[0], w_in_[None], s[4], s[5], conv_w_[None], s[3], s[6], s[7], s[8], s[1], s[2], w_out_[None]]

    return (loss, grad_x2.reshape(Bl, L, D),
            *ordered(g_small[:9], g_w_in, g_conv_w, g_w_out),
            *ordered(d_s, d_in, d_cw, d_out),
            *ordered(nm_s, nm_in, nm_cw, nm_out),
            *ordered(nv_s, nv_in, nv_cw, nv_out))
```
